```python
import jax, jax.numpy as jnp
from jax import lax
import numpy as np

D_MODEL = 1024
BATCH = 16
SEQ = 4096
DEPTH = 2

HEAD_DIM = 64
N_HEADS = D_MODEL // HEAD_DIM
DILATED_GROUPS = ((128, 1), (512, 4), (2048, 16))
N_GROUPS = len(DILATED_GROUPS)
BAND_BLOCK = 128
ROT_DIM = HEAD_DIM // 4
ROPE_THETA = 500000.0
FOX_BLOCK = 128
D_FF = 2816
N_A_LAYERS = DEPTH // 2
N_B_LAYERS = DEPTH - N_A_LAYERS
HD = N_HEADS * HEAD_DIM
EPS = 1e-6

kernel_name = "yoco_dilated_fox_macaron_trunk"


def rms_norm(x, g):
    xf = x.astype(jnp.float32)
    y = xf * lax.rsqrt(jnp.mean(xf * xf, axis=-1, keepdims=True) + EPS)
    return (y * g.astype(jnp.float32)).astype(x.dtype)


def swiglu(x, w_in, w_out):
    gate, up = jnp.split(x @ w_in, 2, axis=-1)
    return (jax.nn.silu(gate) * up) @ w_out


def rope_partial(x, positions):
    half = ROT_DIM // 2
    inv_freq = ROPE_THETA ** (-jnp.arange(0, ROT_DIM, 2, dtype=jnp.float32) / ROT_DIM)
    ang = positions.astype(jnp.float32)[..., None] * inv_freq
    cos, sin = jnp.cos(ang)[:, :, None, :], jnp.sin(ang)[:, :, None, :]
    xr = x[..., :ROT_DIM].astype(jnp.float32)
    x1, x2 = xr[..., :half], xr[..., half:]
    rot = jnp.concatenate([x1 * cos - x2 * sin, x2 * cos + x1 * sin], axis=-1)
    return jnp.concatenate([rot.astype(x.dtype), x[..., ROT_DIM:]], axis=-1)


def dilated_band_attention(q, k, v, window, dilation):
    b, s, h, dh = q.shape
    n_steps = window // dilation
    span = dilation * BAND_BLOCK
    s_pad = -(-s // span) * span
    seq_len = s_pad // dilation
    nb = seq_len // BAND_BLOCK
    pad = ((0, 0), (0, s_pad - s), (0, 0), (0, 0))

    def to_blocks(t):
        t = jnp.pad(t, pad).reshape(b, seq_len, dilation, h, dh).transpose(0, 2, 1, 3, 4)
        return t.reshape(b, dilation, nb, BAND_BLOCK, h, dh)

    def with_prev(t):
        prev = jnp.pad(t, ((0, 0), (0, 0), (1, 0), (0, 0), (0, 0), (0, 0)))[:, :, :-1]
        return jnp.concatenate([prev, t], axis=3)

    qb = to_blocks(q)
    kk = with_prev(to_blocks(k))
    vv = with_prev(to_blocks(v))
    scores = jnp.einsum('brnqhd,brnkhd->brnhqk', qb, kk).astype(jnp.float32) * (dh ** -0.5)
    qi = jnp.arange(BAND_BLOCK)[:, None]
    kj = jnp.arange(2 * BAND_BLOCK)[None, :]
    dist = qi + BAND_BLOCK - kj
    blk = jnp.arange(nb)[:, None, None]
    valid = (dist >= 0) & (dist <= n_steps) & ((blk > 0) | (kj >= BAND_BLOCK))
    scores = jnp.where(valid[:, None], scores, -jnp.inf)
    lse = jax.nn.logsumexp(scores, axis=-1)
    probs = jnp.exp(scores - lse[..., None])
    out = jnp.einsum('brnhqk,brnkhd->brnqhd', probs.astype(v.dtype), vv)
    out = out.transpose(0, 2, 3, 1, 4, 5).reshape(b, s_pad, h, dh)[:, :s]
    lse = lse.transpose(0, 2, 4, 1, 3).reshape(b, s_pad, h)[:, :s]
    return out, lse


def dilated_mixture_mixer(hn, positions, w_qkv, q_norm, k_norm, w_o):
    b, s, _ = hn.shape
    qkv = (hn @ w_qkv).reshape(b, s, N_GROUPS, 3, N_HEADS, HEAD_DIM)
    outs, lses = [], []
    for g, (window, dilation) in enumerate(DILATED_GROUPS):
        q = rope_partial(rms_norm(qkv[:, :, g, 0], q_norm[g]), positions)
        k = rope_partial(rms_norm(qkv[:, :, g, 1], k_norm[g]), positions)
        o, lse = dilated_band_attention(q, k, qkv[:, :, g, 2], window, dilation)
        outs.append(o.astype(jnp.float32))
        lses.append(lse)
    alpha = jax.nn.softmax(jnp.stack(lses, axis=0), axis=0)
    mixed = jnp.sum(alpha[..., None] * jnp.stack(outs, axis=0), axis=0).astype(hn.dtype)
    return mixed.reshape(b, s, HD) @ w_o


def shared_kv(hn, w_kv, b_f, k_norm):
    b, s, _ = hn.shape
    proj = hn @ w_kv
    k = rms_norm(proj[..., :HD].reshape(b, s, N_HEADS, HEAD_DIM), k_norm)
    v = proj[..., HD:2 * HD].reshape(b, s, N_HEADS, HEAD_DIM)
    log_f = jax.nn.log_sigmoid(proj[..., 2 * HD:].astype(jnp.float32) + b_f.astype(jnp.float32))
    cum = jnp.cumsum(log_f, axis=1)
    return k, v, cum


def forgetting_attention(hn, k, v, cum, w_q, q_norm, w_o):
    b, s, _ = hn.shape
    q = rms_norm((hn @ w_q).reshape(b, s, N_HEADS, HEAD_DIM), q_norm)
    nb = s // FOX_BLOCK
    q_blocks = q.reshape(b, nb, FOX_BLOCK, N_HEADS, HEAD_DIM).transpose(1, 0, 2, 3, 4)
    c_blocks = cum.reshape(b, nb, FOX_BLOCK, N_HEADS).transpose(1, 0, 2, 3)
    ck = cum.transpose(0, 2, 1)[:, :, None, :]
    key_pos = jnp.arange(s)
    scale = HEAD_DIM ** -0.5

    def block(args):
        qb, cb, bi = args
        logits = jnp.einsum('bqhd,bshd->bhqs', qb, k).astype(jnp.float32) * scale
        logits = logits + (cb.transpose(0, 2, 1)[..., None] - ck)
        qpos = bi * FOX_BLOCK + jnp.arange(FOX_BLOCK)
        logits = jnp.where(key_pos[None, :] <= qpos[:, None], logits, -jnp.inf)
        p = jax.nn.softmax(logits, axis=-1)
        return jnp.einsum('bhqs,bshd->bqhd', p.astype(v.dtype), v)

    o = lax.map(block, (q_blocks, c_blocks, jnp.arange(nb)))
    o = o.transpose(1, 0, 2, 3, 4).reshape(b, s, HD)
    return o @ w_o


def _fwd_setup_inputs(seed: int = 0) -> dict:
    key = jax.random.key(seed)
    ks = jax.random.split(key, 20)
    f32 = jnp.float32

    def nrm(k, shape, fan_in):
        return jax.random.normal(k, shape, f32) * (fan_in ** -0.5)

    def gain(k, shape):
        return 1.0 + 0.05 * jax.random.normal(k, shape, f32)

    x = jax.random.normal(ks[0], (BATCH, SEQ, D_MODEL), f32)
    offset = jax.random.randint(ks[1], (BATCH, 1), 0, 1024, dtype=jnp.int32)
    positions = (jnp.arange(SEQ, dtype=jnp.int32)[None, :] + offset).astype(jnp.int32)
    return {
        "x": x,
        "positions": positions,
        "ffn_norm": gain(ks[2], (DEPTH, 2, D_MODEL)),
        "ffn_w_in": nrm(ks[3], (DEPTH, 2, D_MODEL, 2 * D_FF), D_MODEL),
        "ffn_w_out": nrm(ks[4], (DEPTH, 2, D_FF, D_MODEL), D_FF),
        "mix_norm": gain(ks[5], (DEPTH, D_MODEL)),
        "a_w_qkv": nrm(ks[6], (N_A_LAYERS, D_MODEL, N_GROUPS * 3 * HD), D_MODEL),
        "a_q_norm": gain(ks[7], (N_A_LAYERS, N_GROUPS, HEAD_DIM)),
        "a_k_norm": gain(ks[8], (N_A_LAYERS, N_GROUPS, HEAD_DIM)),
        "a_w_o": nrm(ks[9], (N_A_LAYERS, HD, D_MODEL), HD),
        "kv_norm": gain(ks[10], (D_MODEL,)),
        "kv_w": nrm(ks[11], (D_MODEL, 2 * HD + N_HEADS), D_MODEL),
        "kv_b_f": 0.1 * jax.random.normal(ks[12], (N_HEADS,), f32),
        "kv_k_norm": gain(ks[13], (HEAD_DIM,)),
        "b_w_q": nrm(ks[14], (N_B_LAYERS, D_MODEL, HD), D_MODEL),
        "b_q_norm": gain(ks[15], (N_B_LAYERS, HEAD_DIM)),
        "b_w_o": nrm(ks[16], (N_B_LAYERS, HD, D_MODEL), HD),
    }


def _fwd_reference(x, positions, ffn_norm, ffn_w_in, ffn_w_out, mix_norm, a_w_qkv, a_q_norm, a_k_norm, a_w_o,
              kv_norm, kv_w, kv_b_f, kv_k_norm, b_w_q, b_q_norm, b_w_o):
    h = x
    k_sh = v_sh = cum_sh = None
    for layer in range(DEPTH):
        if layer == N_A_LAYERS:
            k_sh, v_sh, cum_sh = shared_kv(rms_norm(h, kv_norm), kv_w, kv_b_f, kv_k_norm)
        h = h + 0.5 * swiglu(rms_norm(h, ffn_norm[layer, 0]), ffn_w_in[layer, 0], ffn_w_out[layer, 0])
        hn = rms_norm(h, mix_norm[layer])
        if layer < N_A_LAYERS:
            h = h + dilated_mixture_mixer(hn, positions, a_w_qkv[layer], a_q_norm[layer],
                                          a_k_norm[layer], a_w_o[layer])
        else:
            j = layer - N_A_LAYERS
            h = h + forgetting_attention(hn, k_sh, v_sh, cum_sh, b_w_q[j], b_q_norm[j], b_w_o[j])
        h = h + 0.5 * swiglu(rms_norm(h, ffn_norm[layer, 1]), ffn_w_in[layer, 1], ffn_w_out[layer, 1])
    return h


import jax as _jax
import jax.numpy as _jnp

TWIN_FORMAT = 'train_step'
FWD_PARAMS = ['x', 'positions', 'ffn_norm', 'ffn_w_in', 'ffn_w_out', 'mix_norm', 'a_w_qkv', 'a_q_norm', 'a_k_norm', 'a_w_o', 'kv_norm', 'kv_w', 'kv_b_f', 'kv_k_norm', 'b_w_q', 'b_q_norm', 'b_w_o']
TWIN_WEIGHTS = ['ffn_norm', 'ffn_w_in', 'ffn_w_out', 'mix_norm', 'a_w_qkv', 'a_q_norm', 'a_k_norm', 'a_w_o', 'kv_norm', 'kv_w', 'kv_b_f', 'kv_k_norm', 'b_w_q', 'b_q_norm', 'b_w_o']
TWIN_DIFF_INPUT = 'x'
TWIN_INPUTS = ['x', 'positions', 'ffn_norm', 'ffn_w_in', 'ffn_w_out', 'mix_norm', 'a_w_qkv', 'a_q_norm', 'a_k_norm', 'a_w_o', 'kv_norm', 'kv_w', 'kv_b_f', 'kv_k_norm', 'b_w_q', 'b_q_norm', 'b_w_o', 'loss_target', 'm_ffn_norm', 'm_ffn_w_in', 'm_ffn_w_out', 'm_mix_norm', 'm_a_w_qkv', 'm_a_q_norm', 'm_a_k_norm', 'm_a_w_o', 'm_kv_norm', 'm_kv_w', 'm_kv_b_f', 'm_kv_k_norm', 'm_b_w_q', 'm_b_q_norm', 'm_b_w_o', 'v_ffn_norm', 'v_ffn_w_in', 'v_ffn_w_out', 'v_mix_norm', 'v_a_w_qkv', 'v_a_q_norm', 'v_a_k_norm', 'v_a_w_o', 'v_kv_norm', 'v_kv_w', 'v_kv_b_f', 'v_kv_k_norm', 'v_b_w_q', 'v_b_q_norm', 'v_b_w_o']
TWIN_OUTPUTS = ['loss', 'grad_x', 'grad_ffn_norm', 'grad_ffn_w_in', 'grad_ffn_w_out', 'grad_mix_norm', 'grad_a_w_qkv', 'grad_a_q_norm', 'grad_a_k_norm', 'grad_a_w_o', 'grad_kv_norm', 'grad_kv_w', 'grad_kv_b_f', 'grad_kv_k_norm', 'grad_b_w_q', 'grad_b_q_norm', 'grad_b_w_o', 'delta_ffn_norm', 'delta_ffn_w_in', 'delta_ffn_w_out', 'delta_mix_norm', 'delta_a_w_qkv', 'delta_a_q_norm', 'delta_a_k_norm', 'delta_a_w_o', 'delta_kv_norm', 'delta_kv_w', 'delta_kv_b_f', 'delta_kv_k_norm', 'delta_b_w_q', 'delta_b_q_norm', 'delta_b_w_o', 'new_m_ffn_norm', 'new_m_ffn_w_in', 'new_m_ffn_w_out', 'new_m_mix_norm', 'new_m_a_w_qkv', 'new_m_a_q_norm', 'new_m_a_k_norm', 'new_m_a_w_o', 'new_m_kv_norm', 'new_m_kv_w', 'new_m_kv_b_f', 'new_m_kv_k_norm', 'new_m_b_w_q', 'new_m_b_q_norm', 'new_m_b_w_o', 'new_v_ffn_norm', 'new_v_ffn_w_in', 'new_v_ffn_w_out', 'new_v_mix_norm', 'new_v_a_w_qkv', 'new_v_a_q_norm', 'new_v_a_k_norm', 'new_v_a_w_o', 'new_v_kv_norm', 'new_v_kv_w', 'new_v_kv_b_f', 'new_v_kv_k_norm', 'new_v_b_w_q', 'new_v_b_q_norm', 'new_v_b_w_o']
TWIN_LEAF_KINDS = {'loss': 'loss', 'grad_x': 'grad_x', 'grad_ffn_norm': 'grad_w', 'grad_ffn_w_in': 'grad_w', 'grad_ffn_w_out': 'grad_w', 'grad_mix_norm': 'grad_w', 'grad_a_w_qkv': 'grad_w', 'grad_a_q_norm': 'grad_w', 'grad_a_k_norm': 'grad_w', 'grad_a_w_o': 'grad_w', 'grad_kv_norm': 'grad_w', 'grad_kv_w': 'grad_w', 'grad_kv_b_f': 'grad_w', 'grad_kv_k_norm': 'grad_w', 'grad_b_w_q': 'grad_w', 'grad_b_q_norm': 'grad_w', 'grad_b_w_o': 'grad_w', 'delta_ffn_norm': 'delta_w', 'delta_ffn_w_in': 'delta_w', 'delta_ffn_w_out': 'delta_w', 'delta_mix_norm': 'delta_w', 'delta_a_w_qkv': 'delta_w', 'delta_a_q_norm': 'delta_w', 'delta_a_k_norm': 'delta_w', 'delta_a_w_o': 'delta_w', 'delta_kv_norm': 'delta_w', 'delta_kv_w': 'delta_w', 'delta_kv_b_f': 'delta_w', 'delta_kv_k_norm': 'delta_w', 'delta_b_w_q': 'delta_w', 'delta_b_q_norm': 'delta_w', 'delta_b_w_o': 'delta_w', 'new_m_ffn_norm': 'new_m', 'new_m_ffn_w_in': 'new_m', 'new_m_ffn_w_out': 'new_m', 'new_m_mix_norm': 'new_m', 'new_m_a_w_qkv': 'new_m', 'new_m_a_q_norm': 'new_m', 'new_m_a_k_norm': 'new_m', 'new_m_a_w_o': 'new_m', 'new_m_kv_norm': 'new_m', 'new_m_kv_w': 'new_m', 'new_m_kv_b_f': 'new_m', 'new_m_kv_k_norm': 'new_m', 'new_m_b_w_q': 'new_m', 'new_m_b_q_norm': 'new_m', 'new_m_b_w_o': 'new_m', 'new_v_ffn_norm': 'new_v', 'new_v_ffn_w_in': 'new_v', 'new_v_ffn_w_out': 'new_v', 'new_v_mix_norm': 'new_v', 'new_v_a_w_qkv': 'new_v', 'new_v_a_q_norm': 'new_v', 'new_v_a_k_norm': 'new_v', 'new_v_a_w_o': 'new_v', 'new_v_kv_norm': 'new_v', 'new_v_kv_w': 'new_v', 'new_v_kv_b_f': 'new_v', 'new_v_kv_k_norm': 'new_v', 'new_v_b_w_q': 'new_v', 'new_v_b_q_norm': 'new_v', 'new_v_b_w_o': 'new_v'}


def _forward(args):
    return _fwd_reference(*[args[k] for k in FWD_PARAMS])


def _output_shape():
    out = _jax.eval_shape(lambda: _forward(_fwd_setup_inputs(0)))
    return out.shape, out.dtype

N_MICROBATCH = 1
ADAM_LR = 0.001
ADAM_B1 = 0.9
ADAM_B2 = 0.999
ADAM_EPS = 1e-08
ADAM_WD = 0.01
ADAM_STEP = 10
PER_EXAMPLE_BATCH_AXIS = {'x': 0, 'positions': 0, 'loss_target': 0}
SHARED_INPUTS = []
_WEIGHT_DTYPES = {'ffn_norm': _jnp.float32, 'ffn_w_in': _jnp.float32, 'ffn_w_out': _jnp.float32, 'mix_norm': _jnp.float32, 'a_w_qkv': _jnp.float32, 'a_q_norm': _jnp.float32, 'a_k_norm': _jnp.float32, 'a_w_o': _jnp.float32, 'kv_norm': _jnp.float32, 'kv_w': _jnp.float32, 'kv_b_f': _jnp.float32, 'kv_k_norm': _jnp.float32, 'b_w_q': _jnp.float32, 'b_q_norm': _jnp.float32, 'b_w_o': _jnp.float32}
MOMENT_SCALE = {'ffn_norm': 1.218516e+01, 'ffn_w_in': 1.418437e-01, 'ffn_w_out': 2.488856e-01, 'mix_norm': 3.512955e-01, 'a_w_qkv': 9.257089e-02, 'a_q_norm': 1.266416e+00, 'a_k_norm': 1.271879e+00, 'a_w_o': 2.154147e-01, 'kv_norm': 2.980986e+01, 'kv_w': 7.143961e-01, 'kv_b_f': 3.217426e+02, 'kv_k_norm': 4.938753e+01, 'b_w_q': 4.291376e-01, 'b_q_norm': 5.007434e+01, 'b_w_o': 8.868029e-01}


def _to_microbatches(a, axis):
    t = _jnp.moveaxis(a, axis, 0)
    t = t.reshape((N_MICROBATCH, t.shape[0] // N_MICROBATCH) + t.shape[1:])
    return _jnp.moveaxis(t, 1, axis + 1)


def setup_inputs(seed: int = 0) -> dict:
    inp = _fwd_setup_inputs(seed)
    key = _jax.random.fold_in(_jax.random.key(seed), 7919)
    shape, _ = _output_shape()
    out = dict(inp)
    out["loss_target"] = _jax.random.normal(_jax.random.fold_in(key, 0), shape, _jnp.float32)
    for i, name in enumerate(TWIN_WEIGHTS):
        w = inp[name].astype(_jnp.float32)
        if MOMENT_SCALE is None:
            s = _jnp.sqrt(_jnp.mean(_jnp.square(w)) + 1e-30)
        else:
            s = MOMENT_SCALE[name]
        km, kv = _jax.random.split(_jax.random.fold_in(key, i + 1))
        out[name] = w
        out["m_" + name] = s * _jax.random.normal(km, w.shape, _jnp.float32)
        out["v_" + name] = (s * s) * _jax.random.uniform(kv, w.shape, _jnp.float32, 0.5, 1.5)
    if N_MICROBATCH > 1:
        for name, axis in PER_EXAMPLE_BATCH_AXIS.items():
            out[name] = _to_microbatches(out[name], axis)
    return {'x': out['x'], 'positions': out['positions'], 'ffn_norm': out['ffn_norm'], 'ffn_w_in': out['ffn_w_in'], 'ffn_w_out': out['ffn_w_out'], 'mix_norm': out['mix_norm'], 'a_w_qkv': out['a_w_qkv'], 'a_q_norm': out['a_q_norm'], 'a_k_norm': out['a_k_norm'], 'a_w_o': out['a_w_o'], 'kv_norm': out['kv_norm'], 'kv_w': out['kv_w'], 'kv_b_f': out['kv_b_f'], 'kv_k_norm': out['kv_k_norm'], 'b_w_q': out['b_w_q'], 'b_q_norm': out['b_q_norm'], 'b_w_o': out['b_w_o'], 'loss_target': out['loss_target'], 'm_ffn_norm': out['m_ffn_norm'], 'm_ffn_w_in': out['m_ffn_w_in'], 'm_ffn_w_out': out['m_ffn_w_out'], 'm_mix_norm': out['m_mix_norm'], 'm_a_w_qkv': out['m_a_w_qkv'], 'm_a_q_norm': out['m_a_q_norm'], 'm_a_k_norm': out['m_a_k_norm'], 'm_a_w_o': out['m_a_w_o'], 'm_kv_norm': out['m_kv_norm'], 'm_kv_w': out['m_kv_w'], 'm_kv_b_f': out['m_kv_b_f'], 'm_kv_k_norm': out['m_kv_k_norm'], 'm_b_w_q': out['m_b_w_q'], 'm_b_q_norm': out['m_b_q_norm'], 'm_b_w_o': out['m_b_w_o'], 'v_ffn_norm': out['v_ffn_norm'], 'v_ffn_w_in': out['v_ffn_w_in'], 'v_ffn_w_out': out['v_ffn_w_out'], 'v_mix_norm': out['v_mix_norm'], 'v_a_w_qkv': out['v_a_w_qkv'], 'v_a_q_norm': out['v_a_q_norm'], 'v_a_k_norm': out['v_a_k_norm'], 'v_a_w_o': out['v_a_w_o'], 'v_kv_norm': out['v_kv_norm'], 'v_kv_w': out['v_kv_w'], 'v_kv_b_f': out['v_kv_b_f'], 'v_kv_k_norm': out['v_kv_k_norm'], 'v_b_w_q': out['v_b_w_q'], 'v_b_q_norm': out['v_b_q_norm'], 'v_b_w_o': out['v_b_w_o']}


def _loss(weights, diff, rest, loss_target):
    with _jax.named_scope("forward"):
        args = {**rest, TWIN_DIFF_INPUT: diff, **{k: w.astype(_WEIGHT_DTYPES[k]) for k, w in weights.items()}}
        y = _forward(args)
    with _jax.named_scope("loss_head"):
        err = _jnp.square(y.astype(_jnp.float32) - loss_target)
        return 0.5 * _jnp.sum(_jnp.mean(err, axis=-1)) if err.ndim else 0.5 * err


def _adamw(w, g, m, v):
    m = ADAM_B1 * m + (1.0 - ADAM_B1) * g
    v = ADAM_B2 * v + (1.0 - ADAM_B2) * _jnp.square(g)
    m_hat = m / (1.0 - ADAM_B1 ** ADAM_STEP)
    v_hat = v / (1.0 - ADAM_B2 ** ADAM_STEP)
    delta = -ADAM_LR * (m_hat / (_jnp.sqrt(v_hat) + ADAM_EPS) + ADAM_WD * w)
    return delta, m, v


def reference(x, positions, ffn_norm, ffn_w_in, ffn_w_out, mix_norm, a_w_qkv, a_q_norm, a_k_norm, a_w_o, kv_norm, kv_w, kv_b_f, kv_k_norm, b_w_q, b_q_norm, b_w_o, loss_target, m_ffn_norm, m_ffn_w_in, m_ffn_w_out, m_mix_norm, m_a_w_qkv, m_a_q_norm, m_a_k_norm, m_a_w_o, m_kv_norm, m_kv_w, m_kv_b_f, m_kv_k_norm, m_b_w_q, m_b_q_norm, m_b_w_o, v_ffn_norm, v_ffn_w_in, v_ffn_w_out, v_mix_norm, v_a_w_qkv, v_a_q_norm, v_a_k_norm, v_a_w_o, v_kv_norm, v_kv_w, v_kv_b_f, v_kv_k_norm, v_b_w_q, v_b_q_norm, v_b_w_o):
    given = dict(x=x, positions=positions, ffn_norm=ffn_norm, ffn_w_in=ffn_w_in, ffn_w_out=ffn_w_out, mix_norm=mix_norm, a_w_qkv=a_w_qkv, a_q_norm=a_q_norm, a_k_norm=a_k_norm, a_w_o=a_w_o, kv_norm=kv_norm, kv_w=kv_w, kv_b_f=kv_b_f, kv_k_norm=kv_k_norm, b_w_q=b_w_q, b_q_norm=b_q_norm, b_w_o=b_w_o, loss_target=loss_target, m_ffn_norm=m_ffn_norm, m_ffn_w_in=m_ffn_w_in, m_ffn_w_out=m_ffn_w_out, m_mix_norm=m_mix_norm, m_a_w_qkv=m_a_w_qkv, m_a_q_norm=m_a_q_norm, m_a_k_norm=m_a_k_norm, m_a_w_o=m_a_w_o, m_kv_norm=m_kv_norm, m_kv_w=m_kv_w, m_kv_b_f=m_kv_b_f, m_kv_k_norm=m_kv_k_norm, m_b_w_q=m_b_w_q, m_b_q_norm=m_b_q_norm, m_b_w_o=m_b_w_o, v_ffn_norm=v_ffn_norm, v_ffn_w_in=v_ffn_w_in, v_ffn_w_out=v_ffn_w_out, v_mix_norm=v_mix_norm, v_a_w_qkv=v_a_w_qkv, v_a_q_norm=v_a_q_norm, v_a_k_norm=v_a_k_norm, v_a_w_o=v_a_w_o, v_kv_norm=v_kv_norm, v_kv_w=v_kv_w, v_kv_b_f=v_kv_b_f, v_kv_k_norm=v_kv_k_norm, v_b_w_q=v_b_w_q, v_b_q_norm=v_b_q_norm, v_b_w_o=v_b_w_o)
    weights = {n: given[n] for n in TWIN_WEIGHTS}
    shared = {n: given[n] for n in SHARED_INPUTS}
    per_example = {n: given[n] for n in ['x', 'positions']}
    grad_fn = _jax.value_and_grad(_loss, argnums=(0, 1))

    def one_microbatch(ex, loss_target):
        ex = dict(ex)
        diff = ex.pop(TWIN_DIFF_INPUT)
        return grad_fn(weights, diff, {**shared, **ex}, loss_target)

    if N_MICROBATCH == 1:
        loss, (grad_w, grad_x) = one_microbatch(per_example, given["loss_target"])
    else:
        def body(carry, xs):
            loss_sum, grad_sum = carry
            l_k, (gw_k, gx_k) = one_microbatch(xs[0], xs[1])
            with _jax.named_scope("update"):
                return (loss_sum + l_k, _jax.tree.map(_jnp.add, grad_sum, gw_k)), gx_k

        init = (_jnp.zeros((), _jnp.float32), _jax.tree.map(_jnp.zeros_like, weights))
        (loss, grad_w), grad_x = _jax.lax.scan(body, init, (per_example, given["loss_target"]))
    with _jax.named_scope("update"):
        delta_w, new_m, new_v = {}, {}, {}
        for n in TWIN_WEIGHTS:
            delta_w[n], new_m[n], new_v[n] = _adamw(weights[n], grad_w[n], given["m_" + n], given["v_" + n])
    return (loss, grad_x, *[grad_w[n] for n in TWIN_WEIGHTS], *[delta_w[n] for n in TWIN_WEIGHTS],
            *[new_m[n] for n in TWIN_WEIGHTS], *[new_v[n] for n in TWIN_WEIGHTS])
```

```python
import functools

import jax
import jax.numpy as jnp
import numpy as np
from jax import lax
from jax.experimental import pallas as pl
from jax.experimental.pallas import tpu as pltpu

F32 = jnp.float32
BF16 = jnp.bfloat16
HIGHEST = lax.Precision.HIGHEST

HEAD_DIM = 64
ROT_DIM = HEAD_DIM // 4
ROPE_THETA = 500000.0
DILATED_GROUPS = ((128, 1), (512, 4), (2048, 16))
BAND_BLOCK = 128
EPS = 1e-6
ADAM_LR, ADAM_B1, ADAM_B2, ADAM_EPS, ADAM_WD, ADAM_STEP = 0.001, 0.9, 0.999, 1e-08, 0.01, 10

LANES = 128
N_DEV = 8
VMEM_LIMIT = 56 * 1024 * 1024
NEG = -1e30
SCALE = HEAD_DIM ** -0.5
PACK_ROWS = 1024
MESH_ID = pl.DeviceIdType.MESH


def _params(*sem):
    return pltpu.CompilerParams(dimension_semantics=sem, vmem_limit_bytes=VMEM_LIMIT)


def _divs(n, cap, mult=LANES):
    d = [t for t in range(mult, min(n, cap) + 1, mult) if n % t == 0]
    return d or [n]


def _pick(n, cap, mult=LANES):
    return _divs(n, cap, mult)[-1]


def _mm_tiles(M, N, K):
    best = None
    for tm in _divs(M, 1408):
        for tn in _divs(N, 1408):
            if tm * tn <= 1024 * 1024 and (best is None or tm * tn > best[0] * best[1]):
                best = (tm, tn)
    if best is None:
        best = (_divs(M, 1408)[0], _divs(N, 1408)[0])
    tm, tn = best
    tks = [t for t in _divs(K, 2048) if (tm + tn) * t <= 3 * 1024 * 1024]
    tk = tks[-1] if tks else _divs(K, 2048)[0]
    return tm, tn, tk


def _dot(a, b, dims, precision=None):
    return lax.dot_general(a, b, (dims, ((), ())), precision=precision, preferred_element_type=F32)


NN = ((1,), (0,))
NT = ((1,), (1,))
TN = ((0,), (0,))


def _mm(a, b, mode, *, M, N, K, a_off=(0, 0), b_off=(0, 0), res=None, scale=1.0, out_dtype=F32, name):
    tm, tn, tk = _mm_tiles(M, N, K)
    nk = K // tk
    if mode == "tn":
        a_blk, b_blk, dims = (tk, tm), (tk, tn), TN
    elif mode == "nt":
        a_blk, b_blk, dims = (tm, tk), (tn, tk), NT
    else:
        a_blk, b_blk, dims = (tm, tk), (tk, tn), NN
    for off, blk in ((a_off, a_blk), (b_off, b_blk)):
        assert off[0] % blk[0] == 0 and off[1] % blk[1] == 0, (name, off, blk)
    ao = (a_off[0] // a_blk[0], a_off[1] // a_blk[1])
    bo = (b_off[0] // b_blk[0], b_off[1] // b_blk[1])
    if mode == "tn":
        a_map = lambda i, j, k: (k + ao[0], i + ao[1])
        b_map = lambda i, j, k: (k + bo[0], j + bo[1])
    elif mode == "nt":
        a_map = lambda i, j, k: (i + ao[0], k + ao[1])
        b_map = lambda i, j, k: (j + bo[0], k + bo[1])
    else:
        a_map = lambda i, j, k: (i + ao[0], k + ao[1])
        b_map = lambda i, j, k: (k + bo[0], j + bo[1])
    has_res = res is not None

    def body(*refs):
        if has_res:
            a_ref, b_ref, r_ref, o_ref, acc = refs
        else:
            a_ref, b_ref, o_ref, acc = refs
        k = pl.program_id(2)

        @pl.when(k == 0)
        def _():
            acc[...] = jnp.zeros_like(acc)

        acc[...] += _dot(a_ref[...].astype(BF16), b_ref[...].astype(BF16), dims)

        @pl.when(k == nk - 1)
        def _():
            out = acc[...] * scale
            if has_res:
                out = r_ref[...] + out
            o_ref[...] = out.astype(out_dtype)

    in_specs = [pl.BlockSpec(a_blk, a_map), pl.BlockSpec(b_blk, b_map)]
    args = [a, b]
    if has_res:
        in_specs.append(pl.BlockSpec((tm, tn), lambda i, j, k: (i, j)))
        args.append(res)
    return pl.pallas_call(
        body, name=name, grid=(M // tm, N // tn, nk), in_specs=in_specs,
        out_specs=pl.BlockSpec((tm, tn), lambda i, j, k: (i, j)),
        out_shape=jax.ShapeDtypeStruct((M, N), out_dtype),
        scratch_shapes=[pltpu.VMEM((tm, tn), F32)],
        compiler_params=_params("parallel", "parallel", "arbitrary"),
    )(*args)


def _rmsnorm_fwd(x, g, name):
    T, D = x.shape
    tm = _pick(T, 256, 8)

    def body(x_ref, g_ref, y_ref):
        xv = x_ref[...]
        r = lax.rsqrt(jnp.mean(xv * xv, axis=-1, keepdims=True) + EPS)
        y_ref[...] = (xv * r * g_ref[...]).astype(BF16)

    return pl.pallas_call(
        body, name=name, grid=(T // tm,),
        in_specs=[pl.BlockSpec((tm, D), lambda i: (i, 0)), pl.BlockSpec((1, D), lambda i: (0, 0))],
        out_specs=pl.BlockSpec((tm, D), lambda i: (i, 0)),
        out_shape=jax.ShapeDtypeStruct((T, D), BF16), compiler_params=_params("parallel"),
    )(x, g.reshape(1, D))


def _rmsnorm_bwd(x, g, dy, dres, name):
    T, D = x.shape
    tm = _pick(T, 256, 8)

    def body(x_ref, g_ref, dy_ref, dres_ref, dx_ref, dg_ref):
        xv = x_ref[...]
        r = lax.rsqrt(jnp.mean(xv * xv, axis=-1, keepdims=True) + EPS)
        xn = xv * r
        d = dy_ref[...]
        dgy = d * g_ref[...]
        dx_ref[...] = dres_ref[...] + r * (dgy - xn * jnp.mean(dgy * xn, axis=-1, keepdims=True))

        @pl.when(pl.program_id(0) == 0)
        def _():
            dg_ref[...] = jnp.zeros_like(dg_ref)

        dg_ref[...] += jnp.sum(d * xn, axis=0, keepdims=True)

    row = pl.BlockSpec((tm, D), lambda i: (i, 0))
    vec = pl.BlockSpec((1, D), lambda i: (0, 0))
    dx, dg = pl.pallas_call(
        body, name=name, grid=(T // tm,), in_specs=[row, vec, row, row], out_specs=[row, vec],
        out_shape=[jax.ShapeDtypeStruct((T, D), F32), jax.ShapeDtypeStruct((1, D), F32)],
        compiler_params=_params("arbitrary"),
    )(x, g.reshape(1, D), dy, dres)
    return dx, dg.reshape(D)


def _sigmoid(x):
    return 1.0 / (1.0 + jnp.exp(-x))


def _swiglu_fwd(gu, name):
    T, F2 = gu.shape
    F = F2 // 2
    tm, tn = _pick(T, 512, 8), _pick(F, 512)
    nf = F // tn

    def body(g_ref, u_ref, o_ref):
        gv = g_ref[...]
        o_ref[...] = (gv * _sigmoid(gv) * u_ref[...]).astype(BF16)

    return pl.pallas_call(
        body, name=name, grid=(T // tm, nf),
        in_specs=[pl.BlockSpec((tm, tn), lambda i, j: (i, j)), pl.BlockSpec((tm, tn), lambda i, j: (i, j + nf))],
        out_specs=pl.BlockSpec((tm, tn), lambda i, j: (i, j)),
        out_shape=jax.ShapeDtypeStruct((T, F), BF16), compiler_params=_params("parallel", "parallel"),
    )(gu, gu)


def _swiglu_bwd(gu, dact, name):
    T, F2 = gu.shape
    F = F2 // 2
    tm, tn = _pick(T, 512, 8), _pick(F, 512)
    nf = F // tn

    def body(g_ref, u_ref, d_ref, o_ref):
        gv, uv, dv = g_ref[...], u_ref[...], d_ref[...]
        s = _sigmoid(gv)
        is_gate = pl.program_id(1) < nf
        dgate = dv * uv * (s + gv * s * (1.0 - s))
        dup = dv * gv * s
        o_ref[...] = jnp.where(is_gate, dgate, dup).astype(BF16)

    return pl.pallas_call(
        body, name=name, grid=(T // tm, 2 * nf),
        in_specs=[pl.BlockSpec((tm, tn), lambda i, j: (i, j % nf)),
                  pl.BlockSpec((tm, tn), lambda i, j: (i, j % nf + nf)),
                  pl.BlockSpec((tm, tn), lambda i, j: (i, j % nf))],
        out_specs=pl.BlockSpec((tm, tn), lambda i, j: (i, j)),
        out_shape=jax.ShapeDtypeStruct((T, F2), BF16), compiler_params=_params("parallel", "parallel"),
    )(gu, gu, dact)


def _iota2(shape, dim):
    return lax.broadcasted_iota(jnp.int32, shape, dim)


def _head_ones():
    r, c = _iota2((LANES, LANES), 0), _iota2((LANES, LANES), 1)
    return ((r >> 6) == (c >> 6)).astype(F32)


def _rot_matrix(transpose):
    r, c = _iota2((LANES, LANES), 0), _iota2((LANES, LANES), 1)
    if transpose:
        r, c = c, r
    half = ROT_DIM // 2
    cm = c & (HEAD_DIM - 1)
    neg = (r == c + half) & (cm < half)
    pos = (r == c - half) & (cm >= half) & (cm < ROT_DIM)
    return pos.astype(F32) - neg.astype(F32)


def _rope_tables(positions, name):
    T = positions.size
    tm = _pick(T, 1024, 8)
    inv = ROPE_THETA ** (-jnp.arange(0, ROT_DIM, 2, dtype=F32) / ROT_DIM)
    lane = jnp.concatenate([inv, inv, jnp.zeros((HEAD_DIM - ROT_DIM,), F32)])
    lane = jnp.tile(lane, 2).reshape(1, LANES)

    def body(p_ref, f_ref, c_ref, s_ref):
        ang = p_ref[...].astype(F32) * f_ref[...]
        c_ref[...] = jnp.cos(ang)
        s_ref[...] = jnp.sin(ang)

    out = jax.ShapeDtypeStruct((T, LANES), F32)
    blk = pl.BlockSpec((tm, LANES), lambda i: (i, 0))
    return pl.pallas_call(
        body, name=name, grid=(T // tm,),
        in_specs=[pl.BlockSpec((tm, 1), lambda i: (i, 0)), pl.BlockSpec((1, LANES), lambda i: (0, 0))],
        out_specs=[blk, blk], out_shape=[out, out], compiler_params=_params("parallel"),
    )(positions.reshape(T, 1), lane)


def _qknorm_fwd(src, col_off, gain, cs, D, name):
    T = src.shape[0]
    HP = D // LANES
    tm = _pick(T, 512, 8)
    co = col_off // LANES
    rope = cs is not None

    def body(*refs):
        if rope:
            x_ref, g_ref, c_ref, s_ref, o_ref = refs
        else:
            x_ref, g_ref, o_ref = refs
        xv = x_ref[...]
        ms = _dot(xv * xv, _head_ones(), NN, HIGHEST) * (1.0 / HEAD_DIM)
        y = xv * lax.rsqrt(ms + EPS) * g_ref[...]
        if rope:
            y = y * c_ref[...] + _dot(y, _rot_matrix(False), NN, HIGHEST) * s_ref[...]
        o_ref[...] = y.astype(BF16)

    in_specs = [pl.BlockSpec((tm, LANES), lambda i, h: (i, co + h)), pl.BlockSpec((1, LANES), lambda i, h: (0, 0))]
    args = [src, jnp.tile(gain.reshape(1, HEAD_DIM), (1, 2))]
    if rope:
        in_specs += [pl.BlockSpec((tm, LANES), lambda i, h: (i, 0))] * 2
        args += list(cs)
    return pl.pallas_call(
        body, name=name, grid=(T // tm, HP), in_specs=in_specs,
        out_specs=pl.BlockSpec((tm, LANES), lambda i, h: (i, h)),
        out_shape=jax.ShapeDtypeStruct((T, D), BF16), compiler_params=_params("parallel", "parallel"),
    )(*args)


def _qknorm_bwd(src, col_off, gain, cs, dout, D, name):
    T = src.shape[0]
    HP = D // LANES
    tm = _pick(T, 512, 8)
    co = col_off // LANES
    rope = cs is not None

    def body(*refs):
        if rope:
            x_ref, g_ref, d_ref, c_ref, s_ref, dx_ref, dg_ref = refs
        else:
            x_ref, g_ref, d_ref, dx_ref, dg_ref = refs
        xv = x_ref[...]
        ones = _head_ones()
        ms = _dot(xv * xv, ones, NN, HIGHEST) * (1.0 / HEAD_DIM)
        r = lax.rsqrt(ms + EPS)
        xn = xv * r
        d = d_ref[...]
        if rope:
            d = d * c_ref[...] + _dot(d * s_ref[...], _rot_matrix(True), NN, HIGHEST)
        dgy = d * g_ref[...]
        mean = _dot(dgy * xn, ones, NN, HIGHEST) * (1.0 / HEAD_DIM)
        dx_ref[...] = (r * (dgy - xn * mean)).astype(BF16)

        @pl.when(pl.program_id(1) == 0)
        def _():
            dg_ref[...] = jnp.zeros_like(dg_ref)

        dg_ref[...] += jnp.sum(d * xn, axis=0, keepdims=True)[None]

    in_specs = [pl.BlockSpec((tm, LANES), lambda h, i: (i, co + h)), pl.BlockSpec((1, LANES), lambda h, i: (0, 0)),
                pl.BlockSpec((tm, LANES), lambda h, i: (i, h))]
    args = [src, jnp.tile(gain.reshape(1, HEAD_DIM), (1, 2)), dout]
    if rope:
        in_specs += [pl.BlockSpec((tm, LANES), lambda h, i: (i, 0))] * 2
        args += list(cs)
    dx, dg = pl.pallas_call(
        body, name=name, grid=(HP, T // tm), in_specs=in_specs,
        out_specs=[pl.BlockSpec((tm, LANES), lambda h, i: (i, h)), pl.BlockSpec((1, 1, LANES), lambda h, i: (h, 0, 0))],
        out_shape=[jax.ShapeDtypeStruct((T, D), BF16), jax.ShapeDtypeStruct((HP, 1, LANES), F32)],
        compiler_params=_params("parallel", "arbitrary"),
    )(*args)
    return dx, dg.reshape(2 * HP, HEAD_DIM).sum(axis=0)


def _head_rowsum(a, b, name):
    T, D = a.shape
    tm = _pick(T, 1024, 8)

    def body(a_ref, b_ref, o_ref):
        o_ref[...] = _dot(a_ref[...] * b_ref[...], _head_ones(), NN, HIGHEST)

    blk = pl.BlockSpec((tm, LANES), lambda i, h: (i, h))
    return pl.pallas_call(
        body, name=name, grid=(T // tm, D // LANES), in_specs=[blk, blk], out_specs=blk,
        out_shape=jax.ShapeDtypeStruct((T, D), F32), compiler_params=_params("parallel", "parallel"),
    )(a, b)


def _lane_masks():
    lane = _iota2((1, LANES), 1)
    return (lane < HEAD_DIM, lane >= HEAD_DIM)


def _band_fwd(qh, kh, qkv, g, B, S, D, name):
    window, dil = DILATED_GROUPS[g]
    assert window // dil == BAND_BLOCK and S % (dil * BAND_BLOCK) == 0
    Sd, HP = S // dil, D // LANES
    nb = Sd // BAND_BLOCK
    vo = (3 * g + 2) * HP
    blk = (1, BAND_BLOCK, LANES)

    def body(q_ref, kc_ref, kp_ref, vc_ref, vp_ref, o_ref, lse_ref):
        n = pl.program_id(2)
        q, kc, kp = q_ref[0], kc_ref[0], kp_ref[0]
        vc, vp = vc_ref[0].astype(BF16), vp_ref[0].astype(BF16)
        qi, kj = _iota2((BAND_BLOCK, BAND_BLOCK), 0), _iota2((BAND_BLOCK, BAND_BLOCK), 1)
        cur_ok = kj <= qi
        prev_ok = (kj >= qi) & (n > 0)
        o_acc = jnp.zeros((BAND_BLOCK, LANES), F32)
        l_acc = jnp.zeros((BAND_BLOCK, LANES), F32)
        for hm in _lane_masks():
            qm = jnp.where(hm, q, jnp.zeros_like(q))
            sc = jnp.where(cur_ok, _dot(qm, kc, NT) * SCALE, NEG)
            sp = jnp.where(prev_ok, _dot(qm, kp, NT) * SCALE, NEG)
            m = jnp.maximum(jnp.max(sc, axis=-1, keepdims=True), jnp.max(sp, axis=-1, keepdims=True))
            ec, ep = jnp.exp(sc - m), jnp.exp(sp - m)
            l = jnp.sum(ec, axis=-1, keepdims=True) + jnp.sum(ep, axis=-1, keepdims=True)
            inv = 1.0 / l
            oh = _dot((ec * inv).astype(BF16), vc, NN) + _dot((ep * inv).astype(BF16), vp, NN)
            o_acc = jnp.where(hm, oh, o_acc)
            l_acc = jnp.where(hm, m + jnp.log(l), l_acc)
        o_ref[0] = o_acc
        lse_ref[0] = l_acc

    cur = lambda b, r, n, h: (b, n, r * HP + h)
    prev = lambda b, r, n, h: (b, jnp.maximum(n - 1, 0), r * HP + h)
    vcur = lambda b, r, n, h: (b, n, r * 9 * HP + vo + h)
    vprev = lambda b, r, n, h: (b, jnp.maximum(n - 1, 0), r * 9 * HP + vo + h)
    out = jax.ShapeDtypeStruct((B, Sd, dil * D), F32)
    o, lse = pl.pallas_call(
        body, name=name, grid=(B, dil, nb, HP),
        in_specs=[pl.BlockSpec(blk, cur), pl.BlockSpec(blk, cur), pl.BlockSpec(blk, prev),
                  pl.BlockSpec(blk, vcur), pl.BlockSpec(blk, vprev)],
        out_specs=[pl.BlockSpec(blk, cur), pl.BlockSpec(blk, cur)], out_shape=[out, out],
        compiler_params=_params("parallel", "parallel", "parallel", "parallel"),
    )(qh.reshape(B, Sd, dil * D), kh.reshape(B, Sd, dil * D), kh.reshape(B, Sd, dil * D),
      qkv.reshape(B, Sd, dil * 9 * D), qkv.reshape(B, Sd, dil * 9 * D))
    return o.reshape(B * S, D), lse.reshape(B * S, D)


def _head_col(x, h):
    lane = _iota2((1, LANES), 1)
    return jnp.sum(jnp.where(lane == h * HEAD_DIM, x, 0.0), axis=-1, keepdims=True)


def _band_bwd(qh, kh, qkv, do, lse, delta, g, B, S, D, name):
    window, dil = DILATED_GROUPS[g]
    Sd, HP = S // dil, D // LANES
    nb = Sd // BAND_BLOCK
    vo = (3 * g + 2) * HP
    blk = (1, BAND_BLOCK, LANES)

    def body(q_ref, qn_ref, do_ref, don_ref, l_ref, ln_ref, d_ref, dn_ref, kc_ref, kp_ref, vc_ref, vp_ref,
             dq_ref, dk_ref, dv_ref):
        n = pl.program_id(2)
        q, qn, kc, kp = q_ref[0], qn_ref[0], kc_ref[0], kp_ref[0]
        dout, doutn = do_ref[0].astype(BF16), don_ref[0].astype(BF16)
        vc, vp = vc_ref[0].astype(BF16), vp_ref[0].astype(BF16)
        qi, kj = _iota2((BAND_BLOCK, BAND_BLOCK), 0), _iota2((BAND_BLOCK, BAND_BLOCK), 1)
        cur_ok = kj <= qi
        prev_ok = (kj >= qi) & (n > 0)
        next_ok = (kj >= qi) & (n < nb - 1)
        dq = jnp.zeros((BAND_BLOCK, LANES), F32)
        dk = jnp.zeros((BAND_BLOCK, LANES), F32)
        dv = jnp.zeros((BAND_BLOCK, LANES), F32)
        zero = jnp.zeros_like(q)
        for h, hm in enumerate(_lane_masks()):
            qm, qnm = jnp.where(hm, q, zero), jnp.where(hm, qn, zero)
            dom, donm = jnp.where(hm, dout, zero), jnp.where(hm, doutn, zero)
            kcm, kpm = jnp.where(hm, kc, zero), jnp.where(hm, kp, zero)
            lse_q, lse_n = _head_col(l_ref[0], h), _head_col(ln_ref[0], h)
            del_q, del_n = _head_col(d_ref[0], h), _head_col(dn_ref[0], h)
            pa = jnp.exp(jnp.where(cur_ok, _dot(qm, kc, NT) * SCALE - lse_q, NEG))
            pb = jnp.exp(jnp.where(prev_ok, _dot(qm, kp, NT) * SCALE - lse_q, NEG))
            pc = jnp.exp(jnp.where(next_ok, _dot(qnm, kc, NT) * SCALE - lse_n, NEG))
            dsa = (pa * (_dot(dom, vc, NT) - del_q)).astype(BF16)
            dsb = (pb * (_dot(dom, vp, NT) - del_q)).astype(BF16)
            dsc = (pc * (_dot(donm, vc, NT) - del_n)).astype(BF16)
            dq += (_dot(dsa, kcm, NN) + _dot(dsb, kpm, NN)) * SCALE
            dk += (_dot(dsa, qm, TN) + _dot(dsc, qnm, TN)) * SCALE
            dv += _dot(pa.astype(BF16), dom, TN) + _dot(pc.astype(BF16), donm, TN)
        dq_ref[0] = dq
        dk_ref[0] = dk
        dv_ref[0] = dv.astype(BF16)

    cur = lambda b, r, n, h: (b, n, r * HP + h)
    prev = lambda b, r, n, h: (b, jnp.maximum(n - 1, 0), r * HP + h)
    nxt = lambda b, r, n, h: (b, jnp.minimum(n + 1, nb - 1), r * HP + h)
    vcur = lambda b, r, n, h: (b, n, r * 9 * HP + vo + h)
    vprev = lambda b, r, n, h: (b, jnp.maximum(n - 1, 0), r * 9 * HP + vo + h)
    shp = (B, Sd, dil * D)
    v3 = qkv.reshape(B, Sd, dil * 9 * D)
    q3, k3, do3, l3, d3 = (t.reshape(shp) for t in (qh, kh, do, lse, delta))
    dq, dk, dv = pl.pallas_call(
        body, name=name, grid=(B, dil, nb, HP),
        in_specs=[pl.BlockSpec(blk, cur), pl.BlockSpec(blk, nxt), pl.BlockSpec(blk, cur), pl.BlockSpec(blk, nxt),
                  pl.BlockSpec(blk, cur), pl.BlockSpec(blk, nxt), pl.BlockSpec(blk, cur), pl.BlockSpec(blk, nxt),
                  pl.BlockSpec(blk, cur), pl.BlockSpec(blk, prev), pl.BlockSpec(blk, vcur), pl.BlockSpec(blk, vprev)],
        out_specs=[pl.BlockSpec(blk, cur)] * 3,
        out_shape=[jax.ShapeDtypeStruct(shp, F32), jax.ShapeDtypeStruct(shp, F32), jax.ShapeDtypeStruct(shp, BF16)],
        compiler_params=_params("parallel", "parallel", "parallel", "parallel"),
    )(q3, q3, do3, do3, l3, l3, d3, d3, k3, k3, v3, v3)
    return dq.reshape(B * S, D), dk.reshape(B * S, D), dv.reshape(B * S, D)


def _mix(os_, lses, name):
    T, D = os_[0].shape
    tm = _pick(T, 256, 8)

    def body(o0, o1, o2, l0, l1, l2, mix_ref, lse_ref):
        a, b, c = l0[...], l1[...], l2[...]
        m = jnp.maximum(jnp.maximum(a, b), c)
        ea, eb, ec = jnp.exp(a - m), jnp.exp(b - m), jnp.exp(c - m)
        s = ea + eb + ec
        mix_ref[...] = (ea / s) * o0[...] + (eb / s) * o1[...] + (ec / s) * o2[...]
        lse_ref[...] = m + jnp.log(s)

    blk = pl.BlockSpec((tm, D), lambda i: (i, 0))
    out = jax.ShapeDtypeStruct((T, D), F32)
    return pl.pallas_call(
        body, name=name, grid=(T // tm,), in_specs=[blk] * 6, out_specs=[blk, blk], out_shape=[out, out],
        compiler_params=_params("parallel"),
    )(*os_, *lses)


def _gate_fwd(fpre, b_f, B, S, D, name):
    nb = S // LANES

    def body(f_ref, b_ref, cum_ref, rep_ref, carry):
        @pl.when(pl.program_id(1) == 0)
        def _():
            carry[...] = jnp.zeros_like(carry)

        z = f_ref[...] + b_ref[...]
        lf = jnp.minimum(z, 0.0) - jnp.log(1.0 + jnp.exp(-jnp.abs(z)))
        tri = (_iota2((LANES, LANES), 0) >= _iota2((LANES, LANES), 1)).astype(F32)
        c = _dot(tri, lf, NN, HIGHEST) + carry[...]
        cum_ref[...] = c
        carry[...] += jnp.sum(lf, axis=0, keepdims=True)
        spread = ((_iota2((LANES, D), 1) >> 6) == _iota2((LANES, D), 0)).astype(F32)
        rep_ref[...] = _dot(c, spread, NN, HIGHEST)

    return pl.pallas_call(
        body, name=name, grid=(B, nb),
        in_specs=[pl.BlockSpec((LANES, LANES), lambda b, j: (b * nb + j, 0)), pl.BlockSpec((1, LANES), lambda b, j: (0, 0))],
        out_specs=[pl.BlockSpec((LANES, LANES), lambda b, j: (b * nb + j, 0)), pl.BlockSpec((LANES, D), lambda b, j: (b * nb + j, 0))],
        out_shape=[jax.ShapeDtypeStruct((B * S, LANES), F32), jax.ShapeDtypeStruct((B * S, D), F32)],
        scratch_shapes=[pltpu.VMEM((1, LANES), F32)], compiler_params=_params("arbitrary", "arbitrary"),
    )(fpre, b_f)


def _gate_bwd(dcum_k, dcum_q, fpre, b_f, B, S, name):
    nb = S // LANES

    def body(d_ref, dq_ref, f_ref, b_ref, o_ref, db_ref, carry):
        first = (pl.program_id(0) == 0) & (pl.program_id(1) == 0)

        @pl.when(pl.program_id(1) == 0)
        def _():
            carry[...] = jnp.zeros_like(carry)

        @pl.when(first)
        def _():
            db_ref[...] = jnp.zeros_like(db_ref)

        d = d_ref[...] + dq_ref[...]
        tri = (_iota2((LANES, LANES), 0) <= _iota2((LANES, LANES), 1)).astype(F32)
        dlf = _dot(tri, d, NN, HIGHEST) + carry[...]
        carry[...] += jnp.sum(d, axis=0, keepdims=True)
        z = f_ref[...] + b_ref[...]
        dz = dlf * (1.0 / (1.0 + jnp.exp(z)))
        o_ref[...] = dz.astype(BF16)
        db_ref[...] += jnp.sum(dz, axis=0, keepdims=True)

    rev = lambda b, j: (b * nb + nb - 1 - j, 0)
    blk = pl.BlockSpec((LANES, LANES), rev)
    vec = pl.BlockSpec((1, LANES), lambda b, j: (0, 0))
    return pl.pallas_call(
        body, name=name, grid=(B, nb), in_specs=[blk, blk, blk, vec], out_specs=[blk, vec],
        out_shape=[jax.ShapeDtypeStruct((B * S, LANES), BF16), jax.ShapeDtypeStruct((1, LANES), F32)],
        scratch_shapes=[pltpu.VMEM((1, LANES), F32)], compiler_params=_params("arbitrary", "arbitrary"),
    )(dcum_k, dcum_q, fpre, b_f)


def _fox_scores(q, k, cq, ck_ref, h, hm, ok):
    qm = jnp.where(hm, q, jnp.zeros_like(q))
    s = _dot(qm, k, NT) * SCALE + (_head_col(cq, h) - ck_ref[0, h:h + 1, :])
    return qm, jnp.where(ok, s, NEG)


def _fox_fwd(qh, kh, kv, cq, ck, B, S, D, name):
    HP = D // LANES
    TB = _pick(S, 512)
    nb = S // TB

    def body(q_ref, k_ref, v_ref, cq_ref, ck_ref, o_ref, lse_ref, m_sc, l_sc, acc):
        qb, kb = pl.program_id(1), pl.program_id(3)

        @pl.when(kb == 0)
        def _():
            m_sc[...] = jnp.full_like(m_sc, NEG)
            l_sc[...] = jnp.zeros_like(l_sc)
            acc[...] = jnp.zeros_like(acc)

        @pl.when(kb <= qb)
        def _():
            q, k, v = q_ref[0], k_ref[0], v_ref[0].astype(BF16)
            cqv = cq_ref[0]
            ok = (_iota2((TB, TB), 1) <= _iota2((TB, TB), 0)) | (kb < qb)
            a = acc[...]
            for h, hm in enumerate(_lane_masks()):
                _, s = _fox_scores(q, k, cqv, ck_ref, h, hm, ok)
                m_prev = jnp.max(m_sc[h], axis=-1, keepdims=True)
                l_prev = jnp.max(l_sc[h], axis=-1, keepdims=True)
                m_new = jnp.maximum(m_prev, jnp.max(s, axis=-1, keepdims=True))
                alpha = jnp.exp(m_prev - m_new)
                p = jnp.exp(s - m_new)
                l_new = alpha * l_prev + jnp.sum(p, axis=-1, keepdims=True)
                a = jnp.where(hm, alpha * a + _dot(p.astype(BF16), v, NN), a)
                m_sc[h] = jnp.broadcast_to(m_new, (TB, LANES))
                l_sc[h] = jnp.broadcast_to(l_new, (TB, LANES))
            acc[...] = a

        @pl.when(kb == qb)
        def _():
            a = acc[...]
            out = a
            lse = jnp.zeros_like(a)
            for h, hm in enumerate(_lane_masks()):
                out = jnp.where(hm, a / l_sc[h], out)
                lse = jnp.where(hm, m_sc[h] + jnp.log(l_sc[h]), lse)
            o_ref[0] = out
            lse_ref[0] = lse

    blk = (1, TB, LANES)
    qmap = lambda b, i, h, j: (b, i, h)
    kmap = lambda b, i, h, j: (b, jnp.minimum(j, i), h)
    vmap = lambda b, i, h, j: (b, jnp.minimum(j, i), HP + h)
    cmap = lambda b, i, h, j: (b * HP + h, 0, jnp.minimum(j, i))
    out = jax.ShapeDtypeStruct((B, S, D), F32)
    o, lse = pl.pallas_call(
        body, name=name, grid=(B, nb, HP, nb),
        in_specs=[pl.BlockSpec(blk, qmap), pl.BlockSpec(blk, kmap), pl.BlockSpec(blk, vmap), pl.BlockSpec(blk, qmap),
                  pl.BlockSpec((1, 8, TB), cmap)],
        out_specs=[pl.BlockSpec(blk, qmap)] * 2, out_shape=[out, out],
        scratch_shapes=[pltpu.VMEM((2, TB, LANES), F32), pltpu.VMEM((2, TB, LANES), F32), pltpu.VMEM((TB, LANES), F32)],
        compiler_params=_params("parallel", "parallel", "parallel", "arbitrary"),
    )(qh.reshape(B, S, D), kh.reshape(B, S, D), kv.reshape(B, S, 2 * D), cq.reshape(B, S, D), ck)
    return o.reshape(B * S, D), lse.reshape(B * S, D)


def _fox_bwd_dq(qh, kh, kv, cq, ck, do, lse, delta, B, S, D, name):
    HP = D // LANES
    TB = _pick(S, 512)
    nb = S // TB

    def body(q_ref, k_ref, v_ref, cq_ref, ck_ref, do_ref, l_ref, d_ref, dq_ref, dc_ref, acc, acc_c):
        qb, kb = pl.program_id(1), pl.program_id(3)

        @pl.when(kb == 0)
        def _():
            acc[...] = jnp.zeros_like(acc)
            acc_c[...] = jnp.zeros_like(acc_c)

        @pl.when(kb <= qb)
        def _():
            q, k, v = q_ref[0], k_ref[0], v_ref[0].astype(BF16)
            dout = do_ref[0].astype(BF16)
            ok = (_iota2((TB, TB), 1) <= _iota2((TB, TB), 0)) | (kb < qb)
            a, c = acc[...], acc_c[...]
            for h, hm in enumerate(_lane_masks()):
                _, s = _fox_scores(q, k, cq_ref[0], ck_ref, h, hm, ok)
                p = jnp.exp(s - _head_col(l_ref[0], h))
                dom = jnp.where(hm, dout, jnp.zeros_like(dout))
                ds = p * (_dot(dom, v, NT) - _head_col(d_ref[0], h))
                a += _dot(ds.astype(BF16), jnp.where(hm, k, jnp.zeros_like(k)), NN) * SCALE
                c = jnp.where(hm, c + jnp.sum(ds, axis=-1, keepdims=True), c)
            acc[...] = a
            acc_c[...] = c

        @pl.when(kb == qb)
        def _():
            dq_ref[0] = acc[...]
            dc_ref[0] = acc_c[...]

    blk = (1, TB, LANES)
    qmap = lambda b, i, h, j: (b, i, h)
    kmap = lambda b, i, h, j: (b, jnp.minimum(j, i), h)
    vmap = lambda b, i, h, j: (b, jnp.minimum(j, i), HP + h)
    cmap = lambda b, i, h, j: (b * HP + h, 0, jnp.minimum(j, i))
    r3 = lambda t: t.reshape(B, S, D)
    out = jax.ShapeDtypeStruct((B, S, D), F32)
    dq, dc = pl.pallas_call(
        body, name=name, grid=(B, nb, HP, nb),
        in_specs=[pl.BlockSpec(blk, qmap), pl.BlockSpec(blk, kmap), pl.BlockSpec(blk, vmap), pl.BlockSpec(blk, qmap),
                  pl.BlockSpec((1, 8, TB), cmap), pl.BlockSpec(blk, qmap), pl.BlockSpec(blk, qmap), pl.BlockSpec(blk, qmap)],
        out_specs=[pl.BlockSpec(blk, qmap)] * 2, out_shape=[out, out],
        scratch_shapes=[pltpu.VMEM((TB, LANES), F32), pltpu.VMEM((TB, LANES), F32)],
        compiler_params=_params("parallel", "parallel", "parallel", "arbitrary"),
    )(r3(qh), r3(kh), kv.reshape(B, S, 2 * D), r3(cq), ck, r3(do), r3(lse), r3(delta))
    return dq.reshape(B * S, D), dc.reshape(B * S, D)


def _fox_bwd_dkv(qh, kh, kv, cq, ck, do, lse, delta, B, S, D, name):
    HP = D // LANES
    TB = _pick(S, 512)
    nb = S // TB

    def body(q_ref, k_ref, v_ref, cq_ref, ck_ref, do_ref, l_ref, d_ref, dk_ref, dv_ref, dc_ref, dk_acc, dv_acc, dc_acc):
        kb, qb = pl.program_id(1), pl.program_id(3)

        @pl.when(qb == 0)
        def _():
            dk_acc[...] = jnp.zeros_like(dk_acc)
            dv_acc[...] = jnp.zeros_like(dv_acc)
            dc_acc[...] = jnp.zeros_like(dc_acc)

        @pl.when(qb >= kb)
        def _():
            q, k, v = q_ref[0], k_ref[0], v_ref[0].astype(BF16)
            dout = do_ref[0].astype(BF16)
            ok = (_iota2((TB, TB), 1) <= _iota2((TB, TB), 0)) | (qb > kb)
            dk, dv, dc = dk_acc[...], dv_acc[...], dc_acc[...]
            row = _iota2((8, TB), 0)
            for h, hm in enumerate(_lane_masks()):
                qm, s = _fox_scores(q, k, cq_ref[0], ck_ref, h, hm, ok)
                p = jnp.exp(s - _head_col(l_ref[0], h))
                dom = jnp.where(hm, dout, jnp.zeros_like(dout))
                ds = p * (_dot(dom, v, NT) - _head_col(d_ref[0], h))
                dv += _dot(p.astype(BF16), dom, TN)
                dk += _dot(ds.astype(BF16), qm, TN) * SCALE
                dc = jnp.where(row == h, dc - jnp.sum(ds, axis=0, keepdims=True), dc)
            dk_acc[...] = dk
            dv_acc[...] = dv
            dc_acc[...] = dc

        @pl.when(qb == nb - 1)
        def _():
            dk_ref[0] = dk_acc[...]
            dv_ref[0] = dv_acc[...].astype(BF16)
            dc_ref[0] = dc_acc[...]

    blk = (1, TB, LANES)
    qmap = lambda b, j, h, i: (b, jnp.maximum(i, j), h)
    kmap = lambda b, j, h, i: (b, j, h)
    vmap = lambda b, j, h, i: (b, j, HP + h)
    cmap = lambda b, j, h, i: (b * HP + h, 0, j)
    r3 = lambda t: t.reshape(B, S, D)
    dk, dv, dc = pl.pallas_call(
        body, name=name, grid=(B, nb, HP, nb),
        in_specs=[pl.BlockSpec(blk, qmap), pl.BlockSpec(blk, kmap), pl.BlockSpec(blk, vmap), pl.BlockSpec(blk, qmap),
                  pl.BlockSpec((1, 8, TB), cmap), pl.BlockSpec(blk, qmap), pl.BlockSpec(blk, qmap), pl.BlockSpec(blk, qmap)],
        out_specs=[pl.BlockSpec(blk, kmap), pl.BlockSpec(blk, kmap), pl.BlockSpec((1, 8, TB), cmap)],
        out_shape=[jax.ShapeDtypeStruct((B, S, D), F32), jax.ShapeDtypeStruct((B, S, D), BF16),
                   jax.ShapeDtypeStruct((B * HP, 8, S), F32)],
        scratch_shapes=[pltpu.VMEM((TB, LANES), F32), pltpu.VMEM((TB, LANES), F32), pltpu.VMEM((8, TB), F32)],
        compiler_params=_params("parallel", "parallel", "parallel", "arbitrary"),
    )(r3(qh), r3(kh), kv.reshape(B, S, 2 * D), r3(cq), ck, r3(do), r3(lse), r3(delta))
    return dk.reshape(B * S, D), dv.reshape(B * S, D), dc


def _loss_and_grad(y, target, name):
    T, D = y.shape
    tm = _pick(T, 256, 8)

    def body(y_ref, t_ref, p_ref, dy_ref):
        e = y_ref[...] - t_ref[...]
        dy_ref[...] = e * (1.0 / D)
        p_ref[...] = jnp.sum(e * e, axis=0, keepdims=True)[None]

    blk = pl.BlockSpec((tm, D), lambda i: (i, 0))
    return pl.pallas_call(
        body, name=name, grid=(T // tm,), in_specs=[blk, blk],
        out_specs=[pl.BlockSpec((1, 1, D), lambda i: (i, 0, 0)), blk],
        out_shape=[jax.ShapeDtypeStruct((T // tm, 1, D), F32), jax.ShapeDtypeStruct((T, D), F32)],
        compiler_params=_params("parallel"),
    )(y, target)


def _adamw(parts, w, m, v, name):
    R = w.shape[0]
    tr = _pick(R, PACK_ROWS, 16)

    def body(p_ref, w_ref, m_ref, v_ref, g_out, d_out, m_out, v_out):
        g = p_ref[0].astype(F32)
        for s in range(1, N_DEV):
            g = g + p_ref[s].astype(F32)
        mn = ADAM_B1 * m_ref[...] + (1.0 - ADAM_B1) * g
        vn = ADAM_B2 * v_ref[...] + (1.0 - ADAM_B2) * (g * g)
        m_hat = mn / (1.0 - ADAM_B1 ** ADAM_STEP)
        v_hat = vn / (1.0 - ADAM_B2 ** ADAM_STEP)
        g_out[...] = g
        d_out[...] = -ADAM_LR * (m_hat / (jnp.sqrt(v_hat) + ADAM_EPS) + ADAM_WD * w_ref[...])
        m_out[...] = mn
        v_out[...] = vn

    blk = pl.BlockSpec((tr, LANES), lambda i: (i, 0))
    out = jax.ShapeDtypeStruct((R, LANES), F32)
    return pl.pallas_call(
        body, name=name, grid=(R // tr,),
        in_specs=[pl.BlockSpec((N_DEV, tr, LANES), lambda i: (0, i, 0)), blk, blk, blk],
        out_specs=[blk] * 4, out_shape=[out] * 4, compiler_params=_params("parallel"),
    )(parts, w, m, v)


def _mesh_pos():
    return lax.axis_index("x"), lax.axis_index("y"), lax.axis_index("c")


def _all_gather(x, name):
    def body(x_ref, out_ref, send_sems, recv_sems, local_sem):
        x_, y_, c_ = _mesh_pos()
        me, sibling = (x_, y_, c_), (x_, y_, 1 - c_)
        chips = [(1 - x_, y_), (x_, 1 - y_), (1 - x_, 1 - y_)]

        def slot(px, py, pc):
            return out_ref.at[4 * px + 2 * py + pc]

        def copy(k, block, to, src=None):
            return pltpu.make_async_remote_copy(
                src_ref=slot(*block) if src is None else src, dst_ref=slot(*block),
                send_sem=send_sems.at[k], recv_sem=recv_sems.at[k], device_id=to, device_id_type=MESH_ID)

        mine = pltpu.make_async_copy(x_ref, slot(*me), local_sem)
        mine.start()
        first = [copy(0, me, sibling, src=x_ref)]
        first += [copy(1 + j, me, (*chip, c_), src=x_ref) for j, chip in enumerate(chips)]
        for cp in first:
            cp.start()
        passed = [copy(4 + j, (*chip, c_), sibling) for j, chip in enumerate(chips)]
        for j, chip in enumerate(chips):
            copy(1 + j, (*chip, c_), me).wait_recv()
            passed[j].start()
        copy(0, sibling, me).wait_recv()
        for j, chip in enumerate(chips):
            copy(4 + j, (*chip, 1 - c_), me).wait_recv()
        for cp in first + passed:
            cp.wait_send()
        mine.wait()

    return pl.pallas_call(
        body, name=name, out_shape=jax.ShapeDtypeStruct((N_DEV,) + x.shape, x.dtype),
        in_specs=[pl.BlockSpec(memory_space=pl.ANY)], out_specs=pl.BlockSpec(memory_space=pl.ANY),
        scratch_shapes=[pltpu.SemaphoreType.DMA((7,)), pltpu.SemaphoreType.DMA((7,)), pltpu.SemaphoreType.DMA],
    )(x)


def _all_to_all(x, name):
    def body(x_ref, out_ref, send_sems, recv_sems, local_sem):
        x_, y_, c_ = _mesh_pos()
        me = 4 * x_ + 2 * y_ + c_
        mine = pltpu.make_async_copy(x_ref.at[me], out_ref.at[me], local_sem)
        mine.start()
        copies = []
        for k in range(1, N_DEV):
            px, py, pc = x_ ^ (k >> 2), y_ ^ ((k >> 1) & 1), c_ ^ (k & 1)
            copies.append(pltpu.make_async_remote_copy(
                src_ref=x_ref.at[4 * px + 2 * py + pc], dst_ref=out_ref.at[me],
                send_sem=send_sems.at[k - 1], recv_sem=recv_sems.at[k - 1],
                device_id=(px, py, pc), device_id_type=MESH_ID))
        for cp in copies:
            cp.start()
        for cp in copies:
            cp.wait_recv()
        for cp in copies:
            cp.wait_send()
        mine.wait()

    return pl.pallas_call(
        body, name=name, out_shape=jax.ShapeDtypeStruct(x.shape, x.dtype),
        in_specs=[pl.BlockSpec(memory_space=pl.ANY)], out_specs=pl.BlockSpec(memory_space=pl.ANY),
        scratch_shapes=[pltpu.SemaphoreType.DMA((7,)), pltpu.SemaphoreType.DMA((7,)), pltpu.SemaphoreType.DMA],
    )(x)


def _ffn_fwd(h, norm, w_in, w_out, tag):
    T, D = h.shape
    F = w_out.shape[0]
    hn = _rmsnorm_fwd(h, norm, f"{tag}_norm")
    gu = _mm(hn, w_in, "nn", M=T, N=2 * F, K=D, name=f"{tag}_in")
    act = _swiglu_fwd(gu, f"{tag}_act")
    out = _mm(act, w_out, "nn", M=T, N=D, K=F, res=h, scale=0.5, name=f"{tag}_out")
    return out, (h, hn, gu, act)


def _ffn_bwd(dout, saved, norm, w_in, w_out, tag):
    h, hn, gu, act = saved
    T, D = h.shape
    F = w_out.shape[0]
    dact = _mm(dout, w_out, "nt", M=T, N=F, K=D, scale=0.5, name=f"{tag}_dact")
    dw_out = _mm(act, dout, "tn", M=F, N=D, K=T, scale=0.5, name=f"{tag}_dwout")
    dgu = _swiglu_bwd(gu, dact, f"{tag}_dgu")
    dhn = _mm(dgu, w_in, "nt", M=T, N=D, K=2 * F, name=f"{tag}_dhn")
    dw_in = _mm(hn, dgu, "tn", M=D, N=2 * F, K=T, name=f"{tag}_dwin")
    dh, dnorm = _rmsnorm_bwd(h, norm, dhn, dout, f"{tag}_dnorm")
    return dh, dnorm, dw_in, dw_out


def _local_step(x, positions, target, W):
    B, S, D = x.shape
    T = B * S
    H, HP = D // HEAD_DIM, D // LANES
    G = {}
    h0 = x.reshape(T, D)
    cs = _rope_tables(positions, "rope_tables")

    h1, ffn00 = _ffn_fwd(h0, W["ffn_norm"][0, 0], W["ffn_w_in"][0, 0], W["ffn_w_out"][0, 0], "ffn00")
    hna = _rmsnorm_fwd(h1, W["mix_norm"][0], "mixa_norm")
    qkv = _mm(hna, W["a_w_qkv"], "nn", M=T, N=9 * D, K=D, name="a_qkv")
    qs, ks, os_, lses = [], [], [], []
    for g in range(3):
        qs.append(_qknorm_fwd(qkv, 3 * g * D, W["a_q_norm"][0, g], cs, D, f"a_qnorm{g}"))
        ks.append(_qknorm_fwd(qkv, (3 * g + 1) * D, W["a_k_norm"][0, g], cs, D, f"a_knorm{g}"))
        o, l = _band_fwd(qs[g], ks[g], qkv, g, B, S, D, f"a_band{g}")
        os_.append(o)
        lses.append(l)
    mixed, lse_a = _mix(os_, lses, "a_mix")
    h2 = _mm(mixed, W["a_w_o"], "nn", M=T, N=D, K=D, res=h1, name="a_out")
    h3, ffn01 = _ffn_fwd(h2, W["ffn_norm"][0, 1], W["ffn_w_in"][0, 1], W["ffn_w_out"][0, 1], "ffn01")

    hnkv = _rmsnorm_fwd(h3, W["kv_norm"], "kv_norm")
    kv = _mm(hnkv, W["kv_w_kv"], "nn", M=T, N=2 * D, K=D, name="kv_proj")
    fpre = _mm(hnkv, W["kv_w_f"], "nn", M=T, N=LANES, K=D, name="kv_gate_proj")
    b_f = jnp.pad(W["kv_b_f"], (0, LANES - H)).reshape(1, LANES)
    kh = _qknorm_fwd(kv, 0, W["kv_k_norm"], None, D, "kv_knorm")
    cum, cq = _gate_fwd(fpre, b_f, B, S, D, "kv_gate")
    ck = cum.reshape(B, S, LANES)[:, :, :H].reshape(B, S, HP, 2).transpose(0, 2, 3, 1)
    ck = jnp.pad(ck, ((0, 0), (0, 0), (0, 6), (0, 0))).reshape(B * HP, 8, S)
    h4, ffn10 = _ffn_fwd(h3, W["ffn_norm"][1, 0], W["ffn_w_in"][1, 0], W["ffn_w_out"][1, 0], "ffn10")
    hnb = _rmsnorm_fwd(h4, W["mix_norm"][1], "mixb_norm")
    qraw = _mm(hnb, W["b_w_q"], "nn", M=T, N=D, K=D, name="b_q")
    qh = _qknorm_fwd(qraw, 0, W["b_q_norm"][0], None, D, "b_qnorm")
    ob, lse_b = _fox_fwd(qh, kh, kv, cq, ck, B, S, D, "b_fox")
    h5 = _mm(ob, W["b_w_o"], "nn", M=T, N=D, K=D, res=h4, name="b_out")
    h6, ffn11 = _ffn_fwd(h5, W["ffn_norm"][1, 1], W["ffn_w_in"][1, 1], W["ffn_w_out"][1, 1], "ffn11")

    sq, dy = _loss_and_grad(h6, target.reshape(T, D), "loss")

    dn = [[None, None], [None, None]]
    dwi = [[None, None], [None, None]]
    dwo = [[None, None], [None, None]]
    dh5, dn[1][1], dwi[1][1], dwo[1][1] = _ffn_bwd(dy, ffn11, W["ffn_norm"][1, 1], W["ffn_w_in"][1, 1], W["ffn_w_out"][1, 1], "ffn11")
    dob = _mm(dh5, W["b_w_o"], "nt", M=T, N=D, K=D, name="b_dout")
    G["b_w_o"] = _mm(ob, dh5, "tn", M=D, N=D, K=T, name="b_dwo")[None]
    delta_b = _head_rowsum(dob, ob, "b_delta")
    dqh, dcq = _fox_bwd_dq(qh, kh, kv, cq, ck, dob, lse_b, delta_b, B, S, D, "b_fox_dq")
    dkh, dv_b, dck = _fox_bwd_dkv(qh, kh, kv, cq, ck, dob, lse_b, delta_b, B, S, D, "b_fox_dkv")
    dqraw, dg = _qknorm_bwd(qraw, 0, W["b_q_norm"][0], None, dqh, D, "b_dqnorm")
    G["b_q_norm"] = dg[None]
    dhnb = _mm(dqraw, W["b_w_q"], "nt", M=T, N=D, K=D, name="b_dhn")
    G["b_w_q"] = _mm(hnb, dqraw, "tn", M=D, N=D, K=T, name="b_dwq")[None]
    dh4, dmix_b = _rmsnorm_bwd(h4, W["mix_norm"][1], dhnb, dh5, "mixb_dnorm")
    dh3, dn[1][0], dwi[1][0], dwo[1][0] = _ffn_bwd(dh4, ffn10, W["ffn_norm"][1, 0], W["ffn_w_in"][1, 0], W["ffn_w_out"][1, 0], "ffn10")

    dkraw, G["kv_k_norm"] = _qknorm_bwd(kv, 0, W["kv_k_norm"], None, dkh, D, "kv_dknorm")
    dcum = dck.reshape(B, HP, 8, S)[:, :, :2].transpose(0, 3, 1, 2).reshape(T, H)
    dcum = jnp.pad(dcum, ((0, 0), (0, LANES - H)))
    dcum_q = jnp.pad(dcq.reshape(T, H, HEAD_DIM)[:, :, 0], ((0, 0), (0, LANES - H)))
    dfpre, db_f = _gate_bwd(dcum, dcum_q, fpre, b_f, B, S, "kv_dgate")
    G["kv_b_f"] = db_f[0, :H]
    dhnkv = _mm(dkraw, W["kv_w_kv"], "nt", M=T, N=D, K=D, name="kv_dhn_k")
    dhnkv = _mm(dv_b, W["kv_w_kv"], "nt", M=T, N=D, K=D, b_off=(0, D), res=dhnkv, name="kv_dhn_v")
    dhnkv = _mm(dfpre, W["kv_w_f"], "nt", M=T, N=D, K=LANES, res=dhnkv, name="kv_dhn_f")
    dw_k = _mm(hnkv, dkraw, "tn", M=D, N=D, K=T, name="kv_dwk")
    dw_v = _mm(hnkv, dv_b, "tn", M=D, N=D, K=T, name="kv_dwv")
    dw_f = _mm(hnkv, dfpre, "tn", M=D, N=LANES, K=T, name="kv_dwf")
    G["kv_w"] = jnp.concatenate([dw_k, dw_v, dw_f[:, :H]], axis=1)
    dh3, G["kv_norm"] = _rmsnorm_bwd(h3, W["kv_norm"], dhnkv, dh3, "kv_dnorm")

    dh2, dn[0][1], dwi[0][1], dwo[0][1] = _ffn_bwd(dh3, ffn01, W["ffn_norm"][0, 1], W["ffn_w_in"][0, 1], W["ffn_w_out"][0, 1], "ffn01")
    dmixed = _mm(dh2, W["a_w_o"], "nt", M=T, N=D, K=D, name="a_dmixed")
    G["a_w_o"] = _mm(mixed, dh2, "tn", M=D, N=D, K=T, name="a_dwo")[None]
    delta_a = _head_rowsum(dmixed, mixed, "a_delta")
    dhna = None
    dw_qkv, dqn, dkn = [], [], []
    for g in range(3):
        dq, dk, dv = _band_bwd(qs[g], ks[g], qkv, dmixed, lse_a, delta_a, g, B, S, D, f"a_dband{g}")
        dqr, dgq = _qknorm_bwd(qkv, 3 * g * D, W["a_q_norm"][0, g], cs, dq, D, f"a_dqnorm{g}")
        dkr, dgk = _qknorm_bwd(qkv, (3 * g + 1) * D, W["a_k_norm"][0, g], cs, dk, D, f"a_dknorm{g}")
        dqn.append(dgq)
        dkn.append(dgk)
        for j, d in enumerate((dqr, dkr, dv)):
            c = 3 * g + j
            dhna = _mm(d, W["a_w_qkv"], "nt", M=T, N=D, K=D, b_off=(0, c * D), res=dhna, name=f"a_dhn{c}")
            dw_qkv.append(_mm(hna, d, "tn", M=D, N=D, K=T, name=f"a_dwqkv{c}"))
    G["a_w_qkv"] = jnp.concatenate(dw_qkv, axis=1)[None]
    G["a_q_norm"] = jnp.stack(dqn)[None]
    G["a_k_norm"] = jnp.stack(dkn)[None]
    dh1, dmix_a = _rmsnorm_bwd(h1, W["mix_norm"][0], dhna, dh2, "mixa_dnorm")
    dh0, dn[0][0], dwi[0][0], dwo[0][0] = _ffn_bwd(dh1, ffn00, W["ffn_norm"][0, 0], W["ffn_w_in"][0, 0], W["ffn_w_out"][0, 0], "ffn00")

    G["mix_norm"] = jnp.stack([dmix_a, dmix_b])
    G["ffn_norm"] = jnp.stack([jnp.stack(r) for r in dn])
    G["ffn_w_in"] = jnp.stack([jnp.stack(r) for r in dwi])
    G["ffn_w_out"] = jnp.stack([jnp.stack(r) for r in dwo])
    return jnp.sum(sq), dh0.reshape(B, S, D), G


BIG = {"ffn_w_in": 3, "ffn_w_out": 2, "a_w_qkv": 2, "a_w_o": 1, "kv_w": 1, "b_w_q": 1, "b_w_o": 1}
SMALL = ["mix_norm", "kv_norm", "a_q_norm", "a_k_norm", "kv_b_f", "kv_k_norm", "b_q_norm"]


def _pack(blocks, lead):
    flat = [b.reshape(b.shape[:lead] + (-1,)) for b in blocks]
    flat = jnp.concatenate(flat, axis=lead)
    n = flat.shape[-1]
    unit = PACK_ROWS * LANES
    padded = -(-n // unit) * unit if n > unit else -(-n // (16 * LANES)) * 16 * LANES
    flat = jnp.pad(flat, [(0, 0)] * lead + [(0, padded - n)])
    return flat.reshape(flat.shape[:lead] + (padded // LANES, LANES))


def _unpack(packed, shapes, lead):
    flat = packed.reshape(packed.shape[:lead] + (-1,))
    out, off = [], 0
    for shp in shapes:
        n = int(np.prod(shp))
        out.append(flat[..., off:off + n].reshape(packed.shape[:lead] + tuple(shp)))
        off += n
    return out


def _to_full(blocks, axis):
    moved = jnp.moveaxis(blocks, 0, axis)
    shp = moved.shape
    return moved.reshape(shp[:axis] + (shp[axis] * shp[axis + 1],) + shp[axis + 2:])


def _to_blocks(full, axis):
    shp = full.shape
    cut = full.reshape(shp[:axis] + (N_DEV, shp[axis] // N_DEV) + shp[axis + 1:])
    return jnp.moveaxis(cut, axis, 0)


def kernel(x, positions, ffn_norm, ffn_w_in, ffn_w_out, mix_norm, a_w_qkv, a_q_norm, a_k_norm, a_w_o, kv_norm, kv_w, kv_b_f, kv_k_norm, b_w_q, b_q_norm, b_w_o, loss_target, m_ffn_norm, m_ffn_w_in, m_ffn_w_out, m_mix_norm, m_a_w_qkv, m_a_q_norm, m_a_k_norm, m_a_w_o, m_kv_norm, m_kv_w, m_kv_b_f, m_kv_k_norm, m_b_w_q, m_b_q_norm, m_b_w_o, v_ffn_norm, v_ffn_w_in, v_ffn_w_out, v_mix_norm, v_a_w_qkv, v_a_q_norm, v_a_k_norm, v_a_w_o, v_kv_norm, v_kv_w, v_kv_b_f, v_kv_k_norm, v_b_w_q, v_b_q_norm, v_b_w_o):
    names = ["ffn_norm", "ffn_w_in", "ffn_w_out", "mix_norm", "a_w_qkv", "a_q_norm", "a_k_norm", "a_w_o", "kv_norm",
             "kv_w", "kv_b_f", "kv_k_norm", "b_w_q", "b_q_norm", "b_w_o"]
    w = dict(zip(names, (ffn_norm, ffn_w_in, ffn_w_out, mix_norm, a_w_qkv, a_q_norm, a_k_norm, a_w_o, kv_norm, kv_w,
                         kv_b_f, kv_k_norm, b_w_q, b_q_norm, b_w_o)))
    m = dict(zip(names, (m_ffn_norm, m_ffn_w_in, m_ffn_w_out, m_mix_norm, m_a_w_qkv, m_a_q_norm, m_a_k_norm, m_a_w_o,
                         m_kv_norm, m_kv_w, m_kv_b_f, m_kv_k_norm, m_b_w_q, m_b_q_norm, m_b_w_o)))
    v = dict(zip(names, (v_ffn_norm, v_ffn_w_in, v_ffn_w_out, v_mix_norm, v_a_w_qkv, v_a_q_norm, v_a_k_norm, v_a_w_o,
                         v_kv_norm, v_kv_w, v_kv_b_f, v_kv_k_norm, v_b_w_q, v_b_q_norm, v_b_w_o)))
    B, S, D = x.shape
    H = D // HEAD_DIM
    me = 4 * lax.axis_index("x") + 2 * lax.axis_index("y") + lax.axis_index("c")
    big = list(BIG)
    big_shapes = [w[n].shape for n in big]

    gathered = _all_gather(_pack([w[n].astype(BF16) for n in big], 0), "gather_weights")
    full = {n: _to_full(blk, BIG[n]) for n, blk in zip(big, _unpack(gathered, big_shapes, 1))}
    norm_parts = _all_gather(_pack([ffn_norm], 0), "gather_ffn_norm")
    W = {n: w[n] for n in SMALL}
    W["ffn_norm"] = _to_full(_unpack(norm_parts, [ffn_norm.shape], 1)[0], 2)
    W["ffn_w_in"], W["ffn_w_out"] = full["ffn_w_in"], full["ffn_w_out"]
    W["a_w_qkv"], W["a_w_o"] = full["a_w_qkv"][0], full["a_w_o"][0]
    W["b_w_q"], W["b_w_o"] = full["b_w_q"][0], full["b_w_o"][0]
    W["kv_w_kv"] = full["kv_w"][:, :2 * D]
    W["kv_w_f"] = jnp.pad(full["kv_w"][:, 2 * D:], ((0, 0), (0, LANES - H)))

    sq, grad_x, G = _local_step(x, positions, loss_target, W)
    loss = lax.psum(sq * (0.5 / D), ("x", "y", "c"))

    parts = _all_to_all(_pack([_to_blocks(G[n], BIG[n]).astype(BF16) for n in big], 1), "exchange_grads")
    outs = _adamw(parts, *(_pack([t[n] for n in big], 0) for t in (w, m, v)), "adamw_matrices")
    new = {}
    for kind, packed in zip(("grad", "delta", "new_m", "new_v"), outs):
        for n, arr in zip(big, _unpack(packed, big_shapes, 0)):
            new[kind, n] = arr

    small_all = ["ffn_norm"] + SMALL
    vec_parts = _all_gather(_pack([G[n] for n in small_all], 0), "gather_vector_grads")
    vec_parts = _unpack(vec_parts, [G[n].shape for n in small_all], 1)
    norm_g = lax.dynamic_slice_in_dim(vec_parts[0], me * ffn_norm.shape[2], ffn_norm.shape[2], axis=3)
    rep_g = _pack(vec_parts[1:], 1)
    for ns, parts_s in ((["ffn_norm"], _pack([norm_g], 1)), (SMALL, rep_g)):
        outs = _adamw(parts_s, *(_pack([t[n] for n in ns], 0) for t in (w, m, v)), "adamw_" + ns[0])
        for kind, packed in zip(("grad", "delta", "new_m", "new_v"), outs):
            for n, arr in zip(ns, _unpack(packed, [w[n].shape for n in ns], 0)):
                new[kind, n] = arr

    result = [loss, grad_x]
    for kind in ("grad", "delta", "new_m", "new_v"):
        result += [new[kind, n] for n in names]
    return tuple(result)
```

```python
import functools

import jax
import jax.numpy as jnp
import numpy as np
from jax import lax
from jax.experimental import pallas as pl
from jax.experimental.pallas import tpu as pltpu

F32 = jnp.float32
BF16 = jnp.bfloat16
HIGHEST = lax.Precision.HIGHEST

HEAD_DIM = 64
ROT_DIM = HEAD_DIM // 4
ROPE_THETA = 500000.0
DILATED_GROUPS = ((128, 1), (512, 4), (2048, 16))
BAND_BLOCK = 128
EPS = 1e-6
ADAM_LR, ADAM_B1, ADAM_B2, ADAM_EPS, ADAM_WD, ADAM_STEP = 0.001, 0.9, 0.999, 1e-08, 0.01, 10

LANES = 128
N_DEV = 8
VMEM_LIMIT = 56 * 1024 * 1024
NEG = -1e30
SCALE = HEAD_DIM ** -0.5
PACK_ROWS = 1024
MESH_ID = pl.DeviceIdType.MESH


def _params(*sem):
    return pltpu.CompilerParams(dimension_semantics=sem, vmem_limit_bytes=VMEM_LIMIT)


def _divs(n, cap, mult=LANES):
    d = [t for t in range(mult, min(n, cap) + 1, mult) if n % t == 0]
    return d or [n]


def _pick(n, cap, mult=LANES):
    return _divs(n, cap, mult)[-1]


def _mm_tiles(M, N, K, budget=1024 * 1024):
    best = None
    for tm in _divs(M, 1408):
        for tn in _divs(N, 1408):
            if tm * tn <= budget and (best is None or tm * tn > best[0] * best[1]):
                best = (tm, tn)
    if best is None:
        best = (_divs(M, 1408)[0], _divs(N, 1408)[0])
    tm, tn = best
    tks = [t for t in _divs(K, 2048) if (tm + tn) * t <= 3 * 1024 * 1024]
    tk = tks[-1] if tks else _divs(K, 2048)[0]
    return tm, tn, tk


def _dot(a, b, dims, precision=None):
    return lax.dot_general(a, b, (dims, ((), ())), precision=precision, preferred_element_type=F32)


NN = ((1,), (0,))
NT = ((1,), (1,))
TN = ((0,), (0,))


def _dot_split(x, mat):
    hi = x.astype(BF16)
    lo = (x - hi.astype(F32)).astype(BF16)
    m = mat.astype(BF16)
    return _dot(hi, m, NN) + _dot(lo, m, NN)


def _mm(a, b, mode, *, M, N, K, a_off=(0, 0), b_off=(0, 0), res=None, scale=1.0, out_dtype=F32, name):
    tm, tn, tk = _mm_tiles(M, N, K)
    nk = K // tk
    if mode == "tn":
        a_blk, b_blk, dims = (tk, tm), (tk, tn), TN
    elif mode == "nt":
        a_blk, b_blk, dims = (tm, tk), (tn, tk), NT
    else:
        a_blk, b_blk, dims = (tm, tk), (tk, tn), NN
    for off, blk in ((a_off, a_blk), (b_off, b_blk)):
        assert off[0] % blk[0] == 0 and off[1] % blk[1] == 0, (name, off, blk)
    ao = (a_off[0] // a_blk[0], a_off[1] // a_blk[1])
    bo = (b_off[0] // b_blk[0], b_off[1] // b_blk[1])
    if mode == "tn":
        a_map = lambda i, j, k: (k + ao[0], i + ao[1])
        b_map = lambda i, j, k: (k + bo[0], j + bo[1])
    elif mode == "nt":
        a_map = lambda i, j, k: (i + ao[0], k + ao[1])
        b_map = lambda i, j, k: (j + bo[0], k + bo[1])
    else:
        a_map = lambda i, j, k: (i + ao[0], k + ao[1])
        b_map = lambda i, j, k: (k + bo[0], j + bo[1])
    has_res = res is not None

    def body(*refs):
        if has_res:
            a_ref, b_ref, r_ref, o_ref, acc = refs
        else:
            a_ref, b_ref, o_ref, acc = refs
        k = pl.program_id(2)

        @pl.when(k == 0)
        def _():
            acc[...] = jnp.zeros_like(acc)

        acc[...] += _dot(a_ref[...].astype(BF16), b_ref[...].astype(BF16), dims)

        @pl.when(k == nk - 1)
        def _():
            out = acc[...] * scale
            if has_res:
                out = r_ref[...] + out
            o_ref[...] = out.astype(out_dtype)

    in_specs = [pl.BlockSpec(a_blk, a_map), pl.BlockSpec(b_blk, b_map)]
    args = [a, b]
    if has_res:
        in_specs.append(pl.BlockSpec((tm, tn), lambda i, j, k: (i, j)))
        args.append(res)
    return pl.pallas_call(
        body, name=name, grid=(M // tm, N // tn, nk), in_specs=in_specs,
        out_specs=pl.BlockSpec((tm, tn), lambda i, j, k: (i, j)),
        out_shape=jax.ShapeDtypeStruct((M, N), out_dtype),
        scratch_shapes=[pltpu.VMEM((tm, tn), F32)],
        compiler_params=_params("parallel", "parallel", "arbitrary"),
    )(*args)


def _rmsnorm_fwd(x, g, name):
    T, D = x.shape
    tm = _pick(T, 256, 8)

    def body(x_ref, g_ref, y_ref):
        xv = x_ref[...]
        r = lax.rsqrt(jnp.mean(xv * xv, axis=-1, keepdims=True) + EPS)
        y_ref[...] = (xv * r * g_ref[...]).astype(BF16)

    return pl.pallas_call(
        body, name=name, grid=(T // tm,),
        in_specs=[pl.BlockSpec((tm, D), lambda i: (i, 0)), pl.BlockSpec((1, D), lambda i: (0, 0))],
        out_specs=pl.BlockSpec((tm, D), lambda i: (i, 0)),
        out_shape=jax.ShapeDtypeStruct((T, D), BF16), compiler_params=_params("parallel"),
    )(x, g.reshape(1, D))


def _rmsnorm_bwd(x, g, dy, dres, name):
    T, D = x.shape
    tm = _pick(T, 256, 8)

    def body(x_ref, g_ref, dy_ref, dres_ref, dx_ref, dg_ref):
        xv = x_ref[...]
        r = lax.rsqrt(jnp.mean(xv * xv, axis=-1, keepdims=True) + EPS)
        xn = xv * r
        d = dy_ref[...]
        dgy = d * g_ref[...]
        dx_ref[...] = dres_ref[...] + r * (dgy - xn * jnp.mean(dgy * xn, axis=-1, keepdims=True))

        @pl.when(pl.program_id(0) == 0)
        def _():
            dg_ref[...] = jnp.zeros_like(dg_ref)

        dg_ref[...] += jnp.sum(d * xn, axis=0, keepdims=True)

    row = pl.BlockSpec((tm, D), lambda i: (i, 0))
    vec = pl.BlockSpec((1, D), lambda i: (0, 0))
    dx, dg = pl.pallas_call(
        body, name=name, grid=(T // tm,), in_specs=[row, vec, row, row], out_specs=[row, vec],
        out_shape=[jax.ShapeDtypeStruct((T, D), F32), jax.ShapeDtypeStruct((1, D), F32)],
        compiler_params=_params("arbitrary"),
    )(x, g.reshape(1, D), dy, dres)
    return dx, dg.reshape(D)


def _sigmoid(x):
    return 1.0 / (1.0 + jnp.exp(-x))


def _mm_swiglu_fwd(hn, w_in, name):
    T, D = hn.shape
    F = w_in.shape[1] // 2
    tm, tn, _ = _mm_tiles(T, F, D, budget=768 * 1024)
    tk = [t for t in _divs(D, 2048) if (tm + 2 * tn) * t <= 3 * 1024 * 1024][-1]
    nk, nf = D // tk, F // tn

    def body(a_ref, bg_ref, bu_ref, g_ref, u_ref, act_ref, acc_g, acc_u):
        k = pl.program_id(2)

        @pl.when(k == 0)
        def _():
            acc_g[...] = jnp.zeros_like(acc_g)
            acc_u[...] = jnp.zeros_like(acc_u)

        a = a_ref[...].astype(BF16)
        acc_g[...] += _dot(a, bg_ref[...].astype(BF16), NN)
        acc_u[...] += _dot(a, bu_ref[...].astype(BF16), NN)

        @pl.when(k == nk - 1)
        def _():
            gv, uv = acc_g[...], acc_u[...]
            g_ref[...] = gv
            u_ref[...] = uv
            act_ref[...] = (gv * _sigmoid(gv) * uv).astype(BF16)

    tile = pl.BlockSpec((tm, tn), lambda i, j, k: (i, j))
    f32 = jax.ShapeDtypeStruct((T, F), F32)
    return pl.pallas_call(
        body, name=name, grid=(T // tm, nf, nk),
        in_specs=[pl.BlockSpec((tm, tk), lambda i, j, k: (i, k)), pl.BlockSpec((tk, tn), lambda i, j, k: (k, j)),
                  pl.BlockSpec((tk, tn), lambda i, j, k: (k, j + nf))],
        out_specs=[tile, tile, tile], out_shape=[f32, f32, jax.ShapeDtypeStruct((T, F), BF16)],
        scratch_shapes=[pltpu.VMEM((tm, tn), F32), pltpu.VMEM((tm, tn), F32)],
        compiler_params=_params("parallel", "parallel", "arbitrary"),
    )(hn, w_in, w_in)


def _mm_swiglu_bwd(dout, w_out, gate, up, name):
    T, D = dout.shape
    F = w_out.shape[0]
    tm, tn, tk = _mm_tiles(T, F, D, budget=768 * 1024)
    nk = D // tk

    def body(a_ref, b_ref, g_ref, u_ref, dg_ref, du_ref, acc):
        k = pl.program_id(2)

        @pl.when(k == 0)
        def _():
            acc[...] = jnp.zeros_like(acc)

        acc[...] += _dot(a_ref[...].astype(BF16), b_ref[...].astype(BF16), NT)

        @pl.when(k == nk - 1)
        def _():
            d = acc[...] * 0.5
            gv, uv = g_ref[...], u_ref[...]
            s = _sigmoid(gv)
            dg_ref[...] = (d * uv * (s + gv * s * (1.0 - s))).astype(BF16)
            du_ref[...] = (d * gv * s).astype(BF16)

    tile = pl.BlockSpec((tm, tn), lambda i, j, k: (i, j))
    out = jax.ShapeDtypeStruct((T, F), BF16)
    return pl.pallas_call(
        body, name=name, grid=(T // tm, F // tn, nk),
        in_specs=[pl.BlockSpec((tm, tk), lambda i, j, k: (i, k)), pl.BlockSpec((tn, tk), lambda i, j, k: (j, k)), tile, tile],
        out_specs=[tile, tile], out_shape=[out, out], scratch_shapes=[pltpu.VMEM((tm, tn), F32)],
        compiler_params=_params("parallel", "parallel", "arbitrary"),
    )(dout, w_out, gate, up)


def _iota2(shape, dim):
    return lax.broadcasted_iota(jnp.int32, shape, dim)


def _head_ones():
    r, c = _iota2((LANES, LANES), 0), _iota2((LANES, LANES), 1)
    return ((r >> 6) == (c >> 6)).astype(F32)


def _rot_matrix(transpose):
    r, c = _iota2((LANES, LANES), 0), _iota2((LANES, LANES), 1)
    if transpose:
        r, c = c, r
    half = ROT_DIM // 2
    cm = c & (HEAD_DIM - 1)
    neg = (r == c + half) & (cm < half)
    pos = (r == c - half) & (cm >= half) & (cm < ROT_DIM)
    return pos.astype(F32) - neg.astype(F32)


def _rope_tables(positions, name):
    T = positions.size
    tm = _pick(T, 1024, 8)
    inv = ROPE_THETA ** (-jnp.arange(0, ROT_DIM, 2, dtype=F32) / ROT_DIM)
    lane = jnp.concatenate([inv, inv, jnp.zeros((HEAD_DIM - ROT_DIM,), F32)])
    lane = jnp.tile(lane, 2).reshape(1, LANES)

    def body(p_ref, f_ref, c_ref, s_ref):
        ang = p_ref[...].astype(F32) * f_ref[...]
        c_ref[...] = jnp.cos(ang)
        s_ref[...] = jnp.sin(ang)

    out = jax.ShapeDtypeStruct((T, LANES), F32)
    blk = pl.BlockSpec((tm, LANES), lambda i: (i, 0))
    return pl.pallas_call(
        body, name=name, grid=(T // tm,),
        in_specs=[pl.BlockSpec((tm, 1), lambda i: (i, 0)), pl.BlockSpec((1, LANES), lambda i: (0, 0))],
        out_specs=[blk, blk], out_shape=[out, out], compiler_params=_params("parallel"),
    )(positions.reshape(T, 1), lane)


def _qknorm_fwd(src, col_off, gain, cs, D, out_dtype, name, out_scale=1.0):
    T = src.shape[0]
    HP = D // LANES
    tm = _pick(T, 1024, 8)
    co = col_off // LANES
    rope = cs is not None

    def body(*refs):
        if rope:
            x_ref, g_ref, c_ref, s_ref, o_ref = refs
        else:
            x_ref, g_ref, o_ref = refs
        xv = x_ref[...]
        ms = _dot_split(xv * xv, _head_ones()) * (1.0 / HEAD_DIM)
        y = xv * lax.rsqrt(ms + EPS) * g_ref[...]
        if rope:
            y = y * c_ref[...] + _dot_split(y, _rot_matrix(False)) * s_ref[...]
        o_ref[...] = (y * out_scale).astype(out_dtype)

    in_specs = [pl.BlockSpec((tm, LANES), lambda i, h: (i, co + h)), pl.BlockSpec((1, LANES), lambda i, h: (0, 0))]
    args = [src, jnp.tile(gain.reshape(1, HEAD_DIM), (1, 2))]
    if rope:
        in_specs += [pl.BlockSpec((tm, LANES), lambda i, h: (i, 0))] * 2
        args += list(cs)
    return pl.pallas_call(
        body, name=name, grid=(T // tm, HP), in_specs=in_specs,
        out_specs=pl.BlockSpec((tm, LANES), lambda i, h: (i, h)),
        out_shape=jax.ShapeDtypeStruct((T, D), out_dtype), compiler_params=_params("parallel", "parallel"),
    )(*args)


def _qknorm_bwd(src, col_off, gain, cs, dout, D, name, out_scale=1.0):
    T = src.shape[0]
    HP = D // LANES
    tm = _pick(T, 1024, 8)
    co = col_off // LANES
    rope = cs is not None

    def body(*refs):
        if rope:
            x_ref, g_ref, d_ref, c_ref, s_ref, dx_ref, dg_ref = refs
        else:
            x_ref, g_ref, d_ref, dx_ref, dg_ref = refs
        xv = x_ref[...]
        ones = _head_ones()
        ms = _dot_split(xv * xv, ones) * (1.0 / HEAD_DIM)
        r = lax.rsqrt(ms + EPS)
        xn = xv * r
        d = d_ref[...] * out_scale
        if rope:
            d = d * c_ref[...] + _dot_split(d * s_ref[...], _rot_matrix(True))
        dgy = d * g_ref[...]
        mean = _dot_split(dgy * xn, ones) * (1.0 / HEAD_DIM)
        dx_ref[...] = (r * (dgy - xn * mean)).astype(BF16)

        @pl.when(pl.program_id(1) == 0)
        def _():
            dg_ref[...] = jnp.zeros_like(dg_ref)

        dg_ref[...] += jnp.sum(d * xn, axis=0, keepdims=True)[None]

    in_specs = [pl.BlockSpec((tm, LANES), lambda h, i: (i, co + h)), pl.BlockSpec((1, LANES), lambda h, i: (0, 0)),
                pl.BlockSpec((tm, LANES), lambda h, i: (i, h))]
    args = [src, jnp.tile(gain.reshape(1, HEAD_DIM), (1, 2)), dout]
    if rope:
        in_specs += [pl.BlockSpec((tm, LANES), lambda h, i: (i, 0))] * 2
        args += list(cs)
    dx, dg = pl.pallas_call(
        body, name=name, grid=(HP, T // tm), in_specs=in_specs,
        out_specs=[pl.BlockSpec((tm, LANES), lambda h, i: (i, h)), pl.BlockSpec((1, 1, LANES), lambda h, i: (h, 0, 0))],
        out_shape=[jax.ShapeDtypeStruct((T, D), BF16), jax.ShapeDtypeStruct((HP, 1, LANES), F32)],
        compiler_params=_params("parallel", "arbitrary"),
    )(*args)
    return dx, dg.reshape(2 * HP, HEAD_DIM).sum(axis=0)


def _head_rowsum(a, b, name):
    T, D = a.shape
    tm = _pick(T, 1024, 8)

    def body(a_ref, b_ref, o_ref):
        o_ref[...] = _dot_split(a_ref[...] * b_ref[...], _head_ones())

    blk = pl.BlockSpec((tm, LANES), lambda i, h: (i, h))
    return pl.pallas_call(
        body, name=name, grid=(T // tm, D // LANES), in_specs=[blk, blk], out_specs=blk,
        out_shape=jax.ShapeDtypeStruct((T, D), F32), compiler_params=_params("parallel", "parallel"),
    )(a, b)


def _lane_masks():
    lane = _iota2((1, LANES), 1)
    return (lane < HEAD_DIM, lane >= HEAD_DIM)


def _band_geometry(g, S):
    window, dil = DILATED_GROUPS[g]
    chunk = BAND_BLOCK * dil
    assert window // dil == BAND_BLOCK and S % chunk == 0
    per_step = max(1, 512 // chunk)
    while (S // chunk) % per_step:
        per_step //= 2
    return dil, chunk, per_step


def _band_rows(base, r, dil):
    return pl.ds(base + r, BAND_BLOCK, stride=dil) if dil > 1 else pl.ds(base, BAND_BLOCK)


def _band_fwd(qh, kh, qkv, g, B, S, D, name):
    dil, CH, NB = _band_geometry(g, S)
    HP = D // LANES
    vo = (3 * g + 2) * HP
    nsp = S // (CH * NB)

    def body(q_ref, k_ref, kp_ref, v_ref, vp_ref, o_ref, lse_ref):
        n, r = pl.program_id(1), pl.program_id(3)
        qi, kj = _iota2((BAND_BLOCK, BAND_BLOCK), 0), _iota2((BAND_BLOCK, BAND_BLOCK), 1)
        cur_ok = kj <= qi
        band = kj >= qi
        for j in range(NB):
            rows = _band_rows(j * CH, r, dil)
            q, kc, vc = (t[0, rows, :].astype(BF16) for t in (q_ref, k_ref, v_ref))
            if j == 0:
                kp, vp = (t[0, _band_rows(0, r, dil), :].astype(BF16) for t in (kp_ref, vp_ref))
                prev_ok = band & (n > 0)
            else:
                kp, vp = (t[0, _band_rows((j - 1) * CH, r, dil), :].astype(BF16) for t in (k_ref, v_ref))
                prev_ok = band
            o_acc = jnp.zeros((BAND_BLOCK, LANES), F32)
            l_acc = jnp.zeros((BAND_BLOCK, LANES), F32)
            for hm in _lane_masks():
                qm = jnp.where(hm, q, jnp.zeros_like(q))
                sc = jnp.where(cur_ok, _dot(qm, kc, NT), NEG)
                sp = jnp.where(prev_ok, _dot(qm, kp, NT), NEG)
                m = jnp.maximum(jnp.max(sc, axis=-1, keepdims=True), jnp.max(sp, axis=-1, keepdims=True))
                ec, ep = jnp.exp(sc - m), jnp.exp(sp - m)
                l = jnp.sum(ec, axis=-1, keepdims=True) + jnp.sum(ep, axis=-1, keepdims=True)
                inv = 1.0 / l
                oh = _dot((ec * inv).astype(BF16), vc, NN) + _dot((ep * inv).astype(BF16), vp, NN)
                o_acc = jnp.where(hm, oh, o_acc)
                l_acc = jnp.where(hm, m + jnp.log(l), l_acc)
            o_ref[0, rows, :] = o_acc
            lse_ref[0, rows, :] = l_acc

    span, chunk = (1, CH * NB, LANES), (1, CH, LANES)
    cur = lambda b, n, h, r: (b, n, h)
    prev = lambda b, n, h, r: (b, jnp.maximum(n * NB - 1, 0), h)
    vcur = lambda b, n, h, r: (b, n, vo + h)
    vprev = lambda b, n, h, r: (b, jnp.maximum(n * NB - 1, 0), vo + h)
    out = jax.ShapeDtypeStruct((B, S, D), F32)
    q3, k3, v3 = qh.reshape(B, S, D), kh.reshape(B, S, D), qkv.reshape(B, S, 9 * D)
    o, lse = pl.pallas_call(
        body, name=name, grid=(B, nsp, HP, dil),
        in_specs=[pl.BlockSpec(span, cur), pl.BlockSpec(span, cur), pl.BlockSpec(chunk, prev),
                  pl.BlockSpec(span, vcur), pl.BlockSpec(chunk, vprev)],
        out_specs=[pl.BlockSpec(span, cur), pl.BlockSpec(span, cur)], out_shape=[out, out],
        compiler_params=_params("parallel", "parallel", "parallel", "arbitrary"),
    )(q3, k3, k3, v3, v3)
    return o.reshape(B * S, D), lse.reshape(B * S, D)


def _head_col(x, h):
    lane = _iota2((1, LANES), 1)
    return jnp.sum(jnp.where(lane == h * HEAD_DIM, x, 0.0), axis=-1, keepdims=True)


def _band_bwd(qh, kh, qkv, do, lse, delta, g, B, S, D, name):
    dil, CH, NB = _band_geometry(g, S)
    HP = D // LANES
    vo = (3 * g + 2) * HP
    nsp, nch = S // (CH * NB), S // CH

    def body(q_ref, qn_ref, do_ref, don_ref, l_ref, ln_ref, d_ref, dn_ref, k_ref, kp_ref, v_ref, vp_ref,
             dq_ref, dk_ref, dv_ref):
        n, r = pl.program_id(1), pl.program_id(3)
        qi, kj = _iota2((BAND_BLOCK, BAND_BLOCK), 0), _iota2((BAND_BLOCK, BAND_BLOCK), 1)
        cur_ok = kj <= qi
        band = kj >= qi
        for j in range(NB):
            rows = _band_rows(j * CH, r, dil)
            q, dout, kc, vc = (t[0, rows, :].astype(BF16) for t in (q_ref, do_ref, k_ref, v_ref))
            lse_t, del_t = l_ref[0, rows, :], d_ref[0, rows, :]
            if j == 0:
                kp, vp = (t[0, _band_rows(0, r, dil), :].astype(BF16) for t in (kp_ref, vp_ref))
                prev_ok = band & (n > 0)
            else:
                kp, vp = (t[0, _band_rows((j - 1) * CH, r, dil), :].astype(BF16) for t in (k_ref, v_ref))
                prev_ok = band
            if j == NB - 1:
                nrows = _band_rows(0, r, dil)
                qn, doutn = (t[0, nrows, :].astype(BF16) for t in (qn_ref, don_ref))
                lse_nt, del_nt = ln_ref[0, nrows, :], dn_ref[0, nrows, :]
                next_ok = band & (n < nsp - 1)
            else:
                nrows = _band_rows((j + 1) * CH, r, dil)
                qn, doutn = (t[0, nrows, :].astype(BF16) for t in (q_ref, do_ref))
                lse_nt, del_nt = l_ref[0, nrows, :], d_ref[0, nrows, :]
                next_ok = band
            dq = jnp.zeros((BAND_BLOCK, LANES), F32)
            dk = jnp.zeros((BAND_BLOCK, LANES), F32)
            dv = jnp.zeros((BAND_BLOCK, LANES), F32)
            zero = jnp.zeros_like(q)
            for h, hm in enumerate(_lane_masks()):
                qm, qnm = jnp.where(hm, q, zero), jnp.where(hm, qn, zero)
                dom, donm = jnp.where(hm, dout, zero), jnp.where(hm, doutn, zero)
                kcm, kpm = jnp.where(hm, kc, zero), jnp.where(hm, kp, zero)
                lse_q, lse_n = _head_col(lse_t, h), _head_col(lse_nt, h)
                del_q, del_n = _head_col(del_t, h), _head_col(del_nt, h)
                pa = jnp.exp(jnp.where(cur_ok, _dot(qm, kc, NT) - lse_q, NEG))
                pb = jnp.exp(jnp.where(prev_ok, _dot(qm, kp, NT) - lse_q, NEG))
                pc = jnp.exp(jnp.where(next_ok, _dot(qnm, kc, NT) - lse_n, NEG))
                dsa = (pa * (_dot(dom, vc, NT) - del_q)).astype(BF16)
                dsb = (pb * (_dot(dom, vp, NT) - del_q)).astype(BF16)
                dsc = (pc * (_dot(donm, vc, NT) - del_n)).astype(BF16)
                dq += _dot(dsa, kcm, NN) + _dot(dsb, kpm, NN)
                dk += _dot(dsa, qm, TN) + _dot(dsc, qnm, TN)
                dv += _dot(pa.astype(BF16), dom, TN) + _dot(pc.astype(BF16), donm, TN)
            dq_ref[0, rows, :] = dq
            dk_ref[0, rows, :] = dk
            dv_ref[0, rows, :] = dv

    span, chunk = (1, CH * NB, LANES), (1, CH, LANES)
    cur = lambda b, n, h, r: (b, n, h)
    prev = lambda b, n, h, r: (b, jnp.maximum(n * NB - 1, 0), h)
    nxt = lambda b, n, h, r: (b, jnp.minimum((n + 1) * NB, nch - 1), h)
    vcur = lambda b, n, h, r: (b, n, vo + h)
    vprev = lambda b, n, h, r: (b, jnp.maximum(n * NB - 1, 0), vo + h)
    q3, k3, do3, l3, d3 = (t.reshape(B, S, D) for t in (qh, kh, do, lse, delta))
    v3 = qkv.reshape(B, S, 9 * D)
    sp, ch = (lambda m: pl.BlockSpec(span, m)), (lambda m: pl.BlockSpec(chunk, m))
    out = jax.ShapeDtypeStruct((B, S, D), F32)
    dq, dk, dv = pl.pallas_call(
        body, name=name, grid=(B, nsp, HP, dil),
        in_specs=[sp(cur), ch(nxt), sp(cur), ch(nxt), sp(cur), ch(nxt), sp(cur), ch(nxt),
                  sp(cur), ch(prev), sp(vcur), ch(vprev)],
        out_specs=[sp(cur)] * 3, out_shape=[out] * 3,
        compiler_params=_params("parallel", "parallel", "parallel", "arbitrary"),
    )(q3, q3, do3, do3, l3, l3, d3, d3, k3, k3, v3, v3)
    return dq.reshape(B * S, D), dk.reshape(B * S, D), dv.reshape(B * S, D)


def _mix(os_, lses, name):
    T, D = os_[0].shape
    tm = _pick(T, 256, 8)

    def body(o0, o1, o2, l0, l1, l2, mix_ref, lse_ref):
        a, b, c = l0[...], l1[...], l2[...]
        m = jnp.maximum(jnp.maximum(a, b), c)
        ea, eb, ec = jnp.exp(a - m), jnp.exp(b - m), jnp.exp(c - m)
        s = ea + eb + ec
        mix_ref[...] = (ea / s) * o0[...] + (eb / s) * o1[...] + (ec / s) * o2[...]
        lse_ref[...] = m + jnp.log(s)

    blk = pl.BlockSpec((tm, D), lambda i: (i, 0))
    out = jax.ShapeDtypeStruct((T, D), F32)
    return pl.pallas_call(
        body, name=name, grid=(T // tm,), in_specs=[blk] * 6, out_specs=[blk, blk], out_shape=[out, out],
        compiler_params=_params("parallel"),
    )(*os_, *lses)


def _gate_fwd(fpre, b_f, B, S, D, name):
    nb = S // LANES

    def body(f_ref, b_ref, cum_ref, rep_ref, carry):
        @pl.when(pl.program_id(1) == 0)
        def _():
            carry[...] = jnp.zeros_like(carry)

        z = f_ref[...] + b_ref[...]
        lf = jnp.minimum(z, 0.0) - jnp.log(1.0 + jnp.exp(-jnp.abs(z)))
        tri = (_iota2((LANES, LANES), 0) >= _iota2((LANES, LANES), 1)).astype(F32)
        c = _dot(tri, lf, NN, HIGHEST) + carry[...]
        cum_ref[...] = c
        carry[...] += jnp.sum(lf, axis=0, keepdims=True)
        spread = ((_iota2((LANES, D), 1) >> 6) == _iota2((LANES, D), 0)).astype(F32)
        rep_ref[...] = _dot(c, spread, NN, HIGHEST)

    return pl.pallas_call(
        body, name=name, grid=(B, nb),
        in_specs=[pl.BlockSpec((LANES, LANES), lambda b, j: (b * nb + j, 0)), pl.BlockSpec((1, LANES), lambda b, j: (0, 0))],
        out_specs=[pl.BlockSpec((LANES, LANES), lambda b, j: (b * nb + j, 0)), pl.BlockSpec((LANES, D), lambda b, j: (b * nb + j, 0))],
        out_shape=[jax.ShapeDtypeStruct((B * S, LANES), F32), jax.ShapeDtypeStruct((B * S, D), F32)],
        scratch_shapes=[pltpu.VMEM((1, LANES), F32)], compiler_params=_params("arbitrary", "arbitrary"),
    )(fpre, b_f)


def _gate_bwd(dcum_k, dcum_q, fpre, b_f, B, S, name):
    nb = S // LANES

    def body(d_ref, dq_ref, f_ref, b_ref, o_ref, db_ref, carry):
        first = (pl.program_id(0) == 0) & (pl.program_id(1) == 0)

        @pl.when(pl.program_id(1) == 0)
        def _():
            carry[...] = jnp.zeros_like(carry)

        @pl.when(first)
        def _():
            db_ref[...] = jnp.zeros_like(db_ref)

        d = d_ref[...] + dq_ref[...]
        tri = (_iota2((LANES, LANES), 0) <= _iota2((LANES, LANES), 1)).astype(F32)
        dlf = _dot(tri, d, NN, HIGHEST) + carry[...]
        carry[...] += jnp.sum(d, axis=0, keepdims=True)
        z = f_ref[...] + b_ref[...]
        dz = dlf * (1.0 / (1.0 + jnp.exp(z)))
        o_ref[...] = dz.astype(BF16)
        db_ref[...] += jnp.sum(dz, axis=0, keepdims=True)

    rev = lambda b, j: (b * nb + nb - 1 - j, 0)
    blk = pl.BlockSpec((LANES, LANES), rev)
    vec = pl.BlockSpec((1, LANES), lambda b, j: (0, 0))
    return pl.pallas_call(
        body, name=name, grid=(B, nb), in_specs=[blk, blk, blk, vec], out_specs=[blk, vec],
        out_shape=[jax.ShapeDtypeStruct((B * S, LANES), BF16), jax.ShapeDtypeStruct((1, LANES), F32)],
        scratch_shapes=[pltpu.VMEM((1, LANES), F32)], compiler_params=_params("arbitrary", "arbitrary"),
    )(dcum_k, dcum_q, fpre, b_f)


def _fox_scores(q, k, ck_ref, h, hm, diagonal):
    TB = q.shape[0]
    qm = jnp.where(hm, q, jnp.zeros_like(q))
    t = _dot(qm, k, NT) - ck_ref[0, h:h + 1, :]
    if diagonal:
        t = jnp.where(_iota2((TB, TB), 1) <= _iota2((TB, TB), 0), t, NEG)
    return qm, t


def _fox_fwd(qh, kh, kv, cq, ck, B, S, D, name):
    HP = D // LANES
    TB = _pick(S, 512)
    nb = S // TB

    def body(q_ref, k_ref, v_ref, cq_ref, ck_ref, o_ref, lse_ref, m_sc, l_sc, acc):
        qb, kb = pl.program_id(1), pl.program_id(3)

        @pl.when(kb == 0)
        def _():
            m_sc[...] = jnp.full_like(m_sc, NEG)
            l_sc[...] = jnp.zeros_like(l_sc)
            acc[...] = jnp.zeros_like(acc)

        def step(diagonal):
            q, k, v = q_ref[0], k_ref[0], v_ref[0].astype(BF16)
            cqv = cq_ref[0]
            a = acc[...]
            for h, hm in enumerate(_lane_masks()):
                _, t = _fox_scores(q, k, ck_ref, h, hm, diagonal)
                cqh = _head_col(cqv, h)
                m_prev = jnp.max(m_sc[h], axis=-1, keepdims=True)
                l_prev = jnp.max(l_sc[h], axis=-1, keepdims=True)
                m_new = jnp.maximum(m_prev, jnp.max(t, axis=-1, keepdims=True) + cqh)
                alpha = jnp.exp(m_prev - m_new)
                p = jnp.exp(t + (cqh - m_new))
                l_new = alpha * l_prev + jnp.sum(p, axis=-1, keepdims=True)
                a = jnp.where(hm, alpha * a + _dot(p.astype(BF16), v, NN), a)
                m_sc[h] = jnp.broadcast_to(m_new, (TB, LANES))
                l_sc[h] = jnp.broadcast_to(l_new, (TB, LANES))
            acc[...] = a

        pl.when(kb < qb)(functools.partial(step, False))
        pl.when(kb == qb)(functools.partial(step, True))

        @pl.when(kb == qb)
        def _():
            a = acc[...]
            out = a
            lse = jnp.zeros_like(a)
            for h, hm in enumerate(_lane_masks()):
                out = jnp.where(hm, a / l_sc[h], out)
                lse = jnp.where(hm, m_sc[h] + jnp.log(l_sc[h]), lse)
            o_ref[0] = out
            lse_ref[0] = lse

    blk = (1, TB, LANES)
    qmap = lambda b, i, h, j: (b, i, h)
    kmap = lambda b, i, h, j: (b, jnp.minimum(j, i), h)
    vmap = lambda b, i, h, j: (b, jnp.minimum(j, i), HP + h)
    cmap = lambda b, i, h, j: (b * HP + h, 0, jnp.minimum(j, i))
    out = jax.ShapeDtypeStruct((B, S, D), F32)
    o, lse = pl.pallas_call(
        body, name=name, grid=(B, nb, HP, nb),
        in_specs=[pl.BlockSpec(blk, qmap), pl.BlockSpec(blk, kmap), pl.BlockSpec(blk, vmap), pl.BlockSpec(blk, qmap),
                  pl.BlockSpec((1, 8, TB), cmap)],
        out_specs=[pl.BlockSpec(blk, qmap)] * 2, out_shape=[out, out],
        scratch_shapes=[pltpu.VMEM((2, TB, LANES), F32), pltpu.VMEM((2, TB, LANES), F32), pltpu.VMEM((TB, LANES), F32)],
        compiler_params=_params("parallel", "parallel", "parallel", "arbitrary"),
    )(qh.reshape(B, S, D), kh.reshape(B, S, D), kv.reshape(B, S, 2 * D), cq.reshape(B, S, D), ck)
    return o.reshape(B * S, D), lse.reshape(B * S, D)


def _fox_bwd_dq(qh, kh, kv, cq, ck, do, lse, delta, B, S, D, name):
    HP = D // LANES
    TB = _pick(S, 512)
    nb = S // TB

    def body(q_ref, k_ref, v_ref, cq_ref, ck_ref, do_ref, l_ref, d_ref, dq_ref, dc_ref, acc, acc_c):
        qb, kb = pl.program_id(1), pl.program_id(3)

        @pl.when(kb == 0)
        def _():
            acc[...] = jnp.zeros_like(acc)
            acc_c[...] = jnp.zeros_like(acc_c)

        def step(diagonal):
            q, k, v = q_ref[0], k_ref[0], v_ref[0].astype(BF16)
            dout = do_ref[0].astype(BF16)
            a, c = acc[...], acc_c[...]
            for h, hm in enumerate(_lane_masks()):
                _, t = _fox_scores(q, k, ck_ref, h, hm, diagonal)
                p = jnp.exp(t + (_head_col(cq_ref[0], h) - _head_col(l_ref[0], h)))
                dom = jnp.where(hm, dout, jnp.zeros_like(dout))
                ds = p * (_dot(dom, v, NT) - _head_col(d_ref[0], h))
                a += _dot(ds.astype(BF16), jnp.where(hm, k, jnp.zeros_like(k)), NN)
                c = jnp.where(hm, c + jnp.sum(ds, axis=-1, keepdims=True), c)
            acc[...] = a
            acc_c[...] = c

        pl.when(kb < qb)(functools.partial(step, False))
        pl.when(kb == qb)(functools.partial(step, True))

        @pl.when(kb == qb)
        def _():
            dq_ref[0] = acc[...]
            dc_ref[0] = acc_c[...]

    blk = (1, TB, LANES)
    qmap = lambda b, i, h, j: (b, i, h)
    kmap = lambda b, i, h, j: (b, jnp.minimum(j, i), h)
    vmap = lambda b, i, h, j: (b, jnp.minimum(j, i), HP + h)
    cmap = lambda b, i, h, j: (b * HP + h, 0, jnp.minimum(j, i))
    r3 = lambda t: t.reshape(B, S, D)
    out = jax.ShapeDtypeStruct((B, S, D), F32)
    dq, dc = pl.pallas_call(
        body, name=name, grid=(B, nb, HP, nb),
        in_specs=[pl.BlockSpec(blk, qmap), pl.BlockSpec(blk, kmap), pl.BlockSpec(blk, vmap), pl.BlockSpec(blk, qmap),
                  pl.BlockSpec((1, 8, TB), cmap), pl.BlockSpec(blk, qmap), pl.BlockSpec(blk, qmap), pl.BlockSpec(blk, qmap)],
        out_specs=[pl.BlockSpec(blk, qmap)] * 2, out_shape=[out, out],
        scratch_shapes=[pltpu.VMEM((TB, LANES), F32), pltpu.VMEM((TB, LANES), F32)],
        compiler_params=_params("parallel", "parallel", "parallel", "arbitrary"),
    )(r3(qh), r3(kh), kv.reshape(B, S, 2 * D), r3(cq), ck, r3(do), r3(lse), r3(delta))
    return dq.reshape(B * S, D), dc.reshape(B * S, D)


def _fox_bwd_dkv(qh, kh, kv, cq, ck, do, lse, delta, B, S, D, name):
    HP = D // LANES
    TB = _pick(S, 512)
    nb = S // TB

    def body(q_ref, k_ref, v_ref, cq_ref, ck_ref, do_ref, l_ref, d_ref, dk_ref, dv_ref, dc_ref, dk_acc, dv_acc, dc_acc):
        kb, qb = pl.program_id(1), pl.program_id(3)

        @pl.when(qb == 0)
        def _():
            dk_acc[...] = jnp.zeros_like(dk_acc)
            dv_acc[...] = jnp.zeros_like(dv_acc)
            dc_acc[...] = jnp.zeros_like(dc_acc)

        def step(diagonal):
            q, k, v = q_ref[0], k_ref[0], v_ref[0].astype(BF16)
            dout = do_ref[0].astype(BF16)
            dk, dv, dc = dk_acc[...], dv_acc[...], dc_acc[...]
            row = _iota2((8, TB), 0)
            for h, hm in enumerate(_lane_masks()):
                qm, t = _fox_scores(q, k, ck_ref, h, hm, diagonal)
                p = jnp.exp(t + (_head_col(cq_ref[0], h) - _head_col(l_ref[0], h)))
                dom = jnp.where(hm, dout, jnp.zeros_like(dout))
                ds = p * (_dot(dom, v, NT) - _head_col(d_ref[0], h))
                dv += _dot(p.astype(BF16), dom, TN)
                dk += _dot(ds.astype(BF16), qm, TN)
                dc = jnp.where(row == h, dc - jnp.sum(ds, axis=0, keepdims=True), dc)
            dk_acc[...] = dk
            dv_acc[...] = dv
            dc_acc[...] = dc

        pl.when(qb > kb)(functools.partial(step, False))
        pl.when(qb == kb)(functools.partial(step, True))

        @pl.when(qb == nb - 1)
        def _():
            dk_ref[0] = dk_acc[...]
            dv_ref[0] = dv_acc[...].astype(BF16)
            dc_ref[0] = dc_acc[...]

    blk = (1, TB, LANES)
    qmap = lambda b, j, h, i: (b, jnp.maximum(i, j), h)
    kmap = lambda b, j, h, i: (b, j, h)
    vmap = lambda b, j, h, i: (b, j, HP + h)
    cmap = lambda b, j, h, i: (b * HP + h, 0, j)
    r3 = lambda t: t.reshape(B, S, D)
    dk, dv, dc = pl.pallas_call(
        body, name=name, grid=(B, nb, HP, nb),
        in_specs=[pl.BlockSpec(blk, qmap), pl.BlockSpec(blk, kmap), pl.BlockSpec(blk, vmap), pl.BlockSpec(blk, qmap),
                  pl.BlockSpec((1, 8, TB), cmap), pl.BlockSpec(blk, qmap), pl.BlockSpec(blk, qmap), pl.BlockSpec(blk, qmap)],
        out_specs=[pl.BlockSpec(blk, kmap), pl.BlockSpec(blk, kmap), pl.BlockSpec((1, 8, TB), cmap)],
        out_shape=[jax.ShapeDtypeStruct((B, S, D), F32), jax.ShapeDtypeStruct((B, S, D), BF16),
                   jax.ShapeDtypeStruct((B * HP, 8, S), F32)],
        scratch_shapes=[pltpu.VMEM((TB, LANES), F32), pltpu.VMEM((TB, LANES), F32), pltpu.VMEM((8, TB), F32)],
        compiler_params=_params("parallel", "parallel", "parallel", "arbitrary"),
    )(r3(qh), r3(kh), kv.reshape(B, S, 2 * D), r3(cq), ck, r3(do), r3(lse), r3(delta))
    return dk.reshape(B * S, D), dv.reshape(B * S, D), dc


def _loss_and_grad(y, target, name):
    T, D = y.shape
    tm = _pick(T, 256, 8)

    def body(y_ref, t_ref, p_ref, dy_ref):
        e = y_ref[...] - t_ref[...]
        dy_ref[...] = e * (1.0 / D)
        p_ref[...] = jnp.sum(e * e, axis=0, keepdims=True)[None]

    blk = pl.BlockSpec((tm, D), lambda i: (i, 0))
    return pl.pallas_call(
        body, name=name, grid=(T // tm,), in_specs=[blk, blk],
        out_specs=[pl.BlockSpec((1, 1, D), lambda i: (i, 0, 0)), blk],
        out_shape=[jax.ShapeDtypeStruct((T // tm, 1, D), F32), jax.ShapeDtypeStruct((T, D), F32)],
        compiler_params=_params("parallel"),
    )(y, target)


def _adamw(parts, w, m, v, name):
    R, C = w.shape
    tr = _pick(R, max(16, (PACK_ROWS * LANES) // C), 16)

    def body(p_ref, w_ref, m_ref, v_ref, g_out, d_out, m_out, v_out):
        g = p_ref[0].astype(F32)
        for s in range(1, N_DEV):
            g = g + p_ref[s].astype(F32)
        mn = ADAM_B1 * m_ref[...] + (1.0 - ADAM_B1) * g
        vn = ADAM_B2 * v_ref[...] + (1.0 - ADAM_B2) * (g * g)
        m_hat = mn / (1.0 - ADAM_B1 ** ADAM_STEP)
        v_hat = vn / (1.0 - ADAM_B2 ** ADAM_STEP)
        g_out[...] = g
        d_out[...] = -ADAM_LR * (m_hat / (jnp.sqrt(v_hat) + ADAM_EPS) + ADAM_WD * w_ref[...])
        m_out[...] = mn
        v_out[...] = vn

    blk = pl.BlockSpec((tr, C), lambda i: (i, 0))
    out = jax.ShapeDtypeStruct((R, C), F32)
    return pl.pallas_call(
        body, name=name, grid=(R // tr,),
        in_specs=[pl.BlockSpec((N_DEV, tr, C), lambda i: (0, i, 0)), blk, blk, blk],
        out_specs=[blk] * 4, out_shape=[out] * 4, compiler_params=_params("parallel"),
    )(parts, w, m, v)


def _mesh_pos():
    return lax.axis_index("x"), lax.axis_index("y"), lax.axis_index("c")


def _comm_call(body, xs, out_shapes, name):
    n = len(xs)
    any_spec = pl.BlockSpec(memory_space=pl.ANY)
    return pl.pallas_call(
        body, name=name, out_shape=out_shapes, in_specs=[any_spec] * n, out_specs=[any_spec] * n,
        scratch_shapes=[pltpu.SemaphoreType.DMA((7, n)), pltpu.SemaphoreType.DMA((7, n)), pltpu.SemaphoreType.DMA((n,))],
    )(*xs)


def _all_gather(xs, name):
    n = len(xs)

    def body(*refs):
        x_refs, out_refs = refs[:n], refs[n:2 * n]
        send_sems, recv_sems, local_sems = refs[2 * n:]
        x_, y_, c_ = _mesh_pos()
        me, sibling = (x_, y_, c_), (x_, y_, 1 - c_)
        chips = [(1 - x_, y_), (x_, 1 - y_), (1 - x_, 1 - y_)]

        def slot(t, px, py, pc):
            return out_refs[t].at[4 * px + 2 * py + pc]

        def copy(t, k, block, to, src=None):
            return pltpu.make_async_remote_copy(
                src_ref=slot(t, *block) if src is None else src, dst_ref=slot(t, *block),
                send_sem=send_sems.at[k, t], recv_sem=recv_sems.at[k, t], device_id=to, device_id_type=MESH_ID)

        mine = [pltpu.make_async_copy(x_refs[t], slot(t, *me), local_sems.at[t]) for t in range(n)]
        first = [copy(t, 0, me, sibling, src=x_refs[t]) for t in range(n)]
        first += [copy(t, 1 + j, me, (*chip, c_), src=x_refs[t]) for j, chip in enumerate(chips) for t in range(n)]
        for cp in mine + first:
            cp.start()
        passed = []
        for j, chip in enumerate(chips):
            for t in range(n):
                copy(t, 1 + j, (*chip, c_), me).wait_recv()
                passed.append(copy(t, 4 + j, (*chip, c_), sibling))
                passed[-1].start()
        for t in range(n):
            copy(t, 0, sibling, me).wait_recv()
        for j, chip in enumerate(chips):
            for t in range(n):
                copy(t, 4 + j, (*chip, 1 - c_), me).wait_recv()
        for cp in first + passed:
            cp.wait_send()
        for cp in mine:
            cp.wait()

    return _comm_call(body, xs, [jax.ShapeDtypeStruct((N_DEV,) + x.shape, x.dtype) for x in xs], name)


def _all_to_all(xs, name):
    n = len(xs)

    def body(*refs):
        x_refs, out_refs = refs[:n], refs[n:2 * n]
        send_sems, recv_sems, local_sems = refs[2 * n:]
        x_, y_, c_ = _mesh_pos()
        me = 4 * x_ + 2 * y_ + c_
        mine = [pltpu.make_async_copy(x_refs[t].at[me], out_refs[t].at[me], local_sems.at[t]) for t in range(n)]
        copies = []
        for k in range(1, N_DEV):
            px, py, pc = x_ ^ (k >> 2), y_ ^ ((k >> 1) & 1), c_ ^ (k & 1)
            for t in range(n):
                copies.append(pltpu.make_async_remote_copy(
                    src_ref=x_refs[t].at[4 * px + 2 * py + pc], dst_ref=out_refs[t].at[me],
                    send_sem=send_sems.at[k - 1, t], recv_sem=recv_sems.at[k - 1, t],
                    device_id=(px, py, pc), device_id_type=MESH_ID))
        for cp in mine + copies:
            cp.start()
        for cp in copies:
            cp.wait_recv()
        for cp in copies:
            cp.wait_send()
        for cp in mine:
            cp.wait()

    return _comm_call(body, xs, [jax.ShapeDtypeStruct(x.shape, x.dtype) for x in xs], name)


def _ffn_fwd(h, norm, w_in, w_out, tag):
    T, D = h.shape
    F = w_out.shape[0]
    hn = _rmsnorm_fwd(h, norm, f"{tag}_norm")
    gate, up, act = _mm_swiglu_fwd(hn, w_in, f"{tag}_in")
    out = _mm(act, w_out, "nn", M=T, N=D, K=F, res=h, scale=0.5, name=f"{tag}_out")
    return out, (h, hn, gate, up, act)


def _ffn_bwd(dout, saved, norm, w_in, w_out, tag):
    h, hn, gate, up, act = saved
    T, D = h.shape
    F = w_out.shape[0]
    dg, du = _mm_swiglu_bwd(dout, w_out, gate, up, f"{tag}_dact")
    dw_out = _mm(act, dout, "tn", M=F, N=D, K=T, scale=0.5, out_dtype=BF16, name=f"{tag}_dwout")
    dhn = _mm(dg, w_in, "nt", M=T, N=D, K=F, name=f"{tag}_dhn_g")
    dhn = _mm(du, w_in, "nt", M=T, N=D, K=F, b_off=(0, F), res=dhn, name=f"{tag}_dhn_u")
    dw_g = _mm(hn, dg, "tn", M=D, N=F, K=T, out_dtype=BF16, name=f"{tag}_dwin_g")
    dw_u = _mm(hn, du, "tn", M=D, N=F, K=T, out_dtype=BF16, name=f"{tag}_dwin_u")
    dh, dnorm = _rmsnorm_bwd(h, norm, dhn, dout, f"{tag}_dnorm")
    return dh, dnorm, jnp.concatenate([dw_g, dw_u], axis=1), dw_out


def _local_step(x, positions, target, W):
    B, S, D = x.shape
    T = B * S
    H, HP = D // HEAD_DIM, D // LANES
    G = {}
    h0 = x.reshape(T, D)
    cs = _rope_tables(positions, "rope_tables")

    h1, ffn00 = _ffn_fwd(h0, W["ffn_norm"][0, 0], W["ffn_w_in"][0, 0], W["ffn_w_out"][0, 0], "ffn00")
    hna = _rmsnorm_fwd(h1, W["mix_norm"][0], "mixa_norm")
    qkv = _mm(hna, W["a_w_qkv"], "nn", M=T, N=9 * D, K=D, name="a_qkv")
    qs, ks, os_, lses = [], [], [], []
    for g in range(3):
        qs.append(_qknorm_fwd(qkv, 3 * g * D, W["a_q_norm"][0, g], cs, D, F32, f"a_qnorm{g}", out_scale=SCALE))
        ks.append(_qknorm_fwd(qkv, (3 * g + 1) * D, W["a_k_norm"][0, g], cs, D, F32, f"a_knorm{g}"))
        o, l = _band_fwd(qs[g], ks[g], qkv, g, B, S, D, f"a_band{g}")
        os_.append(o)
        lses.append(l)
    mixed, lse_a = _mix(os_, lses, "a_mix")
    h2 = _mm(mixed, W["a_w_o"], "nn", M=T, N=D, K=D, res=h1, name="a_out")
    h3, ffn01 = _ffn_fwd(h2, W["ffn_norm"][0, 1], W["ffn_w_in"][0, 1], W["ffn_w_out"][0, 1], "ffn01")

    hnkv = _rmsnorm_fwd(h3, W["kv_norm"], "kv_norm")
    kv = _mm(hnkv, W["kv_w_kv"], "nn", M=T, N=2 * D, K=D, name="kv_proj")
    fpre = _mm(hnkv, W["kv_w_f"], "nn", M=T, N=LANES, K=D, name="kv_gate_proj")
    b_f = jnp.pad(W["kv_b_f"], (0, LANES - H)).reshape(1, LANES)
    kh = _qknorm_fwd(kv, 0, W["kv_k_norm"], None, D, BF16, "kv_knorm")
    cum, cq = _gate_fwd(fpre, b_f, B, S, D, "kv_gate")
    ck = cum.reshape(B, S, LANES)[:, :, :H].reshape(B, S, HP, 2).transpose(0, 2, 3, 1)
    ck = jnp.pad(ck, ((0, 0), (0, 0), (0, 6), (0, 0))).reshape(B * HP, 8, S)
    h4, ffn10 = _ffn_fwd(h3, W["ffn_norm"][1, 0], W["ffn_w_in"][1, 0], W["ffn_w_out"][1, 0], "ffn10")
    hnb = _rmsnorm_fwd(h4, W["mix_norm"][1], "mixb_norm")
    qraw = _mm(hnb, W["b_w_q"], "nn", M=T, N=D, K=D, name="b_q")
    qh = _qknorm_fwd(qraw, 0, W["b_q_norm"][0], None, D, BF16, "b_qnorm", out_scale=SCALE)
    ob, lse_b = _fox_fwd(qh, kh, kv, cq, ck, B, S, D, "b_fox")
    h5 = _mm(ob, W["b_w_o"], "nn", M=T, N=D, K=D, res=h4, name="b_out")
    h6, ffn11 = _ffn_fwd(h5, W["ffn_norm"][1, 1], W["ffn_w_in"][1, 1], W["ffn_w_out"][1, 1], "ffn11")

    sq, dy = _loss_and_grad(h6, target.reshape(T, D), "loss")

    dn = [[None, None], [None, None]]
    dwi = [[None, None], [None, None]]
    dwo = [[None, None], [None, None]]
    dh5, dn[1][1], dwi[1][1], dwo[1][1] = _ffn_bwd(dy, ffn11, W["ffn_norm"][1, 1], W["ffn_w_in"][1, 1], W["ffn_w_out"][1, 1], "ffn11")
    dob = _mm(dh5, W["b_w_o"], "nt", M=T, N=D, K=D, name="b_dout")
    G["b_w_o"] = _mm(ob, dh5, "tn", M=D, N=D, K=T, out_dtype=BF16, name="b_dwo")[None]
    delta_b = _head_rowsum(dob, ob, "b_delta")
    dqh, dcq = _fox_bwd_dq(qh, kh, kv, cq, ck, dob, lse_b, delta_b, B, S, D, "b_fox_dq")
    dkh, dv_b, dck = _fox_bwd_dkv(qh, kh, kv, cq, ck, dob, lse_b, delta_b, B, S, D, "b_fox_dkv")
    dqraw, dg = _qknorm_bwd(qraw, 0, W["b_q_norm"][0], None, dqh, D, "b_dqnorm", out_scale=SCALE)
    G["b_q_norm"] = dg[None]
    dhnb = _mm(dqraw, W["b_w_q"], "nt", M=T, N=D, K=D, name="b_dhn")
    G["b_w_q"] = _mm(hnb, dqraw, "tn", M=D, N=D, K=T, out_dtype=BF16, name="b_dwq")[None]
    dh4, dmix_b = _rmsnorm_bwd(h4, W["mix_norm"][1], dhnb, dh5, "mixb_dnorm")
    dh3, dn[1][0], dwi[1][0], dwo[1][0] = _ffn_bwd(dh4, ffn10, W["ffn_norm"][1, 0], W["ffn_w_in"][1, 0], W["ffn_w_out"][1, 0], "ffn10")

    dkraw, G["kv_k_norm"] = _qknorm_bwd(kv, 0, W["kv_k_norm"], None, dkh, D, "kv_dknorm")
    dcum = dck.reshape(B, HP, 8, S)[:, :, :2].transpose(0, 3, 1, 2).reshape(T, H)
    dcum = jnp.pad(dcum, ((0, 0), (0, LANES - H)))
    dcum_q = jnp.pad(dcq.reshape(T, H, HEAD_DIM)[:, :, 0], ((0, 0), (0, LANES - H)))
    dfpre, db_f = _gate_bwd(dcum, dcum_q, fpre, b_f, B, S, "kv_dgate")
    G["kv_b_f"] = db_f[0, :H]
    dhnkv = _mm(dkraw, W["kv_w_kv"], "nt", M=T, N=D, K=D, name="kv_dhn_k")
    dhnkv = _mm(dv_b, W["kv_w_kv"], "nt", M=T, N=D, K=D, b_off=(0, D), res=dhnkv, name="kv_dhn_v")
    dhnkv = _mm(dfpre, W["kv_w_f"], "nt", M=T, N=D, K=LANES, res=dhnkv, name="kv_dhn_f")
    dw_k = _mm(hnkv, dkraw, "tn", M=D, N=D, K=T, out_dtype=BF16, name="kv_dwk")
    dw_v = _mm(hnkv, dv_b, "tn", M=D, N=D, K=T, out_dtype=BF16, name="kv_dwv")
    dw_f = _mm(hnkv, dfpre, "tn", M=D, N=LANES, K=T, out_dtype=BF16, name="kv_dwf")
    G["kv_w"] = jnp.concatenate([dw_k, dw_v, dw_f[:, :H]], axis=1)
    dh3, G["kv_norm"] = _rmsnorm_bwd(h3, W["kv_norm"], dhnkv, dh3, "kv_dnorm")

    dh2, dn[0][1], dwi[0][1], dwo[0][1] = _ffn_bwd(dh3, ffn01, W["ffn_norm"][0, 1], W["ffn_w_in"][0, 1], W["ffn_w_out"][0, 1], "ffn01")
    dmixed = _mm(dh2, W["a_w_o"], "nt", M=T, N=D, K=D, name="a_dmixed")
    G["a_w_o"] = _mm(mixed, dh2, "tn", M=D, N=D, K=T, out_dtype=BF16, name="a_dwo")[None]
    delta_a = _head_rowsum(dmixed, mixed, "a_delta")
    dhna = None
    dw_qkv, dqn, dkn = [], [], []
    for g in range(3):
        dq, dk, dv = _band_bwd(qs[g], ks[g], qkv, dmixed, lse_a, delta_a, g, B, S, D, f"a_dband{g}")
        dqr, dgq = _qknorm_bwd(qkv, 3 * g * D, W["a_q_norm"][0, g], cs, dq, D, f"a_dqnorm{g}", out_scale=SCALE)
        dkr, dgk = _qknorm_bwd(qkv, (3 * g + 1) * D, W["a_k_norm"][0, g], cs, dk, D, f"a_dknorm{g}")
        dqn.append(dgq)
        dkn.append(dgk)
        for j, d in enumerate((dqr, dkr, dv)):
            c = 3 * g + j
            dhna = _mm(d, W["a_w_qkv"], "nt", M=T, N=D, K=D, b_off=(0, c * D), res=dhna, name=f"a_dhn{c}")
            dw_qkv.append(_mm(hna, d, "tn", M=D, N=D, K=T, out_dtype=BF16, name=f"a_dwqkv{c}"))
    G["a_w_qkv"] = jnp.concatenate(dw_qkv, axis=1)[None]
    G["a_q_norm"] = jnp.stack(dqn)[None]
    G["a_k_norm"] = jnp.stack(dkn)[None]
    dh1, dmix_a = _rmsnorm_bwd(h1, W["mix_norm"][0], dhna, dh2, "mixa_dnorm")
    dh0, dn[0][0], dwi[0][0], dwo[0][0] = _ffn_bwd(dh1, ffn00, W["ffn_norm"][0, 0], W["ffn_w_in"][0, 0], W["ffn_w_out"][0, 0], "ffn00")

    G["mix_norm"] = jnp.stack([dmix_a, dmix_b])
    G["ffn_norm"] = jnp.stack([jnp.stack(r) for r in dn])
    G["ffn_w_in"] = jnp.stack([jnp.stack(r) for r in dwi])
    G["ffn_w_out"] = jnp.stack([jnp.stack(r) for r in dwo])
    return jnp.sum(sq), dh0.reshape(B, S, D), G


SMALL =["mix_norm", "kv_norm", "a_q_norm", "a_k_norm", "kv_b_f", "kv_k_norm", "b_q_norm"]


def _pack(blocks, lead):
    flat = [b.reshape(b.shape[:lead] + (-1,)) for b in blocks]
    flat = jnp.concatenate(flat, axis=lead)
    n = flat.shape[-1]
    unit = PACK_ROWS * LANES
    padded = -(-n // unit) * unit if n > unit else -(-n // (16 * LANES)) * 16 * LANES
    flat = jnp.pad(flat, [(0, 0)] * lead + [(0, padded - n)])
    return flat.reshape(flat.shape[:lead] + (padded // LANES, LANES))


def _unpack(packed, shapes, lead):
    flat = packed.reshape(packed.shape[:lead] + (-1,))
    out, off = [], 0
    for shp in shapes:
        n = int(np.prod(shp))
        out.append(flat[..., off:off + n].reshape(packed.shape[:lead] + tuple(shp)))
        off += n
    return out


def kernel(x, positions, ffn_norm, ffn_w_in, ffn_w_out, mix_norm, a_w_qkv, a_q_norm, a_k_norm, a_w_o, kv_norm, kv_w, kv_b_f, kv_k_norm, b_w_q, b_q_norm, b_w_o, loss_target, m_ffn_norm, m_ffn_w_in, m_ffn_w_out, m_mix_norm, m_a_w_qkv, m_a_q_norm, m_a_k_norm, m_a_w_o, m_kv_norm, m_kv_w, m_kv_b_f, m_kv_k_norm, m_b_w_q, m_b_q_norm, m_b_w_o, v_ffn_norm, v_ffn_w_in, v_ffn_w_out, v_mix_norm, v_a_w_qkv, v_a_q_norm, v_a_k_norm, v_a_w_o, v_kv_norm, v_kv_w, v_kv_b_f, v_kv_k_norm, v_b_w_q, v_b_q_norm, v_b_w_o):
    names = ["ffn_norm", "ffn_w_in", "ffn_w_out", "mix_norm", "a_w_qkv", "a_q_norm", "a_k_norm", "a_w_o", "kv_norm",
             "kv_w", "kv_b_f", "kv_k_norm", "b_w_q", "b_q_norm", "b_w_o"]
    w = dict(zip(names, (ffn_norm, ffn_w_in, ffn_w_out, mix_norm, a_w_qkv, a_q_norm, a_k_norm, a_w_o, kv_norm, kv_w,
                         kv_b_f, kv_k_norm, b_w_q, b_q_norm, b_w_o)))
    m = dict(zip(names, (m_ffn_norm, m_ffn_w_in, m_ffn_w_out, m_mix_norm, m_a_w_qkv, m_a_q_norm, m_a_k_norm, m_a_w_o,
                         m_kv_norm, m_kv_w, m_kv_b_f, m_kv_k_norm, m_b_w_q, m_b_q_norm, m_b_w_o)))
    v = dict(zip(names, (v_ffn_norm, v_ffn_w_in, v_ffn_w_out, v_mix_norm, v_a_w_qkv, v_a_q_norm, v_a_k_norm, v_a_w_o,
                         v_kv_norm, v_kv_w, v_kv_b_f, v_kv_k_norm, v_b_w_q, v_b_q_norm, v_b_w_o)))
    B, S, D = x.shape
    H = D // HEAD_DIM
    F = ffn_w_out.shape[2] * N_DEV
    Ds, Fs = D // N_DEV, F // N_DEV
    me = 4 * lax.axis_index("x") + 2 * lax.axis_index("y") + lax.axis_index("c")
    squares = ("a_w_o", "b_w_q", "b_w_o")

    def shard_views(t):
        return [t["ffn_w_in"].reshape(4 * D, -1), t["ffn_w_out"].reshape(-1, D), t["a_w_qkv"][0], t["kv_w"],
                jnp.concatenate([t[n][0] for n in squares], axis=0)]

    def to_shards(views):
        sq3 = views[4].reshape(3, 1, Ds, D)
        out = {"ffn_w_in": views[0].reshape(ffn_w_in.shape), "ffn_w_out": views[1].reshape(ffn_w_out.shape),
               "a_w_qkv": views[2][None], "kv_w": views[3]}
        out.update({n: sq3[i] for i, n in enumerate(squares)})
        return out

    norm_rows = jnp.pad(ffn_norm.reshape(4, Ds), ((0, 4), (0, 0)))
    g_in, g_out, g_qkv, g_kv, g_sq, g_norm = _all_gather(
        [s.astype(BF16) for s in shard_views(w)] + [norm_rows], "gather_weights")
    W = {n: w[n] for n in SMALL}
    W["ffn_norm"] = g_norm[:, :4].transpose(1, 0, 2).reshape(2, 2, D)
    W["ffn_w_in"] = g_in.transpose(1, 0, 2).reshape(2, 2, D, 2 * F)
    W["ffn_w_out"] = g_out.reshape(N_DEV, 4, Fs, D).transpose(1, 0, 2, 3).reshape(2, 2, F, D)
    W["a_w_qkv"] = g_qkv.transpose(1, 0, 2).reshape(D, 9 * D)
    kv_full = g_kv.transpose(1, 0, 2).reshape(D, 2 * D + H)
    W["kv_w_kv"] = kv_full[:, :2 * D]
    W["kv_w_f"] = jnp.pad(kv_full[:, 2 * D:], ((0, 0), (0, LANES - H)))
    sq_full = g_sq.reshape(N_DEV, 3, Ds, D).transpose(1, 0, 2, 3).reshape(3, D, D)
    W["a_w_o"], W["b_w_q"], W["b_w_o"] = sq_full[0], sq_full[1], sq_full[2]

    sq, grad_x, G = _local_step(x, positions, loss_target, W)
    loss = lax.psum(sq * (0.5 / D), ("x", "y", "c"))

    blocks = [G["ffn_w_in"].reshape(4 * D, N_DEV, -1).transpose(1, 0, 2),
              G["ffn_w_out"].reshape(4, N_DEV, Fs, D).transpose(1, 0, 2, 3).reshape(N_DEV, 4 * Fs, D),
              G["a_w_qkv"][0].reshape(D, N_DEV, -1).transpose(1, 0, 2),
              G["kv_w"].reshape(D, N_DEV, -1).transpose(1, 0, 2),
              jnp.concatenate([G[n][0].reshape(N_DEV, Ds, D) for n in squares], axis=1)]
    parts = _all_to_all([b.astype(BF16) for b in blocks], "exchange_grads")
    views = [shard_views(t) for t in (w, m, v)]
    outs = [_adamw(parts[i], views[0][i], views[1][i], views[2][i], f"adamw_matrix{i}") for i in range(len(parts))]
    new = {}
    for k, kind in enumerate(("grad", "delta", "new_m", "new_v")):
        for n, arr in to_shards([o[k] for o in outs]).items():
            new[kind, n] = arr

    small_all = ["ffn_norm"] + SMALL
    vec_parts = _all_gather([_pack([G[n] for n in small_all], 0)], "gather_vector_grads")[0]
    vec_parts = _unpack(vec_parts, [G[n].shape for n in small_all], 1)
    norm_g = lax.dynamic_slice_in_dim(vec_parts[0], me * ffn_norm.shape[2], ffn_norm.shape[2], axis=3)
    rep_g = _pack(vec_parts[1:], 1)
    for ns, parts_s in ((["ffn_norm"], _pack([norm_g], 1)), (SMALL, rep_g)):
        outs = _adamw(parts_s, *(_pack([t[n] for n in ns], 0) for t in (w, m, v)), "adamw_" + ns[0])
        for kind, packed in zip(("grad", "delta", "new_m", "new_v"), outs):
            for n, arr in zip(ns, _unpack(packed, [w[n].shape for n in ns], 0)):
                new[kind, n] = arr

    result = [loss, grad_x]
    for kind in ("grad", "delta", "new_m", "new_v"):
        result += [new[kind, n] for n in names]
    return tuple(result)
```

```python
import functools

import jax
import jax.numpy as jnp
import numpy as np
from jax import lax
from jax.experimental import pallas as pl
from jax.experimental.pallas import tpu as pltpu

F32 = jnp.float32
BF16 = jnp.bfloat16
HIGHEST = lax.Precision.HIGHEST

HEAD_DIM = 64
ROT_DIM = HEAD_DIM // 4
ROPE_THETA = 500000.0
DILATED_GROUPS = ((128, 1), (512, 4), (2048, 16))
BAND_BLOCK = 128
EPS = 1e-6
ADAM_LR, ADAM_B1, ADAM_B2, ADAM_EPS, ADAM_WD, ADAM_STEP = 0.001, 0.9, 0.999, 1e-08, 0.01, 10

LANES = 128
N_DEV = 8
VMEM_LIMIT = 56 * 1024 * 1024
NEG = -1e30
SCALE = HEAD_DIM ** -0.5
PACK_ROWS = 1024
MESH_ID = pl.DeviceIdType.MESH


def _params(*sem):
    return pltpu.CompilerParams(dimension_semantics=sem, vmem_limit_bytes=VMEM_LIMIT)


def _divs(n, cap, mult=LANES):
    d = [t for t in range(mult, min(n, cap) + 1, mult) if n % t == 0]
    return d or [n]


def _pick(n, cap, mult=LANES):
    return _divs(n, cap, mult)[-1]


def _mm_tiles(M, N, K, budget=1024 * 1024):
    best = None
    for tm in _divs(M, 1408):
        for tn in _divs(N, 1408):
            if tm * tn <= budget and (best is None or tm * tn > best[0] * best[1]):
                best = (tm, tn)
    if best is None:
        best = (_divs(M, 1408)[0], _divs(N, 1408)[0])
    tm, tn = best
    tks = [t for t in _divs(K, 2048) if (tm + tn) * t <= 3 * 1024 * 1024]
    tk = tks[-1] if tks else _divs(K, 2048)[0]
    return tm, tn, tk


def _dot(a, b, dims, precision=None):
    return lax.dot_general(a, b, (dims, ((), ())), precision=precision, preferred_element_type=F32)


NN = ((1,), (0,))
NT = ((1,), (1,))
TN = ((0,), (0,))


def _dot_split(x, mat):
    hi = x.astype(BF16)
    lo = (x - hi.astype(F32)).astype(BF16)
    m = mat.astype(BF16)
    return _dot(hi, m, NN) + _dot(lo, m, NN)


def _mm(a, b, mode, *, M, N, K, a_off=(0, 0), b_off=(0, 0), res=None, scale=1.0, out_dtype=F32, name):
    tm, tn, tk = _mm_tiles(M, N, K)
    nk = K // tk
    if mode == "tn":
        a_blk, b_blk, dims = (tk, tm), (tk, tn), TN
    elif mode == "nt":
        a_blk, b_blk, dims = (tm, tk), (tn, tk), NT
    else:
        a_blk, b_blk, dims = (tm, tk), (tk, tn), NN
    for off, blk in ((a_off, a_blk), (b_off, b_blk)):
        assert off[0] % blk[0] == 0 and off[1] % blk[1] == 0, (name, off, blk)
    ao = (a_off[0] // a_blk[0], a_off[1] // a_blk[1])
    bo = (b_off[0] // b_blk[0], b_off[1] // b_blk[1])
    if mode == "tn":
        a_map = lambda i, j, k: (k + ao[0], i + ao[1])
        b_map = lambda i, j, k: (k + bo[0], j + bo[1])
    elif mode == "nt":
        a_map = lambda i, j, k: (i + ao[0], k + ao[1])
        b_map = lambda i, j, k: (j + bo[0], k + bo[1])
    else:
        a_map = lambda i, j, k: (i + ao[0], k + ao[1])
        b_map = lambda i, j, k: (k + bo[0], j + bo[1])
    has_res = res is not None

    def body(*refs):
        if has_res:
            a_ref, b_ref, r_ref, o_ref, acc = refs
        else:
            a_ref, b_ref, o_ref, acc = refs
        k = pl.program_id(2)

        @pl.when(k == 0)
        def _():
            acc[...] = jnp.zeros_like(acc)

        acc[...] += _dot(a_ref[...].astype(BF16), b_ref[...].astype(BF16), dims)

        @pl.when(k == nk - 1)
        def _():
            out = acc[...] * scale
            if has_res:
                out = r_ref[...] + out
            o_ref[...] = out.astype(out_dtype)

    in_specs = [pl.BlockSpec(a_blk, a_map), pl.BlockSpec(b_blk, b_map)]
    args = [a, b]
    if has_res:
        in_specs.append(pl.BlockSpec((tm, tn), lambda i, j, k: (i, j)))
        args.append(res)
    return pl.pallas_call(
        body, name=name, grid=(M // tm, N // tn, nk), in_specs=in_specs,
        out_specs=pl.BlockSpec((tm, tn), lambda i, j, k: (i, j)),
        out_shape=jax.ShapeDtypeStruct((M, N), out_dtype),
        scratch_shapes=[pltpu.VMEM((tm, tn), F32)],
        compiler_params=_params("parallel", "parallel", "arbitrary"),
    )(*args)


def _rmsnorm_fwd(x, g, name):
    T, D = x.shape
    tm = _pick(T, 256, 8)

    def body(x_ref, g_ref, y_ref):
        xv = x_ref[...]
        r = lax.rsqrt(jnp.mean(xv * xv, axis=-1, keepdims=True) + EPS)
        y_ref[...] = (xv * r * g_ref[...]).astype(BF16)

    return pl.pallas_call(
        body, name=name, grid=(T // tm,),
        in_specs=[pl.BlockSpec((tm, D), lambda i: (i, 0)), pl.BlockSpec((1, D), lambda i: (0, 0))],
        out_specs=pl.BlockSpec((tm, D), lambda i: (i, 0)),
        out_shape=jax.ShapeDtypeStruct((T, D), BF16), compiler_params=_params("parallel"),
    )(x, g.reshape(1, D))


def _rmsnorm_bwd(x, g, dy, dres, name):
    T, D = x.shape
    tm = _pick(T, 256, 8)

    def body(x_ref, g_ref, dy_ref, dres_ref, dx_ref, dg_ref):
        xv = x_ref[...]
        r = lax.rsqrt(jnp.mean(xv * xv, axis=-1, keepdims=True) + EPS)
        xn = xv * r
        d = dy_ref[...]
        dgy = d * g_ref[...]
        dx_ref[...] = dres_ref[...] + r * (dgy - xn * jnp.mean(dgy * xn, axis=-1, keepdims=True))

        @pl.when(pl.program_id(0) == 0)
        def _():
            dg_ref[...] = jnp.zeros_like(dg_ref)

        dg_ref[...] += jnp.sum(d * xn, axis=0, keepdims=True)

    row = pl.BlockSpec((tm, D), lambda i: (i, 0))
    vec = pl.BlockSpec((1, D), lambda i: (0, 0))
    dx, dg = pl.pallas_call(
        body, name=name, grid=(T // tm,), in_specs=[row, vec, row, row], out_specs=[row, vec],
        out_shape=[jax.ShapeDtypeStruct((T, D), F32), jax.ShapeDtypeStruct((1, D), F32)],
        compiler_params=_params("arbitrary"),
    )(x, g.reshape(1, D), dy, dres)
    return dx, dg.reshape(D)


def _sigmoid(x):
    return 1.0 / (1.0 + jnp.exp(-x))


def _mm_swiglu_fwd(hn, w_in, name):
    T, D = hn.shape
    F = w_in.shape[1] // 2
    tm, tn, _ = _mm_tiles(T, F, D, budget=768 * 1024)
    tk = [t for t in _divs(D, 2048) if (tm + 2 * tn) * t <= 3 * 1024 * 1024][-1]
    nk, nf = D // tk, F // tn

    def body(a_ref, bg_ref, bu_ref, g_ref, u_ref, act_ref, acc_g, acc_u):
        k = pl.program_id(2)

        @pl.when(k == 0)
        def _():
            acc_g[...] = jnp.zeros_like(acc_g)
            acc_u[...] = jnp.zeros_like(acc_u)

        a = a_ref[...].astype(BF16)
        acc_g[...] += _dot(a, bg_ref[...].astype(BF16), NN)
        acc_u[...] += _dot(a, bu_ref[...].astype(BF16), NN)

        @pl.when(k == nk - 1)
        def _():
            gv, uv = acc_g[...], acc_u[...]
            g_ref[...] = gv
            u_ref[...] = uv
            act_ref[...] = (gv * _sigmoid(gv) * uv).astype(BF16)

    tile = pl.BlockSpec((tm, tn), lambda i, j, k: (i, j))
    f32 = jax.ShapeDtypeStruct((T, F), F32)
    return pl.pallas_call(
        body, name=name, grid=(T // tm, nf, nk),
        in_specs=[pl.BlockSpec((tm, tk), lambda i, j, k: (i, k)), pl.BlockSpec((tk, tn), lambda i, j, k: (k, j)),
                  pl.BlockSpec((tk, tn), lambda i, j, k: (k, j + nf))],
        out_specs=[tile, tile, tile], out_shape=[f32, f32, jax.ShapeDtypeStruct((T, F), BF16)],
        scratch_shapes=[pltpu.VMEM((tm, tn), F32), pltpu.VMEM((tm, tn), F32)],
        compiler_params=_params("parallel", "parallel", "arbitrary"),
    )(hn, w_in, w_in)


def _mm_swiglu_bwd(dout, w_out, gate, up, name):
    T, D = dout.shape
    F = w_out.shape[0]
    tm, tn, tk = _mm_tiles(T, F, D, budget=768 * 1024)
    nk = D // tk

    def body(a_ref, b_ref, g_ref, u_ref, dg_ref, du_ref, acc):
        k = pl.program_id(2)

        @pl.when(k == 0)
        def _():
            acc[...] = jnp.zeros_like(acc)

        acc[...] += _dot(a_ref[...].astype(BF16), b_ref[...].astype(BF16), NT)

        @pl.when(k == nk - 1)
        def _():
            d = acc[...] * 0.5
            gv, uv = g_ref[...], u_ref[...]
            s = _sigmoid(gv)
            dg_ref[...] = (d * uv * (s + gv * s * (1.0 - s))).astype(BF16)
            du_ref[...] = (d * gv * s).astype(BF16)

    tile = pl.BlockSpec((tm, tn), lambda i, j, k: (i, j))
    out = jax.ShapeDtypeStruct((T, F), BF16)
    return pl.pallas_call(
        body, name=name, grid=(T // tm, F // tn, nk),
        in_specs=[pl.BlockSpec((tm, tk), lambda i, j, k: (i, k)), pl.BlockSpec((tn, tk), lambda i, j, k: (j, k)), tile, tile],
        out_specs=[tile, tile], out_shape=[out, out], scratch_shapes=[pltpu.VMEM((tm, tn), F32)],
        compiler_params=_params("parallel", "parallel", "arbitrary"),
    )(dout, w_out, gate, up)


def _iota2(shape, dim):
    return lax.broadcasted_iota(jnp.int32, shape, dim)


def _head_ones():
    r, c = _iota2((LANES, LANES), 0), _iota2((LANES, LANES), 1)
    return ((r >> 6) == (c >> 6)).astype(F32)


def _rot_matrix(transpose):
    r, c = _iota2((LANES, LANES), 0), _iota2((LANES, LANES), 1)
    if transpose:
        r, c = c, r
    half = ROT_DIM // 2
    cm = c & (HEAD_DIM - 1)
    neg = (r == c + half) & (cm < half)
    pos = (r == c - half) & (cm >= half) & (cm < ROT_DIM)
    return pos.astype(F32) - neg.astype(F32)


def _rope_tables(positions, name):
    T = positions.size
    tm = _pick(T, 1024, 8)
    inv = ROPE_THETA ** (-jnp.arange(0, ROT_DIM, 2, dtype=F32) / ROT_DIM)
    lane = jnp.concatenate([inv, inv, jnp.zeros((HEAD_DIM - ROT_DIM,), F32)])
    lane = jnp.tile(lane, 2).reshape(1, LANES)

    def body(p_ref, f_ref, c_ref, s_ref):
        ang = p_ref[...].astype(F32) * f_ref[...]
        c_ref[...] = jnp.cos(ang)
        s_ref[...] = jnp.sin(ang)

    out = jax.ShapeDtypeStruct((T, LANES), F32)
    blk = pl.BlockSpec((tm, LANES), lambda i: (i, 0))
    return pl.pallas_call(
        body, name=name, grid=(T // tm,),
        in_specs=[pl.BlockSpec((tm, 1), lambda i: (i, 0)), pl.BlockSpec((1, LANES), lambda i: (0, 0))],
        out_specs=[blk, blk], out_shape=[out, out], compiler_params=_params("parallel"),
    )(positions.reshape(T, 1), lane)


def _qknorm_fwd(src, col_off, gain, cs, D, out_dtype, name, out_scale=1.0):
    T = src.shape[0]
    HP = D // LANES
    tm = _pick(T, 1024, 8)
    co = col_off // LANES
    rope = cs is not None

    def body(*refs):
        if rope:
            x_ref, g_ref, c_ref, s_ref, o_ref = refs
        else:
            x_ref, g_ref, o_ref = refs
        xv = x_ref[...]
        ms = _dot_split(xv * xv, _head_ones()) * (1.0 / HEAD_DIM)
        y = xv * lax.rsqrt(ms + EPS) * g_ref[...]
        if rope:
            y = y * c_ref[...] + _dot_split(y, _rot_matrix(False)) * s_ref[...]
        o_ref[...] = (y * out_scale).astype(out_dtype)

    in_specs = [pl.BlockSpec((tm, LANES), lambda i, h: (i, co + h)), pl.BlockSpec((1, LANES), lambda i, h: (0, 0))]
    args = [src, jnp.tile(gain.reshape(1, HEAD_DIM), (1, 2))]
    if rope:
        in_specs += [pl.BlockSpec((tm, LANES), lambda i, h: (i, 0))] * 2
        args += list(cs)
    return pl.pallas_call(
        body, name=name, grid=(T // tm, HP), in_specs=in_specs,
        out_specs=pl.BlockSpec((tm, LANES), lambda i, h: (i, h)),
        out_shape=jax.ShapeDtypeStruct((T, D), out_dtype), compiler_params=_params("parallel", "parallel"),
    )(*args)


def _qknorm_bwd(src, col_off, gain, cs, dout, D, name, out_scale=1.0):
    T = src.shape[0]
    HP = D // LANES
    tm = _pick(T, 1024, 8)
    co = col_off // LANES
    rope = cs is not None

    def body(*refs):
        if rope:
            x_ref, g_ref, d_ref, c_ref, s_ref, dx_ref, dg_ref = refs
        else:
            x_ref, g_ref, d_ref, dx_ref, dg_ref = refs
        xv = x_ref[...]
        ones = _head_ones()
        ms = _dot_split(xv * xv, ones) * (1.0 / HEAD_DIM)
        r = lax.rsqrt(ms + EPS)
        xn = xv * r
        d = d_ref[...] * out_scale
        if rope:
            d = d * c_ref[...] + _dot_split(d * s_ref[...], _rot_matrix(True))
        dgy = d * g_ref[...]
        mean = _dot_split(dgy * xn, ones) * (1.0 / HEAD_DIM)
        dx_ref[...] = (r * (dgy - xn * mean)).astype(BF16)

        @pl.when(pl.program_id(1) == 0)
        def _():
            dg_ref[...] = jnp.zeros_like(dg_ref)

        dg_ref[...] += jnp.sum(d * xn, axis=0, keepdims=True)[None]

    in_specs = [pl.BlockSpec((tm, LANES), lambda h, i: (i, co + h)), pl.BlockSpec((1, LANES), lambda h, i: (0, 0)),
                pl.BlockSpec((tm, LANES), lambda h, i: (i, h))]
    args = [src, jnp.tile(gain.reshape(1, HEAD_DIM), (1, 2)), dout]
    if rope:
        in_specs += [pl.BlockSpec((tm, LANES), lambda h, i: (i, 0))] * 2
        args += list(cs)
    dx, dg = pl.pallas_call(
        body, name=name, grid=(HP, T // tm), in_specs=in_specs,
        out_specs=[pl.BlockSpec((tm, LANES), lambda h, i: (i, h)), pl.BlockSpec((1, 1, LANES), lambda h, i: (h, 0, 0))],
        out_shape=[jax.ShapeDtypeStruct((T, D), BF16), jax.ShapeDtypeStruct((HP, 1, LANES), F32)],
        compiler_params=_params("parallel", "arbitrary"),
    )(*args)
    return dx, dg.reshape(2 * HP, HEAD_DIM).sum(axis=0)


def _head_rowsum(a, b, name):
    T, D = a.shape
    tm = _pick(T, 1024, 8)

    def body(a_ref, b_ref, o_ref):
        o_ref[...] = _dot_split(a_ref[...] * b_ref[...], _head_ones())

    blk = pl.BlockSpec((tm, LANES), lambda i, h: (i, h))
    return pl.pallas_call(
        body, name=name, grid=(T // tm, D // LANES), in_specs=[blk, blk], out_specs=blk,
        out_shape=jax.ShapeDtypeStruct((T, D), F32), compiler_params=_params("parallel", "parallel"),
    )(a, b)


def _lane_masks():
    lane = _iota2((1, LANES), 1)
    return (lane < HEAD_DIM, lane >= HEAD_DIM)


BAND_GROUP = 4


def _band_geometry(g, S):
    window, dil = DILATED_GROUPS[g]
    chunk = BAND_BLOCK * dil
    assert window // dil == BAND_BLOCK and S % chunk == 0
    if dil == 1:
        assert (S // chunk) % BAND_GROUP == 0
        return dil, chunk, BAND_GROUP, 1
    assert dil % BAND_GROUP == 0
    return dil, chunk, 1, dil // BAND_GROUP


def _band_rows(i, rho, dil, shift=0):
    if dil > 1:
        return pl.ds(rho * BAND_GROUP + i, BAND_BLOCK, stride=dil), shift != 0
    j = i + shift
    outside = j < 0 or j >= BAND_GROUP
    return pl.ds(0 if outside else j * BAND_BLOCK, BAND_BLOCK), outside


def _stack_heads(x):
    hm0, hm1 = _lane_masks()
    zero = jnp.zeros_like(x)
    return jnp.concatenate([jnp.where(hm0, x, zero), jnp.where(hm1, x, zero)], axis=0)


def _unstack_heads(x2):
    hm0, _ = _lane_masks()
    return jnp.where(hm0, x2[:BAND_BLOCK], x2[BAND_BLOCK:])


def _band_masks():
    qi = _iota2((2 * BAND_BLOCK, 2 * BAND_BLOCK), 0) & (BAND_BLOCK - 1)
    kj = _iota2((2 * BAND_BLOCK, 2 * BAND_BLOCK), 1)
    own = (kj >= BAND_BLOCK) & (kj - BAND_BLOCK <= qi)
    prev = (kj < BAND_BLOCK) & (kj >= qi)
    return own, prev


def _band_fwd(qh, kh, qkv, g, B, S, D, name):
    dil, CH, NB, RG = _band_geometry(g, S)
    HP = D // LANES
    vo = (3 * g + 2) * HP
    nsp = S // (CH * NB)

    def body(q_ref, k_ref, kp_ref, v_ref, vp_ref, o_ref, lse_ref):
        n, rho = pl.program_id(1), pl.program_id(3)
        own, prev = _band_masks()
        for i in range(BAND_GROUP):
            rows, _ = _band_rows(i, rho, dil)
            prows, outside = _band_rows(i, rho, dil, shift=-1)
            q, kc, vc = (t[0, rows, :].astype(BF16) for t in (q_ref, k_ref, v_ref))
            kp, vp = (t[0, prows, :].astype(BF16) for t in ((kp_ref, vp_ref) if outside else (k_ref, v_ref)))
            valid = (own | (prev & (n > 0))) if outside else (own | prev)
            s = jnp.where(valid, _dot(_stack_heads(q), jnp.concatenate([kp, kc], axis=0), NT), NEG)
            m = jnp.max(s, axis=-1, keepdims=True)
            e = jnp.exp(s - m)
            l = jnp.sum(e, axis=-1, keepdims=True)
            o2 = _dot((e * (1.0 / l)).astype(BF16), jnp.concatenate([vp, vc], axis=0), NN)
            o_ref[0, rows, :] = _unstack_heads(o2)
            lse_ref[0, rows, :] = _unstack_heads(m + jnp.log(l))

    span, chunk = (1, CH * NB, LANES), (1, CH, LANES)
    cur = lambda b, n, h, r: (b, n, h)
    prev = lambda b, n, h, r: (b, jnp.maximum(n * NB - 1, 0), h)
    vcur = lambda b, n, h, r: (b, n, vo + h)
    vprev = lambda b, n, h, r: (b, jnp.maximum(n * NB - 1, 0), vo + h)
    out = jax.ShapeDtypeStruct((B, S, D), F32)
    q3, k3, v3 = qh.reshape(B, S, D), kh.reshape(B, S, D), qkv.reshape(B, S, 9 * D)
    o, lse = pl.pallas_call(
        body, name=name, grid=(B, nsp, HP, RG),
        in_specs=[pl.BlockSpec(span, cur), pl.BlockSpec(span, cur), pl.BlockSpec(chunk, prev),
                  pl.BlockSpec(span, vcur), pl.BlockSpec(chunk, vprev)],
        out_specs=[pl.BlockSpec(span, cur), pl.BlockSpec(span, cur)], out_shape=[out, out],
        compiler_params=_params("parallel", "parallel", "parallel", "arbitrary"),
    )(q3, k3, k3, v3, v3)
    return o.reshape(B * S, D), lse.reshape(B * S, D)


def _head_col(x, h):
    lane = _iota2((1, LANES), 1)
    return jnp.sum(jnp.where(lane == h * HEAD_DIM, x, 0.0), axis=-1, keepdims=True)


def _band_bwd(qh, kh, qkv, do, lse, delta, g, B, S, D, name):
    dil, CH, NB, RG = _band_geometry(g, S)
    HP = D // LANES
    vo = (3 * g + 2) * HP
    nsp, nch = S // (CH * NB), S // CH

    def stacked_cols(tile):
        return jnp.concatenate([_head_col(tile, 0), _head_col(tile, 1)], axis=0)

    def body(q_ref, qn_ref, do_ref, don_ref, l_ref, ln_ref, d_ref, dn_ref, k_ref, kp_ref, v_ref, vp_ref,
             dq_ref, dk_ref, dv_ref):
        n, rho = pl.program_id(1), pl.program_id(3)
        own, prev = _band_masks()
        band = (_iota2((2 * BAND_BLOCK, BAND_BLOCK), 1) >= (_iota2((2 * BAND_BLOCK, BAND_BLOCK), 0) & (BAND_BLOCK - 1)))
        for i in range(BAND_GROUP):
            rows, _ = _band_rows(i, rho, dil)
            prows, p_out = _band_rows(i, rho, dil, shift=-1)
            nrows, n_out = _band_rows(i, rho, dil, shift=1)
            kc, vc = (t[0, rows, :].astype(BF16) for t in (k_ref, v_ref))
            kp, vp = (t[0, prows, :].astype(BF16) for t in ((kp_ref, vp_ref) if p_out else (k_ref, v_ref)))
            q2, do2 = (_stack_heads(t[0, rows, :].astype(BF16)) for t in (q_ref, do_ref))
            qn2, don2 = (_stack_heads(t[0, nrows, :].astype(BF16)) for t in ((qn_ref, don_ref) if n_out else (q_ref, do_ref)))
            lse2, del2 = (stacked_cols(t[0, rows, :]) for t in (l_ref, d_ref))
            lsen2, deln2 = (stacked_cols(t[0, nrows, :]) for t in ((ln_ref, dn_ref) if n_out else (l_ref, d_ref)))
            valid = (own | (prev & (n > 0))) if p_out else (own | prev)
            next_ok = (band & (n < nsp - 1)) if n_out else band
            k2, v2 = jnp.concatenate([kp, kc], axis=0), jnp.concatenate([vp, vc], axis=0)
            p = jnp.exp(jnp.where(valid, _dot(q2, k2, NT) - lse2, NEG))
            ds = (p * (_dot(do2, v2, NT) - del2)).astype(BF16)
            pc = jnp.exp(jnp.where(next_ok, _dot(qn2, kc, NT) - lsen2, NEG))
            dsc = (pc * (_dot(don2, vc, NT) - deln2)).astype(BF16)
            dq_ref[0, rows, :] = _unstack_heads(_dot(ds, k2, NN))
            dk_ref[0, rows, :] = _dot(ds[:, BAND_BLOCK:], q2, TN) + _dot(dsc, qn2, TN)
            dv_ref[0, rows, :] = _dot(p[:, BAND_BLOCK:].astype(BF16), do2, TN) + _dot(pc.astype(BF16), don2, TN)

    span, chunk = (1, CH * NB, LANES), (1, CH, LANES)
    cur = lambda b, n, h, r: (b, n, h)
    prev = lambda b, n, h, r: (b, jnp.maximum(n * NB - 1, 0), h)
    nxt = lambda b, n, h, r: (b, jnp.minimum((n + 1) * NB, nch - 1), h)
    vcur = lambda b, n, h, r: (b, n, vo + h)
    vprev = lambda b, n, h, r: (b, jnp.maximum(n * NB - 1, 0), vo + h)
    q3, k3, do3, l3, d3 = (t.reshape(B, S, D) for t in (qh, kh, do, lse, delta))
    v3 = qkv.reshape(B, S, 9 * D)
    sp, ch = (lambda m: pl.BlockSpec(span, m)), (lambda m: pl.BlockSpec(chunk, m))
    out = jax.ShapeDtypeStruct((B, S, D), F32)
    dq, dk, dv = pl.pallas_call(
        body, name=name, grid=(B, nsp, HP, RG),
        in_specs=[sp(cur), ch(nxt), sp(cur), ch(nxt), sp(cur), ch(nxt), sp(cur), ch(nxt),
                  sp(cur), ch(prev), sp(vcur), ch(vprev)],
        out_specs=[sp(cur)] * 3, out_shape=[out] * 3,
        compiler_params=_params("parallel", "parallel", "parallel", "arbitrary"),
    )(q3, q3, do3, do3, l3, l3, d3, d3, k3, k3, v3, v3)
    return dq.reshape(B * S, D), dk.reshape(B * S, D), dv.reshape(B * S, D)


def _mix(os_, lses, name):
    T, D = os_[0].shape
    tm = _pick(T, 256, 8)

    def body(o0, o1, o2, l0, l1, l2, mix_ref, lse_ref):
        a, b, c = l0[...], l1[...], l2[...]
        m = jnp.maximum(jnp.maximum(a, b), c)
        ea, eb, ec = jnp.exp(a - m), jnp.exp(b - m), jnp.exp(c - m)
        s = ea + eb + ec
        mix_ref[...] = (ea / s) * o0[...] + (eb / s) * o1[...] + (ec / s) * o2[...]
        lse_ref[...] = m + jnp.log(s)

    blk = pl.BlockSpec((tm, D), lambda i: (i, 0))
    out = jax.ShapeDtypeStruct((T, D), F32)
    return pl.pallas_call(
        body, name=name, grid=(T // tm,), in_specs=[blk] * 6, out_specs=[blk, blk], out_shape=[out, out],
        compiler_params=_params("parallel"),
    )(*os_, *lses)


def _gate_fwd(fpre, b_f, B, S, D, name):
    nb = S // LANES

    def body(f_ref, b_ref, cum_ref, rep_ref, carry):
        @pl.when(pl.program_id(1) == 0)
        def _():
            carry[...] = jnp.zeros_like(carry)

        z = f_ref[...] + b_ref[...]
        lf = jnp.minimum(z, 0.0) - jnp.log(1.0 + jnp.exp(-jnp.abs(z)))
        tri = (_iota2((LANES, LANES), 0) >= _iota2((LANES, LANES), 1)).astype(F32)
        c = _dot(tri, lf, NN, HIGHEST) + carry[...]
        cum_ref[...] = c
        carry[...] += jnp.sum(lf, axis=0, keepdims=True)
        spread = ((_iota2((LANES, D), 1) >> 6) == _iota2((LANES, D), 0)).astype(F32)
        rep_ref[...] = _dot(c, spread, NN, HIGHEST)

    return pl.pallas_call(
        body, name=name, grid=(B, nb),
        in_specs=[pl.BlockSpec((LANES, LANES), lambda b, j: (b * nb + j, 0)), pl.BlockSpec((1, LANES), lambda b, j: (0, 0))],
        out_specs=[pl.BlockSpec((LANES, LANES), lambda b, j: (b * nb + j, 0)), pl.BlockSpec((LANES, D), lambda b, j: (b * nb + j, 0))],
        out_shape=[jax.ShapeDtypeStruct((B * S, LANES), F32), jax.ShapeDtypeStruct((B * S, D), F32)],
        scratch_shapes=[pltpu.VMEM((1, LANES), F32)], compiler_params=_params("arbitrary", "arbitrary"),
    )(fpre, b_f)


def _gate_bwd(dcum_k, dcum_q, fpre, b_f, B, S, name):
    nb = S // LANES

    def body(d_ref, dq_ref, f_ref, b_ref, o_ref, db_ref, carry):
        first = (pl.program_id(0) == 0) & (pl.program_id(1) == 0)

        @pl.when(pl.program_id(1) == 0)
        def _():
            carry[...] = jnp.zeros_like(carry)

        @pl.when(first)
        def _():
            db_ref[...] = jnp.zeros_like(db_ref)

        d = d_ref[...] + dq_ref[...]
        tri = (_iota2((LANES, LANES), 0) <= _iota2((LANES, LANES), 1)).astype(F32)
        dlf = _dot(tri, d, NN, HIGHEST) + carry[...]
        carry[...] += jnp.sum(d, axis=0, keepdims=True)
        z = f_ref[...] + b_ref[...]
        dz = dlf * (1.0 / (1.0 + jnp.exp(z)))
        o_ref[...] = dz.astype(BF16)
        db_ref[...] += jnp.sum(dz, axis=0, keepdims=True)

    rev = lambda b, j: (b * nb + nb - 1 - j, 0)
    blk = pl.BlockSpec((LANES, LANES), rev)
    vec = pl.BlockSpec((1, LANES), lambda b, j: (0, 0))
    return pl.pallas_call(
        body, name=name, grid=(B, nb), in_specs=[blk, blk, blk, vec], out_specs=[blk, vec],
        out_shape=[jax.ShapeDtypeStruct((B * S, LANES), BF16), jax.ShapeDtypeStruct((1, LANES), F32)],
        scratch_shapes=[pltpu.VMEM((1, LANES), F32)], compiler_params=_params("arbitrary", "arbitrary"),
    )(dcum_k, dcum_q, fpre, b_f)


def _fox_scores(q, k, ck_ref, h, hm, diagonal):
    TB = q.shape[0]
    qm = jnp.where(hm, q, jnp.zeros_like(q))
    t = _dot(qm, k, NT) - ck_ref[0, h:h + 1, :]
    if diagonal:
        t = jnp.where(_iota2((TB, TB), 1) <= _iota2((TB, TB), 0), t, NEG)
    return qm, t


def _fox_fwd(qh, kh, kv, cq, ck, B, S, D, name):
    HP = D // LANES
    TB = _pick(S, 512)
    nb = S // TB

    def body(q_ref, k_ref, v_ref, cq_ref, ck_ref, o_ref, lse_ref, m_sc, l_sc, acc):
        qb, kb = pl.program_id(1), pl.program_id(3)

        @pl.when(kb == 0)
        def _():
            m_sc[...] = jnp.full_like(m_sc, NEG)
            l_sc[...] = jnp.zeros_like(l_sc)
            acc[...] = jnp.zeros_like(acc)

        def step(diagonal):
            q, k, v = q_ref[0], k_ref[0], v_ref[0].astype(BF16)
            cqv = cq_ref[0]
            a = acc[...]
            for h, hm in enumerate(_lane_masks()):
                _, t = _fox_scores(q, k, ck_ref, h, hm, diagonal)
                cqh = _head_col(cqv, h)
                m_prev = jnp.max(m_sc[h], axis=-1, keepdims=True)
                l_prev = jnp.max(l_sc[h], axis=-1, keepdims=True)
                m_new = jnp.maximum(m_prev, jnp.max(t, axis=-1, keepdims=True) + cqh)
                alpha = jnp.exp(m_prev - m_new)
                p = jnp.exp(t + (cqh - m_new))
                l_new = alpha * l_prev + jnp.sum(p, axis=-1, keepdims=True)
                a = jnp.where(hm, alpha * a + _dot(p.astype(BF16), v, NN), a)
                m_sc[h] = jnp.broadcast_to(m_new, (TB, LANES))
                l_sc[h] = jnp.broadcast_to(l_new, (TB, LANES))
            acc[...] = a

        pl.when(kb < qb)(functools.partial(step, False))
        pl.when(kb == qb)(functools.partial(step, True))

        @pl.when(kb == qb)
        def _():
            a = acc[...]
            out = a
            lse = jnp.zeros_like(a)
            for h, hm in enumerate(_lane_masks()):
                out = jnp.where(hm, a / l_sc[h], out)
                lse = jnp.where(hm, m_sc[h] + jnp.log(l_sc[h]), lse)
            o_ref[0] = out
            lse_ref[0] = lse

    blk = (1, TB, LANES)
    qmap = lambda b, i, h, j: (b, i, h)
    kmap = lambda b, i, h, j: (b, jnp.minimum(j, i), h)
    vmap = lambda b, i, h, j: (b, jnp.minimum(j, i), HP + h)
    cmap = lambda b, i, h, j: (b * HP + h, 0, jnp.minimum(j, i))
    out = jax.ShapeDtypeStruct((B, S, D), F32)
    o, lse = pl.pallas_call(
        body, name=name, grid=(B, nb, HP, nb),
        in_specs=[pl.BlockSpec(blk, qmap), pl.BlockSpec(blk, kmap), pl.BlockSpec(blk, vmap), pl.BlockSpec(blk, qmap),
                  pl.BlockSpec((1, 8, TB), cmap)],
        out_specs=[pl.BlockSpec(blk, qmap)] * 2, out_shape=[out, out],
        scratch_shapes=[pltpu.VMEM((2, TB, LANES), F32), pltpu.VMEM((2, TB, LANES), F32), pltpu.VMEM((TB, LANES), F32)],
        compiler_params=_params("parallel", "parallel", "parallel", "arbitrary"),
    )(qh.reshape(B, S, D), kh.reshape(B, S, D), kv.reshape(B, S, 2 * D), cq.reshape(B, S, D), ck)
    return o.reshape(B * S, D), lse.reshape(B * S, D)


def _fox_bwd_dq(qh, kh, kv, cq, ck, do, lse, delta, B, S, D, name):
    HP = D // LANES
    TB = _pick(S, 512)
    nb = S // TB

    def body(q_ref, k_ref, v_ref, cq_ref, ck_ref, do_ref, l_ref, d_ref, dq_ref, dc_ref, acc, acc_c):
        qb, kb = pl.program_id(1), pl.program_id(3)

        @pl.when(kb == 0)
        def _():
            acc[...] = jnp.zeros_like(acc)
            acc_c[...] = jnp.zeros_like(acc_c)

        def step(diagonal):
            q, k, v = q_ref[0], k_ref[0], v_ref[0].astype(BF16)
            dout = do_ref[0].astype(BF16)
            a, c = acc[...], acc_c[...]
            for h, hm in enumerate(_lane_masks()):
                _, t = _fox_scores(q, k, ck_ref, h, hm, diagonal)
                p = jnp.exp(t + (_head_col(cq_ref[0], h) - _head_col(l_ref[0], h)))
                dom = jnp.where(hm, dout, jnp.zeros_like(dout))
                ds = p * (_dot(dom, v, NT) - _head_col(d_ref[0], h))
                a += _dot(ds.astype(BF16), jnp.where(hm, k, jnp.zeros_like(k)), NN)
                c = jnp.where(hm, c + jnp.sum(ds, axis=-1, keepdims=True), c)
            acc[...] = a
            acc_c[...] = c

        pl.when(kb < qb)(functools.partial(step, False))
        pl.when(kb == qb)(functools.partial(step, True))

        @pl.when(kb == qb)
        def _():
            dq_ref[0] = acc[...]
            dc_ref[0] = acc_c[...]

    blk = (1, TB, LANES)
    qmap = lambda b, i, h, j: (b, i, h)
    kmap = lambda b, i, h, j: (b, jnp.minimum(j, i), h)
    vmap = lambda b, i, h, j: (b, jnp.minimum(j, i), HP + h)
    cmap = lambda b, i, h, j: (b * HP + h, 0, jnp.minimum(j, i))
    r3 = lambda t: t.reshape(B, S, D)
    out = jax.ShapeDtypeStruct((B, S, D), F32)
    dq, dc = pl.pallas_call(
        body, name=name, grid=(B, nb, HP, nb),
        in_specs=[pl.BlockSpec(blk, qmap), pl.BlockSpec(blk, kmap), pl.BlockSpec(blk, vmap), pl.BlockSpec(blk, qmap),
                  pl.BlockSpec((1, 8, TB), cmap), pl.BlockSpec(blk, qmap), pl.BlockSpec(blk, qmap), pl.BlockSpec(blk, qmap)],
        out_specs=[pl.BlockSpec(blk, qmap)] * 2, out_shape=[out, out],
        scratch_shapes=[pltpu.VMEM((TB, LANES), F32), pltpu.VMEM((TB, LANES), F32)],
        compiler_params=_params("parallel", "parallel", "parallel", "arbitrary"),
    )(r3(qh), r3(kh), kv.reshape(B, S, 2 * D), r3(cq), ck, r3(do), r3(lse), r3(delta))
    return dq.reshape(B * S, D), dc.reshape(B * S, D)


def _fox_bwd_dkv(qh, kh, kv, cq, ck, do, lse, delta, B, S, D, name):
    HP = D // LANES
    TB = _pick(S, 512)
    nb = S // TB

    def body(q_ref, k_ref, v_ref, cq_ref, ck_ref, do_ref, l_ref, d_ref, dk_ref, dv_ref, dc_ref, dk_acc, dv_acc, dc_acc):
        kb, qb = pl.program_id(1), pl.program_id(3)

        @pl.when(qb == 0)
        def _():
            dk_acc[...] = jnp.zeros_like(dk_acc)
            dv_acc[...] = jnp.zeros_like(dv_acc)
            dc_acc[...] = jnp.zeros_like(dc_acc)

        def step(diagonal):
            q, k, v = q_ref[0], k_ref[0], v_ref[0].astype(BF16)
            dout = do_ref[0].astype(BF16)
            dk, dv, dc = dk_acc[...], dv_acc[...], dc_acc[...]
            row = _iota2((8, TB), 0)
            for h, hm in enumerate(_lane_masks()):
                qm, t = _fox_scores(q, k, ck_ref, h, hm, diagonal)
                p = jnp.exp(t + (_head_col(cq_ref[0], h) - _head_col(l_ref[0], h)))
                dom = jnp.where(hm, dout, jnp.zeros_like(dout))
                ds = p * (_dot(dom, v, NT) - _head_col(d_ref[0], h))
                dv += _dot(p.astype(BF16), dom, TN)
                dk += _dot(ds.astype(BF16), qm, TN)
                dc = jnp.where(row == h, dc - jnp.sum(ds, axis=0, keepdims=True), dc)
            dk_acc[...] = dk
            dv_acc[...] = dv
            dc_acc[...] = dc

        pl.when(qb > kb)(functools.partial(step, False))
        pl.when(qb == kb)(functools.partial(step, True))

        @pl.when(qb == nb - 1)
        def _():
            dk_ref[0] = dk_acc[...]
            dv_ref[0] = dv_acc[...].astype(BF16)
            dc_ref[0] = dc_acc[...]

    blk = (1, TB, LANES)
    qmap = lambda b, j, h, i: (b, jnp.maximum(i, j), h)
    kmap = lambda b, j, h, i: (b, j, h)
    vmap = lambda b, j, h, i: (b, j, HP + h)
    cmap = lambda b, j, h, i: (b * HP + h, 0, j)
    r3 = lambda t: t.reshape(B, S, D)
    dk, dv, dc = pl.pallas_call(
        body, name=name, grid=(B, nb, HP, nb),
        in_specs=[pl.BlockSpec(blk, qmap), pl.BlockSpec(blk, kmap), pl.BlockSpec(blk, vmap), pl.BlockSpec(blk, qmap),
                  pl.BlockSpec((1, 8, TB), cmap), pl.BlockSpec(blk, qmap), pl.BlockSpec(blk, qmap), pl.BlockSpec(blk, qmap)],
        out_specs=[pl.BlockSpec(blk, kmap), pl.BlockSpec(blk, kmap), pl.BlockSpec((1, 8, TB), cmap)],
        out_shape=[jax.ShapeDtypeStruct((B, S, D), F32), jax.ShapeDtypeStruct((B, S, D), BF16),
                   jax.ShapeDtypeStruct((B * HP, 8, S), F32)],
        scratch_shapes=[pltpu.VMEM((TB, LANES), F32), pltpu.VMEM((TB, LANES), F32), pltpu.VMEM((8, TB), F32)],
        compiler_params=_params("parallel", "parallel", "parallel", "arbitrary"),
    )(r3(qh), r3(kh), kv.reshape(B, S, 2 * D), r3(cq), ck, r3(do), r3(lse), r3(delta))
    return dk.reshape(B * S, D), dv.reshape(B * S, D), dc


def _loss_and_grad(y, target, name):
    T, D = y.shape
    tm = _pick(T, 256, 8)

    def body(y_ref, t_ref, p_ref, dy_ref):
        e = y_ref[...] - t_ref[...]
        dy_ref[...] = e * (1.0 / D)
        p_ref[...] = jnp.sum(e * e, axis=0, keepdims=True)[None]

    blk = pl.BlockSpec((tm, D), lambda i: (i, 0))
    return pl.pallas_call(
        body, name=name, grid=(T // tm,), in_specs=[blk, blk],
        out_specs=[pl.BlockSpec((1, 1, D), lambda i: (i, 0, 0)), blk],
        out_shape=[jax.ShapeDtypeStruct((T // tm, 1, D), F32), jax.ShapeDtypeStruct((T, D), F32)],
        compiler_params=_params("parallel"),
    )(y, target)


def _pair_sum(x, got, name):
    _, R, C = x.shape
    tr = _pick(R, max(16, (PACK_ROWS * LANES) // C), 16)

    def body(c_ref, x_ref, g_ref, o_ref):
        o_ref[0] = (x_ref[0].astype(F32) + g_ref[0].astype(F32)).astype(o_ref.dtype)

    spec = pltpu.PrefetchScalarGridSpec(
        num_scalar_prefetch=1, grid=(4, R // tr),
        in_specs=[pl.BlockSpec((1, tr, C), lambda a, i, c: (2 * a + c[0], i, 0)),
                  pl.BlockSpec((1, tr, C), lambda a, i, c: (a, i, 0))],
        out_specs=pl.BlockSpec((1, tr, C), lambda a, i, c: (a, i, 0)))
    return pl.pallas_call(
        body, name=name, grid_spec=spec, out_shape=jax.ShapeDtypeStruct((4, R, C), x.dtype),
        compiler_params=_params("parallel", "parallel"),
    )(lax.axis_index("c").astype(jnp.int32).reshape(1), x, got)


def _adamw(parts, w, m, v, name):
    R, C = w.shape
    P = parts.shape[0]
    tr = _pick(R, max(16, (PACK_ROWS * LANES) // C), 16)

    def body(p_ref, w_ref, m_ref, v_ref, g_out, d_out, m_out, v_out):
        g = p_ref[0].astype(F32)
        for s in range(1, P):
            g = g + p_ref[s].astype(F32)
        mn = ADAM_B1 * m_ref[...] + (1.0 - ADAM_B1) * g
        vn = ADAM_B2 * v_ref[...] + (1.0 - ADAM_B2) * (g * g)
        m_hat = mn / (1.0 - ADAM_B1 ** ADAM_STEP)
        v_hat = vn / (1.0 - ADAM_B2 ** ADAM_STEP)
        g_out[...] = g
        d_out[...] = -ADAM_LR * (m_hat / (jnp.sqrt(v_hat) + ADAM_EPS) + ADAM_WD * w_ref[...])
        m_out[...] = mn
        v_out[...] = vn

    blk = pl.BlockSpec((tr, C), lambda i: (i, 0))
    out = jax.ShapeDtypeStruct((R, C), F32)
    return pl.pallas_call(
        body, name=name, grid=(R // tr,),
        in_specs=[pl.BlockSpec((P, tr, C), lambda i: (0, i, 0)), blk, blk, blk],
        out_specs=[blk] * 4, out_shape=[out] * 4, compiler_params=_params("parallel"),
    )(parts, w, m, v)


def _mesh_pos():
    return lax.axis_index("x"), lax.axis_index("y"), lax.axis_index("c")


def _comm_call(body, xs, out_shapes, name):
    n = len(xs)
    any_spec = pl.BlockSpec(memory_space=pl.ANY)
    return pl.pallas_call(
        body, name=name, out_shape=out_shapes, in_specs=[any_spec] * n, out_specs=[any_spec] * n,
        scratch_shapes=[pltpu.SemaphoreType.DMA((7, n)), pltpu.SemaphoreType.DMA((7, n)), pltpu.SemaphoreType.DMA((n,))],
    )(*xs)


def _all_gather(xs, name):
    n = len(xs)

    def body(*refs):
        x_refs, out_refs = refs[:n], refs[n:2 * n]
        send_sems, recv_sems, local_sems = refs[2 * n:]
        x_, y_, c_ = _mesh_pos()
        me, sibling = (x_, y_, c_), (x_, y_, 1 - c_)
        chips = [(1 - x_, y_), (x_, 1 - y_), (1 - x_, 1 - y_)]

        def slot(t, px, py, pc):
            return out_refs[t].at[4 * px + 2 * py + pc]

        def copy(t, k, block, to, src=None):
            return pltpu.make_async_remote_copy(
                src_ref=slot(t, *block) if src is None else src, dst_ref=slot(t, *block),
                send_sem=send_sems.at[k, t], recv_sem=recv_sems.at[k, t], device_id=to, device_id_type=MESH_ID)

        mine = [pltpu.make_async_copy(x_refs[t], slot(t, *me), local_sems.at[t]) for t in range(n)]
        first = [copy(t, 0, me, sibling, src=x_refs[t]) for t in range(n)]
        first += [copy(t, 1 + j, me, (*chip, c_), src=x_refs[t]) for j, chip in enumerate(chips) for t in range(n)]
        for cp in mine + first:
            cp.start()
        passed = []
        for j, chip in enumerate(chips):
            for t in range(n):
                copy(t, 1 + j, (*chip, c_), me).wait_recv()
                passed.append(copy(t, 4 + j, (*chip, c_), sibling))
                passed[-1].start()
        for t in range(n):
            copy(t, 0, sibling, me).wait_recv()
        for j, chip in enumerate(chips):
            for t in range(n):
                copy(t, 4 + j, (*chip, 1 - c_), me).wait_recv()
        for cp in first + passed:
            cp.wait_send()
        for cp in mine:
            cp.wait()

    return _comm_call(body, xs, [jax.ShapeDtypeStruct((N_DEV,) + x.shape, x.dtype) for x in xs], name)


def _pair_exchange(xs, name):
    n = len(xs)

    def body(*refs):
        x_refs, out_refs = refs[:n], refs[n:2 * n]
        send_sems, recv_sems, _ = refs[2 * n:]
        x_, y_, c_ = _mesh_pos()
        copies = [pltpu.make_async_remote_copy(
            src_ref=x_refs[t].at[2 * a + 1 - c_], dst_ref=out_refs[t].at[a],
            send_sem=send_sems.at[a, t], recv_sem=recv_sems.at[a, t],
            device_id=(x_, y_, 1 - c_), device_id_type=MESH_ID) for a in range(4) for t in range(n)]
        for cp in copies:
            cp.start()
        for cp in copies:
            cp.wait_recv()
        for cp in copies:
            cp.wait_send()

    return _comm_call(body, xs, [jax.ShapeDtypeStruct((4,) + x.shape[1:], x.dtype) for x in xs], name)


def _chip_exchange(xs, name):
    n = len(xs)

    def body(*refs):
        x_refs, out_refs = refs[:n], refs[n:2 * n]
        send_sems, recv_sems, local_sems = refs[2 * n:]
        x_, y_, c_ = _mesh_pos()
        chip = 2 * x_ + y_
        mine = [pltpu.make_async_copy(x_refs[t].at[chip], out_refs[t].at[chip], local_sems.at[t]) for t in range(n)]
        copies = []
        for k in range(1, 4):
            px, py = x_ ^ (k >> 1), y_ ^ (k & 1)
            for t in range(n):
                copies.append(pltpu.make_async_remote_copy(
                    src_ref=x_refs[t].at[2 * px + py], dst_ref=out_refs[t].at[chip],
                    send_sem=send_sems.at[k - 1, t], recv_sem=recv_sems.at[k - 1, t],
                    device_id=(px, py, c_), device_id_type=MESH_ID))
        for cp in mine + copies:
            cp.start()
        for cp in copies:
            cp.wait_recv()
        for cp in copies:
            cp.wait_send()
        for cp in mine:
            cp.wait()

    return _comm_call(body, xs, [jax.ShapeDtypeStruct(x.shape, x.dtype) for x in xs], name)


def _ffn_fwd(h, norm, w_in, w_out, tag):
    T, D = h.shape
    F = w_out.shape[0]
    hn = _rmsnorm_fwd(h, norm, f"{tag}_norm")
    gate, up, act = _mm_swiglu_fwd(hn, w_in, f"{tag}_in")
    out = _mm(act, w_out, "nn", M=T, N=D, K=F, res=h, scale=0.5, name=f"{tag}_out")
    return out, (h, hn, gate, up, act)


def _ffn_bwd(dout, saved, norm, w_in, w_out, tag):
    h, hn, gate, up, act = saved
    T, D = h.shape
    F = w_out.shape[0]
    dg, du = _mm_swiglu_bwd(dout, w_out, gate, up, f"{tag}_dact")
    dw_out = _mm(act, dout, "tn", M=F, N=D, K=T, scale=0.5, out_dtype=BF16, name=f"{tag}_dwout")
    dhn = _mm(dg, w_in, "nt", M=T, N=D, K=F, name=f"{tag}_dhn_g")
    dhn = _mm(du, w_in, "nt", M=T, N=D, K=F, b_off=(0, F), res=dhn, name=f"{tag}_dhn_u")
    dw_g = _mm(hn, dg, "tn", M=D, N=F, K=T, out_dtype=BF16, name=f"{tag}_dwin_g")
    dw_u = _mm(hn, du, "tn", M=D, N=F, K=T, out_dtype=BF16, name=f"{tag}_dwin_u")
    dh, dnorm = _rmsnorm_bwd(h, norm, dhn, dout, f"{tag}_dnorm")
    return dh, dnorm, jnp.concatenate([dw_g, dw_u], axis=1), dw_out


def _local_step(x, positions, target, W):
    B, S, D = x.shape
    T = B * S
    H, HP = D // HEAD_DIM, D // LANES
    G = {}
    h0 = x.reshape(T, D)
    cs = _rope_tables(positions, "rope_tables")

    h1, ffn00 = _ffn_fwd(h0, W["ffn_norm"][0, 0], W["ffn_w_in"][0, 0], W["ffn_w_out"][0, 0], "ffn00")
    hna = _rmsnorm_fwd(h1, W["mix_norm"][0], "mixa_norm")
    qkv = _mm(hna, W["a_w_qkv"], "nn", M=T, N=9 * D, K=D, name="a_qkv")
    qs, ks, os_, lses = [], [], [], []
    for g in range(3):
        qs.append(_qknorm_fwd(qkv, 3 * g * D, W["a_q_norm"][0, g], cs, D, F32, f"a_qnorm{g}", out_scale=SCALE))
        ks.append(_qknorm_fwd(qkv, (3 * g + 1) * D, W["a_k_norm"][0, g], cs, D, F32, f"a_knorm{g}"))
        o, l = _band_fwd(qs[g], ks[g], qkv, g, B, S, D, f"a_band{g}")
        os_.append(o)
        lses.append(l)
    mixed, lse_a = _mix(os_, lses, "a_mix")
    h2 = _mm(mixed, W["a_w_o"], "nn", M=T, N=D, K=D, res=h1, name="a_out")
    h3, ffn01 = _ffn_fwd(h2, W["ffn_norm"][0, 1], W["ffn_w_in"][0, 1], W["ffn_w_out"][0, 1], "ffn01")

    hnkv = _rmsnorm_fwd(h3, W["kv_norm"], "kv_norm")
    kv = _mm(hnkv, W["kv_w_kv"], "nn", M=T, N=2 * D, K=D, name="kv_proj")
    fpre = _mm(hnkv, W["kv_w_f"], "nn", M=T, N=LANES, K=D, name="kv_gate_proj")
    b_f = jnp.pad(W["kv_b_f"], (0, LANES - H)).reshape(1, LANES)
    kh = _qknorm_fwd(kv, 0, W["kv_k_norm"], None, D, BF16, "kv_knorm")
    cum, cq = _gate_fwd(fpre, b_f, B, S, D, "kv_gate")
    ck = cum.reshape(B, S, LANES)[:, :, :H].reshape(B, S, HP, 2).transpose(0, 2, 3, 1)
    ck = jnp.pad(ck, ((0, 0), (0, 0), (0, 6), (0, 0))).reshape(B * HP, 8, S)
    h4, ffn10 = _ffn_fwd(h3, W["ffn_norm"][1, 0], W["ffn_w_in"][1, 0], W["ffn_w_out"][1, 0], "ffn10")
    hnb = _rmsnorm_fwd(h4, W["mix_norm"][1], "mixb_norm")
    qraw = _mm(hnb, W["b_w_q"], "nn", M=T, N=D, K=D, name="b_q")
    qh = _qknorm_fwd(qraw, 0, W["b_q_norm"][0], None, D, BF16, "b_qnorm", out_scale=SCALE)
    ob, lse_b = _fox_fwd(qh, kh, kv, cq, ck, B, S, D, "b_fox")
    h5 = _mm(ob, W["b_w_o"], "nn", M=T, N=D, K=D, res=h4, name="b_out")
    h6, ffn11 = _ffn_fwd(h5, W["ffn_norm"][1, 1], W["ffn_w_in"][1, 1], W["ffn_w_out"][1, 1], "ffn11")

    sq, dy = _loss_and_grad(h6, target.reshape(T, D), "loss")

    dn = [[None, None], [None, None]]
    dwi = [[None, None], [None, None]]
    dwo = [[None, None], [None, None]]
    dh5, dn[1][1], dwi[1][1], dwo[1][1] = _ffn_bwd(dy, ffn11, W["ffn_norm"][1, 1], W["ffn_w_in"][1, 1], W["ffn_w_out"][1, 1], "ffn11")
    dob = _mm(dh5, W["b_w_o"], "nt", M=T, N=D, K=D, name="b_dout")
    G["b_w_o"] = _mm(ob, dh5, "tn", M=D, N=D, K=T, out_dtype=BF16, name="b_dwo")[None]
    delta_b = _head_rowsum(dob, ob, "b_delta")
    dqh, dcq = _fox_bwd_dq(qh, kh, kv, cq, ck, dob, lse_b, delta_b, B, S, D, "b_fox_dq")
    dkh, dv_b, dck = _fox_bwd_dkv(qh, kh, kv, cq, ck, dob, lse_b, delta_b, B, S, D, "b_fox_dkv")
    dqraw, dg = _qknorm_bwd(qraw, 0, W["b_q_norm"][0], None, dqh, D, "b_dqnorm", out_scale=SCALE)
    G["b_q_norm"] = dg[None]
    dhnb = _mm(dqraw, W["b_w_q"], "nt", M=T, N=D, K=D, name="b_dhn")
    G["b_w_q"] = _mm(hnb, dqraw, "tn", M=D, N=D, K=T, out_dtype=BF16, name="b_dwq")[None]
    dh4, dmix_b = _rmsnorm_bwd(h4, W["mix_norm"][1], dhnb, dh5, "mixb_dnorm")
    dh3, dn[1][0], dwi[1][0], dwo[1][0] = _ffn_bwd(dh4, ffn10, W["ffn_norm"][1, 0], W["ffn_w_in"][1, 0], W["ffn_w_out"][1, 0], "ffn10")

    dkraw, G["kv_k_norm"] = _qknorm_bwd(kv, 0, W["kv_k_norm"], None, dkh, D, "kv_dknorm")
    dcum = dck.reshape(B, HP, 8, S)[:, :, :2].transpose(0, 3, 1, 2).reshape(T, H)
    dcum = jnp.pad(dcum, ((0, 0), (0, LANES - H)))
    dcum_q = jnp.pad(dcq.reshape(T, H, HEAD_DIM)[:, :, 0], ((0, 0), (0, LANES - H)))
    dfpre, db_f = _gate_bwd(dcum, dcum_q, fpre, b_f, B, S, "kv_dgate")
    G["kv_b_f"] = db_f[0, :H]
    dhnkv = _mm(dkraw, W["kv_w_kv"], "nt", M=T, N=D, K=D, name="kv_dhn_k")
    dhnkv = _mm(dv_b, W["kv_w_kv"], "nt", M=T, N=D, K=D, b_off=(0, D), res=dhnkv, name="kv_dhn_v")
    dhnkv = _mm(dfpre, W["kv_w_f"], "nt", M=T, N=D, K=LANES, res=dhnkv, name="kv_dhn_f")
    dw_k = _mm(hnkv, dkraw, "tn", M=D, N=D, K=T, out_dtype=BF16, name="kv_dwk")
    dw_v = _mm(hnkv, dv_b, "tn", M=D, N=D, K=T, out_dtype=BF16, name="kv_dwv")
    dw_f = _mm(hnkv, dfpre, "tn", M=D, N=LANES, K=T, out_dtype=BF16, name="kv_dwf")
    G["kv_w"] = jnp.concatenate([dw_k, dw_v, dw_f[:, :H]], axis=1)
    dh3, G["kv_norm"] = _rmsnorm_bwd(h3, W["kv_norm"], dhnkv, dh3, "kv_dnorm")

    dh2, dn[0][1], dwi[0][1], dwo[0][1] = _ffn_bwd(dh3, ffn01, W["ffn_norm"][0, 1], W["ffn_w_in"][0, 1], W["ffn_w_out"][0, 1], "ffn01")
    dmixed = _mm(dh2, W["a_w_o"], "nt", M=T, N=D, K=D, name="a_dmixed")
    G["a_w_o"] = _mm(mixed, dh2, "tn", M=D, N=D, K=T, out_dtype=BF16, name="a_dwo")[None]
    delta_a = _head_rowsum(dmixed, mixed, "a_delta")
    dhna = None
    dw_qkv, dqn, dkn = [], [], []
    for g in range(3):
        dq, dk, dv = _band_bwd(qs[g], ks[g], qkv, dmixed, lse_a, delta_a, g, B, S, D, f"a_dband{g}")
        dqr, dgq = _qknorm_bwd(qkv, 3 * g * D, W["a_q_norm"][0, g], cs, dq, D, f"a_dqnorm{g}", out_scale=SCALE)
        dkr, dgk = _qknorm_bwd(qkv, (3 * g + 1) * D, W["a_k_norm"][0, g], cs, dk, D, f"a_dknorm{g}")
        dqn.append(dgq)
        dkn.append(dgk)
        for j, d in enumerate((dqr, dkr, dv)):
            c = 3 * g + j
            dhna = _mm(d, W["a_w_qkv"], "nt", M=T, N=D, K=D, b_off=(0, c * D), res=dhna, name=f"a_dhn{c}")
            dw_qkv.append(_mm(hna, d, "tn", M=D, N=D, K=T, out_dtype=BF16, name=f"a_dwqkv{c}"))
    G["a_w_qkv"] = jnp.concatenate(dw_qkv, axis=1)[None]
    G["a_q_norm"] = jnp.stack(dqn)[None]
    G["a_k_norm"] = jnp.stack(dkn)[None]
    dh1, dmix_a = _rmsnorm_bwd(h1, W["mix_norm"][0], dhna, dh2, "mixa_dnorm")
    dh0, dn[0][0], dwi[0][0], dwo[0][0] = _ffn_bwd(dh1, ffn00, W["ffn_norm"][0, 0], W["ffn_w_in"][0, 0], W["ffn_w_out"][0, 0], "ffn00")

    G["mix_norm"] = jnp.stack([dmix_a, dmix_b])
    G["ffn_norm"] = jnp.stack([jnp.stack(r) for r in dn])
    G["ffn_w_in"] = jnp.stack([jnp.stack(r) for r in dwi])
    G["ffn_w_out"] = jnp.stack([jnp.stack(r) for r in dwo])
    return jnp.sum(sq), dh0.reshape(B, S, D), G


SMALL =["mix_norm", "kv_norm", "a_q_norm", "a_k_norm", "kv_b_f", "kv_k_norm", "b_q_norm"]


def _pack(blocks, lead):
    flat = [b.reshape(b.shape[:lead] + (-1,)) for b in blocks]
    flat = jnp.concatenate(flat, axis=lead)
    n = flat.shape[-1]
    unit = PACK_ROWS * LANES
    padded = -(-n // unit) * unit if n > unit else -(-n // (16 * LANES)) * 16 * LANES
    flat = jnp.pad(flat, [(0, 0)] * lead + [(0, padded - n)])
    return flat.reshape(flat.shape[:lead] + (padded // LANES, LANES))


def _unpack(packed, shapes, lead):
    flat = packed.reshape(packed.shape[:lead] + (-1,))
    out, off = [], 0
    for shp in shapes:
        n = int(np.prod(shp))
        out.append(flat[..., off:off + n].reshape(packed.shape[:lead] + tuple(shp)))
        off += n
    return out


def kernel(x, positions, ffn_norm, ffn_w_in, ffn_w_out, mix_norm, a_w_qkv, a_q_norm, a_k_norm, a_w_o, kv_norm, kv_w, kv_b_f, kv_k_norm, b_w_q, b_q_norm, b_w_o, loss_target, m_ffn_norm, m_ffn_w_in, m_ffn_w_out, m_mix_norm, m_a_w_qkv, m_a_q_norm, m_a_k_norm, m_a_w_o, m_kv_norm, m_kv_w, m_kv_b_f, m_kv_k_norm, m_b_w_q, m_b_q_norm, m_b_w_o, v_ffn_norm, v_ffn_w_in, v_ffn_w_out, v_mix_norm, v_a_w_qkv, v_a_q_norm, v_a_k_norm, v_a_w_o, v_kv_norm, v_kv_w, v_kv_b_f, v_kv_k_norm, v_b_w_q, v_b_q_norm, v_b_w_o):
    names = ["ffn_norm", "ffn_w_in", "ffn_w_out", "mix_norm", "a_w_qkv", "a_q_norm", "a_k_norm", "a_w_o", "kv_norm",
             "kv_w", "kv_b_f", "kv_k_norm", "b_w_q", "b_q_norm", "b_w_o"]
    w = dict(zip(names, (ffn_norm, ffn_w_in, ffn_w_out, mix_norm, a_w_qkv, a_q_norm, a_k_norm, a_w_o, kv_norm, kv_w,
                         kv_b_f, kv_k_norm, b_w_q, b_q_norm, b_w_o)))
    m = dict(zip(names, (m_ffn_norm, m_ffn_w_in, m_ffn_w_out, m_mix_norm, m_a_w_qkv, m_a_q_norm, m_a_k_norm, m_a_w_o,
                         m_kv_norm, m_kv_w, m_kv_b_f, m_kv_k_norm, m_b_w_q, m_b_q_norm, m_b_w_o)))
    v = dict(zip(names, (v_ffn_norm, v_ffn_w_in, v_ffn_w_out, v_mix_norm, v_a_w_qkv, v_a_q_norm, v_a_k_norm, v_a_w_o,
                         v_kv_norm, v_kv_w, v_kv_b_f, v_kv_k_norm, v_b_w_q, v_b_q_norm, v_b_w_o)))
    B, S, D = x.shape
    H = D // HEAD_DIM
    F = ffn_w_out.shape[2] * N_DEV
    Ds, Fs = D // N_DEV, F // N_DEV
    me = 4 * lax.axis_index("x") + 2 * lax.axis_index("y") + lax.axis_index("c")
    squares = ("a_w_o", "b_w_q", "b_w_o")

    def shard_views(t):
        return [t["ffn_w_in"].reshape(4 * D, -1), t["ffn_w_out"].reshape(-1, D), t["a_w_qkv"][0], t["kv_w"],
                jnp.concatenate([t[n][0] for n in squares], axis=0)]

    def to_shards(views):
        sq3 = views[4].reshape(3, 1, Ds, D)
        out = {"ffn_w_in": views[0].reshape(ffn_w_in.shape), "ffn_w_out": views[1].reshape(ffn_w_out.shape),
               "a_w_qkv": views[2][None], "kv_w": views[3]}
        out.update({n: sq3[i] for i, n in enumerate(squares)})
        return out

    norm_rows = jnp.pad(ffn_norm.reshape(4, Ds), ((0, 4), (0, 0)))
    g_in, g_out, g_qkv, g_kv, g_sq, g_norm = _all_gather(
        [s.astype(BF16) for s in shard_views(w)] + [norm_rows], "gather_weights")
    W = {n: w[n] for n in SMALL}
    W["ffn_norm"] = g_norm[:, :4].transpose(1, 0, 2).reshape(2, 2, D)
    W["ffn_w_in"] = g_in.transpose(1, 0, 2).reshape(2, 2, D, 2 * F)
    W["ffn_w_out"] = g_out.reshape(N_DEV, 4, Fs, D).transpose(1, 0, 2, 3).reshape(2, 2, F, D)
    W["a_w_qkv"] = g_qkv.transpose(1, 0, 2).reshape(D, 9 * D)
    kv_full = g_kv.transpose(1, 0, 2).reshape(D, 2 * D + H)
    W["kv_w_kv"] = kv_full[:, :2 * D]
    W["kv_w_f"] = jnp.pad(kv_full[:, 2 * D:], ((0, 0), (0, LANES - H)))
    sq_full = g_sq.reshape(N_DEV, 3, Ds, D).transpose(1, 0, 2, 3).reshape(3, D, D)
    W["a_w_o"], W["b_w_q"], W["b_w_o"] = sq_full[0], sq_full[1], sq_full[2]

    sq, grad_x, G = _local_step(x, positions, loss_target, W)
    loss = lax.psum(sq * (0.5 / D), ("x", "y", "c"))

    blocks = [G["ffn_w_in"].reshape(4 * D, N_DEV, -1).transpose(1, 0, 2),
              G["ffn_w_out"].reshape(4, N_DEV, Fs, D).transpose(1, 0, 2, 3).reshape(N_DEV, 4 * Fs, D),
              G["a_w_qkv"][0].reshape(D, N_DEV, -1).transpose(1, 0, 2),
              G["kv_w"].reshape(D, N_DEV, -1).transpose(1, 0, 2),
              jnp.concatenate([G[n][0].reshape(N_DEV, Ds, D) for n in squares], axis=1)]
    blocks = [b.astype(BF16) for b in blocks]
    got = _pair_exchange(blocks, "exchange_pair")
    parts = _chip_exchange([_pair_sum(b, s, f"pair_sum{i}") for i, (b, s) in enumerate(zip(blocks, got))], "exchange_chips")
    views = [shard_views(t) for t in (w, m, v)]
    outs = [_adamw(parts[i], views[0][i], views[1][i], views[2][i], f"adamw_matrix{i}") for i in range(len(parts))]
    new = {}
    for k, kind in enumerate(("grad", "delta", "new_m", "new_v")):
        for n, arr in to_shards([o[k] for o in outs]).items():
            new[kind, n] = arr

    small_all = ["ffn_norm"] + SMALL
    vec_parts = _all_gather([_pack([G[n] for n in small_all], 0)], "gather_vector_grads")[0]
    vec_parts = _unpack(vec_parts, [G[n].shape for n in small_all], 1)
    norm_g = lax.dynamic_slice_in_dim(vec_parts[0], me * ffn_norm.shape[2], ffn_norm.shape[2], axis=3)
    rep_g = _pack(vec_parts[1:], 1)
    for ns, parts_s in ((["ffn_norm"], _pack([norm_g], 1)), (SMALL, rep_g)):
        outs = _adamw(parts_s, *(_pack([t[n] for n in ns], 0) for t in (w, m, v)), "adamw_" + ns[0])
        for kind, packed in zip(("grad", "delta", "new_m", "new_v"), outs):
            for n, arr in zip(ns, _unpack(packed, [w[n].shape for n in ns], 0)):
                new[kind, n] = arr

    result = [loss, grad_x]
    for kind in ("grad", "delta", "new_m", "new_v"):
        result += [new[kind, n] for n in names]
    return tuple(result)
```

```python
import functools

import jax
import jax.numpy as jnp
import numpy as np
from jax import lax
from jax.experimental import pallas as pl
from jax.experimental.pallas import tpu as pltpu

F32 = jnp.float32
BF16 = jnp.bfloat16
HIGHEST = lax.Precision.HIGHEST

HEAD_DIM = 64
ROT_DIM = HEAD_DIM // 4
ROPE_THETA = 500000.0
DILATED_GROUPS = ((128, 1), (512, 4), (2048, 16))
BAND_BLOCK = 128
EPS = 1e-6
ADAM_LR, ADAM_B1, ADAM_B2, ADAM_EPS, ADAM_WD, ADAM_STEP = 0.001, 0.9, 0.999, 1e-08, 0.01, 10

LANES = 128
N_DEV = 8
VMEM_LIMIT = 56 * 1024 * 1024
NEG = -1e30
SCALE = HEAD_DIM ** -0.5
PACK_ROWS = 1024
MESH_ID = pl.DeviceIdType.MESH


def _params(*sem):
    return pltpu.CompilerParams(dimension_semantics=sem, vmem_limit_bytes=VMEM_LIMIT)


def _divs(n, cap, mult=LANES):
    d = [t for t in range(mult, min(n, cap) + 1, mult) if n % t == 0]
    return d or [n]


def _pick(n, cap, mult=LANES):
    return _divs(n, cap, mult)[-1]


def _mm_tiles(M, N, K, budget=1408 * 1024):
    best = None
    for tm in _divs(M, 1408):
        for tn in _divs(N, 1408):
            if tm * tn <= budget and (best is None or tm * tn > best[0] * best[1]):
                best = (tm, tn)
    if best is None:
        best = (_divs(M, 1408)[0], _divs(N, 1408)[0])
    tm, tn = best
    tks = [t for t in _divs(K, 2048) if (tm + tn) * t <= 3 * 1024 * 1024]
    tk = tks[-1] if tks else _divs(K, 2048)[0]
    return tm, tn, tk


def _dot(a, b, dims, precision=None):
    return lax.dot_general(a, b, (dims, ((), ())), precision=precision, preferred_element_type=F32)


NN = ((1,), (0,))
NT = ((1,), (1,))
TN = ((0,), (0,))


def _dot_split(x, mat):
    hi = x.astype(BF16)
    lo = (x - hi.astype(F32)).astype(BF16)
    m = mat.astype(BF16)
    return _dot(hi, m, NN) + _dot(lo, m, NN)


def _dot_once(x, mat):
    return _dot(x.astype(BF16), mat.astype(BF16), NN)


def _mm(a, b, mode, *, M, N, K, a_off=(0, 0), b_off=(0, 0), res=None, scale=1.0, out_dtype=F32, name):
    tm, tn, tk = _mm_tiles(M, N, K)
    nk = K // tk
    if mode == "tn":
        a_blk, b_blk, dims = (tk, tm), (tk, tn), TN
    elif mode == "nt":
        a_blk, b_blk, dims = (tm, tk), (tn, tk), NT
    else:
        a_blk, b_blk, dims = (tm, tk), (tk, tn), NN
    for off, blk in ((a_off, a_blk), (b_off, b_blk)):
        assert off[0] % blk[0] == 0 and off[1] % blk[1] == 0, (name, off, blk)
    ao = (a_off[0] // a_blk[0], a_off[1] // a_blk[1])
    bo = (b_off[0] // b_blk[0], b_off[1] // b_blk[1])
    if mode == "tn":
        a_map = lambda i, j, k: (k + ao[0], i + ao[1])
        b_map = lambda i, j, k: (k + bo[0], j + bo[1])
    elif mode == "nt":
        a_map = lambda i, j, k: (i + ao[0], k + ao[1])
        b_map = lambda i, j, k: (j + bo[0], k + bo[1])
    else:
        a_map = lambda i, j, k: (i + ao[0], k + ao[1])
        b_map = lambda i, j, k: (k + bo[0], j + bo[1])
    has_res = res is not None

    def body(*refs):
        if has_res:
            a_ref, b_ref, r_ref, o_ref, acc = refs
        else:
            a_ref, b_ref, o_ref, acc = refs
        k = pl.program_id(2)

        @pl.when(k == 0)
        def _():
            acc[...] = jnp.zeros_like(acc)

        acc[...] += _dot(a_ref[...].astype(BF16), b_ref[...].astype(BF16), dims)

        @pl.when(k == nk - 1)
        def _():
            out = acc[...] * scale
            if has_res:
                out = r_ref[...] + out
            o_ref[...] = out.astype(out_dtype)

    in_specs = [pl.BlockSpec(a_blk, a_map), pl.BlockSpec(b_blk, b_map)]
    args = [a, b]
    if has_res:
        in_specs.append(pl.BlockSpec((tm, tn), lambda i, j, k: (i, j)))
        args.append(res)
    return pl.pallas_call(
        body, name=name, grid=(M // tm, N // tn, nk), in_specs=in_specs,
        out_specs=pl.BlockSpec((tm, tn), lambda i, j, k: (i, j)),
        out_shape=jax.ShapeDtypeStruct((M, N), out_dtype),
        scratch_shapes=[pltpu.VMEM((tm, tn), F32)],
        compiler_params=_params("parallel", "parallel", "arbitrary"),
    )(*args)


def _rmsnorm_fwd(x, g, name):
    T, D = x.shape
    tm = _pick(T, 256, 8)

    def body(x_ref, g_ref, y_ref):
        xv = x_ref[...]
        r = lax.rsqrt(jnp.mean(xv * xv, axis=-1, keepdims=True) + EPS)
        y_ref[...] = (xv * r * g_ref[...]).astype(BF16)

    return pl.pallas_call(
        body, name=name, grid=(T // tm,),
        in_specs=[pl.BlockSpec((tm, D), lambda i: (i, 0)), pl.BlockSpec((1, D), lambda i: (0, 0))],
        out_specs=pl.BlockSpec((tm, D), lambda i: (i, 0)),
        out_shape=jax.ShapeDtypeStruct((T, D), BF16), compiler_params=_params("parallel"),
    )(x, g.reshape(1, D))


def _rmsnorm_bwd(x, g, dy, dres, name):
    T, D = x.shape
    tm = _pick(T, 256, 8)

    def body(x_ref, g_ref, dy_ref, dres_ref, dx_ref, dg_ref):
        xv = x_ref[...]
        r = lax.rsqrt(jnp.mean(xv * xv, axis=-1, keepdims=True) + EPS)
        xn = xv * r
        d = dy_ref[...]
        dgy = d * g_ref[...]
        dx_ref[...] = dres_ref[...] + r * (dgy - xn * jnp.mean(dgy * xn, axis=-1, keepdims=True))

        @pl.when(pl.program_id(0) == 0)
        def _():
            dg_ref[...] = jnp.zeros_like(dg_ref)

        dg_ref[...] += jnp.sum(d * xn, axis=0, keepdims=True)

    row = pl.BlockSpec((tm, D), lambda i: (i, 0))
    vec = pl.BlockSpec((1, D), lambda i: (0, 0))
    dx, dg = pl.pallas_call(
        body, name=name, grid=(T // tm,), in_specs=[row, vec, row, row], out_specs=[row, vec],
        out_shape=[jax.ShapeDtypeStruct((T, D), F32), jax.ShapeDtypeStruct((1, D), F32)],
        compiler_params=_params("arbitrary"),
    )(x, g.reshape(1, D), dy, dres)
    return dx, dg.reshape(D)


def _sigmoid(x):
    return 1.0 / (1.0 + jnp.exp(-x))


def _mm_swiglu_fwd(hn, w_in, name):
    T, D = hn.shape
    F = w_in.shape[1] // 2
    tm, tn, _ = _mm_tiles(T, F, D, budget=768 * 1024)
    tk = [t for t in _divs(D, 2048) if (tm + 2 * tn) * t <= 3 * 1024 * 1024][-1]
    nk, nf = D // tk, F // tn

    def body(a_ref, bg_ref, bu_ref, g_ref, u_ref, act_ref, acc_g, acc_u):
        k = pl.program_id(2)

        @pl.when(k == 0)
        def _():
            acc_g[...] = jnp.zeros_like(acc_g)
            acc_u[...] = jnp.zeros_like(acc_u)

        a = a_ref[...].astype(BF16)
        acc_g[...] += _dot(a, bg_ref[...].astype(BF16), NN)
        acc_u[...] += _dot(a, bu_ref[...].astype(BF16), NN)

        @pl.when(k == nk - 1)
        def _():
            gv, uv = acc_g[...], acc_u[...]
            g_ref[...] = gv
            u_ref[...] = uv
            act_ref[...] = (gv * _sigmoid(gv) * uv).astype(BF16)

    tile = pl.BlockSpec((tm, tn), lambda i, j, k: (i, j))
    f32 = jax.ShapeDtypeStruct((T, F), F32)
    return pl.pallas_call(
        body, name=name, grid=(T // tm, nf, nk),
        in_specs=[pl.BlockSpec((tm, tk), lambda i, j, k: (i, k)), pl.BlockSpec((tk, tn), lambda i, j, k: (k, j)),
                  pl.BlockSpec((tk, tn), lambda i, j, k: (k, j + nf))],
        out_specs=[tile, tile, tile], out_shape=[f32, f32, jax.ShapeDtypeStruct((T, F), BF16)],
        scratch_shapes=[pltpu.VMEM((tm, tn), F32), pltpu.VMEM((tm, tn), F32)],
        compiler_params=_params("parallel", "parallel", "arbitrary"),
    )(hn, w_in, w_in)


def _mm_swiglu_bwd(dout, w_out, gate, up, name):
    T, D = dout.shape
    F = w_out.shape[0]
    tm, tn, tk = _mm_tiles(T, F, D, budget=768 * 1024)
    nk = D // tk

    def body(a_ref, b_ref, g_ref, u_ref, dg_ref, du_ref, acc):
        k = pl.program_id(2)

        @pl.when(k == 0)
        def _():
            acc[...] = jnp.zeros_like(acc)

        acc[...] += _dot(a_ref[...].astype(BF16), b_ref[...].astype(BF16), NT)

        @pl.when(k == nk - 1)
        def _():
            d = acc[...] * 0.5
            gv, uv = g_ref[...], u_ref[...]
            s = _sigmoid(gv)
            dg_ref[...] = (d * uv * (s + gv * s * (1.0 - s))).astype(BF16)
            du_ref[...] = (d * gv * s).astype(BF16)

    tile = pl.BlockSpec((tm, tn), lambda i, j, k: (i, j))
    out = jax.ShapeDtypeStruct((T, F), BF16)
    return pl.pallas_call(
        body, name=name, grid=(T // tm, F // tn, nk),
        in_specs=[pl.BlockSpec((tm, tk), lambda i, j, k: (i, k)), pl.BlockSpec((tn, tk), lambda i, j, k: (j, k)), tile, tile],
        out_specs=[tile, tile], out_shape=[out, out], scratch_shapes=[pltpu.VMEM((tm, tn), F32)],
        compiler_params=_params("parallel", "parallel", "arbitrary"),
    )(dout, w_out, gate, up)


def _iota2(shape, dim):
    return lax.broadcasted_iota(jnp.int32, shape, dim)


def _head_ones():
    r, c = _iota2((LANES, LANES), 0), _iota2((LANES, LANES), 1)
    return ((r >> 6) == (c >> 6)).astype(F32)


def _rot_matrix(transpose):
    r, c = _iota2((LANES, LANES), 0), _iota2((LANES, LANES), 1)
    if transpose:
        r, c = c, r
    half = ROT_DIM // 2
    cm = c & (HEAD_DIM - 1)
    neg = (r == c + half) & (cm < half)
    pos = (r == c - half) & (cm >= half) & (cm < ROT_DIM)
    return pos.astype(F32) - neg.astype(F32)


def _rope_tables(positions, name):
    T = positions.size
    tm = _pick(T, 1024, 8)
    inv = ROPE_THETA ** (-jnp.arange(0, ROT_DIM, 2, dtype=F32) / ROT_DIM)
    lane = jnp.concatenate([inv, inv, jnp.zeros((HEAD_DIM - ROT_DIM,), F32)])
    lane = jnp.tile(lane, 2).reshape(1, LANES)

    def body(p_ref, f_ref, c_ref, s_ref):
        ang = p_ref[...].astype(F32) * f_ref[...]
        c_ref[...] = jnp.cos(ang)
        s_ref[...] = jnp.sin(ang)

    out = jax.ShapeDtypeStruct((T, LANES), F32)
    blk = pl.BlockSpec((tm, LANES), lambda i: (i, 0))
    return pl.pallas_call(
        body, name=name, grid=(T // tm,),
        in_specs=[pl.BlockSpec((tm, 1), lambda i: (i, 0)), pl.BlockSpec((1, LANES), lambda i: (0, 0))],
        out_specs=[blk, blk], out_shape=[out, out], compiler_params=_params("parallel"),
    )(positions.reshape(T, 1), lane)


def _qknorm_fwd(src, col_off, gain, cs, D, out_dtype, name, out_scale=1.0):
    T = src.shape[0]
    HP = D // LANES
    tm = _pick(T, 1024, 8)
    co = col_off // LANES
    rope = cs is not None

    def body(*refs):
        if rope:
            x_ref, g_ref, c_ref, s_ref, o_ref = refs
        else:
            x_ref, g_ref, o_ref = refs
        xv = x_ref[...]
        ms = _dot_once(xv * xv, _head_ones()) * (1.0 / HEAD_DIM)
        y = xv * lax.rsqrt(ms + EPS) * g_ref[...]
        if rope:
            y = y * c_ref[...] + _dot_once(y, _rot_matrix(False)) * s_ref[...]
        o_ref[...] = (y * out_scale).astype(out_dtype)

    in_specs = [pl.BlockSpec((tm, LANES), lambda i, h: (i, co + h)), pl.BlockSpec((1, LANES), lambda i, h: (0, 0))]
    args = [src, jnp.tile(gain.reshape(1, HEAD_DIM), (1, 2))]
    if rope:
        in_specs += [pl.BlockSpec((tm, LANES), lambda i, h: (i, 0))] * 2
        args += list(cs)
    return pl.pallas_call(
        body, name=name, grid=(T // tm, HP), in_specs=in_specs,
        out_specs=pl.BlockSpec((tm, LANES), lambda i, h: (i, h)),
        out_shape=jax.ShapeDtypeStruct((T, D), out_dtype), compiler_params=_params("parallel", "parallel"),
    )(*args)


def _qknorm_bwd(src, col_off, gain, cs, dout, D, name, out_scale=1.0):
    T = src.shape[0]
    HP = D // LANES
    tm = _pick(T, 1024, 8)
    co = col_off // LANES
    rope = cs is not None

    def body(*refs):
        if rope:
            x_ref, g_ref, d_ref, c_ref, s_ref, dx_ref, dg_ref = refs
        else:
            x_ref, g_ref, d_ref, dx_ref, dg_ref = refs
        xv = x_ref[...]
        ones = _head_ones()
        ms = _dot_once(xv * xv, ones) * (1.0 / HEAD_DIM)
        r = lax.rsqrt(ms + EPS)
        xn = xv * r
        d = d_ref[...] * out_scale
        if rope:
            d = d * c_ref[...] + _dot_once(d * s_ref[...], _rot_matrix(True))
        dgy = d * g_ref[...]
        mean = _dot_once(dgy * xn, ones) * (1.0 / HEAD_DIM)
        dx_ref[...] = (r * (dgy - xn * mean)).astype(BF16)

        @pl.when(pl.program_id(1) == 0)
        def _():
            dg_ref[...] = jnp.zeros_like(dg_ref)

        dg_ref[...] += jnp.sum(d * xn, axis=0, keepdims=True)[None]

    in_specs = [pl.BlockSpec((tm, LANES), lambda h, i: (i, co + h)), pl.BlockSpec((1, LANES), lambda h, i: (0, 0)),
                pl.BlockSpec((tm, LANES), lambda h, i: (i, h))]
    args = [src, jnp.tile(gain.reshape(1, HEAD_DIM), (1, 2)), dout]
    if rope:
        in_specs += [pl.BlockSpec((tm, LANES), lambda h, i: (i, 0))] * 2
        args += list(cs)
    dx, dg = pl.pallas_call(
        body, name=name, grid=(HP, T // tm), in_specs=in_specs,
        out_specs=[pl.BlockSpec((tm, LANES), lambda h, i: (i, h)), pl.BlockSpec((1, 1, LANES), lambda h, i: (h, 0, 0))],
        out_shape=[jax.ShapeDtypeStruct((T, D), BF16), jax.ShapeDtypeStruct((HP, 1, LANES), F32)],
        compiler_params=_params("parallel", "arbitrary"),
    )(*args)
    return dx, dg.reshape(2 * HP, HEAD_DIM).sum(axis=0)


def _head_rowsum(a, b, name):
    T, D = a.shape
    tm = _pick(T, 1024, 8)

    def body(a_ref, b_ref, o_ref):
        o_ref[...] = _dot_split(a_ref[...] * b_ref[...], _head_ones())

    blk = pl.BlockSpec((tm, LANES), lambda i, h: (i, h))
    return pl.pallas_call(
        body, name=name, grid=(T // tm, D // LANES), in_specs=[blk, blk], out_specs=blk,
        out_shape=jax.ShapeDtypeStruct((T, D), F32), compiler_params=_params("parallel", "parallel"),
    )(a, b)


def _lane_masks():
    lane = _iota2((1, LANES), 1)
    return (lane < HEAD_DIM, lane >= HEAD_DIM)


BAND_GROUP = 4


def _band_geometry(g, S):
    window, dil = DILATED_GROUPS[g]
    chunk = BAND_BLOCK * dil
    assert window // dil == BAND_BLOCK and S % chunk == 0
    if dil == 1:
        assert (S // chunk) % BAND_GROUP == 0
        return dil, chunk, BAND_GROUP, 1
    assert dil % BAND_GROUP == 0
    return dil, chunk, 1, dil // BAND_GROUP


def _band_rows(i, rho, dil, shift=0):
    if dil > 1:
        return pl.ds(rho * BAND_GROUP + i, BAND_BLOCK, stride=dil), shift != 0
    j = i + shift
    outside = j < 0 or j >= BAND_GROUP
    return pl.ds(0 if outside else j * BAND_BLOCK, BAND_BLOCK), outside


def _stack_heads(x):
    hm0, hm1 = _lane_masks()
    zero = jnp.zeros_like(x)
    return jnp.concatenate([jnp.where(hm0, x, zero), jnp.where(hm1, x, zero)], axis=0)


def _unstack_heads(x2):
    hm0, _ = _lane_masks()
    return jnp.where(hm0, x2[:BAND_BLOCK], x2[BAND_BLOCK:])


def _band_masks():
    qi = _iota2((2 * BAND_BLOCK, 2 * BAND_BLOCK), 0) & (BAND_BLOCK - 1)
    kj = _iota2((2 * BAND_BLOCK, 2 * BAND_BLOCK), 1)
    own = (kj >= BAND_BLOCK) & (kj - BAND_BLOCK <= qi)
    prev = (kj < BAND_BLOCK) & (kj >= qi)
    return own, prev


def _band_fwd(qh, kh, qkv, g, B, S, D, name):
    dil, CH, NB, RG = _band_geometry(g, S)
    HP = D // LANES
    vo = (3 * g + 2) * HP
    nsp = S // (CH * NB)

    def body(q_ref, k_ref, kp_ref, v_ref, vp_ref, o_ref, lse_ref):
        n, rho = pl.program_id(1), pl.program_id(3)
        own, prev = _band_masks()
        for i in range(BAND_GROUP):
            rows, _ = _band_rows(i, rho, dil)
            prows, outside = _band_rows(i, rho, dil, shift=-1)
            q, kc, vc = (t[0, rows, :].astype(BF16) for t in (q_ref, k_ref, v_ref))
            kp, vp = (t[0, prows, :].astype(BF16) for t in ((kp_ref, vp_ref) if outside else (k_ref, v_ref)))
            valid = (own | (prev & (n > 0))) if outside else (own | prev)
            s = jnp.where(valid, _dot(_stack_heads(q), jnp.concatenate([kp, kc], axis=0), NT), NEG)
            m = jnp.max(s, axis=-1, keepdims=True)
            e = jnp.exp(s - m)
            l = jnp.sum(e, axis=-1, keepdims=True)
            o2 = _dot((e * (1.0 / l)).astype(BF16), jnp.concatenate([vp, vc], axis=0), NN)
            o_ref[0, rows, :] = _unstack_heads(o2)
            lse_ref[0, rows, :] = _unstack_heads(m + jnp.log(l))

    span, chunk = (1, CH * NB, LANES), (1, CH, LANES)
    cur = lambda b, n, h, r: (b, n, h)
    prev = lambda b, n, h, r: (b, jnp.maximum(n * NB - 1, 0), h)
    vcur = lambda b, n, h, r: (b, n, vo + h)
    vprev = lambda b, n, h, r: (b, jnp.maximum(n * NB - 1, 0), vo + h)
    out = jax.ShapeDtypeStruct((B, S, D), F32)
    q3, k3, v3 = qh.reshape(B, S, D), kh.reshape(B, S, D), qkv.reshape(B, S, 9 * D)
    o, lse = pl.pallas_call(
        body, name=name, grid=(B, nsp, HP, RG),
        in_specs=[pl.BlockSpec(span, cur), pl.BlockSpec(span, cur), pl.BlockSpec(chunk, prev),
                  pl.BlockSpec(span, vcur), pl.BlockSpec(chunk, vprev)],
        out_specs=[pl.BlockSpec(span, cur), pl.BlockSpec(span, cur)], out_shape=[out, out],
        compiler_params=_params("parallel", "parallel", "parallel", "arbitrary"),
    )(q3, k3, k3, v3, v3)
    return o.reshape(B * S, D), lse.reshape(B * S, D)


def _head_col(x, h):
    lane = _iota2((1, LANES), 1)
    return jnp.sum(jnp.where(lane == h * HEAD_DIM, x, 0.0), axis=-1, keepdims=True)


def _band_bwd(qh, kh, qkv, do, lse, delta, g, B, S, D, name):
    dil, CH, NB, RG = _band_geometry(g, S)
    HP = D // LANES
    vo = (3 * g + 2) * HP
    nsp, nch = S // (CH * NB), S // CH

    def stacked_cols(tile):
        return jnp.concatenate([_head_col(tile, 0), _head_col(tile, 1)], axis=0)

    def body(q_ref, qn_ref, do_ref, don_ref, l_ref, ln_ref, d_ref, dn_ref, k_ref, kp_ref, v_ref, vp_ref,
             dq_ref, dk_ref, dv_ref):
        n, rho = pl.program_id(1), pl.program_id(3)
        own, prev = _band_masks()
        band = (_iota2((2 * BAND_BLOCK, BAND_BLOCK), 1) >= (_iota2((2 * BAND_BLOCK, BAND_BLOCK), 0) & (BAND_BLOCK - 1)))
        for i in range(BAND_GROUP):
            rows, _ = _band_rows(i, rho, dil)
            prows, p_out = _band_rows(i, rho, dil, shift=-1)
            nrows, n_out = _band_rows(i, rho, dil, shift=1)
            kc, vc = (t[0, rows, :].astype(BF16) for t in (k_ref, v_ref))
            kp, vp = (t[0, prows, :].astype(BF16) for t in ((kp_ref, vp_ref) if p_out else (k_ref, v_ref)))
            q2, do2 = (_stack_heads(t[0, rows, :].astype(BF16)) for t in (q_ref, do_ref))
            qn2, don2 = (_stack_heads(t[0, nrows, :].astype(BF16)) for t in ((qn_ref, don_ref) if n_out else (q_ref, do_ref)))
            lse2, del2 = (stacked_cols(t[0, rows, :]) for t in (l_ref, d_ref))
            lsen2, deln2 = (stacked_cols(t[0, nrows, :]) for t in ((ln_ref, dn_ref) if n_out else (l_ref, d_ref)))
            valid = (own | (prev & (n > 0))) if p_out else (own | prev)
            next_ok = (band & (n < nsp - 1)) if n_out else band
            k2, v2 = jnp.concatenate([kp, kc], axis=0), jnp.concatenate([vp, vc], axis=0)
            p = jnp.exp(jnp.where(valid, _dot(q2, k2, NT) - lse2, NEG))
            ds = (p * (_dot(do2, v2, NT) - del2)).astype(BF16)
            pc = jnp.exp(jnp.where(next_ok, _dot(qn2, kc, NT) - lsen2, NEG))
            dsc = (pc * (_dot(don2, vc, NT) - deln2)).astype(BF16)
            dq_ref[0, rows, :] = _unstack_heads(_dot(ds, k2, NN))
            dk_ref[0, rows, :] = _dot(ds[:, BAND_BLOCK:], q2, TN) + _dot(dsc, qn2, TN)
            dv_ref[0, rows, :] = _dot(p[:, BAND_BLOCK:].astype(BF16), do2, TN) + _dot(pc.astype(BF16), don2, TN)

    span, chunk = (1, CH * NB, LANES), (1, CH, LANES)
    cur = lambda b, n, h, r: (b, n, h)
    prev = lambda b, n, h, r: (b, jnp.maximum(n * NB - 1, 0), h)
    nxt = lambda b, n, h, r: (b, jnp.minimum((n + 1) * NB, nch - 1), h)
    vcur = lambda b, n, h, r: (b, n, vo + h)
    vprev = lambda b, n, h, r: (b, jnp.maximum(n * NB - 1, 0), vo + h)
    q3, k3, do3, l3, d3 = (t.reshape(B, S, D) for t in (qh, kh, do, lse, delta))
    v3 = qkv.reshape(B, S, 9 * D)
    sp, ch = (lambda m: pl.BlockSpec(span, m)), (lambda m: pl.BlockSpec(chunk, m))
    out = jax.ShapeDtypeStruct((B, S, D), F32)
    dq, dk, dv = pl.pallas_call(
        body, name=name, grid=(B, nsp, HP, RG),
        in_specs=[sp(cur), ch(nxt), sp(cur), ch(nxt), sp(cur), ch(nxt), sp(cur), ch(nxt),
                  sp(cur), ch(prev), sp(vcur), ch(vprev)],
        out_specs=[sp(cur)] * 3, out_shape=[out] * 3,
        compiler_params=_params("parallel", "parallel", "parallel", "arbitrary"),
    )(q3, q3, do3, do3, l3, l3, d3, d3, k3, k3, v3, v3)
    return dq.reshape(B * S, D), dk.reshape(B * S, D), dv.reshape(B * S, D)


def _mix(os_, lses, name):
    T, D = os_[0].shape
    tm = _pick(T, 256, 8)

    def body(o0, o1, o2, l0, l1, l2, mix_ref, lse_ref):
        a, b, c = l0[...], l1[...], l2[...]
        m = jnp.maximum(jnp.maximum(a, b), c)
        ea, eb, ec = jnp.exp(a - m), jnp.exp(b - m), jnp.exp(c - m)
        s = ea + eb + ec
        mix_ref[...] = (ea / s) * o0[...] + (eb / s) * o1[...] + (ec / s) * o2[...]
        lse_ref[...] = m + jnp.log(s)

    blk = pl.BlockSpec((tm, D), lambda i: (i, 0))
    out = jax.ShapeDtypeStruct((T, D), F32)
    return pl.pallas_call(
        body, name=name, grid=(T // tm,), in_specs=[blk] * 6, out_specs=[blk, blk], out_shape=[out, out],
        compiler_params=_params("parallel"),
    )(*os_, *lses)


def _gate_fwd(fpre, b_f, B, S, D, name):
    nb = S // LANES

    def body(f_ref, b_ref, cum_ref, rep_ref, carry):
        @pl.when(pl.program_id(1) == 0)
        def _():
            carry[...] = jnp.zeros_like(carry)

        z = f_ref[...] + b_ref[...]
        lf = jnp.minimum(z, 0.0) - jnp.log(1.0 + jnp.exp(-jnp.abs(z)))
        tri = (_iota2((LANES, LANES), 0) >= _iota2((LANES, LANES), 1)).astype(F32)
        c = _dot(tri, lf, NN, HIGHEST) + carry[...]
        cum_ref[...] = c
        carry[...] += jnp.sum(lf, axis=0, keepdims=True)
        spread = ((_iota2((LANES, D), 1) >> 6) == _iota2((LANES, D), 0)).astype(F32)
        rep_ref[...] = _dot(c, spread, NN, HIGHEST)

    return pl.pallas_call(
        body, name=name, grid=(B, nb),
        in_specs=[pl.BlockSpec((LANES, LANES), lambda b, j: (b * nb + j, 0)), pl.BlockSpec((1, LANES), lambda b, j: (0, 0))],
        out_specs=[pl.BlockSpec((LANES, LANES), lambda b, j: (b * nb + j, 0)), pl.BlockSpec((LANES, D), lambda b, j: (b * nb + j, 0))],
        out_shape=[jax.ShapeDtypeStruct((B * S, LANES), F32), jax.ShapeDtypeStruct((B * S, D), F32)],
        scratch_shapes=[pltpu.VMEM((1, LANES), F32)], compiler_params=_params("arbitrary", "arbitrary"),
    )(fpre, b_f)


def _gate_bwd(dcum_k, dcum_q, fpre, b_f, B, S, name):
    nb = S // LANES

    def body(d_ref, dq_ref, f_ref, b_ref, o_ref, db_ref, carry):
        first = (pl.program_id(0) == 0) & (pl.program_id(1) == 0)

        @pl.when(pl.program_id(1) == 0)
        def _():
            carry[...] = jnp.zeros_like(carry)

        @pl.when(first)
        def _():
            db_ref[...] = jnp.zeros_like(db_ref)

        d = d_ref[...] + dq_ref[...]
        tri = (_iota2((LANES, LANES), 0) <= _iota2((LANES, LANES), 1)).astype(F32)
        dlf = _dot(tri, d, NN, HIGHEST) + carry[...]
        carry[...] += jnp.sum(d, axis=0, keepdims=True)
        z = f_ref[...] + b_ref[...]
        dz = dlf * (1.0 / (1.0 + jnp.exp(z)))
        o_ref[...] = dz.astype(BF16)
        db_ref[...] += jnp.sum(dz, axis=0, keepdims=True)

    rev = lambda b, j: (b * nb + nb - 1 - j, 0)
    blk = pl.BlockSpec((LANES, LANES), rev)
    vec = pl.BlockSpec((1, LANES), lambda b, j: (0, 0))
    return pl.pallas_call(
        body, name=name, grid=(B, nb), in_specs=[blk, blk, blk, vec], out_specs=[blk, vec],
        out_shape=[jax.ShapeDtypeStruct((B * S, LANES), BF16), jax.ShapeDtypeStruct((1, LANES), F32)],
        scratch_shapes=[pltpu.VMEM((1, LANES), F32)], compiler_params=_params("arbitrary", "arbitrary"),
    )(dcum_k, dcum_q, fpre, b_f)


def _fox_scores(q, k, ck_ref, h, hm, diagonal):
    TB = q.shape[0]
    qm = jnp.where(hm, q, jnp.zeros_like(q))
    t = _dot(qm, k, NT) - ck_ref[0, h:h + 1, :]
    if diagonal:
        t = jnp.where(_iota2((TB, TB), 1) <= _iota2((TB, TB), 0), t, NEG)
    return qm, t


FOX_BLOCK = 512
EXP_UNDERFLOW = 104.0


def _fox_live_blocks(cum, gain_q, gain_k, B, S, D):
    H, HP = D // HEAD_DIM, D // LANES
    nb = S // FOX_BLOCK
    bound = 1.02 * HEAD_DIM * SCALE * jnp.max(jnp.abs(gain_q)) * jnp.max(jnp.abs(gain_k))
    c = cum.reshape(B, S, LANES)[:, :, :H]
    decay = c[:, 0::FOX_BLOCK, None, :] - c[:, None, FOX_BLOCK - 1::FOX_BLOCK, :]
    below = jnp.arange(nb)[None, :] < jnp.arange(nb)[:, None]
    dead = (decay < -(EXP_UNDERFLOW + 2.0 * bound)) & below[None, :, :, None]
    live = jnp.logical_not(dead.reshape(B, nb, nb, HP, 2).all(axis=-1))
    first = jnp.argmax(live, axis=2)
    last = nb - 1 - jnp.argmax(live[:, ::-1], axis=1)
    to_table = lambda t: t.transpose(0, 2, 1).reshape(B * HP * nb).astype(jnp.int32)
    return to_table(first), to_table(last)


def _fox_fwd(qh, kh, kv, cq, ck, first, B, S, D, name):
    HP = D // LANES
    TB = FOX_BLOCK
    nb = S // TB

    def body(first_ref, q_ref, k_ref, v_ref, cq_ref, ck_ref, o_ref, lse_ref, m_sc, l_sc, acc):
        qb, j = pl.program_id(1), pl.program_id(3)
        kb = first_ref[(pl.program_id(0) * HP + pl.program_id(2)) * nb + qb] + j

        @pl.when(j == 0)
        def _():
            m_sc[...] = jnp.full_like(m_sc, NEG)
            l_sc[...] = jnp.zeros_like(l_sc)
            acc[...] = jnp.zeros_like(acc)

        def step(diagonal):
            q, k, v = q_ref[0], k_ref[0], v_ref[0].astype(BF16)
            cqv = cq_ref[0]
            a = acc[...]
            for h, hm in enumerate(_lane_masks()):
                _, t = _fox_scores(q, k, ck_ref, h, hm, diagonal)
                cqh = _head_col(cqv, h)
                m_prev = jnp.max(m_sc[h], axis=-1, keepdims=True)
                l_prev = jnp.max(l_sc[h], axis=-1, keepdims=True)
                m_new = jnp.maximum(m_prev, jnp.max(t, axis=-1, keepdims=True) + cqh)
                alpha = jnp.exp(m_prev - m_new)
                p = jnp.exp(t + (cqh - m_new))
                l_new = alpha * l_prev + jnp.sum(p, axis=-1, keepdims=True)
                a = jnp.where(hm, alpha * a + _dot(p.astype(BF16), v, NN), a)
                m_sc[h] = jnp.broadcast_to(m_new, (TB, LANES))
                l_sc[h] = jnp.broadcast_to(l_new, (TB, LANES))
            acc[...] = a

        pl.when(kb < qb)(functools.partial(step, False))
        pl.when(kb == qb)(functools.partial(step, True))

        @pl.when(kb == qb)
        def _():
            a = acc[...]
            out = a
            lse = jnp.zeros_like(a)
            for h, hm in enumerate(_lane_masks()):
                out = jnp.where(hm, a / l_sc[h], out)
                lse = jnp.where(hm, m_sc[h] + jnp.log(l_sc[h]), lse)
            o_ref[0] = out
            lse_ref[0] = lse

    blk = (1, TB, LANES)
    qmap, kmap, vmap, cmap = _fox_query_major_maps(HP, nb)
    out = jax.ShapeDtypeStruct((B, S, D), F32)
    spec = pltpu.PrefetchScalarGridSpec(
        num_scalar_prefetch=1, grid=(B, nb, HP, nb),
        in_specs=[pl.BlockSpec(blk, qmap), pl.BlockSpec(blk, kmap), pl.BlockSpec(blk, vmap), pl.BlockSpec(blk, qmap),
                  pl.BlockSpec((1, 8, TB), cmap)],
        out_specs=[pl.BlockSpec(blk, qmap)] * 2,
        scratch_shapes=[pltpu.VMEM((2, TB, LANES), F32), pltpu.VMEM((2, TB, LANES), F32), pltpu.VMEM((TB, LANES), F32)])
    o, lse = pl.pallas_call(
        body, name=name, grid_spec=spec, out_shape=[out, out],
        compiler_params=_params("parallel", "parallel", "parallel", "arbitrary"),
    )(first, qh.reshape(B, S, D), kh.reshape(B, S, D), kv.reshape(B, S, 2 * D), cq.reshape(B, S, D), ck)
    return o.reshape(B * S, D), lse.reshape(B * S, D)


def _fox_query_major_maps(HP, nb):
    def kblock(b, i, h, j, first):
        return jnp.minimum(first[(b * HP + h) * nb + i] + j, i)

    qmap = lambda b, i, h, j, first: (b, i, h)
    kmap = lambda b, i, h, j, first: (b, kblock(b, i, h, j, first), h)
    vmap = lambda b, i, h, j, first: (b, kblock(b, i, h, j, first), HP + h)
    cmap = lambda b, i, h, j, first: (b * HP + h, 0, kblock(b, i, h, j, first))
    return qmap, kmap, vmap, cmap


def _fox_bwd_dq(qh, kh, kv, cq, ck, first, do, lse, delta, B, S, D, name):
    HP = D // LANES
    TB = FOX_BLOCK
    nb = S // TB

    def body(first_ref, q_ref, k_ref, v_ref, cq_ref, ck_ref, do_ref, l_ref, d_ref, dq_ref, dc_ref, acc, acc_c):
        qb, j = pl.program_id(1), pl.program_id(3)
        kb = first_ref[(pl.program_id(0) * HP + pl.program_id(2)) * nb + qb] + j

        @pl.when(j == 0)
        def _():
            acc[...] = jnp.zeros_like(acc)
            acc_c[...] = jnp.zeros_like(acc_c)

        def step(diagonal):
            q, k, v = q_ref[0], k_ref[0], v_ref[0].astype(BF16)
            dout = do_ref[0].astype(BF16)
            a, c = acc[...], acc_c[...]
            for h, hm in enumerate(_lane_masks()):
                _, t = _fox_scores(q, k, ck_ref, h, hm, diagonal)
                p = jnp.exp(t + (_head_col(cq_ref[0], h) - _head_col(l_ref[0], h)))
                dom = jnp.where(hm, dout, jnp.zeros_like(dout))
                ds = p * (_dot(dom, v, NT) - _head_col(d_ref[0], h))
                a += _dot(ds.astype(BF16), jnp.where(hm, k, jnp.zeros_like(k)), NN)
                c = jnp.where(hm, c + jnp.sum(ds, axis=-1, keepdims=True), c)
            acc[...] = a
            acc_c[...] = c

        pl.when(kb < qb)(functools.partial(step, False))
        pl.when(kb == qb)(functools.partial(step, True))

        @pl.when(kb == qb)
        def _():
            dq_ref[0] = acc[...]
            dc_ref[0] = acc_c[...]

    blk = (1, TB, LANES)
    qmap, kmap, vmap, cmap = _fox_query_major_maps(HP, nb)
    r3 = lambda t: t.reshape(B, S, D)
    out = jax.ShapeDtypeStruct((B, S, D), F32)
    spec = pltpu.PrefetchScalarGridSpec(
        num_scalar_prefetch=1, grid=(B, nb, HP, nb),
        in_specs=[pl.BlockSpec(blk, qmap), pl.BlockSpec(blk, kmap), pl.BlockSpec(blk, vmap), pl.BlockSpec(blk, qmap),
                  pl.BlockSpec((1, 8, TB), cmap), pl.BlockSpec(blk, qmap), pl.BlockSpec(blk, qmap), pl.BlockSpec(blk, qmap)],
        out_specs=[pl.BlockSpec(blk, qmap)] * 2,
        scratch_shapes=[pltpu.VMEM((TB, LANES), F32), pltpu.VMEM((TB, LANES), F32)])
    dq, dc = pl.pallas_call(
        body, name=name, grid_spec=spec, out_shape=[out, out],
        compiler_params=_params("parallel", "parallel", "parallel", "arbitrary"),
    )(first, r3(qh), r3(kh), kv.reshape(B, S, 2 * D), r3(cq), ck, r3(do), r3(lse), r3(delta))
    return dq.reshape(B * S, D), dc.reshape(B * S, D)


def _fox_bwd_dkv(qh, kh, kv, cq, ck, last, do, lse, delta, B, S, D, name):
    HP = D // LANES
    TB = FOX_BLOCK
    nb = S // TB

    def body(last_ref, q_ref, k_ref, v_ref, cq_ref, ck_ref, do_ref, l_ref, d_ref, dk_ref, dv_ref, dc_ref,
             dk_acc, dv_acc, dc_acc):
        kb, j = pl.program_id(1), pl.program_id(3)
        qb = kb + j
        qb_last = last_ref[(pl.program_id(0) * HP + pl.program_id(2)) * nb + kb]

        @pl.when(j == 0)
        def _():
            dk_acc[...] = jnp.zeros_like(dk_acc)
            dv_acc[...] = jnp.zeros_like(dv_acc)
            dc_acc[...] = jnp.zeros_like(dc_acc)

        def step(diagonal):
            q, k, v = q_ref[0], k_ref[0], v_ref[0].astype(BF16)
            dout = do_ref[0].astype(BF16)
            dk, dv, dc = dk_acc[...], dv_acc[...], dc_acc[...]
            row = _iota2((8, TB), 0)
            for h, hm in enumerate(_lane_masks()):
                qm, t = _fox_scores(q, k, ck_ref, h, hm, diagonal)
                p = jnp.exp(t + (_head_col(cq_ref[0], h) - _head_col(l_ref[0], h)))
                dom = jnp.where(hm, dout, jnp.zeros_like(dout))
                ds = p * (_dot(dom, v, NT) - _head_col(d_ref[0], h))
                dv += _dot(p.astype(BF16), dom, TN)
                dk += _dot(ds.astype(BF16), qm, TN)
                dc = jnp.where(row == h, dc - jnp.sum(ds, axis=0, keepdims=True), dc)
            dk_acc[...] = dk
            dv_acc[...] = dv
            dc_acc[...] = dc

        pl.when((j > 0) & (qb <= qb_last))(functools.partial(step, False))
        pl.when(j == 0)(functools.partial(step, True))

        @pl.when(qb == qb_last)
        def _():
            dk_ref[0] = dk_acc[...]
            dv_ref[0] = dv_acc[...].astype(BF16)
            dc_ref[0] = dc_acc[...]

    def qblock(b, i, h, j, last):
        return jnp.minimum(i + j, last[(b * HP + h) * nb + i])

    blk = (1, TB, LANES)
    qmap = lambda b, i, h, j, last: (b, qblock(b, i, h, j, last), h)
    kmap = lambda b, i, h, j, last: (b, i, h)
    vmap = lambda b, i, h, j, last: (b, i, HP + h)
    cmap = lambda b, i, h, j, last: (b * HP + h, 0, i)
    r3 = lambda t: t.reshape(B, S, D)
    spec = pltpu.PrefetchScalarGridSpec(
        num_scalar_prefetch=1, grid=(B, nb, HP, nb),
        in_specs=[pl.BlockSpec(blk, qmap), pl.BlockSpec(blk, kmap), pl.BlockSpec(blk, vmap), pl.BlockSpec(blk, qmap),
                  pl.BlockSpec((1, 8, TB), cmap), pl.BlockSpec(blk, qmap), pl.BlockSpec(blk, qmap), pl.BlockSpec(blk, qmap)],
        out_specs=[pl.BlockSpec(blk, kmap), pl.BlockSpec(blk, kmap), pl.BlockSpec((1, 8, TB), cmap)],
        scratch_shapes=[pltpu.VMEM((TB, LANES), F32), pltpu.VMEM((TB, LANES), F32), pltpu.VMEM((8, TB), F32)])
    dk, dv, dc = pl.pallas_call(
        body, name=name, grid_spec=spec,
        out_shape=[jax.ShapeDtypeStruct((B, S, D), F32), jax.ShapeDtypeStruct((B, S, D), BF16),
                   jax.ShapeDtypeStruct((B * HP, 8, S), F32)],
        compiler_params=_params("parallel", "parallel", "parallel", "arbitrary"),
    )(last, r3(qh), r3(kh), kv.reshape(B, S, 2 * D), r3(cq), ck, r3(do), r3(lse), r3(delta))
    return dk.reshape(B * S, D), dv.reshape(B * S, D), dc


def _loss_and_grad(y, target, name):
    T, D = y.shape
    tm = _pick(T, 256, 8)

    def body(y_ref, t_ref, p_ref, dy_ref):
        e = y_ref[...] - t_ref[...]
        dy_ref[...] = e * (1.0 / D)
        p_ref[...] = jnp.sum(e * e, axis=0, keepdims=True)[None]

    blk = pl.BlockSpec((tm, D), lambda i: (i, 0))
    return pl.pallas_call(
        body, name=name, grid=(T // tm,), in_specs=[blk, blk],
        out_specs=[pl.BlockSpec((1, 1, D), lambda i: (i, 0, 0)), blk],
        out_shape=[jax.ShapeDtypeStruct((T // tm, 1, D), F32), jax.ShapeDtypeStruct((T, D), F32)],
        compiler_params=_params("parallel"),
    )(y, target)


def _pair_sum(x, got, name):
    _, R, C = x.shape
    tr = _pick(R, max(16, (PACK_ROWS * LANES) // C), 16)

    def body(c_ref, x_ref, g_ref, o_ref):
        o_ref[0] = (x_ref[0].astype(F32) + g_ref[0].astype(F32)).astype(o_ref.dtype)

    spec = pltpu.PrefetchScalarGridSpec(
        num_scalar_prefetch=1, grid=(4, R // tr),
        in_specs=[pl.BlockSpec((1, tr, C), lambda a, i, c: (2 * a + c[0], i, 0)),
                  pl.BlockSpec((1, tr, C), lambda a, i, c: (a, i, 0))],
        out_specs=pl.BlockSpec((1, tr, C), lambda a, i, c: (a, i, 0)))
    return pl.pallas_call(
        body, name=name, grid_spec=spec, out_shape=jax.ShapeDtypeStruct((4, R, C), x.dtype),
        compiler_params=_params("parallel", "parallel"),
    )(lax.axis_index("c").astype(jnp.int32).reshape(1), x, got)


def _adamw(parts, w, m, v, name):
    R, C = w.shape
    P = parts.shape[0]
    tr = _pick(R, max(16, (PACK_ROWS * LANES) // C), 16)

    def body(p_ref, w_ref, m_ref, v_ref, g_out, d_out, m_out, v_out):
        g = p_ref[0].astype(F32)
        for s in range(1, P):
            g = g + p_ref[s].astype(F32)
        mn = ADAM_B1 * m_ref[...] + (1.0 - ADAM_B1) * g
        vn = ADAM_B2 * v_ref[...] + (1.0 - ADAM_B2) * (g * g)
        m_hat = mn / (1.0 - ADAM_B1 ** ADAM_STEP)
        v_hat = vn / (1.0 - ADAM_B2 ** ADAM_STEP)
        g_out[...] = g
        d_out[...] = -ADAM_LR * (m_hat / (jnp.sqrt(v_hat) + ADAM_EPS) + ADAM_WD * w_ref[...])
        m_out[...] = mn
        v_out[...] = vn

    blk = pl.BlockSpec((tr, C), lambda i: (i, 0))
    out = jax.ShapeDtypeStruct((R, C), F32)
    return pl.pallas_call(
        body, name=name, grid=(R // tr,),
        in_specs=[pl.BlockSpec((P, tr, C), lambda i: (0, i, 0)), blk, blk, blk],
        out_specs=[blk] * 4, out_shape=[out] * 4, compiler_params=_params("parallel"),
    )(parts, w, m, v)


def _mesh_pos():
    return lax.axis_index("x"), lax.axis_index("y"), lax.axis_index("c")


def _comm_call(body, xs, out_shapes, name):
    n = len(xs)
    any_spec = pl.BlockSpec(memory_space=pl.ANY)
    return pl.pallas_call(
        body, name=name, out_shape=out_shapes, in_specs=[any_spec] * n, out_specs=[any_spec] * n,
        scratch_shapes=[pltpu.SemaphoreType.DMA((7, n)), pltpu.SemaphoreType.DMA((7, n)), pltpu.SemaphoreType.DMA((n,))],
    )(*xs)


def _all_gather(xs, name):
    n = len(xs)

    def body(*refs):
        x_refs, out_refs = refs[:n], refs[n:2 * n]
        send_sems, recv_sems, local_sems = refs[2 * n:]
        x_, y_, c_ = _mesh_pos()
        me, sibling = (x_, y_, c_), (x_, y_, 1 - c_)
        chips = [(1 - x_, y_), (x_, 1 - y_), (1 - x_, 1 - y_)]

        def slot(t, px, py, pc):
            return out_refs[t].at[4 * px + 2 * py + pc]

        def copy(t, k, block, to, src=None):
            return pltpu.make_async_remote_copy(
                src_ref=slot(t, *block) if src is None else src, dst_ref=slot(t, *block),
                send_sem=send_sems.at[k, t], recv_sem=recv_sems.at[k, t], device_id=to, device_id_type=MESH_ID)

        mine = [pltpu.make_async_copy(x_refs[t], slot(t, *me), local_sems.at[t]) for t in range(n)]
        first = [copy(t, 0, me, sibling, src=x_refs[t]) for t in range(n)]
        first += [copy(t, 1 + j, me, (*chip, c_), src=x_refs[t]) for j, chip in enumerate(chips) for t in range(n)]
        for cp in mine + first:
            cp.start()
        passed = []
        for j, chip in enumerate(chips):
            for t in range(n):
                copy(t, 1 + j, (*chip, c_), me).wait_recv()
                passed.append(copy(t, 4 + j, (*chip, c_), sibling))
                passed[-1].start()
        for t in range(n):
            copy(t, 0, sibling, me).wait_recv()
        for j, chip in enumerate(chips):
            for t in range(n):
                copy(t, 4 + j, (*chip, 1 - c_), me).wait_recv()
        for cp in first + passed:
            cp.wait_send()
        for cp in mine:
            cp.wait()

    return _comm_call(body, xs, [jax.ShapeDtypeStruct((N_DEV,) + x.shape, x.dtype) for x in xs], name)


def _pair_exchange(xs, name):
    n = len(xs)

    def body(*refs):
        x_refs, out_refs = refs[:n], refs[n:2 * n]
        send_sems, recv_sems, _ = refs[2 * n:]
        x_, y_, c_ = _mesh_pos()
        copies = [pltpu.make_async_remote_copy(
            src_ref=x_refs[t].at[2 * a + 1 - c_], dst_ref=out_refs[t].at[a],
            send_sem=send_sems.at[a, t], recv_sem=recv_sems.at[a, t],
            device_id=(x_, y_, 1 - c_), device_id_type=MESH_ID) for a in range(4) for t in range(n)]
        for cp in copies:
            cp.start()
        for cp in copies:
            cp.wait_recv()
        for cp in copies:
            cp.wait_send()

    return _comm_call(body, xs, [jax.ShapeDtypeStruct((4,) + x.shape[1:], x.dtype) for x in xs], name)


def _chip_exchange(xs, name):
    n = len(xs)

    def body(*refs):
        x_refs, out_refs = refs[:n], refs[n:2 * n]
        send_sems, recv_sems, local_sems = refs[2 * n:]
        x_, y_, c_ = _mesh_pos()
        chip = 2 * x_ + y_
        mine = [pltpu.make_async_copy(x_refs[t].at[chip], out_refs[t].at[chip], local_sems.at[t]) for t in range(n)]
        copies = []
        for k in range(1, 4):
            px, py = x_ ^ (k >> 1), y_ ^ (k & 1)
            for t in range(n):
                copies.append(pltpu.make_async_remote_copy(
                    src_ref=x_refs[t].at[2 * px + py], dst_ref=out_refs[t].at[chip],
                    send_sem=send_sems.at[k - 1, t], recv_sem=recv_sems.at[k - 1, t],
                    device_id=(px, py, c_), device_id_type=MESH_ID))
        for cp in mine + copies:
            cp.start()
        for cp in copies:
            cp.wait_recv()
        for cp in copies:
            cp.wait_send()
        for cp in mine:
            cp.wait()

    return _comm_call(body, xs, [jax.ShapeDtypeStruct(x.shape, x.dtype) for x in xs], name)


def _ffn_fwd(h, norm, w_in, w_out, tag):
    T, D = h.shape
    F = w_out.shape[0]
    hn = _rmsnorm_fwd(h, norm, f"{tag}_norm")
    gate, up, act = _mm_swiglu_fwd(hn, w_in, f"{tag}_in")
    out = _mm(act, w_out, "nn", M=T, N=D, K=F, res=h, scale=0.5, name=f"{tag}_out")
    return out, (h, hn, gate, up, act)


def _ffn_bwd(dout, saved, norm, w_in, w_out, tag):
    h, hn, gate, up, act = saved
    T, D = h.shape
    F = w_out.shape[0]
    dg, du = _mm_swiglu_bwd(dout, w_out, gate, up, f"{tag}_dact")
    dw_out = _mm(act, dout, "tn", M=F, N=D, K=T, scale=0.5, out_dtype=BF16, name=f"{tag}_dwout")
    dhn = _mm(dg, w_in, "nt", M=T, N=D, K=F, name=f"{tag}_dhn_g")
    dhn = _mm(du, w_in, "nt", M=T, N=D, K=F, b_off=(0, F), res=dhn, name=f"{tag}_dhn_u")
    dw_g = _mm(hn, dg, "tn", M=D, N=F, K=T, out_dtype=BF16, name=f"{tag}_dwin_g")
    dw_u = _mm(hn, du, "tn", M=D, N=F, K=T, out_dtype=BF16, name=f"{tag}_dwin_u")
    dh, dnorm = _rmsnorm_bwd(h, norm, dhn, dout, f"{tag}_dnorm")
    return dh, dnorm, jnp.concatenate([dw_g, dw_u], axis=1), dw_out


def _local_step(x, positions, target, W):
    B, S, D = x.shape
    T = B * S
    H, HP = D // HEAD_DIM, D // LANES
    G = {}
    h0 = x.reshape(T, D)
    cs = _rope_tables(positions, "rope_tables")

    h1, ffn00 = _ffn_fwd(h0, W["ffn_norm"][0, 0], W["ffn_w_in"][0, 0], W["ffn_w_out"][0, 0], "ffn00")
    hna = _rmsnorm_fwd(h1, W["mix_norm"][0], "mixa_norm")
    qkv = _mm(hna, W["a_w_qkv"], "nn", M=T, N=9 * D, K=D, name="a_qkv")
    qs, ks, os_, lses = [], [], [], []
    for g in range(3):
        qs.append(_qknorm_fwd(qkv, 3 * g * D, W["a_q_norm"][0, g], cs, D, F32, f"a_qnorm{g}", out_scale=SCALE))
        ks.append(_qknorm_fwd(qkv, (3 * g + 1) * D, W["a_k_norm"][0, g], cs, D, F32, f"a_knorm{g}"))
        o, l = _band_fwd(qs[g], ks[g], qkv, g, B, S, D, f"a_band{g}")
        os_.append(o)
        lses.append(l)
    mixed, lse_a = _mix(os_, lses, "a_mix")
    h2 = _mm(mixed, W["a_w_o"], "nn", M=T, N=D, K=D, res=h1, name="a_out")
    h3, ffn01 = _ffn_fwd(h2, W["ffn_norm"][0, 1], W["ffn_w_in"][0, 1], W["ffn_w_out"][0, 1], "ffn01")

    hnkv = _rmsnorm_fwd(h3, W["kv_norm"], "kv_norm")
    kv = _mm(hnkv, W["kv_w_kv"], "nn", M=T, N=2 * D, K=D, name="kv_proj")
    fpre = _mm(hnkv, W["kv_w_f"], "nn", M=T, N=LANES, K=D, name="kv_gate_proj")
    b_f = jnp.pad(W["kv_b_f"], (0, LANES - H)).reshape(1, LANES)
    kh = _qknorm_fwd(kv, 0, W["kv_k_norm"], None, D, BF16, "kv_knorm")
    cum, cq = _gate_fwd(fpre, b_f, B, S, D, "kv_gate")
    ck = cum.reshape(B, S, LANES)[:, :, :H].reshape(B, S, HP, 2).transpose(0, 2, 3, 1)
    ck = jnp.pad(ck, ((0, 0), (0, 0), (0, 6), (0, 0))).reshape(B * HP, 8, S)
    h4, ffn10 = _ffn_fwd(h3, W["ffn_norm"][1, 0], W["ffn_w_in"][1, 0], W["ffn_w_out"][1, 0], "ffn10")
    hnb = _rmsnorm_fwd(h4, W["mix_norm"][1], "mixb_norm")
    qraw = _mm(hnb, W["b_w_q"], "nn", M=T, N=D, K=D, name="b_q")
    qh = _qknorm_fwd(qraw, 0, W["b_q_norm"][0], None, D, BF16, "b_qnorm", out_scale=SCALE)
    first, last = _fox_live_blocks(cum, W["b_q_norm"][0], W["kv_k_norm"], B, S, D)
    ob, lse_b = _fox_fwd(qh, kh, kv, cq, ck, first, B, S, D, "b_fox")
    h5 = _mm(ob, W["b_w_o"], "nn", M=T, N=D, K=D, res=h4, name="b_out")
    h6, ffn11 = _ffn_fwd(h5, W["ffn_norm"][1, 1], W["ffn_w_in"][1, 1], W["ffn_w_out"][1, 1], "ffn11")

    sq, dy = _loss_and_grad(h6, target.reshape(T, D), "loss")

    dn = [[None, None], [None, None]]
    dwi = [[None, None], [None, None]]
    dwo = [[None, None], [None, None]]
    dh5, dn[1][1], dwi[1][1], dwo[1][1] = _ffn_bwd(dy, ffn11, W["ffn_norm"][1, 1], W["ffn_w_in"][1, 1], W["ffn_w_out"][1, 1], "ffn11")
    dob = _mm(dh5, W["b_w_o"], "nt", M=T, N=D, K=D, name="b_dout")
    G["b_w_o"] = _mm(ob, dh5, "tn", M=D, N=D, K=T, out_dtype=BF16, name="b_dwo")[None]
    delta_b = _head_rowsum(dob, ob, "b_delta")
    dqh, dcq = _fox_bwd_dq(qh, kh, kv, cq, ck, first, dob, lse_b, delta_b, B, S, D, "b_fox_dq")
    dkh, dv_b, dck = _fox_bwd_dkv(qh, kh, kv, cq, ck, last, dob, lse_b, delta_b, B, S, D, "b_fox_dkv")
    dqraw, dg = _qknorm_bwd(qraw, 0, W["b_q_norm"][0], None, dqh, D, "b_dqnorm", out_scale=SCALE)
    G["b_q_norm"] = dg[None]
    dhnb = _mm(dqraw, W["b_w_q"], "nt", M=T, N=D, K=D, name="b_dhn")
    G["b_w_q"] = _mm(hnb, dqraw, "tn", M=D, N=D, K=T, out_dtype=BF16, name="b_dwq")[None]
    dh4, dmix_b = _rmsnorm_bwd(h4, W["mix_norm"][1], dhnb, dh5, "mixb_dnorm")
    dh3, dn[1][0], dwi[1][0], dwo[1][0] = _ffn_bwd(dh4, ffn10, W["ffn_norm"][1, 0], W["ffn_w_in"][1, 0], W["ffn_w_out"][1, 0], "ffn10")

    dkraw, G["kv_k_norm"] = _qknorm_bwd(kv, 0, W["kv_k_norm"], None, dkh, D, "kv_dknorm")
    dcum = dck.reshape(B, HP, 8, S)[:, :, :2].transpose(0, 3, 1, 2).reshape(T, H)
    dcum = jnp.pad(dcum, ((0, 0), (0, LANES - H)))
    dcum_q = jnp.pad(dcq.reshape(T, H, HEAD_DIM)[:, :, 0], ((0, 0), (0, LANES - H)))
    dfpre, db_f = _gate_bwd(dcum, dcum_q, fpre, b_f, B, S, "kv_dgate")
    G["kv_b_f"] = db_f[0, :H]
    dhnkv = _mm(dkraw, W["kv_w_kv"], "nt", M=T, N=D, K=D, name="kv_dhn_k")
    dhnkv = _mm(dv_b, W["kv_w_kv"], "nt", M=T, N=D, K=D, b_off=(0, D), res=dhnkv, name="kv_dhn_v")
    dhnkv = _mm(dfpre, W["kv_w_f"], "nt", M=T, N=D, K=LANES, res=dhnkv, name="kv_dhn_f")
    dw_k = _mm(hnkv, dkraw, "tn", M=D, N=D, K=T, out_dtype=BF16, name="kv_dwk")
    dw_v = _mm(hnkv, dv_b, "tn", M=D, N=D, K=T, out_dtype=BF16, name="kv_dwv")
    dw_f = _mm(hnkv, dfpre, "tn", M=D, N=LANES, K=T, out_dtype=BF16, name="kv_dwf")
    G["kv_w"] = jnp.concatenate([dw_k, dw_v, dw_f[:, :H]], axis=1)
    dh3, G["kv_norm"] = _rmsnorm_bwd(h3, W["kv_norm"], dhnkv, dh3, "kv_dnorm")

    dh2, dn[0][1], dwi[0][1], dwo[0][1] = _ffn_bwd(dh3, ffn01, W["ffn_norm"][0, 1], W["ffn_w_in"][0, 1], W["ffn_w_out"][0, 1], "ffn01")
    dmixed = _mm(dh2, W["a_w_o"], "nt", M=T, N=D, K=D, name="a_dmixed")
    G["a_w_o"] = _mm(mixed, dh2, "tn", M=D, N=D, K=T, out_dtype=BF16, name="a_dwo")[None]
    delta_a = _head_rowsum(dmixed, mixed, "a_delta")
    dhna = None
    dw_qkv, dqn, dkn = [], [], []
    for g in range(3):
        dq, dk, dv = _band_bwd(qs[g], ks[g], qkv, dmixed, lse_a, delta_a, g, B, S, D, f"a_dband{g}")
        dqr, dgq = _qknorm_bwd(qkv, 3 * g * D, W["a_q_norm"][0, g], cs, dq, D, f"a_dqnorm{g}", out_scale=SCALE)
        dkr, dgk = _qknorm_bwd(qkv, (3 * g + 1) * D, W["a_k_norm"][0, g], cs, dk, D, f"a_dknorm{g}")
        dqn.append(dgq)
        dkn.append(dgk)
        for j, d in enumerate((dqr, dkr, dv)):
            c = 3 * g + j
            dhna = _mm(d, W["a_w_qkv"], "nt", M=T, N=D, K=D, b_off=(0, c * D), res=dhna, name=f"a_dhn{c}")
            dw_qkv.append(_mm(hna, d, "tn", M=D, N=D, K=T, out_dtype=BF16, name=f"a_dwqkv{c}"))
    G["a_w_qkv"] = jnp.concatenate(dw_qkv, axis=1)[None]
    G["a_q_norm"] = jnp.stack(dqn)[None]
    G["a_k_norm"] = jnp.stack(dkn)[None]
    dh1, dmix_a = _rmsnorm_bwd(h1, W["mix_norm"][0], dhna, dh2, "mixa_dnorm")
    dh0, dn[0][0], dwi[0][0], dwo[0][0] = _ffn_bwd(dh1, ffn00, W["ffn_norm"][0, 0], W["ffn_w_in"][0, 0], W["ffn_w_out"][0, 0], "ffn00")

    G["mix_norm"] = jnp.stack([dmix_a, dmix_b])
    G["ffn_norm"] = jnp.stack([jnp.stack(r) for r in dn])
    G["ffn_w_in"] = jnp.stack([jnp.stack(r) for r in dwi])
    G["ffn_w_out"] = jnp.stack([jnp.stack(r) for r in dwo])
    return jnp.sum(sq), dh0.reshape(B, S, D), G


SMALL =["mix_norm", "kv_norm", "a_q_norm", "a_k_norm", "kv_b_f", "kv_k_norm", "b_q_norm"]


def _pack(blocks, lead):
    flat = [b.reshape(b.shape[:lead] + (-1,)) for b in blocks]
    flat = jnp.concatenate(flat, axis=lead)
    n = flat.shape[-1]
    unit = PACK_ROWS * LANES
    padded = -(-n // unit) * unit if n > unit else -(-n // (16 * LANES)) * 16 * LANES
    flat = jnp.pad(flat, [(0, 0)] * lead + [(0, padded - n)])
    return flat.reshape(flat.shape[:lead] + (padded // LANES, LANES))


def _unpack(packed, shapes, lead):
    flat = packed.reshape(packed.shape[:lead] + (-1,))
    out, off = [], 0
    for shp in shapes:
        n = int(np.prod(shp))
        out.append(flat[..., off:off + n].reshape(packed.shape[:lead] + tuple(shp)))
        off += n
    return out


def kernel(x, positions, ffn_norm, ffn_w_in, ffn_w_out, mix_norm, a_w_qkv, a_q_norm, a_k_norm, a_w_o, kv_norm, kv_w, kv_b_f, kv_k_norm, b_w_q, b_q_norm, b_w_o, loss_target, m_ffn_norm, m_ffn_w_in, m_ffn_w_out, m_mix_norm, m_a_w_qkv, m_a_q_norm, m_a_k_norm, m_a_w_o, m_kv_norm, m_kv_w, m_kv_b_f, m_kv_k_norm, m_b_w_q, m_b_q_norm, m_b_w_o, v_ffn_norm, v_ffn_w_in, v_ffn_w_out, v_mix_norm, v_a_w_qkv, v_a_q_norm, v_a_k_norm, v_a_w_o, v_kv_norm, v_kv_w, v_kv_b_f, v_kv_k_norm, v_b_w_q, v_b_q_norm, v_b_w_o):
    names = ["ffn_norm", "ffn_w_in", "ffn_w_out", "mix_norm", "a_w_qkv", "a_q_norm", "a_k_norm", "a_w_o", "kv_norm",
             "kv_w", "kv_b_f", "kv_k_norm", "b_w_q", "b_q_norm", "b_w_o"]
    w = dict(zip(names, (ffn_norm, ffn_w_in, ffn_w_out, mix_norm, a_w_qkv, a_q_norm, a_k_norm, a_w_o, kv_norm, kv_w,
                         kv_b_f, kv_k_norm, b_w_q, b_q_norm, b_w_o)))
    m = dict(zip(names, (m_ffn_norm, m_ffn_w_in, m_ffn_w_out, m_mix_norm, m_a_w_qkv, m_a_q_norm, m_a_k_norm, m_a_w_o,
                         m_kv_norm, m_kv_w, m_kv_b_f, m_kv_k_norm, m_b_w_q, m_b_q_norm, m_b_w_o)))
    v = dict(zip(names, (v_ffn_norm, v_ffn_w_in, v_ffn_w_out, v_mix_norm, v_a_w_qkv, v_a_q_norm, v_a_k_norm, v_a_w_o,
                         v_kv_norm, v_kv_w, v_kv_b_f, v_kv_k_norm, v_b_w_q, v_b_q_norm, v_b_w_o)))
    B, S, D = x.shape
    H = D // HEAD_DIM
    F = ffn_w_out.shape[2] * N_DEV
    Ds, Fs = D // N_DEV, F // N_DEV
    me = 4 * lax.axis_index("x") + 2 * lax.axis_index("y") + lax.axis_index("c")
    squares = ("a_w_o", "b_w_q", "b_w_o")

    def shard_views(t):
        return [t["ffn_w_in"].reshape(4 * D, -1), t["ffn_w_out"].reshape(-1, D), t["a_w_qkv"][0], t["kv_w"],
                jnp.concatenate([t[n][0] for n in squares], axis=0)]

    def to_shards(views):
        sq3 = views[4].reshape(3, 1, Ds, D)
        out = {"ffn_w_in": views[0].reshape(ffn_w_in.shape), "ffn_w_out": views[1].reshape(ffn_w_out.shape),
               "a_w_qkv": views[2][None], "kv_w": views[3]}
        out.update({n: sq3[i] for i, n in enumerate(squares)})
        return out

    norm_rows = jnp.pad(ffn_norm.reshape(4, Ds), ((0, 4), (0, 0)))
    g_in, g_out, g_qkv, g_kv, g_sq, g_norm = _all_gather(
        [s.astype(BF16) for s in shard_views(w)] + [norm_rows], "gather_weights")
    W = {n: w[n] for n in SMALL}
    W["ffn_norm"] = g_norm[:, :4].transpose(1, 0, 2).reshape(2, 2, D)
    W["ffn_w_in"] = g_in.transpose(1, 0, 2).reshape(2, 2, D, 2 * F)
    W["ffn_w_out"] = g_out.reshape(N_DEV, 4, Fs, D).transpose(1, 0, 2, 3).reshape(2, 2, F, D)
    W["a_w_qkv"] = g_qkv.transpose(1, 0, 2).reshape(D, 9 * D)
    kv_full = g_kv.transpose(1, 0, 2).reshape(D, 2 * D + H)
    W["kv_w_kv"] = kv_full[:, :2 * D]
    W["kv_w_f"] = jnp.pad(kv_full[:, 2 * D:], ((0, 0), (0, LANES - H)))
    sq_full = g_sq.reshape(N_DEV, 3, Ds, D).transpose(1, 0, 2, 3).reshape(3, D, D)
    W["a_w_o"], W["b_w_q"], W["b_w_o"] = sq_full[0], sq_full[1], sq_full[2]

    sq, grad_x, G = _local_step(x, positions, loss_target, W)
    loss = lax.psum(sq * (0.5 / D), ("x", "y", "c"))

    blocks = [G["ffn_w_in"].reshape(4 * D, N_DEV, -1).transpose(1, 0, 2),
              G["ffn_w_out"].reshape(4, N_DEV, Fs, D).transpose(1, 0, 2, 3).reshape(N_DEV, 4 * Fs, D),
              G["a_w_qkv"][0].reshape(D, N_DEV, -1).transpose(1, 0, 2),
              G["kv_w"].reshape(D, N_DEV, -1).transpose(1, 0, 2),
              jnp.concatenate([G[n][0].reshape(N_DEV, Ds, D) for n in squares], axis=1)]
    blocks = [b.astype(BF16) for b in blocks]
    got = _pair_exchange(blocks, "exchange_pair")
    parts = _chip_exchange([_pair_sum(b, s, f"pair_sum{i}") for i, (b, s) in enumerate(zip(blocks, got))], "exchange_chips")
    views = [shard_views(t) for t in (w, m, v)]
    outs = [_adamw(parts[i], views[0][i], views[1][i], views[2][i], f"adamw_matrix{i}") for i in range(len(parts))]
    new = {}
    for k, kind in enumerate(("grad", "delta", "new_m", "new_v")):
        for n, arr in to_shards([o[k] for o in outs]).items():
            new[kind, n] = arr

    small_all = ["ffn_norm"] + SMALL
    vec_parts = _all_gather([_pack([G[n] for n in small_all], 0)], "gather_vector_grads")[0]
    vec_parts = _unpack(vec_parts, [G[n].shape for n in small_all], 1)
    norm_g = lax.dynamic_slice_in_dim(vec_parts[0], me * ffn_norm.shape[2], ffn_norm.shape[2], axis=3)
    rep_g = _pack(vec_parts[1:], 1)
    for ns, parts_s in ((["ffn_norm"], _pack([norm_g], 1)), (SMALL, rep_g)):
        outs = _adamw(parts_s, *(_pack([t[n] for n in ns], 0) for t in (w, m, v)), "adamw_" + ns[0])
        for kind, packed in zip(("grad", "delta", "new_m", "new_v"), outs):
            for n, arr in zip(ns, _unpack(packed, [w[n].shape for n in ns], 0)):
                new[kind, n] = arr

    result = [loss, grad_x]
    for kind in ("grad", "delta", "new_m", "new_v"):
        result += [new[kind, n] for n in names]
    return tuple(result)
```

```python
import functools

import jax
import jax.numpy as jnp
import numpy as np
from jax import lax
from jax.experimental import pallas as pl
from jax.experimental.pallas import tpu as pltpu

F32 = jnp.float32
BF16 = jnp.bfloat16
HIGHEST = lax.Precision.HIGHEST

HEAD_DIM = 64
ROT_DIM = HEAD_DIM // 4
ROPE_THETA = 500000.0
DILATED_GROUPS = ((128, 1), (512, 4), (2048, 16))
BAND_BLOCK = 128
EPS = 1e-6
ADAM_LR, ADAM_B1, ADAM_B2, ADAM_EPS, ADAM_WD, ADAM_STEP = 0.001, 0.9, 0.999, 1e-08, 0.01, 10

LANES = 128
N_DEV = 8
VMEM_LIMIT = 56 * 1024 * 1024
NEG = -1e30
SCALE = HEAD_DIM ** -0.5
PACK_ROWS = 1024
MESH_ID = pl.DeviceIdType.MESH


def _params(*sem):
    return pltpu.CompilerParams(dimension_semantics=sem, vmem_limit_bytes=VMEM_LIMIT)


def _divs(n, cap, mult=LANES):
    d = [t for t in range(mult, min(n, cap) + 1, mult) if n % t == 0]
    return d or [n]


def _pick(n, cap, mult=LANES):
    return _divs(n, cap, mult)[-1]


def _mm_tiles(M, N, K, budget=1408 * 1024):
    best = None
    for tm in _divs(M, 1408):
        for tn in _divs(N, 1408):
            if tm * tn <= budget and (best is None or tm * tn > best[0] * best[1]):
                best = (tm, tn)
    if best is None:
        best = (_divs(M, 1408)[0], _divs(N, 1408)[0])
    tm, tn = best
    tks = [t for t in _divs(K, 2048) if (tm + tn) * t <= 3 * 1024 * 1024]
    tk = tks[-1] if tks else _divs(K, 2048)[0]
    return tm, tn, tk


def _dot(a, b, dims, precision=None):
    return lax.dot_general(a, b, (dims, ((), ())), precision=precision, preferred_element_type=F32)


NN = ((1,), (0,))
NT = ((1,), (1,))
TN = ((0,), (0,))


def _dot_split(x, mat):
    hi = x.astype(BF16)
    lo = (x - hi.astype(F32)).astype(BF16)
    m = mat.astype(BF16)
    return _dot(hi, m, NN) + _dot(lo, m, NN)


def _dot_once(x, mat):
    return _dot(x.astype(BF16), mat.astype(BF16), NN)


def _mm(a, b, mode, *, M, N, K, a_off=(0, 0), b_off=(0, 0), res=None, scale=1.0, out_dtype=F32, name):
    tm, tn, tk = _mm_tiles(M, N, K)
    nk = K // tk
    if mode == "tn":
        a_blk, b_blk, dims = (tk, tm), (tk, tn), TN
    elif mode == "nt":
        a_blk, b_blk, dims = (tm, tk), (tn, tk), NT
    else:
        a_blk, b_blk, dims = (tm, tk), (tk, tn), NN
    for off, blk in ((a_off, a_blk), (b_off, b_blk)):
        assert off[0] % blk[0] == 0 and off[1] % blk[1] == 0, (name, off, blk)
    ao = (a_off[0] // a_blk[0], a_off[1] // a_blk[1])
    bo = (b_off[0] // b_blk[0], b_off[1] // b_blk[1])
    if mode == "tn":
        a_map = lambda i, j, k: (k + ao[0], i + ao[1])
        b_map = lambda i, j, k: (k + bo[0], j + bo[1])
    elif mode == "nt":
        a_map = lambda i, j, k: (i + ao[0], k + ao[1])
        b_map = lambda i, j, k: (j + bo[0], k + bo[1])
    else:
        a_map = lambda i, j, k: (i + ao[0], k + ao[1])
        b_map = lambda i, j, k: (k + bo[0], j + bo[1])
    has_res = res is not None

    def body(*refs):
        a_ref, b_ref = refs[:2]
        r_ref = refs[2] if has_res else None
        o_ref = refs[3] if has_res else refs[2]

        def finish(total):
            out = total * scale
            if has_res:
                out = r_ref[...] + out
            o_ref[...] = out.astype(out_dtype)

        part = _dot(a_ref[...].astype(BF16), b_ref[...].astype(BF16), dims)
        if nk == 1:
            finish(part)
            return
        acc = refs[-1]
        k = pl.program_id(2)

        @pl.when(k == 0)
        def _():
            acc[...] = part

        @pl.when(k > 0)
        def _():
            acc[...] += part

        pl.when(k == nk - 1)(lambda: finish(acc[...]))

    in_specs = [pl.BlockSpec(a_blk, a_map), pl.BlockSpec(b_blk, b_map)]
    args = [a, b]
    if has_res:
        in_specs.append(pl.BlockSpec((tm, tn), lambda i, j, k: (i, j)))
        args.append(res)
    return pl.pallas_call(
        body, name=name, grid=(M // tm, N // tn, nk), in_specs=in_specs,
        out_specs=pl.BlockSpec((tm, tn), lambda i, j, k: (i, j)),
        out_shape=jax.ShapeDtypeStruct((M, N), out_dtype),
        scratch_shapes=[pltpu.VMEM((tm, tn), F32)] if nk > 1 else [],
        compiler_params=_params("parallel", "parallel", "arbitrary"),
    )(*args)


def _rmsnorm_fwd(x, g, name):
    T, D = x.shape
    tm = _pick(T, 256, 8)

    def body(x_ref, g_ref, y_ref):
        xv = x_ref[...]
        r = lax.rsqrt(jnp.mean(xv * xv, axis=-1, keepdims=True) + EPS)
        y_ref[...] = (xv * r * g_ref[...]).astype(BF16)

    return pl.pallas_call(
        body, name=name, grid=(T // tm,),
        in_specs=[pl.BlockSpec((tm, D), lambda i: (i, 0)), pl.BlockSpec((1, D), lambda i: (0, 0))],
        out_specs=pl.BlockSpec((tm, D), lambda i: (i, 0)),
        out_shape=jax.ShapeDtypeStruct((T, D), BF16), compiler_params=_params("parallel"),
    )(x, g.reshape(1, D))


def _rmsnorm_bwd(x, g, dy, dres, name):
    T, D = x.shape
    tm = _pick(T, 256, 8)

    def body(x_ref, g_ref, dy_ref, dres_ref, dx_ref, dg_ref):
        xv = x_ref[...]
        r = lax.rsqrt(jnp.mean(xv * xv, axis=-1, keepdims=True) + EPS)
        xn = xv * r
        d = dy_ref[...]
        dgy = d * g_ref[...]
        dx_ref[...] = dres_ref[...] + r * (dgy - xn * jnp.mean(dgy * xn, axis=-1, keepdims=True))

        @pl.when(pl.program_id(0) == 0)
        def _():
            dg_ref[...] = jnp.zeros_like(dg_ref)

        dg_ref[...] += jnp.sum(d * xn, axis=0, keepdims=True)

    row = pl.BlockSpec((tm, D), lambda i: (i, 0))
    vec = pl.BlockSpec((1, D), lambda i: (0, 0))
    dx, dg = pl.pallas_call(
        body, name=name, grid=(T // tm,), in_specs=[row, vec, row, row], out_specs=[row, vec],
        out_shape=[jax.ShapeDtypeStruct((T, D), F32), jax.ShapeDtypeStruct((1, D), F32)],
        compiler_params=_params("arbitrary"),
    )(x, g.reshape(1, D), dy, dres)
    return dx, dg.reshape(D)


def _sigmoid(x):
    return 1.0 / (1.0 + jnp.exp(-x))


def _mm_swiglu_fwd(hn, w_in, name):
    T, D = hn.shape
    F = w_in.shape[1] // 2
    tm, tn, _ = _mm_tiles(T, F, D, budget=768 * 1024)
    assert (tm + 2 * tn) * D <= 4 * 1024 * 1024, "the whole contraction is held in one step"
    nf = F // tn

    def body(a_ref, bg_ref, bu_ref, g_ref, u_ref, act_ref):
        a = a_ref[...].astype(BF16)
        gv = _dot(a, bg_ref[...].astype(BF16), NN)
        uv = _dot(a, bu_ref[...].astype(BF16), NN)
        g_ref[...] = gv
        u_ref[...] = uv
        act_ref[...] = (gv * _sigmoid(gv) * uv).astype(BF16)

    tile = pl.BlockSpec((tm, tn), lambda j, i: (i, j))
    f32 = jax.ShapeDtypeStruct((T, F), F32)
    return pl.pallas_call(
        body, name=name, grid=(nf, T // tm),
        in_specs=[pl.BlockSpec((tm, D), lambda j, i: (i, 0)), pl.BlockSpec((D, tn), lambda j, i: (0, j)),
                  pl.BlockSpec((D, tn), lambda j, i: (0, j + nf))],
        out_specs=[tile, tile, tile], out_shape=[f32, f32, jax.ShapeDtypeStruct((T, F), BF16)],
        compiler_params=_params("parallel", "parallel"),
    )(hn, w_in, w_in)


def _mm_swiglu_bwd(dout, w_out, gate, up, name):
    T, D = dout.shape
    F = w_out.shape[0]
    tm, tn, _ = _mm_tiles(T, F, D, budget=768 * 1024)
    assert (tm + tn) * D <= 4 * 1024 * 1024, "the whole contraction is held in one step"

    def body(a_ref, b_ref, g_ref, u_ref, dg_ref, du_ref):
        d = _dot(a_ref[...].astype(BF16), b_ref[...].astype(BF16), NT) * 0.5
        gv, uv = g_ref[...], u_ref[...]
        s = _sigmoid(gv)
        gs = gv * s
        dg_ref[...] = (d * uv * (s + gs - gs * s)).astype(BF16)
        du_ref[...] = (d * gs).astype(BF16)

    tile = pl.BlockSpec((tm, tn), lambda j, i: (i, j))
    out = jax.ShapeDtypeStruct((T, F), BF16)
    return pl.pallas_call(
        body, name=name, grid=(F // tn, T // tm),
        in_specs=[pl.BlockSpec((tm, D), lambda j, i: (i, 0)), pl.BlockSpec((tn, D), lambda j, i: (j, 0)), tile, tile],
        out_specs=[tile, tile], out_shape=[out, out], compiler_params=_params("parallel", "parallel"),
    )(dout, w_out, gate, up)


def _iota2(shape, dim):
    return lax.broadcasted_iota(jnp.int32, shape, dim)


def _head_ones():
    r, c = _iota2((LANES, LANES), 0), _iota2((LANES, LANES), 1)
    return ((r >> 6) == (c >> 6)).astype(F32)


def _rot_matrix(transpose):
    r, c = _iota2((LANES, LANES), 0), _iota2((LANES, LANES), 1)
    if transpose:
        r, c = c, r
    half = ROT_DIM // 2
    cm = c & (HEAD_DIM - 1)
    neg = (r == c + half) & (cm < half)
    pos = (r == c - half) & (cm >= half) & (cm < ROT_DIM)
    return pos.astype(F32) - neg.astype(F32)


def _rope_tables(positions, name):
    T = positions.size
    tm = _pick(T, 1024, 8)
    inv = ROPE_THETA ** (-jnp.arange(0, ROT_DIM, 2, dtype=F32) / ROT_DIM)
    lane = jnp.concatenate([inv, inv, jnp.zeros((HEAD_DIM - ROT_DIM,), F32)])
    lane = jnp.tile(lane, 2).reshape(1, LANES)

    def body(p_ref, f_ref, c_ref, s_ref):
        ang = p_ref[...].astype(F32) * f_ref[...]
        c_ref[...] = jnp.cos(ang)
        s_ref[...] = jnp.sin(ang)

    out = jax.ShapeDtypeStruct((T, LANES), F32)
    blk = pl.BlockSpec((tm, LANES), lambda i: (i, 0))
    return pl.pallas_call(
        body, name=name, grid=(T // tm,),
        in_specs=[pl.BlockSpec((tm, 1), lambda i: (i, 0)), pl.BlockSpec((1, LANES), lambda i: (0, 0))],
        out_specs=[blk, blk], out_shape=[out, out], compiler_params=_params("parallel"),
    )(positions.reshape(T, 1), lane)


def _qknorm_fwd(src, col_off, gain, cs, D, out_dtype, name, out_scale=1.0):
    T = src.shape[0]
    HP = D // LANES
    tm = _pick(T, 1024, 8)
    co = col_off // LANES
    rope = cs is not None

    def body(*refs):
        if rope:
            x_ref, g_ref, c_ref, s_ref, o_ref = refs
        else:
            x_ref, g_ref, o_ref = refs
        xv = x_ref[...]
        ms = _dot_once(xv * xv, _head_ones()) * (1.0 / HEAD_DIM)
        y = xv * lax.rsqrt(ms + EPS) * g_ref[...]
        if rope:
            y = y * c_ref[...] + _dot_once(y, _rot_matrix(False)) * s_ref[...]
        o_ref[...] = (y * out_scale).astype(out_dtype)

    in_specs = [pl.BlockSpec((tm, LANES), lambda i, h: (i, co + h)), pl.BlockSpec((1, LANES), lambda i, h: (0, 0))]
    args = [src, jnp.tile(gain.reshape(1, HEAD_DIM), (1, 2))]
    if rope:
        in_specs += [pl.BlockSpec((tm, LANES), lambda i, h: (i, 0))] * 2
        args += list(cs)
    return pl.pallas_call(
        body, name=name, grid=(T // tm, HP), in_specs=in_specs,
        out_specs=pl.BlockSpec((tm, LANES), lambda i, h: (i, h)),
        out_shape=jax.ShapeDtypeStruct((T, D), out_dtype), compiler_params=_params("parallel", "parallel"),
    )(*args)


def _qknorm_bwd(src, col_off, gain, cs, dout, D, name, out_scale=1.0):
    T = src.shape[0]
    HP = D // LANES
    tm = _pick(T, 1024, 8)
    co = col_off // LANES
    rope = cs is not None

    def body(*refs):
        if rope:
            x_ref, g_ref, d_ref, c_ref, s_ref, dx_ref, dg_ref = refs
        else:
            x_ref, g_ref, d_ref, dx_ref, dg_ref = refs
        xv = x_ref[...]
        ones = _head_ones()
        ms = _dot_once(xv * xv, ones) * (1.0 / HEAD_DIM)
        r = lax.rsqrt(ms + EPS)
        xn = xv * r
        d = d_ref[...] * out_scale
        if rope:
            d = d * c_ref[...] + _dot_once(d * s_ref[...], _rot_matrix(True))
        dgy = d * g_ref[...]
        mean = _dot_once(dgy * xn, ones) * (1.0 / HEAD_DIM)
        dx_ref[...] = (r * (dgy - xn * mean)).astype(BF16)

        @pl.when(pl.program_id(1) == 0)
        def _():
            dg_ref[...] = jnp.zeros_like(dg_ref)

        dg_ref[...] += jnp.sum(d * xn, axis=0, keepdims=True)[None]

    in_specs = [pl.BlockSpec((tm, LANES), lambda h, i: (i, co + h)), pl.BlockSpec((1, LANES), lambda h, i: (0, 0)),
                pl.BlockSpec((tm, LANES), lambda h, i: (i, h))]
    args = [src, jnp.tile(gain.reshape(1, HEAD_DIM), (1, 2)), dout]
    if rope:
        in_specs += [pl.BlockSpec((tm, LANES), lambda h, i: (i, 0))] * 2
        args += list(cs)
    dx, dg = pl.pallas_call(
        body, name=name, grid=(HP, T // tm), in_specs=in_specs,
        out_specs=[pl.BlockSpec((tm, LANES), lambda h, i: (i, h)), pl.BlockSpec((1, 1, LANES), lambda h, i: (h, 0, 0))],
        out_shape=[jax.ShapeDtypeStruct((T, D), BF16), jax.ShapeDtypeStruct((HP, 1, LANES), F32)],
        compiler_params=_params("parallel", "arbitrary"),
    )(*args)
    return dx, dg.reshape(2 * HP, HEAD_DIM).sum(axis=0)


def _head_rowsum(a, b, name):
    T, D = a.shape
    tm = _pick(T, 1024, 8)

    def body(a_ref, b_ref, o_ref):
        o_ref[...] = _dot_split(a_ref[...] * b_ref[...], _head_ones())

    blk = pl.BlockSpec((tm, LANES), lambda i, h: (i, h))
    return pl.pallas_call(
        body, name=name, grid=(T // tm, D // LANES), in_specs=[blk, blk], out_specs=blk,
        out_shape=jax.ShapeDtypeStruct((T, D), F32), compiler_params=_params("parallel", "parallel"),
    )(a, b)


def _lane_masks():
    lane = _iota2((1, LANES), 1)
    return (lane < HEAD_DIM, lane >= HEAD_DIM)


BAND_GROUP = 4


def _band_geometry(g, S):
    window, dil = DILATED_GROUPS[g]
    chunk = BAND_BLOCK * dil
    assert window // dil == BAND_BLOCK and S % chunk == 0
    if dil == 1:
        assert (S // chunk) % BAND_GROUP == 0
        return dil, chunk, BAND_GROUP, 1
    assert dil % BAND_GROUP == 0
    return dil, chunk, 1, dil // BAND_GROUP


def _band_rows(i, rho, dil, shift=0):
    if dil > 1:
        return pl.ds(rho * BAND_GROUP + i, BAND_BLOCK, stride=dil), shift != 0
    j = i + shift
    outside = j < 0 or j >= BAND_GROUP
    return pl.ds(0 if outside else j * BAND_BLOCK, BAND_BLOCK), outside


def _stack_heads(x):
    hm0, hm1 = _lane_masks()
    zero = jnp.zeros_like(x)
    return jnp.concatenate([jnp.where(hm0, x, zero), jnp.where(hm1, x, zero)], axis=0)


def _unstack_heads(x2):
    hm0, _ = _lane_masks()
    return jnp.where(hm0, x2[:BAND_BLOCK], x2[BAND_BLOCK:])


def _band_masks():
    qi = _iota2((2 * BAND_BLOCK, 2 * BAND_BLOCK), 0) & (BAND_BLOCK - 1)
    kj = _iota2((2 * BAND_BLOCK, 2 * BAND_BLOCK), 1)
    own = (kj >= BAND_BLOCK) & (kj - BAND_BLOCK <= qi)
    prev = (kj < BAND_BLOCK) & (kj >= qi)
    return own, prev


def _band_fwd(qh, kh, qkv, g, B, S, D, name):
    dil, CH, NB, RG = _band_geometry(g, S)
    HP = D // LANES
    vo = (3 * g + 2) * HP
    nsp = S // (CH * NB)

    def body(q_ref, k_ref, kp_ref, v_ref, vp_ref, o_ref, lse_ref):
        n, rho = pl.program_id(1), pl.program_id(3)
        own, prev = _band_masks()
        for i in range(BAND_GROUP):
            rows, _ = _band_rows(i, rho, dil)
            prows, outside = _band_rows(i, rho, dil, shift=-1)
            q, kc, vc = (t[0, rows, :].astype(BF16) for t in (q_ref, k_ref, v_ref))
            kp, vp = (t[0, prows, :].astype(BF16) for t in ((kp_ref, vp_ref) if outside else (k_ref, v_ref)))
            valid = (own | (prev & (n > 0))) if outside else (own | prev)
            s = jnp.where(valid, _dot(_stack_heads(q), jnp.concatenate([kp, kc], axis=0), NT), NEG)
            m = jnp.max(s, axis=-1, keepdims=True)
            e = jnp.exp(s - m)
            l = jnp.sum(e, axis=-1, keepdims=True)
            o2 = _dot((e * (1.0 / l)).astype(BF16), jnp.concatenate([vp, vc], axis=0), NN)
            o_ref[0, rows, :] = _unstack_heads(o2)
            lse_ref[0, rows, :] = _unstack_heads(m + jnp.log(l))

    span, chunk = (1, CH * NB, LANES), (1, CH, LANES)
    cur = lambda b, n, h, r: (b, n, h)
    prev = lambda b, n, h, r: (b, jnp.maximum(n * NB - 1, 0), h)
    vcur = lambda b, n, h, r: (b, n, vo + h)
    vprev = lambda b, n, h, r: (b, jnp.maximum(n * NB - 1, 0), vo + h)
    out = jax.ShapeDtypeStruct((B, S, D), F32)
    q3, k3, v3 = qh.reshape(B, S, D), kh.reshape(B, S, D), qkv.reshape(B, S, 9 * D)
    o, lse = pl.pallas_call(
        body, name=name, grid=(B, nsp, HP, RG),
        in_specs=[pl.BlockSpec(span, cur), pl.BlockSpec(span, cur), pl.BlockSpec(chunk, prev),
                  pl.BlockSpec(span, vcur), pl.BlockSpec(chunk, vprev)],
        out_specs=[pl.BlockSpec(span, cur), pl.BlockSpec(span, cur)], out_shape=[out, out],
        compiler_params=_params("parallel", "parallel", "parallel", "arbitrary"),
    )(q3, k3, k3, v3, v3)
    return o.reshape(B * S, D), lse.reshape(B * S, D)


def _head_col(x, h):
    lane = _iota2((1, LANES), 1)
    return jnp.sum(jnp.where(lane == h * HEAD_DIM, x, 0.0), axis=-1, keepdims=True)


def _band_bwd(qh, kh, qkv, do, lse, delta, g, B, S, D, name):
    dil, CH, NB, RG = _band_geometry(g, S)
    HP = D // LANES
    vo = (3 * g + 2) * HP
    nsp, nch = S // (CH * NB), S // CH

    def stacked_cols(tile):
        return jnp.concatenate([_head_col(tile, 0), _head_col(tile, 1)], axis=0)

    def body(q_ref, qn_ref, do_ref, don_ref, l_ref, ln_ref, d_ref, dn_ref, k_ref, kp_ref, v_ref, vp_ref,
             dq_ref, dk_ref, dv_ref):
        n, rho = pl.program_id(1), pl.program_id(3)
        own, prev = _band_masks()
        band = (_iota2((2 * BAND_BLOCK, BAND_BLOCK), 1) >= (_iota2((2 * BAND_BLOCK, BAND_BLOCK), 0) & (BAND_BLOCK - 1)))
        for i in range(BAND_GROUP):
            rows, _ = _band_rows(i, rho, dil)
            prows, p_out = _band_rows(i, rho, dil, shift=-1)
            nrows, n_out = _band_rows(i, rho, dil, shift=1)
            kc, vc = (t[0, rows, :].astype(BF16) for t in (k_ref, v_ref))
            kp, vp = (t[0, prows, :].astype(BF16) for t in ((kp_ref, vp_ref) if p_out else (k_ref, v_ref)))
            q2, do2 = (_stack_heads(t[0, rows, :].astype(BF16)) for t in (q_ref, do_ref))
            qn2, don2 = (_stack_heads(t[0, nrows, :].astype(BF16)) for t in ((qn_ref, don_ref) if n_out else (q_ref, do_ref)))
            lse2, del2 = (stacked_cols(t[0, rows, :]) for t in (l_ref, d_ref))
            lsen2, deln2 = (stacked_cols(t[0, nrows, :]) for t in ((ln_ref, dn_ref) if n_out else (l_ref, d_ref)))
            valid = (own | (prev & (n > 0))) if p_out else (own | prev)
            next_ok = (band & (n < nsp - 1)) if n_out else band
            k2, v2 = jnp.concatenate([kp, kc], axis=0), jnp.concatenate([vp, vc], axis=0)
            p = jnp.exp(jnp.where(valid, _dot(q2, k2, NT) - lse2, NEG))
            ds = (p * (_dot(do2, v2, NT) - del2)).astype(BF16)
            pc = jnp.exp(jnp.where(next_ok, _dot(qn2, kc, NT) - lsen2, NEG))
            dsc = (pc * (_dot(don2, vc, NT) - deln2)).astype(BF16)
            dq_ref[0, rows, :] = _unstack_heads(_dot(ds, k2, NN))
            dk_ref[0, rows, :] = _dot(ds[:, BAND_BLOCK:], q2, TN) + _dot(dsc, qn2, TN)
            dv_ref[0, rows, :] = _dot(p[:, BAND_BLOCK:].astype(BF16), do2, TN) + _dot(pc.astype(BF16), don2, TN)

    span, chunk = (1, CH * NB, LANES), (1, CH, LANES)
    cur = lambda b, n, h, r: (b, n, h)
    prev = lambda b, n, h, r: (b, jnp.maximum(n * NB - 1, 0), h)
    nxt = lambda b, n, h, r: (b, jnp.minimum((n + 1) * NB, nch - 1), h)
    vcur = lambda b, n, h, r: (b, n, vo + h)
    vprev = lambda b, n, h, r: (b, jnp.maximum(n * NB - 1, 0), vo + h)
    q3, k3, do3, l3, d3 = (t.reshape(B, S, D) for t in (qh, kh, do, lse, delta))
    v3 = qkv.reshape(B, S, 9 * D)
    sp, ch = (lambda m: pl.BlockSpec(span, m)), (lambda m: pl.BlockSpec(chunk, m))
    out = jax.ShapeDtypeStruct((B, S, D), F32)
    dq, dk, dv = pl.pallas_call(
        body, name=name, grid=(B, nsp, HP, RG),
        in_specs=[sp(cur), ch(nxt), sp(cur), ch(nxt), sp(cur), ch(nxt), sp(cur), ch(nxt),
                  sp(cur), ch(prev), sp(vcur), ch(vprev)],
        out_specs=[sp(cur)] * 3, out_shape=[out] * 3,
        compiler_params=_params("parallel", "parallel", "parallel", "arbitrary"),
    )(q3, q3, do3, do3, l3, l3, d3, d3, k3, k3, v3, v3)
    return dq.reshape(B * S, D), dk.reshape(B * S, D), dv.reshape(B * S, D)


def _mix(os_, lses, name):
    T, D = os_[0].shape
    tm = _pick(T, 256, 8)

    def body(o0, o1, o2, l0, l1, l2, mix_ref, lse_ref):
        a, b, c = l0[...], l1[...], l2[...]
        m = jnp.maximum(jnp.maximum(a, b), c)
        ea, eb, ec = jnp.exp(a - m), jnp.exp(b - m), jnp.exp(c - m)
        s = ea + eb + ec
        mix_ref[...] = (ea / s) * o0[...] + (eb / s) * o1[...] + (ec / s) * o2[...]
        lse_ref[...] = m + jnp.log(s)

    blk = pl.BlockSpec((tm, D), lambda i: (i, 0))
    out = jax.ShapeDtypeStruct((T, D), F32)
    return pl.pallas_call(
        body, name=name, grid=(T // tm,), in_specs=[blk] * 6, out_specs=[blk, blk], out_shape=[out, out],
        compiler_params=_params("parallel"),
    )(*os_, *lses)


def _gate_fwd(fpre, b_f, B, S, D, name):
    nb = S // LANES

    def body(f_ref, b_ref, cum_ref, rep_ref, carry):
        @pl.when(pl.program_id(1) == 0)
        def _():
            carry[...] = jnp.zeros_like(carry)

        z = f_ref[...] + b_ref[...]
        lf = jnp.minimum(z, 0.0) - jnp.log(1.0 + jnp.exp(-jnp.abs(z)))
        tri = (_iota2((LANES, LANES), 0) >= _iota2((LANES, LANES), 1)).astype(F32)
        c = _dot(tri, lf, NN, HIGHEST) + carry[...]
        cum_ref[...] = c
        carry[...] += jnp.sum(lf, axis=0, keepdims=True)
        spread = ((_iota2((LANES, D), 1) >> 6) == _iota2((LANES, D), 0)).astype(F32)
        rep_ref[...] = _dot(c, spread, NN, HIGHEST)

    return pl.pallas_call(
        body, name=name, grid=(B, nb),
        in_specs=[pl.BlockSpec((LANES, LANES), lambda b, j: (b * nb + j, 0)), pl.BlockSpec((1, LANES), lambda b, j: (0, 0))],
        out_specs=[pl.BlockSpec((LANES, LANES), lambda b, j: (b * nb + j, 0)), pl.BlockSpec((LANES, D), lambda b, j: (b * nb + j, 0))],
        out_shape=[jax.ShapeDtypeStruct((B * S, LANES), F32), jax.ShapeDtypeStruct((B * S, D), F32)],
        scratch_shapes=[pltpu.VMEM((1, LANES), F32)], compiler_params=_params("arbitrary", "arbitrary"),
    )(fpre, b_f)


def _gate_bwd(dcum_k, dcum_q, fpre, b_f, B, S, name):
    nb = S // LANES

    def body(d_ref, dq_ref, f_ref, b_ref, o_ref, db_ref, carry):
        first = (pl.program_id(0) == 0) & (pl.program_id(1) == 0)

        @pl.when(pl.program_id(1) == 0)
        def _():
            carry[...] = jnp.zeros_like(carry)

        @pl.when(first)
        def _():
            db_ref[...] = jnp.zeros_like(db_ref)

        d = d_ref[...] + dq_ref[...]
        tri = (_iota2((LANES, LANES), 0) <= _iota2((LANES, LANES), 1)).astype(F32)
        dlf = _dot(tri, d, NN, HIGHEST) + carry[...]
        carry[...] += jnp.sum(d, axis=0, keepdims=True)
        z = f_ref[...] + b_ref[...]
        dz = dlf * (1.0 / (1.0 + jnp.exp(z)))
        o_ref[...] = dz.astype(BF16)
        db_ref[...] += jnp.sum(dz, axis=0, keepdims=True)

    rev = lambda b, j: (b * nb + nb - 1 - j, 0)
    blk = pl.BlockSpec((LANES, LANES), rev)
    vec = pl.BlockSpec((1, LANES), lambda b, j: (0, 0))
    return pl.pallas_call(
        body, name=name, grid=(B, nb), in_specs=[blk, blk, blk, vec], out_specs=[blk, vec],
        out_shape=[jax.ShapeDtypeStruct((B * S, LANES), BF16), jax.ShapeDtypeStruct((1, LANES), F32)],
        scratch_shapes=[pltpu.VMEM((1, LANES), F32)], compiler_params=_params("arbitrary", "arbitrary"),
    )(dcum_k, dcum_q, fpre, b_f)


def _fox_scores(q, k, ck_row, hm, diagonal):
    TB = q.shape[0]
    qm = jnp.where(hm, q, jnp.zeros_like(q))
    t = _dot(qm, k, NT) - ck_row
    if diagonal:
        t = jnp.where(_iota2((TB, TB), 1) <= _iota2((TB, TB), 0), t, NEG)
    return qm, t


def _fox_fetch(srcs, bufs, sems, slot):
    return [pltpu.make_async_copy(src, buf.at[slot], sems.at[slot, i]) for i, (src, buf) in enumerate(zip(srcs, bufs))]


FOX_BLOCK = 256
EXP_UNDERFLOW = 104.0


def _fox_live_blocks(cum, gain_q, gain_k, B, S, D):
    H, HP = D // HEAD_DIM, D // LANES
    nb = S // FOX_BLOCK
    bound = 1.02 * HEAD_DIM * SCALE * jnp.max(jnp.abs(gain_q)) * jnp.max(jnp.abs(gain_k))
    c = cum.reshape(B, S, LANES)[:, :, :H]
    decay = c[:, 0::FOX_BLOCK, None, :] - c[:, None, FOX_BLOCK - 1::FOX_BLOCK, :]
    below = jnp.arange(nb)[None, :] < jnp.arange(nb)[:, None]
    dead = (decay < -(EXP_UNDERFLOW + 2.0 * bound)) & below[None, :, :, None]
    live = jnp.logical_not(dead.reshape(B, nb, nb, HP, 2).all(axis=-1))
    first = jnp.argmax(live, axis=2)
    last = nb - 1 - jnp.argmax(live[:, ::-1], axis=1)
    to_table = lambda t: t.transpose(0, 2, 1).reshape(B * HP * nb).astype(jnp.int32)
    return to_table(first), to_table(last)


def _fox_fwd(qh, kh, kv, cq, ck, first, B, S, D, name):
    HP = D // LANES
    TB = FOX_BLOCK
    nb = S // TB

    def body(first_ref, q_ref, cq_ref, k_hbm, v_hbm, ck_hbm, o_ref, lse_ref, kbuf, vbuf, cbuf, sems, m_sc, l_sc, acc):
        b, qb, hp = pl.program_id(0), pl.program_id(1), pl.program_id(2)
        kb0 = first_ref[(b * HP + hp) * nb + qb]

        def fetch(kb, slot):
            rows = pl.ds(pl.multiple_of(kb * TB, TB), TB)
            srcs = (k_hbm.at[b, rows, pl.ds(pl.multiple_of(hp * LANES, LANES), LANES)],
                    v_hbm.at[b, rows, pl.ds(pl.multiple_of((HP + hp) * LANES, LANES), LANES)],
                    ck_hbm.at[b * HP + hp, :, rows])
            return _fox_fetch(srcs, (kbuf, vbuf, cbuf), sems, slot)

        for cp in fetch(kb0, 0):
            cp.start()
        m_sc[...] = jnp.full_like(m_sc, NEG)
        l_sc[...] = jnp.zeros_like(l_sc)
        acc[...] = jnp.zeros_like(acc)

        def step(slot, diagonal):
            q, k, v = q_ref[0], kbuf[slot], vbuf[slot].astype(BF16)
            cqv = cq_ref[0]
            a = acc[...]
            for h, hm in enumerate(_lane_masks()):
                _, t = _fox_scores(q, k, cbuf[slot, pl.ds(h, 1), :], hm, diagonal)
                cqh = _head_col(cqv, h)
                m_prev = jnp.max(m_sc[h], axis=-1, keepdims=True)
                l_prev = jnp.max(l_sc[h], axis=-1, keepdims=True)
                m_new = jnp.maximum(m_prev, jnp.max(t, axis=-1, keepdims=True) + cqh)
                alpha = jnp.exp(m_prev - m_new)
                p = jnp.exp(t + (cqh - m_new))
                l_new = alpha * l_prev + jnp.sum(p, axis=-1, keepdims=True)
                a = jnp.where(hm, alpha * a + _dot(p.astype(BF16), v, NN), a)
                m_sc[h] = jnp.broadcast_to(m_new, (TB, LANES))
                l_sc[h] = jnp.broadcast_to(l_new, (TB, LANES))
            acc[...] = a

        def walk(i, carry):
            slot, kb = lax.rem(i, 2), kb0 + i

            @pl.when(kb < qb)
            def _():
                for cp in fetch(kb + 1, 1 - slot):
                    cp.start()

            for cp in fetch(kb, slot):
                cp.wait()
            pl.when(kb < qb)(functools.partial(step, slot, False))
            pl.when(kb == qb)(functools.partial(step, slot, True))
            return carry

        lax.fori_loop(0, qb - kb0 + 1, walk, 0)
        a = acc[...]
        out = a
        lse = jnp.zeros_like(a)
        for h, hm in enumerate(_lane_masks()):
            out = jnp.where(hm, a / l_sc[h], out)
            lse = jnp.where(hm, m_sc[h] + jnp.log(l_sc[h]), lse)
        o_ref[0] = out
        lse_ref[0] = lse

    out = jax.ShapeDtypeStruct((B, S, D), F32)
    spec = pltpu.PrefetchScalarGridSpec(
        num_scalar_prefetch=1, grid=(B, nb, HP), in_specs=_fox_q_specs(2, TB) + [_ANY] * 3,
        out_specs=_fox_q_specs(2, TB), scratch_shapes=_fox_k_buffers(TB) + [
            pltpu.VMEM((2, TB, LANES), F32), pltpu.VMEM((2, TB, LANES), F32), pltpu.VMEM((TB, LANES), F32)])
    o, lse = pl.pallas_call(
        body, name=name, grid_spec=spec, out_shape=[out, out], compiler_params=_params("parallel", "parallel", "parallel"),
    )(first, qh.reshape(B, S, D), cq.reshape(B, S, D), kh.reshape(B, S, D), kv.reshape(B, S, 2 * D), ck)
    return o.reshape(B * S, D), lse.reshape(B * S, D)


_ANY = pl.BlockSpec(memory_space=pl.ANY)


def _fox_q_specs(n, TB):
    return [pl.BlockSpec((1, TB, LANES), lambda b, i, h, table: (b, i, h))] * n


def _fox_k_buffers(TB):
    return [pltpu.VMEM((2, TB, LANES), BF16), pltpu.VMEM((2, TB, LANES), F32), pltpu.VMEM((2, 8, TB), F32),
            pltpu.SemaphoreType.DMA((2, 3))]


def _fox_bwd_dq(qh, kh, kv, cq, ck, first, do, lse, delta, B, S, D, name):
    HP = D // LANES
    TB = FOX_BLOCK
    nb = S // TB

    def body(first_ref, q_ref, cq_ref, do_ref, l_ref, d_ref, k_hbm, v_hbm, ck_hbm, dq_ref, dc_ref,
             kbuf, vbuf, cbuf, sems, acc, acc_c):
        b, qb, hp = pl.program_id(0), pl.program_id(1), pl.program_id(2)
        kb0 = first_ref[(b * HP + hp) * nb + qb]

        def fetch(kb, slot):
            rows = pl.ds(pl.multiple_of(kb * TB, TB), TB)
            srcs = (k_hbm.at[b, rows, pl.ds(pl.multiple_of(hp * LANES, LANES), LANES)],
                    v_hbm.at[b, rows, pl.ds(pl.multiple_of((HP + hp) * LANES, LANES), LANES)],
                    ck_hbm.at[b * HP + hp, :, rows])
            return _fox_fetch(srcs, (kbuf, vbuf, cbuf), sems, slot)

        for cp in fetch(kb0, 0):
            cp.start()
        acc[...] = jnp.zeros_like(acc)
        acc_c[...] = jnp.zeros_like(acc_c)

        def step(slot, diagonal):
            q, k, v = q_ref[0], kbuf[slot], vbuf[slot].astype(BF16)
            dout = do_ref[0].astype(BF16)
            a, c = acc[...], acc_c[...]
            for h, hm in enumerate(_lane_masks()):
                _, t = _fox_scores(q, k, cbuf[slot, pl.ds(h, 1), :], hm, diagonal)
                p = jnp.exp(t + (_head_col(cq_ref[0], h) - _head_col(l_ref[0], h)))
                dom = jnp.where(hm, dout, jnp.zeros_like(dout))
                ds = p * (_dot(dom, v, NT) - _head_col(d_ref[0], h))
                a += _dot(ds.astype(BF16), jnp.where(hm, k, jnp.zeros_like(k)), NN)
                c = jnp.where(hm, c + jnp.sum(ds, axis=-1, keepdims=True), c)
            acc[...] = a
            acc_c[...] = c

        def walk(i, carry):
            slot, kb = lax.rem(i, 2), kb0 + i

            @pl.when(kb < qb)
            def _():
                for cp in fetch(kb + 1, 1 - slot):
                    cp.start()

            for cp in fetch(kb, slot):
                cp.wait()
            pl.when(kb < qb)(functools.partial(step, slot, False))
            pl.when(kb == qb)(functools.partial(step, slot, True))
            return carry

        lax.fori_loop(0, qb - kb0 + 1, walk, 0)
        dq_ref[0] = acc[...]
        dc_ref[0] = acc_c[...]

    r3 = lambda t: t.reshape(B, S, D)
    out = jax.ShapeDtypeStruct((B, S, D), F32)
    spec = pltpu.PrefetchScalarGridSpec(
        num_scalar_prefetch=1, grid=(B, nb, HP), in_specs=_fox_q_specs(5, TB) + [_ANY] * 3,
        out_specs=_fox_q_specs(2, TB),
        scratch_shapes=_fox_k_buffers(TB) + [pltpu.VMEM((TB, LANES), F32), pltpu.VMEM((TB, LANES), F32)])
    dq, dc = pl.pallas_call(
        body, name=name, grid_spec=spec, out_shape=[out, out], compiler_params=_params("parallel", "parallel", "parallel"),
    )(first, r3(qh), r3(cq), r3(do), r3(lse), r3(delta), r3(kh), kv.reshape(B, S, 2 * D), ck)
    return dq.reshape(B * S, D), dc.reshape(B * S, D)


def _fox_bwd_dkv(qh, kh, kv, cq, ck, last, do, lse, delta, B, S, D, name):
    HP = D // LANES
    TB = FOX_BLOCK
    nb = S // TB

    def body(last_ref, k_ref, v_ref, ck_ref, q_hbm, cq_hbm, do_hbm, l_hbm, d_hbm, dk_ref, dv_ref, dc_ref,
             qbuf, cqbuf, dobuf, lbuf, dbuf, sems, dk_acc, dv_acc, dc_acc):
        b, kb, hp = pl.program_id(0), pl.program_id(1), pl.program_id(2)
        qb_last = last_ref[(b * HP + hp) * nb + kb]

        def fetch(qb, slot):
            rows = pl.ds(pl.multiple_of(qb * TB, TB), TB)
            cols = pl.ds(pl.multiple_of(hp * LANES, LANES), LANES)
            srcs = [t.at[b, rows, cols] for t in (q_hbm, cq_hbm, do_hbm, l_hbm, d_hbm)]
            return _fox_fetch(srcs, (qbuf, cqbuf, dobuf, lbuf, dbuf), sems, slot)

        for cp in fetch(kb, 0):
            cp.start()
        dk_acc[...] = jnp.zeros_like(dk_acc)
        dv_acc[...] = jnp.zeros_like(dv_acc)
        dc_acc[...] = jnp.zeros_like(dc_acc)

        def step(slot, diagonal):
            q, k, v = qbuf[slot], k_ref[0], v_ref[0].astype(BF16)
            dout = dobuf[slot].astype(BF16)
            dk, dv, dc = dk_acc[...], dv_acc[...], dc_acc[...]
            row = _iota2((8, TB), 0)
            for h, hm in enumerate(_lane_masks()):
                qm, t = _fox_scores(q, k, ck_ref[0, h:h + 1, :], hm, diagonal)
                p = jnp.exp(t + (_head_col(cqbuf[slot], h) - _head_col(lbuf[slot], h)))
                dom = jnp.where(hm, dout, jnp.zeros_like(dout))
                ds = p * (_dot(dom, v, NT) - _head_col(dbuf[slot], h))
                dv += _dot(p.astype(BF16), dom, TN)
                dk += _dot(ds.astype(BF16), qm, TN)
                dc = jnp.where(row == h, dc - jnp.sum(ds, axis=0, keepdims=True), dc)
            dk_acc[...] = dk
            dv_acc[...] = dv
            dc_acc[...] = dc

        def walk(i, carry):
            slot, qb = lax.rem(i, 2), kb + i

            @pl.when(qb < qb_last)
            def _():
                for cp in fetch(qb + 1, 1 - slot):
                    cp.start()

            for cp in fetch(qb, slot):
                cp.wait()
            pl.when(i > 0)(functools.partial(step, slot, False))
            pl.when(i == 0)(functools.partial(step, slot, True))
            return carry

        lax.fori_loop(0, qb_last - kb + 1, walk, 0)
        dk_ref[0] = dk_acc[...]
        dv_ref[0] = dv_acc[...].astype(BF16)
        dc_ref[0] = dc_acc[...]

    blk = (1, TB, LANES)
    kmap = lambda b, i, h, last: (b, i, h)
    vmap = lambda b, i, h, last: (b, i, HP + h)
    cmap = lambda b, i, h, last: (b * HP + h, 0, i)
    r3 = lambda t: t.reshape(B, S, D)
    tile = lambda dt: pltpu.VMEM((2, TB, LANES), dt)
    spec = pltpu.PrefetchScalarGridSpec(
        num_scalar_prefetch=1, grid=(B, nb, HP),
        in_specs=[pl.BlockSpec(blk, kmap), pl.BlockSpec(blk, vmap), pl.BlockSpec((1, 8, TB), cmap)] + [_ANY] * 5,
        out_specs=[pl.BlockSpec(blk, kmap), pl.BlockSpec(blk, kmap), pl.BlockSpec((1, 8, TB), cmap)],
        scratch_shapes=[tile(BF16), tile(F32), tile(F32), tile(F32), tile(F32), pltpu.SemaphoreType.DMA((2, 5)),
                        pltpu.VMEM((TB, LANES), F32), pltpu.VMEM((TB, LANES), F32), pltpu.VMEM((8, TB), F32)])
    dk, dv, dc = pl.pallas_call(
        body, name=name, grid_spec=spec,
        out_shape=[jax.ShapeDtypeStruct((B, S, D), F32), jax.ShapeDtypeStruct((B, S, D), BF16),
                   jax.ShapeDtypeStruct((B * HP, 8, S), F32)],
        compiler_params=_params("parallel", "parallel", "parallel"),
    )(last, r3(kh), kv.reshape(B, S, 2 * D), ck, r3(qh), r3(cq), r3(do), r3(lse), r3(delta))
    return dk.reshape(B * S, D), dv.reshape(B * S, D), dc


def _loss_and_grad(y, target, name):
    T, D = y.shape
    tm = _pick(T, 256, 8)

    def body(y_ref, t_ref, p_ref, dy_ref):
        e = y_ref[...] - t_ref[...]
        dy_ref[...] = e * (1.0 / D)
        p_ref[...] = jnp.sum(e * e, axis=0, keepdims=True)[None]

    blk = pl.BlockSpec((tm, D), lambda i: (i, 0))
    return pl.pallas_call(
        body, name=name, grid=(T // tm,), in_specs=[blk, blk],
        out_specs=[pl.BlockSpec((1, 1, D), lambda i: (i, 0, 0)), blk],
        out_shape=[jax.ShapeDtypeStruct((T // tm, 1, D), F32), jax.ShapeDtypeStruct((T, D), F32)],
        compiler_params=_params("parallel"),
    )(y, target)


def _pair_sum(x, got, name):
    _, R, C = x.shape
    tr = _pick(R, max(16, (PACK_ROWS * LANES) // C), 16)

    def body(c_ref, x_ref, g_ref, o_ref):
        o_ref[0] = (x_ref[0].astype(F32) + g_ref[0].astype(F32)).astype(o_ref.dtype)

    spec = pltpu.PrefetchScalarGridSpec(
        num_scalar_prefetch=1, grid=(4, R // tr),
        in_specs=[pl.BlockSpec((1, tr, C), lambda a, i, c: (2 * a + c[0], i, 0)),
                  pl.BlockSpec((1, tr, C), lambda a, i, c: (a, i, 0))],
        out_specs=pl.BlockSpec((1, tr, C), lambda a, i, c: (a, i, 0)))
    return pl.pallas_call(
        body, name=name, grid_spec=spec, out_shape=jax.ShapeDtypeStruct((4, R, C), x.dtype),
        compiler_params=_params("parallel", "parallel"),
    )(lax.axis_index("c").astype(jnp.int32).reshape(1), x, got)


def _adamw(parts, w, m, v, name):
    R, C = w.shape
    P = parts.shape[0]
    tr = _pick(R, max(16, (PACK_ROWS * LANES) // C), 16)

    def body(p_ref, w_ref, m_ref, v_ref, g_out, d_out, m_out, v_out):
        g = p_ref[0].astype(F32)
        for s in range(1, P):
            g = g + p_ref[s].astype(F32)
        mn = ADAM_B1 * m_ref[...] + (1.0 - ADAM_B1) * g
        vn = ADAM_B2 * v_ref[...] + (1.0 - ADAM_B2) * (g * g)
        m_hat = mn / (1.0 - ADAM_B1 ** ADAM_STEP)
        v_hat = vn / (1.0 - ADAM_B2 ** ADAM_STEP)
        g_out[...] = g
        d_out[...] = -ADAM_LR * (m_hat / (jnp.sqrt(v_hat) + ADAM_EPS) + ADAM_WD * w_ref[...])
        m_out[...] = mn
        v_out[...] = vn

    blk = pl.BlockSpec((tr, C), lambda i: (i, 0))
    out = jax.ShapeDtypeStruct((R, C), F32)
    return pl.pallas_call(
        body, name=name, grid=(R // tr,),
        in_specs=[pl.BlockSpec((P, tr, C), lambda i: (0, i, 0)), blk, blk, blk],
        out_specs=[blk] * 4, out_shape=[out] * 4, compiler_params=_params("parallel"),
    )(parts, w, m, v)


def _mesh_pos():
    return lax.axis_index("x"), lax.axis_index("y"), lax.axis_index("c")


def _comm_call(body, xs, out_shapes, name):
    n = len(xs)
    any_spec = pl.BlockSpec(memory_space=pl.ANY)
    return pl.pallas_call(
        body, name=name, out_shape=out_shapes, in_specs=[any_spec] * n, out_specs=[any_spec] * n,
        scratch_shapes=[pltpu.SemaphoreType.DMA((7, n)), pltpu.SemaphoreType.DMA((7, n)), pltpu.SemaphoreType.DMA((n,))],
    )(*xs)


def _all_gather(xs, name):
    n = len(xs)

    def body(*refs):
        x_refs, out_refs = refs[:n], refs[n:2 * n]
        send_sems, recv_sems, local_sems = refs[2 * n:]
        x_, y_, c_ = _mesh_pos()
        me, sibling = (x_, y_, c_), (x_, y_, 1 - c_)
        chips = [(1 - x_, y_), (x_, 1 - y_), (1 - x_, 1 - y_)]

        def slot(t, px, py, pc):
            return out_refs[t].at[4 * px + 2 * py + pc]

        def copy(t, k, block, to, src=None):
            return pltpu.make_async_remote_copy(
                src_ref=slot(t, *block) if src is None else src, dst_ref=slot(t, *block),
                send_sem=send_sems.at[k, t], recv_sem=recv_sems.at[k, t], device_id=to, device_id_type=MESH_ID)

        mine = [pltpu.make_async_copy(x_refs[t], slot(t, *me), local_sems.at[t]) for t in range(n)]
        first = [copy(t, 0, me, sibling, src=x_refs[t]) for t in range(n)]
        first += [copy(t, 1 + j, me, (*chip, c_), src=x_refs[t]) for j, chip in enumerate(chips) for t in range(n)]
        for cp in mine + first:
            cp.start()
        passed = []
        for j, chip in enumerate(chips):
            for t in range(n):
                copy(t, 1 + j, (*chip, c_), me).wait_recv()
                passed.append(copy(t, 4 + j, (*chip, c_), sibling))
                passed[-1].start()
        for t in range(n):
            copy(t, 0, sibling, me).wait_recv()
        for j, chip in enumerate(chips):
            for t in range(n):
                copy(t, 4 + j, (*chip, 1 - c_), me).wait_recv()
        for cp in first + passed:
            cp.wait_send()
        for cp in mine:
            cp.wait()

    return _comm_call(body, xs, [jax.ShapeDtypeStruct((N_DEV,) + x.shape, x.dtype) for x in xs], name)


def _pair_exchange(xs, name):
    n = len(xs)

    def body(*refs):
        x_refs, out_refs = refs[:n], refs[n:2 * n]
        send_sems, recv_sems, _ = refs[2 * n:]
        x_, y_, c_ = _mesh_pos()
        copies = [pltpu.make_async_remote_copy(
            src_ref=x_refs[t].at[2 * a + 1 - c_], dst_ref=out_refs[t].at[a],
            send_sem=send_sems.at[a, t], recv_sem=recv_sems.at[a, t],
            device_id=(x_, y_, 1 - c_), device_id_type=MESH_ID) for a in range(4) for t in range(n)]
        for cp in copies:
            cp.start()
        for cp in copies:
            cp.wait_recv()
        for cp in copies:
            cp.wait_send()

    return _comm_call(body, xs, [jax.ShapeDtypeStruct((4,) + x.shape[1:], x.dtype) for x in xs], name)


def _chip_exchange(xs, name):
    n = len(xs)

    def body(*refs):
        x_refs, out_refs = refs[:n], refs[n:2 * n]
        send_sems, recv_sems, local_sems = refs[2 * n:]
        x_, y_, c_ = _mesh_pos()
        chip = 2 * x_ + y_
        mine = [pltpu.make_async_copy(x_refs[t].at[chip], out_refs[t].at[chip], local_sems.at[t]) for t in range(n)]
        copies = []
        for k in range(1, 4):
            px, py = x_ ^ (k >> 1), y_ ^ (k & 1)
            for t in range(n):
                copies.append(pltpu.make_async_remote_copy(
                    src_ref=x_refs[t].at[2 * px + py], dst_ref=out_refs[t].at[chip],
                    send_sem=send_sems.at[k - 1, t], recv_sem=recv_sems.at[k - 1, t],
                    device_id=(px, py, c_), device_id_type=MESH_ID))
        for cp in mine + copies:
            cp.start()
        for cp in copies:
            cp.wait_recv()
        for cp in copies:
            cp.wait_send()
        for cp in mine:
            cp.wait()

    return _comm_call(body, xs, [jax.ShapeDtypeStruct(x.shape, x.dtype) for x in xs], name)


def _ffn_fwd(h, norm, w_in, w_out, tag):
    T, D = h.shape
    F = w_out.shape[0]
    hn = _rmsnorm_fwd(h, norm, f"{tag}_norm")
    gate, up, act = _mm_swiglu_fwd(hn, w_in, f"{tag}_in")
    out = _mm(act, w_out, "nn", M=T, N=D, K=F, res=h, scale=0.5, name=f"{tag}_out")
    return out, (h, hn, gate, up, act)


def _ffn_bwd(dout, saved, norm, w_in, w_out, tag):
    h, hn, gate, up, act = saved
    T, D = h.shape
    F = w_out.shape[0]
    dg, du = _mm_swiglu_bwd(dout, w_out, gate, up, f"{tag}_dact")
    dw_out = _mm(act, dout, "tn", M=F, N=D, K=T, scale=0.5, out_dtype=BF16, name=f"{tag}_dwout")
    dhn = _mm(dg, w_in, "nt", M=T, N=D, K=F, name=f"{tag}_dhn_g")
    dhn = _mm(du, w_in, "nt", M=T, N=D, K=F, b_off=(0, F), res=dhn, name=f"{tag}_dhn_u")
    dw_g = _mm(hn, dg, "tn", M=D, N=F, K=T, out_dtype=BF16, name=f"{tag}_dwin_g")
    dw_u = _mm(hn, du, "tn", M=D, N=F, K=T, out_dtype=BF16, name=f"{tag}_dwin_u")
    dh, dnorm = _rmsnorm_bwd(h, norm, dhn, dout, f"{tag}_dnorm")
    return dh, dnorm, jnp.concatenate([dw_g, dw_u], axis=1), dw_out


def _local_step(x, positions, target, W):
    B, S, D = x.shape
    T = B * S
    H, HP = D // HEAD_DIM, D // LANES
    G = {}
    h0 = x.reshape(T, D)
    cs = _rope_tables(positions, "rope_tables")

    h1, ffn00 = _ffn_fwd(h0, W["ffn_norm"][0, 0], W["ffn_w_in"][0, 0], W["ffn_w_out"][0, 0], "ffn00")
    hna = _rmsnorm_fwd(h1, W["mix_norm"][0], "mixa_norm")
    qkv = _mm(hna, W["a_w_qkv"], "nn", M=T, N=9 * D, K=D, name="a_qkv")
    qs, ks, os_, lses = [], [], [], []
    for g in range(3):
        qs.append(_qknorm_fwd(qkv, 3 * g * D, W["a_q_norm"][0, g], cs, D, F32, f"a_qnorm{g}", out_scale=SCALE))
        ks.append(_qknorm_fwd(qkv, (3 * g + 1) * D, W["a_k_norm"][0, g], cs, D, F32, f"a_knorm{g}"))
        o, l = _band_fwd(qs[g], ks[g], qkv, g, B, S, D, f"a_band{g}")
        os_.append(o)
        lses.append(l)
    mixed, lse_a = _mix(os_, lses, "a_mix")
    h2 = _mm(mixed, W["a_w_o"], "nn", M=T, N=D, K=D, res=h1, name="a_out")
    h3, ffn01 = _ffn_fwd(h2, W["ffn_norm"][0, 1], W["ffn_w_in"][0, 1], W["ffn_w_out"][0, 1], "ffn01")

    hnkv = _rmsnorm_fwd(h3, W["kv_norm"], "kv_norm")
    kv = _mm(hnkv, W["kv_w_kv"], "nn", M=T, N=2 * D, K=D, name="kv_proj")
    fpre = _mm(hnkv, W["kv_w_f"], "nn", M=T, N=LANES, K=D, name="kv_gate_proj")
    b_f = jnp.pad(W["kv_b_f"], (0, LANES - H)).reshape(1, LANES)
    kh = _qknorm_fwd(kv, 0, W["kv_k_norm"], None, D, BF16, "kv_knorm")
    cum, cq = _gate_fwd(fpre, b_f, B, S, D, "kv_gate")
    ck = cum.reshape(B, S, LANES)[:, :, :H].reshape(B, S, HP, 2).transpose(0, 2, 3, 1)
    ck = jnp.pad(ck, ((0, 0), (0, 0), (0, 6), (0, 0))).reshape(B * HP, 8, S)
    h4, ffn10 = _ffn_fwd(h3, W["ffn_norm"][1, 0], W["ffn_w_in"][1, 0], W["ffn_w_out"][1, 0], "ffn10")
    hnb = _rmsnorm_fwd(h4, W["mix_norm"][1], "mixb_norm")
    qraw = _mm(hnb, W["b_w_q"], "nn", M=T, N=D, K=D, name="b_q")
    qh = _qknorm_fwd(qraw, 0, W["b_q_norm"][0], None, D, BF16, "b_qnorm", out_scale=SCALE)
    first, last = _fox_live_blocks(cum, W["b_q_norm"][0], W["kv_k_norm"], B, S, D)
    ob, lse_b = _fox_fwd(qh, kh, kv, cq, ck, first, B, S, D, "b_fox")
    h5 = _mm(ob, W["b_w_o"], "nn", M=T, N=D, K=D, res=h4, name="b_out")
    h6, ffn11 = _ffn_fwd(h5, W["ffn_norm"][1, 1], W["ffn_w_in"][1, 1], W["ffn_w_out"][1, 1], "ffn11")

    sq, dy = _loss_and_grad(h6, target.reshape(T, D), "loss")

    dn = [[None, None], [None, None]]
    dwi = [[None, None], [None, None]]
    dwo = [[None, None], [None, None]]
    dh5, dn[1][1], dwi[1][1], dwo[1][1] = _ffn_bwd(dy, ffn11, W["ffn_norm"][1, 1], W["ffn_w_in"][1, 1], W["ffn_w_out"][1, 1], "ffn11")
    dob = _mm(dh5, W["b_w_o"], "nt", M=T, N=D, K=D, name="b_dout")
    G["b_w_o"] = _mm(ob, dh5, "tn", M=D, N=D, K=T, out_dtype=BF16, name="b_dwo")[None]
    delta_b = _head_rowsum(dob, ob, "b_delta")
    dqh, dcq = _fox_bwd_dq(qh, kh, kv, cq, ck, first, dob, lse_b, delta_b, B, S, D, "b_fox_dq")
    dkh, dv_b, dck = _fox_bwd_dkv(qh, kh, kv, cq, ck, last, dob, lse_b, delta_b, B, S, D, "b_fox_dkv")
    dqraw, dg = _qknorm_bwd(qraw, 0, W["b_q_norm"][0], None, dqh, D, "b_dqnorm", out_scale=SCALE)
    G["b_q_norm"] = dg[None]
    dhnb = _mm(dqraw, W["b_w_q"], "nt", M=T, N=D, K=D, name="b_dhn")
    G["b_w_q"] = _mm(hnb, dqraw, "tn", M=D, N=D, K=T, out_dtype=BF16, name="b_dwq")[None]
    dh4, dmix_b = _rmsnorm_bwd(h4, W["mix_norm"][1], dhnb, dh5, "mixb_dnorm")
    dh3, dn[1][0], dwi[1][0], dwo[1][0] = _ffn_bwd(dh4, ffn10, W["ffn_norm"][1, 0], W["ffn_w_in"][1, 0], W["ffn_w_out"][1, 0], "ffn10")

    dkraw, G["kv_k_norm"] = _qknorm_bwd(kv, 0, W["kv_k_norm"], None, dkh, D, "kv_dknorm")
    dcum = dck.reshape(B, HP, 8, S)[:, :, :2].transpose(0, 3, 1, 2).reshape(T, H)
    dcum = jnp.pad(dcum, ((0, 0), (0, LANES - H)))
    dcum_q = jnp.pad(dcq.reshape(T, H, HEAD_DIM)[:, :, 0], ((0, 0), (0, LANES - H)))
    dfpre, db_f = _gate_bwd(dcum, dcum_q, fpre, b_f, B, S, "kv_dgate")
    G["kv_b_f"] = db_f[0, :H]
    dhnkv = _mm(dkraw, W["kv_w_kv"], "nt", M=T, N=D, K=D, name="kv_dhn_k")
    dhnkv = _mm(dv_b, W["kv_w_kv"], "nt", M=T, N=D, K=D, b_off=(0, D), res=dhnkv, name="kv_dhn_v")
    dhnkv = _mm(dfpre, W["kv_w_f"], "nt", M=T, N=D, K=LANES, res=dhnkv, name="kv_dhn_f")
    dw_k = _mm(hnkv, dkraw, "tn", M=D, N=D, K=T, out_dtype=BF16, name="kv_dwk")
    dw_v = _mm(hnkv, dv_b, "tn", M=D, N=D, K=T, out_dtype=BF16, name="kv_dwv")
    dw_f = _mm(hnkv, dfpre, "tn", M=D, N=LANES, K=T, out_dtype=BF16, name="kv_dwf")
    G["kv_w"] = jnp.concatenate([dw_k, dw_v, dw_f[:, :H]], axis=1)
    dh3, G["kv_norm"] = _rmsnorm_bwd(h3, W["kv_norm"], dhnkv, dh3, "kv_dnorm")

    dh2, dn[0][1], dwi[0][1], dwo[0][1] = _ffn_bwd(dh3, ffn01, W["ffn_norm"][0, 1], W["ffn_w_in"][0, 1], W["ffn_w_out"][0, 1], "ffn01")
    dmixed = _mm(dh2, W["a_w_o"], "nt", M=T, N=D, K=D, name="a_dmixed")
    G["a_w_o"] = _mm(mixed, dh2, "tn", M=D, N=D, K=T, out_dtype=BF16, name="a_dwo")[None]
    delta_a = _head_rowsum(dmixed, mixed, "a_delta")
    dhna = None
    dw_qkv, dqn, dkn = [], [], []
    for g in range(3):
        dq, dk, dv = _band_bwd(qs[g], ks[g], qkv, dmixed, lse_a, delta_a, g, B, S, D, f"a_dband{g}")
        dqr, dgq = _qknorm_bwd(qkv, 3 * g * D, W["a_q_norm"][0, g], cs, dq, D, f"a_dqnorm{g}", out_scale=SCALE)
        dkr, dgk = _qknorm_bwd(qkv, (3 * g + 1) * D, W["a_k_norm"][0, g], cs, dk, D, f"a_dknorm{g}")
        dqn.append(dgq)
        dkn.append(dgk)
        for j, d in enumerate((dqr, dkr, dv)):
            c = 3 * g + j
            dhna = _mm(d, W["a_w_qkv"], "nt", M=T, N=D, K=D, b_off=(0, c * D), res=dhna, name=f"a_dhn{c}")
            dw_qkv.append(_mm(hna, d, "tn", M=D, N=D, K=T, out_dtype=BF16, name=f"a_dwqkv{c}"))
    G["a_w_qkv"] = jnp.concatenate(dw_qkv, axis=1)[None]
    G["a_q_norm"] = jnp.stack(dqn)[None]
    G["a_k_norm"] = jnp.stack(dkn)[None]
    dh1, dmix_a = _rmsnorm_bwd(h1, W["mix_norm"][0], dhna, dh2, "mixa_dnorm")
    dh0, dn[0][0], dwi[0][0], dwo[0][0] = _ffn_bwd(dh1, ffn00, W["ffn_norm"][0, 0], W["ffn_w_in"][0, 0], W["ffn_w_out"][0, 0], "ffn00")

    G["mix_norm"] = jnp.stack([dmix_a, dmix_b])
    G["ffn_norm"] = jnp.stack([jnp.stack(r) for r in dn])
    G["ffn_w_in"] = jnp.stack([jnp.stack(r) for r in dwi])
    G["ffn_w_out"] = jnp.stack([jnp.stack(r) for r in dwo])
    return jnp.sum(sq), dh0.reshape(B, S, D), G


SMALL =["mix_norm", "kv_norm", "a_q_norm", "a_k_norm", "kv_b_f", "kv_k_norm", "b_q_norm"]


def _pack(blocks, lead):
    flat = [b.reshape(b.shape[:lead] + (-1,)) for b in blocks]
    flat = jnp.concatenate(flat, axis=lead)
    n = flat.shape[-1]
    unit = PACK_ROWS * LANES
    padded = -(-n // unit) * unit if n > unit else -(-n // (16 * LANES)) * 16 * LANES
    flat = jnp.pad(flat, [(0, 0)] * lead + [(0, padded - n)])
    return flat.reshape(flat.shape[:lead] + (padded // LANES, LANES))


def _unpack(packed, shapes, lead):
    flat = packed.reshape(packed.shape[:lead] + (-1,))
    out, off = [], 0
    for shp in shapes:
        n = int(np.prod(shp))
        out.append(flat[..., off:off + n].reshape(packed.shape[:lead] + tuple(shp)))
        off += n
    return out


def kernel(x, positions, ffn_norm, ffn_w_in, ffn_w_out, mix_norm, a_w_qkv, a_q_norm, a_k_norm, a_w_o, kv_norm, kv_w, kv_b_f, kv_k_norm, b_w_q, b_q_norm, b_w_o, loss_target, m_ffn_norm, m_ffn_w_in, m_ffn_w_out, m_mix_norm, m_a_w_qkv, m_a_q_norm, m_a_k_norm, m_a_w_o, m_kv_norm, m_kv_w, m_kv_b_f, m_kv_k_norm, m_b_w_q, m_b_q_norm, m_b_w_o, v_ffn_norm, v_ffn_w_in, v_ffn_w_out, v_mix_norm, v_a_w_qkv, v_a_q_norm, v_a_k_norm, v_a_w_o, v_kv_norm, v_kv_w, v_kv_b_f, v_kv_k_norm, v_b_w_q, v_b_q_norm, v_b_w_o):
    names = ["ffn_norm", "ffn_w_in", "ffn_w_out", "mix_norm", "a_w_qkv", "a_q_norm", "a_k_norm", "a_w_o", "kv_norm",
             "kv_w", "kv_b_f", "kv_k_norm", "b_w_q", "b_q_norm", "b_w_o"]
    w = dict(zip(names, (ffn_norm, ffn_w_in, ffn_w_out, mix_norm, a_w_qkv, a_q_norm, a_k_norm, a_w_o, kv_norm, kv_w,
                         kv_b_f, kv_k_norm, b_w_q, b_q_norm, b_w_o)))
    m = dict(zip(names, (m_ffn_norm, m_ffn_w_in, m_ffn_w_out, m_mix_norm, m_a_w_qkv, m_a_q_norm, m_a_k_norm, m_a_w_o,
                         m_kv_norm, m_kv_w, m_kv_b_f, m_kv_k_norm, m_b_w_q, m_b_q_norm, m_b_w_o)))
    v = dict(zip(names, (v_ffn_norm, v_ffn_w_in, v_ffn_w_out, v_mix_norm, v_a_w_qkv, v_a_q_norm, v_a_k_norm, v_a_w_o,
                         v_kv_norm, v_kv_w, v_kv_b_f, v_kv_k_norm, v_b_w_q, v_b_q_norm, v_b_w_o)))
    B, S, D = x.shape
    H = D // HEAD_DIM
    F = ffn_w_out.shape[2] * N_DEV
    Ds, Fs = D // N_DEV, F // N_DEV
    me = 4 * lax.axis_index("x") + 2 * lax.axis_index("y") + lax.axis_index("c")
    squares = ("a_w_o", "b_w_q", "b_w_o")

    def shard_views(t):
        return [t["ffn_w_in"].reshape(4 * D, -1), t["ffn_w_out"].reshape(-1, D), t["a_w_qkv"][0], t["kv_w"],
                jnp.concatenate([t[n][0] for n in squares], axis=0)]

    def to_shards(views):
        sq3 = views[4].reshape(3, 1, Ds, D)
        out = {"ffn_w_in": views[0].reshape(ffn_w_in.shape), "ffn_w_out": views[1].reshape(ffn_w_out.shape),
               "a_w_qkv": views[2][None], "kv_w": views[3]}
        out.update({n: sq3[i] for i, n in enumerate(squares)})
        return out

    norm_rows = jnp.pad(ffn_norm.reshape(4, Ds), ((0, 4), (0, 0)))
    g_in, g_out, g_qkv, g_kv, g_sq, g_norm = _all_gather(
        [s.astype(BF16) for s in shard_views(w)] + [norm_rows], "gather_weights")
    W = {n: w[n] for n in SMALL}
    W["ffn_norm"] = g_norm[:, :4].transpose(1, 0, 2).reshape(2, 2, D)
    W["ffn_w_in"] = g_in.transpose(1, 0, 2).reshape(2, 2, D, 2 * F)
    W["ffn_w_out"] = g_out.reshape(N_DEV, 4, Fs, D).transpose(1, 0, 2, 3).reshape(2, 2, F, D)
    W["a_w_qkv"] = g_qkv.transpose(1, 0, 2).reshape(D, 9 * D)
    kv_full = g_kv.transpose(1, 0, 2).reshape(D, 2 * D + H)
    W["kv_w_kv"] = kv_full[:, :2 * D]
    W["kv_w_f"] = jnp.pad(kv_full[:, 2 * D:], ((0, 0), (0, LANES - H)))
    sq_full = g_sq.reshape(N_DEV, 3, Ds, D).transpose(1, 0, 2, 3).reshape(3, D, D)
    W["a_w_o"], W["b_w_q"], W["b_w_o"] = sq_full[0], sq_full[1], sq_full[2]

    sq, grad_x, G = _local_step(x, positions, loss_target, W)
    loss = lax.psum(sq * (0.5 / D), ("x", "y", "c"))

    blocks = [G["ffn_w_in"].reshape(4 * D, N_DEV, -1).transpose(1, 0, 2),
              G["ffn_w_out"].reshape(4, N_DEV, Fs, D).transpose(1, 0, 2, 3).reshape(N_DEV, 4 * Fs, D),
              G["a_w_qkv"][0].reshape(D, N_DEV, -1).transpose(1, 0, 2),
              G["kv_w"].reshape(D, N_DEV, -1).transpose(1, 0, 2),
              jnp.concatenate([G[n][0].reshape(N_DEV, Ds, D) for n in squares], axis=1)]
    blocks = [b.astype(BF16) for b in blocks]
    got = _pair_exchange(blocks, "exchange_pair")
    parts = _chip_exchange([_pair_sum(b, s, f"pair_sum{i}") for i, (b, s) in enumerate(zip(blocks, got))], "exchange_chips")
    views = [shard_views(t) for t in (w, m, v)]
    outs = [_adamw(parts[i], views[0][i], views[1][i], views[2][i], f"adamw_matrix{i}") for i in range(len(parts))]
    new = {}
    for k, kind in enumerate(("grad", "delta", "new_m", "new_v")):
        for n, arr in to_shards([o[k] for o in outs]).items():
            new[kind, n] = arr

    small_all = ["ffn_norm"] + SMALL
    vec_parts = _all_gather([_pack([G[n] for n in small_all], 0)], "gather_vector_grads")[0]
    vec_parts = _unpack(vec_parts, [G[n].shape for n in small_all], 1)
    norm_g = lax.dynamic_slice_in_dim(vec_parts[0], me * ffn_norm.shape[2], ffn_norm.shape[2], axis=3)
    rep_g = _pack(vec_parts[1:], 1)
    for ns, parts_s in ((["ffn_norm"], _pack([norm_g], 1)), (SMALL, rep_g)):
        outs = _adamw(parts_s, *(_pack([t[n] for n in ns], 0) for t in (w, m, v)), "adamw_" + ns[0])
        for kind, packed in zip(("grad", "delta", "new_m", "new_v"), outs):
            for n, arr in zip(ns, _unpack(packed, [w[n].shape for n in ns], 0)):
                new[kind, n] = arr

    result = [loss, grad_x]
    for kind in ("grad", "delta", "new_m", "new_v"):
        result += [new[kind, n] for n in names]
    return tuple(result)
```

```python
import jax
import jax.numpy as jnp
import numpy as np
from jax import lax
from jax.experimental import pallas as pl
from jax.experimental.pallas import tpu as pltpu

F32 = jnp.float32
BF16 = jnp.bfloat16
HIGHEST = lax.Precision.HIGHEST

HEAD_DIM = 64
ROT_DIM = HEAD_DIM // 4
ROPE_THETA = 500000.0
DILATED_GROUPS = ((128, 1), (512, 4), (2048, 16))
BAND_BLOCK = 128
EPS = 1e-6
ADAM_LR, ADAM_B1, ADAM_B2, ADAM_EPS, ADAM_WD, ADAM_STEP = 0.001, 0.9, 0.999, 1e-08, 0.01, 10

LANES = 128
N_DEV = 8
VMEM_LIMIT = 56 * 1024 * 1024
NEG = -1e30
SCALE = HEAD_DIM ** -0.5
PACK_ROWS = 1024
MESH_ID = pl.DeviceIdType.MESH


def _params(*sem):
    return pltpu.CompilerParams(dimension_semantics=sem, vmem_limit_bytes=VMEM_LIMIT)


def _divs(n, cap, mult=LANES):
    d = [t for t in range(mult, min(n, cap) + 1, mult) if n % t == 0]
    return d or [n]


def _pick(n, cap, mult=LANES):
    return _divs(n, cap, mult)[-1]


def _mm_tiles(M, N, K, budget=1408 * 1024):
    best = None
    for tm in _divs(M, 1408):
        for tn in _divs(N, 1408):
            if tm * tn <= budget and (best is None or tm * tn > best[0] * best[1]):
                best = (tm, tn)
    if best is None:
        best = (_divs(M, 1408)[0], _divs(N, 1408)[0])
    tm, tn = best
    tks = [t for t in _divs(K, 2048) if (tm + tn) * t <= 3 * 1024 * 1024]
    tk = tks[-1] if tks else _divs(K, 2048)[0]
    return tm, tn, tk


def _dot(a, b, dims, precision=None):
    return lax.dot_general(a, b, (dims, ((), ())), precision=precision, preferred_element_type=F32)


NN = ((1,), (0,))
NT = ((1,), (1,))
TN = ((0,), (0,))


def _dot_split(x, mat):
    hi = x.astype(BF16)
    lo = (x - hi.astype(F32)).astype(BF16)
    m = mat.astype(BF16)
    return _dot(hi, m, NN) + _dot(lo, m, NN)


def _dot_once(x, mat):
    return _dot(x.astype(BF16), mat.astype(BF16), NN)


def _mm(a, b, mode, *, M, N, K, a_off=(0, 0), b_off=(0, 0), res=None, scale=1.0, out_dtype=F32, name):
    tm, tn, tk = _mm_tiles(M, N, K)
    nk = K // tk
    if mode == "tn":
        a_blk, b_blk, dims = (tk, tm), (tk, tn), TN
    elif mode == "nt":
        a_blk, b_blk, dims = (tm, tk), (tn, tk), NT
    else:
        a_blk, b_blk, dims = (tm, tk), (tk, tn), NN
    for off, blk in ((a_off, a_blk), (b_off, b_blk)):
        assert off[0] % blk[0] == 0 and off[1] % blk[1] == 0, (name, off, blk)
    ao = (a_off[0] // a_blk[0], a_off[1] // a_blk[1])
    bo = (b_off[0] // b_blk[0], b_off[1] // b_blk[1])
    if mode == "tn":
        a_map = lambda i, j, k: (k + ao[0], i + ao[1])
        b_map = lambda i, j, k: (k + bo[0], j + bo[1])
    elif mode == "nt":
        a_map = lambda i, j, k: (i + ao[0], k + ao[1])
        b_map = lambda i, j, k: (j + bo[0], k + bo[1])
    else:
        a_map = lambda i, j, k: (i + ao[0], k + ao[1])
        b_map = lambda i, j, k: (k + bo[0], j + bo[1])
    has_res = res is not None

    def body(*refs):
        a_ref, b_ref = refs[:2]
        r_ref = refs[2] if has_res else None
        o_ref = refs[3] if has_res else refs[2]

        def finish(total):
            out = total * scale
            if has_res:
                out = r_ref[...] + out
            o_ref[...] = out.astype(out_dtype)

        part = _dot(a_ref[...].astype(BF16), b_ref[...].astype(BF16), dims)
        if nk == 1:
            finish(part)
            return
        acc = refs[-1]
        k = pl.program_id(2)

        @pl.when(k == 0)
        def _():
            acc[...] = part

        @pl.when(k > 0)
        def _():
            acc[...] += part

        pl.when(k == nk - 1)(lambda: finish(acc[...]))

    in_specs = [pl.BlockSpec(a_blk, a_map), pl.BlockSpec(b_blk, b_map)]
    args = [a, b]
    if has_res:
        in_specs.append(pl.BlockSpec((tm, tn), lambda i, j, k: (i, j)))
        args.append(res)
    return pl.pallas_call(
        body, name=name, grid=(M // tm, N // tn, nk), in_specs=in_specs,
        out_specs=pl.BlockSpec((tm, tn), lambda i, j, k: (i, j)),
        out_shape=jax.ShapeDtypeStruct((M, N), out_dtype),
        scratch_shapes=[pltpu.VMEM((tm, tn), F32)] if nk > 1 else [],
        compiler_params=_params("parallel", "parallel", "arbitrary"),
    )(*args)


def _rmsnorm_fwd(x, g, name):
    T, D = x.shape
    tm = _pick(T, 256, 8)

    def body(x_ref, g_ref, y_ref):
        xv = x_ref[...]
        r = lax.rsqrt(jnp.mean(xv * xv, axis=-1, keepdims=True) + EPS)
        y_ref[...] = (xv * r * g_ref[...]).astype(BF16)

    return pl.pallas_call(
        body, name=name, grid=(T // tm,),
        in_specs=[pl.BlockSpec((tm, D), lambda i: (i, 0)), pl.BlockSpec((1, D), lambda i: (0, 0))],
        out_specs=pl.BlockSpec((tm, D), lambda i: (i, 0)),
        out_shape=jax.ShapeDtypeStruct((T, D), BF16), compiler_params=_params("parallel"),
    )(x, g.reshape(1, D))


def _rmsnorm_bwd(x, g, dy, dres, name):
    T, D = x.shape
    tm = _pick(T, 256, 8)

    def body(x_ref, g_ref, dy_ref, dres_ref, dx_ref, dg_ref):
        xv = x_ref[...]
        r = lax.rsqrt(jnp.mean(xv * xv, axis=-1, keepdims=True) + EPS)
        xn = xv * r
        d = dy_ref[...]
        dgy = d * g_ref[...]
        dx_ref[...] = dres_ref[...] + r * (dgy - xn * jnp.mean(dgy * xn, axis=-1, keepdims=True))

        @pl.when(pl.program_id(0) == 0)
        def _():
            dg_ref[...] = jnp.zeros_like(dg_ref)

        dg_ref[...] += jnp.sum(d * xn, axis=0, keepdims=True)

    row = pl.BlockSpec((tm, D), lambda i: (i, 0))
    vec = pl.BlockSpec((1, D), lambda i: (0, 0))
    dx, dg = pl.pallas_call(
        body, name=name, grid=(T // tm,), in_specs=[row, vec, row, row], out_specs=[row, vec],
        out_shape=[jax.ShapeDtypeStruct((T, D), F32), jax.ShapeDtypeStruct((1, D), F32)],
        compiler_params=_params("arbitrary"),
    )(x, g.reshape(1, D), dy, dres)
    return dx, dg.reshape(D)


def _sigmoid(x):
    return 1.0 / (1.0 + jnp.exp(-x))


def _mm_swiglu_fwd(hn, w_in, name):
    T, D = hn.shape
    F = w_in.shape[1] // 2
    tm, tn, _ = _mm_tiles(T, F, D, budget=768 * 1024)
    assert (tm + 2 * tn) * D <= 4 * 1024 * 1024, "the whole contraction is held in one step"
    nf = F // tn

    def body(a_ref, bg_ref, bu_ref, g_ref, u_ref, act_ref):
        a = a_ref[...].astype(BF16)
        gv = _dot(a, bg_ref[...].astype(BF16), NN)
        uv = _dot(a, bu_ref[...].astype(BF16), NN)
        g_ref[...] = gv
        u_ref[...] = uv
        act_ref[...] = (gv * _sigmoid(gv) * uv).astype(BF16)

    tile = pl.BlockSpec((tm, tn), lambda j, i: (i, j))
    f32 = jax.ShapeDtypeStruct((T, F), F32)
    return pl.pallas_call(
        body, name=name, grid=(nf, T // tm),
        in_specs=[pl.BlockSpec((tm, D), lambda j, i: (i, 0)), pl.BlockSpec((D, tn), lambda j, i: (0, j)),
                  pl.BlockSpec((D, tn), lambda j, i: (0, j + nf))],
        out_specs=[tile, tile, tile], out_shape=[f32, f32, jax.ShapeDtypeStruct((T, F), BF16)],
        compiler_params=_params("parallel", "parallel"),
    )(hn, w_in, w_in)


def _mm_swiglu_bwd(dout, w_out, gate, up, name):
    T, D = dout.shape
    F = w_out.shape[0]
    tm, tn, _ = _mm_tiles(T, F, D, budget=768 * 1024)
    assert (tm + tn) * D <= 4 * 1024 * 1024, "the whole contraction is held in one step"

    def body(a_ref, b_ref, g_ref, u_ref, dg_ref, du_ref):
        d = _dot(a_ref[...].astype(BF16), b_ref[...].astype(BF16), NT) * 0.5
        gv, uv = g_ref[...], u_ref[...]
        s = _sigmoid(gv)
        gs = gv * s
        dg_ref[...] = (d * uv * (s + gs - gs * s)).astype(BF16)
        du_ref[...] = (d * gs).astype(BF16)

    tile = pl.BlockSpec((tm, tn), lambda j, i: (i, j))
    out = jax.ShapeDtypeStruct((T, F), BF16)
    return pl.pallas_call(
        body, name=name, grid=(F // tn, T // tm),
        in_specs=[pl.BlockSpec((tm, D), lambda j, i: (i, 0)), pl.BlockSpec((tn, D), lambda j, i: (j, 0)), tile, tile],
        out_specs=[tile, tile], out_shape=[out, out], compiler_params=_params("parallel", "parallel"),
    )(dout, w_out, gate, up)


def _iota2(shape, dim):
    return lax.broadcasted_iota(jnp.int32, shape, dim)


def _head_ones():
    r, c = _iota2((LANES, LANES), 0), _iota2((LANES, LANES), 1)
    return ((r >> 6) == (c >> 6)).astype(F32)


def _rot_matrix(transpose):
    r, c = _iota2((LANES, LANES), 0), _iota2((LANES, LANES), 1)
    if transpose:
        r, c = c, r
    half = ROT_DIM // 2
    cm = c & (HEAD_DIM - 1)
    neg = (r == c + half) & (cm < half)
    pos = (r == c - half) & (cm >= half) & (cm < ROT_DIM)
    return pos.astype(F32) - neg.astype(F32)


def _rope_tables(positions, name):
    T = positions.size
    tm = _pick(T, 1024, 8)
    inv = ROPE_THETA ** (-jnp.arange(0, ROT_DIM, 2, dtype=F32) / ROT_DIM)
    lane = jnp.concatenate([inv, inv, jnp.zeros((HEAD_DIM - ROT_DIM,), F32)])
    lane = jnp.tile(lane, 2).reshape(1, LANES)

    def body(p_ref, f_ref, c_ref, s_ref):
        ang = p_ref[...].astype(F32) * f_ref[...]
        c_ref[...] = jnp.cos(ang)
        s_ref[...] = jnp.sin(ang)

    out = jax.ShapeDtypeStruct((T, LANES), F32)
    blk = pl.BlockSpec((tm, LANES), lambda i: (i, 0))
    return pl.pallas_call(
        body, name=name, grid=(T // tm,),
        in_specs=[pl.BlockSpec((tm, 1), lambda i: (i, 0)), pl.BlockSpec((1, LANES), lambda i: (0, 0))],
        out_specs=[blk, blk], out_shape=[out, out], compiler_params=_params("parallel"),
    )(positions.reshape(T, 1), lane)


def _qknorm_fwd(src, col_off, gain, cs, D, out_dtype, name, out_scale=1.0):
    T = src.shape[0]
    HP = D // LANES
    tm = _pick(T, 1024, 8)
    co = col_off // LANES
    rope = cs is not None

    def body(*refs):
        if rope:
            x_ref, g_ref, c_ref, s_ref, o_ref = refs
        else:
            x_ref, g_ref, o_ref = refs
        xv = x_ref[...]
        ms = _dot_once(xv * xv, _head_ones()) * (1.0 / HEAD_DIM)
        y = xv * lax.rsqrt(ms + EPS) * g_ref[...]
        if rope:
            y = y * c_ref[...] + _dot_once(y, _rot_matrix(False)) * s_ref[...]
        o_ref[...] = (y * out_scale).astype(out_dtype)

    in_specs = [pl.BlockSpec((tm, LANES), lambda i, h: (i, co + h)), pl.BlockSpec((1, LANES), lambda i, h: (0, 0))]
    args = [src, jnp.tile(gain.reshape(1, HEAD_DIM), (1, 2))]
    if rope:
        in_specs += [pl.BlockSpec((tm, LANES), lambda i, h: (i, 0))] * 2
        args += list(cs)
    return pl.pallas_call(
        body, name=name, grid=(T // tm, HP), in_specs=in_specs,
        out_specs=pl.BlockSpec((tm, LANES), lambda i, h: (i, h)),
        out_shape=jax.ShapeDtypeStruct((T, D), out_dtype), compiler_params=_params("parallel", "parallel"),
    )(*args)


def _qknorm_bwd(src, col_off, gain, cs, dout, D, name, out_scale=1.0):
    T = src.shape[0]
    HP = D // LANES
    tm = _pick(T, 1024, 8)
    co = col_off // LANES
    rope = cs is not None

    def body(*refs):
        if rope:
            x_ref, g_ref, d_ref, c_ref, s_ref, dx_ref, dg_ref = refs
        else:
            x_ref, g_ref, d_ref, dx_ref, dg_ref = refs
        xv = x_ref[...]
        ones = _head_ones()
        ms = _dot_once(xv * xv, ones) * (1.0 / HEAD_DIM)
        r = lax.rsqrt(ms + EPS)
        xn = xv * r
        d = d_ref[...] * out_scale
        if rope:
            d = d * c_ref[...] + _dot_once(d * s_ref[...], _rot_matrix(True))
        dgy = d * g_ref[...]
        mean = _dot_once(dgy * xn, ones) * (1.0 / HEAD_DIM)
        dx_ref[...] = (r * (dgy - xn * mean)).astype(BF16)

        @pl.when(pl.program_id(1) == 0)
        def _():
            dg_ref[...] = jnp.zeros_like(dg_ref)

        dg_ref[...] += jnp.sum(d * xn, axis=0, keepdims=True)[None]

    in_specs = [pl.BlockSpec((tm, LANES), lambda h, i: (i, co + h)), pl.BlockSpec((1, LANES), lambda h, i: (0, 0)),
                pl.BlockSpec((tm, LANES), lambda h, i: (i, h))]
    args = [src, jnp.tile(gain.reshape(1, HEAD_DIM), (1, 2)), dout]
    if rope:
        in_specs += [pl.BlockSpec((tm, LANES), lambda h, i: (i, 0))] * 2
        args += list(cs)
    dx, dg = pl.pallas_call(
        body, name=name, grid=(HP, T // tm), in_specs=in_specs,
        out_specs=[pl.BlockSpec((tm, LANES), lambda h, i: (i, h)), pl.BlockSpec((1, 1, LANES), lambda h, i: (h, 0, 0))],
        out_shape=[jax.ShapeDtypeStruct((T, D), BF16), jax.ShapeDtypeStruct((HP, 1, LANES), F32)],
        compiler_params=_params("parallel", "arbitrary"),
    )(*args)
    return dx, dg.reshape(2 * HP, HEAD_DIM).sum(axis=0)


def _head_rowsum(a, b, name):
    T, D = a.shape
    tm = _pick(T, 1024, 8)

    def body(a_ref, b_ref, o_ref):
        o_ref[...] = _dot_split(a_ref[...] * b_ref[...], _head_ones())

    blk = pl.BlockSpec((tm, LANES), lambda i, h: (i, h))
    return pl.pallas_call(
        body, name=name, grid=(T // tm, D // LANES), in_specs=[blk, blk], out_specs=blk,
        out_shape=jax.ShapeDtypeStruct((T, D), F32), compiler_params=_params("parallel", "parallel"),
    )(a, b)


def _lane_masks():
    lane = _iota2((1, LANES), 1)
    return (lane < HEAD_DIM, lane >= HEAD_DIM)


BAND_GROUP = 4


def _band_geometry(g, S):
    window, dil = DILATED_GROUPS[g]
    chunk = BAND_BLOCK * dil
    assert window // dil == BAND_BLOCK and S % chunk == 0
    if dil == 1:
        assert (S // chunk) % BAND_GROUP == 0
        return dil, chunk, BAND_GROUP, 1
    assert dil % BAND_GROUP == 0
    return dil, chunk, 1, dil // BAND_GROUP


def _band_rows(i, rho, dil, shift=0):
    if dil > 1:
        return pl.ds(rho * BAND_GROUP + i, BAND_BLOCK, stride=dil), shift != 0
    j = i + shift
    outside = j < 0 or j >= BAND_GROUP
    return pl.ds(0 if outside else j * BAND_BLOCK, BAND_BLOCK), outside


def _stack_heads(x):
    hm0, hm1 = _lane_masks()
    zero = jnp.zeros_like(x)
    return jnp.concatenate([jnp.where(hm0, x, zero), jnp.where(hm1, x, zero)], axis=0)


def _unstack_heads(x2):
    hm0, _ = _lane_masks()
    return jnp.where(hm0, x2[:BAND_BLOCK], x2[BAND_BLOCK:])


def _band_masks():
    qi = _iota2((2 * BAND_BLOCK, 2 * BAND_BLOCK), 0) & (BAND_BLOCK - 1)
    kj = _iota2((2 * BAND_BLOCK, 2 * BAND_BLOCK), 1)
    own = (kj >= BAND_BLOCK) & (kj - BAND_BLOCK <= qi)
    prev = (kj < BAND_BLOCK) & (kj >= qi)
    return own, prev


def _band_fwd(qh, kh, qkv, g, B, S, D, name):
    dil, CH, NB, RG = _band_geometry(g, S)
    HP = D // LANES
    vo = (3 * g + 2) * HP
    nsp = S // (CH * NB)

    def body(q_ref, k_ref, kp_ref, v_ref, vp_ref, o_ref, lse_ref):
        n, rho = pl.program_id(1), pl.program_id(3)
        own, prev = _band_masks()
        for i in range(BAND_GROUP):
            rows, _ = _band_rows(i, rho, dil)
            prows, outside = _band_rows(i, rho, dil, shift=-1)
            q, kc, vc = (t[0, rows, :].astype(BF16) for t in (q_ref, k_ref, v_ref))
            kp, vp = (t[0, prows, :].astype(BF16) for t in ((kp_ref, vp_ref) if outside else (k_ref, v_ref)))
            valid = (own | (prev & (n > 0))) if outside else (own | prev)
            s = jnp.where(valid, _dot(_stack_heads(q), jnp.concatenate([kp, kc], axis=0), NT), NEG)
            m = jnp.max(s, axis=-1, keepdims=True)
            e = jnp.exp(s - m)
            l = jnp.sum(e, axis=-1, keepdims=True)
            o2 = _dot((e * (1.0 / l)).astype(BF16), jnp.concatenate([vp, vc], axis=0), NN)
            o_ref[0, rows, :] = _unstack_heads(o2)
            lse_ref[0, rows, :] = _unstack_heads(m + jnp.log(l))

    span, chunk = (1, CH * NB, LANES), (1, CH, LANES)
    cur = lambda b, n, h, r: (b, n, h)
    prev = lambda b, n, h, r: (b, jnp.maximum(n * NB - 1, 0), h)
    vcur = lambda b, n, h, r: (b, n, vo + h)
    vprev = lambda b, n, h, r: (b, jnp.maximum(n * NB - 1, 0), vo + h)
    out = jax.ShapeDtypeStruct((B, S, D), F32)
    q3, k3, v3 = qh.reshape(B, S, D), kh.reshape(B, S, D), qkv.reshape(B, S, 9 * D)
    o, lse = pl.pallas_call(
        body, name=name, grid=(B, nsp, HP, RG),
        in_specs=[pl.BlockSpec(span, cur), pl.BlockSpec(span, cur), pl.BlockSpec(chunk, prev),
                  pl.BlockSpec(span, vcur), pl.BlockSpec(chunk, vprev)],
        out_specs=[pl.BlockSpec(span, cur), pl.BlockSpec(span, cur)], out_shape=[out, out],
        compiler_params=_params("parallel", "parallel", "parallel", "arbitrary"),
    )(q3, k3, k3, v3, v3)
    return o.reshape(B * S, D), lse.reshape(B * S, D)


def _head_col(x, h):
    lane = _iota2((1, LANES), 1)
    return jnp.sum(jnp.where(lane == h * HEAD_DIM, x, 0.0), axis=-1, keepdims=True)


def _band_bwd(qh, kh, qkv, do, lse, delta, g, B, S, D, name):
    dil, CH, NB, RG = _band_geometry(g, S)
    HP = D // LANES
    vo = (3 * g + 2) * HP
    nsp, nch = S // (CH * NB), S // CH

    def stacked_cols(tile):
        return jnp.concatenate([_head_col(tile, 0), _head_col(tile, 1)], axis=0)

    def body(q_ref, qn_ref, do_ref, don_ref, l_ref, ln_ref, d_ref, dn_ref, k_ref, kp_ref, v_ref, vp_ref,
             dq_ref, dk_ref, dv_ref):
        n, rho = pl.program_id(1), pl.program_id(3)
        own, prev = _band_masks()
        band = (_iota2((2 * BAND_BLOCK, BAND_BLOCK), 1) >= (_iota2((2 * BAND_BLOCK, BAND_BLOCK), 0) & (BAND_BLOCK - 1)))
        for i in range(BAND_GROUP):
            rows, _ = _band_rows(i, rho, dil)
            prows, p_out = _band_rows(i, rho, dil, shift=-1)
            nrows, n_out = _band_rows(i, rho, dil, shift=1)
            kc, vc = (t[0, rows, :].astype(BF16) for t in (k_ref, v_ref))
            kp, vp = (t[0, prows, :].astype(BF16) for t in ((kp_ref, vp_ref) if p_out else (k_ref, v_ref)))
            q2, do2 = (_stack_heads(t[0, rows, :].astype(BF16)) for t in (q_ref, do_ref))
            qn2, don2 = (_stack_heads(t[0, nrows, :].astype(BF16)) for t in ((qn_ref, don_ref) if n_out else (q_ref, do_ref)))
            lse2, del2 = (stacked_cols(t[0, rows, :]) for t in (l_ref, d_ref))
            lsen2, deln2 = (stacked_cols(t[0, nrows, :]) for t in ((ln_ref, dn_ref) if n_out else (l_ref, d_ref)))
            valid = (own | (prev & (n > 0))) if p_out else (own | prev)
            next_ok = (band & (n < nsp - 1)) if n_out else band
            k2, v2 = jnp.concatenate([kp, kc], axis=0), jnp.concatenate([vp, vc], axis=0)
            p = jnp.exp(jnp.where(valid, _dot(q2, k2, NT) - lse2, NEG))
            ds = (p * (_dot(do2, v2, NT) - del2)).astype(BF16)
            pc = jnp.exp(jnp.where(next_ok, _dot(qn2, kc, NT) - lsen2, NEG))
            dsc = (pc * (_dot(don2, vc, NT) - deln2)).astype(BF16)
            dq_ref[0, rows, :] = _unstack_heads(_dot(ds, k2, NN))
            dk_ref[0, rows, :] = _dot(ds[:, BAND_BLOCK:], q2, TN) + _dot(dsc, qn2, TN)
            dv_ref[0, rows, :] = _dot(p[:, BAND_BLOCK:].astype(BF16), do2, TN) + _dot(pc.astype(BF16), don2, TN)

    span, chunk = (1, CH * NB, LANES), (1, CH, LANES)
    cur = lambda b, n, h, r: (b, n, h)
    prev = lambda b, n, h, r: (b, jnp.maximum(n * NB - 1, 0), h)
    nxt = lambda b, n, h, r: (b, jnp.minimum((n + 1) * NB, nch - 1), h)
    vcur = lambda b, n, h, r: (b, n, vo + h)
    vprev = lambda b, n, h, r: (b, jnp.maximum(n * NB - 1, 0), vo + h)
    q3, k3, do3, l3, d3 = (t.reshape(B, S, D) for t in (qh, kh, do, lse, delta))
    v3 = qkv.reshape(B, S, 9 * D)
    sp, ch = (lambda m: pl.BlockSpec(span, m)), (lambda m: pl.BlockSpec(chunk, m))
    out = jax.ShapeDtypeStruct((B, S, D), F32)
    dq, dk, dv = pl.pallas_call(
        body, name=name, grid=(B, nsp, HP, RG),
        in_specs=[sp(cur), ch(nxt), sp(cur), ch(nxt), sp(cur), ch(nxt), sp(cur), ch(nxt),
                  sp(cur), ch(prev), sp(vcur), ch(vprev)],
        out_specs=[sp(cur)] * 3, out_shape=[out] * 3,
        compiler_params=_params("parallel", "parallel", "parallel", "arbitrary"),
    )(q3, q3, do3, do3, l3, l3, d3, d3, k3, k3, v3, v3)
    return dq.reshape(B * S, D), dk.reshape(B * S, D), dv.reshape(B * S, D)


def _mix(os_, lses, name):
    T, D = os_[0].shape
    tm = _pick(T, 256, 8)

    def body(o0, o1, o2, l0, l1, l2, mix_ref, lse_ref):
        a, b, c = l0[...], l1[...], l2[...]
        m = jnp.maximum(jnp.maximum(a, b), c)
        ea, eb, ec = jnp.exp(a - m), jnp.exp(b - m), jnp.exp(c - m)
        s = ea + eb + ec
        mix_ref[...] = (ea / s) * o0[...] + (eb / s) * o1[...] + (ec / s) * o2[...]
        lse_ref[...] = m + jnp.log(s)

    blk = pl.BlockSpec((tm, D), lambda i: (i, 0))
    out = jax.ShapeDtypeStruct((T, D), F32)
    return pl.pallas_call(
        body, name=name, grid=(T // tm,), in_specs=[blk] * 6, out_specs=[blk, blk], out_shape=[out, out],
        compiler_params=_params("parallel"),
    )(*os_, *lses)


def _gate_fwd(fpre, b_f, B, S, D, name):
    nb = S // LANES

    def body(f_ref, b_ref, cum_ref, rep_ref, carry):
        @pl.when(pl.program_id(1) == 0)
        def _():
            carry[...] = jnp.zeros_like(carry)

        z = f_ref[...] + b_ref[...]
        lf = jnp.minimum(z, 0.0) - jnp.log(1.0 + jnp.exp(-jnp.abs(z)))
        tri = (_iota2((LANES, LANES), 0) >= _iota2((LANES, LANES), 1)).astype(F32)
        c = _dot(tri, lf, NN, HIGHEST) + carry[...]
        cum_ref[...] = c
        carry[...] += jnp.sum(lf, axis=0, keepdims=True)
        spread = ((_iota2((LANES, D), 1) >> 6) == _iota2((LANES, D), 0)).astype(F32)
        rep_ref[...] = _dot(c, spread, NN, HIGHEST)

    return pl.pallas_call(
        body, name=name, grid=(B, nb),
        in_specs=[pl.BlockSpec((LANES, LANES), lambda b, j: (b * nb + j, 0)), pl.BlockSpec((1, LANES), lambda b, j: (0, 0))],
        out_specs=[pl.BlockSpec((LANES, LANES), lambda b, j: (b * nb + j, 0)), pl.BlockSpec((LANES, D), lambda b, j: (b * nb + j, 0))],
        out_shape=[jax.ShapeDtypeStruct((B * S, LANES), F32), jax.ShapeDtypeStruct((B * S, D), F32)],
        scratch_shapes=[pltpu.VMEM((1, LANES), F32)], compiler_params=_params("arbitrary", "arbitrary"),
    )(fpre, b_f)


def _gate_bwd(dcum_k, dcum_q, fpre, b_f, B, S, name):
    nb = S // LANES

    def body(d_ref, dq_ref, f_ref, b_ref, o_ref, db_ref, carry):
        first = (pl.program_id(0) == 0) & (pl.program_id(1) == 0)

        @pl.when(pl.program_id(1) == 0)
        def _():
            carry[...] = jnp.zeros_like(carry)

        @pl.when(first)
        def _():
            db_ref[...] = jnp.zeros_like(db_ref)

        d = d_ref[...] + dq_ref[...]
        tri = (_iota2((LANES, LANES), 0) <= _iota2((LANES, LANES), 1)).astype(F32)
        dlf = _dot(tri, d, NN, HIGHEST) + carry[...]
        carry[...] += jnp.sum(d, axis=0, keepdims=True)
        z = f_ref[...] + b_ref[...]
        dz = dlf * (1.0 / (1.0 + jnp.exp(z)))
        o_ref[...] = dz.astype(BF16)
        db_ref[...] += jnp.sum(dz, axis=0, keepdims=True)

    rev = lambda b, j: (b * nb + nb - 1 - j, 0)
    blk = pl.BlockSpec((LANES, LANES), rev)
    vec = pl.BlockSpec((1, LANES), lambda b, j: (0, 0))
    return pl.pallas_call(
        body, name=name, grid=(B, nb), in_specs=[blk, blk, blk, vec], out_specs=[blk, vec],
        out_shape=[jax.ShapeDtypeStruct((B * S, LANES), BF16), jax.ShapeDtypeStruct((1, LANES), F32)],
        scratch_shapes=[pltpu.VMEM((1, LANES), F32)], compiler_params=_params("arbitrary", "arbitrary"),
    )(dcum_k, dcum_q, fpre, b_f)


def _fox_scores(q, k, ck_row, hm, diagonal):
    TB = q.shape[0]
    qm = jnp.where(hm, q, jnp.zeros_like(q))
    t = _dot(qm, k, NT) - ck_row
    if diagonal:
        t = jnp.where(_iota2((TB, TB), 1) <= _iota2((TB, TB), 0), t, NEG)
    return qm, t


def _fox_fetch(srcs, bufs, sems, slot):
    return [pltpu.make_async_copy(src, buf.at[slot], sems.at[slot, i]) for i, (src, buf) in enumerate(zip(srcs, bufs))]


FOX_BLOCK = 256
EXP_UNDERFLOW = 104.0


def _fox_live_blocks(cum, gain_q, gain_k, B, S, D):
    H, HP = D // HEAD_DIM, D // LANES
    nb = S // FOX_BLOCK
    bound = 1.02 * HEAD_DIM * SCALE * jnp.max(jnp.abs(gain_q)) * jnp.max(jnp.abs(gain_k))
    c = cum.reshape(B, S, LANES)[:, :, :H]
    decay = c[:, 0::FOX_BLOCK, None, :] - c[:, None, FOX_BLOCK - 1::FOX_BLOCK, :]
    below = jnp.arange(nb)[None, :] < jnp.arange(nb)[:, None]
    dead = (decay < -(EXP_UNDERFLOW + 2.0 * bound)) & below[None, :, :, None]
    live = jnp.logical_not(dead.reshape(B, nb, nb, HP, 2).all(axis=-1))
    first = jnp.argmax(live, axis=2)
    last = nb - 1 - jnp.argmax(live[:, ::-1], axis=1)
    to_table = lambda t: t.transpose(0, 2, 1).reshape(B * HP * nb).astype(jnp.int32)
    return to_table(first), to_table(last)


def _fox_fwd(qh, kh, kv, cq, ck, first, B, S, D, name):
    HP = D // LANES
    TB = FOX_BLOCK
    nb = S // TB

    def body(first_ref, q_ref, cq_ref, k_ref, v_ref, ck_ref, kp_ref, vp_ref, ckp_ref, k_hbm, v_hbm, ck_hbm,
             o_ref, lse_ref, kbuf, vbuf, cbuf, sems, m_sc, l_sc, acc):
        b, qb, hp = pl.program_id(0), pl.program_id(1), pl.program_id(2)
        kb0 = first_ref[(b * HP + hp) * nb + qb]
        n_far = jnp.maximum(qb - 1 - kb0, 0)

        def fetch(kb, slot):
            rows = pl.ds(pl.multiple_of(kb * TB, TB), TB)
            srcs = (k_hbm.at[b, rows, pl.ds(pl.multiple_of(hp * LANES, LANES), LANES)],
                    v_hbm.at[b, rows, pl.ds(pl.multiple_of((HP + hp) * LANES, LANES), LANES)],
                    ck_hbm.at[b * HP + hp, :, rows])
            return _fox_fetch(srcs, (kbuf, vbuf, cbuf), sems, slot)

        @pl.when(n_far > 0)
        def _():
            for cp in fetch(kb0, 0):
                cp.start()

        m_sc[...] = jnp.full_like(m_sc, NEG)
        l_sc[...] = jnp.zeros_like(l_sc)
        acc[...] = jnp.zeros_like(acc)

        def step(k, v, ck_rows, diagonal):
            q, v = q_ref[0], v.astype(BF16)
            cqv = cq_ref[0]
            a = acc[...]
            for h, hm in enumerate(_lane_masks()):
                _, t = _fox_scores(q, k, ck_rows[h], hm, diagonal)
                cqh = _head_col(cqv, h)
                m_prev = jnp.max(m_sc[h], axis=-1, keepdims=True)
                l_prev = jnp.max(l_sc[h], axis=-1, keepdims=True)
                m_new = jnp.maximum(m_prev, jnp.max(t, axis=-1, keepdims=True) + cqh)
                alpha = jnp.exp(m_prev - m_new)
                p = jnp.exp(t + (cqh - m_new))
                l_new = alpha * l_prev + jnp.sum(p, axis=-1, keepdims=True)
                a = jnp.where(hm, alpha * a + _dot(p.astype(BF16), v, NN), a)
                m_sc[h] = jnp.broadcast_to(m_new, (TB, LANES))
                l_sc[h] = jnp.broadcast_to(l_new, (TB, LANES))
            acc[...] = a

        def walk(i, carry):
            slot = lax.rem(i, 2)

            @pl.when(i + 1 < n_far)
            def _():
                for cp in fetch(kb0 + i + 1, 1 - slot):
                    cp.start()

            for cp in fetch(kb0 + i, slot):
                cp.wait()
            step(kbuf[slot], vbuf[slot], [cbuf[slot, pl.ds(h, 1), :] for h in range(2)], False)
            return carry

        lax.fori_loop(0, n_far, walk, 0)
        pl.when(kb0 < qb)(lambda: step(kp_ref[0], vp_ref[0], [ckp_ref[0, h:h + 1, :] for h in range(2)], False))
        step(k_ref[0], v_ref[0], [ck_ref[0, h:h + 1, :] for h in range(2)], True)
        a = acc[...]
        out = a
        lse = jnp.zeros_like(a)
        for h, hm in enumerate(_lane_masks()):
            out = jnp.where(hm, a / l_sc[h], out)
            lse = jnp.where(hm, m_sc[h] + jnp.log(l_sc[h]), lse)
        o_ref[0] = out
        lse_ref[0] = lse

    out = jax.ShapeDtypeStruct((B, S, D), F32)
    spec = pltpu.PrefetchScalarGridSpec(
        num_scalar_prefetch=1, grid=(B, nb, HP),
        in_specs=_fox_q_specs(2, TB) + _fox_k_specs(TB, HP, 0) + _fox_k_specs(TB, HP, -1) + [_ANY] * 3,
        out_specs=_fox_q_specs(2, TB), scratch_shapes=_fox_k_buffers(TB) + [
            pltpu.VMEM((2, TB, LANES), F32), pltpu.VMEM((2, TB, LANES), F32), pltpu.VMEM((TB, LANES), F32)])
    k3, v3 = kh.reshape(B, S, D), kv.reshape(B, S, 2 * D)
    o, lse = pl.pallas_call(
        body, name=name, grid_spec=spec, out_shape=[out, out], compiler_params=_params("parallel", "parallel", "parallel"),
    )(first, qh.reshape(B, S, D), cq.reshape(B, S, D), k3, v3, ck, k3, v3, ck, k3, v3, ck)
    return o.reshape(B * S, D), lse.reshape(B * S, D)


_ANY = pl.BlockSpec(memory_space=pl.ANY)


def _fox_q_specs(n, TB):
    return [pl.BlockSpec((1, TB, LANES), lambda b, i, h, table: (b, i, h))] * n


def _fox_k_specs(TB, HP, shift):
    blk = lambda i: jnp.maximum(i + shift, 0)
    return [pl.BlockSpec((1, TB, LANES), lambda b, i, h, table: (b, blk(i), h)),
            pl.BlockSpec((1, TB, LANES), lambda b, i, h, table: (b, blk(i), HP + h)),
            pl.BlockSpec((1, 8, TB), lambda b, i, h, table: (b * HP + h, 0, blk(i)))]


def _fox_k_buffers(TB):
    return [pltpu.VMEM((2, TB, LANES), BF16), pltpu.VMEM((2, TB, LANES), F32), pltpu.VMEM((2, 8, TB), F32),
            pltpu.SemaphoreType.DMA((2, 3))]


def _fox_bwd_dq(qh, kh, kv, cq, ck, first, do, lse, delta, B, S, D, name):
    HP = D // LANES
    TB = FOX_BLOCK
    nb = S // TB

    def body(first_ref, q_ref, cq_ref, do_ref, l_ref, d_ref, k_ref, v_ref, ck_ref, kp_ref, vp_ref, ckp_ref,
             k_hbm, v_hbm, ck_hbm, dq_ref, dc_ref, kbuf, vbuf, cbuf, sems, acc, acc_c):
        b, qb, hp = pl.program_id(0), pl.program_id(1), pl.program_id(2)
        kb0 = first_ref[(b * HP + hp) * nb + qb]
        n_far = jnp.maximum(qb - 1 - kb0, 0)

        def fetch(kb, slot):
            rows = pl.ds(pl.multiple_of(kb * TB, TB), TB)
            srcs = (k_hbm.at[b, rows, pl.ds(pl.multiple_of(hp * LANES, LANES), LANES)],
                    v_hbm.at[b, rows, pl.ds(pl.multiple_of((HP + hp) * LANES, LANES), LANES)],
                    ck_hbm.at[b * HP + hp, :, rows])
            return _fox_fetch(srcs, (kbuf, vbuf, cbuf), sems, slot)

        @pl.when(n_far > 0)
        def _():
            for cp in fetch(kb0, 0):
                cp.start()

        acc[...] = jnp.zeros_like(acc)
        acc_c[...] = jnp.zeros_like(acc_c)

        def step(k, v, ck_rows, diagonal):
            q, v = q_ref[0], v.astype(BF16)
            dout = do_ref[0].astype(BF16)
            a, c = acc[...], acc_c[...]
            for h, hm in enumerate(_lane_masks()):
                _, t = _fox_scores(q, k, ck_rows[h], hm, diagonal)
                p = jnp.exp(t + (_head_col(cq_ref[0], h) - _head_col(l_ref[0], h)))
                dom = jnp.where(hm, dout, jnp.zeros_like(dout))
                ds = p * (_dot(dom, v, NT) - _head_col(d_ref[0], h))
                a += _dot(ds.astype(BF16), jnp.where(hm, k, jnp.zeros_like(k)), NN)
                c = jnp.where(hm, c + jnp.sum(ds, axis=-1, keepdims=True), c)
            acc[...] = a
            acc_c[...] = c

        def walk(i, carry):
            slot = lax.rem(i, 2)

            @pl.when(i + 1 < n_far)
            def _():
                for cp in fetch(kb0 + i + 1, 1 - slot):
                    cp.start()

            for cp in fetch(kb0 + i, slot):
                cp.wait()
            step(kbuf[slot], vbuf[slot], [cbuf[slot, pl.ds(h, 1), :] for h in range(2)], False)
            return carry

        lax.fori_loop(0, n_far, walk, 0)
        pl.when(kb0 < qb)(lambda: step(kp_ref[0], vp_ref[0], [ckp_ref[0, h:h + 1, :] for h in range(2)], False))
        step(k_ref[0], v_ref[0], [ck_ref[0, h:h + 1, :] for h in range(2)], True)
        dq_ref[0] = acc[...]
        dc_ref[0] = acc_c[...]

    r3 = lambda t: t.reshape(B, S, D)
    out = jax.ShapeDtypeStruct((B, S, D), F32)
    spec = pltpu.PrefetchScalarGridSpec(
        num_scalar_prefetch=1, grid=(B, nb, HP),
        in_specs=_fox_q_specs(5, TB) + _fox_k_specs(TB, HP, 0) + _fox_k_specs(TB, HP, -1) + [_ANY] * 3,
        out_specs=_fox_q_specs(2, TB),
        scratch_shapes=_fox_k_buffers(TB) + [pltpu.VMEM((TB, LANES), F32), pltpu.VMEM((TB, LANES), F32)])
    k3, v3 = r3(kh), kv.reshape(B, S, 2 * D)
    dq, dc = pl.pallas_call(
        body, name=name, grid_spec=spec, out_shape=[out, out], compiler_params=_params("parallel", "parallel", "parallel"),
    )(first, r3(qh), r3(cq), r3(do), r3(lse), r3(delta), k3, v3, ck, k3, v3, ck, k3, v3, ck)
    return dq.reshape(B * S, D), dc.reshape(B * S, D)


def _fox_bwd_dkv(qh, kh, kv, cq, ck, last, do, lse, delta, B, S, D, name):
    HP = D // LANES
    TB = FOX_BLOCK
    nb = S // TB

    def body(last_ref, k_ref, v_ref, ck_ref, q_ref, cq_ref, do_ref, l_ref, d_ref, qn_ref, cqn_ref, don_ref, ln_ref,
             dn_ref, q_hbm, cq_hbm, do_hbm, l_hbm, d_hbm, dk_ref, dv_ref, dc_ref,
             qbuf, cqbuf, dobuf, lbuf, dbuf, sems, dk_acc, dv_acc, dc_acc):
        b, kb, hp = pl.program_id(0), pl.program_id(1), pl.program_id(2)
        qb_last = last_ref[(b * HP + hp) * nb + kb]
        n_far = jnp.maximum(qb_last - kb - 1, 0)

        def fetch(qb, slot):
            rows = pl.ds(pl.multiple_of(qb * TB, TB), TB)
            cols = pl.ds(pl.multiple_of(hp * LANES, LANES), LANES)
            srcs = [t.at[b, rows, cols] for t in (q_hbm, cq_hbm, do_hbm, l_hbm, d_hbm)]
            return _fox_fetch(srcs, (qbuf, cqbuf, dobuf, lbuf, dbuf), sems, slot)

        @pl.when(n_far > 0)
        def _():
            for cp in fetch(kb + 2, 0):
                cp.start()

        dk_acc[...] = jnp.zeros_like(dk_acc)
        dv_acc[...] = jnp.zeros_like(dv_acc)
        dc_acc[...] = jnp.zeros_like(dc_acc)

        def step(q, cqv, dout, lse_t, del_t, diagonal):
            k, v = k_ref[0], v_ref[0].astype(BF16)
            dout = dout.astype(BF16)
            dk, dv, dc = dk_acc[...], dv_acc[...], dc_acc[...]
            row = _iota2((8, TB), 0)
            for h, hm in enumerate(_lane_masks()):
                qm, t = _fox_scores(q, k, ck_ref[0, h:h + 1, :], hm, diagonal)
                p = jnp.exp(t + (_head_col(cqv, h) - _head_col(lse_t, h)))
                dom = jnp.where(hm, dout, jnp.zeros_like(dout))
                ds = p * (_dot(dom, v, NT) - _head_col(del_t, h))
                dv += _dot(p.astype(BF16), dom, TN)
                dk += _dot(ds.astype(BF16), qm, TN)
                dc = jnp.where(row == h, dc - jnp.sum(ds, axis=0, keepdims=True), dc)
            dk_acc[...] = dk
            dv_acc[...] = dv
            dc_acc[...] = dc

        step(q_ref[0], cq_ref[0], do_ref[0], l_ref[0], d_ref[0], True)
        pl.when(kb < qb_last)(lambda: step(qn_ref[0], cqn_ref[0], don_ref[0], ln_ref[0], dn_ref[0], False))

        def walk(i, carry):
            slot = lax.rem(i, 2)

            @pl.when(i + 1 < n_far)
            def _():
                for cp in fetch(kb + 3 + i, 1 - slot):
                    cp.start()

            for cp in fetch(kb + 2 + i, slot):
                cp.wait()
            step(qbuf[slot], cqbuf[slot], dobuf[slot], lbuf[slot], dbuf[slot], False)
            return carry

        lax.fori_loop(0, n_far, walk, 0)
        dk_ref[0] = dk_acc[...]
        dv_ref[0] = dv_acc[...].astype(BF16)
        dc_ref[0] = dc_acc[...]

    blk = (1, TB, LANES)
    kmap = lambda b, i, h, last: (b, i, h)
    vmap = lambda b, i, h, last: (b, i, HP + h)
    cmap = lambda b, i, h, last: (b * HP + h, 0, i)
    nmap = lambda b, i, h, last: (b, jnp.minimum(i + 1, nb - 1), h)
    r3 = lambda t: t.reshape(B, S, D)
    tile = lambda dt: pltpu.VMEM((2, TB, LANES), dt)
    qside = [r3(qh), r3(cq), r3(do), r3(lse), r3(delta)]
    spec = pltpu.PrefetchScalarGridSpec(
        num_scalar_prefetch=1, grid=(B, nb, HP),
        in_specs=[pl.BlockSpec(blk, kmap), pl.BlockSpec(blk, vmap), pl.BlockSpec((1, 8, TB), cmap)]
        + [pl.BlockSpec(blk, kmap)] * 5 + [pl.BlockSpec(blk, nmap)] * 5 + [_ANY] * 5,
        out_specs=[pl.BlockSpec(blk, kmap), pl.BlockSpec(blk, kmap), pl.BlockSpec((1, 8, TB), cmap)],
        scratch_shapes=[tile(BF16), tile(F32), tile(F32), tile(F32), tile(F32), pltpu.SemaphoreType.DMA((2, 5)),
                        pltpu.VMEM((TB, LANES), F32), pltpu.VMEM((TB, LANES), F32), pltpu.VMEM((8, TB), F32)])
    dk, dv, dc = pl.pallas_call(
        body, name=name, grid_spec=spec,
        out_shape=[jax.ShapeDtypeStruct((B, S, D), F32), jax.ShapeDtypeStruct((B, S, D), BF16),
                   jax.ShapeDtypeStruct((B * HP, 8, S), F32)],
        compiler_params=_params("parallel", "parallel", "parallel"),
    )(last, r3(kh), kv.reshape(B, S, 2 * D), ck, *qside, *qside, *qside)
    return dk.reshape(B * S, D), dv.reshape(B * S, D), dc


def _loss_and_grad(y, target, name):
    T, D = y.shape
    tm = _pick(T, 256, 8)

    def body(y_ref, t_ref, p_ref, dy_ref):
        e = y_ref[...] - t_ref[...]
        dy_ref[...] = e * (1.0 / D)
        p_ref[...] = jnp.sum(e * e, axis=0, keepdims=True)[None]

    blk = pl.BlockSpec((tm, D), lambda i: (i, 0))
    return pl.pallas_call(
        body, name=name, grid=(T // tm,), in_specs=[blk, blk],
        out_specs=[pl.BlockSpec((1, 1, D), lambda i: (i, 0, 0)), blk],
        out_shape=[jax.ShapeDtypeStruct((T // tm, 1, D), F32), jax.ShapeDtypeStruct((T, D), F32)],
        compiler_params=_params("parallel"),
    )(y, target)


def _pair_sum(x, got, name):
    _, R, C = x.shape
    tr = _pick(R, max(16, (PACK_ROWS * LANES) // C), 16)

    def body(c_ref, x_ref, g_ref, o_ref):
        o_ref[0] = (x_ref[0].astype(F32) + g_ref[0].astype(F32)).astype(o_ref.dtype)

    spec = pltpu.PrefetchScalarGridSpec(
        num_scalar_prefetch=1, grid=(4, R // tr),
        in_specs=[pl.BlockSpec((1, tr, C), lambda a, i, c: (2 * a + c[0], i, 0)),
                  pl.BlockSpec((1, tr, C), lambda a, i, c: (a, i, 0))],
        out_specs=pl.BlockSpec((1, tr, C), lambda a, i, c: (a, i, 0)))
    return pl.pallas_call(
        body, name=name, grid_spec=spec, out_shape=jax.ShapeDtypeStruct((4, R, C), x.dtype),
        compiler_params=_params("parallel", "parallel"),
    )(lax.axis_index("c").astype(jnp.int32).reshape(1), x, got)


def _adamw(parts, w, m, v, name):
    R, C = w.shape
    P = parts.shape[0]
    tr = _pick(R, max(16, (PACK_ROWS * LANES) // C), 16)

    def body(p_ref, w_ref, m_ref, v_ref, g_out, d_out, m_out, v_out):
        g = p_ref[0].astype(F32)
        for s in range(1, P):
            g = g + p_ref[s].astype(F32)
        mn = ADAM_B1 * m_ref[...] + (1.0 - ADAM_B1) * g
        vn = ADAM_B2 * v_ref[...] + (1.0 - ADAM_B2) * (g * g)
        m_hat = mn / (1.0 - ADAM_B1 ** ADAM_STEP)
        v_hat = vn / (1.0 - ADAM_B2 ** ADAM_STEP)
        g_out[...] = g
        d_out[...] = -ADAM_LR * (m_hat / (jnp.sqrt(v_hat) + ADAM_EPS) + ADAM_WD * w_ref[...])
        m_out[...] = mn
        v_out[...] = vn

    blk = pl.BlockSpec((tr, C), lambda i: (i, 0))
    out = jax.ShapeDtypeStruct((R, C), F32)
    return pl.pallas_call(
        body, name=name, grid=(R // tr,),
        in_specs=[pl.BlockSpec((P, tr, C), lambda i: (0, i, 0)), blk, blk, blk],
        out_specs=[blk] * 4, out_shape=[out] * 4, compiler_params=_params("parallel"),
    )(parts, w, m, v)


def _mesh_pos():
    return lax.axis_index("x"), lax.axis_index("y"), lax.axis_index("c")


def _comm_call(body, xs, out_shapes, name):
    n = len(xs)
    any_spec = pl.BlockSpec(memory_space=pl.ANY)
    return pl.pallas_call(
        body, name=name, out_shape=out_shapes, in_specs=[any_spec] * n, out_specs=[any_spec] * n,
        scratch_shapes=[pltpu.SemaphoreType.DMA((7, n)), pltpu.SemaphoreType.DMA((7, n)), pltpu.SemaphoreType.DMA((n,))],
    )(*xs)


def _all_gather(xs, name):
    n = len(xs)

    def body(*refs):
        x_refs, out_refs = refs[:n], refs[n:2 * n]
        send_sems, recv_sems, local_sems = refs[2 * n:]
        x_, y_, c_ = _mesh_pos()
        me, sibling = (x_, y_, c_), (x_, y_, 1 - c_)
        chips = [(1 - x_, y_), (x_, 1 - y_), (1 - x_, 1 - y_)]

        def slot(t, px, py, pc):
            return out_refs[t].at[4 * px + 2 * py + pc]

        def copy(t, k, block, to, src=None):
            return pltpu.make_async_remote_copy(
                src_ref=slot(t, *block) if src is None else src, dst_ref=slot(t, *block),
                send_sem=send_sems.at[k, t], recv_sem=recv_sems.at[k, t], device_id=to, device_id_type=MESH_ID)

        mine = [pltpu.make_async_copy(x_refs[t], slot(t, *me), local_sems.at[t]) for t in range(n)]
        first = [copy(t, 0, me, sibling, src=x_refs[t]) for t in range(n)]
        first += [copy(t, 1 + j, me, (*chip, c_), src=x_refs[t]) for j, chip in enumerate(chips) for t in range(n)]
        for cp in mine + first:
            cp.start()
        passed = []
        for j, chip in enumerate(chips):
            for t in range(n):
                copy(t, 1 + j, (*chip, c_), me).wait_recv()
                passed.append(copy(t, 4 + j, (*chip, c_), sibling))
                passed[-1].start()
        for t in range(n):
            copy(t, 0, sibling, me).wait_recv()
        for j, chip in enumerate(chips):
            for t in range(n):
                copy(t, 4 + j, (*chip, 1 - c_), me).wait_recv()
        for cp in first + passed:
            cp.wait_send()
        for cp in mine:
            cp.wait()

    return _comm_call(body, xs, [jax.ShapeDtypeStruct((N_DEV,) + x.shape, x.dtype) for x in xs], name)


def _pair_exchange(xs, name):
    n = len(xs)

    def body(*refs):
        x_refs, out_refs = refs[:n], refs[n:2 * n]
        send_sems, recv_sems, _ = refs[2 * n:]
        x_, y_, c_ = _mesh_pos()
        copies = [pltpu.make_async_remote_copy(
            src_ref=x_refs[t].at[2 * a + 1 - c_], dst_ref=out_refs[t].at[a],
            send_sem=send_sems.at[a, t], recv_sem=recv_sems.at[a, t],
            device_id=(x_, y_, 1 - c_), device_id_type=MESH_ID) for a in range(4) for t in range(n)]
        for cp in copies:
            cp.start()
        for cp in copies:
            cp.wait_recv()
        for cp in copies:
            cp.wait_send()

    return _comm_call(body, xs, [jax.ShapeDtypeStruct((4,) + x.shape[1:], x.dtype) for x in xs], name)


def _chip_exchange(xs, name):
    n = len(xs)

    def body(*refs):
        x_refs, out_refs = refs[:n], refs[n:2 * n]
        send_sems, recv_sems, local_sems = refs[2 * n:]
        x_, y_, c_ = _mesh_pos()
        chip = 2 * x_ + y_
        mine = [pltpu.make_async_copy(x_refs[t].at[chip], out_refs[t].at[chip], local_sems.at[t]) for t in range(n)]
        copies = []
        for k in range(1, 4):
            px, py = x_ ^ (k >> 1), y_ ^ (k & 1)
            for t in range(n):
                copies.append(pltpu.make_async_remote_copy(
                    src_ref=x_refs[t].at[2 * px + py], dst_ref=out_refs[t].at[chip],
                    send_sem=send_sems.at[k - 1, t], recv_sem=recv_sems.at[k - 1, t],
                    device_id=(px, py, c_), device_id_type=MESH_ID))
        for cp in mine + copies:
            cp.start()
        for cp in copies:
            cp.wait_recv()
        for cp in copies:
            cp.wait_send()
        for cp in mine:
            cp.wait()

    return _comm_call(body, xs, [jax.ShapeDtypeStruct(x.shape, x.dtype) for x in xs], name)


def _ffn_fwd(h, norm, w_in, w_out, tag):
    T, D = h.shape
    F = w_out.shape[0]
    hn = _rmsnorm_fwd(h, norm, f"{tag}_norm")
    gate, up, act = _mm_swiglu_fwd(hn, w_in, f"{tag}_in")
    out = _mm(act, w_out, "nn", M=T, N=D, K=F, res=h, scale=0.5, name=f"{tag}_out")
    return out, (h, hn, gate, up, act)


def _ffn_bwd(dout, saved, norm, w_in, w_out, tag):
    h, hn, gate, up, act = saved
    T, D = h.shape
    F = w_out.shape[0]
    dg, du = _mm_swiglu_bwd(dout, w_out, gate, up, f"{tag}_dact")
    dw_out = _mm(act, dout, "tn", M=F, N=D, K=T, scale=0.5, out_dtype=BF16, name=f"{tag}_dwout")
    dhn = _mm(dg, w_in, "nt", M=T, N=D, K=F, name=f"{tag}_dhn_g")
    dhn = _mm(du, w_in, "nt", M=T, N=D, K=F, b_off=(0, F), res=dhn, name=f"{tag}_dhn_u")
    dw_g = _mm(hn, dg, "tn", M=D, N=F, K=T, out_dtype=BF16, name=f"{tag}_dwin_g")
    dw_u = _mm(hn, du, "tn", M=D, N=F, K=T, out_dtype=BF16, name=f"{tag}_dwin_u")
    dh, dnorm = _rmsnorm_bwd(h, norm, dhn, dout, f"{tag}_dnorm")
    return dh, dnorm, jnp.concatenate([dw_g, dw_u], axis=1), dw_out


def _local_step(x, positions, target, W):
    B, S, D = x.shape
    T = B * S
    H, HP = D // HEAD_DIM, D // LANES
    G = {}
    h0 = x.reshape(T, D)
    cs = _rope_tables(positions, "rope_tables")

    h1, ffn00 = _ffn_fwd(h0, W["ffn_norm"][0, 0], W["ffn_w_in"][0, 0], W["ffn_w_out"][0, 0], "ffn00")
    hna = _rmsnorm_fwd(h1, W["mix_norm"][0], "mixa_norm")
    qkv = _mm(hna, W["a_w_qkv"], "nn", M=T, N=9 * D, K=D, name="a_qkv")
    qs, ks, os_, lses = [], [], [], []
    for g in range(3):
        qs.append(_qknorm_fwd(qkv, 3 * g * D, W["a_q_norm"][0, g], cs, D, F32, f"a_qnorm{g}", out_scale=SCALE))
        ks.append(_qknorm_fwd(qkv, (3 * g + 1) * D, W["a_k_norm"][0, g], cs, D, F32, f"a_knorm{g}"))
        o, l = _band_fwd(qs[g], ks[g], qkv, g, B, S, D, f"a_band{g}")
        os_.append(o)
        lses.append(l)
    mixed, lse_a = _mix(os_, lses, "a_mix")
    h2 = _mm(mixed, W["a_w_o"], "nn", M=T, N=D, K=D, res=h1, name="a_out")
    h3, ffn01 = _ffn_fwd(h2, W["ffn_norm"][0, 1], W["ffn_w_in"][0, 1], W["ffn_w_out"][0, 1], "ffn01")

    hnkv = _rmsnorm_fwd(h3, W["kv_norm"], "kv_norm")
    kv = _mm(hnkv, W["kv_w_kv"], "nn", M=T, N=2 * D, K=D, name="kv_proj")
    fpre = _mm(hnkv, W["kv_w_f"], "nn", M=T, N=LANES, K=D, name="kv_gate_proj")
    b_f = jnp.pad(W["kv_b_f"], (0, LANES - H)).reshape(1, LANES)
    kh = _qknorm_fwd(kv, 0, W["kv_k_norm"], None, D, BF16, "kv_knorm")
    cum, cq = _gate_fwd(fpre, b_f, B, S, D, "kv_gate")
    ck = cum.reshape(B, S, LANES)[:, :, :H].reshape(B, S, HP, 2).transpose(0, 2, 3, 1)
    ck = jnp.pad(ck, ((0, 0), (0, 0), (0, 6), (0, 0))).reshape(B * HP, 8, S)
    h4, ffn10 = _ffn_fwd(h3, W["ffn_norm"][1, 0], W["ffn_w_in"][1, 0], W["ffn_w_out"][1, 0], "ffn10")
    hnb = _rmsnorm_fwd(h4, W["mix_norm"][1], "mixb_norm")
    qraw = _mm(hnb, W["b_w_q"], "nn", M=T, N=D, K=D, name="b_q")
    qh = _qknorm_fwd(qraw, 0, W["b_q_norm"][0], None, D, BF16, "b_qnorm", out_scale=SCALE)
    first, last = _fox_live_blocks(cum, W["b_q_norm"][0], W["kv_k_norm"], B, S, D)
    ob, lse_b = _fox_fwd(qh, kh, kv, cq, ck, first, B, S, D, "b_fox")
    h5 = _mm(ob, W["b_w_o"], "nn", M=T, N=D, K=D, res=h4, name="b_out")
    h6, ffn11 = _ffn_fwd(h5, W["ffn_norm"][1, 1], W["ffn_w_in"][1, 1], W["ffn_w_out"][1, 1], "ffn11")

    sq, dy = _loss_and_grad(h6, target.reshape(T, D), "loss")

    dn = [[None, None], [None, None]]
    dwi = [[None, None], [None, None]]
    dwo = [[None, None], [None, None]]
    dh5, dn[1][1], dwi[1][1], dwo[1][1] = _ffn_bwd(dy, ffn11, W["ffn_norm"][1, 1], W["ffn_w_in"][1, 1], W["ffn_w_out"][1, 1], "ffn11")
    dob = _mm(dh5, W["b_w_o"], "nt", M=T, N=D, K=D, name="b_dout")
    G["b_w_o"] = _mm(ob, dh5, "tn", M=D, N=D, K=T, out_dtype=BF16, name="b_dwo")[None]
    delta_b = _head_rowsum(dob, ob, "b_delta")
    dqh, dcq = _fox_bwd_dq(qh, kh, kv, cq, ck, first, dob, lse_b, delta_b, B, S, D, "b_fox_dq")
    dkh, dv_b, dck = _fox_bwd_dkv(qh, kh, kv, cq, ck, last, dob, lse_b, delta_b, B, S, D, "b_fox_dkv")
    dqraw, dg = _qknorm_bwd(qraw, 0, W["b_q_norm"][0], None, dqh, D, "b_dqnorm", out_scale=SCALE)
    G["b_q_norm"] = dg[None]
    dhnb = _mm(dqraw, W["b_w_q"], "nt", M=T, N=D, K=D, name="b_dhn")
    G["b_w_q"] = _mm(hnb, dqraw, "tn", M=D, N=D, K=T, out_dtype=BF16, name="b_dwq")[None]
    dh4, dmix_b = _rmsnorm_bwd(h4, W["mix_norm"][1], dhnb, dh5, "mixb_dnorm")
    dh3, dn[1][0], dwi[1][0], dwo[1][0] = _ffn_bwd(dh4, ffn10, W["ffn_norm"][1, 0], W["ffn_w_in"][1, 0], W["ffn_w_out"][1, 0], "ffn10")

    dkraw, G["kv_k_norm"] = _qknorm_bwd(kv, 0, W["kv_k_norm"], None, dkh, D, "kv_dknorm")
    dcum = dck.reshape(B, HP, 8, S)[:, :, :2].transpose(0, 3, 1, 2).reshape(T, H)
    dcum = jnp.pad(dcum, ((0, 0), (0, LANES - H)))
    dcum_q = jnp.pad(dcq.reshape(T, H, HEAD_DIM)[:, :, 0], ((0, 0), (0, LANES - H)))
    dfpre, db_f = _gate_bwd(dcum, dcum_q, fpre, b_f, B, S, "kv_dgate")
    G["kv_b_f"] = db_f[0, :H]
    dhnkv = _mm(dkraw, W["kv_w_kv"], "nt", M=T, N=D, K=D, name="kv_dhn_k")
    dhnkv = _mm(dv_b, W["kv_w_kv"], "nt", M=T, N=D, K=D, b_off=(0, D), res=dhnkv, name="kv_dhn_v")
    dhnkv = _mm(dfpre, W["kv_w_f"], "nt", M=T, N=D, K=LANES, res=dhnkv, name="kv_dhn_f")
    dw_k = _mm(hnkv, dkraw, "tn", M=D, N=D, K=T, out_dtype=BF16, name="kv_dwk")
    dw_v = _mm(hnkv, dv_b, "tn", M=D, N=D, K=T, out_dtype=BF16, name="kv_dwv")
    dw_f = _mm(hnkv, dfpre, "tn", M=D, N=LANES, K=T, out_dtype=BF16, name="kv_dwf")
    G["kv_w"] = jnp.concatenate([dw_k, dw_v, dw_f[:, :H]], axis=1)
    dh3, G["kv_norm"] = _rmsnorm_bwd(h3, W["kv_norm"], dhnkv, dh3, "kv_dnorm")

    dh2, dn[0][1], dwi[0][1], dwo[0][1] = _ffn_bwd(dh3, ffn01, W["ffn_norm"][0, 1], W["ffn_w_in"][0, 1], W["ffn_w_out"][0, 1], "ffn01")
    dmixed = _mm(dh2, W["a_w_o"], "nt", M=T, N=D, K=D, name="a_dmixed")
    G["a_w_o"] = _mm(mixed, dh2, "tn", M=D, N=D, K=T, out_dtype=BF16, name="a_dwo")[None]
    delta_a = _head_rowsum(dmixed, mixed, "a_delta")
    dhna = None
    dw_qkv, dqn, dkn = [], [], []
    for g in range(3):
        dq, dk, dv = _band_bwd(qs[g], ks[g], qkv, dmixed, lse_a, delta_a, g, B, S, D, f"a_dband{g}")
        dqr, dgq = _qknorm_bwd(qkv, 3 * g * D, W["a_q_norm"][0, g], cs, dq, D, f"a_dqnorm{g}", out_scale=SCALE)
        dkr, dgk = _qknorm_bwd(qkv, (3 * g + 1) * D, W["a_k_norm"][0, g], cs, dk, D, f"a_dknorm{g}")
        dqn.append(dgq)
        dkn.append(dgk)
        for j, d in enumerate((dqr, dkr, dv)):
            c = 3 * g + j
            dhna = _mm(d, W["a_w_qkv"], "nt", M=T, N=D, K=D, b_off=(0, c * D), res=dhna, name=f"a_dhn{c}")
            dw_qkv.append(_mm(hna, d, "tn", M=D, N=D, K=T, out_dtype=BF16, name=f"a_dwqkv{c}"))
    G["a_w_qkv"] = jnp.concatenate(dw_qkv, axis=1)[None]
    G["a_q_norm"] = jnp.stack(dqn)[None]
    G["a_k_norm"] = jnp.stack(dkn)[None]
    dh1, dmix_a = _rmsnorm_bwd(h1, W["mix_norm"][0], dhna, dh2, "mixa_dnorm")
    dh0, dn[0][0], dwi[0][0], dwo[0][0] = _ffn_bwd(dh1, ffn00, W["ffn_norm"][0, 0], W["ffn_w_in"][0, 0], W["ffn_w_out"][0, 0], "ffn00")

    G["mix_norm"] = jnp.stack([dmix_a, dmix_b])
    G["ffn_norm"] = jnp.stack([jnp.stack(r) for r in dn])
    G["ffn_w_in"] = jnp.stack([jnp.stack(r) for r in dwi])
    G["ffn_w_out"] = jnp.stack([jnp.stack(r) for r in dwo])
    return jnp.sum(sq), dh0.reshape(B, S, D), G


SMALL =["mix_norm", "kv_norm", "a_q_norm", "a_k_norm", "kv_b_f", "kv_k_norm", "b_q_norm"]


def _pack(blocks, lead):
    flat = [b.reshape(b.shape[:lead] + (-1,)) for b in blocks]
    flat = jnp.concatenate(flat, axis=lead)
    n = flat.shape[-1]
    unit = PACK_ROWS * LANES
    padded = -(-n // unit) * unit if n > unit else -(-n // (16 * LANES)) * 16 * LANES
    flat = jnp.pad(flat, [(0, 0)] * lead + [(0, padded - n)])
    return flat.reshape(flat.shape[:lead] + (padded // LANES, LANES))


def _unpack(packed, shapes, lead):
    flat = packed.reshape(packed.shape[:lead] + (-1,))
    out, off = [], 0
    for shp in shapes:
        n = int(np.prod(shp))
        out.append(flat[..., off:off + n].reshape(packed.shape[:lead] + tuple(shp)))
        off += n
    return out


def kernel(x, positions, ffn_norm, ffn_w_in, ffn_w_out, mix_norm, a_w_qkv, a_q_norm, a_k_norm, a_w_o, kv_norm, kv_w, kv_b_f, kv_k_norm, b_w_q, b_q_norm, b_w_o, loss_target, m_ffn_norm, m_ffn_w_in, m_ffn_w_out, m_mix_norm, m_a_w_qkv, m_a_q_norm, m_a_k_norm, m_a_w_o, m_kv_norm, m_kv_w, m_kv_b_f, m_kv_k_norm, m_b_w_q, m_b_q_norm, m_b_w_o, v_ffn_norm, v_ffn_w_in, v_ffn_w_out, v_mix_norm, v_a_w_qkv, v_a_q_norm, v_a_k_norm, v_a_w_o, v_kv_norm, v_kv_w, v_kv_b_f, v_kv_k_norm, v_b_w_q, v_b_q_norm, v_b_w_o):
    names = ["ffn_norm", "ffn_w_in", "ffn_w_out", "mix_norm", "a_w_qkv", "a_q_norm", "a_k_norm", "a_w_o", "kv_norm",
             "kv_w", "kv_b_f", "kv_k_norm", "b_w_q", "b_q_norm", "b_w_o"]
    w = dict(zip(names, (ffn_norm, ffn_w_in, ffn_w_out, mix_norm, a_w_qkv, a_q_norm, a_k_norm, a_w_o, kv_norm, kv_w,
                         kv_b_f, kv_k_norm, b_w_q, b_q_norm, b_w_o)))
    m = dict(zip(names, (m_ffn_norm, m_ffn_w_in, m_ffn_w_out, m_mix_norm, m_a_w_qkv, m_a_q_norm, m_a_k_norm, m_a_w_o,
                         m_kv_norm, m_kv_w, m_kv_b_f, m_kv_k_norm, m_b_w_q, m_b_q_norm, m_b_w_o)))
    v = dict(zip(names, (v_ffn_norm, v_ffn_w_in, v_ffn_w_out, v_mix_norm, v_a_w_qkv, v_a_q_norm, v_a_k_norm, v_a_w_o,
                         v_kv_norm, v_kv_w, v_kv_b_f, v_kv_k_norm, v_b_w_q, v_b_q_norm, v_b_w_o)))
    B, S, D = x.shape
    H = D // HEAD_DIM
    F = ffn_w_out.shape[2] * N_DEV
    Ds, Fs = D // N_DEV, F // N_DEV
    me = 4 * lax.axis_index("x") + 2 * lax.axis_index("y") + lax.axis_index("c")
    squares = ("a_w_o", "b_w_q", "b_w_o")

    def shard_views(t):
        return [t["ffn_w_in"].reshape(4 * D, -1), t["ffn_w_out"].reshape(-1, D), t["a_w_qkv"][0], t["kv_w"],
                jnp.concatenate([t[n][0] for n in squares], axis=0)]

    def to_shards(views):
        sq3 = views[4].reshape(3, 1, Ds, D)
        out = {"ffn_w_in": views[0].reshape(ffn_w_in.shape), "ffn_w_out": views[1].reshape(ffn_w_out.shape),
               "a_w_qkv": views[2][None], "kv_w": views[3]}
        out.update({n: sq3[i] for i, n in enumerate(squares)})
        return out

    norm_rows = jnp.pad(ffn_norm.reshape(4, Ds), ((0, 4), (0, 0)))
    g_in, g_out, g_qkv, g_kv, g_sq, g_norm = _all_gather(
        [s.astype(BF16) for s in shard_views(w)] + [norm_rows], "gather_weights")
    W = {n: w[n] for n in SMALL}
    W["ffn_norm"] = g_norm[:, :4].transpose(1, 0, 2).reshape(2, 2, D)
    W["ffn_w_in"] = g_in.transpose(1, 0, 2).reshape(2, 2, D, 2 * F)
    W["ffn_w_out"] = g_out.reshape(N_DEV, 4, Fs, D).transpose(1, 0, 2, 3).reshape(2, 2, F, D)
    W["a_w_qkv"] = g_qkv.transpose(1, 0, 2).reshape(D, 9 * D)
    kv_full = g_kv.transpose(1, 0, 2).reshape(D, 2 * D + H)
    W["kv_w_kv"] = kv_full[:, :2 * D]
    W["kv_w_f"] = jnp.pad(kv_full[:, 2 * D:], ((0, 0), (0, LANES - H)))
    sq_full = g_sq.reshape(N_DEV, 3, Ds, D).transpose(1, 0, 2, 3).reshape(3, D, D)
    W["a_w_o"], W["b_w_q"], W["b_w_o"] = sq_full[0], sq_full[1], sq_full[2]

    sq, grad_x, G = _local_step(x, positions, loss_target, W)
    loss = lax.psum(sq * (0.5 / D), ("x", "y", "c"))

    blocks = [G["ffn_w_in"].reshape(4 * D, N_DEV, -1).transpose(1, 0, 2),
              G["ffn_w_out"].reshape(4, N_DEV, Fs, D).transpose(1, 0, 2, 3).reshape(N_DEV, 4 * Fs, D),
              G["a_w_qkv"][0].reshape(D, N_DEV, -1).transpose(1, 0, 2),
              G["kv_w"].reshape(D, N_DEV, -1).transpose(1, 0, 2),
              jnp.concatenate([G[n][0].reshape(N_DEV, Ds, D) for n in squares], axis=1)]
    blocks = [b.astype(BF16) for b in blocks]
    got = _pair_exchange(blocks, "exchange_pair")
    parts = _chip_exchange([_pair_sum(b, s, f"pair_sum{i}") for i, (b, s) in enumerate(zip(blocks, got))], "exchange_chips")
    views = [shard_views(t) for t in (w, m, v)]
    outs = [_adamw(parts[i], views[0][i], views[1][i], views[2][i], f"adamw_matrix{i}") for i in range(len(parts))]
    new = {}
    for k, kind in enumerate(("grad", "delta", "new_m", "new_v")):
        for n, arr in to_shards([o[k] for o in outs]).items():
            new[kind, n] = arr

    small_all = ["ffn_norm"] + SMALL
    vec_parts = _all_gather([_pack([G[n] for n in small_all], 0)], "gather_vector_grads")[0]
    vec_parts = _unpack(vec_parts, [G[n].shape for n in small_all], 1)
    norm_g = lax.dynamic_slice_in_dim(vec_parts[0], me * ffn_norm.shape[2], ffn_norm.shape[2], axis=3)
    rep_g = _pack(vec_parts[1:], 1)
    for ns, parts_s in ((["ffn_norm"], _pack([norm_g], 1)), (SMALL, rep_g)):
        outs = _adamw(parts_s, *(_pack([t[n] for n in ns], 0) for t in (w, m, v)), "adamw_" + ns[0])
        for kind, packed in zip(("grad", "delta", "new_m", "new_v"), outs):
            for n, arr in zip(ns, _unpack(packed, [w[n].shape for n in ns], 0)):
                new[kind, n] = arr

    result = [loss, grad_x]
    for kind in ("grad", "delta", "new_m", "new_v"):
        result += [new[kind, n] for n in names]
    return tuple(result)
```

```python
import jax
import jax.numpy as jnp
import numpy as np
from jax import lax
from jax.experimental import pallas as pl
from jax.experimental.pallas import tpu as pltpu

F32 = jnp.float32
BF16 = jnp.bfloat16
HIGHEST = lax.Precision.HIGHEST

HEAD_DIM = 64
ROT_DIM = HEAD_DIM // 4
ROPE_THETA = 500000.0
DILATED_GROUPS = ((128, 1), (512, 4), (2048, 16))
BAND_BLOCK = 128
EPS = 1e-6
ADAM_LR, ADAM_B1, ADAM_B2, ADAM_EPS, ADAM_WD, ADAM_STEP = 0.001, 0.9, 0.999, 1e-08, 0.01, 10

LANES = 128
N_DEV = 8
VMEM_LIMIT = 56 * 1024 * 1024
NEG = -1e30
SCALE = HEAD_DIM ** -0.5
PACK_ROWS = 1024
MESH_ID = pl.DeviceIdType.MESH


def _params(*sem):
    return pltpu.CompilerParams(dimension_semantics=sem, vmem_limit_bytes=VMEM_LIMIT)


def _divs(n, cap, mult=LANES):
    d = [t for t in range(mult, min(n, cap) + 1, mult) if n % t == 0]
    return d or [n]


def _pick(n, cap, mult=LANES):
    return _divs(n, cap, mult)[-1]


def _mm_tiles(M, N, K, budget=1408 * 1024):
    best = None
    for tm in _divs(M, 1408):
        for tn in _divs(N, 1408):
            if tm * tn <= budget and (best is None or tm * tn > best[0] * best[1]):
                best = (tm, tn)
    if best is None:
        best = (_divs(M, 1408)[0], _divs(N, 1408)[0])
    tm, tn = best
    tks = [t for t in _divs(K, 2048) if (tm + tn) * t <= 3 * 1024 * 1024]
    tk = tks[-1] if tks else _divs(K, 2048)[0]
    return tm, tn, tk


def _dot(a, b, dims, precision=None):
    return lax.dot_general(a, b, (dims, ((), ())), precision=precision, preferred_element_type=F32)


NN = ((1,), (0,))
NT = ((1,), (1,))
TN = ((0,), (0,))


def _dot_split(x, mat):
    hi = x.astype(BF16)
    lo = (x - hi.astype(F32)).astype(BF16)
    m = mat.astype(BF16)
    return _dot(hi, m, NN) + _dot(lo, m, NN)


def _dot_once(x, mat):
    return _dot(x.astype(BF16), mat.astype(BF16), NN)


def _mm(a, b, mode, *, M, N, K, a_off=(0, 0), b_off=(0, 0), a2=None, b2=None, res=None, scale=1.0, out_dtype=F32, name):
    tm, tn, tk = _mm_tiles(M, N // 2 if b2 is not None else N, K // 2 if a2 is not None else K)
    nk = K // tk
    k_half, n_half = nk // 2, (N // tn) // 2
    if mode == "tn":
        a_blk, b_blk, dims = (tk, tm), (tk, tn), TN
    elif mode == "nt":
        a_blk, b_blk, dims = (tm, tk), (tn, tk), NT
    else:
        a_blk, b_blk, dims = (tm, tk), (tk, tn), NN
    for off, blk in ((a_off, a_blk), (b_off, b_blk)):
        assert off[0] % blk[0] == 0 and off[1] % blk[1] == 0, (name, off, blk)
    ao = (a_off[0] // a_blk[0], a_off[1] // a_blk[1])
    bo = (b_off[0] // b_blk[0], b_off[1] // b_blk[1])
    if mode == "tn":
        a_map = lambda i, j, k: (k + ao[0], i + ao[1])
        b_map = lambda i, j, k: (k + bo[0], j + bo[1])
    elif mode == "nt":
        a_map = lambda i, j, k: (i + ao[0], k + ao[1])
        b_map = lambda i, j, k: (j + bo[0], k + bo[1])
    else:
        a_map = lambda i, j, k: (i + ao[0], k + ao[1])
        b_map = lambda i, j, k: (k + bo[0], j + bo[1])
    has_res = res is not None
    if a2 is not None:
        assert mode != "tn" and a_off == (0, 0)
        a_map = lambda i, j, k: (i, jnp.minimum(k, k_half - 1))
        a2_map = lambda i, j, k: (i, jnp.maximum(k - k_half, 0))
    if b2 is not None:
        assert mode != "nt" and b_off == (0, 0)
        b_map = lambda i, j, k: (k, jnp.minimum(j, n_half - 1))
        b2_map = lambda i, j, k: (k, jnp.maximum(j - n_half, 0))

    def body(*refs):
        refs = list(refs)
        a_ref, b_ref = refs.pop(0), refs.pop(0)
        a_val, b_val = a_ref[...].astype(BF16), b_ref[...].astype(BF16)
        if a2 is not None:
            a_val = jnp.where(pl.program_id(2) < k_half, a_val, refs.pop(0)[...].astype(BF16))
        if b2 is not None:
            b_val = jnp.where(pl.program_id(1) < n_half, b_val, refs.pop(0)[...].astype(BF16))
        r_ref = refs.pop(0) if has_res else None
        o_ref = refs.pop(0)

        def finish(total):
            out = total * scale
            if has_res:
                out = r_ref[...] + out
            o_ref[...] = out.astype(out_dtype)

        part = _dot(a_val, b_val, dims)
        if nk == 1:
            finish(part)
            return
        acc = refs[-1]
        k = pl.program_id(2)

        @pl.when(k == 0)
        def _():
            acc[...] = part

        @pl.when(k > 0)
        def _():
            acc[...] += part

        pl.when(k == nk - 1)(lambda: finish(acc[...]))

    in_specs = [pl.BlockSpec(a_blk, a_map), pl.BlockSpec(b_blk, b_map)]
    args = [a, b]
    if a2 is not None:
        in_specs.append(pl.BlockSpec(a_blk, a2_map))
        args.append(a2)
    if b2 is not None:
        in_specs.append(pl.BlockSpec(b_blk, b2_map))
        args.append(b2)
    if has_res:
        in_specs.append(pl.BlockSpec((tm, tn), lambda i, j, k: (i, j)))
        args.append(res)
    return pl.pallas_call(
        body, name=name, grid=(M // tm, N // tn, nk), in_specs=in_specs,
        out_specs=pl.BlockSpec((tm, tn), lambda i, j, k: (i, j)),
        out_shape=jax.ShapeDtypeStruct((M, N), out_dtype),
        scratch_shapes=[pltpu.VMEM((tm, tn), F32)] if nk > 1 else [],
        compiler_params=_params("parallel", "parallel", "arbitrary"),
    )(*args)


def _rmsnorm_fwd(x, g, name):
    T, D = x.shape
    tm = _pick(T, 256, 8)

    def body(x_ref, g_ref, y_ref):
        xv = x_ref[...]
        r = lax.rsqrt(jnp.mean(xv * xv, axis=-1, keepdims=True) + EPS)
        y_ref[...] = (xv * r * g_ref[...]).astype(BF16)

    return pl.pallas_call(
        body, name=name, grid=(T // tm,),
        in_specs=[pl.BlockSpec((tm, D), lambda i: (i, 0)), pl.BlockSpec((1, D), lambda i: (0, 0))],
        out_specs=pl.BlockSpec((tm, D), lambda i: (i, 0)),
        out_shape=jax.ShapeDtypeStruct((T, D), BF16), compiler_params=_params("parallel"),
    )(x, g.reshape(1, D))


def _rmsnorm_bwd(x, g, dy, dres, name):
    T, D = x.shape
    tm = _pick(T, 256, 8)

    def body(x_ref, g_ref, dy_ref, dres_ref, dx_ref, dg_ref):
        xv = x_ref[...]
        r = lax.rsqrt(jnp.mean(xv * xv, axis=-1, keepdims=True) + EPS)
        xn = xv * r
        d = dy_ref[...]
        dgy = d * g_ref[...]
        dx_ref[...] = dres_ref[...] + r * (dgy - xn * jnp.mean(dgy * xn, axis=-1, keepdims=True))

        @pl.when(pl.program_id(0) == 0)
        def _():
            dg_ref[...] = jnp.zeros_like(dg_ref)

        dg_ref[...] += jnp.sum(d * xn, axis=0, keepdims=True)

    row = pl.BlockSpec((tm, D), lambda i: (i, 0))
    vec = pl.BlockSpec((1, D), lambda i: (0, 0))
    dx, dg = pl.pallas_call(
        body, name=name, grid=(T // tm,), in_specs=[row, vec, row, row], out_specs=[row, vec],
        out_shape=[jax.ShapeDtypeStruct((T, D), F32), jax.ShapeDtypeStruct((1, D), F32)],
        compiler_params=_params("arbitrary"),
    )(x, g.reshape(1, D), dy, dres)
    return dx, dg.reshape(D)


def _sigmoid(x):
    return 1.0 / (1.0 + jnp.exp(-x))


MXU_WIDTH = 256


def _col_chunks(n):
    return [(lo, min(lo + MXU_WIDTH, n)) for lo in range(0, n, MXU_WIDTH)]


def _mm_swiglu_fwd(hn, w_in, name):
    T, D = hn.shape
    F = w_in.shape[1] // 2
    tm, tn, _ = _mm_tiles(T, F, D, budget=768 * 1024)
    assert (tm + 2 * tn) * D <= 4 * 1024 * 1024, "the whole contraction is held in one step"
    nf = F // tn

    def body(a_ref, bg_ref, bu_ref, g_ref, u_ref, act_ref):
        a = a_ref[...].astype(BF16)
        for lo, hi in _col_chunks(tn):
            gv = _dot(a, bg_ref[:, lo:hi].astype(BF16), NN)
            uv = _dot(a, bu_ref[:, lo:hi].astype(BF16), NN)
            g_ref[:, lo:hi] = gv
            u_ref[:, lo:hi] = uv
            act_ref[:, lo:hi] = (gv * _sigmoid(gv) * uv).astype(BF16)

    tile = pl.BlockSpec((tm, tn), lambda j, i: (i, j))
    f32 = jax.ShapeDtypeStruct((T, F), F32)
    return pl.pallas_call(
        body, name=name, grid=(nf, T // tm),
        in_specs=[pl.BlockSpec((tm, D), lambda j, i: (i, 0)), pl.BlockSpec((D, tn), lambda j, i: (0, j)),
                  pl.BlockSpec((D, tn), lambda j, i: (0, j + nf))],
        out_specs=[tile, tile, tile], out_shape=[f32, f32, jax.ShapeDtypeStruct((T, F), BF16)],
        compiler_params=_params("parallel", "parallel"),
    )(hn, w_in, w_in)


def _mm_swiglu_bwd(dout, w_out, gate, up, name):
    T, D = dout.shape
    F = w_out.shape[0]
    tm, tn, _ = _mm_tiles(T, F, D, budget=768 * 1024)
    assert (tm + tn) * D <= 4 * 1024 * 1024, "the whole contraction is held in one step"

    def body(a_ref, b_ref, g_ref, u_ref, dg_ref, du_ref):
        a = a_ref[...].astype(BF16)
        for lo, hi in _col_chunks(tn):
            d = _dot(a, b_ref[lo:hi, :].astype(BF16), NT) * 0.5
            gv, uv = g_ref[:, lo:hi], u_ref[:, lo:hi]
            s = _sigmoid(gv)
            gs = gv * s
            dg_ref[:, lo:hi] = (d * uv * (s + gs - gs * s)).astype(BF16)
            du_ref[:, lo:hi] = (d * gs).astype(BF16)

    tile = pl.BlockSpec((tm, tn), lambda j, i: (i, j))
    out = jax.ShapeDtypeStruct((T, F), BF16)
    return pl.pallas_call(
        body, name=name, grid=(F // tn, T // tm),
        in_specs=[pl.BlockSpec((tm, D), lambda j, i: (i, 0)), pl.BlockSpec((tn, D), lambda j, i: (j, 0)), tile, tile],
        out_specs=[tile, tile], out_shape=[out, out], compiler_params=_params("parallel", "parallel"),
    )(dout, w_out, gate, up)


def _iota2(shape, dim):
    return lax.broadcasted_iota(jnp.int32, shape, dim)


def _head_ones():
    r, c = _iota2((LANES, LANES), 0), _iota2((LANES, LANES), 1)
    return ((r >> 6) == (c >> 6)).astype(F32)


def _rot_matrix(transpose):
    r, c = _iota2((LANES, LANES), 0), _iota2((LANES, LANES), 1)
    if transpose:
        r, c = c, r
    half = ROT_DIM // 2
    cm = c & (HEAD_DIM - 1)
    neg = (r == c + half) & (cm < half)
    pos = (r == c - half) & (cm >= half) & (cm < ROT_DIM)
    return pos.astype(F32) - neg.astype(F32)


def _rope_tables(positions, name):
    T = positions.size
    tm = _pick(T, 1024, 8)
    inv = ROPE_THETA ** (-jnp.arange(0, ROT_DIM, 2, dtype=F32) / ROT_DIM)
    lane = jnp.concatenate([inv, inv, jnp.zeros((HEAD_DIM - ROT_DIM,), F32)])
    lane = jnp.tile(lane, 2).reshape(1, LANES)

    def body(p_ref, f_ref, c_ref, s_ref):
        ang = p_ref[...].astype(F32) * f_ref[...]
        c_ref[...] = jnp.cos(ang)
        s_ref[...] = jnp.sin(ang)

    out = jax.ShapeDtypeStruct((T, LANES), F32)
    blk = pl.BlockSpec((tm, LANES), lambda i: (i, 0))
    return pl.pallas_call(
        body, name=name, grid=(T // tm,),
        in_specs=[pl.BlockSpec((tm, 1), lambda i: (i, 0)), pl.BlockSpec((1, LANES), lambda i: (0, 0))],
        out_specs=[blk, blk], out_shape=[out, out], compiler_params=_params("parallel"),
    )(positions.reshape(T, 1), lane)


def _qknorm_fwd(src, col_off, gain, cs, D, out_dtype, name, out_scale=1.0):
    T = src.shape[0]
    HP = D // LANES
    tm = _pick(T, 1024, 8)
    co = col_off // LANES
    rope = cs is not None

    def body(*refs):
        if rope:
            x_ref, g_ref, c_ref, s_ref, o_ref = refs
        else:
            x_ref, g_ref, o_ref = refs
        xv = x_ref[...]
        ms = _dot_once(xv * xv, _head_ones()) * (1.0 / HEAD_DIM)
        y = xv * lax.rsqrt(ms + EPS) * g_ref[...]
        if rope:
            y = y * c_ref[...] + _dot_once(y, _rot_matrix(False)) * s_ref[...]
        o_ref[...] = (y * out_scale).astype(out_dtype)

    in_specs = [pl.BlockSpec((tm, LANES), lambda i, h: (i, co + h)), pl.BlockSpec((1, LANES), lambda i, h: (0, 0))]
    args = [src, jnp.tile(gain.reshape(1, HEAD_DIM), (1, 2))]
    if rope:
        in_specs += [pl.BlockSpec((tm, LANES), lambda i, h: (i, 0))] * 2
        args += list(cs)
    return pl.pallas_call(
        body, name=name, grid=(T // tm, HP), in_specs=in_specs,
        out_specs=pl.BlockSpec((tm, LANES), lambda i, h: (i, h)),
        out_shape=jax.ShapeDtypeStruct((T, D), out_dtype), compiler_params=_params("parallel", "parallel"),
    )(*args)


def _qknorm_bwd(src, col_off, gain, cs, dout, D, name, out_scale=1.0):
    T = src.shape[0]
    HP = D // LANES
    tm = _pick(T, 1024, 8)
    co = col_off // LANES
    rope = cs is not None

    def body(*refs):
        if rope:
            x_ref, g_ref, d_ref, c_ref, s_ref, dx_ref, dg_ref = refs
        else:
            x_ref, g_ref, d_ref, dx_ref, dg_ref = refs
        xv = x_ref[...]
        ones = _head_ones()
        ms = _dot_once(xv * xv, ones) * (1.0 / HEAD_DIM)
        r = lax.rsqrt(ms + EPS)
        xn = xv * r
        d = d_ref[...] * out_scale
        if rope:
            d = d * c_ref[...] + _dot_once(d * s_ref[...], _rot_matrix(True))
        dgy = d * g_ref[...]
        mean = _dot_once(dgy * xn, ones) * (1.0 / HEAD_DIM)
        dx_ref[...] = (r * (dgy - xn * mean)).astype(BF16)

        @pl.when(pl.program_id(1) == 0)
        def _():
            dg_ref[...] = jnp.zeros_like(dg_ref)

        dg_ref[...] += jnp.sum(d * xn, axis=0, keepdims=True)[None]

    in_specs = [pl.BlockSpec((tm, LANES), lambda h, i: (i, co + h)), pl.BlockSpec((1, LANES), lambda h, i: (0, 0)),
                pl.BlockSpec((tm, LANES), lambda h, i: (i, h))]
    args = [src, jnp.tile(gain.reshape(1, HEAD_DIM), (1, 2)), dout]
    if rope:
        in_specs += [pl.BlockSpec((tm, LANES), lambda h, i: (i, 0))] * 2
        args += list(cs)
    dx, dg = pl.pallas_call(
        body, name=name, grid=(HP, T // tm), in_specs=in_specs,
        out_specs=[pl.BlockSpec((tm, LANES), lambda h, i: (i, h)), pl.BlockSpec((1, 1, LANES), lambda h, i: (h, 0, 0))],
        out_shape=[jax.ShapeDtypeStruct((T, D), BF16), jax.ShapeDtypeStruct((HP, 1, LANES), F32)],
        compiler_params=_params("parallel", "arbitrary"),
    )(*args)
    return dx, dg.reshape(2 * HP, HEAD_DIM).sum(axis=0)


def _head_rowsum(a, b, name):
    T, D = a.shape
    tm = _pick(T, 1024, 8)

    def body(a_ref, b_ref, o_ref):
        o_ref[...] = _dot_split(a_ref[...] * b_ref[...], _head_ones())

    blk = pl.BlockSpec((tm, LANES), lambda i, h: (i, h))
    return pl.pallas_call(
        body, name=name, grid=(T // tm, D // LANES), in_specs=[blk, blk], out_specs=blk,
        out_shape=jax.ShapeDtypeStruct((T, D), F32), compiler_params=_params("parallel", "parallel"),
    )(a, b)


def _lane_masks():
    lane = _iota2((1, LANES), 1)
    return (lane < HEAD_DIM, lane >= HEAD_DIM)


BAND_GROUP = 4


def _band_geometry(g, S):
    window, dil = DILATED_GROUPS[g]
    chunk = BAND_BLOCK * dil
    assert window // dil == BAND_BLOCK and S % chunk == 0
    if dil == 1:
        assert (S // chunk) % BAND_GROUP == 0
        return dil, chunk, BAND_GROUP, 1
    assert dil % BAND_GROUP == 0
    return dil, chunk, 1, dil // BAND_GROUP


def _band_rows(i, rho, dil, shift=0):
    if dil > 1:
        return pl.ds(rho * BAND_GROUP + i, BAND_BLOCK, stride=dil), shift != 0
    j = i + shift
    outside = j < 0 or j >= BAND_GROUP
    return pl.ds(0 if outside else j * BAND_BLOCK, BAND_BLOCK), outside


def _stack_heads(x):
    hm0, hm1 = _lane_masks()
    zero = jnp.zeros_like(x)
    return jnp.concatenate([jnp.where(hm0, x, zero), jnp.where(hm1, x, zero)], axis=0)


def _unstack_heads(x2):
    hm0, _ = _lane_masks()
    return jnp.where(hm0, x2[:BAND_BLOCK], x2[BAND_BLOCK:])


def _band_masks():
    qi = _iota2((2 * BAND_BLOCK, 2 * BAND_BLOCK), 0) & (BAND_BLOCK - 1)
    kj = _iota2((2 * BAND_BLOCK, 2 * BAND_BLOCK), 1)
    own = (kj >= BAND_BLOCK) & (kj - BAND_BLOCK <= qi)
    prev = (kj < BAND_BLOCK) & (kj >= qi)
    return own, prev


def _band_fwd(qh, kh, qkv, g, B, S, D, name):
    dil, CH, NB, RG = _band_geometry(g, S)
    HP = D // LANES
    vo = (3 * g + 2) * HP
    nsp = S // (CH * NB)

    def body(q_ref, k_ref, kp_ref, v_ref, vp_ref, o_ref, lse_ref):
        n, rho = pl.program_id(1), pl.program_id(3)
        own, prev = _band_masks()
        for i in range(BAND_GROUP):
            rows, _ = _band_rows(i, rho, dil)
            prows, outside = _band_rows(i, rho, dil, shift=-1)
            q, kc, vc = (t[0, rows, :].astype(BF16) for t in (q_ref, k_ref, v_ref))
            kp, vp = (t[0, prows, :].astype(BF16) for t in ((kp_ref, vp_ref) if outside else (k_ref, v_ref)))
            valid = (own | (prev & (n > 0))) if outside else (own | prev)
            s = jnp.where(valid, _dot(_stack_heads(q), jnp.concatenate([kp, kc], axis=0), NT), NEG)
            m = jnp.max(s, axis=-1, keepdims=True)
            e = jnp.exp(s - m)
            l = jnp.sum(e, axis=-1, keepdims=True)
            o2 = _dot((e * (1.0 / l)).astype(BF16), jnp.concatenate([vp, vc], axis=0), NN)
            o_ref[0, rows, :] = _unstack_heads(o2)
            lse_ref[0, rows, :] = _unstack_heads(m + jnp.log(l))

    span, chunk = (1, CH * NB, LANES), (1, CH, LANES)
    cur = lambda b, n, h, r: (b, n, h)
    prev = lambda b, n, h, r: (b, jnp.maximum(n * NB - 1, 0), h)
    vcur = lambda b, n, h, r: (b, n, vo + h)
    vprev = lambda b, n, h, r: (b, jnp.maximum(n * NB - 1, 0), vo + h)
    out = jax.ShapeDtypeStruct((B, S, D), F32)
    q3, k3, v3 = qh.reshape(B, S, D), kh.reshape(B, S, D), qkv.reshape(B, S, 9 * D)
    o, lse = pl.pallas_call(
        body, name=name, grid=(B, nsp, HP, RG),
        in_specs=[pl.BlockSpec(span, cur), pl.BlockSpec(span, cur), pl.BlockSpec(chunk, prev),
                  pl.BlockSpec(span, vcur), pl.BlockSpec(chunk, vprev)],
        out_specs=[pl.BlockSpec(span, cur), pl.BlockSpec(span, cur)], out_shape=[out, out],
        compiler_params=_params("parallel", "parallel", "parallel", "arbitrary"),
    )(q3, k3, k3, v3, v3)
    return o.reshape(B * S, D), lse.reshape(B * S, D)


def _head_col(x, h):
    lane = _iota2((1, LANES), 1)
    return jnp.sum(jnp.where(lane == h * HEAD_DIM, x, 0.0), axis=-1, keepdims=True)


def _band_bwd(qh, kh, qkv, do, lse, delta, g, B, S, D, name):
    dil, CH, NB, RG = _band_geometry(g, S)
    HP = D // LANES
    vo = (3 * g + 2) * HP
    nsp, nch = S // (CH * NB), S // CH

    def stacked_cols(tile):
        return jnp.concatenate([_head_col(tile, 0), _head_col(tile, 1)], axis=0)

    def body(q_ref, qn_ref, do_ref, don_ref, l_ref, ln_ref, d_ref, dn_ref, k_ref, kp_ref, v_ref, vp_ref,
             dq_ref, dk_ref, dv_ref):
        n, rho = pl.program_id(1), pl.program_id(3)
        own, prev = _band_masks()
        band = (_iota2((2 * BAND_BLOCK, BAND_BLOCK), 1) >= (_iota2((2 * BAND_BLOCK, BAND_BLOCK), 0) & (BAND_BLOCK - 1)))
        for i in range(BAND_GROUP):
            rows, _ = _band_rows(i, rho, dil)
            prows, p_out = _band_rows(i, rho, dil, shift=-1)
            nrows, n_out = _band_rows(i, rho, dil, shift=1)
            kc, vc = (t[0, rows, :].astype(BF16) for t in (k_ref, v_ref))
            kp, vp = (t[0, prows, :].astype(BF16) for t in ((kp_ref, vp_ref) if p_out else (k_ref, v_ref)))
            q2, do2 = (_stack_heads(t[0, rows, :].astype(BF16)) for t in (q_ref, do_ref))
            qn2, don2 = (_stack_heads(t[0, nrows, :].astype(BF16)) for t in ((qn_ref, don_ref) if n_out else (q_ref, do_ref)))
            lse2, del2 = (stacked_cols(t[0, rows, :]) for t in (l_ref, d_ref))
            lsen2, deln2 = (stacked_cols(t[0, nrows, :]) for t in ((ln_ref, dn_ref) if n_out else (l_ref, d_ref)))
            valid = (own | (prev & (n > 0))) if p_out else (own | prev)
            next_ok = (band & (n < nsp - 1)) if n_out else band
            k2, v2 = jnp.concatenate([kp, kc], axis=0), jnp.concatenate([vp, vc], axis=0)
            p = jnp.exp(jnp.where(valid, _dot(q2, k2, NT) - lse2, NEG))
            ds = (p * (_dot(do2, v2, NT) - del2)).astype(BF16)
            pc = jnp.exp(jnp.where(next_ok, _dot(qn2, kc, NT) - lsen2, NEG))
            dsc = (pc * (_dot(don2, vc, NT) - deln2)).astype(BF16)
            dq_ref[0, rows, :] = _unstack_heads(_dot(ds, k2, NN))
            dk_ref[0, rows, :] = _dot(ds[:, BAND_BLOCK:], q2, TN) + _dot(dsc, qn2, TN)
            dv_ref[0, rows, :] = _dot(p[:, BAND_BLOCK:].astype(BF16), do2, TN) + _dot(pc.astype(BF16), don2, TN)

    span, chunk = (1, CH * NB, LANES), (1, CH, LANES)
    cur = lambda b, n, h, r: (b, n, h)
    prev = lambda b, n, h, r: (b, jnp.maximum(n * NB - 1, 0), h)
    nxt = lambda b, n, h, r: (b, jnp.minimum((n + 1) * NB, nch - 1), h)
    vcur = lambda b, n, h, r: (b, n, vo + h)
    vprev = lambda b, n, h, r: (b, jnp.maximum(n * NB - 1, 0), vo + h)
    q3, k3, do3, l3, d3 = (t.reshape(B, S, D) for t in (qh, kh, do, lse, delta))
    v3 = qkv.reshape(B, S, 9 * D)
    sp, ch = (lambda m: pl.BlockSpec(span, m)), (lambda m: pl.BlockSpec(chunk, m))
    out = jax.ShapeDtypeStruct((B, S, D), F32)
    dq, dk, dv = pl.pallas_call(
        body, name=name, grid=(B, nsp, HP, RG),
        in_specs=[sp(cur), ch(nxt), sp(cur), ch(nxt), sp(cur), ch(nxt), sp(cur), ch(nxt),
                  sp(cur), ch(prev), sp(vcur), ch(vprev)],
        out_specs=[sp(cur)] * 3, out_shape=[out] * 3,
        compiler_params=_params("parallel", "parallel", "parallel", "arbitrary"),
    )(q3, q3, do3, do3, l3, l3, d3, d3, k3, k3, v3, v3)
    return dq.reshape(B * S, D), dk.reshape(B * S, D), dv.reshape(B * S, D)


def _mix(os_, lses, name):
    T, D = os_[0].shape
    tm = _pick(T, 256, 8)

    def body(o0, o1, o2, l0, l1, l2, mix_ref, lse_ref):
        a, b, c = l0[...], l1[...], l2[...]
        m = jnp.maximum(jnp.maximum(a, b), c)
        ea, eb, ec = jnp.exp(a - m), jnp.exp(b - m), jnp.exp(c - m)
        s = ea + eb + ec
        mix_ref[...] = (ea / s) * o0[...] + (eb / s) * o1[...] + (ec / s) * o2[...]
        lse_ref[...] = m + jnp.log(s)

    blk = pl.BlockSpec((tm, D), lambda i: (i, 0))
    out = jax.ShapeDtypeStruct((T, D), F32)
    return pl.pallas_call(
        body, name=name, grid=(T // tm,), in_specs=[blk] * 6, out_specs=[blk, blk], out_shape=[out, out],
        compiler_params=_params("parallel"),
    )(*os_, *lses)


def _gate_fwd(fpre, b_f, B, S, D, name):
    nb = S // LANES

    def body(f_ref, b_ref, cum_ref, rep_ref, carry):
        @pl.when(pl.program_id(1) == 0)
        def _():
            carry[...] = jnp.zeros_like(carry)

        z = f_ref[...] + b_ref[...]
        lf = jnp.minimum(z, 0.0) - jnp.log(1.0 + jnp.exp(-jnp.abs(z)))
        tri = (_iota2((LANES, LANES), 0) >= _iota2((LANES, LANES), 1)).astype(F32)
        c = _dot(tri, lf, NN, HIGHEST) + carry[...]
        cum_ref[...] = c
        carry[...] += jnp.sum(lf, axis=0, keepdims=True)
        spread = ((_iota2((LANES, D), 1) >> 6) == _iota2((LANES, D), 0)).astype(F32)
        rep_ref[...] = _dot(c, spread, NN, HIGHEST)

    return pl.pallas_call(
        body, name=name, grid=(B, nb),
        in_specs=[pl.BlockSpec((LANES, LANES), lambda b, j: (b * nb + j, 0)), pl.BlockSpec((1, LANES), lambda b, j: (0, 0))],
        out_specs=[pl.BlockSpec((LANES, LANES), lambda b, j: (b * nb + j, 0)), pl.BlockSpec((LANES, D), lambda b, j: (b * nb + j, 0))],
        out_shape=[jax.ShapeDtypeStruct((B * S, LANES), F32), jax.ShapeDtypeStruct((B * S, D), F32)],
        scratch_shapes=[pltpu.VMEM((1, LANES), F32)], compiler_params=_params("arbitrary", "arbitrary"),
    )(fpre, b_f)


def _gate_bwd(dcum_k, dcum_q, fpre, b_f, B, S, name):
    nb = S // LANES

    def body(d_ref, dq_ref, f_ref, b_ref, o_ref, db_ref, carry):
        first = (pl.program_id(0) == 0) & (pl.program_id(1) == 0)

        @pl.when(pl.program_id(1) == 0)
        def _():
            carry[...] = jnp.zeros_like(carry)

        @pl.when(first)
        def _():
            db_ref[...] = jnp.zeros_like(db_ref)

        d = d_ref[...] + dq_ref[...]
        tri = (_iota2((LANES, LANES), 0) <= _iota2((LANES, LANES), 1)).astype(F32)
        dlf = _dot(tri, d, NN, HIGHEST) + carry[...]
        carry[...] += jnp.sum(d, axis=0, keepdims=True)
        z = f_ref[...] + b_ref[...]
        dz = dlf * (1.0 / (1.0 + jnp.exp(z)))
        o_ref[...] = dz.astype(BF16)
        db_ref[...] += jnp.sum(dz, axis=0, keepdims=True)

    rev = lambda b, j: (b * nb + nb - 1 - j, 0)
    blk = pl.BlockSpec((LANES, LANES), rev)
    vec = pl.BlockSpec((1, LANES), lambda b, j: (0, 0))
    return pl.pallas_call(
        body, name=name, grid=(B, nb), in_specs=[blk, blk, blk, vec], out_specs=[blk, vec],
        out_shape=[jax.ShapeDtypeStruct((B * S, LANES), BF16), jax.ShapeDtypeStruct((1, LANES), F32)],
        scratch_shapes=[pltpu.VMEM((1, LANES), F32)], compiler_params=_params("arbitrary", "arbitrary"),
    )(dcum_k, dcum_q, fpre, b_f)


def _fox_scores(q, k, ck_row, hm, diagonal):
    TB = q.shape[0]
    qm = jnp.where(hm, q, jnp.zeros_like(q))
    t = _dot(qm, k, NT) - ck_row
    if diagonal:
        t = jnp.where(_iota2((TB, TB), 1) <= _iota2((TB, TB), 0), t, NEG)
    return qm, t


def _fox_fetch(srcs, bufs, sems, slot):
    return [pltpu.make_async_copy(src, buf.at[slot], sems.at[slot, i]) for i, (src, buf) in enumerate(zip(srcs, bufs))]


FOX_BLOCK = 256
EXP_UNDERFLOW = 104.0


def _fox_live_blocks(cum, gain_q, gain_k, B, S, D):
    H, HP = D // HEAD_DIM, D // LANES
    nb = S // FOX_BLOCK
    bound = 1.02 * HEAD_DIM * SCALE * jnp.max(jnp.abs(gain_q)) * jnp.max(jnp.abs(gain_k))
    c = cum.reshape(B, S, LANES)[:, :, :H]
    decay = c[:, 0::FOX_BLOCK, None, :] - c[:, None, FOX_BLOCK - 1::FOX_BLOCK, :]
    below = jnp.arange(nb)[None, :] < jnp.arange(nb)[:, None]
    dead = (decay < -(EXP_UNDERFLOW + 2.0 * bound)) & below[None, :, :, None]
    live = jnp.logical_not(dead.reshape(B, nb, nb, HP, 2).all(axis=-1))
    first = jnp.argmax(live, axis=2)
    last = nb - 1 - jnp.argmax(live[:, ::-1], axis=1)
    to_table = lambda t: t.transpose(0, 2, 1).reshape(B * HP * nb).astype(jnp.int32)
    return to_table(first), to_table(last)


def _fox_fwd(qh, kh, kv, cq, ck, first, B, S, D, name):
    HP = D // LANES
    TB = FOX_BLOCK
    nb = S // TB

    def body(first_ref, q_ref, cq_ref, k_ref, v_ref, ck_ref, kp_ref, vp_ref, ckp_ref, k_hbm, v_hbm, ck_hbm,
             o_ref, lse_ref, kbuf, vbuf, cbuf, sems, m_sc, l_sc, acc):
        b, qb, hp = pl.program_id(0), pl.program_id(1), pl.program_id(2)
        kb0 = first_ref[(b * HP + hp) * nb + qb]
        n_far = jnp.maximum(qb - 1 - kb0, 0)

        def fetch(kb, slot):
            rows = pl.ds(pl.multiple_of(kb * TB, TB), TB)
            srcs = (k_hbm.at[b, rows, pl.ds(pl.multiple_of(hp * LANES, LANES), LANES)],
                    v_hbm.at[b, rows, pl.ds(pl.multiple_of((HP + hp) * LANES, LANES), LANES)],
                    ck_hbm.at[b * HP + hp, :, rows])
            return _fox_fetch(srcs, (kbuf, vbuf, cbuf), sems, slot)

        @pl.when(n_far > 0)
        def _():
            for cp in fetch(kb0, 0):
                cp.start()

        m_sc[...] = jnp.full_like(m_sc, NEG)
        l_sc[...] = jnp.zeros_like(l_sc)
        acc[...] = jnp.zeros_like(acc)

        def step(k, v, ck_rows, diagonal):
            q, v = q_ref[0], v.astype(BF16)
            cqv = cq_ref[0]
            a = acc[...]
            for h, hm in enumerate(_lane_masks()):
                _, t = _fox_scores(q, k, ck_rows[h], hm, diagonal)
                cqh = _head_col(cqv, h)
                m_prev = jnp.max(m_sc[h], axis=-1, keepdims=True)
                l_prev = jnp.max(l_sc[h], axis=-1, keepdims=True)
                m_new = jnp.maximum(m_prev, jnp.max(t, axis=-1, keepdims=True) + cqh)
                alpha = jnp.exp(m_prev - m_new)
                p = jnp.exp(t + (cqh - m_new))
                l_new = alpha * l_prev + jnp.sum(p, axis=-1, keepdims=True)
                a = jnp.where(hm, alpha * a + _dot(p.astype(BF16), v, NN), a)
                m_sc[h] = jnp.broadcast_to(m_new, (TB, LANES))
                l_sc[h] = jnp.broadcast_to(l_new, (TB, LANES))
            acc[...] = a

        def walk(i, carry):
            slot = lax.rem(i, 2)

            @pl.when(i + 1 < n_far)
            def _():
                for cp in fetch(kb0 + i + 1, 1 - slot):
                    cp.start()

            for cp in fetch(kb0 + i, slot):
                cp.wait()
            step(kbuf[slot], vbuf[slot], [cbuf[slot, pl.ds(h, 1), :] for h in range(2)], False)
            return carry

        lax.fori_loop(0, n_far, walk, 0)
        pl.when(kb0 < qb)(lambda: step(kp_ref[0], vp_ref[0], [ckp_ref[0, h:h + 1, :] for h in range(2)], False))
        step(k_ref[0], v_ref[0], [ck_ref[0, h:h + 1, :] for h in range(2)], True)
        a = acc[...]
        out = a
        lse = jnp.zeros_like(a)
        for h, hm in enumerate(_lane_masks()):
            out = jnp.where(hm, a / l_sc[h], out)
            lse = jnp.where(hm, m_sc[h] + jnp.log(l_sc[h]), lse)
        o_ref[0] = out
        lse_ref[0] = lse

    out = jax.ShapeDtypeStruct((B, S, D), F32)
    spec = pltpu.PrefetchScalarGridSpec(
        num_scalar_prefetch=1, grid=(B, nb, HP),
        in_specs=_fox_q_specs(2, TB) + _fox_k_specs(TB, HP, 0) + _fox_k_specs(TB, HP, -1) + [_ANY] * 3,
        out_specs=_fox_q_specs(2, TB), scratch_shapes=_fox_k_buffers(TB) + [
            pltpu.VMEM((2, TB, LANES), F32), pltpu.VMEM((2, TB, LANES), F32), pltpu.VMEM((TB, LANES), F32)])
    k3, v3 = kh.reshape(B, S, D), kv.reshape(B, S, 2 * D)
    o, lse = pl.pallas_call(
        body, name=name, grid_spec=spec, out_shape=[out, out], compiler_params=_params("parallel", "parallel", "parallel"),
    )(first, qh.reshape(B, S, D), cq.reshape(B, S, D), k3, v3, ck, k3, v3, ck, k3, v3, ck)
    return o.reshape(B * S, D), lse.reshape(B * S, D)


_ANY = pl.BlockSpec(memory_space=pl.ANY)


def _fox_q_specs(n, TB):
    return [pl.BlockSpec((1, TB, LANES), lambda b, i, h, table: (b, i, h))] * n


def _fox_k_specs(TB, HP, shift):
    blk = lambda i: jnp.maximum(i + shift, 0)
    return [pl.BlockSpec((1, TB, LANES), lambda b, i, h, table: (b, blk(i), h)),
            pl.BlockSpec((1, TB, LANES), lambda b, i, h, table: (b, blk(i), HP + h)),
            pl.BlockSpec((1, 8, TB), lambda b, i, h, table: (b * HP + h, 0, blk(i)))]


def _fox_k_buffers(TB):
    return [pltpu.VMEM((2, TB, LANES), BF16), pltpu.VMEM((2, TB, LANES), F32), pltpu.VMEM((2, 8, TB), F32),
            pltpu.SemaphoreType.DMA((2, 3))]


def _fox_bwd_dq(qh, kh, kv, cq, ck, first, do, lse, delta, B, S, D, name):
    HP = D // LANES
    TB = FOX_BLOCK
    nb = S // TB

    def body(first_ref, q_ref, cq_ref, do_ref, l_ref, d_ref, k_ref, v_ref, ck_ref, kp_ref, vp_ref, ckp_ref,
             k_hbm, v_hbm, ck_hbm, dq_ref, dc_ref, kbuf, vbuf, cbuf, sems, acc, acc_c):
        b, qb, hp = pl.program_id(0), pl.program_id(1), pl.program_id(2)
        kb0 = first_ref[(b * HP + hp) * nb + qb]
        n_far = jnp.maximum(qb - 1 - kb0, 0)

        def fetch(kb, slot):
            rows = pl.ds(pl.multiple_of(kb * TB, TB), TB)
            srcs = (k_hbm.at[b, rows, pl.ds(pl.multiple_of(hp * LANES, LANES), LANES)],
                    v_hbm.at[b, rows, pl.ds(pl.multiple_of((HP + hp) * LANES, LANES), LANES)],
                    ck_hbm.at[b * HP + hp, :, rows])
            return _fox_fetch(srcs, (kbuf, vbuf, cbuf), sems, slot)

        @pl.when(n_far > 0)
        def _():
            for cp in fetch(kb0, 0):
                cp.start()

        acc[...] = jnp.zeros_like(acc)
        acc_c[...] = jnp.zeros_like(acc_c)

        def step(k, v, ck_rows, diagonal):
            q, v = q_ref[0], v.astype(BF16)
            dout = do_ref[0].astype(BF16)
            a, c = acc[...], acc_c[...]
            for h, hm in enumerate(_lane_masks()):
                _, t = _fox_scores(q, k, ck_rows[h], hm, diagonal)
                p = jnp.exp(t + (_head_col(cq_ref[0], h) - _head_col(l_ref[0], h)))
                dom = jnp.where(hm, dout, jnp.zeros_like(dout))
                ds = p * (_dot(dom, v, NT) - _head_col(d_ref[0], h))
                a += _dot(ds.astype(BF16), jnp.where(hm, k, jnp.zeros_like(k)), NN)
                c = jnp.where(hm, c + jnp.sum(ds, axis=-1, keepdims=True), c)
            acc[...] = a
            acc_c[...] = c

        def walk(i, carry):
            slot = lax.rem(i, 2)

            @pl.when(i + 1 < n_far)
            def _():
                for cp in fetch(kb0 + i + 1, 1 - slot):
                    cp.start()

            for cp in fetch(kb0 + i, slot):
                cp.wait()
            step(kbuf[slot], vbuf[slot], [cbuf[slot, pl.ds(h, 1), :] for h in range(2)], False)
            return carry

        lax.fori_loop(0, n_far, walk, 0)
        pl.when(kb0 < qb)(lambda: step(kp_ref[0], vp_ref[0], [ckp_ref[0, h:h + 1, :] for h in range(2)], False))
        step(k_ref[0], v_ref[0], [ck_ref[0, h:h + 1, :] for h in range(2)], True)
        dq_ref[0] = acc[...]
        dc_ref[0] = acc_c[...]

    r3 = lambda t: t.reshape(B, S, D)
    out = jax.ShapeDtypeStruct((B, S, D), F32)
    spec = pltpu.PrefetchScalarGridSpec(
        num_scalar_prefetch=1, grid=(B, nb, HP),
        in_specs=_fox_q_specs(5, TB) + _fox_k_specs(TB, HP, 0) + _fox_k_specs(TB, HP, -1) + [_ANY] * 3,
        out_specs=_fox_q_specs(2, TB),
        scratch_shapes=_fox_k_buffers(TB) + [pltpu.VMEM((TB, LANES), F32), pltpu.VMEM((TB, LANES), F32)])
    k3, v3 = r3(kh), kv.reshape(B, S, 2 * D)
    dq, dc = pl.pallas_call(
        body, name=name, grid_spec=spec, out_shape=[out, out], compiler_params=_params("parallel", "parallel", "parallel"),
    )(first, r3(qh), r3(cq), r3(do), r3(lse), r3(delta), k3, v3, ck, k3, v3, ck, k3, v3, ck)
    return dq.reshape(B * S, D), dc.reshape(B * S, D)


def _fox_bwd_dkv(qh, kh, kv, cq, ck, last, do, lse, delta, B, S, D, name):
    HP = D // LANES
    TB = FOX_BLOCK
    nb = S // TB

    def body(last_ref, k_ref, v_ref, ck_ref, q_ref, cq_ref, do_ref, l_ref, d_ref, qn_ref, cqn_ref, don_ref, ln_ref,
             dn_ref, q_hbm, cq_hbm, do_hbm, l_hbm, d_hbm, dk_ref, dv_ref, dc_ref,
             qbuf, cqbuf, dobuf, lbuf, dbuf, sems, dk_acc, dv_acc, dc_acc):
        b, kb, hp = pl.program_id(0), pl.program_id(1), pl.program_id(2)
        qb_last = last_ref[(b * HP + hp) * nb + kb]
        n_far = jnp.maximum(qb_last - kb - 1, 0)

        def fetch(qb, slot):
            rows = pl.ds(pl.multiple_of(qb * TB, TB), TB)
            cols = pl.ds(pl.multiple_of(hp * LANES, LANES), LANES)
            srcs = [t.at[b, rows, cols] for t in (q_hbm, cq_hbm, do_hbm, l_hbm, d_hbm)]
            return _fox_fetch(srcs, (qbuf, cqbuf, dobuf, lbuf, dbuf), sems, slot)

        @pl.when(n_far > 0)
        def _():
            for cp in fetch(kb + 2, 0):
                cp.start()

        dk_acc[...] = jnp.zeros_like(dk_acc)
        dv_acc[...] = jnp.zeros_like(dv_acc)
        dc_acc[...] = jnp.zeros_like(dc_acc)

        def step(q, cqv, dout, lse_t, del_t, diagonal):
            k, v = k_ref[0], v_ref[0].astype(BF16)
            dout = dout.astype(BF16)
            dk, dv, dc = dk_acc[...], dv_acc[...], dc_acc[...]
            row = _iota2((8, TB), 0)
            for h, hm in enumerate(_lane_masks()):
                qm, t = _fox_scores(q, k, ck_ref[0, h:h + 1, :], hm, diagonal)
                p = jnp.exp(t + (_head_col(cqv, h) - _head_col(lse_t, h)))
                dom = jnp.where(hm, dout, jnp.zeros_like(dout))
                ds = p * (_dot(dom, v, NT) - _head_col(del_t, h))
                dv += _dot(p.astype(BF16), dom, TN)
                dk += _dot(ds.astype(BF16), qm, TN)
                dc = jnp.where(row == h, dc - jnp.sum(ds, axis=0, keepdims=True), dc)
            dk_acc[...] = dk
            dv_acc[...] = dv
            dc_acc[...] = dc

        step(q_ref[0], cq_ref[0], do_ref[0], l_ref[0], d_ref[0], True)
        pl.when(kb < qb_last)(lambda: step(qn_ref[0], cqn_ref[0], don_ref[0], ln_ref[0], dn_ref[0], False))

        def walk(i, carry):
            slot = lax.rem(i, 2)

            @pl.when(i + 1 < n_far)
            def _():
                for cp in fetch(kb + 3 + i, 1 - slot):
                    cp.start()

            for cp in fetch(kb + 2 + i, slot):
                cp.wait()
            step(qbuf[slot], cqbuf[slot], dobuf[slot], lbuf[slot], dbuf[slot], False)
            return carry

        lax.fori_loop(0, n_far, walk, 0)
        dk_ref[0] = dk_acc[...]
        dv_ref[0] = dv_acc[...].astype(BF16)
        dc_ref[0] = dc_acc[...]

    blk = (1, TB, LANES)
    kmap = lambda b, i, h, last: (b, i, h)
    vmap = lambda b, i, h, last: (b, i, HP + h)
    cmap = lambda b, i, h, last: (b * HP + h, 0, i)
    nmap = lambda b, i, h, last: (b, jnp.minimum(i + 1, nb - 1), h)
    r3 = lambda t: t.reshape(B, S, D)
    tile = lambda dt: pltpu.VMEM((2, TB, LANES), dt)
    qside = [r3(qh), r3(cq), r3(do), r3(lse), r3(delta)]
    spec = pltpu.PrefetchScalarGridSpec(
        num_scalar_prefetch=1, grid=(B, nb, HP),
        in_specs=[pl.BlockSpec(blk, kmap), pl.BlockSpec(blk, vmap), pl.BlockSpec((1, 8, TB), cmap)]
        + [pl.BlockSpec(blk, kmap)] * 5 + [pl.BlockSpec(blk, nmap)] * 5 + [_ANY] * 5,
        out_specs=[pl.BlockSpec(blk, kmap), pl.BlockSpec(blk, kmap), pl.BlockSpec((1, 8, TB), cmap)],
        scratch_shapes=[tile(BF16), tile(F32), tile(F32), tile(F32), tile(F32), pltpu.SemaphoreType.DMA((2, 5)),
                        pltpu.VMEM((TB, LANES), F32), pltpu.VMEM((TB, LANES), F32), pltpu.VMEM((8, TB), F32)])
    dk, dv, dc = pl.pallas_call(
        body, name=name, grid_spec=spec,
        out_shape=[jax.ShapeDtypeStruct((B, S, D), F32), jax.ShapeDtypeStruct((B, S, D), BF16),
                   jax.ShapeDtypeStruct((B * HP, 8, S), F32)],
        compiler_params=_params("parallel", "parallel", "parallel"),
    )(last, r3(kh), kv.reshape(B, S, 2 * D), ck, *qside, *qside, *qside)
    return dk.reshape(B * S, D), dv.reshape(B * S, D), dc


def _loss_and_grad(y, target, name):
    T, D = y.shape
    tm = _pick(T, 256, 8)

    def body(y_ref, t_ref, p_ref, dy_ref):
        e = y_ref[...] - t_ref[...]
        dy_ref[...] = e * (1.0 / D)
        p_ref[...] = jnp.sum(e * e, axis=0, keepdims=True)[None]

    blk = pl.BlockSpec((tm, D), lambda i: (i, 0))
    return pl.pallas_call(
        body, name=name, grid=(T // tm,), in_specs=[blk, blk],
        out_specs=[pl.BlockSpec((1, 1, D), lambda i: (i, 0, 0)), blk],
        out_shape=[jax.ShapeDtypeStruct((T // tm, 1, D), F32), jax.ShapeDtypeStruct((T, D), F32)],
        compiler_params=_params("parallel"),
    )(y, target)


def _pair_sum(x, got, name):
    _, R, C = x.shape
    tr = _pick(R, max(16, (8 * PACK_ROWS * LANES) // C), 16)

    def body(c_ref, x_ref, g_ref, o_ref):
        o_ref[0] = (x_ref[0].astype(F32) + g_ref[0].astype(F32)).astype(o_ref.dtype)

    spec = pltpu.PrefetchScalarGridSpec(
        num_scalar_prefetch=1, grid=(4, R // tr),
        in_specs=[pl.BlockSpec((1, tr, C), lambda a, i, c: (2 * a + c[0], i, 0)),
                  pl.BlockSpec((1, tr, C), lambda a, i, c: (a, i, 0))],
        out_specs=pl.BlockSpec((1, tr, C), lambda a, i, c: (a, i, 0)))
    return pl.pallas_call(
        body, name=name, grid_spec=spec, out_shape=jax.ShapeDtypeStruct((4, R, C), x.dtype),
        compiler_params=_params("parallel", "parallel"),
    )(lax.axis_index("c").astype(jnp.int32).reshape(1), x, got)


def _adamw(parts, w, m, v, name):
    R, C = w.shape
    P = parts.shape[0]
    tr = _pick(R, max(16, (PACK_ROWS * LANES) // C), 16)

    def body(p_ref, w_ref, m_ref, v_ref, g_out, d_out, m_out, v_out):
        g = p_ref[0].astype(F32)
        for s in range(1, P):
            g = g + p_ref[s].astype(F32)
        mn = ADAM_B1 * m_ref[...] + (1.0 - ADAM_B1) * g
        vn = ADAM_B2 * v_ref[...] + (1.0 - ADAM_B2) * (g * g)
        m_hat = mn / (1.0 - ADAM_B1 ** ADAM_STEP)
        v_hat = vn / (1.0 - ADAM_B2 ** ADAM_STEP)
        g_out[...] = g
        d_out[...] = -ADAM_LR * (m_hat / (jnp.sqrt(v_hat) + ADAM_EPS) + ADAM_WD * w_ref[...])
        m_out[...] = mn
        v_out[...] = vn

    blk = pl.BlockSpec((tr, C), lambda i: (i, 0))
    out = jax.ShapeDtypeStruct((R, C), F32)
    return pl.pallas_call(
        body, name=name, grid=(R // tr,),
        in_specs=[pl.BlockSpec((P, tr, C), lambda i: (0, i, 0)), blk, blk, blk],
        out_specs=[blk] * 4, out_shape=[out] * 4, compiler_params=_params("parallel"),
    )(parts, w, m, v)


def _mesh_pos():
    return lax.axis_index("x"), lax.axis_index("y"), lax.axis_index("c")


def _comm_call(body, xs, out_shapes, name):
    n = len(xs)
    any_spec = pl.BlockSpec(memory_space=pl.ANY)
    return pl.pallas_call(
        body, name=name, out_shape=out_shapes, in_specs=[any_spec] * n, out_specs=[any_spec] * n,
        scratch_shapes=[pltpu.SemaphoreType.DMA((7, n)), pltpu.SemaphoreType.DMA((7, n)), pltpu.SemaphoreType.DMA((n,))],
    )(*xs)


def _all_gather(xs, name):
    n = len(xs)

    def body(*refs):
        x_refs, out_refs = refs[:n], refs[n:2 * n]
        send_sems, recv_sems, local_sems = refs[2 * n:]
        x_, y_, c_ = _mesh_pos()
        me, sibling = (x_, y_, c_), (x_, y_, 1 - c_)
        chips = [(1 - x_, y_), (x_, 1 - y_), (1 - x_, 1 - y_)]

        def slot(t, px, py, pc):
            return out_refs[t].at[4 * px + 2 * py + pc]

        def copy(t, k, block, to, src=None):
            return pltpu.make_async_remote_copy(
                src_ref=slot(t, *block) if src is None else src, dst_ref=slot(t, *block),
                send_sem=send_sems.at[k, t], recv_sem=recv_sems.at[k, t], device_id=to, device_id_type=MESH_ID)

        mine = [pltpu.make_async_copy(x_refs[t], slot(t, *me), local_sems.at[t]) for t in range(n)]
        first = [copy(t, 0, me, sibling, src=x_refs[t]) for t in range(n)]
        first += [copy(t, 1 + j, me, (*chip, c_), src=x_refs[t]) for j, chip in enumerate(chips) for t in range(n)]
        for cp in mine + first:
            cp.start()
        passed = []
        for j, chip in enumerate(chips):
            for t in range(n):
                copy(t, 1 + j, (*chip, c_), me).wait_recv()
                passed.append(copy(t, 4 + j, (*chip, c_), sibling))
                passed[-1].start()
        for t in range(n):
            copy(t, 0, sibling, me).wait_recv()
        for j, chip in enumerate(chips):
            for t in range(n):
                copy(t, 4 + j, (*chip, 1 - c_), me).wait_recv()
        for cp in first + passed:
            cp.wait_send()
        for cp in mine:
            cp.wait()

    return _comm_call(body, xs, [jax.ShapeDtypeStruct((N_DEV,) + x.shape, x.dtype) for x in xs], name)


def _pair_exchange(xs, name):
    n = len(xs)

    def body(*refs):
        x_refs, out_refs = refs[:n], refs[n:2 * n]
        send_sems, recv_sems, _ = refs[2 * n:]
        x_, y_, c_ = _mesh_pos()
        copies = [pltpu.make_async_remote_copy(
            src_ref=x_refs[t].at[2 * a + 1 - c_], dst_ref=out_refs[t].at[a],
            send_sem=send_sems.at[a, t], recv_sem=recv_sems.at[a, t],
            device_id=(x_, y_, 1 - c_), device_id_type=MESH_ID) for a in range(4) for t in range(n)]
        for cp in copies:
            cp.start()
        for cp in copies:
            cp.wait_recv()
        for cp in copies:
            cp.wait_send()

    return _comm_call(body, xs, [jax.ShapeDtypeStruct((4,) + x.shape[1:], x.dtype) for x in xs], name)


def _chip_exchange(xs, name):
    n = len(xs)

    def body(*refs):
        x_refs, out_refs = refs[:n], refs[n:2 * n]
        send_sems, recv_sems, local_sems = refs[2 * n:]
        x_, y_, c_ = _mesh_pos()
        chip = 2 * x_ + y_
        mine = [pltpu.make_async_copy(x_refs[t].at[chip], out_refs[t].at[chip], local_sems.at[t]) for t in range(n)]
        copies = []
        for k in range(1, 4):
            px, py = x_ ^ (k >> 1), y_ ^ (k & 1)
            for t in range(n):
                copies.append(pltpu.make_async_remote_copy(
                    src_ref=x_refs[t].at[2 * px + py], dst_ref=out_refs[t].at[chip],
                    send_sem=send_sems.at[k - 1, t], recv_sem=recv_sems.at[k - 1, t],
                    device_id=(px, py, c_), device_id_type=MESH_ID))
        for cp in mine + copies:
            cp.start()
        for cp in copies:
            cp.wait_recv()
        for cp in copies:
            cp.wait_send()
        for cp in mine:
            cp.wait()

    return _comm_call(body, xs, [jax.ShapeDtypeStruct(x.shape, x.dtype) for x in xs], name)


def _ffn_fwd(h, norm, w_in, w_out, tag):
    T, D = h.shape
    F = w_out.shape[0]
    hn = _rmsnorm_fwd(h, norm, f"{tag}_norm")
    gate, up, act = _mm_swiglu_fwd(hn, w_in, f"{tag}_in")
    out = _mm(act, w_out, "nn", M=T, N=D, K=F, res=h, scale=0.5, name=f"{tag}_out")
    return out, (h, hn, gate, up, act)


def _ffn_bwd(dout, saved, norm, w_in, w_out, tag):
    h, hn, gate, up, act = saved
    T, D = h.shape
    F = w_out.shape[0]
    dg, du = _mm_swiglu_bwd(dout, w_out, gate, up, f"{tag}_dact")
    dw_out = _mm(act, dout, "tn", M=F, N=D, K=T, scale=0.5, out_dtype=BF16, name=f"{tag}_dwout")
    dhn = _mm(dg, w_in, "nt", M=T, N=D, K=2 * F, a2=du, name=f"{tag}_dhn")
    dw_in = _mm(hn, dg, "tn", M=D, N=2 * F, K=T, b2=du, out_dtype=BF16, name=f"{tag}_dwin")
    dh, dnorm = _rmsnorm_bwd(h, norm, dhn, dout, f"{tag}_dnorm")
    return dh, dnorm, dw_in, dw_out


def _local_step(x, positions, target, W):
    B, S, D = x.shape
    T = B * S
    H, HP = D // HEAD_DIM, D // LANES
    G = {}
    h0 = x.reshape(T, D)
    cs = _rope_tables(positions, "rope_tables")

    h1, ffn00 = _ffn_fwd(h0, W["ffn_norm"][0, 0], W["ffn_w_in"][0, 0], W["ffn_w_out"][0, 0], "ffn00")
    hna = _rmsnorm_fwd(h1, W["mix_norm"][0], "mixa_norm")
    qkv = _mm(hna, W["a_w_qkv"], "nn", M=T, N=9 * D, K=D, name="a_qkv")
    qs, ks, os_, lses = [], [], [], []
    for g in range(3):
        qs.append(_qknorm_fwd(qkv, 3 * g * D, W["a_q_norm"][0, g], cs, D, F32, f"a_qnorm{g}", out_scale=SCALE))
        ks.append(_qknorm_fwd(qkv, (3 * g + 1) * D, W["a_k_norm"][0, g], cs, D, F32, f"a_knorm{g}"))
        o, l = _band_fwd(qs[g], ks[g], qkv, g, B, S, D, f"a_band{g}")
        os_.append(o)
        lses.append(l)
    mixed, lse_a = _mix(os_, lses, "a_mix")
    h2 = _mm(mixed, W["a_w_o"], "nn", M=T, N=D, K=D, res=h1, name="a_out")
    h3, ffn01 = _ffn_fwd(h2, W["ffn_norm"][0, 1], W["ffn_w_in"][0, 1], W["ffn_w_out"][0, 1], "ffn01")

    hnkv = _rmsnorm_fwd(h3, W["kv_norm"], "kv_norm")
    kv = _mm(hnkv, W["kv_w_kv"], "nn", M=T, N=2 * D, K=D, name="kv_proj")
    fpre = _mm(hnkv, W["kv_w_f"], "nn", M=T, N=LANES, K=D, name="kv_gate_proj")
    b_f = jnp.pad(W["kv_b_f"], (0, LANES - H)).reshape(1, LANES)
    kh = _qknorm_fwd(kv, 0, W["kv_k_norm"], None, D, BF16, "kv_knorm")
    cum, cq = _gate_fwd(fpre, b_f, B, S, D, "kv_gate")
    ck = cum.reshape(B, S, LANES)[:, :, :H].reshape(B, S, HP, 2).transpose(0, 2, 3, 1)
    ck = jnp.pad(ck, ((0, 0), (0, 0), (0, 6), (0, 0))).reshape(B * HP, 8, S)
    h4, ffn10 = _ffn_fwd(h3, W["ffn_norm"][1, 0], W["ffn_w_in"][1, 0], W["ffn_w_out"][1, 0], "ffn10")
    hnb = _rmsnorm_fwd(h4, W["mix_norm"][1], "mixb_norm")
    qraw = _mm(hnb, W["b_w_q"], "nn", M=T, N=D, K=D, name="b_q")
    qh = _qknorm_fwd(qraw, 0, W["b_q_norm"][0], None, D, BF16, "b_qnorm", out_scale=SCALE)
    first, last = _fox_live_blocks(cum, W["b_q_norm"][0], W["kv_k_norm"], B, S, D)
    ob, lse_b = _fox_fwd(qh, kh, kv, cq, ck, first, B, S, D, "b_fox")
    h5 = _mm(ob, W["b_w_o"], "nn", M=T, N=D, K=D, res=h4, name="b_out")
    h6, ffn11 = _ffn_fwd(h5, W["ffn_norm"][1, 1], W["ffn_w_in"][1, 1], W["ffn_w_out"][1, 1], "ffn11")

    sq, dy = _loss_and_grad(h6, target.reshape(T, D), "loss")

    dn = [[None, None], [None, None]]
    dwi = [[None, None], [None, None]]
    dwo = [[None, None], [None, None]]
    dh5, dn[1][1], dwi[1][1], dwo[1][1] = _ffn_bwd(dy, ffn11, W["ffn_norm"][1, 1], W["ffn_w_in"][1, 1], W["ffn_w_out"][1, 1], "ffn11")
    dob = _mm(dh5, W["b_w_o"], "nt", M=T, N=D, K=D, name="b_dout")
    G["b_w_o"] = _mm(ob, dh5, "tn", M=D, N=D, K=T, out_dtype=BF16, name="b_dwo")[None]
    delta_b = _head_rowsum(dob, ob, "b_delta")
    dqh, dcq = _fox_bwd_dq(qh, kh, kv, cq, ck, first, dob, lse_b, delta_b, B, S, D, "b_fox_dq")
    dkh, dv_b, dck = _fox_bwd_dkv(qh, kh, kv, cq, ck, last, dob, lse_b, delta_b, B, S, D, "b_fox_dkv")
    dqraw, dg = _qknorm_bwd(qraw, 0, W["b_q_norm"][0], None, dqh, D, "b_dqnorm", out_scale=SCALE)
    G["b_q_norm"] = dg[None]
    dhnb = _mm(dqraw, W["b_w_q"], "nt", M=T, N=D, K=D, name="b_dhn")
    G["b_w_q"] = _mm(hnb, dqraw, "tn", M=D, N=D, K=T, out_dtype=BF16, name="b_dwq")[None]
    dh4, dmix_b = _rmsnorm_bwd(h4, W["mix_norm"][1], dhnb, dh5, "mixb_dnorm")
    dh3, dn[1][0], dwi[1][0], dwo[1][0] = _ffn_bwd(dh4, ffn10, W["ffn_norm"][1, 0], W["ffn_w_in"][1, 0], W["ffn_w_out"][1, 0], "ffn10")

    dkraw, G["kv_k_norm"] = _qknorm_bwd(kv, 0, W["kv_k_norm"], None, dkh, D, "kv_dknorm")
    dcum = dck.reshape(B, HP, 8, S)[:, :, :2].transpose(0, 3, 1, 2).reshape(T, H)
    dcum = jnp.pad(dcum, ((0, 0), (0, LANES - H)))
    dcum_q = jnp.pad(dcq.reshape(T, H, HEAD_DIM)[:, :, 0], ((0, 0), (0, LANES - H)))
    dfpre, db_f = _gate_bwd(dcum, dcum_q, fpre, b_f, B, S, "kv_dgate")
    G["kv_b_f"] = db_f[0, :H]
    dhnkv = _mm(dkraw, W["kv_w_kv"], "nt", M=T, N=D, K=D, name="kv_dhn_k")
    dhnkv = _mm(dv_b, W["kv_w_kv"], "nt", M=T, N=D, K=D, b_off=(0, D), res=dhnkv, name="kv_dhn_v")
    dhnkv = _mm(dfpre, W["kv_w_f"], "nt", M=T, N=D, K=LANES, res=dhnkv, name="kv_dhn_f")
    dw_k = _mm(hnkv, dkraw, "tn", M=D, N=D, K=T, out_dtype=BF16, name="kv_dwk")
    dw_v = _mm(hnkv, dv_b, "tn", M=D, N=D, K=T, out_dtype=BF16, name="kv_dwv")
    dw_f = _mm(hnkv, dfpre, "tn", M=D, N=LANES, K=T, out_dtype=BF16, name="kv_dwf")
    G["kv_w"] = jnp.concatenate([dw_k, dw_v, dw_f[:, :H]], axis=1)
    dh3, G["kv_norm"] = _rmsnorm_bwd(h3, W["kv_norm"], dhnkv, dh3, "kv_dnorm")

    dh2, dn[0][1], dwi[0][1], dwo[0][1] = _ffn_bwd(dh3, ffn01, W["ffn_norm"][0, 1], W["ffn_w_in"][0, 1], W["ffn_w_out"][0, 1], "ffn01")
    dmixed = _mm(dh2, W["a_w_o"], "nt", M=T, N=D, K=D, name="a_dmixed")
    G["a_w_o"] = _mm(mixed, dh2, "tn", M=D, N=D, K=T, out_dtype=BF16, name="a_dwo")[None]
    delta_a = _head_rowsum(dmixed, mixed, "a_delta")
    dhna = None
    dw_qkv, dqn, dkn = [], [], []
    for g in range(3):
        dq, dk, dv = _band_bwd(qs[g], ks[g], qkv, dmixed, lse_a, delta_a, g, B, S, D, f"a_dband{g}")
        dqr, dgq = _qknorm_bwd(qkv, 3 * g * D, W["a_q_norm"][0, g], cs, dq, D, f"a_dqnorm{g}", out_scale=SCALE)
        dkr, dgk = _qknorm_bwd(qkv, (3 * g + 1) * D, W["a_k_norm"][0, g], cs, dk, D, f"a_dknorm{g}")
        dqn.append(dgq)
        dkn.append(dgk)
        for j, d in enumerate((dqr, dkr, dv)):
            c = 3 * g + j
            dhna = _mm(d, W["a_w_qkv"], "nt", M=T, N=D, K=D, b_off=(0, c * D), res=dhna, name=f"a_dhn{c}")
            dw_qkv.append(_mm(hna, d, "tn", M=D, N=D, K=T, out_dtype=BF16, name=f"a_dwqkv{c}"))
    G["a_w_qkv"] = jnp.concatenate(dw_qkv, axis=1)[None]
    G["a_q_norm"] = jnp.stack(dqn)[None]
    G["a_k_norm"] = jnp.stack(dkn)[None]
    dh1, dmix_a = _rmsnorm_bwd(h1, W["mix_norm"][0], dhna, dh2, "mixa_dnorm")
    dh0, dn[0][0], dwi[0][0], dwo[0][0] = _ffn_bwd(dh1, ffn00, W["ffn_norm"][0, 0], W["ffn_w_in"][0, 0], W["ffn_w_out"][0, 0], "ffn00")

    G["mix_norm"] = jnp.stack([dmix_a, dmix_b])
    G["ffn_norm"] = jnp.stack([jnp.stack(r) for r in dn])
    G["ffn_w_in"] = jnp.stack([jnp.stack(r) for r in dwi])
    G["ffn_w_out"] = jnp.stack([jnp.stack(r) for r in dwo])
    return jnp.sum(sq), dh0.reshape(B, S, D), G


SMALL =["mix_norm", "kv_norm", "a_q_norm", "a_k_norm", "kv_b_f", "kv_k_norm", "b_q_norm"]


def _pack(blocks, lead):
    flat = [b.reshape(b.shape[:lead] + (-1,)) for b in blocks]
    flat = jnp.concatenate(flat, axis=lead)
    n = flat.shape[-1]
    unit = PACK_ROWS * LANES
    padded = -(-n // unit) * unit if n > unit else -(-n // (16 * LANES)) * 16 * LANES
    flat = jnp.pad(flat, [(0, 0)] * lead + [(0, padded - n)])
    return flat.reshape(flat.shape[:lead] + (padded // LANES, LANES))


def _unpack(packed, shapes, lead):
    flat = packed.reshape(packed.shape[:lead] + (-1,))
    out, off = [], 0
    for shp in shapes:
        n = int(np.prod(shp))
        out.append(flat[..., off:off + n].reshape(packed.shape[:lead] + tuple(shp)))
        off += n
    return out


def kernel(x, positions, ffn_norm, ffn_w_in, ffn_w_out, mix_norm, a_w_qkv, a_q_norm, a_k_norm, a_w_o, kv_norm, kv_w, kv_b_f, kv_k_norm, b_w_q, b_q_norm, b_w_o, loss_target, m_ffn_norm, m_ffn_w_in, m_ffn_w_out, m_mix_norm, m_a_w_qkv, m_a_q_norm, m_a_k_norm, m_a_w_o, m_kv_norm, m_kv_w, m_kv_b_f, m_kv_k_norm, m_b_w_q, m_b_q_norm, m_b_w_o, v_ffn_norm, v_ffn_w_in, v_ffn_w_out, v_mix_norm, v_a_w_qkv, v_a_q_norm, v_a_k_norm, v_a_w_o, v_kv_norm, v_kv_w, v_kv_b_f, v_kv_k_norm, v_b_w_q, v_b_q_norm, v_b_w_o):
    names = ["ffn_norm", "ffn_w_in", "ffn_w_out", "mix_norm", "a_w_qkv", "a_q_norm", "a_k_norm", "a_w_o", "kv_norm",
             "kv_w", "kv_b_f", "kv_k_norm", "b_w_q", "b_q_norm", "b_w_o"]
    w = dict(zip(names, (ffn_norm, ffn_w_in, ffn_w_out, mix_norm, a_w_qkv, a_q_norm, a_k_norm, a_w_o, kv_norm, kv_w,
                         kv_b_f, kv_k_norm, b_w_q, b_q_norm, b_w_o)))
    m = dict(zip(names, (m_ffn_norm, m_ffn_w_in, m_ffn_w_out, m_mix_norm, m_a_w_qkv, m_a_q_norm, m_a_k_norm, m_a_w_o,
                         m_kv_norm, m_kv_w, m_kv_b_f, m_kv_k_norm, m_b_w_q, m_b_q_norm, m_b_w_o)))
    v = dict(zip(names, (v_ffn_norm, v_ffn_w_in, v_ffn_w_out, v_mix_norm, v_a_w_qkv, v_a_q_norm, v_a_k_norm, v_a_w_o,
                         v_kv_norm, v_kv_w, v_kv_b_f, v_kv_k_norm, v_b_w_q, v_b_q_norm, v_b_w_o)))
    B, S, D = x.shape
    H = D // HEAD_DIM
    F = ffn_w_out.shape[2] * N_DEV
    Ds, Fs = D // N_DEV, F // N_DEV
    me = 4 * lax.axis_index("x") + 2 * lax.axis_index("y") + lax.axis_index("c")
    squares = ("a_w_o", "b_w_q", "b_w_o")

    def shard_views(t):
        return [t["ffn_w_in"].reshape(4 * D, -1), t["ffn_w_out"].reshape(-1, D), t["a_w_qkv"][0], t["kv_w"],
                jnp.concatenate([t[n][0] for n in squares], axis=0)]

    def to_shards(views):
        sq3 = views[4].reshape(3, 1, Ds, D)
        out = {"ffn_w_in": views[0].reshape(ffn_w_in.shape), "ffn_w_out": views[1].reshape(ffn_w_out.shape),
               "a_w_qkv": views[2][None], "kv_w": views[3]}
        out.update({n: sq3[i] for i, n in enumerate(squares)})
        return out

    norm_rows = jnp.pad(ffn_norm.reshape(4, Ds), ((0, 4), (0, 0)))
    g_in, g_out, g_qkv, g_kv, g_sq, g_norm = _all_gather(
        [s.astype(BF16) for s in shard_views(w)] + [norm_rows], "gather_weights")
    W = {n: w[n] for n in SMALL}
    W["ffn_norm"] = g_norm[:, :4].transpose(1, 0, 2).reshape(2, 2, D)
    W["ffn_w_in"] = g_in.transpose(1, 0, 2).reshape(2, 2, D, 2 * F)
    W["ffn_w_out"] = g_out.reshape(N_DEV, 4, Fs, D).transpose(1, 0, 2, 3).reshape(2, 2, F, D)
    W["a_w_qkv"] = g_qkv.transpose(1, 0, 2).reshape(D, 9 * D)
    kv_full = g_kv.transpose(1, 0, 2).reshape(D, 2 * D + H)
    W["kv_w_kv"] = kv_full[:, :2 * D]
    W["kv_w_f"] = jnp.pad(kv_full[:, 2 * D:], ((0, 0), (0, LANES - H)))
    sq_full = g_sq.reshape(N_DEV, 3, Ds, D).transpose(1, 0, 2, 3).reshape(3, D, D)
    W["a_w_o"], W["b_w_q"], W["b_w_o"] = sq_full[0], sq_full[1], sq_full[2]

    sq, grad_x, G = _local_step(x, positions, loss_target, W)
    loss = lax.psum(sq * (0.5 / D), ("x", "y", "c"))

    blocks = [G["ffn_w_in"].reshape(4 * D, N_DEV, -1).transpose(1, 0, 2),
              G["ffn_w_out"].reshape(4, N_DEV, Fs, D).transpose(1, 0, 2, 3).reshape(N_DEV, 4 * Fs, D),
              G["a_w_qkv"][0].reshape(D, N_DEV, -1).transpose(1, 0, 2),
              G["kv_w"].reshape(D, N_DEV, -1).transpose(1, 0, 2),
              jnp.concatenate([G[n][0].reshape(N_DEV, Ds, D) for n in squares], axis=1)]
    blocks = [b.astype(BF16) for b in blocks]
    got = _pair_exchange(blocks, "exchange_pair")
    parts = _chip_exchange([_pair_sum(b, s, f"pair_sum{i}") for i, (b, s) in enumerate(zip(blocks, got))], "exchange_chips")
    views = [shard_views(t) for t in (w, m, v)]
    outs = [_adamw(parts[i], views[0][i], views[1][i], views[2][i], f"adamw_matrix{i}") for i in range(len(parts))]
    new = {}
    for k, kind in enumerate(("grad", "delta", "new_m", "new_v")):
        for n, arr in to_shards([o[k] for o in outs]).items():
            new[kind, n] = arr

    small_all = ["ffn_norm"] + SMALL
    vec_parts = _all_gather([_pack([G[n] for n in small_all], 0)], "gather_vector_grads")[0]
    vec_parts = _unpack(vec_parts, [G[n].shape for n in small_all], 1)
    norm_g = lax.dynamic_slice_in_dim(vec_parts[0], me * ffn_norm.shape[2], ffn_norm.shape[2], axis=3)
    rep_g = _pack(vec_parts[1:], 1)
    for ns, parts_s in ((["ffn_norm"], _pack([norm_g], 1)), (SMALL, rep_g)):
        outs = _adamw(parts_s, *(_pack([t[n] for n in ns], 0) for t in (w, m, v)), "adamw_" + ns[0])
        for kind, packed in zip(("grad", "delta", "new_m", "new_v"), outs):
            for n, arr in zip(ns, _unpack(packed, [w[n].shape for n in ns], 0)):
                new[kind, n] = arr

    result = [loss, grad_x]
    for kind in ("grad", "delta", "new_m", "new_v"):
        result += [new[kind, n] for n in names]
    return tuple(result)
```

```python
import jax
import jax.numpy as jnp
import numpy as np
from jax import lax
from jax.experimental import pallas as pl
from jax.experimental.pallas import tpu as pltpu

F32 = jnp.float32
BF16 = jnp.bfloat16
HIGHEST = lax.Precision.HIGHEST

HEAD_DIM = 64
ROT_DIM = HEAD_DIM // 4
ROPE_THETA = 500000.0
DILATED_GROUPS = ((128, 1), (512, 4), (2048, 16))
BAND_BLOCK = 128
EPS = 1e-6
ADAM_LR, ADAM_B1, ADAM_B2, ADAM_EPS, ADAM_WD, ADAM_STEP = 0.001, 0.9, 0.999, 1e-08, 0.01, 10

LANES = 128
N_DEV = 8
VMEM_LIMIT = 56 * 1024 * 1024
NEG = -1e30
SCALE = HEAD_DIM ** -0.5
PACK_ROWS = 1024
MESH_ID = pl.DeviceIdType.MESH


def _params(*sem):
    return pltpu.CompilerParams(dimension_semantics=sem, vmem_limit_bytes=VMEM_LIMIT)


def _divs(n, cap, mult=LANES):
    d = [t for t in range(mult, min(n, cap) + 1, mult) if n % t == 0]
    return d or [n]


def _pick(n, cap, mult=LANES):
    return _divs(n, cap, mult)[-1]


def _mm_tiles(M, N, K, budget=1408 * 1024):
    best = None
    for tm in _divs(M, 1408):
        for tn in _divs(N, 1408):
            if tm * tn <= budget and (best is None or tm * tn > best[0] * best[1]):
                best = (tm, tn)
    if best is None:
        best = (_divs(M, 1408)[0], _divs(N, 1408)[0])
    tm, tn = best
    tks = [t for t in _divs(K, 2048) if (tm + tn) * t <= 3 * 1024 * 1024]
    tk = tks[-1] if tks else _divs(K, 2048)[0]
    return tm, tn, tk


def _dot(a, b, dims, precision=None):
    return lax.dot_general(a, b, (dims, ((), ())), precision=precision, preferred_element_type=F32)


NN = ((1,), (0,))
NT = ((1,), (1,))
TN = ((0,), (0,))


def _dot_split(x, mat):
    hi = x.astype(BF16)
    lo = (x - hi.astype(F32)).astype(BF16)
    m = mat.astype(BF16)
    return _dot(hi, m, NN) + _dot(lo, m, NN)


def _dot_once(x, mat):
    return _dot(x.astype(BF16), mat.astype(BF16), NN)


def _mm(a, b, mode, *, M, N, K, a_off=(0, 0), b_off=(0, 0), a_rest=(), b2=None, res=None, scale=1.0, out_dtype=F32,
        name):
    a_parts = 1 + len(a_rest)
    tm, tn, tk = _mm_tiles(M, N // 2 if b2 is not None else N, K // a_parts)
    nk = K // tk
    k_part, n_half = nk // a_parts, (N // tn) // 2
    if mode == "tn":
        a_blk, b_blk, dims = (tk, tm), (tk, tn), TN
    elif mode == "nt":
        a_blk, b_blk, dims = (tm, tk), (tn, tk), NT
    else:
        a_blk, b_blk, dims = (tm, tk), (tk, tn), NN
    for off, blk in ((a_off, a_blk), (b_off, b_blk)):
        assert off[0] % blk[0] == 0 and off[1] % blk[1] == 0, (name, off, blk)
    ao = (a_off[0] // a_blk[0], a_off[1] // a_blk[1])
    bo = (b_off[0] // b_blk[0], b_off[1] // b_blk[1])
    if mode == "tn":
        a_map = lambda i, j, k: (k + ao[0], i + ao[1])
        b_map = lambda i, j, k: (k + bo[0], j + bo[1])
    elif mode == "nt":
        a_map = lambda i, j, k: (i + ao[0], k + ao[1])
        b_map = lambda i, j, k: (j + bo[0], k + bo[1])
    else:
        a_map = lambda i, j, k: (i + ao[0], k + ao[1])
        b_map = lambda i, j, k: (k + bo[0], j + bo[1])
    has_res = res is not None
    if a_rest:
        assert mode != "tn" and a_off == (0, 0)
        part_map = lambda p: (lambda i, j, k: (i, jnp.clip(k - p * k_part, 0, k_part - 1)))
        a_map = part_map(0)
    if b2 is not None:
        assert mode != "nt" and b_off == (0, 0)
        b_map = lambda i, j, k: (k, jnp.minimum(j, n_half - 1))
        b2_map = lambda i, j, k: (k, jnp.maximum(j - n_half, 0))

    def body(*refs):
        refs = list(refs)
        a_ref, b_ref = refs.pop(0), refs.pop(0)
        a_val, b_val = a_ref[...].astype(BF16), b_ref[...].astype(BF16)
        for p in range(1, a_parts):
            a_val = jnp.where(pl.program_id(2) >= p * k_part, refs.pop(0)[...].astype(BF16), a_val)
        if b2 is not None:
            b_val = jnp.where(pl.program_id(1) < n_half, b_val, refs.pop(0)[...].astype(BF16))
        r_ref = refs.pop(0) if has_res else None
        o_ref = refs.pop(0)

        def finish(total):
            out = total * scale
            if has_res:
                out = r_ref[...] + out
            o_ref[...] = out.astype(out_dtype)

        part = _dot(a_val, b_val, dims)
        if nk == 1:
            finish(part)
            return
        acc = refs[-1]
        k = pl.program_id(2)

        @pl.when(k == 0)
        def _():
            acc[...] = part

        @pl.when(k > 0)
        def _():
            acc[...] += part

        pl.when(k == nk - 1)(lambda: finish(acc[...]))

    in_specs = [pl.BlockSpec(a_blk, a_map), pl.BlockSpec(b_blk, b_map)]
    args = [a, b]
    for p, extra in enumerate(a_rest, 1):
        in_specs.append(pl.BlockSpec(a_blk, part_map(p)))
        args.append(extra)
    if b2 is not None:
        in_specs.append(pl.BlockSpec(b_blk, b2_map))
        args.append(b2)
    if has_res:
        in_specs.append(pl.BlockSpec((tm, tn), lambda i, j, k: (i, j)))
        args.append(res)
    return pl.pallas_call(
        body, name=name, grid=(M // tm, N // tn, nk), in_specs=in_specs,
        out_specs=pl.BlockSpec((tm, tn), lambda i, j, k: (i, j)),
        out_shape=jax.ShapeDtypeStruct((M, N), out_dtype),
        scratch_shapes=[pltpu.VMEM((tm, tn), F32)] if nk > 1 else [],
        compiler_params=_params("parallel", "parallel", "arbitrary"),
    )(*args)


def _rmsnorm_fwd(x, g, name):
    T, D = x.shape
    tm = _pick(T, 256, 8)

    def body(x_ref, g_ref, y_ref):
        xv = x_ref[...]
        r = lax.rsqrt(jnp.mean(xv * xv, axis=-1, keepdims=True) + EPS)
        y_ref[...] = (xv * r * g_ref[...]).astype(BF16)

    return pl.pallas_call(
        body, name=name, grid=(T // tm,),
        in_specs=[pl.BlockSpec((tm, D), lambda i: (i, 0)), pl.BlockSpec((1, D), lambda i: (0, 0))],
        out_specs=pl.BlockSpec((tm, D), lambda i: (i, 0)),
        out_shape=jax.ShapeDtypeStruct((T, D), BF16), compiler_params=_params("parallel"),
    )(x, g.reshape(1, D))


def _rmsnorm_bwd(x, g, dy, dres, name):
    T, D = x.shape
    tm = _pick(T, 256, 8)

    def body(x_ref, g_ref, dy_ref, dres_ref, dx_ref, dg_ref):
        xv = x_ref[...]
        r = lax.rsqrt(jnp.mean(xv * xv, axis=-1, keepdims=True) + EPS)
        xn = xv * r
        d = dy_ref[...]
        dgy = d * g_ref[...]
        dx_ref[...] = dres_ref[...] + r * (dgy - xn * jnp.mean(dgy * xn, axis=-1, keepdims=True))

        @pl.when(pl.program_id(0) == 0)
        def _():
            dg_ref[...] = jnp.zeros_like(dg_ref)

        dg_ref[...] += jnp.sum(d * xn, axis=0, keepdims=True)

    row = pl.BlockSpec((tm, D), lambda i: (i, 0))
    vec = pl.BlockSpec((1, D), lambda i: (0, 0))
    dx, dg = pl.pallas_call(
        body, name=name, grid=(T // tm,), in_specs=[row, vec, row, row], out_specs=[row, vec],
        out_shape=[jax.ShapeDtypeStruct((T, D), F32), jax.ShapeDtypeStruct((1, D), F32)],
        compiler_params=_params("arbitrary"),
    )(x, g.reshape(1, D), dy, dres)
    return dx, dg.reshape(D)


def _sigmoid(x):
    return 1.0 / (1.0 + jnp.exp(-x))


MXU_WIDTH = 256


def _col_chunks(n):
    return [(lo, min(lo + MXU_WIDTH, n)) for lo in range(0, n, MXU_WIDTH)]


def _mm_swiglu_fwd(hn, w_in, name):
    T, D = hn.shape
    F = w_in.shape[1] // 2
    tm, tn, _ = _mm_tiles(T, F, D)
    assert (tm + 2 * tn) * D <= 4 * 1024 * 1024, "the whole contraction is held in one step"
    nf = F // tn

    def body(a_ref, bg_ref, bu_ref, g_ref, u_ref, act_ref):
        a = a_ref[...].astype(BF16)
        for lo, hi in _col_chunks(tn):
            gv = _dot(a, bg_ref[:, lo:hi].astype(BF16), NN)
            uv = _dot(a, bu_ref[:, lo:hi].astype(BF16), NN)
            g_ref[:, lo:hi] = gv.astype(BF16)
            u_ref[:, lo:hi] = uv.astype(BF16)
            act_ref[:, lo:hi] = (gv * _sigmoid(gv) * uv).astype(BF16)

    tile = pl.BlockSpec((tm, tn), lambda j, i: (i, j))
    out = jax.ShapeDtypeStruct((T, F), BF16)
    return pl.pallas_call(
        body, name=name, grid=(nf, T // tm),
        in_specs=[pl.BlockSpec((tm, D), lambda j, i: (i, 0)), pl.BlockSpec((D, tn), lambda j, i: (0, j)),
                  pl.BlockSpec((D, tn), lambda j, i: (0, j + nf))],
        out_specs=[tile, tile, tile], out_shape=[out, out, out],
        compiler_params=_params("parallel", "parallel"),
    )(hn, w_in, w_in)


def _mm_swiglu_bwd(dout, w_out, gate, up, name):
    T, D = dout.shape
    F = w_out.shape[0]
    tm, tn, _ = _mm_tiles(T, F, D)
    assert (tm + tn) * D <= 4 * 1024 * 1024, "the whole contraction is held in one step"

    def body(a_ref, b_ref, g_ref, u_ref, dg_ref, du_ref):
        a = a_ref[...].astype(BF16)
        for lo, hi in _col_chunks(tn):
            d = _dot(a, b_ref[lo:hi, :].astype(BF16), NT) * 0.5
            gv, uv = g_ref[:, lo:hi].astype(F32), u_ref[:, lo:hi].astype(F32)
            s = _sigmoid(gv)
            gs = gv * s
            dg_ref[:, lo:hi] = (d * uv * (s + gs - gs * s)).astype(BF16)
            du_ref[:, lo:hi] = (d * gs).astype(BF16)

    tile = pl.BlockSpec((tm, tn), lambda j, i: (i, j))
    out = jax.ShapeDtypeStruct((T, F), BF16)
    return pl.pallas_call(
        body, name=name, grid=(F // tn, T // tm),
        in_specs=[pl.BlockSpec((tm, D), lambda j, i: (i, 0)), pl.BlockSpec((tn, D), lambda j, i: (j, 0)), tile, tile],
        out_specs=[tile, tile], out_shape=[out, out], compiler_params=_params("parallel", "parallel"),
    )(dout, w_out, gate, up)


def _iota2(shape, dim):
    return lax.broadcasted_iota(jnp.int32, shape, dim)


def _head_ones():
    r, c = _iota2((LANES, LANES), 0), _iota2((LANES, LANES), 1)
    return ((r >> 6) == (c >> 6)).astype(F32)


def _rot_matrix(transpose):
    r, c = _iota2((LANES, LANES), 0), _iota2((LANES, LANES), 1)
    if transpose:
        r, c = c, r
    half = ROT_DIM // 2
    cm = c & (HEAD_DIM - 1)
    neg = (r == c + half) & (cm < half)
    pos = (r == c - half) & (cm >= half) & (cm < ROT_DIM)
    return pos.astype(F32) - neg.astype(F32)


def _rope_tables(positions, name):
    T = positions.size
    tm = _pick(T, 1024, 8)
    inv = ROPE_THETA ** (-jnp.arange(0, ROT_DIM, 2, dtype=F32) / ROT_DIM)
    lane = jnp.concatenate([inv, inv, jnp.zeros((HEAD_DIM - ROT_DIM,), F32)])
    lane = jnp.tile(lane, 2).reshape(1, LANES)

    def body(p_ref, f_ref, c_ref, s_ref):
        ang = p_ref[...].astype(F32) * f_ref[...]
        c_ref[...] = jnp.cos(ang)
        s_ref[...] = jnp.sin(ang)

    out = jax.ShapeDtypeStruct((T, LANES), F32)
    blk = pl.BlockSpec((tm, LANES), lambda i: (i, 0))
    return pl.pallas_call(
        body, name=name, grid=(T // tm,),
        in_specs=[pl.BlockSpec((tm, 1), lambda i: (i, 0)), pl.BlockSpec((1, LANES), lambda i: (0, 0))],
        out_specs=[blk, blk], out_shape=[out, out], compiler_params=_params("parallel"),
    )(positions.reshape(T, 1), lane)


def _qknorm_fwd(src, col_off, gain, cs, D, out_dtype, name, out_scale=1.0):
    T = src.shape[0]
    HP = D // LANES
    tm = _pick(T, 1024, 8)
    co = col_off // LANES
    rope = cs is not None

    def body(*refs):
        if rope:
            x_ref, g_ref, c_ref, s_ref, o_ref = refs
        else:
            x_ref, g_ref, o_ref = refs
        xv = x_ref[...]
        ms = _dot_once(xv * xv, _head_ones()) * (1.0 / HEAD_DIM)
        y = xv * lax.rsqrt(ms + EPS) * g_ref[...]
        if rope:
            y = y * c_ref[...] + _dot_once(y, _rot_matrix(False)) * s_ref[...]
        o_ref[...] = (y * out_scale).astype(out_dtype)

    in_specs = [pl.BlockSpec((tm, LANES), lambda i, h: (i, co + h)), pl.BlockSpec((1, LANES), lambda i, h: (0, 0))]
    args = [src, jnp.tile(gain.reshape(1, HEAD_DIM), (1, 2))]
    if rope:
        in_specs += [pl.BlockSpec((tm, LANES), lambda i, h: (i, 0))] * 2
        args += list(cs)
    return pl.pallas_call(
        body, name=name, grid=(T // tm, HP), in_specs=in_specs,
        out_specs=pl.BlockSpec((tm, LANES), lambda i, h: (i, h)),
        out_shape=jax.ShapeDtypeStruct((T, D), out_dtype), compiler_params=_params("parallel", "parallel"),
    )(*args)


def _qknorm_bwd(src, col_off, gain, cs, dout, D, name, out_scale=1.0):
    T = src.shape[0]
    HP = D // LANES
    tm = _pick(T, 1024, 8)
    co = col_off // LANES
    rope = cs is not None

    def body(*refs):
        if rope:
            x_ref, g_ref, d_ref, c_ref, s_ref, dx_ref, dg_ref = refs
        else:
            x_ref, g_ref, d_ref, dx_ref, dg_ref = refs
        xv = x_ref[...]
        ones = _head_ones()
        ms = _dot_once(xv * xv, ones) * (1.0 / HEAD_DIM)
        r = lax.rsqrt(ms + EPS)
        xn = xv * r
        d = d_ref[...] * out_scale
        if rope:
            d = d * c_ref[...] + _dot_once(d * s_ref[...], _rot_matrix(True))
        dgy = d * g_ref[...]
        mean = _dot_once(dgy * xn, ones) * (1.0 / HEAD_DIM)
        dx_ref[...] = (r * (dgy - xn * mean)).astype(BF16)

        @pl.when(pl.program_id(1) == 0)
        def _():
            dg_ref[...] = jnp.zeros_like(dg_ref)

        dg_ref[...] += jnp.sum(d * xn, axis=0, keepdims=True)[None]

    in_specs = [pl.BlockSpec((tm, LANES), lambda h, i: (i, co + h)), pl.BlockSpec((1, LANES), lambda h, i: (0, 0)),
                pl.BlockSpec((tm, LANES), lambda h, i: (i, h))]
    args = [src, jnp.tile(gain.reshape(1, HEAD_DIM), (1, 2)), dout]
    if rope:
        in_specs += [pl.BlockSpec((tm, LANES), lambda h, i: (i, 0))] * 2
        args += list(cs)
    dx, dg = pl.pallas_call(
        body, name=name, grid=(HP, T // tm), in_specs=in_specs,
        out_specs=[pl.BlockSpec((tm, LANES), lambda h, i: (i, h)), pl.BlockSpec((1, 1, LANES), lambda h, i: (h, 0, 0))],
        out_shape=[jax.ShapeDtypeStruct((T, D), BF16), jax.ShapeDtypeStruct((HP, 1, LANES), F32)],
        compiler_params=_params("parallel", "arbitrary"),
    )(*args)
    return dx, dg.reshape(2 * HP, HEAD_DIM).sum(axis=0)


def _head_rowsum(a, b, name):
    T, D = a.shape
    tm = _pick(T, 1024, 8)

    def body(a_ref, b_ref, o_ref):
        o_ref[...] = _dot_split(a_ref[...] * b_ref[...], _head_ones())

    blk = pl.BlockSpec((tm, LANES), lambda i, h: (i, h))
    return pl.pallas_call(
        body, name=name, grid=(T // tm, D // LANES), in_specs=[blk, blk], out_specs=blk,
        out_shape=jax.ShapeDtypeStruct((T, D), F32), compiler_params=_params("parallel", "parallel"),
    )(a, b)


def _lane_masks():
    lane = _iota2((1, LANES), 1)
    return (lane < HEAD_DIM, lane >= HEAD_DIM)


BAND_GROUP = 4


def _band_geometry(g, S):
    window, dil = DILATED_GROUPS[g]
    chunk = BAND_BLOCK * dil
    assert window // dil == BAND_BLOCK and S % chunk == 0
    if dil == 1:
        assert (S // chunk) % BAND_GROUP == 0
        return dil, chunk, BAND_GROUP, 1
    assert dil % BAND_GROUP == 0
    return dil, chunk, 1, dil // BAND_GROUP


def _band_rows(i, rho, dil, shift=0):
    if dil > 1:
        return pl.ds(rho * BAND_GROUP + i, BAND_BLOCK, stride=dil), shift != 0
    j = i + shift
    outside = j < 0 or j >= BAND_GROUP
    return pl.ds(0 if outside else j * BAND_BLOCK, BAND_BLOCK), outside


def _stack_heads(x):
    hm0, hm1 = _lane_masks()
    zero = jnp.zeros_like(x)
    return jnp.concatenate([jnp.where(hm0, x, zero), jnp.where(hm1, x, zero)], axis=0)


def _unstack_heads(x2):
    hm0, _ = _lane_masks()
    return jnp.where(hm0, x2[:BAND_BLOCK], x2[BAND_BLOCK:])


def _band_masks():
    qi = _iota2((2 * BAND_BLOCK, 2 * BAND_BLOCK), 0) & (BAND_BLOCK - 1)
    kj = _iota2((2 * BAND_BLOCK, 2 * BAND_BLOCK), 1)
    own = (kj >= BAND_BLOCK) & (kj - BAND_BLOCK <= qi)
    prev = (kj < BAND_BLOCK) & (kj >= qi)
    return own, prev


def _band_fwd(qh, kh, qkv, g, B, S, D, name):
    dil, CH, NB, RG = _band_geometry(g, S)
    HP = D // LANES
    vo = (3 * g + 2) * HP
    nsp = S // (CH * NB)

    def body(q_ref, k_ref, kp_ref, v_ref, vp_ref, o_ref, lse_ref):
        n, rho = pl.program_id(1), pl.program_id(3)
        own, prev = _band_masks()
        for i in range(BAND_GROUP):
            rows, _ = _band_rows(i, rho, dil)
            prows, outside = _band_rows(i, rho, dil, shift=-1)
            q, kc, vc = (t[0, rows, :].astype(BF16) for t in (q_ref, k_ref, v_ref))
            kp, vp = (t[0, prows, :].astype(BF16) for t in ((kp_ref, vp_ref) if outside else (k_ref, v_ref)))
            valid = (own | (prev & (n > 0))) if outside else (own | prev)
            s = jnp.where(valid, _dot(_stack_heads(q), jnp.concatenate([kp, kc], axis=0), NT), NEG)
            m = jnp.max(s, axis=-1, keepdims=True)
            e = jnp.exp(s - m)
            l = jnp.sum(e, axis=-1, keepdims=True)
            o2 = _dot((e * (1.0 / l)).astype(BF16), jnp.concatenate([vp, vc], axis=0), NN)
            o_ref[0, rows, :] = _unstack_heads(o2)
            lse_ref[0, rows, :] = _unstack_heads(m + jnp.log(l))

    span, chunk = (1, CH * NB, LANES), (1, CH, LANES)
    cur = lambda b, n, h, r: (b, n, h)
    prev = lambda b, n, h, r: (b, jnp.maximum(n * NB - 1, 0), h)
    vcur = lambda b, n, h, r: (b, n, vo + h)
    vprev = lambda b, n, h, r: (b, jnp.maximum(n * NB - 1, 0), vo + h)
    out = jax.ShapeDtypeStruct((B, S, D), F32)
    q3, k3, v3 = qh.reshape(B, S, D), kh.reshape(B, S, D), qkv.reshape(B, S, 9 * D)
    o, lse = pl.pallas_call(
        body, name=name, grid=(B, nsp, HP, RG),
        in_specs=[pl.BlockSpec(span, cur), pl.BlockSpec(span, cur), pl.BlockSpec(chunk, prev),
                  pl.BlockSpec(span, vcur), pl.BlockSpec(chunk, vprev)],
        out_specs=[pl.BlockSpec(span, cur), pl.BlockSpec(span, cur)], out_shape=[out, out],
        compiler_params=_params("parallel", "parallel", "parallel", "arbitrary"),
    )(q3, k3, k3, v3, v3)
    return o.reshape(B * S, D), lse.reshape(B * S, D)


def _head_col(x, h):
    lane = _iota2((1, LANES), 1)
    return jnp.sum(jnp.where(lane == h * HEAD_DIM, x, 0.0), axis=-1, keepdims=True)


def _band_bwd(qh, kh, qkv, do, lse, delta, g, B, S, D, name):
    dil, CH, NB, RG = _band_geometry(g, S)
    HP = D // LANES
    vo = (3 * g + 2) * HP
    nsp, nch = S // (CH * NB), S // CH

    def stacked_cols(tile):
        return jnp.concatenate([_head_col(tile, 0), _head_col(tile, 1)], axis=0)

    def body(q_ref, qn_ref, do_ref, don_ref, l_ref, ln_ref, d_ref, dn_ref, k_ref, kp_ref, v_ref, vp_ref,
             dq_ref, dk_ref, dv_ref):
        n, rho = pl.program_id(1), pl.program_id(3)
        own, prev = _band_masks()
        band = (_iota2((2 * BAND_BLOCK, BAND_BLOCK), 1) >= (_iota2((2 * BAND_BLOCK, BAND_BLOCK), 0) & (BAND_BLOCK - 1)))
        for i in range(BAND_GROUP):
            rows, _ = _band_rows(i, rho, dil)
            prows, p_out = _band_rows(i, rho, dil, shift=-1)
            nrows, n_out = _band_rows(i, rho, dil, shift=1)
            kc, vc = (t[0, rows, :].astype(BF16) for t in (k_ref, v_ref))
            kp, vp = (t[0, prows, :].astype(BF16) for t in ((kp_ref, vp_ref) if p_out else (k_ref, v_ref)))
            q2, do2 = (_stack_heads(t[0, rows, :].astype(BF16)) for t in (q_ref, do_ref))
            qn2, don2 = (_stack_heads(t[0, nrows, :].astype(BF16)) for t in ((qn_ref, don_ref) if n_out else (q_ref, do_ref)))
            lse2, del2 = (stacked_cols(t[0, rows, :]) for t in (l_ref, d_ref))
            lsen2, deln2 = (stacked_cols(t[0, nrows, :]) for t in ((ln_ref, dn_ref) if n_out else (l_ref, d_ref)))
            valid = (own | (prev & (n > 0))) if p_out else (own | prev)
            next_ok = (band & (n < nsp - 1)) if n_out else band
            k2, v2 = jnp.concatenate([kp, kc], axis=0), jnp.concatenate([vp, vc], axis=0)
            p = jnp.exp(jnp.where(valid, _dot(q2, k2, NT) - lse2, NEG))
            ds = (p * (_dot(do2, v2, NT) - del2)).astype(BF16)
            pc = jnp.exp(jnp.where(next_ok, _dot(qn2, kc, NT) - lsen2, NEG))
            dsc = (pc * (_dot(don2, vc, NT) - deln2)).astype(BF16)
            dq_ref[0, rows, :] = _unstack_heads(_dot(ds, k2, NN))
            dk_ref[0, rows, :] = _dot(ds[:, BAND_BLOCK:], q2, TN) + _dot(dsc, qn2, TN)
            dv_ref[0, rows, :] = _dot(p[:, BAND_BLOCK:].astype(BF16), do2, TN) + _dot(pc.astype(BF16), don2, TN)

    span, chunk = (1, CH * NB, LANES), (1, CH, LANES)
    cur = lambda b, n, h, r: (b, n, h)
    prev = lambda b, n, h, r: (b, jnp.maximum(n * NB - 1, 0), h)
    nxt = lambda b, n, h, r: (b, jnp.minimum((n + 1) * NB, nch - 1), h)
    vcur = lambda b, n, h, r: (b, n, vo + h)
    vprev = lambda b, n, h, r: (b, jnp.maximum(n * NB - 1, 0), vo + h)
    q3, k3, do3, l3, d3 = (t.reshape(B, S, D) for t in (qh, kh, do, lse, delta))
    v3 = qkv.reshape(B, S, 9 * D)
    sp, ch = (lambda m: pl.BlockSpec(span, m)), (lambda m: pl.BlockSpec(chunk, m))
    out = jax.ShapeDtypeStruct((B, S, D), F32)
    dq, dk, dv = pl.pallas_call(
        body, name=name, grid=(B, nsp, HP, RG),
        in_specs=[sp(cur), ch(nxt), sp(cur), ch(nxt), sp(cur), ch(nxt), sp(cur), ch(nxt),
                  sp(cur), ch(prev), sp(vcur), ch(vprev)],
        out_specs=[sp(cur)] * 3, out_shape=[out] * 3,
        compiler_params=_params("parallel", "parallel", "parallel", "arbitrary"),
    )(q3, q3, do3, do3, l3, l3, d3, d3, k3, k3, v3, v3)
    return dq.reshape(B * S, D), dk.reshape(B * S, D), dv.reshape(B * S, D)


def _mix(os_, lses, name):
    T, D = os_[0].shape
    tm = _pick(T, 256, 8)

    def body(o0, o1, o2, l0, l1, l2, mix_ref, lse_ref):
        a, b, c = l0[...], l1[...], l2[...]
        m = jnp.maximum(jnp.maximum(a, b), c)
        ea, eb, ec = jnp.exp(a - m), jnp.exp(b - m), jnp.exp(c - m)
        s = ea + eb + ec
        mix_ref[...] = (ea / s) * o0[...] + (eb / s) * o1[...] + (ec / s) * o2[...]
        lse_ref[...] = m + jnp.log(s)

    blk = pl.BlockSpec((tm, D), lambda i: (i, 0))
    out = jax.ShapeDtypeStruct((T, D), F32)
    return pl.pallas_call(
        body, name=name, grid=(T // tm,), in_specs=[blk] * 6, out_specs=[blk, blk], out_shape=[out, out],
        compiler_params=_params("parallel"),
    )(*os_, *lses)


def _gate_fwd(fpre, b_f, B, S, D, name):
    nb = S // LANES

    def body(f_ref, b_ref, cum_ref, rep_ref, carry):
        @pl.when(pl.program_id(1) == 0)
        def _():
            carry[...] = jnp.zeros_like(carry)

        z = f_ref[...] + b_ref[...]
        lf = jnp.minimum(z, 0.0) - jnp.log(1.0 + jnp.exp(-jnp.abs(z)))
        tri = (_iota2((LANES, LANES), 0) >= _iota2((LANES, LANES), 1)).astype(F32)
        c = _dot(tri, lf, NN, HIGHEST) + carry[...]
        cum_ref[...] = c
        carry[...] += jnp.sum(lf, axis=0, keepdims=True)
        spread = ((_iota2((LANES, D), 1) >> 6) == _iota2((LANES, D), 0)).astype(F32)
        rep_ref[...] = _dot(c, spread, NN, HIGHEST)

    return pl.pallas_call(
        body, name=name, grid=(B, nb),
        in_specs=[pl.BlockSpec((LANES, LANES), lambda b, j: (b * nb + j, 0)), pl.BlockSpec((1, LANES), lambda b, j: (0, 0))],
        out_specs=[pl.BlockSpec((LANES, LANES), lambda b, j: (b * nb + j, 0)), pl.BlockSpec((LANES, D), lambda b, j: (b * nb + j, 0))],
        out_shape=[jax.ShapeDtypeStruct((B * S, LANES), F32), jax.ShapeDtypeStruct((B * S, D), F32)],
        scratch_shapes=[pltpu.VMEM((1, LANES), F32)], compiler_params=_params("arbitrary", "arbitrary"),
    )(fpre, b_f)


def _gate_bwd(dcum_k, dcum_q, fpre, b_f, B, S, name):
    nb = S // LANES

    def body(d_ref, dq_ref, f_ref, b_ref, o_ref, db_ref, carry):
        first = (pl.program_id(0) == 0) & (pl.program_id(1) == 0)

        @pl.when(pl.program_id(1) == 0)
        def _():
            carry[...] = jnp.zeros_like(carry)

        @pl.when(first)
        def _():
            db_ref[...] = jnp.zeros_like(db_ref)

        d = d_ref[...] + dq_ref[...]
        tri = (_iota2((LANES, LANES), 0) <= _iota2((LANES, LANES), 1)).astype(F32)
        dlf = _dot(tri, d, NN, HIGHEST) + carry[...]
        carry[...] += jnp.sum(d, axis=0, keepdims=True)
        z = f_ref[...] + b_ref[...]
        dz = dlf * (1.0 / (1.0 + jnp.exp(z)))
        o_ref[...] = dz.astype(BF16)
        db_ref[...] += jnp.sum(dz, axis=0, keepdims=True)

    rev = lambda b, j: (b * nb + nb - 1 - j, 0)
    blk = pl.BlockSpec((LANES, LANES), rev)
    vec = pl.BlockSpec((1, LANES), lambda b, j: (0, 0))
    return pl.pallas_call(
        body, name=name, grid=(B, nb), in_specs=[blk, blk, blk, vec], out_specs=[blk, vec],
        out_shape=[jax.ShapeDtypeStruct((B * S, LANES), BF16), jax.ShapeDtypeStruct((1, LANES), F32)],
        scratch_shapes=[pltpu.VMEM((1, LANES), F32)], compiler_params=_params("arbitrary", "arbitrary"),
    )(dcum_k, dcum_q, fpre, b_f)


def _fox_scores(q, k, ck_row, hm, diagonal):
    TB = q.shape[0]
    qm = jnp.where(hm, q, jnp.zeros_like(q))
    t = _dot(qm, k, NT) - ck_row
    if diagonal:
        t = jnp.where(_iota2((TB, TB), 1) <= _iota2((TB, TB), 0), t, NEG)
    return qm, t


def _fox_fetch(srcs, bufs, sems, slot):
    return [pltpu.make_async_copy(src, buf.at[slot], sems.at[slot, i]) for i, (src, buf) in enumerate(zip(srcs, bufs))]


FOX_BLOCK = 256
EXP_UNDERFLOW = 104.0


def _fox_live_blocks(cum, gain_q, gain_k, B, S, D):
    H, HP = D // HEAD_DIM, D // LANES
    nb = S // FOX_BLOCK
    bound = 1.02 * HEAD_DIM * SCALE * jnp.max(jnp.abs(gain_q)) * jnp.max(jnp.abs(gain_k))
    c = cum.reshape(B, S, LANES)[:, :, :H]
    decay = c[:, 0::FOX_BLOCK, None, :] - c[:, None, FOX_BLOCK - 1::FOX_BLOCK, :]
    below = jnp.arange(nb)[None, :] < jnp.arange(nb)[:, None]
    dead = (decay < -(EXP_UNDERFLOW + 2.0 * bound)) & below[None, :, :, None]
    live = jnp.logical_not(dead.reshape(B, nb, nb, HP, 2).all(axis=-1))
    first = jnp.argmax(live, axis=2)
    last = nb - 1 - jnp.argmax(live[:, ::-1], axis=1)
    to_table = lambda t: t.transpose(0, 2, 1).reshape(B * HP * nb).astype(jnp.int32)
    return to_table(first), to_table(last)


def _fox_fwd(qh, kh, kv, cq, ck, first, B, S, D, name):
    HP = D // LANES
    TB = FOX_BLOCK
    nb = S // TB

    def body(first_ref, q_ref, cq_ref, k_ref, v_ref, ck_ref, kp_ref, vp_ref, ckp_ref, k_hbm, v_hbm, ck_hbm,
             o_ref, lse_ref, kbuf, vbuf, cbuf, sems, m_sc, l_sc, acc):
        b, qb, hp = pl.program_id(0), pl.program_id(1), pl.program_id(2)
        kb0 = first_ref[(b * HP + hp) * nb + qb]
        n_far = jnp.maximum(qb - 1 - kb0, 0)

        def fetch(kb, slot):
            rows = pl.ds(pl.multiple_of(kb * TB, TB), TB)
            srcs = (k_hbm.at[b, rows, pl.ds(pl.multiple_of(hp * LANES, LANES), LANES)],
                    v_hbm.at[b, rows, pl.ds(pl.multiple_of((HP + hp) * LANES, LANES), LANES)],
                    ck_hbm.at[b * HP + hp, :, rows])
            return _fox_fetch(srcs, (kbuf, vbuf, cbuf), sems, slot)

        @pl.when(n_far > 0)
        def _():
            for cp in fetch(kb0, 0):
                cp.start()

        m_sc[...] = jnp.full_like(m_sc, NEG)
        l_sc[...] = jnp.zeros_like(l_sc)
        acc[...] = jnp.zeros_like(acc)

        def step(k, v, ck_rows, diagonal):
            q, v = q_ref[0], v.astype(BF16)
            cqv = cq_ref[0]
            a = acc[...]
            for h, hm in enumerate(_lane_masks()):
                _, t = _fox_scores(q, k, ck_rows[h], hm, diagonal)
                cqh = _head_col(cqv, h)
                m_prev = jnp.max(m_sc[h], axis=-1, keepdims=True)
                l_prev = jnp.max(l_sc[h], axis=-1, keepdims=True)
                m_new = jnp.maximum(m_prev, jnp.max(t, axis=-1, keepdims=True) + cqh)
                alpha = jnp.exp(m_prev - m_new)
                p = jnp.exp(t + (cqh - m_new))
                l_new = alpha * l_prev + jnp.sum(p, axis=-1, keepdims=True)
                a = jnp.where(hm, alpha * a + _dot(p.astype(BF16), v, NN), a)
                m_sc[h] = jnp.broadcast_to(m_new, (TB, LANES))
                l_sc[h] = jnp.broadcast_to(l_new, (TB, LANES))
            acc[...] = a

        def walk(i, carry):
            slot = lax.rem(i, 2)

            @pl.when(i + 1 < n_far)
            def _():
                for cp in fetch(kb0 + i + 1, 1 - slot):
                    cp.start()

            for cp in fetch(kb0 + i, slot):
                cp.wait()
            step(kbuf[slot], vbuf[slot], [cbuf[slot, pl.ds(h, 1), :] for h in range(2)], False)
            return carry

        lax.fori_loop(0, n_far, walk, 0)
        pl.when(kb0 < qb)(lambda: step(kp_ref[0], vp_ref[0], [ckp_ref[0, h:h + 1, :] for h in range(2)], False))
        step(k_ref[0], v_ref[0], [ck_ref[0, h:h + 1, :] for h in range(2)], True)
        a = acc[...]
        out = a
        lse = jnp.zeros_like(a)
        for h, hm in enumerate(_lane_masks()):
            out = jnp.where(hm, a / l_sc[h], out)
            lse = jnp.where(hm, m_sc[h] + jnp.log(l_sc[h]), lse)
        o_ref[0] = out
        lse_ref[0] = lse

    out = jax.ShapeDtypeStruct((B, S, D), F32)
    spec = pltpu.PrefetchScalarGridSpec(
        num_scalar_prefetch=1, grid=(B, nb, HP),
        in_specs=_fox_q_specs(2, TB) + _fox_k_specs(TB, HP, 0) + _fox_k_specs(TB, HP, -1) + [_ANY] * 3,
        out_specs=_fox_q_specs(2, TB), scratch_shapes=_fox_k_buffers(TB) + [
            pltpu.VMEM((2, TB, LANES), F32), pltpu.VMEM((2, TB, LANES), F32), pltpu.VMEM((TB, LANES), F32)])
    k3, v3 = kh.reshape(B, S, D), kv.reshape(B, S, 2 * D)
    o, lse = pl.pallas_call(
        body, name=name, grid_spec=spec, out_shape=[out, out], compiler_params=_params("parallel", "parallel", "parallel"),
    )(first, qh.reshape(B, S, D), cq.reshape(B, S, D), k3, v3, ck, k3, v3, ck, k3, v3, ck)
    return o.reshape(B * S, D), lse.reshape(B * S, D)


_ANY = pl.BlockSpec(memory_space=pl.ANY)


def _fox_q_specs(n, TB):
    return [pl.BlockSpec((1, TB, LANES), lambda b, i, h, table: (b, i, h))] * n


def _fox_k_specs(TB, HP, shift):
    blk = lambda i: jnp.maximum(i + shift, 0)
    return [pl.BlockSpec((1, TB, LANES), lambda b, i, h, table: (b, blk(i), h)),
            pl.BlockSpec((1, TB, LANES), lambda b, i, h, table: (b, blk(i), HP + h)),
            pl.BlockSpec((1, 8, TB), lambda b, i, h, table: (b * HP + h, 0, blk(i)))]


def _fox_k_buffers(TB):
    return [pltpu.VMEM((2, TB, LANES), BF16), pltpu.VMEM((2, TB, LANES), F32), pltpu.VMEM((2, 8, TB), F32),
            pltpu.SemaphoreType.DMA((2, 3))]


def _fox_bwd_dq(qh, kh, kv, cq, ck, first, do, lse, delta, B, S, D, name):
    HP = D // LANES
    TB = FOX_BLOCK
    nb = S // TB

    def body(first_ref, q_ref, cq_ref, do_ref, l_ref, d_ref, k_ref, v_ref, ck_ref, kp_ref, vp_ref, ckp_ref,
             k_hbm, v_hbm, ck_hbm, dq_ref, dc_ref, kbuf, vbuf, cbuf, sems, acc, acc_c):
        b, qb, hp = pl.program_id(0), pl.program_id(1), pl.program_id(2)
        kb0 = first_ref[(b * HP + hp) * nb + qb]
        n_far = jnp.maximum(qb - 1 - kb0, 0)

        def fetch(kb, slot):
            rows = pl.ds(pl.multiple_of(kb * TB, TB), TB)
            srcs = (k_hbm.at[b, rows, pl.ds(pl.multiple_of(hp * LANES, LANES), LANES)],
                    v_hbm.at[b, rows, pl.ds(pl.multiple_of((HP + hp) * LANES, LANES), LANES)],
                    ck_hbm.at[b * HP + hp, :, rows])
            return _fox_fetch(srcs, (kbuf, vbuf, cbuf), sems, slot)

        @pl.when(n_far > 0)
        def _():
            for cp in fetch(kb0, 0):
                cp.start()

        acc[...] = jnp.zeros_like(acc)
        acc_c[...] = jnp.zeros_like(acc_c)

        def step(k, v, ck_rows, diagonal):
            q, v = q_ref[0], v.astype(BF16)
            dout = do_ref[0].astype(BF16)
            a, c = acc[...], acc_c[...]
            for h, hm in enumerate(_lane_masks()):
                _, t = _fox_scores(q, k, ck_rows[h], hm, diagonal)
                p = jnp.exp(t + (_head_col(cq_ref[0], h) - _head_col(l_ref[0], h)))
                dom = jnp.where(hm, dout, jnp.zeros_like(dout))
                ds = p * (_dot(dom, v, NT) - _head_col(d_ref[0], h))
                a += _dot(ds.astype(BF16), jnp.where(hm, k, jnp.zeros_like(k)), NN)
                c = jnp.where(hm, c + jnp.sum(ds, axis=-1, keepdims=True), c)
            acc[...] = a
            acc_c[...] = c

        def walk(i, carry):
            slot = lax.rem(i, 2)

            @pl.when(i + 1 < n_far)
            def _():
                for cp in fetch(kb0 + i + 1, 1 - slot):
                    cp.start()

            for cp in fetch(kb0 + i, slot):
                cp.wait()
            step(kbuf[slot], vbuf[slot], [cbuf[slot, pl.ds(h, 1), :] for h in range(2)], False)
            return carry

        lax.fori_loop(0, n_far, walk, 0)
        pl.when(kb0 < qb)(lambda: step(kp_ref[0], vp_ref[0], [ckp_ref[0, h:h + 1, :] for h in range(2)], False))
        step(k_ref[0], v_ref[0], [ck_ref[0, h:h + 1, :] for h in range(2)], True)
        dq_ref[0] = acc[...]
        dc_ref[0] = acc_c[...]

    r3 = lambda t: t.reshape(B, S, D)
    out = jax.ShapeDtypeStruct((B, S, D), F32)
    spec = pltpu.PrefetchScalarGridSpec(
        num_scalar_prefetch=1, grid=(B, nb, HP),
        in_specs=_fox_q_specs(5, TB) + _fox_k_specs(TB, HP, 0) + _fox_k_specs(TB, HP, -1) + [_ANY] * 3,
        out_specs=_fox_q_specs(2, TB),
        scratch_shapes=_fox_k_buffers(TB) + [pltpu.VMEM((TB, LANES), F32), pltpu.VMEM((TB, LANES), F32)])
    k3, v3 = r3(kh), kv.reshape(B, S, 2 * D)
    dq, dc = pl.pallas_call(
        body, name=name, grid_spec=spec, out_shape=[out, out], compiler_params=_params("parallel", "parallel", "parallel"),
    )(first, r3(qh), r3(cq), r3(do), r3(lse), r3(delta), k3, v3, ck, k3, v3, ck, k3, v3, ck)
    return dq.reshape(B * S, D), dc.reshape(B * S, D)


def _fox_bwd_dkv(qh, kh, kv, cq, ck, last, do, lse, delta, B, S, D, name):
    HP = D // LANES
    TB = FOX_BLOCK
    nb = S // TB

    def body(last_ref, k_ref, v_ref, ck_ref, q_ref, cq_ref, do_ref, l_ref, d_ref, qn_ref, cqn_ref, don_ref, ln_ref,
             dn_ref, q_hbm, cq_hbm, do_hbm, l_hbm, d_hbm, dk_ref, dv_ref, dc_ref,
             qbuf, cqbuf, dobuf, lbuf, dbuf, sems, dk_acc, dv_acc, dc_acc):
        b, kb, hp = pl.program_id(0), pl.program_id(1), pl.program_id(2)
        qb_last = last_ref[(b * HP + hp) * nb + kb]
        n_far = jnp.maximum(qb_last - kb - 1, 0)

        def fetch(qb, slot):
            rows = pl.ds(pl.multiple_of(qb * TB, TB), TB)
            cols = pl.ds(pl.multiple_of(hp * LANES, LANES), LANES)
            srcs = [t.at[b, rows, cols] for t in (q_hbm, cq_hbm, do_hbm, l_hbm, d_hbm)]
            return _fox_fetch(srcs, (qbuf, cqbuf, dobuf, lbuf, dbuf), sems, slot)

        @pl.when(n_far > 0)
        def _():
            for cp in fetch(kb + 2, 0):
                cp.start()

        dk_acc[...] = jnp.zeros_like(dk_acc)
        dv_acc[...] = jnp.zeros_like(dv_acc)
        dc_acc[...] = jnp.zeros_like(dc_acc)

        def step(q, cqv, dout, lse_t, del_t, diagonal):
            k, v = k_ref[0], v_ref[0].astype(BF16)
            dout = dout.astype(BF16)
            dk, dv, dc = dk_acc[...], dv_acc[...], dc_acc[...]
            row = _iota2((8, TB), 0)
            for h, hm in enumerate(_lane_masks()):
                qm, t = _fox_scores(q, k, ck_ref[0, h:h + 1, :], hm, diagonal)
                p = jnp.exp(t + (_head_col(cqv, h) - _head_col(lse_t, h)))
                dom = jnp.where(hm, dout, jnp.zeros_like(dout))
                ds = p * (_dot(dom, v, NT) - _head_col(del_t, h))
                dv += _dot(p.astype(BF16), dom, TN)
                dk += _dot(ds.astype(BF16), qm, TN)
                dc = jnp.where(row == h, dc - jnp.sum(ds, axis=0, keepdims=True), dc)
            dk_acc[...] = dk
            dv_acc[...] = dv
            dc_acc[...] = dc

        step(q_ref[0], cq_ref[0], do_ref[0], l_ref[0], d_ref[0], True)
        pl.when(kb < qb_last)(lambda: step(qn_ref[0], cqn_ref[0], don_ref[0], ln_ref[0], dn_ref[0], False))

        def walk(i, carry):
            slot = lax.rem(i, 2)

            @pl.when(i + 1 < n_far)
            def _():
                for cp in fetch(kb + 3 + i, 1 - slot):
                    cp.start()

            for cp in fetch(kb + 2 + i, slot):
                cp.wait()
            step(qbuf[slot], cqbuf[slot], dobuf[slot], lbuf[slot], dbuf[slot], False)
            return carry

        lax.fori_loop(0, n_far, walk, 0)
        dk_ref[0] = dk_acc[...]
        dv_ref[0] = dv_acc[...].astype(BF16)
        dc_ref[0] = dc_acc[...]

    blk = (1, TB, LANES)
    kmap = lambda b, i, h, last: (b, i, h)
    vmap = lambda b, i, h, last: (b, i, HP + h)
    cmap = lambda b, i, h, last: (b * HP + h, 0, i)
    nmap = lambda b, i, h, last: (b, jnp.minimum(i + 1, nb - 1), h)
    r3 = lambda t: t.reshape(B, S, D)
    tile = lambda dt: pltpu.VMEM((2, TB, LANES), dt)
    qside = [r3(qh), r3(cq), r3(do), r3(lse), r3(delta)]
    spec = pltpu.PrefetchScalarGridSpec(
        num_scalar_prefetch=1, grid=(B, nb, HP),
        in_specs=[pl.BlockSpec(blk, kmap), pl.BlockSpec(blk, vmap), pl.BlockSpec((1, 8, TB), cmap)]
        + [pl.BlockSpec(blk, kmap)] * 5 + [pl.BlockSpec(blk, nmap)] * 5 + [_ANY] * 5,
        out_specs=[pl.BlockSpec(blk, kmap), pl.BlockSpec(blk, kmap), pl.BlockSpec((1, 8, TB), cmap)],
        scratch_shapes=[tile(BF16), tile(F32), tile(F32), tile(F32), tile(F32), pltpu.SemaphoreType.DMA((2, 5)),
                        pltpu.VMEM((TB, LANES), F32), pltpu.VMEM((TB, LANES), F32), pltpu.VMEM((8, TB), F32)])
    dk, dv, dc = pl.pallas_call(
        body, name=name, grid_spec=spec,
        out_shape=[jax.ShapeDtypeStruct((B, S, D), F32), jax.ShapeDtypeStruct((B, S, D), BF16),
                   jax.ShapeDtypeStruct((B * HP, 8, S), F32)],
        compiler_params=_params("parallel", "parallel", "parallel"),
    )(last, r3(kh), kv.reshape(B, S, 2 * D), ck, *qside, *qside, *qside)
    return dk.reshape(B * S, D), dv.reshape(B * S, D), dc


def _loss_and_grad(y, target, name):
    T, D = y.shape
    tm = _pick(T, 256, 8)

    def body(y_ref, t_ref, p_ref, dy_ref):
        e = y_ref[...] - t_ref[...]
        dy_ref[...] = e * (1.0 / D)
        p_ref[...] = jnp.sum(e * e, axis=0, keepdims=True)[None]

    blk = pl.BlockSpec((tm, D), lambda i: (i, 0))
    return pl.pallas_call(
        body, name=name, grid=(T // tm,), in_specs=[blk, blk],
        out_specs=[pl.BlockSpec((1, 1, D), lambda i: (i, 0, 0)), blk],
        out_shape=[jax.ShapeDtypeStruct((T // tm, 1, D), F32), jax.ShapeDtypeStruct((T, D), F32)],
        compiler_params=_params("parallel"),
    )(y, target)


def _pair_sum(x, got, name):
    _, R, C = x.shape
    tr = _pick(R, max(16, (8 * PACK_ROWS * LANES) // C), 16)

    def body(c_ref, x_ref, g_ref, o_ref):
        o_ref[0] = (x_ref[0].astype(F32) + g_ref[0].astype(F32)).astype(o_ref.dtype)

    spec = pltpu.PrefetchScalarGridSpec(
        num_scalar_prefetch=1, grid=(4, R // tr),
        in_specs=[pl.BlockSpec((1, tr, C), lambda a, i, c: (2 * a + c[0], i, 0)),
                  pl.BlockSpec((1, tr, C), lambda a, i, c: (a, i, 0))],
        out_specs=pl.BlockSpec((1, tr, C), lambda a, i, c: (a, i, 0)))
    return pl.pallas_call(
        body, name=name, grid_spec=spec, out_shape=jax.ShapeDtypeStruct((4, R, C), x.dtype),
        compiler_params=_params("parallel", "parallel"),
    )(lax.axis_index("c").astype(jnp.int32).reshape(1), x, got)


def _adamw(parts, w, m, v, name):
    R, C = w.shape
    P = parts.shape[0]
    tr = _pick(R, max(16, (PACK_ROWS * LANES) // C), 16)

    def body(p_ref, w_ref, m_ref, v_ref, g_out, d_out, m_out, v_out):
        g = p_ref[0].astype(F32)
        for s in range(1, P):
            g = g + p_ref[s].astype(F32)
        mn = ADAM_B1 * m_ref[...] + (1.0 - ADAM_B1) * g
        vn = ADAM_B2 * v_ref[...] + (1.0 - ADAM_B2) * (g * g)
        m_hat = mn / (1.0 - ADAM_B1 ** ADAM_STEP)
        v_hat = vn / (1.0 - ADAM_B2 ** ADAM_STEP)
        g_out[...] = g
        d_out[...] = -ADAM_LR * (m_hat / (jnp.sqrt(v_hat) + ADAM_EPS) + ADAM_WD * w_ref[...])
        m_out[...] = mn
        v_out[...] = vn

    blk = pl.BlockSpec((tr, C), lambda i: (i, 0))
    out = jax.ShapeDtypeStruct((R, C), F32)
    return pl.pallas_call(
        body, name=name, grid=(R // tr,),
        in_specs=[pl.BlockSpec((P, tr, C), lambda i: (0, i, 0)), blk, blk, blk],
        out_specs=[blk] * 4, out_shape=[out] * 4, compiler_params=_params("parallel"),
    )(parts, w, m, v)


def _mesh_pos():
    return lax.axis_index("x"), lax.axis_index("y"), lax.axis_index("c")


def _comm_call(body, xs, out_shapes, name):
    n = len(xs)
    any_spec = pl.BlockSpec(memory_space=pl.ANY)
    return pl.pallas_call(
        body, name=name, out_shape=out_shapes, in_specs=[any_spec] * n, out_specs=[any_spec] * n,
        scratch_shapes=[pltpu.SemaphoreType.DMA((7, n)), pltpu.SemaphoreType.DMA((7, n)), pltpu.SemaphoreType.DMA((n,))],
    )(*xs)


def _all_gather(xs, name):
    n = len(xs)

    def body(*refs):
        x_refs, out_refs = refs[:n], refs[n:2 * n]
        send_sems, recv_sems, local_sems = refs[2 * n:]
        x_, y_, c_ = _mesh_pos()
        me, sibling = (x_, y_, c_), (x_, y_, 1 - c_)
        chips = [(1 - x_, y_), (x_, 1 - y_), (1 - x_, 1 - y_)]

        def slot(t, px, py, pc):
            return out_refs[t].at[4 * px + 2 * py + pc]

        def copy(t, k, block, to, src=None):
            return pltpu.make_async_remote_copy(
                src_ref=slot(t, *block) if src is None else src, dst_ref=slot(t, *block),
                send_sem=send_sems.at[k, t], recv_sem=recv_sems.at[k, t], device_id=to, device_id_type=MESH_ID)

        mine = [pltpu.make_async_copy(x_refs[t], slot(t, *me), local_sems.at[t]) for t in range(n)]
        first = [copy(t, 0, me, sibling, src=x_refs[t]) for t in range(n)]
        first += [copy(t, 1 + j, me, (*chip, c_), src=x_refs[t]) for j, chip in enumerate(chips) for t in range(n)]
        for cp in mine + first:
            cp.start()
        passed = []
        for j, chip in enumerate(chips):
            for t in range(n):
                copy(t, 1 + j, (*chip, c_), me).wait_recv()
                passed.append(copy(t, 4 + j, (*chip, c_), sibling))
                passed[-1].start()
        for t in range(n):
            copy(t, 0, sibling, me).wait_recv()
        for j, chip in enumerate(chips):
            for t in range(n):
                copy(t, 4 + j, (*chip, 1 - c_), me).wait_recv()
        for cp in first + passed:
            cp.wait_send()
        for cp in mine:
            cp.wait()

    return _comm_call(body, xs, [jax.ShapeDtypeStruct((N_DEV,) + x.shape, x.dtype) for x in xs], name)


def _pair_exchange(xs, name):
    n = len(xs)

    def body(*refs):
        x_refs, out_refs = refs[:n], refs[n:2 * n]
        send_sems, recv_sems, _ = refs[2 * n:]
        x_, y_, c_ = _mesh_pos()
        copies = [pltpu.make_async_remote_copy(
            src_ref=x_refs[t].at[2 * a + 1 - c_], dst_ref=out_refs[t].at[a],
            send_sem=send_sems.at[a, t], recv_sem=recv_sems.at[a, t],
            device_id=(x_, y_, 1 - c_), device_id_type=MESH_ID) for a in range(4) for t in range(n)]
        for cp in copies:
            cp.start()
        for cp in copies:
            cp.wait_recv()
        for cp in copies:
            cp.wait_send()

    return _comm_call(body, xs, [jax.ShapeDtypeStruct((4,) + x.shape[1:], x.dtype) for x in xs], name)


def _chip_exchange(xs, name):
    n = len(xs)

    def body(*refs):
        x_refs, out_refs = refs[:n], refs[n:2 * n]
        send_sems, recv_sems, local_sems = refs[2 * n:]
        x_, y_, c_ = _mesh_pos()
        chip = 2 * x_ + y_
        mine = [pltpu.make_async_copy(x_refs[t].at[chip], out_refs[t].at[chip], local_sems.at[t]) for t in range(n)]
        copies = []
        for k in range(1, 4):
            px, py = x_ ^ (k >> 1), y_ ^ (k & 1)
            for t in range(n):
                copies.append(pltpu.make_async_remote_copy(
                    src_ref=x_refs[t].at[2 * px + py], dst_ref=out_refs[t].at[chip],
                    send_sem=send_sems.at[k - 1, t], recv_sem=recv_sems.at[k - 1, t],
                    device_id=(px, py, c_), device_id_type=MESH_ID))
        for cp in mine + copies:
            cp.start()
        for cp in copies:
            cp.wait_recv()
        for cp in copies:
            cp.wait_send()
        for cp in mine:
            cp.wait()

    return _comm_call(body, xs, [jax.ShapeDtypeStruct(x.shape, x.dtype) for x in xs], name)


def _ffn_fwd(h, norm, w_in, w_out, tag):
    T, D = h.shape
    F = w_out.shape[0]
    hn = _rmsnorm_fwd(h, norm, f"{tag}_norm")
    gate, up, act = _mm_swiglu_fwd(hn, w_in, f"{tag}_in")
    out = _mm(act, w_out, "nn", M=T, N=D, K=F, res=h, scale=0.5, name=f"{tag}_out")
    return out, (h, hn, gate, up, act)


def _ffn_bwd(dout, saved, norm, w_in, w_out, tag):
    h, hn, gate, up, act = saved
    T, D = h.shape
    F = w_out.shape[0]
    dg, du = _mm_swiglu_bwd(dout, w_out, gate, up, f"{tag}_dact")
    dw_out = _mm(act, dout, "tn", M=F, N=D, K=T, scale=0.5, out_dtype=BF16, name=f"{tag}_dwout")
    dhn = _mm(dg, w_in, "nt", M=T, N=D, K=2 * F, a_rest=(du,), name=f"{tag}_dhn")
    dw_in = _mm(hn, dg, "tn", M=D, N=2 * F, K=T, b2=du, out_dtype=BF16, name=f"{tag}_dwin")
    dh, dnorm = _rmsnorm_bwd(h, norm, dhn, dout, f"{tag}_dnorm")
    return dh, dnorm, dw_in, dw_out


def _local_step(x, positions, target, W):
    B, S, D = x.shape
    T = B * S
    H, HP = D // HEAD_DIM, D // LANES
    G = {}
    h0 = x.reshape(T, D)
    cs = _rope_tables(positions, "rope_tables")

    h1, ffn00 = _ffn_fwd(h0, W["ffn_norm"][0, 0], W["ffn_w_in"][0, 0], W["ffn_w_out"][0, 0], "ffn00")
    hna = _rmsnorm_fwd(h1, W["mix_norm"][0], "mixa_norm")
    qkv = _mm(hna, W["a_w_qkv"], "nn", M=T, N=9 * D, K=D, name="a_qkv")
    qs, ks, os_, lses = [], [], [], []
    for g in range(3):
        qs.append(_qknorm_fwd(qkv, 3 * g * D, W["a_q_norm"][0, g], cs, D, F32, f"a_qnorm{g}", out_scale=SCALE))
        ks.append(_qknorm_fwd(qkv, (3 * g + 1) * D, W["a_k_norm"][0, g], cs, D, F32, f"a_knorm{g}"))
        o, l = _band_fwd(qs[g], ks[g], qkv, g, B, S, D, f"a_band{g}")
        os_.append(o)
        lses.append(l)
    mixed, lse_a = _mix(os_, lses, "a_mix")
    h2 = _mm(mixed, W["a_w_o"], "nn", M=T, N=D, K=D, res=h1, name="a_out")
    h3, ffn01 = _ffn_fwd(h2, W["ffn_norm"][0, 1], W["ffn_w_in"][0, 1], W["ffn_w_out"][0, 1], "ffn01")

    hnkv = _rmsnorm_fwd(h3, W["kv_norm"], "kv_norm")
    kv = _mm(hnkv, W["kv_w_kv"], "nn", M=T, N=2 * D, K=D, name="kv_proj")
    fpre = _mm(hnkv, W["kv_w_f"], "nn", M=T, N=LANES, K=D, name="kv_gate_proj")
    b_f = jnp.pad(W["kv_b_f"], (0, LANES - H)).reshape(1, LANES)
    kh = _qknorm_fwd(kv, 0, W["kv_k_norm"], None, D, BF16, "kv_knorm")
    cum, cq = _gate_fwd(fpre, b_f, B, S, D, "kv_gate")
    ck = cum.reshape(B, S, LANES)[:, :, :H].reshape(B, S, HP, 2).transpose(0, 2, 3, 1)
    ck = jnp.pad(ck, ((0, 0), (0, 0), (0, 6), (0, 0))).reshape(B * HP, 8, S)
    h4, ffn10 = _ffn_fwd(h3, W["ffn_norm"][1, 0], W["ffn_w_in"][1, 0], W["ffn_w_out"][1, 0], "ffn10")
    hnb = _rmsnorm_fwd(h4, W["mix_norm"][1], "mixb_norm")
    qraw = _mm(hnb, W["b_w_q"], "nn", M=T, N=D, K=D, name="b_q")
    qh = _qknorm_fwd(qraw, 0, W["b_q_norm"][0], None, D, BF16, "b_qnorm", out_scale=SCALE)
    first, last = _fox_live_blocks(cum, W["b_q_norm"][0], W["kv_k_norm"], B, S, D)
    ob, lse_b = _fox_fwd(qh, kh, kv, cq, ck, first, B, S, D, "b_fox")
    h5 = _mm(ob, W["b_w_o"], "nn", M=T, N=D, K=D, res=h4, name="b_out")
    h6, ffn11 = _ffn_fwd(h5, W["ffn_norm"][1, 1], W["ffn_w_in"][1, 1], W["ffn_w_out"][1, 1], "ffn11")

    sq, dy = _loss_and_grad(h6, target.reshape(T, D), "loss")

    dn = [[None, None], [None, None]]
    dwi = [[None, None], [None, None]]
    dwo = [[None, None], [None, None]]
    dh5, dn[1][1], dwi[1][1], dwo[1][1] = _ffn_bwd(dy, ffn11, W["ffn_norm"][1, 1], W["ffn_w_in"][1, 1], W["ffn_w_out"][1, 1], "ffn11")
    dob = _mm(dh5, W["b_w_o"], "nt", M=T, N=D, K=D, name="b_dout")
    G["b_w_o"] = _mm(ob, dh5, "tn", M=D, N=D, K=T, out_dtype=BF16, name="b_dwo")[None]
    delta_b = _head_rowsum(dob, ob, "b_delta")
    dqh, dcq = _fox_bwd_dq(qh, kh, kv, cq, ck, first, dob, lse_b, delta_b, B, S, D, "b_fox_dq")
    dkh, dv_b, dck = _fox_bwd_dkv(qh, kh, kv, cq, ck, last, dob, lse_b, delta_b, B, S, D, "b_fox_dkv")
    dqraw, dg = _qknorm_bwd(qraw, 0, W["b_q_norm"][0], None, dqh, D, "b_dqnorm", out_scale=SCALE)
    G["b_q_norm"] = dg[None]
    dhnb = _mm(dqraw, W["b_w_q"], "nt", M=T, N=D, K=D, name="b_dhn")
    G["b_w_q"] = _mm(hnb, dqraw, "tn", M=D, N=D, K=T, out_dtype=BF16, name="b_dwq")[None]
    dh4, dmix_b = _rmsnorm_bwd(h4, W["mix_norm"][1], dhnb, dh5, "mixb_dnorm")
    dh3, dn[1][0], dwi[1][0], dwo[1][0] = _ffn_bwd(dh4, ffn10, W["ffn_norm"][1, 0], W["ffn_w_in"][1, 0], W["ffn_w_out"][1, 0], "ffn10")

    dkraw, G["kv_k_norm"] = _qknorm_bwd(kv, 0, W["kv_k_norm"], None, dkh, D, "kv_dknorm")
    dcum = dck.reshape(B, HP, 8, S)[:, :, :2].transpose(0, 3, 1, 2).reshape(T, H)
    dcum = jnp.pad(dcum, ((0, 0), (0, LANES - H)))
    dcum_q = jnp.pad(dcq.reshape(T, H, HEAD_DIM)[:, :, 0], ((0, 0), (0, LANES - H)))
    dfpre, db_f = _gate_bwd(dcum, dcum_q, fpre, b_f, B, S, "kv_dgate")
    G["kv_b_f"] = db_f[0, :H]
    dhnkv = _mm(dkraw, W["kv_w_kv"], "nt", M=T, N=D, K=D, name="kv_dhn_k")
    dhnkv = _mm(dv_b, W["kv_w_kv"], "nt", M=T, N=D, K=D, b_off=(0, D), res=dhnkv, name="kv_dhn_v")
    dhnkv = _mm(dfpre, W["kv_w_f"], "nt", M=T, N=D, K=LANES, res=dhnkv, name="kv_dhn_f")
    dw_k = _mm(hnkv, dkraw, "tn", M=D, N=D, K=T, out_dtype=BF16, name="kv_dwk")
    dw_v = _mm(hnkv, dv_b, "tn", M=D, N=D, K=T, out_dtype=BF16, name="kv_dwv")
    dw_f = _mm(hnkv, dfpre, "tn", M=D, N=LANES, K=T, out_dtype=BF16, name="kv_dwf")
    G["kv_w"] = jnp.concatenate([dw_k, dw_v, dw_f[:, :H]], axis=1)
    dh3, G["kv_norm"] = _rmsnorm_bwd(h3, W["kv_norm"], dhnkv, dh3, "kv_dnorm")

    dh2, dn[0][1], dwi[0][1], dwo[0][1] = _ffn_bwd(dh3, ffn01, W["ffn_norm"][0, 1], W["ffn_w_in"][0, 1], W["ffn_w_out"][0, 1], "ffn01")
    dmixed = _mm(dh2, W["a_w_o"], "nt", M=T, N=D, K=D, name="a_dmixed")
    G["a_w_o"] = _mm(mixed, dh2, "tn", M=D, N=D, K=T, out_dtype=BF16, name="a_dwo")[None]
    delta_a = _head_rowsum(dmixed, mixed, "a_delta")
    dhna = None
    dw_qkv, dqn, dkn = [], [], []
    for g in range(3):
        dq, dk, dv = _band_bwd(qs[g], ks[g], qkv, dmixed, lse_a, delta_a, g, B, S, D, f"a_dband{g}")
        dqr, dgq = _qknorm_bwd(qkv, 3 * g * D, W["a_q_norm"][0, g], cs, dq, D, f"a_dqnorm{g}", out_scale=SCALE)
        dkr, dgk = _qknorm_bwd(qkv, (3 * g + 1) * D, W["a_k_norm"][0, g], cs, dk, D, f"a_dknorm{g}")
        dqn.append(dgq)
        dkn.append(dgk)
        dhna = _mm(dqr, W["a_w_qkv"], "nt", M=T, N=D, K=3 * D, a_rest=(dkr, dv), b_off=(0, 3 * g * D), res=dhna,
                   name=f"a_dhn{g}")
        for j, d in enumerate((dqr, dkr, dv)):
            dw_qkv.append(_mm(hna, d, "tn", M=D, N=D, K=T, out_dtype=BF16, name=f"a_dwqkv{3 * g + j}"))
    G["a_w_qkv"] = jnp.concatenate(dw_qkv, axis=1)[None]
    G["a_q_norm"] = jnp.stack(dqn)[None]
    G["a_k_norm"] = jnp.stack(dkn)[None]
    dh1, dmix_a = _rmsnorm_bwd(h1, W["mix_norm"][0], dhna, dh2, "mixa_dnorm")
    dh0, dn[0][0], dwi[0][0], dwo[0][0] = _ffn_bwd(dh1, ffn00, W["ffn_norm"][0, 0], W["ffn_w_in"][0, 0], W["ffn_w_out"][0, 0], "ffn00")

    G["mix_norm"] = jnp.stack([dmix_a, dmix_b])
    G["ffn_norm"] = jnp.stack([jnp.stack(r) for r in dn])
    G["ffn_w_in"] = jnp.stack([jnp.stack(r) for r in dwi])
    G["ffn_w_out"] = jnp.stack([jnp.stack(r) for r in dwo])
    return jnp.sum(sq), dh0.reshape(B, S, D), G


SMALL =["mix_norm", "kv_norm", "a_q_norm", "a_k_norm", "kv_b_f", "kv_k_norm", "b_q_norm"]


def _pack(blocks, lead):
    flat = [b.reshape(b.shape[:lead] + (-1,)) for b in blocks]
    flat = jnp.concatenate(flat, axis=lead)
    n = flat.shape[-1]
    unit = PACK_ROWS * LANES
    padded = -(-n // unit) * unit if n > unit else -(-n // (16 * LANES)) * 16 * LANES
    flat = jnp.pad(flat, [(0, 0)] * lead + [(0, padded - n)])
    return flat.reshape(flat.shape[:lead] + (padded // LANES, LANES))


def _unpack(packed, shapes, lead):
    flat = packed.reshape(packed.shape[:lead] + (-1,))
    out, off = [], 0
    for shp in shapes:
        n = int(np.prod(shp))
        out.append(flat[..., off:off + n].reshape(packed.shape[:lead] + tuple(shp)))
        off += n
    return out


def kernel(x, positions, ffn_norm, ffn_w_in, ffn_w_out, mix_norm, a_w_qkv, a_q_norm, a_k_norm, a_w_o, kv_norm, kv_w, kv_b_f, kv_k_norm, b_w_q, b_q_norm, b_w_o, loss_target, m_ffn_norm, m_ffn_w_in, m_ffn_w_out, m_mix_norm, m_a_w_qkv, m_a_q_norm, m_a_k_norm, m_a_w_o, m_kv_norm, m_kv_w, m_kv_b_f, m_kv_k_norm, m_b_w_q, m_b_q_norm, m_b_w_o, v_ffn_norm, v_ffn_w_in, v_ffn_w_out, v_mix_norm, v_a_w_qkv, v_a_q_norm, v_a_k_norm, v_a_w_o, v_kv_norm, v_kv_w, v_kv_b_f, v_kv_k_norm, v_b_w_q, v_b_q_norm, v_b_w_o):
    names = ["ffn_norm", "ffn_w_in", "ffn_w_out", "mix_norm", "a_w_qkv", "a_q_norm", "a_k_norm", "a_w_o", "kv_norm",
             "kv_w", "kv_b_f", "kv_k_norm", "b_w_q", "b_q_norm", "b_w_o"]
    w = dict(zip(names, (ffn_norm, ffn_w_in, ffn_w_out, mix_norm, a_w_qkv, a_q_norm, a_k_norm, a_w_o, kv_norm, kv_w,
                         kv_b_f, kv_k_norm, b_w_q, b_q_norm, b_w_o)))
    m = dict(zip(names, (m_ffn_norm, m_ffn_w_in, m_ffn_w_out, m_mix_norm, m_a_w_qkv, m_a_q_norm, m_a_k_norm, m_a_w_o,
                         m_kv_norm, m_kv_w, m_kv_b_f, m_kv_k_norm, m_b_w_q, m_b_q_norm, m_b_w_o)))
    v = dict(zip(names, (v_ffn_norm, v_ffn_w_in, v_ffn_w_out, v_mix_norm, v_a_w_qkv, v_a_q_norm, v_a_k_norm, v_a_w_o,
                         v_kv_norm, v_kv_w, v_kv_b_f, v_kv_k_norm, v_b_w_q, v_b_q_norm, v_b_w_o)))
    B, S, D = x.shape
    H = D // HEAD_DIM
    F = ffn_w_out.shape[2] * N_DEV
    Ds, Fs = D // N_DEV, F // N_DEV
    me = 4 * lax.axis_index("x") + 2 * lax.axis_index("y") + lax.axis_index("c")
    squares = ("a_w_o", "b_w_q", "b_w_o")

    def shard_views(t):
        return [t["ffn_w_in"].reshape(4 * D, -1), t["ffn_w_out"].reshape(-1, D), t["a_w_qkv"][0], t["kv_w"],
                jnp.concatenate([t[n][0] for n in squares], axis=0)]

    def to_shards(views):
        sq3 = views[4].reshape(3, 1, Ds, D)
        out = {"ffn_w_in": views[0].reshape(ffn_w_in.shape), "ffn_w_out": views[1].reshape(ffn_w_out.shape),
               "a_w_qkv": views[2][None], "kv_w": views[3]}
        out.update({n: sq3[i] for i, n in enumerate(squares)})
        return out

    norm_rows = jnp.pad(ffn_norm.reshape(4, Ds), ((0, 4), (0, 0)))
    g_in, g_out, g_qkv, g_kv, g_sq, g_norm = _all_gather(
        [s.astype(BF16) for s in shard_views(w)] + [norm_rows], "gather_weights")
    W = {n: w[n] for n in SMALL}
    W["ffn_norm"] = g_norm[:, :4].transpose(1, 0, 2).reshape(2, 2, D)
    W["ffn_w_in"] = g_in.transpose(1, 0, 2).reshape(2, 2, D, 2 * F)
    W["ffn_w_out"] = g_out.reshape(N_DEV, 4, Fs, D).transpose(1, 0, 2, 3).reshape(2, 2, F, D)
    W["a_w_qkv"] = g_qkv.transpose(1, 0, 2).reshape(D, 9 * D)
    kv_full = g_kv.transpose(1, 0, 2).reshape(D, 2 * D + H)
    W["kv_w_kv"] = kv_full[:, :2 * D]
    W["kv_w_f"] = jnp.pad(kv_full[:, 2 * D:], ((0, 0), (0, LANES - H)))
    sq_full = g_sq.reshape(N_DEV, 3, Ds, D).transpose(1, 0, 2, 3).reshape(3, D, D)
    W["a_w_o"], W["b_w_q"], W["b_w_o"] = sq_full[0], sq_full[1], sq_full[2]

    sq, grad_x, G = _local_step(x, positions, loss_target, W)
    loss = lax.psum(sq * (0.5 / D), ("x", "y", "c"))

    blocks = [G["ffn_w_in"].reshape(4 * D, N_DEV, -1).transpose(1, 0, 2),
              G["ffn_w_out"].reshape(4, N_DEV, Fs, D).transpose(1, 0, 2, 3).reshape(N_DEV, 4 * Fs, D),
              G["a_w_qkv"][0].reshape(D, N_DEV, -1).transpose(1, 0, 2),
              G["kv_w"].reshape(D, N_DEV, -1).transpose(1, 0, 2),
              jnp.concatenate([G[n][0].reshape(N_DEV, Ds, D) for n in squares], axis=1)]
    blocks = [b.astype(BF16) for b in blocks]
    got = _pair_exchange(blocks, "exchange_pair")
    parts = _chip_exchange([_pair_sum(b, s, f"pair_sum{i}") for i, (b, s) in enumerate(zip(blocks, got))], "exchange_chips")
    views = [shard_views(t) for t in (w, m, v)]
    outs = [_adamw(parts[i], views[0][i], views[1][i], views[2][i], f"adamw_matrix{i}") for i in range(len(parts))]
    new = {}
    for k, kind in enumerate(("grad", "delta", "new_m", "new_v")):
        for n, arr in to_shards([o[k] for o in outs]).items():
            new[kind, n] = arr

    small_all = ["ffn_norm"] + SMALL
    vec_parts = _all_gather([_pack([G[n] for n in small_all], 0)], "gather_vector_grads")[0]
    vec_parts = _unpack(vec_parts, [G[n].shape for n in small_all], 1)
    norm_g = lax.dynamic_slice_in_dim(vec_parts[0], me * ffn_norm.shape[2], ffn_norm.shape[2], axis=3)
    rep_g = _pack(vec_parts[1:], 1)
    for ns, parts_s in ((["ffn_norm"], _pack([norm_g], 1)), (SMALL, rep_g)):
        outs = _adamw(parts_s, *(_pack([t[n] for n in ns], 0) for t in (w, m, v)), "adamw_" + ns[0])
        for kind, packed in zip(("grad", "delta", "new_m", "new_v"), outs):
            for n, arr in zip(ns, _unpack(packed, [w[n].shape for n in ns], 0)):
                new[kind, n] = arr

    result = [loss, grad_x]
    for kind in ("grad", "delta", "new_m", "new_v"):
        result += [new[kind, n] for n in names]
    return tuple(result)
```

```python
import functools

import jax
import jax.numpy as jnp
import numpy as np
from jax import lax
from jax.experimental import pallas as pl
from jax.experimental.pallas import tpu as pltpu

F32 = jnp.float32
BF16 = jnp.bfloat16
HIGHEST = lax.Precision.HIGHEST

HEAD_DIM = 64
ROT_DIM = HEAD_DIM // 4
ROPE_THETA = 500000.0
DILATED_GROUPS = ((128, 1), (512, 4), (2048, 16))
BAND_BLOCK = 128
EPS = 1e-6
ADAM_LR, ADAM_B1, ADAM_B2, ADAM_EPS, ADAM_WD, ADAM_STEP = 0.001, 0.9, 0.999, 1e-08, 0.01, 10

LANES = 128
N_DEV = 8
VMEM_LIMIT = 56 * 1024 * 1024
NEG = -1e30
SCALE = HEAD_DIM ** -0.5
PACK_ROWS = 1024
MESH_ID = pl.DeviceIdType.MESH


def _params(*sem):
    return pltpu.CompilerParams(dimension_semantics=sem, vmem_limit_bytes=VMEM_LIMIT)


def _divs(n, cap, mult=LANES):
    d = [t for t in range(mult, min(n, cap) + 1, mult) if n % t == 0]
    return d or [n]


def _pick(n, cap, mult=LANES):
    return _divs(n, cap, mult)[-1]


def _mm_tiles(M, N, K, budget=1408 * 1024):
    best = None
    for tm in _divs(M, 1408):
        for tn in _divs(N, 1408):
            if tm * tn <= budget and (best is None or tm * tn > best[0] * best[1]):
                best = (tm, tn)
    if best is None:
        best = (_divs(M, 1408)[0], _divs(N, 1408)[0])
    tm, tn = best
    tks = [t for t in _divs(K, 2048) if (tm + tn) * t <= 3 * 1024 * 1024]
    tk = tks[-1] if tks else _divs(K, 2048)[0]
    return tm, tn, tk


def _dot(a, b, dims, precision=None):
    return lax.dot_general(a, b, (dims, ((), ())), precision=precision, preferred_element_type=F32)


NN = ((1,), (0,))
NT = ((1,), (1,))
TN = ((0,), (0,))


def _dot_split(x, mat):
    hi = x.astype(BF16)
    lo = (x - hi.astype(F32)).astype(BF16)
    m = mat.astype(BF16)
    return _dot(hi, m, NN) + _dot(lo, m, NN)


def _dot_once(x, mat):
    return _dot(x.astype(BF16), mat.astype(BF16), NN)


def _mm(a, b, mode, *, M, N, K, a_off=(0, 0), b_off=(0, 0), a_rest=(), b2=None, res=None, scale=1.0, out_dtype=F32,
        name, gather=None):
    a_parts = 1 + len(a_rest)
    tm, tn, tk = _mm_tiles(M, N // 2 if b2 is not None else N, K // a_parts)
    nk = K // tk
    k_part, n_half = nk // a_parts, (N // tn) // 2
    if mode == "tn":
        a_blk, b_blk, dims = (tk, tm), (tk, tn), TN
    elif mode == "nt":
        a_blk, b_blk, dims = (tm, tk), (tn, tk), NT
    else:
        a_blk, b_blk, dims = (tm, tk), (tk, tn), NN
    for off, blk in ((a_off, a_blk), (b_off, b_blk)):
        assert off[0] % blk[0] == 0 and off[1] % blk[1] == 0, (name, off, blk)
    ao = (a_off[0] // a_blk[0], a_off[1] // a_blk[1])
    bo = (b_off[0] // b_blk[0], b_off[1] // b_blk[1])
    if mode == "tn":
        a_map = lambda i, j, k: (k + ao[0], i + ao[1])
        b_map = lambda i, j, k: (k + bo[0], j + bo[1])
    elif mode == "nt":
        a_map = lambda i, j, k: (i + ao[0], k + ao[1])
        b_map = lambda i, j, k: (j + bo[0], k + bo[1])
    else:
        a_map = lambda i, j, k: (i + ao[0], k + ao[1])
        b_map = lambda i, j, k: (k + bo[0], j + bo[1])
    has_res = res is not None
    if a_rest:
        assert mode != "tn" and a_off == (0, 0)
        part_map = lambda p: (lambda i, j, k: (i, jnp.clip(k - p * k_part, 0, k_part - 1)))
        a_map = part_map(0)
    if b2 is not None:
        assert mode != "nt" and b_off == (0, 0)
        b_map = lambda i, j, k: (k, jnp.minimum(j, n_half - 1))
        b2_map = lambda i, j, k: (k, jnp.maximum(j - n_half, 0))

    def body(*refs):
        refs = list(refs)
        a_ref, b_ref = refs.pop(0), refs.pop(0)
        a_val, b_val = a_ref[...].astype(BF16), b_ref[...].astype(BF16)
        for p in range(1, a_parts):
            a_val = jnp.where(pl.program_id(2) >= p * k_part, refs.pop(0)[...].astype(BF16), a_val)
        if b2 is not None:
            b_val = jnp.where(pl.program_id(1) < n_half, b_val, refs.pop(0)[...].astype(BF16))
        r_ref = refs.pop(0) if has_res else None
        o_ref = refs.pop(0)

        def finish(total):
            out = total * scale
            if has_res:
                out = r_ref[...] + out
            o_ref[...] = out.astype(out_dtype)

        part = _dot(a_val, b_val, dims)
        if nk == 1:
            finish(part)
            return
        acc = refs[-1]
        k = pl.program_id(2)

        @pl.when(k == 0)
        def _():
            acc[...] = part

        @pl.when(k > 0)
        def _():
            acc[...] += part

        pl.when(k == nk - 1)(lambda: finish(acc[...]))

    in_specs = [pl.BlockSpec(a_blk, a_map), pl.BlockSpec(b_blk, b_map)]
    args = [a, b]
    for p, extra in enumerate(a_rest, 1):
        in_specs.append(pl.BlockSpec(a_blk, part_map(p)))
        args.append(extra)
    if b2 is not None:
        in_specs.append(pl.BlockSpec(b_blk, b2_map))
        args.append(b2)
    if has_res:
        in_specs.append(pl.BlockSpec((tm, tn), lambda i, j, k: (i, j)))
        args.append(res)
    out, gathered = _call(
        body, name=name, grid=(M // tm, N // tn, nk), in_specs=in_specs,
        out_specs=pl.BlockSpec((tm, tn), lambda i, j, k: (i, j)),
        out_shape=jax.ShapeDtypeStruct((M, N), out_dtype),
        scratch_shapes=[pltpu.VMEM((tm, tn), F32)] if nk > 1 else [],
        semantics=("parallel", "parallel", "arbitrary"), args=args, gather=gather)
    return out if gather is None else (out, gathered)


def _rmsnorm_fwd(x, g, name):
    T, D = x.shape
    tm = _pick(T, 256, 8)

    def body(x_ref, g_ref, y_ref):
        xv = x_ref[...]
        r = lax.rsqrt(jnp.mean(xv * xv, axis=-1, keepdims=True) + EPS)
        y_ref[...] = (xv * r * g_ref[...]).astype(BF16)

    return pl.pallas_call(
        body, name=name, grid=(T // tm,),
        in_specs=[pl.BlockSpec((tm, D), lambda i: (i, 0)), pl.BlockSpec((1, D), lambda i: (0, 0))],
        out_specs=pl.BlockSpec((tm, D), lambda i: (i, 0)),
        out_shape=jax.ShapeDtypeStruct((T, D), BF16), compiler_params=_params("parallel"),
    )(x, g.reshape(1, D))


def _rmsnorm_bwd(x, g, dy, dres, name):
    T, D = x.shape
    tm = _pick(T, 256, 8)

    def body(x_ref, g_ref, dy_ref, dres_ref, dx_ref, dg_ref):
        xv = x_ref[...]
        r = lax.rsqrt(jnp.mean(xv * xv, axis=-1, keepdims=True) + EPS)
        xn = xv * r
        d = dy_ref[...]
        dgy = d * g_ref[...]
        dx_ref[...] = dres_ref[...] + r * (dgy - xn * jnp.mean(dgy * xn, axis=-1, keepdims=True))

        @pl.when(pl.program_id(0) == 0)
        def _():
            dg_ref[...] = jnp.zeros_like(dg_ref)

        dg_ref[...] += jnp.sum(d * xn, axis=0, keepdims=True)

    row = pl.BlockSpec((tm, D), lambda i: (i, 0))
    vec = pl.BlockSpec((1, D), lambda i: (0, 0))
    dx, dg = pl.pallas_call(
        body, name=name, grid=(T // tm,), in_specs=[row, vec, row, row], out_specs=[row, vec],
        out_shape=[jax.ShapeDtypeStruct((T, D), F32), jax.ShapeDtypeStruct((1, D), F32)],
        compiler_params=_params("arbitrary"),
    )(x, g.reshape(1, D), dy, dres)
    return dx, dg.reshape(D)


def _sigmoid(x):
    return 1.0 / (1.0 + jnp.exp(-x))


MXU_WIDTH = 256


def _col_chunks(n):
    return [(lo, min(lo + MXU_WIDTH, n)) for lo in range(0, n, MXU_WIDTH)]


def _mm_swiglu_fwd(hn, w_in, name, gather=None):
    T, D = hn.shape
    F = w_in.shape[1] // 2
    tm, tn, _ = _mm_tiles(T, F, D)
    assert (tm + 2 * tn) * D <= 4 * 1024 * 1024, "the whole contraction is held in one step"
    nf = F // tn

    def body(a_ref, bg_ref, bu_ref, g_ref, u_ref, act_ref):
        a = a_ref[...].astype(BF16)
        for lo, hi in _col_chunks(tn):
            gv = _dot(a, bg_ref[:, lo:hi].astype(BF16), NN)
            uv = _dot(a, bu_ref[:, lo:hi].astype(BF16), NN)
            g_ref[:, lo:hi] = gv.astype(BF16)
            u_ref[:, lo:hi] = uv.astype(BF16)
            act_ref[:, lo:hi] = (gv * _sigmoid(gv) * uv).astype(BF16)

    tile = pl.BlockSpec((tm, tn), lambda j, i: (i, j))
    out = jax.ShapeDtypeStruct((T, F), BF16)
    outs, gathered = _call(
        body, name=name, grid=(nf, T // tm),
        in_specs=[pl.BlockSpec((tm, D), lambda j, i: (i, 0)), pl.BlockSpec((D, tn), lambda j, i: (0, j)),
                  pl.BlockSpec((D, tn), lambda j, i: (0, j + nf))],
        out_specs=[tile, tile, tile], out_shape=[out, out, out], semantics=("parallel", "parallel"),
        args=(hn, w_in, w_in), gather=gather)
    return outs if gather is None else (outs, gathered)


def _mm_swiglu_bwd(dout, w_out, gate, up, name):
    T, D = dout.shape
    F = w_out.shape[0]
    tm, tn, _ = _mm_tiles(T, F, D)
    assert (tm + tn) * D <= 4 * 1024 * 1024, "the whole contraction is held in one step"

    def body(a_ref, b_ref, g_ref, u_ref, dg_ref, du_ref):
        a = a_ref[...].astype(BF16)
        for lo, hi in _col_chunks(tn):
            d = _dot(a, b_ref[lo:hi, :].astype(BF16), NT) * 0.5
            gv, uv = g_ref[:, lo:hi].astype(F32), u_ref[:, lo:hi].astype(F32)
            s = _sigmoid(gv)
            gs = gv * s
            dg_ref[:, lo:hi] = (d * uv * (s + gs - gs * s)).astype(BF16)
            du_ref[:, lo:hi] = (d * gs).astype(BF16)

    tile = pl.BlockSpec((tm, tn), lambda j, i: (i, j))
    out = jax.ShapeDtypeStruct((T, F), BF16)
    return pl.pallas_call(
        body, name=name, grid=(F // tn, T // tm),
        in_specs=[pl.BlockSpec((tm, D), lambda j, i: (i, 0)), pl.BlockSpec((tn, D), lambda j, i: (j, 0)), tile, tile],
        out_specs=[tile, tile], out_shape=[out, out], compiler_params=_params("parallel", "parallel"),
    )(dout, w_out, gate, up)


def _iota2(shape, dim):
    return lax.broadcasted_iota(jnp.int32, shape, dim)


def _head_ones():
    r, c = _iota2((LANES, LANES), 0), _iota2((LANES, LANES), 1)
    return ((r >> 6) == (c >> 6)).astype(F32)


def _rot_matrix(transpose):
    r, c = _iota2((LANES, LANES), 0), _iota2((LANES, LANES), 1)
    if transpose:
        r, c = c, r
    half = ROT_DIM // 2
    cm = c & (HEAD_DIM - 1)
    neg = (r == c + half) & (cm < half)
    pos = (r == c - half) & (cm >= half) & (cm < ROT_DIM)
    return pos.astype(F32) - neg.astype(F32)


def _rope_tables(positions, name):
    T = positions.size
    tm = _pick(T, 1024, 8)
    inv = ROPE_THETA ** (-jnp.arange(0, ROT_DIM, 2, dtype=F32) / ROT_DIM)
    lane = jnp.concatenate([inv, inv, jnp.zeros((HEAD_DIM - ROT_DIM,), F32)])
    lane = jnp.tile(lane, 2).reshape(1, LANES)

    def body(p_ref, f_ref, c_ref, s_ref):
        ang = p_ref[...].astype(F32) * f_ref[...]
        c_ref[...] = jnp.cos(ang)
        s_ref[...] = jnp.sin(ang)

    out = jax.ShapeDtypeStruct((T, LANES), F32)
    blk = pl.BlockSpec((tm, LANES), lambda i: (i, 0))
    return pl.pallas_call(
        body, name=name, grid=(T // tm,),
        in_specs=[pl.BlockSpec((tm, 1), lambda i: (i, 0)), pl.BlockSpec((1, LANES), lambda i: (0, 0))],
        out_specs=[blk, blk], out_shape=[out, out], compiler_params=_params("parallel"),
    )(positions.reshape(T, 1), lane)


def _qknorm_fwd(src, col_off, gain, cs, D, out_dtype, name, out_scale=1.0):
    T = src.shape[0]
    HP = D // LANES
    tm = _pick(T, 1024, 8)
    co = col_off // LANES
    rope = cs is not None

    def body(*refs):
        if rope:
            x_ref, g_ref, c_ref, s_ref, o_ref = refs
        else:
            x_ref, g_ref, o_ref = refs
        xv = x_ref[...]
        ms = _dot_once(xv * xv, _head_ones()) * (1.0 / HEAD_DIM)
        y = xv * lax.rsqrt(ms + EPS) * g_ref[...]
        if rope:
            y = y * c_ref[...] + _dot_once(y, _rot_matrix(False)) * s_ref[...]
        o_ref[...] = (y * out_scale).astype(out_dtype)

    in_specs = [pl.BlockSpec((tm, LANES), lambda i, h: (i, co + h)), pl.BlockSpec((1, LANES), lambda i, h: (0, 0))]
    args = [src, jnp.tile(gain.reshape(1, HEAD_DIM), (1, 2))]
    if rope:
        in_specs += [pl.BlockSpec((tm, LANES), lambda i, h: (i, 0))] * 2
        args += list(cs)
    return pl.pallas_call(
        body, name=name, grid=(T // tm, HP), in_specs=in_specs,
        out_specs=pl.BlockSpec((tm, LANES), lambda i, h: (i, h)),
        out_shape=jax.ShapeDtypeStruct((T, D), out_dtype), compiler_params=_params("parallel", "parallel"),
    )(*args)


def _qknorm_bwd(src, col_off, gain, cs, dout, D, name, out_scale=1.0):
    T = src.shape[0]
    HP = D // LANES
    tm = _pick(T, 1024, 8)
    co = col_off // LANES
    rope = cs is not None

    def body(*refs):
        if rope:
            x_ref, g_ref, d_ref, c_ref, s_ref, dx_ref, dg_ref = refs
        else:
            x_ref, g_ref, d_ref, dx_ref, dg_ref = refs
        xv = x_ref[...]
        ones = _head_ones()
        ms = _dot_once(xv * xv, ones) * (1.0 / HEAD_DIM)
        r = lax.rsqrt(ms + EPS)
        xn = xv * r
        d = d_ref[...] * out_scale
        if rope:
            d = d * c_ref[...] + _dot_once(d * s_ref[...], _rot_matrix(True))
        dgy = d * g_ref[...]
        mean = _dot_once(dgy * xn, ones) * (1.0 / HEAD_DIM)
        dx_ref[...] = (r * (dgy - xn * mean)).astype(BF16)

        @pl.when(pl.program_id(1) == 0)
        def _():
            dg_ref[...] = jnp.zeros_like(dg_ref)

        dg_ref[...] += jnp.sum(d * xn, axis=0, keepdims=True)[None]

    in_specs = [pl.BlockSpec((tm, LANES), lambda h, i: (i, co + h)), pl.BlockSpec((1, LANES), lambda h, i: (0, 0)),
                pl.BlockSpec((tm, LANES), lambda h, i: (i, h))]
    args = [src, jnp.tile(gain.reshape(1, HEAD_DIM), (1, 2)), dout]
    if rope:
        in_specs += [pl.BlockSpec((tm, LANES), lambda h, i: (i, 0))] * 2
        args += list(cs)
    dx, dg = pl.pallas_call(
        body, name=name, grid=(HP, T // tm), in_specs=in_specs,
        out_specs=[pl.BlockSpec((tm, LANES), lambda h, i: (i, h)), pl.BlockSpec((1, 1, LANES), lambda h, i: (h, 0, 0))],
        out_shape=[jax.ShapeDtypeStruct((T, D), BF16), jax.ShapeDtypeStruct((HP, 1, LANES), F32)],
        compiler_params=_params("parallel", "arbitrary"),
    )(*args)
    return dx, dg.reshape(2 * HP, HEAD_DIM).sum(axis=0)


def _head_rowsum(a, b, name):
    T, D = a.shape
    tm = _pick(T, 1024, 8)

    def body(a_ref, b_ref, o_ref):
        o_ref[...] = _dot_split(a_ref[...] * b_ref[...], _head_ones())

    blk = pl.BlockSpec((tm, LANES), lambda i, h: (i, h))
    return pl.pallas_call(
        body, name=name, grid=(T // tm, D // LANES), in_specs=[blk, blk], out_specs=blk,
        out_shape=jax.ShapeDtypeStruct((T, D), F32), compiler_params=_params("parallel", "parallel"),
    )(a, b)


def _lane_masks():
    lane = _iota2((1, LANES), 1)
    return (lane < HEAD_DIM, lane >= HEAD_DIM)


BAND_GROUP = 4


def _band_geometry(g, S):
    window, dil = DILATED_GROUPS[g]
    chunk = BAND_BLOCK * dil
    assert window // dil == BAND_BLOCK and S % chunk == 0
    if dil == 1:
        assert (S // chunk) % BAND_GROUP == 0
        return dil, chunk, BAND_GROUP, 1
    assert dil % BAND_GROUP == 0
    return dil, chunk, 1, dil // BAND_GROUP


def _band_rows(i, rho, dil, shift=0):
    if dil > 1:
        return pl.ds(rho * BAND_GROUP + i, BAND_BLOCK, stride=dil), shift != 0
    j = i + shift
    outside = j < 0 or j >= BAND_GROUP
    return pl.ds(0 if outside else j * BAND_BLOCK, BAND_BLOCK), outside


def _stack_heads(x):
    hm0, hm1 = _lane_masks()
    zero = jnp.zeros_like(x)
    return jnp.concatenate([jnp.where(hm0, x, zero), jnp.where(hm1, x, zero)], axis=0)


def _unstack_heads(x2):
    hm0, _ = _lane_masks()
    return jnp.where(hm0, x2[:BAND_BLOCK], x2[BAND_BLOCK:])


def _band_masks():
    qi = _iota2((2 * BAND_BLOCK, 2 * BAND_BLOCK), 0) & (BAND_BLOCK - 1)
    kj = _iota2((2 * BAND_BLOCK, 2 * BAND_BLOCK), 1)
    own = (kj >= BAND_BLOCK) & (kj - BAND_BLOCK <= qi)
    prev = (kj < BAND_BLOCK) & (kj >= qi)
    return own, prev


def _band_fwd(qh, kh, qkv, g, B, S, D, name, gather=None):
    dil, CH, NB, RG = _band_geometry(g, S)
    HP = D // LANES
    vo = (3 * g + 2) * HP
    nsp = S // (CH * NB)

    def body(q_ref, k_ref, kp_ref, v_ref, vp_ref, o_ref, lse_ref):
        n, rho = pl.program_id(1), pl.program_id(3)
        own, prev = _band_masks()
        for i in range(BAND_GROUP):
            rows, _ = _band_rows(i, rho, dil)
            prows, outside = _band_rows(i, rho, dil, shift=-1)
            q, kc, vc = (t[0, rows, :].astype(BF16) for t in (q_ref, k_ref, v_ref))
            kp, vp = (t[0, prows, :].astype(BF16) for t in ((kp_ref, vp_ref) if outside else (k_ref, v_ref)))
            valid = (own | (prev & (n > 0))) if outside else (own | prev)
            s = jnp.where(valid, _dot(_stack_heads(q), jnp.concatenate([kp, kc], axis=0), NT), NEG)
            m = jnp.max(s, axis=-1, keepdims=True)
            e = jnp.exp(s - m)
            l = jnp.sum(e, axis=-1, keepdims=True)
            o2 = _dot((e * (1.0 / l)).astype(BF16), jnp.concatenate([vp, vc], axis=0), NN)
            o_ref[0, rows, :] = _unstack_heads(o2)
            lse_ref[0, rows, :] = _unstack_heads(m + jnp.log(l))

    span, chunk = (1, CH * NB, LANES), (1, CH, LANES)
    cur = lambda b, n, h, r: (b, n, h)
    prev = lambda b, n, h, r: (b, jnp.maximum(n * NB - 1, 0), h)
    vcur = lambda b, n, h, r: (b, n, vo + h)
    vprev = lambda b, n, h, r: (b, jnp.maximum(n * NB - 1, 0), vo + h)
    out = jax.ShapeDtypeStruct((B, S, D), F32)
    q3, k3, v3 = qh.reshape(B, S, D), kh.reshape(B, S, D), qkv.reshape(B, S, 9 * D)
    (o, lse), gathered = _call(
        body, name=name, grid=(B, nsp, HP, RG),
        in_specs=[pl.BlockSpec(span, cur), pl.BlockSpec(span, cur), pl.BlockSpec(chunk, prev),
                  pl.BlockSpec(span, vcur), pl.BlockSpec(chunk, vprev)],
        out_specs=[pl.BlockSpec(span, cur), pl.BlockSpec(span, cur)], out_shape=[out, out],
        semantics=("parallel", "parallel", "parallel", "arbitrary"), args=(q3, k3, k3, v3, v3), gather=gather)
    o, lse = o.reshape(B * S, D), lse.reshape(B * S, D)
    return (o, lse) if gather is None else (o, lse, gathered)


def _head_col(x, h):
    lane = _iota2((1, LANES), 1)
    return jnp.sum(jnp.where(lane == h * HEAD_DIM, x, 0.0), axis=-1, keepdims=True)


def _band_bwd(qh, kh, qkv, do, lse, delta, g, B, S, D, name):
    dil, CH, NB, RG = _band_geometry(g, S)
    HP = D // LANES
    vo = (3 * g + 2) * HP
    nsp, nch = S // (CH * NB), S // CH

    def stacked_cols(tile):
        return jnp.concatenate([_head_col(tile, 0), _head_col(tile, 1)], axis=0)

    def body(q_ref, qn_ref, do_ref, don_ref, l_ref, ln_ref, d_ref, dn_ref, k_ref, kp_ref, v_ref, vp_ref,
             dq_ref, dk_ref, dv_ref):
        n, rho = pl.program_id(1), pl.program_id(3)
        own, prev = _band_masks()
        band = (_iota2((2 * BAND_BLOCK, BAND_BLOCK), 1) >= (_iota2((2 * BAND_BLOCK, BAND_BLOCK), 0) & (BAND_BLOCK - 1)))
        for i in range(BAND_GROUP):
            rows, _ = _band_rows(i, rho, dil)
            prows, p_out = _band_rows(i, rho, dil, shift=-1)
            nrows, n_out = _band_rows(i, rho, dil, shift=1)
            kc, vc = (t[0, rows, :].astype(BF16) for t in (k_ref, v_ref))
            kp, vp = (t[0, prows, :].astype(BF16) for t in ((kp_ref, vp_ref) if p_out else (k_ref, v_ref)))
            q2, do2 = (_stack_heads(t[0, rows, :].astype(BF16)) for t in (q_ref, do_ref))
            qn2, don2 = (_stack_heads(t[0, nrows, :].astype(BF16)) for t in ((qn_ref, don_ref) if n_out else (q_ref, do_ref)))
            lse2, del2 = (stacked_cols(t[0, rows, :]) for t in (l_ref, d_ref))
            lsen2, deln2 = (stacked_cols(t[0, nrows, :]) for t in ((ln_ref, dn_ref) if n_out else (l_ref, d_ref)))
            valid = (own | (prev & (n > 0))) if p_out else (own | prev)
            next_ok = (band & (n < nsp - 1)) if n_out else band
            k2, v2 = jnp.concatenate([kp, kc], axis=0), jnp.concatenate([vp, vc], axis=0)
            p = jnp.exp(jnp.where(valid, _dot(q2, k2, NT) - lse2, NEG))
            ds = (p * (_dot(do2, v2, NT) - del2)).astype(BF16)
            pc = jnp.exp(jnp.where(next_ok, _dot(qn2, kc, NT) - lsen2, NEG))
            dsc = (pc * (_dot(don2, vc, NT) - deln2)).astype(BF16)
            dq_ref[0, rows, :] = _unstack_heads(_dot(ds, k2, NN))
            dk_ref[0, rows, :] = _dot(ds[:, BAND_BLOCK:], q2, TN) + _dot(dsc, qn2, TN)
            dv_ref[0, rows, :] = _dot(p[:, BAND_BLOCK:].astype(BF16), do2, TN) + _dot(pc.astype(BF16), don2, TN)

    span, chunk = (1, CH * NB, LANES), (1, CH, LANES)
    cur = lambda b, n, h, r: (b, n, h)
    prev = lambda b, n, h, r: (b, jnp.maximum(n * NB - 1, 0), h)
    nxt = lambda b, n, h, r: (b, jnp.minimum((n + 1) * NB, nch - 1), h)
    vcur = lambda b, n, h, r: (b, n, vo + h)
    vprev = lambda b, n, h, r: (b, jnp.maximum(n * NB - 1, 0), vo + h)
    q3, k3, do3, l3, d3 = (t.reshape(B, S, D) for t in (qh, kh, do, lse, delta))
    v3 = qkv.reshape(B, S, 9 * D)
    sp, ch = (lambda m: pl.BlockSpec(span, m)), (lambda m: pl.BlockSpec(chunk, m))
    out = jax.ShapeDtypeStruct((B, S, D), F32)
    dq, dk, dv = pl.pallas_call(
        body, name=name, grid=(B, nsp, HP, RG),
        in_specs=[sp(cur), ch(nxt), sp(cur), ch(nxt), sp(cur), ch(nxt), sp(cur), ch(nxt),
                  sp(cur), ch(prev), sp(vcur), ch(vprev)],
        out_specs=[sp(cur)] * 3, out_shape=[out] * 3,
        compiler_params=_params("parallel", "parallel", "parallel", "arbitrary"),
    )(q3, q3, do3, do3, l3, l3, d3, d3, k3, k3, v3, v3)
    return dq.reshape(B * S, D), dk.reshape(B * S, D), dv.reshape(B * S, D)


def _mix(os_, lses, name):
    T, D = os_[0].shape
    tm = _pick(T, 256, 8)

    def body(o0, o1, o2, l0, l1, l2, mix_ref, lse_ref):
        a, b, c = l0[...], l1[...], l2[...]
        m = jnp.maximum(jnp.maximum(a, b), c)
        ea, eb, ec = jnp.exp(a - m), jnp.exp(b - m), jnp.exp(c - m)
        s = ea + eb + ec
        mix_ref[...] = (ea / s) * o0[...] + (eb / s) * o1[...] + (ec / s) * o2[...]
        lse_ref[...] = m + jnp.log(s)

    blk = pl.BlockSpec((tm, D), lambda i: (i, 0))
    out = jax.ShapeDtypeStruct((T, D), F32)
    return pl.pallas_call(
        body, name=name, grid=(T // tm,), in_specs=[blk] * 6, out_specs=[blk, blk], out_shape=[out, out],
        compiler_params=_params("parallel"),
    )(*os_, *lses)


def _gate_fwd(fpre, b_f, B, S, D, name):
    nb = S // LANES

    def body(f_ref, b_ref, cum_ref, rep_ref, carry):
        @pl.when(pl.program_id(1) == 0)
        def _():
            carry[...] = jnp.zeros_like(carry)

        z = f_ref[...] + b_ref[...]
        lf = jnp.minimum(z, 0.0) - jnp.log(1.0 + jnp.exp(-jnp.abs(z)))
        tri = (_iota2((LANES, LANES), 0) >= _iota2((LANES, LANES), 1)).astype(F32)
        c = _dot(tri, lf, NN, HIGHEST) + carry[...]
        cum_ref[...] = c
        carry[...] += jnp.sum(lf, axis=0, keepdims=True)
        spread = ((_iota2((LANES, D), 1) >> 6) == _iota2((LANES, D), 0)).astype(F32)
        rep_ref[...] = _dot(c, spread, NN, HIGHEST)

    return pl.pallas_call(
        body, name=name, grid=(B, nb),
        in_specs=[pl.BlockSpec((LANES, LANES), lambda b, j: (b * nb + j, 0)), pl.BlockSpec((1, LANES), lambda b, j: (0, 0))],
        out_specs=[pl.BlockSpec((LANES, LANES), lambda b, j: (b * nb + j, 0)), pl.BlockSpec((LANES, D), lambda b, j: (b * nb + j, 0))],
        out_shape=[jax.ShapeDtypeStruct((B * S, LANES), F32), jax.ShapeDtypeStruct((B * S, D), F32)],
        scratch_shapes=[pltpu.VMEM((1, LANES), F32)], compiler_params=_params("arbitrary", "arbitrary"),
    )(fpre, b_f)


def _gate_bwd(dcum_k, dcum_q, fpre, b_f, B, S, name):
    nb = S // LANES

    def body(d_ref, dq_ref, f_ref, b_ref, o_ref, db_ref, carry):
        first = (pl.program_id(0) == 0) & (pl.program_id(1) == 0)

        @pl.when(pl.program_id(1) == 0)
        def _():
            carry[...] = jnp.zeros_like(carry)

        @pl.when(first)
        def _():
            db_ref[...] = jnp.zeros_like(db_ref)

        d = d_ref[...] + dq_ref[...]
        tri = (_iota2((LANES, LANES), 0) <= _iota2((LANES, LANES), 1)).astype(F32)
        dlf = _dot(tri, d, NN, HIGHEST) + carry[...]
        carry[...] += jnp.sum(d, axis=0, keepdims=True)
        z = f_ref[...] + b_ref[...]
        dz = dlf * (1.0 / (1.0 + jnp.exp(z)))
        o_ref[...] = dz.astype(BF16)
        db_ref[...] += jnp.sum(dz, axis=0, keepdims=True)

    rev = lambda b, j: (b * nb + nb - 1 - j, 0)
    blk = pl.BlockSpec((LANES, LANES), rev)
    vec = pl.BlockSpec((1, LANES), lambda b, j: (0, 0))
    return pl.pallas_call(
        body, name=name, grid=(B, nb), in_specs=[blk, blk, blk, vec], out_specs=[blk, vec],
        out_shape=[jax.ShapeDtypeStruct((B * S, LANES), BF16), jax.ShapeDtypeStruct((1, LANES), F32)],
        scratch_shapes=[pltpu.VMEM((1, LANES), F32)], compiler_params=_params("arbitrary", "arbitrary"),
    )(dcum_k, dcum_q, fpre, b_f)


def _fox_scores(q, k, ck_row, hm, diagonal):
    TB = q.shape[0]
    qm = jnp.where(hm, q, jnp.zeros_like(q))
    t = _dot(qm, k, NT) - ck_row
    if diagonal:
        t = jnp.where(_iota2((TB, TB), 1) <= _iota2((TB, TB), 0), t, NEG)
    return qm, t


def _fox_fetch(srcs, bufs, sems, slot):
    return [pltpu.make_async_copy(src, buf.at[slot], sems.at[slot, i]) for i, (src, buf) in enumerate(zip(srcs, bufs))]


FOX_BLOCK = 256
EXP_UNDERFLOW = 104.0


def _fox_live_blocks(cum, gain_q, gain_k, B, S, D):
    H, HP = D // HEAD_DIM, D // LANES
    nb = S // FOX_BLOCK
    bound = 1.02 * HEAD_DIM * SCALE * jnp.max(jnp.abs(gain_q)) * jnp.max(jnp.abs(gain_k))
    c = cum.reshape(B, S, LANES)[:, :, :H]
    decay = c[:, 0::FOX_BLOCK, None, :] - c[:, None, FOX_BLOCK - 1::FOX_BLOCK, :]
    below = jnp.arange(nb)[None, :] < jnp.arange(nb)[:, None]
    dead = (decay < -(EXP_UNDERFLOW + 2.0 * bound)) & below[None, :, :, None]
    live = jnp.logical_not(dead.reshape(B, nb, nb, HP, 2).all(axis=-1))
    first = jnp.argmax(live, axis=2)
    last = nb - 1 - jnp.argmax(live[:, ::-1], axis=1)
    to_table = lambda t: t.transpose(0, 2, 1).reshape(B * HP * nb).astype(jnp.int32)
    return to_table(first), to_table(last)


def _fox_fwd(qh, kh, kv, cq, ck, first, B, S, D, name):
    HP = D // LANES
    TB = FOX_BLOCK
    nb = S // TB

    def body(first_ref, q_ref, cq_ref, k_ref, v_ref, ck_ref, kp_ref, vp_ref, ckp_ref, k_hbm, v_hbm, ck_hbm,
             o_ref, lse_ref, kbuf, vbuf, cbuf, sems, m_sc, l_sc, acc):
        b, qb, hp = pl.program_id(0), pl.program_id(1), pl.program_id(2)
        kb0 = first_ref[(b * HP + hp) * nb + qb]
        n_far = jnp.maximum(qb - 1 - kb0, 0)

        def fetch(kb, slot):
            rows = pl.ds(pl.multiple_of(kb * TB, TB), TB)
            srcs = (k_hbm.at[b, rows, pl.ds(pl.multiple_of(hp * LANES, LANES), LANES)],
                    v_hbm.at[b, rows, pl.ds(pl.multiple_of((HP + hp) * LANES, LANES), LANES)],
                    ck_hbm.at[b * HP + hp, :, rows])
            return _fox_fetch(srcs, (kbuf, vbuf, cbuf), sems, slot)

        @pl.when(n_far > 0)
        def _():
            for cp in fetch(kb0, 0):
                cp.start()

        m_sc[...] = jnp.full_like(m_sc, NEG)
        l_sc[...] = jnp.zeros_like(l_sc)
        acc[...] = jnp.zeros_like(acc)

        def step(k, v, ck_rows, diagonal):
            q, v = q_ref[0], v.astype(BF16)
            cqv = cq_ref[0]
            a = acc[...]
            for h, hm in enumerate(_lane_masks()):
                _, t = _fox_scores(q, k, ck_rows[h], hm, diagonal)
                cqh = _head_col(cqv, h)
                m_prev = jnp.max(m_sc[h], axis=-1, keepdims=True)
                l_prev = jnp.max(l_sc[h], axis=-1, keepdims=True)
                m_new = jnp.maximum(m_prev, jnp.max(t, axis=-1, keepdims=True) + cqh)
                alpha = jnp.exp(m_prev - m_new)
                p = jnp.exp(t + (cqh - m_new))
                l_new = alpha * l_prev + jnp.sum(p, axis=-1, keepdims=True)
                a = jnp.where(hm, alpha * a + _dot(p.astype(BF16), v, NN), a)
                m_sc[h] = jnp.broadcast_to(m_new, (TB, LANES))
                l_sc[h] = jnp.broadcast_to(l_new, (TB, LANES))
            acc[...] = a

        def walk(i, carry):
            slot = lax.rem(i, 2)

            @pl.when(i + 1 < n_far)
            def _():
                for cp in fetch(kb0 + i + 1, 1 - slot):
                    cp.start()

            for cp in fetch(kb0 + i, slot):
                cp.wait()
            step(kbuf[slot], vbuf[slot], [cbuf[slot, pl.ds(h, 1), :] for h in range(2)], False)
            return carry

        lax.fori_loop(0, n_far, walk, 0)
        pl.when(kb0 < qb)(lambda: step(kp_ref[0], vp_ref[0], [ckp_ref[0, h:h + 1, :] for h in range(2)], False))
        step(k_ref[0], v_ref[0], [ck_ref[0, h:h + 1, :] for h in range(2)], True)
        a = acc[...]
        out = a
        lse = jnp.zeros_like(a)
        for h, hm in enumerate(_lane_masks()):
            out = jnp.where(hm, a / l_sc[h], out)
            lse = jnp.where(hm, m_sc[h] + jnp.log(l_sc[h]), lse)
        o_ref[0] = out
        lse_ref[0] = lse

    out = jax.ShapeDtypeStruct((B, S, D), F32)
    spec = pltpu.PrefetchScalarGridSpec(
        num_scalar_prefetch=1, grid=(B, nb, HP),
        in_specs=_fox_q_specs(2, TB) + _fox_k_specs(TB, HP, 0) + _fox_k_specs(TB, HP, -1) + [_ANY] * 3,
        out_specs=_fox_q_specs(2, TB), scratch_shapes=_fox_k_buffers(TB) + [
            pltpu.VMEM((2, TB, LANES), F32), pltpu.VMEM((2, TB, LANES), F32), pltpu.VMEM((TB, LANES), F32)])
    k3, v3 = kh.reshape(B, S, D), kv.reshape(B, S, 2 * D)
    o, lse = pl.pallas_call(
        body, name=name, grid_spec=spec, out_shape=[out, out], compiler_params=_params("parallel", "parallel", "parallel"),
    )(first, qh.reshape(B, S, D), cq.reshape(B, S, D), k3, v3, ck, k3, v3, ck, k3, v3, ck)
    return o.reshape(B * S, D), lse.reshape(B * S, D)


_ANY = pl.BlockSpec(memory_space=pl.ANY)


def _fox_q_specs(n, TB):
    return [pl.BlockSpec((1, TB, LANES), lambda b, i, h, table: (b, i, h))] * n


def _fox_k_specs(TB, HP, shift):
    blk = lambda i: jnp.maximum(i + shift, 0)
    return [pl.BlockSpec((1, TB, LANES), lambda b, i, h, table: (b, blk(i), h)),
            pl.BlockSpec((1, TB, LANES), lambda b, i, h, table: (b, blk(i), HP + h)),
            pl.BlockSpec((1, 8, TB), lambda b, i, h, table: (b * HP + h, 0, blk(i)))]


def _fox_k_buffers(TB):
    return [pltpu.VMEM((2, TB, LANES), BF16), pltpu.VMEM((2, TB, LANES), F32), pltpu.VMEM((2, 8, TB), F32),
            pltpu.SemaphoreType.DMA((2, 3))]


def _fox_bwd_dq(qh, kh, kv, cq, ck, first, do, lse, delta, B, S, D, name):
    HP = D // LANES
    TB = FOX_BLOCK
    nb = S // TB

    def body(first_ref, q_ref, cq_ref, do_ref, l_ref, d_ref, k_ref, v_ref, ck_ref, kp_ref, vp_ref, ckp_ref,
             k_hbm, v_hbm, ck_hbm, dq_ref, dc_ref, kbuf, vbuf, cbuf, sems, acc, acc_c):
        b, qb, hp = pl.program_id(0), pl.program_id(1), pl.program_id(2)
        kb0 = first_ref[(b * HP + hp) * nb + qb]
        n_far = jnp.maximum(qb - 1 - kb0, 0)

        def fetch(kb, slot):
            rows = pl.ds(pl.multiple_of(kb * TB, TB), TB)
            srcs = (k_hbm.at[b, rows, pl.ds(pl.multiple_of(hp * LANES, LANES), LANES)],
                    v_hbm.at[b, rows, pl.ds(pl.multiple_of((HP + hp) * LANES, LANES), LANES)],
                    ck_hbm.at[b * HP + hp, :, rows])
            return _fox_fetch(srcs, (kbuf, vbuf, cbuf), sems, slot)

        @pl.when(n_far > 0)
        def _():
            for cp in fetch(kb0, 0):
                cp.start()

        acc[...] = jnp.zeros_like(acc)
        acc_c[...] = jnp.zeros_like(acc_c)

        def step(k, v, ck_rows, diagonal):
            q, v = q_ref[0], v.astype(BF16)
            dout = do_ref[0].astype(BF16)
            a, c = acc[...], acc_c[...]
            for h, hm in enumerate(_lane_masks()):
                _, t = _fox_scores(q, k, ck_rows[h], hm, diagonal)
                p = jnp.exp(t + (_head_col(cq_ref[0], h) - _head_col(l_ref[0], h)))
                dom = jnp.where(hm, dout, jnp.zeros_like(dout))
                ds = p * (_dot(dom, v, NT) - _head_col(d_ref[0], h))
                a += _dot(ds.astype(BF16), jnp.where(hm, k, jnp.zeros_like(k)), NN)
                c = jnp.where(hm, c + jnp.sum(ds, axis=-1, keepdims=True), c)
            acc[...] = a
            acc_c[...] = c

        def walk(i, carry):
            slot = lax.rem(i, 2)

            @pl.when(i + 1 < n_far)
            def _():
                for cp in fetch(kb0 + i + 1, 1 - slot):
                    cp.start()

            for cp in fetch(kb0 + i, slot):
                cp.wait()
            step(kbuf[slot], vbuf[slot], [cbuf[slot, pl.ds(h, 1), :] for h in range(2)], False)
            return carry

        lax.fori_loop(0, n_far, walk, 0)
        pl.when(kb0 < qb)(lambda: step(kp_ref[0], vp_ref[0], [ckp_ref[0, h:h + 1, :] for h in range(2)], False))
        step(k_ref[0], v_ref[0], [ck_ref[0, h:h + 1, :] for h in range(2)], True)
        dq_ref[0] = acc[...]
        dc_ref[0] = acc_c[...]

    r3 = lambda t: t.reshape(B, S, D)
    out = jax.ShapeDtypeStruct((B, S, D), F32)
    spec = pltpu.PrefetchScalarGridSpec(
        num_scalar_prefetch=1, grid=(B, nb, HP),
        in_specs=_fox_q_specs(5, TB) + _fox_k_specs(TB, HP, 0) + _fox_k_specs(TB, HP, -1) + [_ANY] * 3,
        out_specs=_fox_q_specs(2, TB),
        scratch_shapes=_fox_k_buffers(TB) + [pltpu.VMEM((TB, LANES), F32), pltpu.VMEM((TB, LANES), F32)])
    k3, v3 = r3(kh), kv.reshape(B, S, 2 * D)
    dq, dc = pl.pallas_call(
        body, name=name, grid_spec=spec, out_shape=[out, out], compiler_params=_params("parallel", "parallel", "parallel"),
    )(first, r3(qh), r3(cq), r3(do), r3(lse), r3(delta), k3, v3, ck, k3, v3, ck, k3, v3, ck)
    return dq.reshape(B * S, D), dc.reshape(B * S, D)


def _fox_bwd_dkv(qh, kh, kv, cq, ck, last, do, lse, delta, B, S, D, name):
    HP = D // LANES
    TB = FOX_BLOCK
    nb = S // TB

    def body(last_ref, k_ref, v_ref, ck_ref, q_ref, cq_ref, do_ref, l_ref, d_ref, qn_ref, cqn_ref, don_ref, ln_ref,
             dn_ref, q_hbm, cq_hbm, do_hbm, l_hbm, d_hbm, dk_ref, dv_ref, dc_ref,
             qbuf, cqbuf, dobuf, lbuf, dbuf, sems, dk_acc, dv_acc, dc_acc):
        b, kb, hp = pl.program_id(0), pl.program_id(1), pl.program_id(2)
        qb_last = last_ref[(b * HP + hp) * nb + kb]
        n_far = jnp.maximum(qb_last - kb - 1, 0)

        def fetch(qb, slot):
            rows = pl.ds(pl.multiple_of(qb * TB, TB), TB)
            cols = pl.ds(pl.multiple_of(hp * LANES, LANES), LANES)
            srcs = [t.at[b, rows, cols] for t in (q_hbm, cq_hbm, do_hbm, l_hbm, d_hbm)]
            return _fox_fetch(srcs, (qbuf, cqbuf, dobuf, lbuf, dbuf), sems, slot)

        @pl.when(n_far > 0)
        def _():
            for cp in fetch(kb + 2, 0):
                cp.start()

        dk_acc[...] = jnp.zeros_like(dk_acc)
        dv_acc[...] = jnp.zeros_like(dv_acc)
        dc_acc[...] = jnp.zeros_like(dc_acc)

        def step(q, cqv, dout, lse_t, del_t, diagonal):
            k, v = k_ref[0], v_ref[0].astype(BF16)
            dout = dout.astype(BF16)
            dk, dv, dc = dk_acc[...], dv_acc[...], dc_acc[...]
            row = _iota2((8, TB), 0)
            for h, hm in enumerate(_lane_masks()):
                qm, t = _fox_scores(q, k, ck_ref[0, h:h + 1, :], hm, diagonal)
                p = jnp.exp(t + (_head_col(cqv, h) - _head_col(lse_t, h)))
                dom = jnp.where(hm, dout, jnp.zeros_like(dout))
                ds = p * (_dot(dom, v, NT) - _head_col(del_t, h))
                dv += _dot(p.astype(BF16), dom, TN)
                dk += _dot(ds.astype(BF16), qm, TN)
                dc = jnp.where(row == h, dc - jnp.sum(ds, axis=0, keepdims=True), dc)
            dk_acc[...] = dk
            dv_acc[...] = dv
            dc_acc[...] = dc

        step(q_ref[0], cq_ref[0], do_ref[0], l_ref[0], d_ref[0], True)
        pl.when(kb < qb_last)(lambda: step(qn_ref[0], cqn_ref[0], don_ref[0], ln_ref[0], dn_ref[0], False))

        def walk(i, carry):
            slot = lax.rem(i, 2)

            @pl.when(i + 1 < n_far)
            def _():
                for cp in fetch(kb + 3 + i, 1 - slot):
                    cp.start()

            for cp in fetch(kb + 2 + i, slot):
                cp.wait()
            step(qbuf[slot], cqbuf[slot], dobuf[slot], lbuf[slot], dbuf[slot], False)
            return carry

        lax.fori_loop(0, n_far, walk, 0)
        dk_ref[0] = dk_acc[...]
        dv_ref[0] = dv_acc[...].astype(BF16)
        dc_ref[0] = dc_acc[...]

    blk = (1, TB, LANES)
    kmap = lambda b, i, h, last: (b, i, h)
    vmap = lambda b, i, h, last: (b, i, HP + h)
    cmap = lambda b, i, h, last: (b * HP + h, 0, i)
    nmap = lambda b, i, h, last: (b, jnp.minimum(i + 1, nb - 1), h)
    r3 = lambda t: t.reshape(B, S, D)
    tile = lambda dt: pltpu.VMEM((2, TB, LANES), dt)
    qside = [r3(qh), r3(cq), r3(do), r3(lse), r3(delta)]
    spec = pltpu.PrefetchScalarGridSpec(
        num_scalar_prefetch=1, grid=(B, nb, HP),
        in_specs=[pl.BlockSpec(blk, kmap), pl.BlockSpec(blk, vmap), pl.BlockSpec((1, 8, TB), cmap)]
        + [pl.BlockSpec(blk, kmap)] * 5 + [pl.BlockSpec(blk, nmap)] * 5 + [_ANY] * 5,
        out_specs=[pl.BlockSpec(blk, kmap), pl.BlockSpec(blk, kmap), pl.BlockSpec((1, 8, TB), cmap)],
        scratch_shapes=[tile(BF16), tile(F32), tile(F32), tile(F32), tile(F32), pltpu.SemaphoreType.DMA((2, 5)),
                        pltpu.VMEM((TB, LANES), F32), pltpu.VMEM((TB, LANES), F32), pltpu.VMEM((8, TB), F32)])
    dk, dv, dc = pl.pallas_call(
        body, name=name, grid_spec=spec,
        out_shape=[jax.ShapeDtypeStruct((B, S, D), F32), jax.ShapeDtypeStruct((B, S, D), BF16),
                   jax.ShapeDtypeStruct((B * HP, 8, S), F32)],
        compiler_params=_params("parallel", "parallel", "parallel"),
    )(last, r3(kh), kv.reshape(B, S, 2 * D), ck, *qside, *qside, *qside)
    return dk.reshape(B * S, D), dv.reshape(B * S, D), dc


def _loss_and_grad(y, target, name):
    T, D = y.shape
    tm = _pick(T, 256, 8)

    def body(y_ref, t_ref, p_ref, dy_ref):
        e = y_ref[...] - t_ref[...]
        dy_ref[...] = e * (1.0 / D)
        p_ref[...] = jnp.sum(e * e, axis=0, keepdims=True)[None]

    blk = pl.BlockSpec((tm, D), lambda i: (i, 0))
    return pl.pallas_call(
        body, name=name, grid=(T // tm,), in_specs=[blk, blk],
        out_specs=[pl.BlockSpec((1, 1, D), lambda i: (i, 0, 0)), blk],
        out_shape=[jax.ShapeDtypeStruct((T // tm, 1, D), F32), jax.ShapeDtypeStruct((T, D), F32)],
        compiler_params=_params("parallel"),
    )(y, target)


def _pair_sum(x, got, name):
    _, R, C = x.shape
    tr = _pick(R, max(16, (8 * PACK_ROWS * LANES) // C), 16)

    def body(c_ref, x_ref, g_ref, o_ref):
        o_ref[0] = (x_ref[0].astype(F32) + g_ref[0].astype(F32)).astype(o_ref.dtype)

    spec = pltpu.PrefetchScalarGridSpec(
        num_scalar_prefetch=1, grid=(4, R // tr),
        in_specs=[pl.BlockSpec((1, tr, C), lambda a, i, c: (2 * a + c[0], i, 0)),
                  pl.BlockSpec((1, tr, C), lambda a, i, c: (a, i, 0))],
        out_specs=pl.BlockSpec((1, tr, C), lambda a, i, c: (a, i, 0)))
    return pl.pallas_call(
        body, name=name, grid_spec=spec, out_shape=jax.ShapeDtypeStruct((4, R, C), x.dtype),
        compiler_params=_params("parallel", "parallel"),
    )(lax.axis_index("c").astype(jnp.int32).reshape(1), x, got)


def _adamw(parts, w, m, v, name):
    R, C = w.shape
    P = parts.shape[0]
    tr = _pick(R, max(16, (PACK_ROWS * LANES) // C), 16)

    def body(p_ref, w_ref, m_ref, v_ref, g_out, d_out, m_out, v_out):
        g = p_ref[0].astype(F32)
        for s in range(1, P):
            g = g + p_ref[s].astype(F32)
        mn = ADAM_B1 * m_ref[...] + (1.0 - ADAM_B1) * g
        vn = ADAM_B2 * v_ref[...] + (1.0 - ADAM_B2) * (g * g)
        m_hat = mn / (1.0 - ADAM_B1 ** ADAM_STEP)
        v_hat = vn / (1.0 - ADAM_B2 ** ADAM_STEP)
        g_out[...] = g
        d_out[...] = -ADAM_LR * (m_hat / (jnp.sqrt(v_hat) + ADAM_EPS) + ADAM_WD * w_ref[...])
        m_out[...] = mn
        v_out[...] = vn

    blk = pl.BlockSpec((tr, C), lambda i: (i, 0))
    out = jax.ShapeDtypeStruct((R, C), F32)
    return pl.pallas_call(
        body, name=name, grid=(R // tr,),
        in_specs=[pl.BlockSpec((P, tr, C), lambda i: (0, i, 0)), blk, blk, blk],
        out_specs=[blk] * 4, out_shape=[out] * 4, compiler_params=_params("parallel"),
    )(parts, w, m, v)


def _mesh_pos():
    return lax.axis_index("x"), lax.axis_index("y"), lax.axis_index("c")


def _comm_call(body, xs, out_shapes, name):
    n = len(xs)
    any_spec = pl.BlockSpec(memory_space=pl.ANY)
    return pl.pallas_call(
        body, name=name, out_shape=out_shapes, in_specs=[any_spec] * n, out_specs=[any_spec] * n,
        scratch_shapes=[pltpu.SemaphoreType.DMA((7, n)), pltpu.SemaphoreType.DMA((7, n)), pltpu.SemaphoreType.DMA((n,))],
    )(*xs)


def _all_gather(xs, name):
    n = len(xs)

    def body(*refs):
        start, finish = _gather_phases(refs[:n], refs[n:2 * n], *refs[2 * n:])
        start()
        finish()

    return _comm_call(body, xs, _gathered_shapes(xs), name)


def _gathered_shapes(xs):
    return [jax.ShapeDtypeStruct((N_DEV,) + x.shape, x.dtype) for x in xs]


def _gather_phases(x_refs, out_refs, send_sems, recv_sems, local_sems):
    n = len(x_refs)
    x_, y_, c_ = _mesh_pos()
    me, sibling = (x_, y_, c_), (x_, y_, 1 - c_)
    chips = [(1 - x_, y_), (x_, 1 - y_), (1 - x_, 1 - y_)]

    def slot(t, px, py, pc):
        return out_refs[t].at[4 * px + 2 * py + pc]

    def copy(t, k, block, to, src=None):
        return pltpu.make_async_remote_copy(
            src_ref=slot(t, *block) if src is None else src, dst_ref=slot(t, *block),
            send_sem=send_sems.at[k, t], recv_sem=recv_sems.at[k, t], device_id=to, device_id_type=MESH_ID)

    def own_copies():
        mine = [pltpu.make_async_copy(x_refs[t], slot(t, *me), local_sems.at[t]) for t in range(n)]
        first = [copy(t, 0, me, sibling, src=x_refs[t]) for t in range(n)]
        first += [copy(t, 1 + j, me, (*chip, c_), src=x_refs[t]) for j, chip in enumerate(chips) for t in range(n)]
        return mine, first

    def start():
        mine, first = own_copies()
        for cp in mine + first:
            cp.start()

    def finish():
        mine, first = own_copies()
        passed = []
        for j, chip in enumerate(chips):
            for t in range(n):
                copy(t, 1 + j, (*chip, c_), me).wait_recv()
                passed.append(copy(t, 4 + j, (*chip, c_), sibling))
                passed[-1].start()
        for t in range(n):
            copy(t, 0, sibling, me).wait_recv()
        for j, chip in enumerate(chips):
            for t in range(n):
                copy(t, 4 + j, (*chip, 1 - c_), me).wait_recv()
        for cp in first + passed:
            cp.wait_send()
        for cp in mine:
            cp.wait()

    return start, finish


def _call(body, *, name, grid, in_specs, out_specs, out_shape, scratch_shapes=(), semantics, args, gather=None):
    if gather is None:
        return pl.pallas_call(
            body, name=name, grid=grid, in_specs=in_specs, out_specs=out_specs, out_shape=out_shape,
            scratch_shapes=list(scratch_shapes), compiler_params=_params(*semantics))(*args), None
    n, n_in, n_sc = len(gather), len(in_specs), len(scratch_shapes)
    outs = list(out_shape) if isinstance(out_shape, (list, tuple)) else [out_shape]
    specs = list(out_specs) if isinstance(out_specs, (list, tuple)) else [out_specs]
    n_out = len(outs)

    def carried(*refs):
        ins, xs = refs[:n_in], refs[n_in:n_in + n]
        ys, gs = refs[n_in + n:n_in + n + n_out], refs[n_in + n + n_out:n_in + 2 * n + n_out]
        scratch, sems = refs[n_in + 2 * n + n_out:n_in + 2 * n + n_out + n_sc], refs[n_in + 2 * n + n_out + n_sc:]
        start, finish = _gather_phases(xs, gs, *sems)
        at_first = functools.reduce(jnp.logical_and, [pl.program_id(a) == 0 for a in range(len(grid))])
        at_last = functools.reduce(jnp.logical_and, [pl.program_id(a) == grid[a] - 1 for a in range(len(grid))])
        pl.when(at_first)(start)
        body(*ins, *ys, *scratch)
        pl.when(at_last)(finish)

    results = pl.pallas_call(
        carried, name=name, grid=grid, in_specs=list(in_specs) + [_ANY] * n, out_specs=specs + [_ANY] * n,
        out_shape=outs + _gathered_shapes(gather),
        scratch_shapes=list(scratch_shapes) + [pltpu.SemaphoreType.DMA((7, n)), pltpu.SemaphoreType.DMA((7, n)),
                                               pltpu.SemaphoreType.DMA((n,))],
        compiler_params=_params(*(["arbitrary"] * len(grid))))(*args, *gather)
    main = results[:n_out]
    return (main if isinstance(out_shape, (list, tuple)) else main[0]), list(results[n_out:])


def _pair_exchange(xs, name):
    n = len(xs)

    def body(*refs):
        x_refs, out_refs = refs[:n], refs[n:2 * n]
        send_sems, recv_sems, _ = refs[2 * n:]
        x_, y_, c_ = _mesh_pos()
        copies = [pltpu.make_async_remote_copy(
            src_ref=x_refs[t].at[2 * a + 1 - c_], dst_ref=out_refs[t].at[a],
            send_sem=send_sems.at[a, t], recv_sem=recv_sems.at[a, t],
            device_id=(x_, y_, 1 - c_), device_id_type=MESH_ID) for a in range(4) for t in range(n)]
        for cp in copies:
            cp.start()
        for cp in copies:
            cp.wait_recv()
        for cp in copies:
            cp.wait_send()

    return _comm_call(body, xs, [jax.ShapeDtypeStruct((4,) + x.shape[1:], x.dtype) for x in xs], name)


def _chip_exchange(xs, name):
    n = len(xs)

    def body(*refs):
        x_refs, out_refs = refs[:n], refs[n:2 * n]
        send_sems, recv_sems, local_sems = refs[2 * n:]
        x_, y_, c_ = _mesh_pos()
        chip = 2 * x_ + y_
        mine = [pltpu.make_async_copy(x_refs[t].at[chip], out_refs[t].at[chip], local_sems.at[t]) for t in range(n)]
        copies = []
        for k in range(1, 4):
            px, py = x_ ^ (k >> 1), y_ ^ (k & 1)
            for t in range(n):
                copies.append(pltpu.make_async_remote_copy(
                    src_ref=x_refs[t].at[2 * px + py], dst_ref=out_refs[t].at[chip],
                    send_sem=send_sems.at[k - 1, t], recv_sem=recv_sems.at[k - 1, t],
                    device_id=(px, py, c_), device_id_type=MESH_ID))
        for cp in mine + copies:
            cp.start()
        for cp in copies:
            cp.wait_recv()
        for cp in copies:
            cp.wait_send()
        for cp in mine:
            cp.wait()

    return _comm_call(body, xs, [jax.ShapeDtypeStruct(x.shape, x.dtype) for x in xs], name)


def _ffn_fwd(h, norm, w_in, w_out, tag, gather=None):
    T, D = h.shape
    F = w_out.shape[0]
    hn = _rmsnorm_fwd(h, norm, f"{tag}_norm")
    got = None
    if gather is None:
        gate, up, act = _mm_swiglu_fwd(hn, w_in, f"{tag}_in")
    else:
        (gate, up, act), got = _mm_swiglu_fwd(hn, w_in, f"{tag}_in", gather=gather)
    out = _mm(act, w_out, "nn", M=T, N=D, K=F, res=h, scale=0.5, name=f"{tag}_out")
    return out, (h, hn, gate, up, act), got


def _ffn_bwd(dout, saved, norm, w_in, w_out, tag):
    h, hn, gate, up, act = saved
    T, D = h.shape
    F = w_out.shape[0]
    dg, du = _mm_swiglu_bwd(dout, w_out, gate, up, f"{tag}_dact")
    dw_out = _mm(act, dout, "tn", M=F, N=D, K=T, scale=0.5, out_dtype=BF16, name=f"{tag}_dwout")
    dhn = _mm(dg, w_in, "nt", M=T, N=D, K=2 * F, a_rest=(du,), name=f"{tag}_dhn")
    dw_in = _mm(hn, dg, "tn", M=D, N=2 * F, K=T, b2=du, out_dtype=BF16, name=f"{tag}_dwin")
    dh, dnorm = _rmsnorm_bwd(h, norm, dhn, dout, f"{tag}_dnorm")
    return dh, dnorm, dw_in, dw_out


def _local_step(x, positions, target, W, pending=None):
    B, S, D = x.shape
    T = B * S
    H, HP = D // HEAD_DIM, D // LANES
    G = {}
    pending = pending or {}
    sending = lambda call: pending[call][0] if call in pending else None

    def arrived(call, got):
        if call in pending:
            pending[call][1](W, got)

    h0 = x.reshape(T, D)
    cs = _rope_tables(positions, "rope_tables")

    h1, ffn00, got = _ffn_fwd(h0, W["ffn_norm"][0, 0], W["ffn_w_in"][0, 0], W["ffn_w_out"][0, 0], "ffn00",
                              gather=sending("ffn00_in"))
    arrived("ffn00_in", got)
    hna = _rmsnorm_fwd(h1, W["mix_norm"][0], "mixa_norm")
    qkv = _mm(hna, W["a_w_qkv"], "nn", M=T, N=9 * D, K=D, name="a_qkv", gather=sending("a_qkv"))
    if sending("a_qkv") is not None:
        qkv, got = qkv
        arrived("a_qkv", got)
    qs, ks, os_, lses = [], [], [], []
    for g in range(3):
        qs.append(_qknorm_fwd(qkv, 3 * g * D, W["a_q_norm"][0, g], cs, D, F32, f"a_qnorm{g}", out_scale=SCALE))
        ks.append(_qknorm_fwd(qkv, (3 * g + 1) * D, W["a_k_norm"][0, g], cs, D, F32, f"a_knorm{g}"))
        o, l, *got = _band_fwd(qs[g], ks[g], qkv, g, B, S, D, f"a_band{g}", gather=sending(f"a_band{g}"))
        arrived(f"a_band{g}", got[0] if got else None)
        os_.append(o)
        lses.append(l)
    mixed, lse_a = _mix(os_, lses, "a_mix")
    h2 = _mm(mixed, W["a_w_o"], "nn", M=T, N=D, K=D, res=h1, name="a_out")
    h3, ffn01, _ = _ffn_fwd(h2, W["ffn_norm"][0, 1], W["ffn_w_in"][0, 1], W["ffn_w_out"][0, 1], "ffn01")

    hnkv = _rmsnorm_fwd(h3, W["kv_norm"], "kv_norm")
    kv = _mm(hnkv, W["kv_w_kv"], "nn", M=T, N=2 * D, K=D, name="kv_proj")
    fpre = _mm(hnkv, W["kv_w_f"], "nn", M=T, N=LANES, K=D, name="kv_gate_proj")
    b_f = jnp.pad(W["kv_b_f"], (0, LANES - H)).reshape(1, LANES)
    kh = _qknorm_fwd(kv, 0, W["kv_k_norm"], None, D, BF16, "kv_knorm")
    cum, cq = _gate_fwd(fpre, b_f, B, S, D, "kv_gate")
    ck = cum.reshape(B, S, LANES)[:, :, :H].reshape(B, S, HP, 2).transpose(0, 2, 3, 1)
    ck = jnp.pad(ck, ((0, 0), (0, 0), (0, 6), (0, 0))).reshape(B * HP, 8, S)
    h4, ffn10, _ = _ffn_fwd(h3, W["ffn_norm"][1, 0], W["ffn_w_in"][1, 0], W["ffn_w_out"][1, 0], "ffn10")
    hnb = _rmsnorm_fwd(h4, W["mix_norm"][1], "mixb_norm")
    qraw = _mm(hnb, W["b_w_q"], "nn", M=T, N=D, K=D, name="b_q")
    qh = _qknorm_fwd(qraw, 0, W["b_q_norm"][0], None, D, BF16, "b_qnorm", out_scale=SCALE)
    first, last = _fox_live_blocks(cum, W["b_q_norm"][0], W["kv_k_norm"], B, S, D)
    ob, lse_b = _fox_fwd(qh, kh, kv, cq, ck, first, B, S, D, "b_fox")
    h5 = _mm(ob, W["b_w_o"], "nn", M=T, N=D, K=D, res=h4, name="b_out")
    h6, ffn11, _ = _ffn_fwd(h5, W["ffn_norm"][1, 1], W["ffn_w_in"][1, 1], W["ffn_w_out"][1, 1], "ffn11")

    sq, dy = _loss_and_grad(h6, target.reshape(T, D), "loss")

    dn = [[None, None], [None, None]]
    dwi = [[None, None], [None, None]]
    dwo = [[None, None], [None, None]]
    dh5, dn[1][1], dwi[1][1], dwo[1][1] = _ffn_bwd(dy, ffn11, W["ffn_norm"][1, 1], W["ffn_w_in"][1, 1], W["ffn_w_out"][1, 1], "ffn11")
    dob = _mm(dh5, W["b_w_o"], "nt", M=T, N=D, K=D, name="b_dout")
    G["b_w_o"] = _mm(ob, dh5, "tn", M=D, N=D, K=T, out_dtype=BF16, name="b_dwo")[None]
    delta_b = _head_rowsum(dob, ob, "b_delta")
    dqh, dcq = _fox_bwd_dq(qh, kh, kv, cq, ck, first, dob, lse_b, delta_b, B, S, D, "b_fox_dq")
    dkh, dv_b, dck = _fox_bwd_dkv(qh, kh, kv, cq, ck, last, dob, lse_b, delta_b, B, S, D, "b_fox_dkv")
    dqraw, dg = _qknorm_bwd(qraw, 0, W["b_q_norm"][0], None, dqh, D, "b_dqnorm", out_scale=SCALE)
    G["b_q_norm"] = dg[None]
    dhnb = _mm(dqraw, W["b_w_q"], "nt", M=T, N=D, K=D, name="b_dhn")
    G["b_w_q"] = _mm(hnb, dqraw, "tn", M=D, N=D, K=T, out_dtype=BF16, name="b_dwq")[None]
    dh4, dmix_b = _rmsnorm_bwd(h4, W["mix_norm"][1], dhnb, dh5, "mixb_dnorm")
    dh3, dn[1][0], dwi[1][0], dwo[1][0] = _ffn_bwd(dh4, ffn10, W["ffn_norm"][1, 0], W["ffn_w_in"][1, 0], W["ffn_w_out"][1, 0], "ffn10")

    dkraw, G["kv_k_norm"] = _qknorm_bwd(kv, 0, W["kv_k_norm"], None, dkh, D, "kv_dknorm")
    dcum = dck.reshape(B, HP, 8, S)[:, :, :2].transpose(0, 3, 1, 2).reshape(T, H)
    dcum = jnp.pad(dcum, ((0, 0), (0, LANES - H)))
    dcum_q = jnp.pad(dcq.reshape(T, H, HEAD_DIM)[:, :, 0], ((0, 0), (0, LANES - H)))
    dfpre, db_f = _gate_bwd(dcum, dcum_q, fpre, b_f, B, S, "kv_dgate")
    G["kv_b_f"] = db_f[0, :H]
    dhnkv = _mm(dkraw, W["kv_w_kv"], "nt", M=T, N=D, K=D, name="kv_dhn_k")
    dhnkv = _mm(dv_b, W["kv_w_kv"], "nt", M=T, N=D, K=D, b_off=(0, D), res=dhnkv, name="kv_dhn_v")
    dhnkv = _mm(dfpre, W["kv_w_f"], "nt", M=T, N=D, K=LANES, res=dhnkv, name="kv_dhn_f")
    dw_k = _mm(hnkv, dkraw, "tn", M=D, N=D, K=T, out_dtype=BF16, name="kv_dwk")
    dw_v = _mm(hnkv, dv_b, "tn", M=D, N=D, K=T, out_dtype=BF16, name="kv_dwv")
    dw_f = _mm(hnkv, dfpre, "tn", M=D, N=LANES, K=T, out_dtype=BF16, name="kv_dwf")
    G["kv_w"] = jnp.concatenate([dw_k, dw_v, dw_f[:, :H]], axis=1)
    dh3, G["kv_norm"] = _rmsnorm_bwd(h3, W["kv_norm"], dhnkv, dh3, "kv_dnorm")

    dh2, dn[0][1], dwi[0][1], dwo[0][1] = _ffn_bwd(dh3, ffn01, W["ffn_norm"][0, 1], W["ffn_w_in"][0, 1], W["ffn_w_out"][0, 1], "ffn01")
    dmixed = _mm(dh2, W["a_w_o"], "nt", M=T, N=D, K=D, name="a_dmixed")
    G["a_w_o"] = _mm(mixed, dh2, "tn", M=D, N=D, K=T, out_dtype=BF16, name="a_dwo")[None]
    delta_a = _head_rowsum(dmixed, mixed, "a_delta")
    dhna = None
    dw_qkv, dqn, dkn = [], [], []
    for g in range(3):
        dq, dk, dv = _band_bwd(qs[g], ks[g], qkv, dmixed, lse_a, delta_a, g, B, S, D, f"a_dband{g}")
        dqr, dgq = _qknorm_bwd(qkv, 3 * g * D, W["a_q_norm"][0, g], cs, dq, D, f"a_dqnorm{g}", out_scale=SCALE)
        dkr, dgk = _qknorm_bwd(qkv, (3 * g + 1) * D, W["a_k_norm"][0, g], cs, dk, D, f"a_dknorm{g}")
        dqn.append(dgq)
        dkn.append(dgk)
        dhna = _mm(dqr, W["a_w_qkv"], "nt", M=T, N=D, K=3 * D, a_rest=(dkr, dv), b_off=(0, 3 * g * D), res=dhna,
                   name=f"a_dhn{g}")
        for j, d in enumerate((dqr, dkr, dv)):
            dw_qkv.append(_mm(hna, d, "tn", M=D, N=D, K=T, out_dtype=BF16, name=f"a_dwqkv{3 * g + j}"))
    G["a_w_qkv"] = jnp.concatenate(dw_qkv, axis=1)[None]
    G["a_q_norm"] = jnp.stack(dqn)[None]
    G["a_k_norm"] = jnp.stack(dkn)[None]
    dh1, dmix_a = _rmsnorm_bwd(h1, W["mix_norm"][0], dhna, dh2, "mixa_dnorm")
    dh0, dn[0][0], dwi[0][0], dwo[0][0] = _ffn_bwd(dh1, ffn00, W["ffn_norm"][0, 0], W["ffn_w_in"][0, 0], W["ffn_w_out"][0, 0], "ffn00")

    G["mix_norm"] = jnp.stack([dmix_a, dmix_b])
    G["ffn_norm"] = jnp.stack([jnp.stack(r) for r in dn])
    G["ffn_w_in"] = jnp.stack([jnp.stack(r) for r in dwi])
    G["ffn_w_out"] = jnp.stack([jnp.stack(r) for r in dwo])
    return jnp.sum(sq), dh0.reshape(B, S, D), G


SMALL =["mix_norm", "kv_norm", "a_q_norm", "a_k_norm", "kv_b_f", "kv_k_norm", "b_q_norm"]


def _pack(blocks, lead):
    flat = [b.reshape(b.shape[:lead] + (-1,)) for b in blocks]
    flat = jnp.concatenate(flat, axis=lead)
    n = flat.shape[-1]
    unit = PACK_ROWS * LANES
    padded = -(-n // unit) * unit if n > unit else -(-n // (16 * LANES)) * 16 * LANES
    flat = jnp.pad(flat, [(0, 0)] * lead + [(0, padded - n)])
    return flat.reshape(flat.shape[:lead] + (padded // LANES, LANES))


def _unpack(packed, shapes, lead):
    flat = packed.reshape(packed.shape[:lead] + (-1,))
    out, off = [], 0
    for shp in shapes:
        n = int(np.prod(shp))
        out.append(flat[..., off:off + n].reshape(packed.shape[:lead] + tuple(shp)))
        off += n
    return out


def kernel(x, positions, ffn_norm, ffn_w_in, ffn_w_out, mix_norm, a_w_qkv, a_q_norm, a_k_norm, a_w_o, kv_norm, kv_w, kv_b_f, kv_k_norm, b_w_q, b_q_norm, b_w_o, loss_target, m_ffn_norm, m_ffn_w_in, m_ffn_w_out, m_mix_norm, m_a_w_qkv, m_a_q_norm, m_a_k_norm, m_a_w_o, m_kv_norm, m_kv_w, m_kv_b_f, m_kv_k_norm, m_b_w_q, m_b_q_norm, m_b_w_o, v_ffn_norm, v_ffn_w_in, v_ffn_w_out, v_mix_norm, v_a_w_qkv, v_a_q_norm, v_a_k_norm, v_a_w_o, v_kv_norm, v_kv_w, v_kv_b_f, v_kv_k_norm, v_b_w_q, v_b_q_norm, v_b_w_o):
    names = ["ffn_norm", "ffn_w_in", "ffn_w_out", "mix_norm", "a_w_qkv", "a_q_norm", "a_k_norm", "a_w_o", "kv_norm",
             "kv_w", "kv_b_f", "kv_k_norm", "b_w_q", "b_q_norm", "b_w_o"]
    w = dict(zip(names, (ffn_norm, ffn_w_in, ffn_w_out, mix_norm, a_w_qkv, a_q_norm, a_k_norm, a_w_o, kv_norm, kv_w,
                         kv_b_f, kv_k_norm, b_w_q, b_q_norm, b_w_o)))
    m = dict(zip(names, (m_ffn_norm, m_ffn_w_in, m_ffn_w_out, m_mix_norm, m_a_w_qkv, m_a_q_norm, m_a_k_norm, m_a_w_o,
                         m_kv_norm, m_kv_w, m_kv_b_f, m_kv_k_norm, m_b_w_q, m_b_q_norm, m_b_w_o)))
    v = dict(zip(names, (v_ffn_norm, v_ffn_w_in, v_ffn_w_out, v_mix_norm, v_a_w_qkv, v_a_q_norm, v_a_k_norm, v_a_w_o,
                         v_kv_norm, v_kv_w, v_kv_b_f, v_kv_k_norm, v_b_w_q, v_b_q_norm, v_b_w_o)))
    B, S, D = x.shape
    H = D // HEAD_DIM
    F = ffn_w_out.shape[2] * N_DEV
    Ds, Fs = D // N_DEV, F // N_DEV
    me = 4 * lax.axis_index("x") + 2 * lax.axis_index("y") + lax.axis_index("c")
    squares = ("a_w_o", "b_w_q", "b_w_o")

    def shard_views(t):
        return [t["ffn_w_in"].reshape(4 * D, -1), t["ffn_w_out"].reshape(-1, D), t["a_w_qkv"][0], t["kv_w"],
                jnp.concatenate([t[n][0] for n in squares], axis=0)]

    def to_shards(views):
        sq3 = views[4].reshape(3, 1, Ds, D)
        out = {"ffn_w_in": views[0].reshape(ffn_w_in.shape), "ffn_w_out": views[1].reshape(ffn_w_out.shape),
               "a_w_qkv": views[2][None], "kv_w": views[3]}
        out.update({n: sq3[i] for i, n in enumerate(squares)})
        return out

    sh_in, sh_out, sh_qkv, sh_kv, sh_sq = [s.astype(BF16) for s in shard_views(w)]
    in_of, out_of = (lambda f: sh_in[f * D:(f + 1) * D]), (lambda f: sh_out[f * Fs:(f + 1) * Fs])
    square_of = lambda i: sh_sq[i * Ds:(i + 1) * Ds]
    side_by_side = lambda g: g.transpose(1, 0, 2).reshape(g.shape[1], N_DEV * g.shape[2])
    stacked = lambda g: g.reshape(N_DEV * g.shape[1], g.shape[2])

    def store_ffn(f):
        def store(W, got):
            W["ffn_w_in"][f // 2, f % 2], W["ffn_w_out"][f // 2, f % 2] = side_by_side(got[0]), stacked(got[1])
        return store

    def store_mixer_a(W, got):
        W["a_w_qkv"], W["a_w_o"] = side_by_side(got[0]), stacked(got[1])

    def store_layer_b(W, got):
        store_ffn(1)(W, got)
        kv_full = side_by_side(got[2])
        W["kv_w_kv"] = kv_full[:, :2 * D]
        W["kv_w_f"] = jnp.pad(kv_full[:, 2 * D:], ((0, 0), (0, LANES - H)))
        W["b_w_q"], W["b_w_o"] = stacked(got[3]), stacked(got[4])

    norm_rows = jnp.pad(ffn_norm.reshape(4, Ds), ((0, 4), (0, 0)))
    first = _all_gather([in_of(0), out_of(0), norm_rows], "gather_weights")
    W = {n: w[n] for n in SMALL}
    W["ffn_norm"] = first[2][:, :4].transpose(1, 0, 2).reshape(2, 2, D)
    W["ffn_w_in"], W["ffn_w_out"] = {}, {}
    store_ffn(0)(W, first)
    pending = {"ffn00_in": ([sh_qkv, square_of(0)], store_mixer_a),
               "a_qkv": ([in_of(1), out_of(1), sh_kv, square_of(1), square_of(2)], store_layer_b),
               "a_band0": ([in_of(2), out_of(2)], store_ffn(2)),
               "a_band1": ([in_of(3), out_of(3)], store_ffn(3))}

    sq, grad_x, G = _local_step(x, positions, loss_target, W, pending)
    loss = lax.psum(sq * (0.5 / D), ("x", "y", "c"))

    blocks = [G["ffn_w_in"].reshape(4 * D, N_DEV, -1).transpose(1, 0, 2),
              G["ffn_w_out"].reshape(4, N_DEV, Fs, D).transpose(1, 0, 2, 3).reshape(N_DEV, 4 * Fs, D),
              G["a_w_qkv"][0].reshape(D, N_DEV, -1).transpose(1, 0, 2),
              G["kv_w"].reshape(D, N_DEV, -1).transpose(1, 0, 2),
              jnp.concatenate([G[n][0].reshape(N_DEV, Ds, D) for n in squares], axis=1)]
    blocks = [b.astype(BF16) for b in blocks]
    got = _pair_exchange(blocks, "exchange_pair")
    parts = _chip_exchange([_pair_sum(b, s, f"pair_sum{i}") for i, (b, s) in enumerate(zip(blocks, got))], "exchange_chips")
    views = [shard_views(t) for t in (w, m, v)]
    outs = [_adamw(parts[i], views[0][i], views[1][i], views[2][i], f"adamw_matrix{i}") for i in range(len(parts))]
    new = {}
    for k, kind in enumerate(("grad", "delta", "new_m", "new_v")):
        for n, arr in to_shards([o[k] for o in outs]).items():
            new[kind, n] = arr

    small_all = ["ffn_norm"] + SMALL
    vec_parts = _all_gather([_pack([G[n] for n in small_all], 0)], "gather_vector_grads")[0]
    vec_parts = _unpack(vec_parts, [G[n].shape for n in small_all], 1)
    norm_g = lax.dynamic_slice_in_dim(vec_parts[0], me * ffn_norm.shape[2], ffn_norm.shape[2], axis=3)
    rep_g = _pack(vec_parts[1:], 1)
    for ns, parts_s in ((["ffn_norm"], _pack([norm_g], 1)), (SMALL, rep_g)):
        outs = _adamw(parts_s, *(_pack([t[n] for n in ns], 0) for t in (w, m, v)), "adamw_" + ns[0])
        for kind, packed in zip(("grad", "delta", "new_m", "new_v"), outs):
            for n, arr in zip(ns, _unpack(packed, [w[n].shape for n in ns], 0)):
                new[kind, n] = arr

    result = [loss, grad_x]
    for kind in ("grad", "delta", "new_m", "new_v"):
        result += [new[kind, n] for n in names]
    return tuple(result)
```

```python
import functools

import jax
import jax.numpy as jnp
import numpy as np
from jax import lax
from jax.experimental import pallas as pl
from jax.experimental.pallas import tpu as pltpu

F32 = jnp.float32
BF16 = jnp.bfloat16
HIGHEST = lax.Precision.HIGHEST

HEAD_DIM = 64
ROT_DIM = HEAD_DIM // 4
ROPE_THETA = 500000.0
DILATED_GROUPS = ((128, 1), (512, 4), (2048, 16))
BAND_BLOCK = 128
EPS = 1e-6
ADAM_LR, ADAM_B1, ADAM_B2, ADAM_EPS, ADAM_WD, ADAM_STEP = 0.001, 0.9, 0.999, 1e-08, 0.01, 10

LANES = 128
N_DEV = 8
VMEM_LIMIT = 56 * 1024 * 1024
NEG = -1e30
SCALE = HEAD_DIM ** -0.5
PACK_ROWS = 1024
MESH_ID = pl.DeviceIdType.MESH


def _params(*sem):
    return pltpu.CompilerParams(dimension_semantics=sem, vmem_limit_bytes=VMEM_LIMIT)


def _divs(n, cap, mult=LANES):
    d = [t for t in range(mult, min(n, cap) + 1, mult) if n % t == 0]
    return d or [n]


def _pick(n, cap, mult=LANES):
    return _divs(n, cap, mult)[-1]


def _mm_tiles(M, N, K, budget=1408 * 1024):
    best = None
    for tm in _divs(M, 1408):
        for tn in _divs(N, 1408):
            if tm * tn <= budget and (best is None or tm * tn > best[0] * best[1]):
                best = (tm, tn)
    if best is None:
        best = (_divs(M, 1408)[0], _divs(N, 1408)[0])
    tm, tn = best
    tks = [t for t in _divs(K, 2048) if (tm + tn) * t <= 3 * 1024 * 1024]
    tk = tks[-1] if tks else _divs(K, 2048)[0]
    return tm, tn, tk


def _dot(a, b, dims, precision=None):
    return lax.dot_general(a, b, (dims, ((), ())), precision=precision, preferred_element_type=F32)


NN = ((1,), (0,))
NT = ((1,), (1,))
TN = ((0,), (0,))


def _dot_split(x, mat):
    hi = x.astype(BF16)
    lo = (x - hi.astype(F32)).astype(BF16)
    m = mat.astype(BF16)
    return _dot(hi, m, NN) + _dot(lo, m, NN)


def _dot_once(x, mat):
    return _dot(x.astype(BF16), mat.astype(BF16), NN)


def _mm(a, b, mode, *, M, N, K, a_off=(0, 0), b_off=(0, 0), a_rest=(), b2=None, res=None, scale=1.0, out_dtype=F32,
        name, gather=None):
    a_parts = 1 + len(a_rest)
    tm, tn, tk = _mm_tiles(M, N // 2 if b2 is not None else N, K // a_parts)
    nk = K // tk
    k_part, n_half = nk // a_parts, (N // tn) // 2
    if mode == "tn":
        a_blk, b_blk, dims = (tk, tm), (tk, tn), TN
    elif mode == "nt":
        a_blk, b_blk, dims = (tm, tk), (tn, tk), NT
    else:
        a_blk, b_blk, dims = (tm, tk), (tk, tn), NN
    for off, blk in ((a_off, a_blk), (b_off, b_blk)):
        assert off[0] % blk[0] == 0 and off[1] % blk[1] == 0, (name, off, blk)
    ao = (a_off[0] // a_blk[0], a_off[1] // a_blk[1])
    bo = (b_off[0] // b_blk[0], b_off[1] // b_blk[1])
    if mode == "tn":
        a_map = lambda i, j, k: (k + ao[0], i + ao[1])
        b_map = lambda i, j, k: (k + bo[0], j + bo[1])
    elif mode == "nt":
        a_map = lambda i, j, k: (i + ao[0], k + ao[1])
        b_map = lambda i, j, k: (j + bo[0], k + bo[1])
    else:
        a_map = lambda i, j, k: (i + ao[0], k + ao[1])
        b_map = lambda i, j, k: (k + bo[0], j + bo[1])
    has_res = res is not None
    if a_rest:
        assert mode != "tn" and a_off == (0, 0)
        part_map = lambda p: (lambda i, j, k: (i, jnp.clip(k - p * k_part, 0, k_part - 1)))
        a_map = part_map(0)
    if b2 is not None:
        assert mode != "nt" and b_off == (0, 0)
        b_map = lambda i, j, k: (k, jnp.minimum(j, n_half - 1))
        b2_map = lambda i, j, k: (k, jnp.maximum(j - n_half, 0))

    def body(*refs):
        refs = list(refs)
        a_ref, b_ref = refs.pop(0), refs.pop(0)
        a_val, b_val = a_ref[...].astype(BF16), b_ref[...].astype(BF16)
        for p in range(1, a_parts):
            a_val = jnp.where(pl.program_id(2) >= p * k_part, refs.pop(0)[...].astype(BF16), a_val)
        if b2 is not None:
            b_val = jnp.where(pl.program_id(1) < n_half, b_val, refs.pop(0)[...].astype(BF16))
        r_ref = refs.pop(0) if has_res else None
        o_ref = refs.pop(0)

        def finish(total):
            out = total * scale
            if has_res:
                out = r_ref[...] + out
            o_ref[...] = out.astype(out_dtype)

        part = _dot(a_val, b_val, dims)
        if nk == 1:
            finish(part)
            return
        acc = refs[-1]
        k = pl.program_id(2)

        @pl.when(k == 0)
        def _():
            acc[...] = part

        @pl.when(k > 0)
        def _():
            acc[...] += part

        pl.when(k == nk - 1)(lambda: finish(acc[...]))

    in_specs = [pl.BlockSpec(a_blk, a_map), pl.BlockSpec(b_blk, b_map)]
    args = [a, b]
    for p, extra in enumerate(a_rest, 1):
        in_specs.append(pl.BlockSpec(a_blk, part_map(p)))
        args.append(extra)
    if b2 is not None:
        in_specs.append(pl.BlockSpec(b_blk, b2_map))
        args.append(b2)
    if has_res:
        in_specs.append(pl.BlockSpec((tm, tn), lambda i, j, k: (i, j)))
        args.append(res)
    out, gathered = _call(
        body, name=name, grid=(M // tm, N // tn, nk), in_specs=in_specs,
        out_specs=pl.BlockSpec((tm, tn), lambda i, j, k: (i, j)),
        out_shape=jax.ShapeDtypeStruct((M, N), out_dtype),
        scratch_shapes=[pltpu.VMEM((tm, tn), F32)] if nk > 1 else [],
        semantics=("parallel", "parallel", "arbitrary"), args=args, gather=gather)
    return out if gather is None else (out, gathered)


def _rmsnorm_fwd(x, g, name):
    T, D = x.shape
    tm = _pick(T, 256, 8)

    def body(x_ref, g_ref, y_ref):
        xv = x_ref[...]
        r = lax.rsqrt(jnp.mean(xv * xv, axis=-1, keepdims=True) + EPS)
        y_ref[...] = (xv * r * g_ref[...]).astype(BF16)

    return pl.pallas_call(
        body, name=name, grid=(T // tm,),
        in_specs=[pl.BlockSpec((tm, D), lambda i: (i, 0)), pl.BlockSpec((1, D), lambda i: (0, 0))],
        out_specs=pl.BlockSpec((tm, D), lambda i: (i, 0)),
        out_shape=jax.ShapeDtypeStruct((T, D), BF16), compiler_params=_params("parallel"),
    )(x, g.reshape(1, D))


def _rmsnorm_bwd(x, g, dy, dres, name):
    T, D = x.shape
    tm = _pick(T, 256, 8)

    def body(x_ref, g_ref, dy_ref, dres_ref, dx_ref, dg_ref):
        xv = x_ref[...]
        r = lax.rsqrt(jnp.mean(xv * xv, axis=-1, keepdims=True) + EPS)
        xn = xv * r
        d = dy_ref[...]
        dgy = d * g_ref[...]
        dx_ref[...] = dres_ref[...] + r * (dgy - xn * jnp.mean(dgy * xn, axis=-1, keepdims=True))

        @pl.when(pl.program_id(0) == 0)
        def _():
            dg_ref[...] = jnp.zeros_like(dg_ref)

        dg_ref[...] += jnp.sum(d * xn, axis=0, keepdims=True)

    row = pl.BlockSpec((tm, D), lambda i: (i, 0))
    vec = pl.BlockSpec((1, D), lambda i: (0, 0))
    dx, dg = pl.pallas_call(
        body, name=name, grid=(T // tm,), in_specs=[row, vec, row, row], out_specs=[row, vec],
        out_shape=[jax.ShapeDtypeStruct((T, D), F32), jax.ShapeDtypeStruct((1, D), F32)],
        compiler_params=_params("arbitrary"),
    )(x, g.reshape(1, D), dy, dres)
    return dx, dg.reshape(D)


def _sigmoid(x):
    return 1.0 / (1.0 + jnp.exp(-x))


MXU_WIDTH = 256


def _col_chunks(n):
    return [(lo, min(lo + MXU_WIDTH, n)) for lo in range(0, n, MXU_WIDTH)]


def _mm_swiglu_fwd(hn, w_in, name, gather=None):
    T, D = hn.shape
    F = w_in.shape[1] // 2
    tm, tn, _ = _mm_tiles(T, F, D)
    assert (tm + 2 * tn) * D <= 4 * 1024 * 1024, "the whole contraction is held in one step"
    nf = F // tn

    def body(a_ref, bg_ref, bu_ref, g_ref, u_ref, act_ref):
        a = a_ref[...].astype(BF16)
        for lo, hi in _col_chunks(tn):
            gv = _dot(a, bg_ref[:, lo:hi].astype(BF16), NN)
            uv = _dot(a, bu_ref[:, lo:hi].astype(BF16), NN)
            g_ref[:, lo:hi] = gv.astype(BF16)
            u_ref[:, lo:hi] = uv.astype(BF16)
            act_ref[:, lo:hi] = (gv * _sigmoid(gv) * uv).astype(BF16)

    tile = pl.BlockSpec((tm, tn), lambda j, i: (i, j))
    out = jax.ShapeDtypeStruct((T, F), BF16)
    outs, gathered = _call(
        body, name=name, grid=(nf, T // tm),
        in_specs=[pl.BlockSpec((tm, D), lambda j, i: (i, 0)), pl.BlockSpec((D, tn), lambda j, i: (0, j)),
                  pl.BlockSpec((D, tn), lambda j, i: (0, j + nf))],
        out_specs=[tile, tile, tile], out_shape=[out, out, out], semantics=("parallel", "parallel"),
        args=(hn, w_in, w_in), gather=gather)
    return outs if gather is None else (outs, gathered)


def _mm_swiglu_bwd(dout, w_out, gate, up, name, exchange=None):
    T, D = dout.shape
    F = w_out.shape[0]
    tm, tn, _ = _mm_tiles(T, F, D)
    assert (tm + tn) * D <= 4 * 1024 * 1024, "the whole contraction is held in one step"

    def body(a_ref, b_ref, g_ref, u_ref, dg_ref, du_ref):
        a = a_ref[...].astype(BF16)
        for lo, hi in _col_chunks(tn):
            d = _dot(a, b_ref[lo:hi, :].astype(BF16), NT) * 0.5
            gv, uv = g_ref[:, lo:hi].astype(F32), u_ref[:, lo:hi].astype(F32)
            s = _sigmoid(gv)
            gs = gv * s
            dg_ref[:, lo:hi] = (d * uv * (s + gs - gs * s)).astype(BF16)
            du_ref[:, lo:hi] = (d * gs).astype(BF16)

    tile = pl.BlockSpec((tm, tn), lambda j, i: (i, j))
    out = jax.ShapeDtypeStruct((T, F), BF16)
    (dg, du), parts = _call(
        body, name=name, grid=(F // tn, T // tm),
        in_specs=[pl.BlockSpec((tm, D), lambda j, i: (i, 0)), pl.BlockSpec((tn, D), lambda j, i: (j, 0)), tile, tile],
        out_specs=[tile, tile], out_shape=[out, out], semantics=("parallel", "parallel"),
        args=(dout, w_out, gate, up), exchange=exchange)
    return dg, du, parts


def _iota2(shape, dim):
    return lax.broadcasted_iota(jnp.int32, shape, dim)


def _head_ones():
    r, c = _iota2((LANES, LANES), 0), _iota2((LANES, LANES), 1)
    return ((r >> 6) == (c >> 6)).astype(F32)


def _rot_matrix(transpose):
    r, c = _iota2((LANES, LANES), 0), _iota2((LANES, LANES), 1)
    if transpose:
        r, c = c, r
    half = ROT_DIM // 2
    cm = c & (HEAD_DIM - 1)
    neg = (r == c + half) & (cm < half)
    pos = (r == c - half) & (cm >= half) & (cm < ROT_DIM)
    return pos.astype(F32) - neg.astype(F32)


def _rope_tables(positions, name):
    T = positions.size
    tm = _pick(T, 1024, 8)
    inv = ROPE_THETA ** (-jnp.arange(0, ROT_DIM, 2, dtype=F32) / ROT_DIM)
    lane = jnp.concatenate([inv, inv, jnp.zeros((HEAD_DIM - ROT_DIM,), F32)])
    lane = jnp.tile(lane, 2).reshape(1, LANES)

    def body(p_ref, f_ref, c_ref, s_ref):
        ang = p_ref[...].astype(F32) * f_ref[...]
        c_ref[...] = jnp.cos(ang)
        s_ref[...] = jnp.sin(ang)

    out = jax.ShapeDtypeStruct((T, LANES), F32)
    blk = pl.BlockSpec((tm, LANES), lambda i: (i, 0))
    return pl.pallas_call(
        body, name=name, grid=(T // tm,),
        in_specs=[pl.BlockSpec((tm, 1), lambda i: (i, 0)), pl.BlockSpec((1, LANES), lambda i: (0, 0))],
        out_specs=[blk, blk], out_shape=[out, out], compiler_params=_params("parallel"),
    )(positions.reshape(T, 1), lane)


def _qknorm_fwd(src, col_off, gain, cs, D, out_dtype, name, out_scale=1.0):
    T = src.shape[0]
    HP = D // LANES
    tm = _pick(T, 1024, 8)
    co = col_off // LANES
    rope = cs is not None

    def body(*refs):
        if rope:
            x_ref, g_ref, c_ref, s_ref, o_ref = refs
        else:
            x_ref, g_ref, o_ref = refs
        xv = x_ref[...]
        ms = _dot_once(xv * xv, _head_ones()) * (1.0 / HEAD_DIM)
        y = xv * lax.rsqrt(ms + EPS) * g_ref[...]
        if rope:
            y = y * c_ref[...] + _dot_once(y, _rot_matrix(False)) * s_ref[...]
        o_ref[...] = (y * out_scale).astype(out_dtype)

    in_specs = [pl.BlockSpec((tm, LANES), lambda i, h: (i, co + h)), pl.BlockSpec((1, LANES), lambda i, h: (0, 0))]
    args = [src, jnp.tile(gain.reshape(1, HEAD_DIM), (1, 2))]
    if rope:
        in_specs += [pl.BlockSpec((tm, LANES), lambda i, h: (i, 0))] * 2
        args += list(cs)
    return pl.pallas_call(
        body, name=name, grid=(T // tm, HP), in_specs=in_specs,
        out_specs=pl.BlockSpec((tm, LANES), lambda i, h: (i, h)),
        out_shape=jax.ShapeDtypeStruct((T, D), out_dtype), compiler_params=_params("parallel", "parallel"),
    )(*args)


def _qknorm_bwd(src, col_off, gain, cs, dout, D, name, out_scale=1.0):
    T = src.shape[0]
    HP = D // LANES
    tm = _pick(T, 1024, 8)
    co = col_off // LANES
    rope = cs is not None

    def body(*refs):
        if rope:
            x_ref, g_ref, d_ref, c_ref, s_ref, dx_ref, dg_ref = refs
        else:
            x_ref, g_ref, d_ref, dx_ref, dg_ref = refs
        xv = x_ref[...]
        ones = _head_ones()
        ms = _dot_once(xv * xv, ones) * (1.0 / HEAD_DIM)
        r = lax.rsqrt(ms + EPS)
        xn = xv * r
        d = d_ref[...] * out_scale
        if rope:
            d = d * c_ref[...] + _dot_once(d * s_ref[...], _rot_matrix(True))
        dgy = d * g_ref[...]
        mean = _dot_once(dgy * xn, ones) * (1.0 / HEAD_DIM)
        dx_ref[...] = (r * (dgy - xn * mean)).astype(BF16)

        @pl.when(pl.program_id(1) == 0)
        def _():
            dg_ref[...] = jnp.zeros_like(dg_ref)

        dg_ref[...] += jnp.sum(d * xn, axis=0, keepdims=True)[None]

    in_specs = [pl.BlockSpec((tm, LANES), lambda h, i: (i, co + h)), pl.BlockSpec((1, LANES), lambda h, i: (0, 0)),
                pl.BlockSpec((tm, LANES), lambda h, i: (i, h))]
    args = [src, jnp.tile(gain.reshape(1, HEAD_DIM), (1, 2)), dout]
    if rope:
        in_specs += [pl.BlockSpec((tm, LANES), lambda h, i: (i, 0))] * 2
        args += list(cs)
    dx, dg = pl.pallas_call(
        body, name=name, grid=(HP, T // tm), in_specs=in_specs,
        out_specs=[pl.BlockSpec((tm, LANES), lambda h, i: (i, h)), pl.BlockSpec((1, 1, LANES), lambda h, i: (h, 0, 0))],
        out_shape=[jax.ShapeDtypeStruct((T, D), BF16), jax.ShapeDtypeStruct((HP, 1, LANES), F32)],
        compiler_params=_params("parallel", "arbitrary"),
    )(*args)
    return dx, dg.reshape(2 * HP, HEAD_DIM).sum(axis=0)


def _head_rowsum(a, b, name):
    T, D = a.shape
    tm = _pick(T, 1024, 8)

    def body(a_ref, b_ref, o_ref):
        o_ref[...] = _dot_split(a_ref[...] * b_ref[...], _head_ones())

    blk = pl.BlockSpec((tm, LANES), lambda i, h: (i, h))
    return pl.pallas_call(
        body, name=name, grid=(T // tm, D // LANES), in_specs=[blk, blk], out_specs=blk,
        out_shape=jax.ShapeDtypeStruct((T, D), F32), compiler_params=_params("parallel", "parallel"),
    )(a, b)


def _lane_masks():
    lane = _iota2((1, LANES), 1)
    return (lane < HEAD_DIM, lane >= HEAD_DIM)


BAND_GROUP = 4


def _band_geometry(g, S):
    window, dil = DILATED_GROUPS[g]
    chunk = BAND_BLOCK * dil
    assert window // dil == BAND_BLOCK and S % chunk == 0
    if dil == 1:
        assert (S // chunk) % BAND_GROUP == 0
        return dil, chunk, BAND_GROUP, 1
    assert dil % BAND_GROUP == 0
    return dil, chunk, 1, dil // BAND_GROUP


def _band_rows(i, rho, dil, shift=0):
    if dil > 1:
        return pl.ds(rho * BAND_GROUP + i, BAND_BLOCK, stride=dil), shift != 0
    j = i + shift
    outside = j < 0 or j >= BAND_GROUP
    return pl.ds(0 if outside else j * BAND_BLOCK, BAND_BLOCK), outside


def _stack_heads(x):
    hm0, hm1 = _lane_masks()
    zero = jnp.zeros_like(x)
    return jnp.concatenate([jnp.where(hm0, x, zero), jnp.where(hm1, x, zero)], axis=0)


def _unstack_heads(x2):
    hm0, _ = _lane_masks()
    return jnp.where(hm0, x2[:BAND_BLOCK], x2[BAND_BLOCK:])


def _band_masks():
    qi = _iota2((2 * BAND_BLOCK, 2 * BAND_BLOCK), 0) & (BAND_BLOCK - 1)
    kj = _iota2((2 * BAND_BLOCK, 2 * BAND_BLOCK), 1)
    own = (kj >= BAND_BLOCK) & (kj - BAND_BLOCK <= qi)
    prev = (kj < BAND_BLOCK) & (kj >= qi)
    return own, prev


def _band_fwd(qh, kh, qkv, g, B, S, D, name, gather=None):
    dil, CH, NB, RG = _band_geometry(g, S)
    HP = D // LANES
    vo = (3 * g + 2) * HP
    nsp = S // (CH * NB)

    def body(q_ref, k_ref, kp_ref, v_ref, vp_ref, o_ref, lse_ref):
        n, rho = pl.program_id(1), pl.program_id(3)
        own, prev = _band_masks()
        for i in range(BAND_GROUP):
            rows, _ = _band_rows(i, rho, dil)
            prows, outside = _band_rows(i, rho, dil, shift=-1)
            q, kc, vc = (t[0, rows, :].astype(BF16) for t in (q_ref, k_ref, v_ref))
            kp, vp = (t[0, prows, :].astype(BF16) for t in ((kp_ref, vp_ref) if outside else (k_ref, v_ref)))
            valid = (own | (prev & (n > 0))) if outside else (own | prev)
            s = jnp.where(valid, _dot(_stack_heads(q), jnp.concatenate([kp, kc], axis=0), NT), NEG)
            m = jnp.max(s, axis=-1, keepdims=True)
            e = jnp.exp(s - m)
            l = jnp.sum(e, axis=-1, keepdims=True)
            o2 = _dot((e * (1.0 / l)).astype(BF16), jnp.concatenate([vp, vc], axis=0), NN)
            o_ref[0, rows, :] = _unstack_heads(o2)
            lse_ref[0, rows, :] = _unstack_heads(m + jnp.log(l))

    span, chunk = (1, CH * NB, LANES), (1, CH, LANES)
    cur = lambda b, n, h, r: (b, n, h)
    prev = lambda b, n, h, r: (b, jnp.maximum(n * NB - 1, 0), h)
    vcur = lambda b, n, h, r: (b, n, vo + h)
    vprev = lambda b, n, h, r: (b, jnp.maximum(n * NB - 1, 0), vo + h)
    out = jax.ShapeDtypeStruct((B, S, D), F32)
    q3, k3, v3 = qh.reshape(B, S, D), kh.reshape(B, S, D), qkv.reshape(B, S, 9 * D)
    (o, lse), gathered = _call(
        body, name=name, grid=(B, nsp, HP, RG),
        in_specs=[pl.BlockSpec(span, cur), pl.BlockSpec(span, cur), pl.BlockSpec(chunk, prev),
                  pl.BlockSpec(span, vcur), pl.BlockSpec(chunk, vprev)],
        out_specs=[pl.BlockSpec(span, cur), pl.BlockSpec(span, cur)], out_shape=[out, out],
        semantics=("parallel", "parallel", "parallel", "arbitrary"), args=(q3, k3, k3, v3, v3), gather=gather)
    o, lse = o.reshape(B * S, D), lse.reshape(B * S, D)
    return (o, lse) if gather is None else (o, lse, gathered)


def _head_col(x, h):
    lane = _iota2((1, LANES), 1)
    return jnp.sum(jnp.where(lane == h * HEAD_DIM, x, 0.0), axis=-1, keepdims=True)


def _band_bwd(qh, kh, qkv, do, lse, delta, g, B, S, D, name, exchanges=None):
    dil, CH, NB, RG = _band_geometry(g, S)
    HP = D // LANES
    vo = (3 * g + 2) * HP
    nsp, nch = S // (CH * NB), S // CH

    def stacked_cols(tile):
        return jnp.concatenate([_head_col(tile, 0), _head_col(tile, 1)], axis=0)

    def body(q_ref, qn_ref, do_ref, don_ref, l_ref, ln_ref, d_ref, dn_ref, k_ref, kp_ref, v_ref, vp_ref,
             dq_ref, dk_ref, dv_ref):
        n, rho = pl.program_id(1), pl.program_id(3)
        own, prev = _band_masks()
        band = (_iota2((2 * BAND_BLOCK, BAND_BLOCK), 1) >= (_iota2((2 * BAND_BLOCK, BAND_BLOCK), 0) & (BAND_BLOCK - 1)))
        for i in range(BAND_GROUP):
            rows, _ = _band_rows(i, rho, dil)
            prows, p_out = _band_rows(i, rho, dil, shift=-1)
            nrows, n_out = _band_rows(i, rho, dil, shift=1)
            kc, vc = (t[0, rows, :].astype(BF16) for t in (k_ref, v_ref))
            kp, vp = (t[0, prows, :].astype(BF16) for t in ((kp_ref, vp_ref) if p_out else (k_ref, v_ref)))
            q2, do2 = (_stack_heads(t[0, rows, :].astype(BF16)) for t in (q_ref, do_ref))
            qn2, don2 = (_stack_heads(t[0, nrows, :].astype(BF16)) for t in ((qn_ref, don_ref) if n_out else (q_ref, do_ref)))
            lse2, del2 = (stacked_cols(t[0, rows, :]) for t in (l_ref, d_ref))
            lsen2, deln2 = (stacked_cols(t[0, nrows, :]) for t in ((ln_ref, dn_ref) if n_out else (l_ref, d_ref)))
            valid = (own | (prev & (n > 0))) if p_out else (own | prev)
            next_ok = (band & (n < nsp - 1)) if n_out else band
            k2, v2 = jnp.concatenate([kp, kc], axis=0), jnp.concatenate([vp, vc], axis=0)
            p = jnp.exp(jnp.where(valid, _dot(q2, k2, NT) - lse2, NEG))
            ds = (p * (_dot(do2, v2, NT) - del2)).astype(BF16)
            pc = jnp.exp(jnp.where(next_ok, _dot(qn2, kc, NT) - lsen2, NEG))
            dsc = (pc * (_dot(don2, vc, NT) - deln2)).astype(BF16)
            dq_ref[0, rows, :] = _unstack_heads(_dot(ds, k2, NN))
            dk_ref[0, rows, :] = _dot(ds[:, BAND_BLOCK:], q2, TN) + _dot(dsc, qn2, TN)
            dv_ref[0, rows, :] = _dot(p[:, BAND_BLOCK:].astype(BF16), do2, TN) + _dot(pc.astype(BF16), don2, TN)

    span, chunk = (1, CH * NB, LANES), (1, CH, LANES)
    cur = lambda b, n, h, r: (b, n, h)
    prev = lambda b, n, h, r: (b, jnp.maximum(n * NB - 1, 0), h)
    nxt = lambda b, n, h, r: (b, jnp.minimum((n + 1) * NB, nch - 1), h)
    vcur = lambda b, n, h, r: (b, n, vo + h)
    vprev = lambda b, n, h, r: (b, jnp.maximum(n * NB - 1, 0), vo + h)
    q3, k3, do3, l3, d3 = (t.reshape(B, S, D) for t in (qh, kh, do, lse, delta))
    v3 = qkv.reshape(B, S, 9 * D)
    sp, ch = (lambda m: pl.BlockSpec(span, m)), (lambda m: pl.BlockSpec(chunk, m))
    out = jax.ShapeDtypeStruct((B, S, D), F32)
    job = exchanges.take() if exchanges else None
    (dq, dk, dv), parts = _call(
        body, name=name, grid=(B, nsp, HP, RG),
        in_specs=[sp(cur), ch(nxt), sp(cur), ch(nxt), sp(cur), ch(nxt), sp(cur), ch(nxt),
                  sp(cur), ch(prev), sp(vcur), ch(vprev)],
        out_specs=[sp(cur)] * 3, out_shape=[out] * 3, semantics=("parallel", "parallel", "parallel", "arbitrary"),
        args=(q3, q3, do3, do3, l3, l3, d3, d3, k3, k3, v3, v3), exchange=job.arrays if job else None)
    if job:
        job.finish(parts)
    return dq.reshape(B * S, D), dk.reshape(B * S, D), dv.reshape(B * S, D)


def _mix(os_, lses, name):
    T, D = os_[0].shape
    tm = _pick(T, 256, 8)

    def body(o0, o1, o2, l0, l1, l2, mix_ref, lse_ref):
        a, b, c = l0[...], l1[...], l2[...]
        m = jnp.maximum(jnp.maximum(a, b), c)
        ea, eb, ec = jnp.exp(a - m), jnp.exp(b - m), jnp.exp(c - m)
        s = ea + eb + ec
        mix_ref[...] = (ea / s) * o0[...] + (eb / s) * o1[...] + (ec / s) * o2[...]
        lse_ref[...] = m + jnp.log(s)

    blk = pl.BlockSpec((tm, D), lambda i: (i, 0))
    out = jax.ShapeDtypeStruct((T, D), F32)
    return pl.pallas_call(
        body, name=name, grid=(T // tm,), in_specs=[blk] * 6, out_specs=[blk, blk], out_shape=[out, out],
        compiler_params=_params("parallel"),
    )(*os_, *lses)


def _gate_fwd(fpre, b_f, B, S, D, name):
    nb = S // LANES

    def body(f_ref, b_ref, cum_ref, rep_ref, carry):
        @pl.when(pl.program_id(1) == 0)
        def _():
            carry[...] = jnp.zeros_like(carry)

        z = f_ref[...] + b_ref[...]
        lf = jnp.minimum(z, 0.0) - jnp.log(1.0 + jnp.exp(-jnp.abs(z)))
        tri = (_iota2((LANES, LANES), 0) >= _iota2((LANES, LANES), 1)).astype(F32)
        c = _dot(tri, lf, NN, HIGHEST) + carry[...]
        cum_ref[...] = c
        carry[...] += jnp.sum(lf, axis=0, keepdims=True)
        spread = ((_iota2((LANES, D), 1) >> 6) == _iota2((LANES, D), 0)).astype(F32)
        rep_ref[...] = _dot(c, spread, NN, HIGHEST)

    return pl.pallas_call(
        body, name=name, grid=(B, nb),
        in_specs=[pl.BlockSpec((LANES, LANES), lambda b, j: (b * nb + j, 0)), pl.BlockSpec((1, LANES), lambda b, j: (0, 0))],
        out_specs=[pl.BlockSpec((LANES, LANES), lambda b, j: (b * nb + j, 0)), pl.BlockSpec((LANES, D), lambda b, j: (b * nb + j, 0))],
        out_shape=[jax.ShapeDtypeStruct((B * S, LANES), F32), jax.ShapeDtypeStruct((B * S, D), F32)],
        scratch_shapes=[pltpu.VMEM((1, LANES), F32)], compiler_params=_params("arbitrary", "arbitrary"),
    )(fpre, b_f)


def _gate_bwd(dcum_k, dcum_q, fpre, b_f, B, S, name):
    nb = S // LANES

    def body(d_ref, dq_ref, f_ref, b_ref, o_ref, db_ref, carry):
        first = (pl.program_id(0) == 0) & (pl.program_id(1) == 0)

        @pl.when(pl.program_id(1) == 0)
        def _():
            carry[...] = jnp.zeros_like(carry)

        @pl.when(first)
        def _():
            db_ref[...] = jnp.zeros_like(db_ref)

        d = d_ref[...] + dq_ref[...]
        tri = (_iota2((LANES, LANES), 0) <= _iota2((LANES, LANES), 1)).astype(F32)
        dlf = _dot(tri, d, NN, HIGHEST) + carry[...]
        carry[...] += jnp.sum(d, axis=0, keepdims=True)
        z = f_ref[...] + b_ref[...]
        dz = dlf * (1.0 / (1.0 + jnp.exp(z)))
        o_ref[...] = dz.astype(BF16)
        db_ref[...] += jnp.sum(dz, axis=0, keepdims=True)

    rev = lambda b, j: (b * nb + nb - 1 - j, 0)
    blk = pl.BlockSpec((LANES, LANES), rev)
    vec = pl.BlockSpec((1, LANES), lambda b, j: (0, 0))
    return pl.pallas_call(
        body, name=name, grid=(B, nb), in_specs=[blk, blk, blk, vec], out_specs=[blk, vec],
        out_shape=[jax.ShapeDtypeStruct((B * S, LANES), BF16), jax.ShapeDtypeStruct((1, LANES), F32)],
        scratch_shapes=[pltpu.VMEM((1, LANES), F32)], compiler_params=_params("arbitrary", "arbitrary"),
    )(dcum_k, dcum_q, fpre, b_f)


def _fox_scores(q, k, ck_row, hm, diagonal):
    TB = q.shape[0]
    qm = jnp.where(hm, q, jnp.zeros_like(q))
    t = _dot(qm, k, NT) - ck_row
    if diagonal:
        t = jnp.where(_iota2((TB, TB), 1) <= _iota2((TB, TB), 0), t, NEG)
    return qm, t


def _fox_fetch(srcs, bufs, sems, slot):
    return [pltpu.make_async_copy(src, buf.at[slot], sems.at[slot, i]) for i, (src, buf) in enumerate(zip(srcs, bufs))]


FOX_BLOCK = 256
EXP_UNDERFLOW = 104.0


def _fox_live_blocks(cum, gain_q, gain_k, B, S, D):
    H, HP = D // HEAD_DIM, D // LANES
    nb = S // FOX_BLOCK
    bound = 1.02 * HEAD_DIM * SCALE * jnp.max(jnp.abs(gain_q)) * jnp.max(jnp.abs(gain_k))
    c = cum.reshape(B, S, LANES)[:, :, :H]
    decay = c[:, 0::FOX_BLOCK, None, :] - c[:, None, FOX_BLOCK - 1::FOX_BLOCK, :]
    below = jnp.arange(nb)[None, :] < jnp.arange(nb)[:, None]
    dead = (decay < -(EXP_UNDERFLOW + 2.0 * bound)) & below[None, :, :, None]
    live = jnp.logical_not(dead.reshape(B, nb, nb, HP, 2).all(axis=-1))
    first = jnp.argmax(live, axis=2)
    last = nb - 1 - jnp.argmax(live[:, ::-1], axis=1)
    to_table = lambda t: t.transpose(0, 2, 1).reshape(B * HP * nb).astype(jnp.int32)
    return to_table(first), to_table(last)


def _fox_fwd(qh, kh, kv, cq, ck, first, B, S, D, name):
    HP = D // LANES
    TB = FOX_BLOCK
    nb = S // TB

    def body(first_ref, q_ref, cq_ref, k_ref, v_ref, ck_ref, kp_ref, vp_ref, ckp_ref, k_hbm, v_hbm, ck_hbm,
             o_ref, lse_ref, kbuf, vbuf, cbuf, sems, m_sc, l_sc, acc):
        b, qb, hp = pl.program_id(0), pl.program_id(1), pl.program_id(2)
        kb0 = first_ref[(b * HP + hp) * nb + qb]
        n_far = jnp.maximum(qb - 1 - kb0, 0)

        def fetch(kb, slot):
            rows = pl.ds(pl.multiple_of(kb * TB, TB), TB)
            srcs = (k_hbm.at[b, rows, pl.ds(pl.multiple_of(hp * LANES, LANES), LANES)],
                    v_hbm.at[b, rows, pl.ds(pl.multiple_of((HP + hp) * LANES, LANES), LANES)],
                    ck_hbm.at[b * HP + hp, :, rows])
            return _fox_fetch(srcs, (kbuf, vbuf, cbuf), sems, slot)

        @pl.when(n_far > 0)
        def _():
            for cp in fetch(kb0, 0):
                cp.start()

        m_sc[...] = jnp.full_like(m_sc, NEG)
        l_sc[...] = jnp.zeros_like(l_sc)
        acc[...] = jnp.zeros_like(acc)

        def step(k, v, ck_rows, diagonal):
            q, v = q_ref[0], v.astype(BF16)
            cqv = cq_ref[0]
            a = acc[...]
            for h, hm in enumerate(_lane_masks()):
                _, t = _fox_scores(q, k, ck_rows[h], hm, diagonal)
                cqh = _head_col(cqv, h)
                m_prev = jnp.max(m_sc[h], axis=-1, keepdims=True)
                l_prev = jnp.max(l_sc[h], axis=-1, keepdims=True)
                m_new = jnp.maximum(m_prev, jnp.max(t, axis=-1, keepdims=True) + cqh)
                alpha = jnp.exp(m_prev - m_new)
                p = jnp.exp(t + (cqh - m_new))
                l_new = alpha * l_prev + jnp.sum(p, axis=-1, keepdims=True)
                a = jnp.where(hm, alpha * a + _dot(p.astype(BF16), v, NN), a)
                m_sc[h] = jnp.broadcast_to(m_new, (TB, LANES))
                l_sc[h] = jnp.broadcast_to(l_new, (TB, LANES))
            acc[...] = a

        def walk(i, carry):
            slot = lax.rem(i, 2)

            @pl.when(i + 1 < n_far)
            def _():
                for cp in fetch(kb0 + i + 1, 1 - slot):
                    cp.start()

            for cp in fetch(kb0 + i, slot):
                cp.wait()
            step(kbuf[slot], vbuf[slot], [cbuf[slot, pl.ds(h, 1), :] for h in range(2)], False)
            return carry

        lax.fori_loop(0, n_far, walk, 0)
        pl.when(kb0 < qb)(lambda: step(kp_ref[0], vp_ref[0], [ckp_ref[0, h:h + 1, :] for h in range(2)], False))
        step(k_ref[0], v_ref[0], [ck_ref[0, h:h + 1, :] for h in range(2)], True)
        a = acc[...]
        out = a
        lse = jnp.zeros_like(a)
        for h, hm in enumerate(_lane_masks()):
            out = jnp.where(hm, a / l_sc[h], out)
            lse = jnp.where(hm, m_sc[h] + jnp.log(l_sc[h]), lse)
        o_ref[0] = out
        lse_ref[0] = lse

    out = jax.ShapeDtypeStruct((B, S, D), F32)
    spec = pltpu.PrefetchScalarGridSpec(
        num_scalar_prefetch=1, grid=(B, nb, HP),
        in_specs=_fox_q_specs(2, TB) + _fox_k_specs(TB, HP, 0) + _fox_k_specs(TB, HP, -1) + [_ANY] * 3,
        out_specs=_fox_q_specs(2, TB), scratch_shapes=_fox_k_buffers(TB) + [
            pltpu.VMEM((2, TB, LANES), F32), pltpu.VMEM((2, TB, LANES), F32), pltpu.VMEM((TB, LANES), F32)])
    k3, v3 = kh.reshape(B, S, D), kv.reshape(B, S, 2 * D)
    o, lse = pl.pallas_call(
        body, name=name, grid_spec=spec, out_shape=[out, out], compiler_params=_params("parallel", "parallel", "parallel"),
    )(first, qh.reshape(B, S, D), cq.reshape(B, S, D), k3, v3, ck, k3, v3, ck, k3, v3, ck)
    return o.reshape(B * S, D), lse.reshape(B * S, D)


_ANY = pl.BlockSpec(memory_space=pl.ANY)


def _fox_q_specs(n, TB):
    return [pl.BlockSpec((1, TB, LANES), lambda b, i, h, table: (b, i, h))] * n


def _fox_k_specs(TB, HP, shift):
    blk = lambda i: jnp.maximum(i + shift, 0)
    return [pl.BlockSpec((1, TB, LANES), lambda b, i, h, table: (b, blk(i), h)),
            pl.BlockSpec((1, TB, LANES), lambda b, i, h, table: (b, blk(i), HP + h)),
            pl.BlockSpec((1, 8, TB), lambda b, i, h, table: (b * HP + h, 0, blk(i)))]


def _fox_k_buffers(TB):
    return [pltpu.VMEM((2, TB, LANES), BF16), pltpu.VMEM((2, TB, LANES), F32), pltpu.VMEM((2, 8, TB), F32),
            pltpu.SemaphoreType.DMA((2, 3))]


def _fox_bwd_dq(qh, kh, kv, cq, ck, first, do, lse, delta, B, S, D, name):
    HP = D // LANES
    TB = FOX_BLOCK
    nb = S // TB

    def body(first_ref, q_ref, cq_ref, do_ref, l_ref, d_ref, k_ref, v_ref, ck_ref, kp_ref, vp_ref, ckp_ref,
             k_hbm, v_hbm, ck_hbm, dq_ref, dc_ref, kbuf, vbuf, cbuf, sems, acc, acc_c):
        b, qb, hp = pl.program_id(0), pl.program_id(1), pl.program_id(2)
        kb0 = first_ref[(b * HP + hp) * nb + qb]
        n_far = jnp.maximum(qb - 1 - kb0, 0)

        def fetch(kb, slot):
            rows = pl.ds(pl.multiple_of(kb * TB, TB), TB)
            srcs = (k_hbm.at[b, rows, pl.ds(pl.multiple_of(hp * LANES, LANES), LANES)],
                    v_hbm.at[b, rows, pl.ds(pl.multiple_of((HP + hp) * LANES, LANES), LANES)],
                    ck_hbm.at[b * HP + hp, :, rows])
            return _fox_fetch(srcs, (kbuf, vbuf, cbuf), sems, slot)

        @pl.when(n_far > 0)
        def _():
            for cp in fetch(kb0, 0):
                cp.start()

        acc[...] = jnp.zeros_like(acc)
        acc_c[...] = jnp.zeros_like(acc_c)

        def step(k, v, ck_rows, diagonal):
            q, v = q_ref[0], v.astype(BF16)
            dout = do_ref[0].astype(BF16)
            a, c = acc[...], acc_c[...]
            for h, hm in enumerate(_lane_masks()):
                _, t = _fox_scores(q, k, ck_rows[h], hm, diagonal)
                p = jnp.exp(t + (_head_col(cq_ref[0], h) - _head_col(l_ref[0], h)))
                dom = jnp.where(hm, dout, jnp.zeros_like(dout))
                ds = p * (_dot(dom, v, NT) - _head_col(d_ref[0], h))
                a += _dot(ds.astype(BF16), jnp.where(hm, k, jnp.zeros_like(k)), NN)
                c = jnp.where(hm, c + jnp.sum(ds, axis=-1, keepdims=True), c)
            acc[...] = a
            acc_c[...] = c

        def walk(i, carry):
            slot = lax.rem(i, 2)

            @pl.when(i + 1 < n_far)
            def _():
                for cp in fetch(kb0 + i + 1, 1 - slot):
                    cp.start()

            for cp in fetch(kb0 + i, slot):
                cp.wait()
            step(kbuf[slot], vbuf[slot], [cbuf[slot, pl.ds(h, 1), :] for h in range(2)], False)
            return carry

        lax.fori_loop(0, n_far, walk, 0)
        pl.when(kb0 < qb)(lambda: step(kp_ref[0], vp_ref[0], [ckp_ref[0, h:h + 1, :] for h in range(2)], False))
        step(k_ref[0], v_ref[0], [ck_ref[0, h:h + 1, :] for h in range(2)], True)
        dq_ref[0] = acc[...]
        dc_ref[0] = acc_c[...]

    r3 = lambda t: t.reshape(B, S, D)
    out = jax.ShapeDtypeStruct((B, S, D), F32)
    spec = pltpu.PrefetchScalarGridSpec(
        num_scalar_prefetch=1, grid=(B, nb, HP),
        in_specs=_fox_q_specs(5, TB) + _fox_k_specs(TB, HP, 0) + _fox_k_specs(TB, HP, -1) + [_ANY] * 3,
        out_specs=_fox_q_specs(2, TB),
        scratch_shapes=_fox_k_buffers(TB) + [pltpu.VMEM((TB, LANES), F32), pltpu.VMEM((TB, LANES), F32)])
    k3, v3 = r3(kh), kv.reshape(B, S, 2 * D)
    dq, dc = pl.pallas_call(
        body, name=name, grid_spec=spec, out_shape=[out, out], compiler_params=_params("parallel", "parallel", "parallel"),
    )(first, r3(qh), r3(cq), r3(do), r3(lse), r3(delta), k3, v3, ck, k3, v3, ck, k3, v3, ck)
    return dq.reshape(B * S, D), dc.reshape(B * S, D)


def _fox_bwd_dkv(qh, kh, kv, cq, ck, last, do, lse, delta, B, S, D, name):
    HP = D // LANES
    TB = FOX_BLOCK
    nb = S // TB

    def body(last_ref, k_ref, v_ref, ck_ref, q_ref, cq_ref, do_ref, l_ref, d_ref, qn_ref, cqn_ref, don_ref, ln_ref,
             dn_ref, q_hbm, cq_hbm, do_hbm, l_hbm, d_hbm, dk_ref, dv_ref, dc_ref,
             qbuf, cqbuf, dobuf, lbuf, dbuf, sems, dk_acc, dv_acc, dc_acc):
        b, kb, hp = pl.program_id(0), pl.program_id(1), pl.program_id(2)
        qb_last = last_ref[(b * HP + hp) * nb + kb]
        n_far = jnp.maximum(qb_last - kb - 1, 0)

        def fetch(qb, slot):
            rows = pl.ds(pl.multiple_of(qb * TB, TB), TB)
            cols = pl.ds(pl.multiple_of(hp * LANES, LANES), LANES)
            srcs = [t.at[b, rows, cols] for t in (q_hbm, cq_hbm, do_hbm, l_hbm, d_hbm)]
            return _fox_fetch(srcs, (qbuf, cqbuf, dobuf, lbuf, dbuf), sems, slot)

        @pl.when(n_far > 0)
        def _():
            for cp in fetch(kb + 2, 0):
                cp.start()

        dk_acc[...] = jnp.zeros_like(dk_acc)
        dv_acc[...] = jnp.zeros_like(dv_acc)
        dc_acc[...] = jnp.zeros_like(dc_acc)

        def step(q, cqv, dout, lse_t, del_t, diagonal):
            k, v = k_ref[0], v_ref[0].astype(BF16)
            dout = dout.astype(BF16)
            dk, dv, dc = dk_acc[...], dv_acc[...], dc_acc[...]
            row = _iota2((8, TB), 0)
            for h, hm in enumerate(_lane_masks()):
                qm, t = _fox_scores(q, k, ck_ref[0, h:h + 1, :], hm, diagonal)
                p = jnp.exp(t + (_head_col(cqv, h) - _head_col(lse_t, h)))
                dom = jnp.where(hm, dout, jnp.zeros_like(dout))
                ds = p * (_dot(dom, v, NT) - _head_col(del_t, h))
                dv += _dot(p.astype(BF16), dom, TN)
                dk += _dot(ds.astype(BF16), qm, TN)
                dc = jnp.where(row == h, dc - jnp.sum(ds, axis=0, keepdims=True), dc)
            dk_acc[...] = dk
            dv_acc[...] = dv
            dc_acc[...] = dc

        step(q_ref[0], cq_ref[0], do_ref[0], l_ref[0], d_ref[0], True)
        pl.when(kb < qb_last)(lambda: step(qn_ref[0], cqn_ref[0], don_ref[0], ln_ref[0], dn_ref[0], False))

        def walk(i, carry):
            slot = lax.rem(i, 2)

            @pl.when(i + 1 < n_far)
            def _():
                for cp in fetch(kb + 3 + i, 1 - slot):
                    cp.start()

            for cp in fetch(kb + 2 + i, slot):
                cp.wait()
            step(qbuf[slot], cqbuf[slot], dobuf[slot], lbuf[slot], dbuf[slot], False)
            return carry

        lax.fori_loop(0, n_far, walk, 0)
        dk_ref[0] = dk_acc[...]
        dv_ref[0] = dv_acc[...].astype(BF16)
        dc_ref[0] = dc_acc[...]

    blk = (1, TB, LANES)
    kmap = lambda b, i, h, last: (b, i, h)
    vmap = lambda b, i, h, last: (b, i, HP + h)
    cmap = lambda b, i, h, last: (b * HP + h, 0, i)
    nmap = lambda b, i, h, last: (b, jnp.minimum(i + 1, nb - 1), h)
    r3 = lambda t: t.reshape(B, S, D)
    tile = lambda dt: pltpu.VMEM((2, TB, LANES), dt)
    qside = [r3(qh), r3(cq), r3(do), r3(lse), r3(delta)]
    spec = pltpu.PrefetchScalarGridSpec(
        num_scalar_prefetch=1, grid=(B, nb, HP),
        in_specs=[pl.BlockSpec(blk, kmap), pl.BlockSpec(blk, vmap), pl.BlockSpec((1, 8, TB), cmap)]
        + [pl.BlockSpec(blk, kmap)] * 5 + [pl.BlockSpec(blk, nmap)] * 5 + [_ANY] * 5,
        out_specs=[pl.BlockSpec(blk, kmap), pl.BlockSpec(blk, kmap), pl.BlockSpec((1, 8, TB), cmap)],
        scratch_shapes=[tile(BF16), tile(F32), tile(F32), tile(F32), tile(F32), pltpu.SemaphoreType.DMA((2, 5)),
                        pltpu.VMEM((TB, LANES), F32), pltpu.VMEM((TB, LANES), F32), pltpu.VMEM((8, TB), F32)])
    dk, dv, dc = pl.pallas_call(
        body, name=name, grid_spec=spec,
        out_shape=[jax.ShapeDtypeStruct((B, S, D), F32), jax.ShapeDtypeStruct((B, S, D), BF16),
                   jax.ShapeDtypeStruct((B * HP, 8, S), F32)],
        compiler_params=_params("parallel", "parallel", "parallel"),
    )(last, r3(kh), kv.reshape(B, S, 2 * D), ck, *qside, *qside, *qside)
    return dk.reshape(B * S, D), dv.reshape(B * S, D), dc


def _loss_and_grad(y, target, name):
    T, D = y.shape
    tm = _pick(T, 256, 8)

    def body(y_ref, t_ref, p_ref, dy_ref):
        e = y_ref[...] - t_ref[...]
        dy_ref[...] = e * (1.0 / D)
        p_ref[...] = jnp.sum(e * e, axis=0, keepdims=True)[None]

    blk = pl.BlockSpec((tm, D), lambda i: (i, 0))
    return pl.pallas_call(
        body, name=name, grid=(T // tm,), in_specs=[blk, blk],
        out_specs=[pl.BlockSpec((1, 1, D), lambda i: (i, 0, 0)), blk],
        out_shape=[jax.ShapeDtypeStruct((T // tm, 1, D), F32), jax.ShapeDtypeStruct((T, D), F32)],
        compiler_params=_params("parallel"),
    )(y, target)


def _pair_sum(x, got, name):
    _, R, C = x.shape
    tr = _pick(R, max(16, (8 * PACK_ROWS * LANES) // C), 16)

    def body(c_ref, x_ref, g_ref, o_ref):
        o_ref[0] = (x_ref[0].astype(F32) + g_ref[0].astype(F32)).astype(o_ref.dtype)

    spec = pltpu.PrefetchScalarGridSpec(
        num_scalar_prefetch=1, grid=(4, R // tr),
        in_specs=[pl.BlockSpec((1, tr, C), lambda a, i, c: (2 * a + c[0], i, 0)),
                  pl.BlockSpec((1, tr, C), lambda a, i, c: (a, i, 0))],
        out_specs=pl.BlockSpec((1, tr, C), lambda a, i, c: (a, i, 0)))
    return pl.pallas_call(
        body, name=name, grid_spec=spec, out_shape=jax.ShapeDtypeStruct((4, R, C), x.dtype),
        compiler_params=_params("parallel", "parallel"),
    )(lax.axis_index("c").astype(jnp.int32).reshape(1), x, got)


def _adamw(parts, w, m, v, name):
    R, C = w.shape
    P = parts.shape[0]
    tr = _pick(R, max(16, (PACK_ROWS * LANES) // C), 16)

    def body(p_ref, w_ref, m_ref, v_ref, g_out, d_out, m_out, v_out):
        g = p_ref[0].astype(F32)
        for s in range(1, P):
            g = g + p_ref[s].astype(F32)
        mn = ADAM_B1 * m_ref[...] + (1.0 - ADAM_B1) * g
        vn = ADAM_B2 * v_ref[...] + (1.0 - ADAM_B2) * (g * g)
        m_hat = mn / (1.0 - ADAM_B1 ** ADAM_STEP)
        v_hat = vn / (1.0 - ADAM_B2 ** ADAM_STEP)
        g_out[...] = g
        d_out[...] = -ADAM_LR * (m_hat / (jnp.sqrt(v_hat) + ADAM_EPS) + ADAM_WD * w_ref[...])
        m_out[...] = mn
        v_out[...] = vn

    blk = pl.BlockSpec((tr, C), lambda i: (i, 0))
    out = jax.ShapeDtypeStruct((R, C), F32)
    return pl.pallas_call(
        body, name=name, grid=(R // tr,),
        in_specs=[pl.BlockSpec((P, tr, C), lambda i: (0, i, 0)), blk, blk, blk],
        out_specs=[blk] * 4, out_shape=[out] * 4, compiler_params=_params("parallel"),
    )(parts, w, m, v)


def _mesh_pos():
    return lax.axis_index("x"), lax.axis_index("y"), lax.axis_index("c")


def _comm_call(body, xs, out_shapes, name):
    n = len(xs)
    any_spec = pl.BlockSpec(memory_space=pl.ANY)
    return pl.pallas_call(
        body, name=name, out_shape=out_shapes, in_specs=[any_spec] * n, out_specs=[any_spec] * n,
        scratch_shapes=[pltpu.SemaphoreType.DMA((7, n)), pltpu.SemaphoreType.DMA((7, n)), pltpu.SemaphoreType.DMA((n,))],
    )(*xs)


def _all_gather(xs, name):
    n = len(xs)

    def body(*refs):
        start, finish = _gather_phases(refs[:n], refs[n:2 * n], *refs[2 * n:])
        start()
        finish()

    return _comm_call(body, xs, _gathered_shapes(xs), name)


def _gathered_shapes(xs):
    return [jax.ShapeDtypeStruct((N_DEV,) + x.shape, x.dtype) for x in xs]


def _gather_phases(x_refs, out_refs, send_sems, recv_sems, local_sems):
    n = len(x_refs)
    x_, y_, c_ = _mesh_pos()
    me, sibling = (x_, y_, c_), (x_, y_, 1 - c_)
    chips = [(1 - x_, y_), (x_, 1 - y_), (1 - x_, 1 - y_)]

    def slot(t, px, py, pc):
        return out_refs[t].at[4 * px + 2 * py + pc]

    def copy(t, k, block, to, src=None):
        return pltpu.make_async_remote_copy(
            src_ref=slot(t, *block) if src is None else src, dst_ref=slot(t, *block),
            send_sem=send_sems.at[k, t], recv_sem=recv_sems.at[k, t], device_id=to, device_id_type=MESH_ID)

    def own_copies():
        mine = [pltpu.make_async_copy(x_refs[t], slot(t, *me), local_sems.at[t]) for t in range(n)]
        first = [copy(t, 0, me, sibling, src=x_refs[t]) for t in range(n)]
        first += [copy(t, 1 + j, me, (*chip, c_), src=x_refs[t]) for j, chip in enumerate(chips) for t in range(n)]
        return mine, first

    def start():
        mine, first = own_copies()
        for cp in mine + first:
            cp.start()

    def finish():
        mine, first = own_copies()
        passed = []
        for j, chip in enumerate(chips):
            for t in range(n):
                copy(t, 1 + j, (*chip, c_), me).wait_recv()
                passed.append(copy(t, 4 + j, (*chip, c_), sibling))
                passed[-1].start()
        for t in range(n):
            copy(t, 0, sibling, me).wait_recv()
        for j, chip in enumerate(chips):
            for t in range(n):
                copy(t, 4 + j, (*chip, 1 - c_), me).wait_recv()
        for cp in first + passed:
            cp.wait_send()
        for cp in mine:
            cp.wait()

    return start, finish


def _call(body, *, name, grid, in_specs, out_specs, out_shape, scratch_shapes=(), semantics, args, gather=None,
          exchange=None):
    if gather is None and exchange is None:
        return pl.pallas_call(
            body, name=name, grid=grid, in_specs=in_specs, out_specs=out_specs, out_shape=out_shape,
            scratch_shapes=list(scratch_shapes), compiler_params=_params(*semantics))(*args), None
    (phases, shapes), gather = (GATHER, gather) if exchange is None else (CHIP_EXCHANGE, exchange)
    n, n_in, n_sc = len(gather), len(in_specs), len(scratch_shapes)
    outs = list(out_shape) if isinstance(out_shape, (list, tuple)) else [out_shape]
    specs = list(out_specs) if isinstance(out_specs, (list, tuple)) else [out_specs]
    n_out = len(outs)

    def carried(*refs):
        ins, xs = refs[:n_in], refs[n_in:n_in + n]
        ys, gs = refs[n_in + n:n_in + n + n_out], refs[n_in + n + n_out:n_in + 2 * n + n_out]
        scratch, sems = refs[n_in + 2 * n + n_out:n_in + 2 * n + n_out + n_sc], refs[n_in + 2 * n + n_out + n_sc:]
        start, finish = phases(xs, gs, *sems)
        at_first = functools.reduce(jnp.logical_and, [pl.program_id(a) == 0 for a in range(len(grid))])
        at_last = functools.reduce(jnp.logical_and, [pl.program_id(a) == grid[a] - 1 for a in range(len(grid))])
        pl.when(at_first)(start)
        body(*ins, *ys, *scratch)
        pl.when(at_last)(finish)

    results = pl.pallas_call(
        carried, name=name, grid=grid, in_specs=list(in_specs) + [_ANY] * n, out_specs=specs + [_ANY] * n,
        out_shape=outs + shapes(gather),
        scratch_shapes=list(scratch_shapes) + [pltpu.SemaphoreType.DMA((7, n)), pltpu.SemaphoreType.DMA((7, n)),
                                               pltpu.SemaphoreType.DMA((n,))],
        compiler_params=_params(*(["arbitrary"] * len(grid))))(*args, *gather)
    main = results[:n_out]
    return (main if isinstance(out_shape, (list, tuple)) else main[0]), list(results[n_out:])


def _pair_exchange(xs, name):
    n = len(xs)

    def body(*refs):
        x_refs, out_refs = refs[:n], refs[n:2 * n]
        send_sems, recv_sems, _ = refs[2 * n:]
        x_, y_, c_ = _mesh_pos()
        copies = [pltpu.make_async_remote_copy(
            src_ref=x_refs[t].at[2 * a + 1 - c_], dst_ref=out_refs[t].at[a],
            send_sem=send_sems.at[a, t], recv_sem=recv_sems.at[a, t],
            device_id=(x_, y_, 1 - c_), device_id_type=MESH_ID) for a in range(4) for t in range(n)]
        for cp in copies:
            cp.start()
        for cp in copies:
            cp.wait_recv()
        for cp in copies:
            cp.wait_send()

    return _comm_call(body, xs, [jax.ShapeDtypeStruct((4,) + x.shape[1:], x.dtype) for x in xs], name)


def _chip_exchange(xs, name):
    n = len(xs)

    def body(*refs):
        start, finish = _chip_phases(refs[:n], refs[n:2 * n], *refs[2 * n:])
        start()
        finish()

    return _comm_call(body, xs, _same_shapes(xs), name)


def _same_shapes(xs):
    return [jax.ShapeDtypeStruct(x.shape, x.dtype) for x in xs]


def _chip_phases(x_refs, out_refs, send_sems, recv_sems, local_sems):
    n = len(x_refs)
    x_, y_, c_ = _mesh_pos()
    chip = 2 * x_ + y_

    def all_copies():
        mine = [pltpu.make_async_copy(x_refs[t].at[chip], out_refs[t].at[chip], local_sems.at[t]) for t in range(n)]
        copies = []
        for k in range(1, 4):
            px, py = x_ ^ (k >> 1), y_ ^ (k & 1)
            for t in range(n):
                copies.append(pltpu.make_async_remote_copy(
                    src_ref=x_refs[t].at[2 * px + py], dst_ref=out_refs[t].at[chip],
                    send_sem=send_sems.at[k - 1, t], recv_sem=recv_sems.at[k - 1, t],
                    device_id=(px, py, c_), device_id_type=MESH_ID))
        return mine, copies

    def start():
        mine, copies = all_copies()
        for cp in mine + copies:
            cp.start()

    def finish():
        mine, copies = all_copies()
        for cp in copies:
            cp.wait_recv()
        for cp in copies:
            cp.wait_send()
        for cp in mine:
            cp.wait()

    return start, finish


GATHER = (_gather_phases, _gathered_shapes)
CHIP_EXCHANGE = (_chip_phases, _same_shapes)


class _Exchanges:
    class Job:
        def __init__(self, arrays, finish):
            self.arrays, self.finish = arrays, finish

    def __init__(self):
        self.jobs = []

    def add(self, arrays, finish):
        self.jobs.append(self.Job(arrays, finish))

    def take(self):
        return self.jobs.pop(0) if self.jobs else None


def _ffn_fwd(h, norm, w_in, w_out, tag, gather=None):
    T, D = h.shape
    F = w_out.shape[0]
    hn = _rmsnorm_fwd(h, norm, f"{tag}_norm")
    got = None
    if gather is None:
        gate, up, act = _mm_swiglu_fwd(hn, w_in, f"{tag}_in")
    else:
        (gate, up, act), got = _mm_swiglu_fwd(hn, w_in, f"{tag}_in", gather=gather)
    out = _mm(act, w_out, "nn", M=T, N=D, K=F, res=h, scale=0.5, name=f"{tag}_out")
    return out, (h, hn, gate, up, act), got


def _ffn_bwd(dout, saved, norm, w_in, w_out, tag, exchanges=None):
    h, hn, gate, up, act = saved
    T, D = h.shape
    F = w_out.shape[0]
    job = exchanges.take() if exchanges else None
    dg, du, parts = _mm_swiglu_bwd(dout, w_out, gate, up, f"{tag}_dact", exchange=job.arrays if job else None)
    if job:
        job.finish(parts)
    dw_out = _mm(act, dout, "tn", M=F, N=D, K=T, scale=0.5, out_dtype=BF16, name=f"{tag}_dwout")
    dhn = _mm(dg, w_in, "nt", M=T, N=D, K=2 * F, a_rest=(du,), name=f"{tag}_dhn")
    dw_in = _mm(hn, dg, "tn", M=D, N=2 * F, K=T, b2=du, out_dtype=BF16, name=f"{tag}_dwin")
    dh, dnorm = _rmsnorm_bwd(h, norm, dhn, dout, f"{tag}_dnorm")
    return dh, dnorm, dw_in, dw_out


def _local_step(x, positions, target, W, pending=None, exchanges=None, on_grads=None):
    B, S, D = x.shape
    T = B * S
    H, HP = D // HEAD_DIM, D // LANES
    G = {}
    pending = pending or {}
    sending = lambda call: pending[call][0] if call in pending else None

    def arrived(call, got):
        if call in pending:
            pending[call][1](W, got)

    def ready(grads):
        if on_grads is not None:
            on_grads(grads)

    h0 = x.reshape(T, D)
    cs = _rope_tables(positions, "rope_tables")

    h1, ffn00, got = _ffn_fwd(h0, W["ffn_norm"][0, 0], W["ffn_w_in"][0, 0], W["ffn_w_out"][0, 0], "ffn00",
                              gather=sending("ffn00_in"))
    arrived("ffn00_in", got)
    hna = _rmsnorm_fwd(h1, W["mix_norm"][0], "mixa_norm")
    qkv = _mm(hna, W["a_w_qkv"], "nn", M=T, N=9 * D, K=D, name="a_qkv", gather=sending("a_qkv"))
    if sending("a_qkv") is not None:
        qkv, got = qkv
        arrived("a_qkv", got)
    qs, ks, os_, lses = [], [], [], []
    for g in range(3):
        qs.append(_qknorm_fwd(qkv, 3 * g * D, W["a_q_norm"][0, g], cs, D, F32, f"a_qnorm{g}", out_scale=SCALE))
        ks.append(_qknorm_fwd(qkv, (3 * g + 1) * D, W["a_k_norm"][0, g], cs, D, F32, f"a_knorm{g}"))
        o, l, *got = _band_fwd(qs[g], ks[g], qkv, g, B, S, D, f"a_band{g}", gather=sending(f"a_band{g}"))
        arrived(f"a_band{g}", got[0] if got else None)
        os_.append(o)
        lses.append(l)
    mixed, lse_a = _mix(os_, lses, "a_mix")
    h2 = _mm(mixed, W["a_w_o"], "nn", M=T, N=D, K=D, res=h1, name="a_out")
    h3, ffn01, _ = _ffn_fwd(h2, W["ffn_norm"][0, 1], W["ffn_w_in"][0, 1], W["ffn_w_out"][0, 1], "ffn01")

    hnkv = _rmsnorm_fwd(h3, W["kv_norm"], "kv_norm")
    kv = _mm(hnkv, W["kv_w_kv"], "nn", M=T, N=2 * D, K=D, name="kv_proj")
    fpre = _mm(hnkv, W["kv_w_f"], "nn", M=T, N=LANES, K=D, name="kv_gate_proj")
    b_f = jnp.pad(W["kv_b_f"], (0, LANES - H)).reshape(1, LANES)
    kh = _qknorm_fwd(kv, 0, W["kv_k_norm"], None, D, BF16, "kv_knorm")
    cum, cq = _gate_fwd(fpre, b_f, B, S, D, "kv_gate")
    ck = cum.reshape(B, S, LANES)[:, :, :H].reshape(B, S, HP, 2).transpose(0, 2, 3, 1)
    ck = jnp.pad(ck, ((0, 0), (0, 0), (0, 6), (0, 0))).reshape(B * HP, 8, S)
    h4, ffn10, _ = _ffn_fwd(h3, W["ffn_norm"][1, 0], W["ffn_w_in"][1, 0], W["ffn_w_out"][1, 0], "ffn10")
    hnb = _rmsnorm_fwd(h4, W["mix_norm"][1], "mixb_norm")
    qraw = _mm(hnb, W["b_w_q"], "nn", M=T, N=D, K=D, name="b_q")
    qh = _qknorm_fwd(qraw, 0, W["b_q_norm"][0], None, D, BF16, "b_qnorm", out_scale=SCALE)
    first, last = _fox_live_blocks(cum, W["b_q_norm"][0], W["kv_k_norm"], B, S, D)
    ob, lse_b = _fox_fwd(qh, kh, kv, cq, ck, first, B, S, D, "b_fox")
    h5 = _mm(ob, W["b_w_o"], "nn", M=T, N=D, K=D, res=h4, name="b_out")
    h6, ffn11, _ = _ffn_fwd(h5, W["ffn_norm"][1, 1], W["ffn_w_in"][1, 1], W["ffn_w_out"][1, 1], "ffn11")

    sq, dy = _loss_and_grad(h6, target.reshape(T, D), "loss")

    dn = [[None, None], [None, None]]
    dwi = [[None, None], [None, None]]
    dwo = [[None, None], [None, None]]
    dh5, dn[1][1], dwi[1][1], dwo[1][1] = _ffn_bwd(dy, ffn11, W["ffn_norm"][1, 1], W["ffn_w_in"][1, 1], W["ffn_w_out"][1, 1], "ffn11")
    ready({("ffn_w_in", 3): dwi[1][1], ("ffn_w_out", 3): dwo[1][1]})
    dob = _mm(dh5, W["b_w_o"], "nt", M=T, N=D, K=D, name="b_dout")
    G["b_w_o"] = _mm(ob, dh5, "tn", M=D, N=D, K=T, out_dtype=BF16, name="b_dwo")[None]
    delta_b = _head_rowsum(dob, ob, "b_delta")
    dqh, dcq = _fox_bwd_dq(qh, kh, kv, cq, ck, first, dob, lse_b, delta_b, B, S, D, "b_fox_dq")
    dkh, dv_b, dck = _fox_bwd_dkv(qh, kh, kv, cq, ck, last, dob, lse_b, delta_b, B, S, D, "b_fox_dkv")
    dqraw, dg = _qknorm_bwd(qraw, 0, W["b_q_norm"][0], None, dqh, D, "b_dqnorm", out_scale=SCALE)
    G["b_q_norm"] = dg[None]
    dhnb = _mm(dqraw, W["b_w_q"], "nt", M=T, N=D, K=D, name="b_dhn")
    G["b_w_q"] = _mm(hnb, dqraw, "tn", M=D, N=D, K=T, out_dtype=BF16, name="b_dwq")[None]
    dh4, dmix_b = _rmsnorm_bwd(h4, W["mix_norm"][1], dhnb, dh5, "mixb_dnorm")
    dh3, dn[1][0], dwi[1][0], dwo[1][0] = _ffn_bwd(dh4, ffn10, W["ffn_norm"][1, 0], W["ffn_w_in"][1, 0], W["ffn_w_out"][1, 0], "ffn10",
                                                   exchanges)

    dkraw, G["kv_k_norm"] = _qknorm_bwd(kv, 0, W["kv_k_norm"], None, dkh, D, "kv_dknorm")
    dcum = dck.reshape(B, HP, 8, S)[:, :, :2].transpose(0, 3, 1, 2).reshape(T, H)
    dcum = jnp.pad(dcum, ((0, 0), (0, LANES - H)))
    dcum_q = jnp.pad(dcq.reshape(T, H, HEAD_DIM)[:, :, 0], ((0, 0), (0, LANES - H)))
    dfpre, db_f = _gate_bwd(dcum, dcum_q, fpre, b_f, B, S, "kv_dgate")
    G["kv_b_f"] = db_f[0, :H]
    dhnkv = _mm(dkraw, W["kv_w_kv"], "nt", M=T, N=D, K=D, name="kv_dhn_k")
    dhnkv = _mm(dv_b, W["kv_w_kv"], "nt", M=T, N=D, K=D, b_off=(0, D), res=dhnkv, name="kv_dhn_v")
    dhnkv = _mm(dfpre, W["kv_w_f"], "nt", M=T, N=D, K=LANES, res=dhnkv, name="kv_dhn_f")
    dw_k = _mm(hnkv, dkraw, "tn", M=D, N=D, K=T, out_dtype=BF16, name="kv_dwk")
    dw_v = _mm(hnkv, dv_b, "tn", M=D, N=D, K=T, out_dtype=BF16, name="kv_dwv")
    dw_f = _mm(hnkv, dfpre, "tn", M=D, N=LANES, K=T, out_dtype=BF16, name="kv_dwf")
    G["kv_w"] = jnp.concatenate([dw_k, dw_v, dw_f[:, :H]], axis=1)
    dh3, G["kv_norm"] = _rmsnorm_bwd(h3, W["kv_norm"], dhnkv, dh3, "kv_dnorm")
    ready({("ffn_w_in", 2): dwi[1][0], ("ffn_w_out", 2): dwo[1][0], "kv_w": G["kv_w"], "b_w_q": G["b_w_q"][0],
           "b_w_o": G["b_w_o"][0]})

    dh2, dn[0][1], dwi[0][1], dwo[0][1] = _ffn_bwd(dh3, ffn01, W["ffn_norm"][0, 1], W["ffn_w_in"][0, 1], W["ffn_w_out"][0, 1], "ffn01",
                                                   exchanges)
    dmixed = _mm(dh2, W["a_w_o"], "nt", M=T, N=D, K=D, name="a_dmixed")
    G["a_w_o"] = _mm(mixed, dh2, "tn", M=D, N=D, K=T, out_dtype=BF16, name="a_dwo")[None]
    ready({("ffn_w_in", 1): dwi[0][1], ("ffn_w_out", 1): dwo[0][1], "a_w_o": G["a_w_o"][0]})
    delta_a = _head_rowsum(dmixed, mixed, "a_delta")
    dhna = None
    dw_qkv, dqn, dkn = [], [], []
    for g in range(3):
        dq, dk, dv = _band_bwd(qs[g], ks[g], qkv, dmixed, lse_a, delta_a, g, B, S, D, f"a_dband{g}", exchanges)
        dqr, dgq = _qknorm_bwd(qkv, 3 * g * D, W["a_q_norm"][0, g], cs, dq, D, f"a_dqnorm{g}", out_scale=SCALE)
        dkr, dgk = _qknorm_bwd(qkv, (3 * g + 1) * D, W["a_k_norm"][0, g], cs, dk, D, f"a_dknorm{g}")
        dqn.append(dgq)
        dkn.append(dgk)
        dhna = _mm(dqr, W["a_w_qkv"], "nt", M=T, N=D, K=3 * D, a_rest=(dkr, dv), b_off=(0, 3 * g * D), res=dhna,
                   name=f"a_dhn{g}")
        for j, d in enumerate((dqr, dkr, dv)):
            dw_qkv.append(_mm(hna, d, "tn", M=D, N=D, K=T, out_dtype=BF16, name=f"a_dwqkv{3 * g + j}"))
    G["a_w_qkv"] = jnp.concatenate(dw_qkv, axis=1)[None]
    ready({"a_w_qkv": G["a_w_qkv"][0]})
    G["a_q_norm"] = jnp.stack(dqn)[None]
    G["a_k_norm"] = jnp.stack(dkn)[None]
    dh1, dmix_a = _rmsnorm_bwd(h1, W["mix_norm"][0], dhna, dh2, "mixa_dnorm")
    dh0, dn[0][0], dwi[0][0], dwo[0][0] = _ffn_bwd(dh1, ffn00, W["ffn_norm"][0, 0], W["ffn_w_in"][0, 0], W["ffn_w_out"][0, 0], "ffn00",
                                                   exchanges)
    ready({("ffn_w_in", 0): dwi[0][0], ("ffn_w_out", 0): dwo[0][0]})

    G["mix_norm"] = jnp.stack([dmix_a, dmix_b])
    G["ffn_norm"] = jnp.stack([jnp.stack(r) for r in dn])
    G["ffn_w_in"] = jnp.stack([jnp.stack(r) for r in dwi])
    G["ffn_w_out"] = jnp.stack([jnp.stack(r) for r in dwo])
    return jnp.sum(sq), dh0.reshape(B, S, D), G


SMALL =["mix_norm", "kv_norm", "a_q_norm", "a_k_norm", "kv_b_f", "kv_k_norm", "b_q_norm"]


def _pack(blocks, lead):
    flat = [b.reshape(b.shape[:lead] + (-1,)) for b in blocks]
    flat = jnp.concatenate(flat, axis=lead)
    n = flat.shape[-1]
    unit = PACK_ROWS * LANES
    padded = -(-n // unit) * unit if n > unit else -(-n // (16 * LANES)) * 16 * LANES
    flat = jnp.pad(flat, [(0, 0)] * lead + [(0, padded - n)])
    return flat.reshape(flat.shape[:lead] + (padded // LANES, LANES))


def _unpack(packed, shapes, lead):
    flat = packed.reshape(packed.shape[:lead] + (-1,))
    out, off = [], 0
    for shp in shapes:
        n = int(np.prod(shp))
        out.append(flat[..., off:off + n].reshape(packed.shape[:lead] + tuple(shp)))
        off += n
    return out


def kernel(x, positions, ffn_norm, ffn_w_in, ffn_w_out, mix_norm, a_w_qkv, a_q_norm, a_k_norm, a_w_o, kv_norm, kv_w, kv_b_f, kv_k_norm, b_w_q, b_q_norm, b_w_o, loss_target, m_ffn_norm, m_ffn_w_in, m_ffn_w_out, m_mix_norm, m_a_w_qkv, m_a_q_norm, m_a_k_norm, m_a_w_o, m_kv_norm, m_kv_w, m_kv_b_f, m_kv_k_norm, m_b_w_q, m_b_q_norm, m_b_w_o, v_ffn_norm, v_ffn_w_in, v_ffn_w_out, v_mix_norm, v_a_w_qkv, v_a_q_norm, v_a_k_norm, v_a_w_o, v_kv_norm, v_kv_w, v_kv_b_f, v_kv_k_norm, v_b_w_q, v_b_q_norm, v_b_w_o):
    names = ["ffn_norm", "ffn_w_in", "ffn_w_out", "mix_norm", "a_w_qkv", "a_q_norm", "a_k_norm", "a_w_o", "kv_norm",
             "kv_w", "kv_b_f", "kv_k_norm", "b_w_q", "b_q_norm", "b_w_o"]
    w = dict(zip(names, (ffn_norm, ffn_w_in, ffn_w_out, mix_norm, a_w_qkv, a_q_norm, a_k_norm, a_w_o, kv_norm, kv_w,
                         kv_b_f, kv_k_norm, b_w_q, b_q_norm, b_w_o)))
    m = dict(zip(names, (m_ffn_norm, m_ffn_w_in, m_ffn_w_out, m_mix_norm, m_a_w_qkv, m_a_q_norm, m_a_k_norm, m_a_w_o,
                         m_kv_norm, m_kv_w, m_kv_b_f, m_kv_k_norm, m_b_w_q, m_b_q_norm, m_b_w_o)))
    v = dict(zip(names, (v_ffn_norm, v_ffn_w_in, v_ffn_w_out, v_mix_norm, v_a_w_qkv, v_a_q_norm, v_a_k_norm, v_a_w_o,
                         v_kv_norm, v_kv_w, v_kv_b_f, v_kv_k_norm, v_b_w_q, v_b_q_norm, v_b_w_o)))
    B, S, D = x.shape
    H = D // HEAD_DIM
    F = ffn_w_out.shape[2] * N_DEV
    Ds, Fs = D // N_DEV, F // N_DEV
    me = 4 * lax.axis_index("x") + 2 * lax.axis_index("y") + lax.axis_index("c")
    squares = ("a_w_o", "b_w_q", "b_w_o")

    def shard_views(t):
        return [t["ffn_w_in"].reshape(4 * D, -1), t["ffn_w_out"].reshape(-1, D), t["a_w_qkv"][0], t["kv_w"],
                jnp.concatenate([t[n][0] for n in squares], axis=0)]

    def to_shards(views):
        sq3 = views[4].reshape(3, 1, Ds, D)
        out = {"ffn_w_in": views[0].reshape(ffn_w_in.shape), "ffn_w_out": views[1].reshape(ffn_w_out.shape),
               "a_w_qkv": views[2][None], "kv_w": views[3]}
        out.update({n: sq3[i] for i, n in enumerate(squares)})
        return out

    sh_in, sh_out, sh_qkv, sh_kv, sh_sq = [s.astype(BF16) for s in shard_views(w)]
    in_of, out_of = (lambda f: sh_in[f * D:(f + 1) * D]), (lambda f: sh_out[f * Fs:(f + 1) * Fs])
    square_of = lambda i: sh_sq[i * Ds:(i + 1) * Ds]
    side_by_side = lambda g: g.transpose(1, 0, 2).reshape(g.shape[1], N_DEV * g.shape[2])
    stacked = lambda g: g.reshape(N_DEV * g.shape[1], g.shape[2])

    def store_ffn(f):
        def store(W, got):
            W["ffn_w_in"][f // 2, f % 2], W["ffn_w_out"][f // 2, f % 2] = side_by_side(got[0]), stacked(got[1])
        return store

    def store_mixer_a(W, got):
        W["a_w_qkv"], W["a_w_o"] = side_by_side(got[0]), stacked(got[1])

    def store_layer_b(W, got):
        store_ffn(1)(W, got)
        kv_full = side_by_side(got[2])
        W["kv_w_kv"] = kv_full[:, :2 * D]
        W["kv_w_f"] = jnp.pad(kv_full[:, 2 * D:], ((0, 0), (0, LANES - H)))
        W["b_w_q"], W["b_w_o"] = stacked(got[3]), stacked(got[4])

    norm_rows = jnp.pad(ffn_norm.reshape(4, Ds), ((0, 4), (0, 0)))
    first = _all_gather([in_of(0), out_of(0), norm_rows], "gather_weights")
    W = {n: w[n] for n in SMALL}
    W["ffn_norm"] = first[2][:, :4].transpose(1, 0, 2).reshape(2, 2, D)
    W["ffn_w_in"], W["ffn_w_out"] = {}, {}
    store_ffn(0)(W, first)
    pending = {"ffn00_in": ([sh_qkv, square_of(0)], store_mixer_a),
               "a_qkv": ([in_of(1), out_of(1), sh_kv, square_of(1), square_of(2)], store_layer_b),
               "a_band0": ([in_of(2), out_of(2)], store_ffn(2)),
               "a_band1": ([in_of(3), out_of(3)], store_ffn(3))}

    wmv = [shard_views(t) for t in (w, m, v)]
    columns = lambda g: g.reshape(g.shape[0], N_DEV, -1).transpose(1, 0, 2)
    rows = lambda g: g.reshape(N_DEV, g.shape[0] // N_DEV, g.shape[1])
    square_index = {n: i for i, n in enumerate(squares)}

    def placed(key):
        name, f = key if isinstance(key, tuple) else (key, None)
        if name == "ffn_w_in":
            return columns, 0, slice(f * D, (f + 1) * D)
        if name == "ffn_w_out":
            return rows, 1, slice(f * Fs, (f + 1) * Fs)
        if name in square_index:
            return rows, 4, slice(square_index[name] * Ds, (square_index[name] + 1) * Ds)
        return columns, {"a_w_qkv": 2, "kv_w": 3}[name], slice(None)

    exchanges = _Exchanges()
    updated = [{} for _ in wmv[0]]
    groups = []

    def on_grads(grads):
        tag = str(len(groups))
        groups.append(tag)
        keys = list(grads)
        blocks = [placed(k)[0](grads[k]).astype(BF16) for k in keys]
        got = _pair_exchange(blocks, "exchange_pair" + tag)
        sums = [_pair_sum(b, s, f"pair_sum{tag}_{i}") for i, (b, s) in enumerate(zip(blocks, got))]

        def finish(parts):
            for i, k in enumerate(keys):
                _, view, span = placed(k)
                outs = _adamw(parts[i], *(t[view][span] for t in wmv), f"adamw{tag}_{i}")
                updated[view][span.start or 0] = outs

        exchanges.add(sums, finish)

    sq, grad_x, G = _local_step(x, positions, loss_target, W, pending, exchanges, on_grads)
    loss = lax.psum(sq * (0.5 / D), ("x", "y", "c"))
    while exchanges.jobs:
        job = exchanges.take()
        job.finish(_chip_exchange(job.arrays, "exchange_chips" + str(len(exchanges.jobs))))
    new = {}
    for k, kind in enumerate(("grad", "delta", "new_m", "new_v")):
        whole = [jnp.concatenate([u[r][k] for r in sorted(u)], axis=0) for u in updated]
        for n, arr in to_shards(whole).items():
            new[kind, n] = arr

    small_all = ["ffn_norm"] + SMALL
    vec_parts = _all_gather([_pack([G[n] for n in small_all], 0)], "gather_vector_grads")[0]
    vec_parts = _unpack(vec_parts, [G[n].shape for n in small_all], 1)
    norm_g = lax.dynamic_slice_in_dim(vec_parts[0], me * ffn_norm.shape[2], ffn_norm.shape[2], axis=3)
    rep_g = _pack(vec_parts[1:], 1)
    for ns, parts_s in ((["ffn_norm"], _pack([norm_g], 1)), (SMALL, rep_g)):
        outs = _adamw(parts_s, *(_pack([t[n] for n in ns], 0) for t in (w, m, v)), "adamw_" + ns[0])
        for kind, packed in zip(("grad", "delta", "new_m", "new_v"), outs):
            for n, arr in zip(ns, _unpack(packed, [w[n].shape for n in ns], 0)):
                new[kind, n] = arr

    result = [loss, grad_x]
    for kind in ("grad", "delta", "new_m", "new_v"):
        result += [new[kind, n] for n in names]
    return tuple(result)
```

```python
import functools

import jax
import jax.numpy as jnp
import numpy as np
from jax import lax
from jax.experimental import pallas as pl
from jax.experimental.pallas import tpu as pltpu

F32 = jnp.float32
BF16 = jnp.bfloat16
HIGHEST = lax.Precision.HIGHEST

HEAD_DIM = 64
ROT_DIM = HEAD_DIM // 4
ROPE_THETA = 500000.0
DILATED_GROUPS = ((128, 1), (512, 4), (2048, 16))
BAND_BLOCK = 128
EPS = 1e-6
ADAM_LR, ADAM_B1, ADAM_B2, ADAM_EPS, ADAM_WD, ADAM_STEP = 0.001, 0.9, 0.999, 1e-08, 0.01, 10

LANES = 128
N_DEV = 8
VMEM_LIMIT = 56 * 1024 * 1024
NEG = -1e30
SCALE = HEAD_DIM ** -0.5
PACK_ROWS = 1024
MESH_ID = pl.DeviceIdType.MESH


def _params(*sem):
    return pltpu.CompilerParams(dimension_semantics=sem, vmem_limit_bytes=VMEM_LIMIT)


def _divs(n, cap, mult=LANES):
    d = [t for t in range(mult, min(n, cap) + 1, mult) if n % t == 0]
    return d or [n]


def _pick(n, cap, mult=LANES):
    return _divs(n, cap, mult)[-1]


def _mm_tiles(M, N, K, budget=1408 * 1024):
    best = None
    for tm in _divs(M, 1408):
        for tn in _divs(N, 1408):
            if tm * tn <= budget and (best is None or tm * tn > best[0] * best[1]):
                best = (tm, tn)
    if best is None:
        best = (_divs(M, 1408)[0], _divs(N, 1408)[0])
    tm, tn = best
    tks = [t for t in _divs(K, 2048) if (tm + tn) * t <= 3 * 1024 * 1024]
    tk = tks[-1] if tks else _divs(K, 2048)[0]
    return tm, tn, tk


def _dot(a, b, dims, precision=None):
    return lax.dot_general(a, b, (dims, ((), ())), precision=precision, preferred_element_type=F32)


NN = ((1,), (0,))
NT = ((1,), (1,))
TN = ((0,), (0,))


def _dot_split(x, mat):
    hi = x.astype(BF16)
    lo = (x - hi.astype(F32)).astype(BF16)
    m = mat.astype(BF16)
    return _dot(hi, m, NN) + _dot(lo, m, NN)


def _dot_once(x, mat):
    return _dot(x.astype(BF16), mat.astype(BF16), NN)


def _mm(a, b, mode, *, M, N, K, a_off=(0, 0), b_off=(0, 0), a_rest=(), b2=None, res=None, scale=1.0, out_dtype=F32,
        name, gather=None, exchange=None):
    a_parts = 1 + len(a_rest)
    tm, tn, tk = _mm_tiles(M, N // 2 if b2 is not None else N, K // a_parts)
    nk = K // tk
    k_part, n_half = nk // a_parts, (N // tn) // 2
    if mode == "tn":
        a_blk, b_blk, dims = (tk, tm), (tk, tn), TN
    elif mode == "nt":
        a_blk, b_blk, dims = (tm, tk), (tn, tk), NT
    else:
        a_blk, b_blk, dims = (tm, tk), (tk, tn), NN
    for off, blk in ((a_off, a_blk), (b_off, b_blk)):
        assert off[0] % blk[0] == 0 and off[1] % blk[1] == 0, (name, off, blk)
    ao = (a_off[0] // a_blk[0], a_off[1] // a_blk[1])
    bo = (b_off[0] // b_blk[0], b_off[1] // b_blk[1])
    if mode == "tn":
        a_map = lambda i, j, k: (k + ao[0], i + ao[1])
        b_map = lambda i, j, k: (k + bo[0], j + bo[1])
    elif mode == "nt":
        a_map = lambda i, j, k: (i + ao[0], k + ao[1])
        b_map = lambda i, j, k: (j + bo[0], k + bo[1])
    else:
        a_map = lambda i, j, k: (i + ao[0], k + ao[1])
        b_map = lambda i, j, k: (k + bo[0], j + bo[1])
    has_res = res is not None
    if a_rest:
        assert mode != "tn" and a_off == (0, 0)
        part_map = lambda p: (lambda i, j, k: (i, jnp.clip(k - p * k_part, 0, k_part - 1)))
        a_map = part_map(0)
    if b2 is not None:
        assert mode != "nt" and b_off == (0, 0)
        b_map = lambda i, j, k: (k, jnp.minimum(j, n_half - 1))
        b2_map = lambda i, j, k: (k, jnp.maximum(j - n_half, 0))

    def body(*refs):
        refs = list(refs)
        a_ref, b_ref = refs.pop(0), refs.pop(0)
        a_val, b_val = a_ref[...].astype(BF16), b_ref[...].astype(BF16)
        for p in range(1, a_parts):
            a_val = jnp.where(pl.program_id(2) >= p * k_part, refs.pop(0)[...].astype(BF16), a_val)
        if b2 is not None:
            b_val = jnp.where(pl.program_id(1) < n_half, b_val, refs.pop(0)[...].astype(BF16))
        r_ref = refs.pop(0) if has_res else None
        o_ref = refs.pop(0)

        def finish(total):
            out = total * scale
            if has_res:
                out = r_ref[...] + out
            o_ref[...] = out.astype(out_dtype)

        part = _dot(a_val, b_val, dims)
        if nk == 1:
            finish(part)
            return
        acc = refs[-1]
        k = pl.program_id(2)

        @pl.when(k == 0)
        def _():
            acc[...] = part

        @pl.when(k > 0)
        def _():
            acc[...] += part

        pl.when(k == nk - 1)(lambda: finish(acc[...]))

    in_specs = [pl.BlockSpec(a_blk, a_map), pl.BlockSpec(b_blk, b_map)]
    args = [a, b]
    for p, extra in enumerate(a_rest, 1):
        in_specs.append(pl.BlockSpec(a_blk, part_map(p)))
        args.append(extra)
    if b2 is not None:
        in_specs.append(pl.BlockSpec(b_blk, b2_map))
        args.append(b2)
    if has_res:
        in_specs.append(pl.BlockSpec((tm, tn), lambda i, j, k: (i, j)))
        args.append(res)
    out, gathered = _call(
        body, name=name, grid=(M // tm, N // tn, nk), in_specs=in_specs,
        out_specs=pl.BlockSpec((tm, tn), lambda i, j, k: (i, j)),
        out_shape=jax.ShapeDtypeStruct((M, N), out_dtype),
        scratch_shapes=[pltpu.VMEM((tm, tn), F32)] if nk > 1 else [],
        semantics=("parallel", "parallel", "arbitrary"), args=args, gather=gather, exchange=exchange)
    return out if gather is None and exchange is None else (out, gathered)


def _rmsnorm_fwd(x, g, name):
    T, D = x.shape
    tm = _pick(T, 256, 8)

    def body(x_ref, g_ref, y_ref):
        xv = x_ref[...]
        r = lax.rsqrt(jnp.mean(xv * xv, axis=-1, keepdims=True) + EPS)
        y_ref[...] = (xv * r * g_ref[...]).astype(BF16)

    return pl.pallas_call(
        body, name=name, grid=(T // tm,),
        in_specs=[pl.BlockSpec((tm, D), lambda i: (i, 0)), pl.BlockSpec((1, D), lambda i: (0, 0))],
        out_specs=pl.BlockSpec((tm, D), lambda i: (i, 0)),
        out_shape=jax.ShapeDtypeStruct((T, D), BF16), compiler_params=_params("parallel"),
    )(x, g.reshape(1, D))


def _rmsnorm_bwd(x, g, dy, dres, name):
    T, D = x.shape
    tm = _pick(T, 256, 8)

    def body(x_ref, g_ref, dy_ref, dres_ref, dx_ref, dg_ref):
        xv = x_ref[...]
        r = lax.rsqrt(jnp.mean(xv * xv, axis=-1, keepdims=True) + EPS)
        xn = xv * r
        d = dy_ref[...]
        dgy = d * g_ref[...]
        dx_ref[...] = dres_ref[...] + r * (dgy - xn * jnp.mean(dgy * xn, axis=-1, keepdims=True))

        @pl.when(pl.program_id(0) == 0)
        def _():
            dg_ref[...] = jnp.zeros_like(dg_ref)

        dg_ref[...] += jnp.sum(d * xn, axis=0, keepdims=True)

    row = pl.BlockSpec((tm, D), lambda i: (i, 0))
    vec = pl.BlockSpec((1, D), lambda i: (0, 0))
    dx, dg = pl.pallas_call(
        body, name=name, grid=(T // tm,), in_specs=[row, vec, row, row], out_specs=[row, vec],
        out_shape=[jax.ShapeDtypeStruct((T, D), F32), jax.ShapeDtypeStruct((1, D), F32)],
        compiler_params=_params("arbitrary"),
    )(x, g.reshape(1, D), dy, dres)
    return dx, dg.reshape(D)


def _sigmoid(x):
    return 1.0 / (1.0 + jnp.exp(-x))


MXU_WIDTH = 256


def _col_chunks(n):
    return [(lo, min(lo + MXU_WIDTH, n)) for lo in range(0, n, MXU_WIDTH)]


def _mm_swiglu_fwd(hn, w_in, name, gather=None):
    T, D = hn.shape
    F = w_in.shape[1] // 2
    tm, tn, _ = _mm_tiles(T, F, D)
    assert (tm + 2 * tn) * D <= 4 * 1024 * 1024, "the whole contraction is held in one step"
    nf = F // tn

    def body(a_ref, bg_ref, bu_ref, g_ref, u_ref, act_ref):
        a = a_ref[...].astype(BF16)
        for lo, hi in _col_chunks(tn):
            gv = _dot(a, bg_ref[:, lo:hi].astype(BF16), NN)
            uv = _dot(a, bu_ref[:, lo:hi].astype(BF16), NN)
            g_ref[:, lo:hi] = gv.astype(BF16)
            u_ref[:, lo:hi] = uv.astype(BF16)
            act_ref[:, lo:hi] = (gv * _sigmoid(gv) * uv).astype(BF16)

    tile = pl.BlockSpec((tm, tn), lambda j, i: (i, j))
    out = jax.ShapeDtypeStruct((T, F), BF16)
    outs, gathered = _call(
        body, name=name, grid=(nf, T // tm),
        in_specs=[pl.BlockSpec((tm, D), lambda j, i: (i, 0)), pl.BlockSpec((D, tn), lambda j, i: (0, j)),
                  pl.BlockSpec((D, tn), lambda j, i: (0, j + nf))],
        out_specs=[tile, tile, tile], out_shape=[out, out, out], semantics=("parallel", "parallel"),
        args=(hn, w_in, w_in), gather=gather)
    return outs if gather is None else (outs, gathered)


def _mm_swiglu_bwd(dout, w_out, gate, up, name, exchange=None):
    T, D = dout.shape
    F = w_out.shape[0]
    tm, tn, _ = _mm_tiles(T, F, D)
    assert (tm + tn) * D <= 4 * 1024 * 1024, "the whole contraction is held in one step"

    def body(a_ref, b_ref, g_ref, u_ref, dg_ref, du_ref):
        a = a_ref[...].astype(BF16)
        for lo, hi in _col_chunks(tn):
            d = _dot(a, b_ref[lo:hi, :].astype(BF16), NT) * 0.5
            gv, uv = g_ref[:, lo:hi].astype(F32), u_ref[:, lo:hi].astype(F32)
            s = _sigmoid(gv)
            gs = gv * s
            dg_ref[:, lo:hi] = (d * uv * (s + gs - gs * s)).astype(BF16)
            du_ref[:, lo:hi] = (d * gs).astype(BF16)

    tile = pl.BlockSpec((tm, tn), lambda j, i: (i, j))
    out = jax.ShapeDtypeStruct((T, F), BF16)
    (dg, du), parts = _call(
        body, name=name, grid=(F // tn, T // tm),
        in_specs=[pl.BlockSpec((tm, D), lambda j, i: (i, 0)), pl.BlockSpec((tn, D), lambda j, i: (j, 0)), tile, tile],
        out_specs=[tile, tile], out_shape=[out, out], semantics=("parallel", "parallel"),
        args=(dout, w_out, gate, up), exchange=exchange)
    return dg, du, parts


def _iota2(shape, dim):
    return lax.broadcasted_iota(jnp.int32, shape, dim)


def _head_ones():
    r, c = _iota2((LANES, LANES), 0), _iota2((LANES, LANES), 1)
    return ((r >> 6) == (c >> 6)).astype(F32)


def _rot_matrix(transpose):
    r, c = _iota2((LANES, LANES), 0), _iota2((LANES, LANES), 1)
    if transpose:
        r, c = c, r
    half = ROT_DIM // 2
    cm = c & (HEAD_DIM - 1)
    neg = (r == c + half) & (cm < half)
    pos = (r == c - half) & (cm >= half) & (cm < ROT_DIM)
    return pos.astype(F32) - neg.astype(F32)


def _rope_tables(positions, name):
    T = positions.size
    tm = _pick(T, 1024, 8)
    inv = ROPE_THETA ** (-jnp.arange(0, ROT_DIM, 2, dtype=F32) / ROT_DIM)
    lane = jnp.concatenate([inv, inv, jnp.zeros((HEAD_DIM - ROT_DIM,), F32)])
    lane = jnp.tile(lane, 2).reshape(1, LANES)

    def body(p_ref, f_ref, c_ref, s_ref):
        ang = p_ref[...].astype(F32) * f_ref[...]
        c_ref[...] = jnp.cos(ang)
        s_ref[...] = jnp.sin(ang)

    out = jax.ShapeDtypeStruct((T, LANES), F32)
    blk = pl.BlockSpec((tm, LANES), lambda i: (i, 0))
    return pl.pallas_call(
        body, name=name, grid=(T // tm,),
        in_specs=[pl.BlockSpec((tm, 1), lambda i: (i, 0)), pl.BlockSpec((1, LANES), lambda i: (0, 0))],
        out_specs=[blk, blk], out_shape=[out, out], compiler_params=_params("parallel"),
    )(positions.reshape(T, 1), lane)


def _qknorm_fwd(src, col_off, gain, cs, D, out_dtype, name, out_scale=1.0):
    T = src.shape[0]
    HP = D // LANES
    tm = _pick(T, 1024, 8)
    co = col_off // LANES
    rope = cs is not None

    def body(*refs):
        if rope:
            x_ref, g_ref, c_ref, s_ref, o_ref = refs
        else:
            x_ref, g_ref, o_ref = refs
        xv = x_ref[...]
        ms = _dot_once(xv * xv, _head_ones()) * (1.0 / HEAD_DIM)
        y = xv * lax.rsqrt(ms + EPS) * g_ref[...]
        if rope:
            y = y * c_ref[...] + _dot_once(y, _rot_matrix(False)) * s_ref[...]
        o_ref[...] = (y * out_scale).astype(out_dtype)

    in_specs = [pl.BlockSpec((tm, LANES), lambda i, h: (i, co + h)), pl.BlockSpec((1, LANES), lambda i, h: (0, 0))]
    args = [src, jnp.tile(gain.reshape(1, HEAD_DIM), (1, 2))]
    if rope:
        in_specs += [pl.BlockSpec((tm, LANES), lambda i, h: (i, 0))] * 2
        args += list(cs)
    return pl.pallas_call(
        body, name=name, grid=(T // tm, HP), in_specs=in_specs,
        out_specs=pl.BlockSpec((tm, LANES), lambda i, h: (i, h)),
        out_shape=jax.ShapeDtypeStruct((T, D), out_dtype), compiler_params=_params("parallel", "parallel"),
    )(*args)


def _qknorm_bwd(src, col_off, gain, cs, dout, D, name, out_scale=1.0):
    T = src.shape[0]
    HP = D // LANES
    tm = _pick(T, 1024, 8)
    co = col_off // LANES
    rope = cs is not None

    def body(*refs):
        if rope:
            x_ref, g_ref, d_ref, c_ref, s_ref, dx_ref, dg_ref = refs
        else:
            x_ref, g_ref, d_ref, dx_ref, dg_ref = refs
        xv = x_ref[...]
        ones = _head_ones()
        ms = _dot_once(xv * xv, ones) * (1.0 / HEAD_DIM)
        r = lax.rsqrt(ms + EPS)
        xn = xv * r
        d = d_ref[...] * out_scale
        if rope:
            d = d * c_ref[...] + _dot_once(d * s_ref[...], _rot_matrix(True))
        dgy = d * g_ref[...]
        mean = _dot_once(dgy * xn, ones) * (1.0 / HEAD_DIM)
        dx_ref[...] = (r * (dgy - xn * mean)).astype(BF16)

        @pl.when(pl.program_id(1) == 0)
        def _():
            dg_ref[...] = jnp.zeros_like(dg_ref)

        dg_ref[...] += jnp.sum(d * xn, axis=0, keepdims=True)[None]

    in_specs = [pl.BlockSpec((tm, LANES), lambda h, i: (i, co + h)), pl.BlockSpec((1, LANES), lambda h, i: (0, 0)),
                pl.BlockSpec((tm, LANES), lambda h, i: (i, h))]
    args = [src, jnp.tile(gain.reshape(1, HEAD_DIM), (1, 2)), dout]
    if rope:
        in_specs += [pl.BlockSpec((tm, LANES), lambda h, i: (i, 0))] * 2
        args += list(cs)
    dx, dg = pl.pallas_call(
        body, name=name, grid=(HP, T // tm), in_specs=in_specs,
        out_specs=[pl.BlockSpec((tm, LANES), lambda h, i: (i, h)), pl.BlockSpec((1, 1, LANES), lambda h, i: (h, 0, 0))],
        out_shape=[jax.ShapeDtypeStruct((T, D), BF16), jax.ShapeDtypeStruct((HP, 1, LANES), F32)],
        compiler_params=_params("parallel", "arbitrary"),
    )(*args)
    return dx, dg.reshape(2 * HP, HEAD_DIM).sum(axis=0)


def _head_rowsum(a, b, name):
    T, D = a.shape
    tm = _pick(T, 1024, 8)

    def body(a_ref, b_ref, o_ref):
        o_ref[...] = _dot_split(a_ref[...] * b_ref[...], _head_ones())

    blk = pl.BlockSpec((tm, LANES), lambda i, h: (i, h))
    return pl.pallas_call(
        body, name=name, grid=(T // tm, D // LANES), in_specs=[blk, blk], out_specs=blk,
        out_shape=jax.ShapeDtypeStruct((T, D), F32), compiler_params=_params("parallel", "parallel"),
    )(a, b)


def _lane_masks():
    lane = _iota2((1, LANES), 1)
    return (lane < HEAD_DIM, lane >= HEAD_DIM)


BAND_GROUP = 4


def _band_geometry(g, S):
    window, dil = DILATED_GROUPS[g]
    chunk = BAND_BLOCK * dil
    assert window // dil == BAND_BLOCK and S % chunk == 0
    if dil == 1:
        assert (S // chunk) % BAND_GROUP == 0
        return dil, chunk, BAND_GROUP, 1
    assert dil % BAND_GROUP == 0
    return dil, chunk, 1, dil // BAND_GROUP


def _band_rows(i, rho, dil, shift=0):
    if dil > 1:
        return pl.ds(rho * BAND_GROUP + i, BAND_BLOCK, stride=dil), shift != 0
    j = i + shift
    outside = j < 0 or j >= BAND_GROUP
    return pl.ds(0 if outside else j * BAND_BLOCK, BAND_BLOCK), outside


def _stack_heads(x):
    hm0, hm1 = _lane_masks()
    zero = jnp.zeros_like(x)
    return jnp.concatenate([jnp.where(hm0, x, zero), jnp.where(hm1, x, zero)], axis=0)


def _unstack_heads(x2):
    hm0, _ = _lane_masks()
    return jnp.where(hm0, x2[:BAND_BLOCK], x2[BAND_BLOCK:])


def _band_masks():
    qi = _iota2((2 * BAND_BLOCK, 2 * BAND_BLOCK), 0) & (BAND_BLOCK - 1)
    kj = _iota2((2 * BAND_BLOCK, 2 * BAND_BLOCK), 1)
    own = (kj >= BAND_BLOCK) & (kj - BAND_BLOCK <= qi)
    prev = (kj < BAND_BLOCK) & (kj >= qi)
    return own, prev


def _band_fwd(qh, kh, qkv, g, B, S, D, name, gather=None):
    dil, CH, NB, RG = _band_geometry(g, S)
    HP = D // LANES
    vo = (3 * g + 2) * HP
    nsp = S // (CH * NB)

    def body(q_ref, k_ref, kp_ref, v_ref, vp_ref, o_ref, lse_ref):
        n, rho = pl.program_id(1), pl.program_id(3)
        own, prev = _band_masks()
        for i in range(BAND_GROUP):
            rows, _ = _band_rows(i, rho, dil)
            prows, outside = _band_rows(i, rho, dil, shift=-1)
            q, kc, vc = (t[0, rows, :].astype(BF16) for t in (q_ref, k_ref, v_ref))
            kp, vp = (t[0, prows, :].astype(BF16) for t in ((kp_ref, vp_ref) if outside else (k_ref, v_ref)))
            valid = (own | (prev & (n > 0))) if outside else (own | prev)
            s = jnp.where(valid, _dot(_stack_heads(q), jnp.concatenate([kp, kc], axis=0), NT), NEG)
            m = jnp.max(s, axis=-1, keepdims=True)
            e = jnp.exp(s - m)
            l = jnp.sum(e, axis=-1, keepdims=True)
            o2 = _dot((e * (1.0 / l)).astype(BF16), jnp.concatenate([vp, vc], axis=0), NN)
            o_ref[0, rows, :] = _unstack_heads(o2)
            lse_ref[0, rows, :] = _unstack_heads(m + jnp.log(l))

    span, chunk = (1, CH * NB, LANES), (1, CH, LANES)
    cur = lambda b, n, h, r: (b, n, h)
    prev = lambda b, n, h, r: (b, jnp.maximum(n * NB - 1, 0), h)
    vcur = lambda b, n, h, r: (b, n, vo + h)
    vprev = lambda b, n, h, r: (b, jnp.maximum(n * NB - 1, 0), vo + h)
    out = jax.ShapeDtypeStruct((B, S, D), F32)
    q3, k3, v3 = qh.reshape(B, S, D), kh.reshape(B, S, D), qkv.reshape(B, S, 9 * D)
    (o, lse), gathered = _call(
        body, name=name, grid=(B, nsp, HP, RG),
        in_specs=[pl.BlockSpec(span, cur), pl.BlockSpec(span, cur), pl.BlockSpec(chunk, prev),
                  pl.BlockSpec(span, vcur), pl.BlockSpec(chunk, vprev)],
        out_specs=[pl.BlockSpec(span, cur), pl.BlockSpec(span, cur)], out_shape=[out, out],
        semantics=("parallel", "parallel", "parallel", "arbitrary"), args=(q3, k3, k3, v3, v3), gather=gather)
    o, lse = o.reshape(B * S, D), lse.reshape(B * S, D)
    return (o, lse) if gather is None else (o, lse, gathered)


def _head_col(x, h):
    lane = _iota2((1, LANES), 1)
    return jnp.sum(jnp.where(lane == h * HEAD_DIM, x, 0.0), axis=-1, keepdims=True)


def _band_bwd(qh, kh, qkv, do, lse, delta, g, B, S, D, name, exchanges=None):
    dil, CH, NB, RG = _band_geometry(g, S)
    HP = D // LANES
    vo = (3 * g + 2) * HP
    nsp, nch = S // (CH * NB), S // CH

    def stacked_cols(tile):
        return jnp.concatenate([_head_col(tile, 0), _head_col(tile, 1)], axis=0)

    def body(q_ref, qn_ref, do_ref, don_ref, l_ref, ln_ref, d_ref, dn_ref, k_ref, kp_ref, v_ref, vp_ref,
             dq_ref, dk_ref, dv_ref):
        n, rho = pl.program_id(1), pl.program_id(3)
        own, prev = _band_masks()
        band = (_iota2((2 * BAND_BLOCK, BAND_BLOCK), 1) >= (_iota2((2 * BAND_BLOCK, BAND_BLOCK), 0) & (BAND_BLOCK - 1)))
        for i in range(BAND_GROUP):
            rows, _ = _band_rows(i, rho, dil)
            prows, p_out = _band_rows(i, rho, dil, shift=-1)
            nrows, n_out = _band_rows(i, rho, dil, shift=1)
            kc, vc = (t[0, rows, :].astype(BF16) for t in (k_ref, v_ref))
            kp, vp = (t[0, prows, :].astype(BF16) for t in ((kp_ref, vp_ref) if p_out else (k_ref, v_ref)))
            q2, do2 = (_stack_heads(t[0, rows, :].astype(BF16)) for t in (q_ref, do_ref))
            qn2, don2 = (_stack_heads(t[0, nrows, :].astype(BF16)) for t in ((qn_ref, don_ref) if n_out else (q_ref, do_ref)))
            lse2, del2 = (stacked_cols(t[0, rows, :]) for t in (l_ref, d_ref))
            lsen2, deln2 = (stacked_cols(t[0, nrows, :]) for t in ((ln_ref, dn_ref) if n_out else (l_ref, d_ref)))
            valid = (own | (prev & (n > 0))) if p_out else (own | prev)
            next_ok = (band & (n < nsp - 1)) if n_out else band
            k2, v2 = jnp.concatenate([kp, kc], axis=0), jnp.concatenate([vp, vc], axis=0)
            p = jnp.exp(jnp.where(valid, _dot(q2, k2, NT) - lse2, NEG))
            ds = (p * (_dot(do2, v2, NT) - del2)).astype(BF16)
            pc = jnp.exp(jnp.where(next_ok, _dot(qn2, kc, NT) - lsen2, NEG))
            dsc = (pc * (_dot(don2, vc, NT) - deln2)).astype(BF16)
            dq_ref[0, rows, :] = _unstack_heads(_dot(ds, k2, NN))
            dk_ref[0, rows, :] = _dot(ds[:, BAND_BLOCK:], q2, TN) + _dot(dsc, qn2, TN)
            dv_ref[0, rows, :] = _dot(p[:, BAND_BLOCK:].astype(BF16), do2, TN) + _dot(pc.astype(BF16), don2, TN)

    span, chunk = (1, CH * NB, LANES), (1, CH, LANES)
    cur = lambda b, n, h, r: (b, n, h)
    prev = lambda b, n, h, r: (b, jnp.maximum(n * NB - 1, 0), h)
    nxt = lambda b, n, h, r: (b, jnp.minimum((n + 1) * NB, nch - 1), h)
    vcur = lambda b, n, h, r: (b, n, vo + h)
    vprev = lambda b, n, h, r: (b, jnp.maximum(n * NB - 1, 0), vo + h)
    q3, k3, do3, l3, d3 = (t.reshape(B, S, D) for t in (qh, kh, do, lse, delta))
    v3 = qkv.reshape(B, S, 9 * D)
    sp, ch = (lambda m: pl.BlockSpec(span, m)), (lambda m: pl.BlockSpec(chunk, m))
    out = jax.ShapeDtypeStruct((B, S, D), F32)
    job = exchanges.take() if exchanges else None
    (dq, dk, dv), parts = _call(
        body, name=name, grid=(B, nsp, HP, RG),
        in_specs=[sp(cur), ch(nxt), sp(cur), ch(nxt), sp(cur), ch(nxt), sp(cur), ch(nxt),
                  sp(cur), ch(prev), sp(vcur), ch(vprev)],
        out_specs=[sp(cur)] * 3, out_shape=[out] * 3, semantics=("parallel", "parallel", "parallel", "arbitrary"),
        args=(q3, q3, do3, do3, l3, l3, d3, d3, k3, k3, v3, v3), exchange=job.arrays if job else None)
    if job:
        job.finish(parts)
    return dq.reshape(B * S, D), dk.reshape(B * S, D), dv.reshape(B * S, D)


def _mix(os_, lses, name):
    T, D = os_[0].shape
    tm = _pick(T, 256, 8)

    def body(o0, o1, o2, l0, l1, l2, mix_ref, lse_ref):
        a, b, c = l0[...], l1[...], l2[...]
        m = jnp.maximum(jnp.maximum(a, b), c)
        ea, eb, ec = jnp.exp(a - m), jnp.exp(b - m), jnp.exp(c - m)
        s = ea + eb + ec
        mix_ref[...] = (ea / s) * o0[...] + (eb / s) * o1[...] + (ec / s) * o2[...]
        lse_ref[...] = m + jnp.log(s)

    blk = pl.BlockSpec((tm, D), lambda i: (i, 0))
    out = jax.ShapeDtypeStruct((T, D), F32)
    return pl.pallas_call(
        body, name=name, grid=(T // tm,), in_specs=[blk] * 6, out_specs=[blk, blk], out_shape=[out, out],
        compiler_params=_params("parallel"),
    )(*os_, *lses)


def _gate_fwd(fpre, b_f, B, S, D, name):
    nb = S // LANES

    def body(f_ref, b_ref, cum_ref, rep_ref, carry):
        @pl.when(pl.program_id(1) == 0)
        def _():
            carry[...] = jnp.zeros_like(carry)

        z = f_ref[...] + b_ref[...]
        lf = jnp.minimum(z, 0.0) - jnp.log(1.0 + jnp.exp(-jnp.abs(z)))
        tri = (_iota2((LANES, LANES), 0) >= _iota2((LANES, LANES), 1)).astype(F32)
        c = _dot(tri, lf, NN, HIGHEST) + carry[...]
        cum_ref[...] = c
        carry[...] += jnp.sum(lf, axis=0, keepdims=True)
        spread = ((_iota2((LANES, D), 1) >> 6) == _iota2((LANES, D), 0)).astype(F32)
        rep_ref[...] = _dot(c, spread, NN, HIGHEST)

    return pl.pallas_call(
        body, name=name, grid=(B, nb),
        in_specs=[pl.BlockSpec((LANES, LANES), lambda b, j: (b * nb + j, 0)), pl.BlockSpec((1, LANES), lambda b, j: (0, 0))],
        out_specs=[pl.BlockSpec((LANES, LANES), lambda b, j: (b * nb + j, 0)), pl.BlockSpec((LANES, D), lambda b, j: (b * nb + j, 0))],
        out_shape=[jax.ShapeDtypeStruct((B * S, LANES), F32), jax.ShapeDtypeStruct((B * S, D), F32)],
        scratch_shapes=[pltpu.VMEM((1, LANES), F32)], compiler_params=_params("arbitrary", "arbitrary"),
    )(fpre, b_f)


def _gate_bwd(dcum_k, dcum_q, fpre, b_f, B, S, name):
    nb = S // LANES

    def body(d_ref, dq_ref, f_ref, b_ref, o_ref, db_ref, carry):
        first = (pl.program_id(0) == 0) & (pl.program_id(1) == 0)

        @pl.when(pl.program_id(1) == 0)
        def _():
            carry[...] = jnp.zeros_like(carry)

        @pl.when(first)
        def _():
            db_ref[...] = jnp.zeros_like(db_ref)

        d = d_ref[...] + dq_ref[...]
        tri = (_iota2((LANES, LANES), 0) <= _iota2((LANES, LANES), 1)).astype(F32)
        dlf = _dot(tri, d, NN, HIGHEST) + carry[...]
        carry[...] += jnp.sum(d, axis=0, keepdims=True)
        z = f_ref[...] + b_ref[...]
        dz = dlf * (1.0 / (1.0 + jnp.exp(z)))
        o_ref[...] = dz.astype(BF16)
        db_ref[...] += jnp.sum(dz, axis=0, keepdims=True)

    rev = lambda b, j: (b * nb + nb - 1 - j, 0)
    blk = pl.BlockSpec((LANES, LANES), rev)
    vec = pl.BlockSpec((1, LANES), lambda b, j: (0, 0))
    return pl.pallas_call(
        body, name=name, grid=(B, nb), in_specs=[blk, blk, blk, vec], out_specs=[blk, vec],
        out_shape=[jax.ShapeDtypeStruct((B * S, LANES), BF16), jax.ShapeDtypeStruct((1, LANES), F32)],
        scratch_shapes=[pltpu.VMEM((1, LANES), F32)], compiler_params=_params("arbitrary", "arbitrary"),
    )(dcum_k, dcum_q, fpre, b_f)


def _fox_scores(q, k, ck_row, hm, diagonal):
    TB = q.shape[0]
    qm = jnp.where(hm, q, jnp.zeros_like(q))
    t = _dot(qm, k, NT) - ck_row
    if diagonal:
        t = jnp.where(_iota2((TB, TB), 1) <= _iota2((TB, TB), 0), t, NEG)
    return qm, t


def _fox_fetch(srcs, bufs, sems, slot):
    return [pltpu.make_async_copy(src, buf.at[slot], sems.at[slot, i]) for i, (src, buf) in enumerate(zip(srcs, bufs))]


FOX_BLOCK = 256
EXP_UNDERFLOW = 104.0


def _fox_live_blocks(cum, gain_q, gain_k, B, S, D):
    H, HP = D // HEAD_DIM, D // LANES
    nb = S // FOX_BLOCK
    bound = 1.02 * HEAD_DIM * SCALE * jnp.max(jnp.abs(gain_q)) * jnp.max(jnp.abs(gain_k))
    c = cum.reshape(B, S, LANES)[:, :, :H]
    decay = c[:, 0::FOX_BLOCK, None, :] - c[:, None, FOX_BLOCK - 1::FOX_BLOCK, :]
    below = jnp.arange(nb)[None, :] < jnp.arange(nb)[:, None]
    dead = (decay < -(EXP_UNDERFLOW + 2.0 * bound)) & below[None, :, :, None]
    live = jnp.logical_not(dead.reshape(B, nb, nb, HP, 2).all(axis=-1))
    first = jnp.argmax(live, axis=2)
    last = nb - 1 - jnp.argmax(live[:, ::-1], axis=1)
    to_table = lambda t: t.transpose(0, 2, 1).reshape(B * HP * nb).astype(jnp.int32)
    return to_table(first), to_table(last)


def _fox_fwd(qh, kh, kv, cq, ck, first, B, S, D, name):
    HP = D // LANES
    TB = FOX_BLOCK
    nb = S // TB

    def body(first_ref, q_ref, cq_ref, k_ref, v_ref, ck_ref, kp_ref, vp_ref, ckp_ref, k_hbm, v_hbm, ck_hbm,
             o_ref, lse_ref, kbuf, vbuf, cbuf, sems, m_sc, l_sc, acc):
        b, qb, hp = pl.program_id(0), pl.program_id(1), pl.program_id(2)
        kb0 = first_ref[(b * HP + hp) * nb + qb]
        n_far = jnp.maximum(qb - 1 - kb0, 0)

        def fetch(kb, slot):
            rows = pl.ds(pl.multiple_of(kb * TB, TB), TB)
            srcs = (k_hbm.at[b, rows, pl.ds(pl.multiple_of(hp * LANES, LANES), LANES)],
                    v_hbm.at[b, rows, pl.ds(pl.multiple_of((HP + hp) * LANES, LANES), LANES)],
                    ck_hbm.at[b * HP + hp, :, rows])
            return _fox_fetch(srcs, (kbuf, vbuf, cbuf), sems, slot)

        @pl.when(n_far > 0)
        def _():
            for cp in fetch(kb0, 0):
                cp.start()

        m_sc[...] = jnp.full_like(m_sc, NEG)
        l_sc[...] = jnp.zeros_like(l_sc)
        acc[...] = jnp.zeros_like(acc)

        def step(k, v, ck_rows, diagonal):
            q, v = q_ref[0], v.astype(BF16)
            cqv = cq_ref[0]
            a = acc[...]
            for h, hm in enumerate(_lane_masks()):
                _, t = _fox_scores(q, k, ck_rows[h], hm, diagonal)
                cqh = _head_col(cqv, h)
                m_prev = jnp.max(m_sc[h], axis=-1, keepdims=True)
                l_prev = jnp.max(l_sc[h], axis=-1, keepdims=True)
                m_new = jnp.maximum(m_prev, jnp.max(t, axis=-1, keepdims=True) + cqh)
                alpha = jnp.exp(m_prev - m_new)
                p = jnp.exp(t + (cqh - m_new))
                l_new = alpha * l_prev + jnp.sum(p, axis=-1, keepdims=True)
                a = jnp.where(hm, alpha * a + _dot(p.astype(BF16), v, NN), a)
                m_sc[h] = jnp.broadcast_to(m_new, (TB, LANES))
                l_sc[h] = jnp.broadcast_to(l_new, (TB, LANES))
            acc[...] = a

        def walk(i, carry):
            slot = lax.rem(i, 2)

            @pl.when(i + 1 < n_far)
            def _():
                for cp in fetch(kb0 + i + 1, 1 - slot):
                    cp.start()

            for cp in fetch(kb0 + i, slot):
                cp.wait()
            step(kbuf[slot], vbuf[slot], [cbuf[slot, pl.ds(h, 1), :] for h in range(2)], False)
            return carry

        lax.fori_loop(0, n_far, walk, 0)
        pl.when(kb0 < qb)(lambda: step(kp_ref[0], vp_ref[0], [ckp_ref[0, h:h + 1, :] for h in range(2)], False))
        step(k_ref[0], v_ref[0], [ck_ref[0, h:h + 1, :] for h in range(2)], True)
        a = acc[...]
        out = a
        lse = jnp.zeros_like(a)
        for h, hm in enumerate(_lane_masks()):
            out = jnp.where(hm, a / l_sc[h], out)
            lse = jnp.where(hm, m_sc[h] + jnp.log(l_sc[h]), lse)
        o_ref[0] = out
        lse_ref[0] = lse

    out = jax.ShapeDtypeStruct((B, S, D), F32)
    spec = pltpu.PrefetchScalarGridSpec(
        num_scalar_prefetch=1, grid=(B, nb, HP),
        in_specs=_fox_q_specs(2, TB) + _fox_k_specs(TB, HP, 0) + _fox_k_specs(TB, HP, -1) + [_ANY] * 3,
        out_specs=_fox_q_specs(2, TB), scratch_shapes=_fox_k_buffers(TB) + [
            pltpu.VMEM((2, TB, LANES), F32), pltpu.VMEM((2, TB, LANES), F32), pltpu.VMEM((TB, LANES), F32)])
    k3, v3 = kh.reshape(B, S, D), kv.reshape(B, S, 2 * D)
    o, lse = pl.pallas_call(
        body, name=name, grid_spec=spec, out_shape=[out, out], compiler_params=_params("parallel", "parallel", "parallel"),
    )(first, qh.reshape(B, S, D), cq.reshape(B, S, D), k3, v3, ck, k3, v3, ck, k3, v3, ck)
    return o.reshape(B * S, D), lse.reshape(B * S, D)


_ANY = pl.BlockSpec(memory_space=pl.ANY)


def _fox_q_specs(n, TB):
    return [pl.BlockSpec((1, TB, LANES), lambda b, i, h, table: (b, i, h))] * n


def _fox_k_specs(TB, HP, shift):
    blk = lambda i: jnp.maximum(i + shift, 0)
    return [pl.BlockSpec((1, TB, LANES), lambda b, i, h, table: (b, blk(i), h)),
            pl.BlockSpec((1, TB, LANES), lambda b, i, h, table: (b, blk(i), HP + h)),
            pl.BlockSpec((1, 8, TB), lambda b, i, h, table: (b * HP + h, 0, blk(i)))]


def _fox_k_buffers(TB):
    return [pltpu.VMEM((2, TB, LANES), BF16), pltpu.VMEM((2, TB, LANES), F32), pltpu.VMEM((2, 8, TB), F32),
            pltpu.SemaphoreType.DMA((2, 3))]


def _fox_bwd_dq(qh, kh, kv, cq, ck, first, do, lse, delta, B, S, D, name):
    HP = D // LANES
    TB = FOX_BLOCK
    nb = S // TB

    def body(first_ref, q_ref, cq_ref, do_ref, l_ref, d_ref, k_ref, v_ref, ck_ref, kp_ref, vp_ref, ckp_ref,
             k_hbm, v_hbm, ck_hbm, dq_ref, dc_ref, kbuf, vbuf, cbuf, sems, acc, acc_c):
        b, qb, hp = pl.program_id(0), pl.program_id(1), pl.program_id(2)
        kb0 = first_ref[(b * HP + hp) * nb + qb]
        n_far = jnp.maximum(qb - 1 - kb0, 0)

        def fetch(kb, slot):
            rows = pl.ds(pl.multiple_of(kb * TB, TB), TB)
            srcs = (k_hbm.at[b, rows, pl.ds(pl.multiple_of(hp * LANES, LANES), LANES)],
                    v_hbm.at[b, rows, pl.ds(pl.multiple_of((HP + hp) * LANES, LANES), LANES)],
                    ck_hbm.at[b * HP + hp, :, rows])
            return _fox_fetch(srcs, (kbuf, vbuf, cbuf), sems, slot)

        @pl.when(n_far > 0)
        def _():
            for cp in fetch(kb0, 0):
                cp.start()

        acc[...] = jnp.zeros_like(acc)
        acc_c[...] = jnp.zeros_like(acc_c)

        def step(k, v, ck_rows, diagonal):
            q, v = q_ref[0], v.astype(BF16)
            dout = do_ref[0].astype(BF16)
            a, c = acc[...], acc_c[...]
            for h, hm in enumerate(_lane_masks()):
                _, t = _fox_scores(q, k, ck_rows[h], hm, diagonal)
                p = jnp.exp(t + (_head_col(cq_ref[0], h) - _head_col(l_ref[0], h)))
                dom = jnp.where(hm, dout, jnp.zeros_like(dout))
                ds = p * (_dot(dom, v, NT) - _head_col(d_ref[0], h))
                a += _dot(ds.astype(BF16), jnp.where(hm, k, jnp.zeros_like(k)), NN)
                c = jnp.where(hm, c + jnp.sum(ds, axis=-1, keepdims=True), c)
            acc[...] = a
            acc_c[...] = c

        def walk(i, carry):
            slot = lax.rem(i, 2)

            @pl.when(i + 1 < n_far)
            def _():
                for cp in fetch(kb0 + i + 1, 1 - slot):
                    cp.start()

            for cp in fetch(kb0 + i, slot):
                cp.wait()
            step(kbuf[slot], vbuf[slot], [cbuf[slot, pl.ds(h, 1), :] for h in range(2)], False)
            return carry

        lax.fori_loop(0, n_far, walk, 0)
        pl.when(kb0 < qb)(lambda: step(kp_ref[0], vp_ref[0], [ckp_ref[0, h:h + 1, :] for h in range(2)], False))
        step(k_ref[0], v_ref[0], [ck_ref[0, h:h + 1, :] for h in range(2)], True)
        dq_ref[0] = acc[...]
        dc_ref[0] = acc_c[...]

    r3 = lambda t: t.reshape(B, S, D)
    out = jax.ShapeDtypeStruct((B, S, D), F32)
    spec = pltpu.PrefetchScalarGridSpec(
        num_scalar_prefetch=1, grid=(B, nb, HP),
        in_specs=_fox_q_specs(5, TB) + _fox_k_specs(TB, HP, 0) + _fox_k_specs(TB, HP, -1) + [_ANY] * 3,
        out_specs=_fox_q_specs(2, TB),
        scratch_shapes=_fox_k_buffers(TB) + [pltpu.VMEM((TB, LANES), F32), pltpu.VMEM((TB, LANES), F32)])
    k3, v3 = r3(kh), kv.reshape(B, S, 2 * D)
    dq, dc = pl.pallas_call(
        body, name=name, grid_spec=spec, out_shape=[out, out], compiler_params=_params("parallel", "parallel", "parallel"),
    )(first, r3(qh), r3(cq), r3(do), r3(lse), r3(delta), k3, v3, ck, k3, v3, ck, k3, v3, ck)
    return dq.reshape(B * S, D), dc.reshape(B * S, D)


def _fox_bwd_dkv(qh, kh, kv, cq, ck, last, do, lse, delta, B, S, D, name):
    HP = D // LANES
    TB = FOX_BLOCK
    nb = S // TB

    def body(last_ref, k_ref, v_ref, ck_ref, q_ref, cq_ref, do_ref, l_ref, d_ref, qn_ref, cqn_ref, don_ref, ln_ref,
             dn_ref, q_hbm, cq_hbm, do_hbm, l_hbm, d_hbm, dk_ref, dv_ref, dc_ref,
             qbuf, cqbuf, dobuf, lbuf, dbuf, sems, dk_acc, dv_acc, dc_acc):
        b, kb, hp = pl.program_id(0), pl.program_id(1), pl.program_id(2)
        qb_last = last_ref[(b * HP + hp) * nb + kb]
        n_far = jnp.maximum(qb_last - kb - 1, 0)

        def fetch(qb, slot):
            rows = pl.ds(pl.multiple_of(qb * TB, TB), TB)
            cols = pl.ds(pl.multiple_of(hp * LANES, LANES), LANES)
            srcs = [t.at[b, rows, cols] for t in (q_hbm, cq_hbm, do_hbm, l_hbm, d_hbm)]
            return _fox_fetch(srcs, (qbuf, cqbuf, dobuf, lbuf, dbuf), sems, slot)

        @pl.when(n_far > 0)
        def _():
            for cp in fetch(kb + 2, 0):
                cp.start()

        dk_acc[...] = jnp.zeros_like(dk_acc)
        dv_acc[...] = jnp.zeros_like(dv_acc)
        dc_acc[...] = jnp.zeros_like(dc_acc)

        def step(q, cqv, dout, lse_t, del_t, diagonal):
            k, v = k_ref[0], v_ref[0].astype(BF16)
            dout = dout.astype(BF16)
            dk, dv, dc = dk_acc[...], dv_acc[...], dc_acc[...]
            row = _iota2((8, TB), 0)
            for h, hm in enumerate(_lane_masks()):
                qm, t = _fox_scores(q, k, ck_ref[0, h:h + 1, :], hm, diagonal)
                p = jnp.exp(t + (_head_col(cqv, h) - _head_col(lse_t, h)))
                dom = jnp.where(hm, dout, jnp.zeros_like(dout))
                ds = p * (_dot(dom, v, NT) - _head_col(del_t, h))
                dv += _dot(p.astype(BF16), dom, TN)
                dk += _dot(ds.astype(BF16), qm, TN)
                dc = jnp.where(row == h, dc - jnp.sum(ds, axis=0, keepdims=True), dc)
            dk_acc[...] = dk
            dv_acc[...] = dv
            dc_acc[...] = dc

        step(q_ref[0], cq_ref[0], do_ref[0], l_ref[0], d_ref[0], True)
        pl.when(kb < qb_last)(lambda: step(qn_ref[0], cqn_ref[0], don_ref[0], ln_ref[0], dn_ref[0], False))

        def walk(i, carry):
            slot = lax.rem(i, 2)

            @pl.when(i + 1 < n_far)
            def _():
                for cp in fetch(kb + 3 + i, 1 - slot):
                    cp.start()

            for cp in fetch(kb + 2 + i, slot):
                cp.wait()
            step(qbuf[slot], cqbuf[slot], dobuf[slot], lbuf[slot], dbuf[slot], False)
            return carry

        lax.fori_loop(0, n_far, walk, 0)
        dk_ref[0] = dk_acc[...]
        dv_ref[0] = dv_acc[...].astype(BF16)
        dc_ref[0] = dc_acc[...]

    blk = (1, TB, LANES)
    kmap = lambda b, i, h, last: (b, i, h)
    vmap = lambda b, i, h, last: (b, i, HP + h)
    cmap = lambda b, i, h, last: (b * HP + h, 0, i)
    nmap = lambda b, i, h, last: (b, jnp.minimum(i + 1, nb - 1), h)
    r3 = lambda t: t.reshape(B, S, D)
    tile = lambda dt: pltpu.VMEM((2, TB, LANES), dt)
    qside = [r3(qh), r3(cq), r3(do), r3(lse), r3(delta)]
    spec = pltpu.PrefetchScalarGridSpec(
        num_scalar_prefetch=1, grid=(B, nb, HP),
        in_specs=[pl.BlockSpec(blk, kmap), pl.BlockSpec(blk, vmap), pl.BlockSpec((1, 8, TB), cmap)]
        + [pl.BlockSpec(blk, kmap)] * 5 + [pl.BlockSpec(blk, nmap)] * 5 + [_ANY] * 5,
        out_specs=[pl.BlockSpec(blk, kmap), pl.BlockSpec(blk, kmap), pl.BlockSpec((1, 8, TB), cmap)],
        scratch_shapes=[tile(BF16), tile(F32), tile(F32), tile(F32), tile(F32), pltpu.SemaphoreType.DMA((2, 5)),
                        pltpu.VMEM((TB, LANES), F32), pltpu.VMEM((TB, LANES), F32), pltpu.VMEM((8, TB), F32)])
    dk, dv, dc = pl.pallas_call(
        body, name=name, grid_spec=spec,
        out_shape=[jax.ShapeDtypeStruct((B, S, D), F32), jax.ShapeDtypeStruct((B, S, D), BF16),
                   jax.ShapeDtypeStruct((B * HP, 8, S), F32)],
        compiler_params=_params("parallel", "parallel", "parallel"),
    )(last, r3(kh), kv.reshape(B, S, 2 * D), ck, *qside, *qside, *qside)
    return dk.reshape(B * S, D), dv.reshape(B * S, D), dc


def _loss_and_grad(y, target, name):
    T, D = y.shape
    tm = _pick(T, 256, 8)

    def body(y_ref, t_ref, p_ref, dy_ref):
        e = y_ref[...] - t_ref[...]
        dy_ref[...] = e * (1.0 / D)
        p_ref[...] = jnp.sum(e * e, axis=0, keepdims=True)[None]

    blk = pl.BlockSpec((tm, D), lambda i: (i, 0))
    return pl.pallas_call(
        body, name=name, grid=(T // tm,), in_specs=[blk, blk],
        out_specs=[pl.BlockSpec((1, 1, D), lambda i: (i, 0, 0)), blk],
        out_shape=[jax.ShapeDtypeStruct((T // tm, 1, D), F32), jax.ShapeDtypeStruct((T, D), F32)],
        compiler_params=_params("parallel"),
    )(y, target)


def _pair_sum(x, got, name):
    _, R, C = x.shape
    tr = _pick(R, max(16, (8 * PACK_ROWS * LANES) // C), 16)

    def body(c_ref, x_ref, g_ref, o_ref):
        o_ref[0] = (x_ref[0].astype(F32) + g_ref[0].astype(F32)).astype(o_ref.dtype)

    spec = pltpu.PrefetchScalarGridSpec(
        num_scalar_prefetch=1, grid=(4, R // tr),
        in_specs=[pl.BlockSpec((1, tr, C), lambda a, i, c: (2 * a + c[0], i, 0)),
                  pl.BlockSpec((1, tr, C), lambda a, i, c: (a, i, 0))],
        out_specs=pl.BlockSpec((1, tr, C), lambda a, i, c: (a, i, 0)))
    return pl.pallas_call(
        body, name=name, grid_spec=spec, out_shape=jax.ShapeDtypeStruct((4, R, C), x.dtype),
        compiler_params=_params("parallel", "parallel"),
    )(lax.axis_index("c").astype(jnp.int32).reshape(1), x, got)


def _adamw(parts, w, m, v, name):
    R, C = w.shape
    P = parts.shape[0]
    tr = _pick(R, max(16, (PACK_ROWS * LANES) // C), 16)

    def body(p_ref, w_ref, m_ref, v_ref, g_out, d_out, m_out, v_out):
        g = p_ref[0].astype(F32)
        for s in range(1, P):
            g = g + p_ref[s].astype(F32)
        mn = ADAM_B1 * m_ref[...] + (1.0 - ADAM_B1) * g
        vn = ADAM_B2 * v_ref[...] + (1.0 - ADAM_B2) * (g * g)
        m_hat = mn / (1.0 - ADAM_B1 ** ADAM_STEP)
        v_hat = vn / (1.0 - ADAM_B2 ** ADAM_STEP)
        g_out[...] = g
        d_out[...] = -ADAM_LR * (m_hat / (jnp.sqrt(v_hat) + ADAM_EPS) + ADAM_WD * w_ref[...])
        m_out[...] = mn
        v_out[...] = vn

    blk = pl.BlockSpec((tr, C), lambda i: (i, 0))
    out = jax.ShapeDtypeStruct((R, C), F32)
    return pl.pallas_call(
        body, name=name, grid=(R // tr,),
        in_specs=[pl.BlockSpec((P, tr, C), lambda i: (0, i, 0)), blk, blk, blk],
        out_specs=[blk] * 4, out_shape=[out] * 4, compiler_params=_params("parallel"),
    )(parts, w, m, v)


def _mesh_pos():
    return lax.axis_index("x"), lax.axis_index("y"), lax.axis_index("c")


def _comm_call(body, xs, out_shapes, name):
    n = len(xs)
    any_spec = pl.BlockSpec(memory_space=pl.ANY)
    return pl.pallas_call(
        body, name=name, out_shape=out_shapes, in_specs=[any_spec] * n, out_specs=[any_spec] * n,
        scratch_shapes=[pltpu.SemaphoreType.DMA((7, n)), pltpu.SemaphoreType.DMA((7, n)), pltpu.SemaphoreType.DMA((n,))],
    )(*xs)


def _all_gather(xs, name):
    n = len(xs)

    def body(*refs):
        start, finish = _gather_phases(refs[:n], refs[n:2 * n], *refs[2 * n:])
        start()
        finish()

    return _comm_call(body, xs, _gathered_shapes(xs), name)


def _gathered_shapes(xs):
    return [jax.ShapeDtypeStruct((N_DEV,) + x.shape, x.dtype) for x in xs]


def _gather_phases(x_refs, out_refs, send_sems, recv_sems, local_sems):
    n = len(x_refs)
    x_, y_, c_ = _mesh_pos()
    me, sibling = (x_, y_, c_), (x_, y_, 1 - c_)
    chips = [(1 - x_, y_), (x_, 1 - y_), (1 - x_, 1 - y_)]

    def slot(t, px, py, pc):
        return out_refs[t].at[4 * px + 2 * py + pc]

    def copy(t, k, block, to, src=None):
        return pltpu.make_async_remote_copy(
            src_ref=slot(t, *block) if src is None else src, dst_ref=slot(t, *block),
            send_sem=send_sems.at[k, t], recv_sem=recv_sems.at[k, t], device_id=to, device_id_type=MESH_ID)

    def own_copies():
        mine = [pltpu.make_async_copy(x_refs[t], slot(t, *me), local_sems.at[t]) for t in range(n)]
        first = [copy(t, 0, me, sibling, src=x_refs[t]) for t in range(n)]
        first += [copy(t, 1 + j, me, (*chip, c_), src=x_refs[t]) for j, chip in enumerate(chips) for t in range(n)]
        return mine, first

    def start():
        mine, first = own_copies()
        for cp in mine + first:
            cp.start()

    def finish():
        mine, first = own_copies()
        passed = []
        for j, chip in enumerate(chips):
            for t in range(n):
                copy(t, 1 + j, (*chip, c_), me).wait_recv()
                passed.append(copy(t, 4 + j, (*chip, c_), sibling))
                passed[-1].start()
        for t in range(n):
            copy(t, 0, sibling, me).wait_recv()
        for j, chip in enumerate(chips):
            for t in range(n):
                copy(t, 4 + j, (*chip, 1 - c_), me).wait_recv()
        for cp in first + passed:
            cp.wait_send()
        for cp in mine:
            cp.wait()

    return start, finish


def _call(body, *, name, grid, in_specs, out_specs, out_shape, scratch_shapes=(), semantics, args, gather=None,
          exchange=None):
    if gather is None and exchange is None:
        return pl.pallas_call(
            body, name=name, grid=grid, in_specs=in_specs, out_specs=out_specs, out_shape=out_shape,
            scratch_shapes=list(scratch_shapes), compiler_params=_params(*semantics))(*args), None
    (phases, shapes), gather = (GATHER, gather) if exchange is None else (CHIP_EXCHANGE, exchange)
    n, n_in, n_sc = len(gather), len(in_specs), len(scratch_shapes)
    outs = list(out_shape) if isinstance(out_shape, (list, tuple)) else [out_shape]
    specs = list(out_specs) if isinstance(out_specs, (list, tuple)) else [out_specs]
    n_out = len(outs)

    def carried(*refs):
        ins, xs = refs[:n_in], refs[n_in:n_in + n]
        ys, gs = refs[n_in + n:n_in + n + n_out], refs[n_in + n + n_out:n_in + 2 * n + n_out]
        scratch, sems = refs[n_in + 2 * n + n_out:n_in + 2 * n + n_out + n_sc], refs[n_in + 2 * n + n_out + n_sc:]
        start, finish = phases(xs, gs, *sems)
        at_first = functools.reduce(jnp.logical_and, [pl.program_id(a) == 0 for a in range(len(grid))])
        at_last = functools.reduce(jnp.logical_and, [pl.program_id(a) == grid[a] - 1 for a in range(len(grid))])
        pl.when(at_first)(start)
        body(*ins, *ys, *scratch)
        pl.when(at_last)(finish)

    results = pl.pallas_call(
        carried, name=name, grid=grid, in_specs=list(in_specs) + [_ANY] * n, out_specs=specs + [_ANY] * n,
        out_shape=outs + shapes(gather),
        scratch_shapes=list(scratch_shapes) + [pltpu.SemaphoreType.DMA((7, n)), pltpu.SemaphoreType.DMA((7, n)),
                                               pltpu.SemaphoreType.DMA((n,))],
        compiler_params=_params(*(["arbitrary"] * len(grid))))(*args, *gather)
    main = results[:n_out]
    return (main if isinstance(out_shape, (list, tuple)) else main[0]), list(results[n_out:])


def _pair_exchange(xs, name):
    n = len(xs)

    def body(*refs):
        x_refs, out_refs = refs[:n], refs[n:2 * n]
        send_sems, recv_sems, _ = refs[2 * n:]
        x_, y_, c_ = _mesh_pos()
        copies = [pltpu.make_async_remote_copy(
            src_ref=x_refs[t].at[2 * a + 1 - c_], dst_ref=out_refs[t].at[a],
            send_sem=send_sems.at[a, t], recv_sem=recv_sems.at[a, t],
            device_id=(x_, y_, 1 - c_), device_id_type=MESH_ID) for a in range(4) for t in range(n)]
        for cp in copies:
            cp.start()
        for cp in copies:
            cp.wait_recv()
        for cp in copies:
            cp.wait_send()

    return _comm_call(body, xs, [jax.ShapeDtypeStruct((4,) + x.shape[1:], x.dtype) for x in xs], name)


def _chip_exchange(xs, name):
    n = len(xs)

    def body(*refs):
        start, finish = _chip_phases(refs[:n], refs[n:2 * n], *refs[2 * n:])
        start()
        finish()

    return _comm_call(body, xs, _same_shapes(xs), name)


def _same_shapes(xs):
    return [jax.ShapeDtypeStruct(x.shape, x.dtype) for x in xs]


def _chip_phases(x_refs, out_refs, send_sems, recv_sems, local_sems):
    n = len(x_refs)
    x_, y_, c_ = _mesh_pos()
    chip = 2 * x_ + y_

    def all_copies():
        mine = [pltpu.make_async_copy(x_refs[t].at[chip], out_refs[t].at[chip], local_sems.at[t]) for t in range(n)]
        copies = []
        for k in range(1, 4):
            px, py = x_ ^ (k >> 1), y_ ^ (k & 1)
            for t in range(n):
                copies.append(pltpu.make_async_remote_copy(
                    src_ref=x_refs[t].at[2 * px + py], dst_ref=out_refs[t].at[chip],
                    send_sem=send_sems.at[k - 1, t], recv_sem=recv_sems.at[k - 1, t],
                    device_id=(px, py, c_), device_id_type=MESH_ID))
        return mine, copies

    def start():
        mine, copies = all_copies()
        for cp in mine + copies:
            cp.start()

    def finish():
        mine, copies = all_copies()
        for cp in copies:
            cp.wait_recv()
        for cp in copies:
            cp.wait_send()
        for cp in mine:
            cp.wait()

    return start, finish


GATHER = (_gather_phases, _gathered_shapes)
CHIP_EXCHANGE = (_chip_phases, _same_shapes)


class _Exchanges:
    class Job:
        def __init__(self, arrays, finish):
            self.arrays, self.finish = arrays, finish

    def __init__(self):
        self.jobs = []

    def add(self, arrays, finish):
        self.jobs.append(self.Job(arrays, finish))

    def take(self):
        return self.jobs.pop(0) if self.jobs else None


def _ffn_fwd(h, norm, w_in, w_out, tag, gather=None):
    T, D = h.shape
    F = w_out.shape[0]
    hn = _rmsnorm_fwd(h, norm, f"{tag}_norm")
    got = None
    if gather is None:
        gate, up, act = _mm_swiglu_fwd(hn, w_in, f"{tag}_in")
    else:
        (gate, up, act), got = _mm_swiglu_fwd(hn, w_in, f"{tag}_in", gather=gather)
    out = _mm(act, w_out, "nn", M=T, N=D, K=F, res=h, scale=0.5, name=f"{tag}_out")
    return out, (h, hn, gate, up, act), got


def _ffn_bwd(dout, saved, norm, w_in, w_out, tag, exchanges=None, ready=None, index=None):
    h, hn, gate, up, act = saved
    T, D = h.shape
    F = w_out.shape[0]
    job = exchanges.take() if exchanges else None
    dg, du, parts = _mm_swiglu_bwd(dout, w_out, gate, up, f"{tag}_dact", exchange=job.arrays if job else None)
    if job:
        job.finish(parts)
    dw_out = _mm(act, dout, "tn", M=F, N=D, K=T, scale=0.5, out_dtype=BF16, name=f"{tag}_dwout")
    dw_in = _mm(hn, dg, "tn", M=D, N=2 * F, K=T, b2=du, out_dtype=BF16, name=f"{tag}_dwin")
    if ready is not None:
        ready({("ffn_w_in", index): dw_in, ("ffn_w_out", index): dw_out})
    job = exchanges.take() if exchanges else None
    dhn = _mm(dg, w_in, "nt", M=T, N=D, K=2 * F, a_rest=(du,), name=f"{tag}_dhn", exchange=job.arrays if job else None)
    if job:
        dhn, parts = dhn
        job.finish(parts)
    dh, dnorm = _rmsnorm_bwd(h, norm, dhn, dout, f"{tag}_dnorm")
    return dh, dnorm, dw_in, dw_out


def _local_step(x, positions, target, W, pending=None, exchanges=None, on_grads=None):
    B, S, D = x.shape
    T = B * S
    H, HP = D // HEAD_DIM, D // LANES
    G = {}
    pending = pending or {}
    sending = lambda call: pending[call][0] if call in pending else None

    def arrived(call, got):
        if call in pending:
            pending[call][1](W, got)

    def ready(grads):
        if on_grads is not None:
            on_grads(grads)

    h0 = x.reshape(T, D)
    cs = _rope_tables(positions, "rope_tables")

    h1, ffn00, got = _ffn_fwd(h0, W["ffn_norm"][0, 0], W["ffn_w_in"][0, 0], W["ffn_w_out"][0, 0], "ffn00",
                              gather=sending("ffn00_in"))
    arrived("ffn00_in", got)
    hna = _rmsnorm_fwd(h1, W["mix_norm"][0], "mixa_norm")
    qkv = _mm(hna, W["a_w_qkv"], "nn", M=T, N=9 * D, K=D, name="a_qkv", gather=sending("a_qkv"))
    if sending("a_qkv") is not None:
        qkv, got = qkv
        arrived("a_qkv", got)
    qs, ks, os_, lses = [], [], [], []
    for g in range(3):
        qs.append(_qknorm_fwd(qkv, 3 * g * D, W["a_q_norm"][0, g], cs, D, F32, f"a_qnorm{g}", out_scale=SCALE))
        ks.append(_qknorm_fwd(qkv, (3 * g + 1) * D, W["a_k_norm"][0, g], cs, D, F32, f"a_knorm{g}"))
        o, l, *got = _band_fwd(qs[g], ks[g], qkv, g, B, S, D, f"a_band{g}", gather=sending(f"a_band{g}"))
        arrived(f"a_band{g}", got[0] if got else None)
        os_.append(o)
        lses.append(l)
    mixed, lse_a = _mix(os_, lses, "a_mix")
    h2 = _mm(mixed, W["a_w_o"], "nn", M=T, N=D, K=D, res=h1, name="a_out")
    h3, ffn01, _ = _ffn_fwd(h2, W["ffn_norm"][0, 1], W["ffn_w_in"][0, 1], W["ffn_w_out"][0, 1], "ffn01")

    hnkv = _rmsnorm_fwd(h3, W["kv_norm"], "kv_norm")
    kv = _mm(hnkv, W["kv_w_kv"], "nn", M=T, N=2 * D, K=D, name="kv_proj")
    fpre = _mm(hnkv, W["kv_w_f"], "nn", M=T, N=LANES, K=D, name="kv_gate_proj")
    b_f = jnp.pad(W["kv_b_f"], (0, LANES - H)).reshape(1, LANES)
    kh = _qknorm_fwd(kv, 0, W["kv_k_norm"], None, D, BF16, "kv_knorm")
    cum, cq = _gate_fwd(fpre, b_f, B, S, D, "kv_gate")
    ck = cum.reshape(B, S, LANES)[:, :, :H].reshape(B, S, HP, 2).transpose(0, 2, 3, 1)
    ck = jnp.pad(ck, ((0, 0), (0, 0), (0, 6), (0, 0))).reshape(B * HP, 8, S)
    h4, ffn10, _ = _ffn_fwd(h3, W["ffn_norm"][1, 0], W["ffn_w_in"][1, 0], W["ffn_w_out"][1, 0], "ffn10")
    hnb = _rmsnorm_fwd(h4, W["mix_norm"][1], "mixb_norm")
    qraw = _mm(hnb, W["b_w_q"], "nn", M=T, N=D, K=D, name="b_q")
    qh = _qknorm_fwd(qraw, 0, W["b_q_norm"][0], None, D, BF16, "b_qnorm", out_scale=SCALE)
    first, last = _fox_live_blocks(cum, W["b_q_norm"][0], W["kv_k_norm"], B, S, D)
    ob, lse_b = _fox_fwd(qh, kh, kv, cq, ck, first, B, S, D, "b_fox")
    h5 = _mm(ob, W["b_w_o"], "nn", M=T, N=D, K=D, res=h4, name="b_out")
    h6, ffn11, _ = _ffn_fwd(h5, W["ffn_norm"][1, 1], W["ffn_w_in"][1, 1], W["ffn_w_out"][1, 1], "ffn11")

    sq, dy = _loss_and_grad(h6, target.reshape(T, D), "loss")

    dn = [[None, None], [None, None]]
    dwi = [[None, None], [None, None]]
    dwo = [[None, None], [None, None]]
    dh5, dn[1][1], dwi[1][1], dwo[1][1] = _ffn_bwd(dy, ffn11, W["ffn_norm"][1, 1], W["ffn_w_in"][1, 1], W["ffn_w_out"][1, 1], "ffn11",
                                                   exchanges, ready, 3)
    dob = _mm(dh5, W["b_w_o"], "nt", M=T, N=D, K=D, name="b_dout")
    G["b_w_o"] = _mm(ob, dh5, "tn", M=D, N=D, K=T, out_dtype=BF16, name="b_dwo")[None]
    delta_b = _head_rowsum(dob, ob, "b_delta")
    dqh, dcq = _fox_bwd_dq(qh, kh, kv, cq, ck, first, dob, lse_b, delta_b, B, S, D, "b_fox_dq")
    dkh, dv_b, dck = _fox_bwd_dkv(qh, kh, kv, cq, ck, last, dob, lse_b, delta_b, B, S, D, "b_fox_dkv")
    dqraw, dg = _qknorm_bwd(qraw, 0, W["b_q_norm"][0], None, dqh, D, "b_dqnorm", out_scale=SCALE)
    G["b_q_norm"] = dg[None]
    dhnb = _mm(dqraw, W["b_w_q"], "nt", M=T, N=D, K=D, name="b_dhn")
    G["b_w_q"] = _mm(hnb, dqraw, "tn", M=D, N=D, K=T, out_dtype=BF16, name="b_dwq")[None]
    dh4, dmix_b = _rmsnorm_bwd(h4, W["mix_norm"][1], dhnb, dh5, "mixb_dnorm")
    dh3, dn[1][0], dwi[1][0], dwo[1][0] = _ffn_bwd(dh4, ffn10, W["ffn_norm"][1, 0], W["ffn_w_in"][1, 0], W["ffn_w_out"][1, 0], "ffn10",
                                                   exchanges, ready, 2)

    dkraw, G["kv_k_norm"] = _qknorm_bwd(kv, 0, W["kv_k_norm"], None, dkh, D, "kv_dknorm")
    dcum = dck.reshape(B, HP, 8, S)[:, :, :2].transpose(0, 3, 1, 2).reshape(T, H)
    dcum = jnp.pad(dcum, ((0, 0), (0, LANES - H)))
    dcum_q = jnp.pad(dcq.reshape(T, H, HEAD_DIM)[:, :, 0], ((0, 0), (0, LANES - H)))
    dfpre, db_f = _gate_bwd(dcum, dcum_q, fpre, b_f, B, S, "kv_dgate")
    G["kv_b_f"] = db_f[0, :H]
    dhnkv = _mm(dkraw, W["kv_w_kv"], "nt", M=T, N=D, K=D, name="kv_dhn_k")
    dhnkv = _mm(dv_b, W["kv_w_kv"], "nt", M=T, N=D, K=D, b_off=(0, D), res=dhnkv, name="kv_dhn_v")
    dhnkv = _mm(dfpre, W["kv_w_f"], "nt", M=T, N=D, K=LANES, res=dhnkv, name="kv_dhn_f")
    dw_k = _mm(hnkv, dkraw, "tn", M=D, N=D, K=T, out_dtype=BF16, name="kv_dwk")
    dw_v = _mm(hnkv, dv_b, "tn", M=D, N=D, K=T, out_dtype=BF16, name="kv_dwv")
    dw_f = _mm(hnkv, dfpre, "tn", M=D, N=LANES, K=T, out_dtype=BF16, name="kv_dwf")
    G["kv_w"] = jnp.concatenate([dw_k, dw_v, dw_f[:, :H]], axis=1)
    dh3, G["kv_norm"] = _rmsnorm_bwd(h3, W["kv_norm"], dhnkv, dh3, "kv_dnorm")
    ready({"kv_w": G["kv_w"], "b_w_q": G["b_w_q"][0], "b_w_o": G["b_w_o"][0]})

    dh2, dn[0][1], dwi[0][1], dwo[0][1] = _ffn_bwd(dh3, ffn01, W["ffn_norm"][0, 1], W["ffn_w_in"][0, 1], W["ffn_w_out"][0, 1], "ffn01",
                                                   exchanges, ready, 1)
    dmixed = _mm(dh2, W["a_w_o"], "nt", M=T, N=D, K=D, name="a_dmixed")
    G["a_w_o"] = _mm(mixed, dh2, "tn", M=D, N=D, K=T, out_dtype=BF16, name="a_dwo")[None]
    ready({"a_w_o": G["a_w_o"][0]})
    delta_a = _head_rowsum(dmixed, mixed, "a_delta")
    dhna = None
    dw_qkv, dqn, dkn = [], [], []
    for g in range(3):
        dq, dk, dv = _band_bwd(qs[g], ks[g], qkv, dmixed, lse_a, delta_a, g, B, S, D, f"a_dband{g}", exchanges)
        dqr, dgq = _qknorm_bwd(qkv, 3 * g * D, W["a_q_norm"][0, g], cs, dq, D, f"a_dqnorm{g}", out_scale=SCALE)
        dkr, dgk = _qknorm_bwd(qkv, (3 * g + 1) * D, W["a_k_norm"][0, g], cs, dk, D, f"a_dknorm{g}")
        dqn.append(dgq)
        dkn.append(dgk)
        dhna = _mm(dqr, W["a_w_qkv"], "nt", M=T, N=D, K=3 * D, a_rest=(dkr, dv), b_off=(0, 3 * g * D), res=dhna,
                   name=f"a_dhn{g}")
        for j, d in enumerate((dqr, dkr, dv)):
            dw_qkv.append(_mm(hna, d, "tn", M=D, N=D, K=T, out_dtype=BF16, name=f"a_dwqkv{3 * g + j}"))
    G["a_w_qkv"] = jnp.concatenate(dw_qkv, axis=1)[None]
    ready({"a_w_qkv": G["a_w_qkv"][0]})
    G["a_q_norm"] = jnp.stack(dqn)[None]
    G["a_k_norm"] = jnp.stack(dkn)[None]
    dh1, dmix_a = _rmsnorm_bwd(h1, W["mix_norm"][0], dhna, dh2, "mixa_dnorm")
    dh0, dn[0][0], dwi[0][0], dwo[0][0] = _ffn_bwd(dh1, ffn00, W["ffn_norm"][0, 0], W["ffn_w_in"][0, 0], W["ffn_w_out"][0, 0], "ffn00",
                                                   exchanges, ready, 0)

    G["mix_norm"] = jnp.stack([dmix_a, dmix_b])
    G["ffn_norm"] = jnp.stack([jnp.stack(r) for r in dn])
    G["ffn_w_in"] = jnp.stack([jnp.stack(r) for r in dwi])
    G["ffn_w_out"] = jnp.stack([jnp.stack(r) for r in dwo])
    return jnp.sum(sq), dh0.reshape(B, S, D), G


SMALL =["mix_norm", "kv_norm", "a_q_norm", "a_k_norm", "kv_b_f", "kv_k_norm", "b_q_norm"]


def _pack(blocks, lead):
    flat = [b.reshape(b.shape[:lead] + (-1,)) for b in blocks]
    flat = jnp.concatenate(flat, axis=lead)
    n = flat.shape[-1]
    unit = PACK_ROWS * LANES
    padded = -(-n // unit) * unit if n > unit else -(-n // (16 * LANES)) * 16 * LANES
    flat = jnp.pad(flat, [(0, 0)] * lead + [(0, padded - n)])
    return flat.reshape(flat.shape[:lead] + (padded // LANES, LANES))


def _unpack(packed, shapes, lead):
    flat = packed.reshape(packed.shape[:lead] + (-1,))
    out, off = [], 0
    for shp in shapes:
        n = int(np.prod(shp))
        out.append(flat[..., off:off + n].reshape(packed.shape[:lead] + tuple(shp)))
        off += n
    return out


def kernel(x, positions, ffn_norm, ffn_w_in, ffn_w_out, mix_norm, a_w_qkv, a_q_norm, a_k_norm, a_w_o, kv_norm, kv_w, kv_b_f, kv_k_norm, b_w_q, b_q_norm, b_w_o, loss_target, m_ffn_norm, m_ffn_w_in, m_ffn_w_out, m_mix_norm, m_a_w_qkv, m_a_q_norm, m_a_k_norm, m_a_w_o, m_kv_norm, m_kv_w, m_kv_b_f, m_kv_k_norm, m_b_w_q, m_b_q_norm, m_b_w_o, v_ffn_norm, v_ffn_w_in, v_ffn_w_out, v_mix_norm, v_a_w_qkv, v_a_q_norm, v_a_k_norm, v_a_w_o, v_kv_norm, v_kv_w, v_kv_b_f, v_kv_k_norm, v_b_w_q, v_b_q_norm, v_b_w_o):
    names = ["ffn_norm", "ffn_w_in", "ffn_w_out", "mix_norm", "a_w_qkv", "a_q_norm", "a_k_norm", "a_w_o", "kv_norm",
             "kv_w", "kv_b_f", "kv_k_norm", "b_w_q", "b_q_norm", "b_w_o"]
    w = dict(zip(names, (ffn_norm, ffn_w_in, ffn_w_out, mix_norm, a_w_qkv, a_q_norm, a_k_norm, a_w_o, kv_norm, kv_w,
                         kv_b_f, kv_k_norm, b_w_q, b_q_norm, b_w_o)))
    m = dict(zip(names, (m_ffn_norm, m_ffn_w_in, m_ffn_w_out, m_mix_norm, m_a_w_qkv, m_a_q_norm, m_a_k_norm, m_a_w_o,
                         m_kv_norm, m_kv_w, m_kv_b_f, m_kv_k_norm, m_b_w_q, m_b_q_norm, m_b_w_o)))
    v = dict(zip(names, (v_ffn_norm, v_ffn_w_in, v_ffn_w_out, v_mix_norm, v_a_w_qkv, v_a_q_norm, v_a_k_norm, v_a_w_o,
                         v_kv_norm, v_kv_w, v_kv_b_f, v_kv_k_norm, v_b_w_q, v_b_q_norm, v_b_w_o)))
    B, S, D = x.shape
    H = D // HEAD_DIM
    F = ffn_w_out.shape[2] * N_DEV
    Ds, Fs = D // N_DEV, F // N_DEV
    me = 4 * lax.axis_index("x") + 2 * lax.axis_index("y") + lax.axis_index("c")
    squares = ("a_w_o", "b_w_q", "b_w_o")

    def shard_views(t):
        return [t["ffn_w_in"].reshape(4 * D, -1), t["ffn_w_out"].reshape(-1, D), t["a_w_qkv"][0], t["kv_w"],
                jnp.concatenate([t[n][0] for n in squares], axis=0)]

    def to_shards(views):
        sq3 = views[4].reshape(3, 1, Ds, D)
        out = {"ffn_w_in": views[0].reshape(ffn_w_in.shape), "ffn_w_out": views[1].reshape(ffn_w_out.shape),
               "a_w_qkv": views[2][None], "kv_w": views[3]}
        out.update({n: sq3[i] for i, n in enumerate(squares)})
        return out

    sh_in, sh_out, sh_qkv, sh_kv, sh_sq = [s.astype(BF16) for s in shard_views(w)]
    in_of, out_of = (lambda f: sh_in[f * D:(f + 1) * D]), (lambda f: sh_out[f * Fs:(f + 1) * Fs])
    square_of = lambda i: sh_sq[i * Ds:(i + 1) * Ds]
    side_by_side = lambda g: g.transpose(1, 0, 2).reshape(g.shape[1], N_DEV * g.shape[2])
    stacked = lambda g: g.reshape(N_DEV * g.shape[1], g.shape[2])

    def store_ffn(f):
        def store(W, got):
            W["ffn_w_in"][f // 2, f % 2], W["ffn_w_out"][f // 2, f % 2] = side_by_side(got[0]), stacked(got[1])
        return store

    def store_mixer_a(W, got):
        W["a_w_qkv"], W["a_w_o"] = side_by_side(got[0]), stacked(got[1])

    def store_layer_b(W, got):
        store_ffn(1)(W, got)
        kv_full = side_by_side(got[2])
        W["kv_w_kv"] = kv_full[:, :2 * D]
        W["kv_w_f"] = jnp.pad(kv_full[:, 2 * D:], ((0, 0), (0, LANES - H)))
        W["b_w_q"], W["b_w_o"] = stacked(got[3]), stacked(got[4])

    norm_rows = jnp.pad(ffn_norm.reshape(4, Ds), ((0, 4), (0, 0)))
    first = _all_gather([in_of(0), out_of(0), norm_rows], "gather_weights")
    W = {n: w[n] for n in SMALL}
    W["ffn_norm"] = first[2][:, :4].transpose(1, 0, 2).reshape(2, 2, D)
    W["ffn_w_in"], W["ffn_w_out"] = {}, {}
    store_ffn(0)(W, first)
    pending = {"ffn00_in": ([sh_qkv, square_of(0)], store_mixer_a),
               "a_qkv": ([in_of(1), out_of(1), sh_kv, square_of(1), square_of(2)], store_layer_b),
               "a_band0": ([in_of(2), out_of(2)], store_ffn(2)),
               "a_band1": ([in_of(3), out_of(3)], store_ffn(3))}

    wmv = [shard_views(t) for t in (w, m, v)]
    columns = lambda g: g.reshape(g.shape[0], N_DEV, -1).transpose(1, 0, 2)
    rows = lambda g: g.reshape(N_DEV, g.shape[0] // N_DEV, g.shape[1])
    square_index = {n: i for i, n in enumerate(squares)}

    def placed(key):
        name, f = key if isinstance(key, tuple) else (key, None)
        if name == "ffn_w_in":
            return columns, 0, slice(f * D, (f + 1) * D)
        if name == "ffn_w_out":
            return rows, 1, slice(f * Fs, (f + 1) * Fs)
        if name in square_index:
            return rows, 4, slice(square_index[name] * Ds, (square_index[name] + 1) * Ds)
        return columns, {"a_w_qkv": 2, "kv_w": 3}[name], slice(None)

    exchanges = _Exchanges()
    updated = [{} for _ in wmv[0]]
    groups = []

    def on_grads(grads):
        tag = str(len(groups))
        groups.append(tag)
        keys = list(grads)
        blocks = [placed(k)[0](grads[k]).astype(BF16) for k in keys]
        got = _pair_exchange(blocks, "exchange_pair" + tag)
        sums = [_pair_sum(b, s, f"pair_sum{tag}_{i}") for i, (b, s) in enumerate(zip(blocks, got))]

        def finish(parts):
            for i, k in enumerate(keys):
                _, view, span = placed(k)
                outs = _adamw(parts[i], *(t[view][span] for t in wmv), f"adamw{tag}_{i}")
                updated[view][span.start or 0] = outs

        exchanges.add(sums, finish)

    sq, grad_x, G = _local_step(x, positions, loss_target, W, pending, exchanges, on_grads)
    loss = lax.psum(sq * (0.5 / D), ("x", "y", "c"))
    while exchanges.jobs:
        job = exchanges.take()
        job.finish(_chip_exchange(job.arrays, "exchange_chips" + str(len(exchanges.jobs))))
    new = {}
    for k, kind in enumerate(("grad", "delta", "new_m", "new_v")):
        whole = [jnp.concatenate([u[r][k] for r in sorted(u)], axis=0) for u in updated]
        for n, arr in to_shards(whole).items():
            new[kind, n] = arr

    small_all = ["ffn_norm"] + SMALL
    vec_parts = _all_gather([_pack([G[n] for n in small_all], 0)], "gather_vector_grads")[0]
    vec_parts = _unpack(vec_parts, [G[n].shape for n in small_all], 1)
    norm_g = lax.dynamic_slice_in_dim(vec_parts[0], me * ffn_norm.shape[2], ffn_norm.shape[2], axis=3)
    rep_g = _pack(vec_parts[1:], 1)
    for ns, parts_s in ((["ffn_norm"], _pack([norm_g], 1)), (SMALL, rep_g)):
        outs = _adamw(parts_s, *(_pack([t[n] for n in ns], 0) for t in (w, m, v)), "adamw_" + ns[0])
        for kind, packed in zip(("grad", "delta", "new_m", "new_v"), outs):
            for n, arr in zip(ns, _unpack(packed, [w[n].shape for n in ns], 0)):
                new[kind, n] = arr

    result = [loss, grad_x]
    for kind in ("grad", "delta", "new_m", "new_v"):
        result += [new[kind, n] for n in names]
    return tuple(result)
```

```python
import functools

import jax
import jax.numpy as jnp
import numpy as np
from jax import lax
from jax.experimental import pallas as pl
from jax.experimental.pallas import tpu as pltpu

F32 = jnp.float32
BF16 = jnp.bfloat16
HIGHEST = lax.Precision.HIGHEST

HEAD_DIM = 64
ROT_DIM = HEAD_DIM // 4
ROPE_THETA = 500000.0
DILATED_GROUPS = ((128, 1), (512, 4), (2048, 16))
BAND_BLOCK = 128
EPS = 1e-6
ADAM_LR, ADAM_B1, ADAM_B2, ADAM_EPS, ADAM_WD, ADAM_STEP = 0.001, 0.9, 0.999, 1e-08, 0.01, 10

LANES = 128
N_DEV = 8
VMEM_LIMIT = 56 * 1024 * 1024
NEG = -1e30
SCALE = HEAD_DIM ** -0.5
PACK_ROWS = 1024
MESH_ID = pl.DeviceIdType.MESH


def _params(*sem):
    return pltpu.CompilerParams(dimension_semantics=sem, vmem_limit_bytes=VMEM_LIMIT)


def _divs(n, cap, mult=LANES):
    d = [t for t in range(mult, min(n, cap) + 1, mult) if n % t == 0]
    return d or [n]


def _pick(n, cap, mult=LANES):
    return _divs(n, cap, mult)[-1]


def _mm_tiles(M, N, K, budget=1408 * 1024):
    best = None
    for tm in _divs(M, 1408):
        for tn in _divs(N, 1408):
            if tm * tn <= budget and (best is None or tm * tn > best[0] * best[1]):
                best = (tm, tn)
    if best is None:
        best = (_divs(M, 1408)[0], _divs(N, 1408)[0])
    tm, tn = best
    tks = [t for t in _divs(K, 2048) if (tm + tn) * t <= 3 * 1024 * 1024]
    tk = tks[-1] if tks else _divs(K, 2048)[0]
    return tm, tn, tk


def _dot(a, b, dims, precision=None):
    return lax.dot_general(a, b, (dims, ((), ())), precision=precision, preferred_element_type=F32)


NN = ((1,), (0,))
NT = ((1,), (1,))
TN = ((0,), (0,))


def _dot_split(x, mat):
    hi = x.astype(BF16)
    lo = (x - hi.astype(F32)).astype(BF16)
    m = mat.astype(BF16)
    return _dot(hi, m, NN) + _dot(lo, m, NN)


def _dot_once(x, mat):
    return _dot(x.astype(BF16), mat.astype(BF16), NN)


def _mm(a, b, mode, *, M, N, K, a_off=(0, 0), b_off=(0, 0), a_rest=(), b2=None, res=None, scale=1.0, out_dtype=F32,
        name, gather=None, exchange=None):
    a_parts = 1 + len(a_rest)
    tm, tn, tk = _mm_tiles(M, N // 2 if b2 is not None else N, K // a_parts)
    nk = K // tk
    k_part, n_half = nk // a_parts, (N // tn) // 2
    if mode == "tn":
        a_blk, b_blk, dims = (tk, tm), (tk, tn), TN
    elif mode == "nt":
        a_blk, b_blk, dims = (tm, tk), (tn, tk), NT
    else:
        a_blk, b_blk, dims = (tm, tk), (tk, tn), NN
    for off, blk in ((a_off, a_blk), (b_off, b_blk)):
        assert off[0] % blk[0] == 0 and off[1] % blk[1] == 0, (name, off, blk)
    ao = (a_off[0] // a_blk[0], a_off[1] // a_blk[1])
    bo = (b_off[0] // b_blk[0], b_off[1] // b_blk[1])
    if mode == "tn":
        a_map = lambda i, j, k: (k + ao[0], i + ao[1])
        b_map = lambda i, j, k: (k + bo[0], j + bo[1])
    elif mode == "nt":
        a_map = lambda i, j, k: (i + ao[0], k + ao[1])
        b_map = lambda i, j, k: (j + bo[0], k + bo[1])
    else:
        a_map = lambda i, j, k: (i + ao[0], k + ao[1])
        b_map = lambda i, j, k: (k + bo[0], j + bo[1])
    has_res = res is not None
    if a_rest:
        assert mode != "tn" and a_off == (0, 0)
        part_map = lambda p: (lambda i, j, k: (i, jnp.clip(k - p * k_part, 0, k_part - 1)))
        a_map = part_map(0)
    if b2 is not None:
        assert mode != "nt" and b_off == (0, 0)
        b_map = lambda i, j, k: (k, jnp.minimum(j, n_half - 1))
        b2_map = lambda i, j, k: (k, jnp.maximum(j - n_half, 0))

    def body(*refs):
        refs = list(refs)
        a_ref, b_ref = refs.pop(0), refs.pop(0)
        a_val, b_val = a_ref[...].astype(BF16), b_ref[...].astype(BF16)
        for p in range(1, a_parts):
            a_val = jnp.where(pl.program_id(2) >= p * k_part, refs.pop(0)[...].astype(BF16), a_val)
        if b2 is not None:
            b_val = jnp.where(pl.program_id(1) < n_half, b_val, refs.pop(0)[...].astype(BF16))
        r_ref = refs.pop(0) if has_res else None
        o_ref = refs.pop(0)

        def finish(total):
            out = total * scale
            if has_res:
                out = r_ref[...] + out
            o_ref[...] = out.astype(out_dtype)

        part = _dot(a_val, b_val, dims)
        if nk == 1:
            finish(part)
            return
        acc = refs[-1]
        k = pl.program_id(2)

        @pl.when(k == 0)
        def _():
            acc[...] = part

        @pl.when(k > 0)
        def _():
            acc[...] += part

        pl.when(k == nk - 1)(lambda: finish(acc[...]))

    in_specs = [pl.BlockSpec(a_blk, a_map), pl.BlockSpec(b_blk, b_map)]
    args = [a, b]
    for p, extra in enumerate(a_rest, 1):
        in_specs.append(pl.BlockSpec(a_blk, part_map(p)))
        args.append(extra)
    if b2 is not None:
        in_specs.append(pl.BlockSpec(b_blk, b2_map))
        args.append(b2)
    if has_res:
        in_specs.append(pl.BlockSpec((tm, tn), lambda i, j, k: (i, j)))
        args.append(res)
    out, gathered = _call(
        body, name=name, grid=(M // tm, N // tn, nk), in_specs=in_specs,
        out_specs=pl.BlockSpec((tm, tn), lambda i, j, k: (i, j)),
        out_shape=jax.ShapeDtypeStruct((M, N), out_dtype),
        scratch_shapes=[pltpu.VMEM((tm, tn), F32)] if nk > 1 else [],
        semantics=("parallel", "parallel", "arbitrary"), args=args, gather=gather, exchange=exchange)
    return out if gather is None and exchange is None else (out, gathered)


def _rmsnorm_fwd(x, g, name):
    T, D = x.shape
    tm = _pick(T, 256, 8)

    def body(x_ref, g_ref, y_ref):
        xv = x_ref[...]
        r = lax.rsqrt(jnp.mean(xv * xv, axis=-1, keepdims=True) + EPS)
        y_ref[...] = (xv * r * g_ref[...]).astype(BF16)

    return pl.pallas_call(
        body, name=name, grid=(T // tm,),
        in_specs=[pl.BlockSpec((tm, D), lambda i: (i, 0)), pl.BlockSpec((1, D), lambda i: (0, 0))],
        out_specs=pl.BlockSpec((tm, D), lambda i: (i, 0)),
        out_shape=jax.ShapeDtypeStruct((T, D), BF16), compiler_params=_params("parallel"),
    )(x, g.reshape(1, D))


def _rmsnorm_bwd(x, g, dy, dres, name):
    T, D = x.shape
    tm = _pick(T, 256, 8)

    def body(x_ref, g_ref, dy_ref, dres_ref, dx_ref, dg_ref):
        xv = x_ref[...]
        r = lax.rsqrt(jnp.mean(xv * xv, axis=-1, keepdims=True) + EPS)
        xn = xv * r
        d = dy_ref[...]
        dgy = d * g_ref[...]
        dx_ref[...] = dres_ref[...] + r * (dgy - xn * jnp.mean(dgy * xn, axis=-1, keepdims=True))

        @pl.when(pl.program_id(0) == 0)
        def _():
            dg_ref[...] = jnp.zeros_like(dg_ref)

        dg_ref[...] += jnp.sum(d * xn, axis=0, keepdims=True)

    row = pl.BlockSpec((tm, D), lambda i: (i, 0))
    vec = pl.BlockSpec((1, D), lambda i: (0, 0))
    dx, dg = pl.pallas_call(
        body, name=name, grid=(T // tm,), in_specs=[row, vec, row, row], out_specs=[row, vec],
        out_shape=[jax.ShapeDtypeStruct((T, D), F32), jax.ShapeDtypeStruct((1, D), F32)],
        compiler_params=_params("arbitrary"),
    )(x, g.reshape(1, D), dy, dres)
    return dx, dg.reshape(D)


def _sigmoid(x):
    return 1.0 / (1.0 + jnp.exp(-x))


MXU_WIDTH = 256


def _col_chunks(n):
    return [(lo, min(lo + MXU_WIDTH, n)) for lo in range(0, n, MXU_WIDTH)]


def _mm_swiglu_fwd(hn, w_in, name, gather=None):
    T, D = hn.shape
    F = w_in.shape[1] // 2
    tm, tn, _ = _mm_tiles(T, F, D)
    assert (tm + 2 * tn) * D <= 4 * 1024 * 1024, "the whole contraction is held in one step"
    nf = F // tn

    def body(a_ref, bg_ref, bu_ref, g_ref, u_ref, act_ref):
        a = a_ref[...].astype(BF16)
        for lo, hi in _col_chunks(tn):
            gv = _dot(a, bg_ref[:, lo:hi].astype(BF16), NN)
            uv = _dot(a, bu_ref[:, lo:hi].astype(BF16), NN)
            g_ref[:, lo:hi] = gv.astype(BF16)
            u_ref[:, lo:hi] = uv.astype(BF16)
            act_ref[:, lo:hi] = (gv * _sigmoid(gv) * uv).astype(BF16)

    tile = pl.BlockSpec((tm, tn), lambda j, i: (i, j))
    out = jax.ShapeDtypeStruct((T, F), BF16)
    outs, gathered = _call(
        body, name=name, grid=(nf, T // tm),
        in_specs=[pl.BlockSpec((tm, D), lambda j, i: (i, 0)), pl.BlockSpec((D, tn), lambda j, i: (0, j)),
                  pl.BlockSpec((D, tn), lambda j, i: (0, j + nf))],
        out_specs=[tile, tile, tile], out_shape=[out, out, out], semantics=("parallel", "parallel"),
        args=(hn, w_in, w_in), gather=gather)
    return outs if gather is None else (outs, gathered)


def _mm_swiglu_bwd(dout, w_out, gate, up, name, exchange=None):
    T, D = dout.shape
    F = w_out.shape[0]
    tm, tn, _ = _mm_tiles(T, F, D)
    assert (tm + tn) * D <= 4 * 1024 * 1024, "the whole contraction is held in one step"

    def body(a_ref, b_ref, g_ref, u_ref, dg_ref, du_ref):
        a = a_ref[...].astype(BF16)
        for lo, hi in _col_chunks(tn):
            d = _dot(a, b_ref[lo:hi, :].astype(BF16), NT) * 0.5
            gv, uv = g_ref[:, lo:hi].astype(F32), u_ref[:, lo:hi].astype(F32)
            s = _sigmoid(gv)
            gs = gv * s
            dg_ref[:, lo:hi] = (d * uv * (s + gs - gs * s)).astype(BF16)
            du_ref[:, lo:hi] = (d * gs).astype(BF16)

    tile = pl.BlockSpec((tm, tn), lambda j, i: (i, j))
    out = jax.ShapeDtypeStruct((T, F), BF16)
    (dg, du), parts = _call(
        body, name=name, grid=(F // tn, T // tm),
        in_specs=[pl.BlockSpec((tm, D), lambda j, i: (i, 0)), pl.BlockSpec((tn, D), lambda j, i: (j, 0)), tile, tile],
        out_specs=[tile, tile], out_shape=[out, out], semantics=("parallel", "parallel"),
        args=(dout, w_out, gate, up), exchange=exchange)
    return dg, du, parts


def _iota2(shape, dim):
    return lax.broadcasted_iota(jnp.int32, shape, dim)


def _head_ones(value=1.0):
    r, c = _iota2((LANES, LANES), 0), _iota2((LANES, LANES), 1)
    return jnp.where((r >> 6) == (c >> 6), value, 0.0).astype(F32)


def _rot_matrix(transpose):
    r, c = _iota2((LANES, LANES), 0), _iota2((LANES, LANES), 1)
    if transpose:
        r, c = c, r
    half = ROT_DIM // 2
    cm = c & (HEAD_DIM - 1)
    neg = (r == c + half) & (cm < half)
    pos = (r == c - half) & (cm >= half) & (cm < ROT_DIM)
    return pos.astype(F32) - neg.astype(F32)


def _rope_tables(positions, name):
    T = positions.size
    tm = _pick(T, 1024, 8)
    inv = ROPE_THETA ** (-jnp.arange(0, ROT_DIM, 2, dtype=F32) / ROT_DIM)
    lane = jnp.concatenate([inv, inv, jnp.zeros((HEAD_DIM - ROT_DIM,), F32)])
    lane = jnp.tile(lane, 2).reshape(1, LANES)

    def body(p_ref, f_ref, c_ref, s_ref):
        ang = p_ref[...].astype(F32) * f_ref[...]
        c_ref[...] = jnp.cos(ang)
        s_ref[...] = jnp.sin(ang)

    out = jax.ShapeDtypeStruct((T, LANES), F32)
    blk = pl.BlockSpec((tm, LANES), lambda i: (i, 0))
    return pl.pallas_call(
        body, name=name, grid=(T // tm,),
        in_specs=[pl.BlockSpec((tm, 1), lambda i: (i, 0)), pl.BlockSpec((1, LANES), lambda i: (0, 0))],
        out_specs=[blk, blk], out_shape=[out, out], compiler_params=_params("parallel"),
    )(positions.reshape(T, 1), lane)


def _qknorm_fwd(src, col_off, gain, cs, D, out_dtype, name, out_scale=1.0):
    T = src.shape[0]
    HP = D // LANES
    tm = _pick(T, 1024, 8)
    co = col_off // LANES
    rope = cs is not None

    def body(*refs):
        if rope:
            x_ref, g_ref, c_ref, s_ref, o_ref = refs
        else:
            x_ref, g_ref, o_ref = refs
        xv = x_ref[...]
        ms = _dot_once(xv * xv, _head_ones(1.0 / HEAD_DIM))
        y = xv * lax.rsqrt(ms + EPS) * g_ref[...]
        if rope:
            y = y * c_ref[...] + _dot_once(y, _rot_matrix(False)) * s_ref[...]
        o_ref[...] = (y if out_scale == 1.0 else y * out_scale).astype(out_dtype)

    in_specs = [pl.BlockSpec((tm, LANES), lambda i, h: (i, co + h)), pl.BlockSpec((1, LANES), lambda i, h: (0, 0))]
    args = [src, jnp.tile(gain.reshape(1, HEAD_DIM), (1, 2))]
    if rope:
        in_specs += [pl.BlockSpec((tm, LANES), lambda i, h: (i, 0))] * 2
        args += list(cs)
    return pl.pallas_call(
        body, name=name, grid=(T // tm, HP), in_specs=in_specs,
        out_specs=pl.BlockSpec((tm, LANES), lambda i, h: (i, h)),
        out_shape=jax.ShapeDtypeStruct((T, D), out_dtype), compiler_params=_params("parallel", "parallel"),
    )(*args)


def _qknorm_bwd(src, col_off, gain, cs, dout, D, name, out_scale=1.0):
    T = src.shape[0]
    HP = D // LANES
    tm = _pick(T, 1024, 8)
    co = col_off // LANES
    rope = cs is not None

    def body(*refs):
        if rope:
            x_ref, g_ref, d_ref, c_ref, s_ref, dx_ref, dg_ref = refs
        else:
            x_ref, g_ref, d_ref, dx_ref, dg_ref = refs
        xv = x_ref[...]
        mean_of = _head_ones(1.0 / HEAD_DIM)
        ms = _dot_once(xv * xv, mean_of)
        r = lax.rsqrt(ms + EPS)
        xn = xv * r
        d = d_ref[...] if out_scale == 1.0 else d_ref[...] * out_scale
        if rope:
            d = d * c_ref[...] + _dot_once(d * s_ref[...], _rot_matrix(True))
        dgy = d * g_ref[...]
        mean = _dot_once(dgy * xn, mean_of)
        dx_ref[...] = (r * (dgy - xn * mean)).astype(BF16)

        @pl.when(pl.program_id(1) == 0)
        def _():
            dg_ref[...] = jnp.zeros_like(dg_ref)

        dg_ref[...] += jnp.sum(d * xn, axis=0, keepdims=True)[None]

    in_specs = [pl.BlockSpec((tm, LANES), lambda h, i: (i, co + h)), pl.BlockSpec((1, LANES), lambda h, i: (0, 0)),
                pl.BlockSpec((tm, LANES), lambda h, i: (i, h))]
    args = [src, jnp.tile(gain.reshape(1, HEAD_DIM), (1, 2)), dout]
    if rope:
        in_specs += [pl.BlockSpec((tm, LANES), lambda h, i: (i, 0))] * 2
        args += list(cs)
    dx, dg = pl.pallas_call(
        body, name=name, grid=(HP, T // tm), in_specs=in_specs,
        out_specs=[pl.BlockSpec((tm, LANES), lambda h, i: (i, h)), pl.BlockSpec((1, 1, LANES), lambda h, i: (h, 0, 0))],
        out_shape=[jax.ShapeDtypeStruct((T, D), BF16), jax.ShapeDtypeStruct((HP, 1, LANES), F32)],
        compiler_params=_params("parallel", "arbitrary"),
    )(*args)
    return dx, dg.reshape(2 * HP, HEAD_DIM).sum(axis=0)


def _head_rowsum(a, b, name):
    T, D = a.shape
    tm = _pick(T, 1024, 8)

    def body(a_ref, b_ref, o_ref):
        o_ref[...] = _dot_split(a_ref[...] * b_ref[...], _head_ones())

    blk = pl.BlockSpec((tm, LANES), lambda i, h: (i, h))
    return pl.pallas_call(
        body, name=name, grid=(T // tm, D // LANES), in_specs=[blk, blk], out_specs=blk,
        out_shape=jax.ShapeDtypeStruct((T, D), F32), compiler_params=_params("parallel", "parallel"),
    )(a, b)


def _lane_masks():
    lane = _iota2((1, LANES), 1)
    return (lane < HEAD_DIM, lane >= HEAD_DIM)


BAND_GROUP = 4


def _band_geometry(g, S):
    window, dil = DILATED_GROUPS[g]
    chunk = BAND_BLOCK * dil
    assert window // dil == BAND_BLOCK and S % chunk == 0
    if dil == 1:
        assert (S // chunk) % BAND_GROUP == 0
        return dil, chunk, BAND_GROUP, 1
    assert dil % BAND_GROUP == 0
    return dil, chunk, 1, dil // BAND_GROUP


def _band_rows(i, rho, dil, shift=0):
    if dil > 1:
        return pl.ds(rho * BAND_GROUP + i, BAND_BLOCK, stride=dil), shift != 0
    j = i + shift
    outside = j < 0 or j >= BAND_GROUP
    return pl.ds(0 if outside else j * BAND_BLOCK, BAND_BLOCK), outside


def _stack_heads(x):
    hm0, hm1 = _lane_masks()
    zero = jnp.zeros_like(x)
    return jnp.concatenate([jnp.where(hm0, x, zero), jnp.where(hm1, x, zero)], axis=0)


def _unstack_heads(x2):
    hm0, _ = _lane_masks()
    return jnp.where(hm0, x2[:BAND_BLOCK], x2[BAND_BLOCK:])


def _band_masks():
    qi = _iota2((2 * BAND_BLOCK, 2 * BAND_BLOCK), 0) & (BAND_BLOCK - 1)
    kj = _iota2((2 * BAND_BLOCK, 2 * BAND_BLOCK), 1)
    own = (kj >= BAND_BLOCK) & (kj - BAND_BLOCK <= qi)
    prev = (kj < BAND_BLOCK) & (kj >= qi)
    return own, prev


def _band_fwd(qh, kh, qkv, g, B, S, D, name, gather=None):
    dil, CH, NB, RG = _band_geometry(g, S)
    HP = D // LANES
    vo = (3 * g + 2) * HP
    nsp = S // (CH * NB)

    def body(q_ref, k_ref, kp_ref, v_ref, vp_ref, o_ref, lse_ref):
        n, rho = pl.program_id(1), pl.program_id(3)
        own, prev = _band_masks()
        for i in range(BAND_GROUP):
            rows, _ = _band_rows(i, rho, dil)
            prows, outside = _band_rows(i, rho, dil, shift=-1)
            q, kc, vc = (t[0, rows, :].astype(BF16) for t in (q_ref, k_ref, v_ref))
            kp, vp = (t[0, prows, :].astype(BF16) for t in ((kp_ref, vp_ref) if outside else (k_ref, v_ref)))
            valid = (own | (prev & (n > 0))) if outside else (own | prev)
            s = jnp.where(valid, _dot(_stack_heads(q), jnp.concatenate([kp, kc], axis=0), NT), NEG)
            m = jnp.max(s, axis=-1, keepdims=True)
            e = jnp.exp(s - m)
            l = jnp.sum(e, axis=-1, keepdims=True)
            o2 = _dot((e * (1.0 / l)).astype(BF16), jnp.concatenate([vp, vc], axis=0), NN)
            o_ref[0, rows, :] = _unstack_heads(o2)
            lse_ref[0, rows, :] = _unstack_heads(m + jnp.log(l))

    span, chunk = (1, CH * NB, LANES), (1, CH, LANES)
    cur = lambda b, n, h, r: (b, n, h)
    prev = lambda b, n, h, r: (b, jnp.maximum(n * NB - 1, 0), h)
    vcur = lambda b, n, h, r: (b, n, vo + h)
    vprev = lambda b, n, h, r: (b, jnp.maximum(n * NB - 1, 0), vo + h)
    out = jax.ShapeDtypeStruct((B, S, D), F32)
    q3, k3, v3 = qh.reshape(B, S, D), kh.reshape(B, S, D), qkv.reshape(B, S, 9 * D)
    (o, lse), gathered = _call(
        body, name=name, grid=(B, nsp, HP, RG),
        in_specs=[pl.BlockSpec(span, cur), pl.BlockSpec(span, cur), pl.BlockSpec(chunk, prev),
                  pl.BlockSpec(span, vcur), pl.BlockSpec(chunk, vprev)],
        out_specs=[pl.BlockSpec(span, cur), pl.BlockSpec(span, cur)], out_shape=[out, out],
        semantics=("parallel", "parallel", "parallel", "arbitrary"), args=(q3, k3, k3, v3, v3), gather=gather)
    o, lse = o.reshape(B * S, D), lse.reshape(B * S, D)
    return (o, lse) if gather is None else (o, lse, gathered)


def _head_col(x, h):
    lane = _iota2((1, LANES), 1)
    return jnp.sum(jnp.where(lane == h * HEAD_DIM, x, 0.0), axis=-1, keepdims=True)


def _band_bwd(qh, kh, qkv, do, lse, delta, g, B, S, D, name, exchanges=None):
    dil, CH, NB, RG = _band_geometry(g, S)
    HP = D // LANES
    vo = (3 * g + 2) * HP
    nsp, nch = S // (CH * NB), S // CH

    def stacked_cols(tile):
        return jnp.concatenate([_head_col(tile, 0), _head_col(tile, 1)], axis=0)

    def body(q_ref, qn_ref, do_ref, don_ref, l_ref, ln_ref, d_ref, dn_ref, k_ref, kp_ref, v_ref, vp_ref,
             dq_ref, dk_ref, dv_ref):
        n, rho = pl.program_id(1), pl.program_id(3)
        own, prev = _band_masks()
        band = (_iota2((2 * BAND_BLOCK, BAND_BLOCK), 1) >= (_iota2((2 * BAND_BLOCK, BAND_BLOCK), 0) & (BAND_BLOCK - 1)))
        for i in range(BAND_GROUP):
            rows, _ = _band_rows(i, rho, dil)
            prows, p_out = _band_rows(i, rho, dil, shift=-1)
            nrows, n_out = _band_rows(i, rho, dil, shift=1)
            kc, vc = (t[0, rows, :].astype(BF16) for t in (k_ref, v_ref))
            kp, vp = (t[0, prows, :].astype(BF16) for t in ((kp_ref, vp_ref) if p_out else (k_ref, v_ref)))
            q2, do2 = (_stack_heads(t[0, rows, :].astype(BF16)) for t in (q_ref, do_ref))
            qn2, don2 = (_stack_heads(t[0, nrows, :].astype(BF16)) for t in ((qn_ref, don_ref) if n_out else (q_ref, do_ref)))
            lse2, del2 = (stacked_cols(t[0, rows, :]) for t in (l_ref, d_ref))
            lsen2, deln2 = (stacked_cols(t[0, nrows, :]) for t in ((ln_ref, dn_ref) if n_out else (l_ref, d_ref)))
            valid = (own | (prev & (n > 0))) if p_out else (own | prev)
            next_ok = (band & (n < nsp - 1)) if n_out else band
            k2, v2 = jnp.concatenate([kp, kc], axis=0), jnp.concatenate([vp, vc], axis=0)
            p = jnp.exp(jnp.where(valid, _dot(q2, k2, NT) - lse2, NEG))
            ds = (p * (_dot(do2, v2, NT) - del2)).astype(BF16)
            pc = jnp.exp(jnp.where(next_ok, _dot(qn2, kc, NT) - lsen2, NEG))
            dsc = (pc * (_dot(don2, vc, NT) - deln2)).astype(BF16)
            dq_ref[0, rows, :] = _unstack_heads(_dot(ds, k2, NN))
            dk_ref[0, rows, :] = _dot(ds[:, BAND_BLOCK:], q2, TN) + _dot(dsc, qn2, TN)
            dv_ref[0, rows, :] = _dot(p[:, BAND_BLOCK:].astype(BF16), do2, TN) + _dot(pc.astype(BF16), don2, TN)

    span, chunk = (1, CH * NB, LANES), (1, CH, LANES)
    cur = lambda b, n, h, r: (b, n, h)
    prev = lambda b, n, h, r: (b, jnp.maximum(n * NB - 1, 0), h)
    nxt = lambda b, n, h, r: (b, jnp.minimum((n + 1) * NB, nch - 1), h)
    vcur = lambda b, n, h, r: (b, n, vo + h)
    vprev = lambda b, n, h, r: (b, jnp.maximum(n * NB - 1, 0), vo + h)
    q3, k3, do3, l3, d3 = (t.reshape(B, S, D) for t in (qh, kh, do, lse, delta))
    v3 = qkv.reshape(B, S, 9 * D)
    sp, ch = (lambda m: pl.BlockSpec(span, m)), (lambda m: pl.BlockSpec(chunk, m))
    out = jax.ShapeDtypeStruct((B, S, D), F32)
    job = exchanges.take() if exchanges else None
    (dq, dk, dv), parts = _call(
        body, name=name, grid=(B, nsp, HP, RG),
        in_specs=[sp(cur), ch(nxt), sp(cur), ch(nxt), sp(cur), ch(nxt), sp(cur), ch(nxt),
                  sp(cur), ch(prev), sp(vcur), ch(vprev)],
        out_specs=[sp(cur)] * 3, out_shape=[out] * 3, semantics=("parallel", "parallel", "parallel", "arbitrary"),
        args=(q3, q3, do3, do3, l3, l3, d3, d3, k3, k3, v3, v3), exchange=job.arrays if job else None)
    if job:
        job.finish(parts)
    return dq.reshape(B * S, D), dk.reshape(B * S, D), dv.reshape(B * S, D)


def _mix(os_, lses, name):
    T, D = os_[0].shape
    tm = _pick(T, 256, 8)

    def body(o0, o1, o2, l0, l1, l2, mix_ref, lse_ref):
        a, b, c = l0[...], l1[...], l2[...]
        m = jnp.maximum(jnp.maximum(a, b), c)
        ea, eb, ec = jnp.exp(a - m), jnp.exp(b - m), jnp.exp(c - m)
        s = ea + eb + ec
        mix_ref[...] = (ea / s) * o0[...] + (eb / s) * o1[...] + (ec / s) * o2[...]
        lse_ref[...] = m + jnp.log(s)

    blk = pl.BlockSpec((tm, D), lambda i: (i, 0))
    out = jax.ShapeDtypeStruct((T, D), F32)
    return pl.pallas_call(
        body, name=name, grid=(T // tm,), in_specs=[blk] * 6, out_specs=[blk, blk], out_shape=[out, out],
        compiler_params=_params("parallel"),
    )(*os_, *lses)


def _gate_fwd(fpre, b_f, B, S, D, name):
    nb = S // LANES

    def body(f_ref, b_ref, cum_ref, rep_ref, carry):
        @pl.when(pl.program_id(1) == 0)
        def _():
            carry[...] = jnp.zeros_like(carry)

        z = f_ref[...] + b_ref[...]
        lf = jnp.minimum(z, 0.0) - jnp.log(1.0 + jnp.exp(-jnp.abs(z)))
        tri = (_iota2((LANES, LANES), 0) >= _iota2((LANES, LANES), 1)).astype(F32)
        c = _dot(tri, lf, NN, HIGHEST) + carry[...]
        cum_ref[...] = c
        carry[...] += jnp.sum(lf, axis=0, keepdims=True)
        spread = ((_iota2((LANES, D), 1) >> 6) == _iota2((LANES, D), 0)).astype(F32)
        rep_ref[...] = _dot(c, spread, NN, HIGHEST)

    return pl.pallas_call(
        body, name=name, grid=(B, nb),
        in_specs=[pl.BlockSpec((LANES, LANES), lambda b, j: (b * nb + j, 0)), pl.BlockSpec((1, LANES), lambda b, j: (0, 0))],
        out_specs=[pl.BlockSpec((LANES, LANES), lambda b, j: (b * nb + j, 0)), pl.BlockSpec((LANES, D), lambda b, j: (b * nb + j, 0))],
        out_shape=[jax.ShapeDtypeStruct((B * S, LANES), F32), jax.ShapeDtypeStruct((B * S, D), F32)],
        scratch_shapes=[pltpu.VMEM((1, LANES), F32)], compiler_params=_params("arbitrary", "arbitrary"),
    )(fpre, b_f)


def _gate_bwd(dcum_k, dcum_q, fpre, b_f, B, S, name):
    nb = S // LANES

    def body(d_ref, dq_ref, f_ref, b_ref, o_ref, db_ref, carry):
        first = (pl.program_id(0) == 0) & (pl.program_id(1) == 0)

        @pl.when(pl.program_id(1) == 0)
        def _():
            carry[...] = jnp.zeros_like(carry)

        @pl.when(first)
        def _():
            db_ref[...] = jnp.zeros_like(db_ref)

        d = d_ref[...] + dq_ref[...]
        tri = (_iota2((LANES, LANES), 0) <= _iota2((LANES, LANES), 1)).astype(F32)
        dlf = _dot(tri, d, NN, HIGHEST) + carry[...]
        carry[...] += jnp.sum(d, axis=0, keepdims=True)
        z = f_ref[...] + b_ref[...]
        dz = dlf * (1.0 / (1.0 + jnp.exp(z)))
        o_ref[...] = dz.astype(BF16)
        db_ref[...] += jnp.sum(dz, axis=0, keepdims=True)

    rev = lambda b, j: (b * nb + nb - 1 - j, 0)
    blk = pl.BlockSpec((LANES, LANES), rev)
    vec = pl.BlockSpec((1, LANES), lambda b, j: (0, 0))
    return pl.pallas_call(
        body, name=name, grid=(B, nb), in_specs=[blk, blk, blk, vec], out_specs=[blk, vec],
        out_shape=[jax.ShapeDtypeStruct((B * S, LANES), BF16), jax.ShapeDtypeStruct((1, LANES), F32)],
        scratch_shapes=[pltpu.VMEM((1, LANES), F32)], compiler_params=_params("arbitrary", "arbitrary"),
    )(dcum_k, dcum_q, fpre, b_f)


def _fox_scores(q, k, ck_row, hm, diagonal):
    TB = q.shape[0]
    qm = jnp.where(hm, q, jnp.zeros_like(q))
    t = _dot(qm, k, NT) - ck_row
    if diagonal:
        t = jnp.where(_iota2((TB, TB), 1) <= _iota2((TB, TB), 0), t, NEG)
    return qm, t


def _fox_fetch(srcs, bufs, sems, slot):
    return [pltpu.make_async_copy(src, buf.at[slot], sems.at[slot, i]) for i, (src, buf) in enumerate(zip(srcs, bufs))]


FOX_BLOCK = 256
EXP_UNDERFLOW = 104.0


def _fox_live_blocks(cum, gain_q, gain_k, B, S, D):
    H, HP = D // HEAD_DIM, D // LANES
    nb = S // FOX_BLOCK
    bound = 1.02 * HEAD_DIM * SCALE * jnp.max(jnp.abs(gain_q)) * jnp.max(jnp.abs(gain_k))
    c = cum.reshape(B, S, LANES)[:, :, :H]
    decay = c[:, 0::FOX_BLOCK, None, :] - c[:, None, FOX_BLOCK - 1::FOX_BLOCK, :]
    below = jnp.arange(nb)[None, :] < jnp.arange(nb)[:, None]
    dead = (decay < -(EXP_UNDERFLOW + 2.0 * bound)) & below[None, :, :, None]
    live = jnp.logical_not(dead.reshape(B, nb, nb, HP, 2).all(axis=-1))
    first = jnp.argmax(live, axis=2)
    last = nb - 1 - jnp.argmax(live[:, ::-1], axis=1)
    to_table = lambda t: t.transpose(0, 2, 1).reshape(B * HP * nb).astype(jnp.int32)
    return to_table(first), to_table(last)


def _fox_fwd(qh, kh, kv, cq, ck, first, B, S, D, name):
    HP = D // LANES
    TB = FOX_BLOCK
    nb = S // TB

    def body(first_ref, q_ref, cq_ref, k_ref, v_ref, ck_ref, kp_ref, vp_ref, ckp_ref, k_hbm, v_hbm, ck_hbm,
             o_ref, lse_ref, kbuf, vbuf, cbuf, sems, m_sc, l_sc, acc):
        b, qb, hp = pl.program_id(0), pl.program_id(1), pl.program_id(2)
        kb0 = first_ref[(b * HP + hp) * nb + qb]
        n_far = jnp.maximum(qb - 1 - kb0, 0)

        def fetch(kb, slot):
            rows = pl.ds(pl.multiple_of(kb * TB, TB), TB)
            srcs = (k_hbm.at[b, rows, pl.ds(pl.multiple_of(hp * LANES, LANES), LANES)],
                    v_hbm.at[b, rows, pl.ds(pl.multiple_of((HP + hp) * LANES, LANES), LANES)],
                    ck_hbm.at[b * HP + hp, :, rows])
            return _fox_fetch(srcs, (kbuf, vbuf, cbuf), sems, slot)

        @pl.when(n_far > 0)
        def _():
            for cp in fetch(kb0, 0):
                cp.start()

        m_sc[...] = jnp.full_like(m_sc, NEG)
        l_sc[...] = jnp.zeros_like(l_sc)
        acc[...] = jnp.zeros_like(acc)

        def step(k, v, ck_rows, diagonal):
            q, v = q_ref[0], v.astype(BF16)
            cqv = cq_ref[0]
            a = acc[...]
            for h, hm in enumerate(_lane_masks()):
                _, t = _fox_scores(q, k, ck_rows[h], hm, diagonal)
                cqh = _head_col(cqv, h)
                m_prev = jnp.max(m_sc[h], axis=-1, keepdims=True)
                l_prev = jnp.max(l_sc[h], axis=-1, keepdims=True)
                m_new = jnp.maximum(m_prev, jnp.max(t, axis=-1, keepdims=True) + cqh)
                alpha = jnp.exp(m_prev - m_new)
                p = jnp.exp(t + (cqh - m_new))
                l_new = alpha * l_prev + jnp.sum(p, axis=-1, keepdims=True)
                a = jnp.where(hm, alpha * a + _dot(p.astype(BF16), v, NN), a)
                m_sc[h] = jnp.broadcast_to(m_new, (TB, LANES))
                l_sc[h] = jnp.broadcast_to(l_new, (TB, LANES))
            acc[...] = a

        def walk(i, carry):
            slot = lax.rem(i, 2)

            @pl.when(i + 1 < n_far)
            def _():
                for cp in fetch(kb0 + i + 1, 1 - slot):
                    cp.start()

            for cp in fetch(kb0 + i, slot):
                cp.wait()
            step(kbuf[slot], vbuf[slot], [cbuf[slot, pl.ds(h, 1), :] for h in range(2)], False)
            return carry

        lax.fori_loop(0, n_far, walk, 0)
        pl.when(kb0 < qb)(lambda: step(kp_ref[0], vp_ref[0], [ckp_ref[0, h:h + 1, :] for h in range(2)], False))
        step(k_ref[0], v_ref[0], [ck_ref[0, h:h + 1, :] for h in range(2)], True)
        a = acc[...]
        out = a
        lse = jnp.zeros_like(a)
        for h, hm in enumerate(_lane_masks()):
            out = jnp.where(hm, a / l_sc[h], out)
            lse = jnp.where(hm, m_sc[h] + jnp.log(l_sc[h]), lse)
        o_ref[0] = out
        lse_ref[0] = lse

    out = jax.ShapeDtypeStruct((B, S, D), F32)
    spec = pltpu.PrefetchScalarGridSpec(
        num_scalar_prefetch=1, grid=(B, nb, HP),
        in_specs=_fox_q_specs(2, TB) + _fox_k_specs(TB, HP, 0) + _fox_k_specs(TB, HP, -1) + [_ANY] * 3,
        out_specs=_fox_q_specs(2, TB), scratch_shapes=_fox_k_buffers(TB) + [
            pltpu.VMEM((2, TB, LANES), F32), pltpu.VMEM((2, TB, LANES), F32), pltpu.VMEM((TB, LANES), F32)])
    k3, v3 = kh.reshape(B, S, D), kv.reshape(B, S, 2 * D)
    o, lse = pl.pallas_call(
        body, name=name, grid_spec=spec, out_shape=[out, out], compiler_params=_params("parallel", "parallel", "parallel"),
    )(first, qh.reshape(B, S, D), cq.reshape(B, S, D), k3, v3, ck, k3, v3, ck, k3, v3, ck)
    return o.reshape(B * S, D), lse.reshape(B * S, D)


_ANY = pl.BlockSpec(memory_space=pl.ANY)


def _fox_q_specs(n, TB):
    return [pl.BlockSpec((1, TB, LANES), lambda b, i, h, table: (b, i, h))] * n


def _fox_k_specs(TB, HP, shift):
    blk = lambda i: jnp.maximum(i + shift, 0)
    return [pl.BlockSpec((1, TB, LANES), lambda b, i, h, table: (b, blk(i), h)),
            pl.BlockSpec((1, TB, LANES), lambda b, i, h, table: (b, blk(i), HP + h)),
            pl.BlockSpec((1, 8, TB), lambda b, i, h, table: (b * HP + h, 0, blk(i)))]


def _fox_k_buffers(TB):
    return [pltpu.VMEM((2, TB, LANES), BF16), pltpu.VMEM((2, TB, LANES), F32), pltpu.VMEM((2, 8, TB), F32),
            pltpu.SemaphoreType.DMA((2, 3))]


def _fox_bwd_dq(qh, kh, kv, cq, ck, first, do, lse, delta, B, S, D, name):
    HP = D // LANES
    TB = FOX_BLOCK
    nb = S // TB

    def body(first_ref, q_ref, cq_ref, do_ref, l_ref, d_ref, k_ref, v_ref, ck_ref, kp_ref, vp_ref, ckp_ref,
             k_hbm, v_hbm, ck_hbm, dq_ref, dc_ref, kbuf, vbuf, cbuf, sems, acc, acc_c):
        b, qb, hp = pl.program_id(0), pl.program_id(1), pl.program_id(2)
        kb0 = first_ref[(b * HP + hp) * nb + qb]
        n_far = jnp.maximum(qb - 1 - kb0, 0)

        def fetch(kb, slot):
            rows = pl.ds(pl.multiple_of(kb * TB, TB), TB)
            srcs = (k_hbm.at[b, rows, pl.ds(pl.multiple_of(hp * LANES, LANES), LANES)],
                    v_hbm.at[b, rows, pl.ds(pl.multiple_of((HP + hp) * LANES, LANES), LANES)],
                    ck_hbm.at[b * HP + hp, :, rows])
            return _fox_fetch(srcs, (kbuf, vbuf, cbuf), sems, slot)

        @pl.when(n_far > 0)
        def _():
            for cp in fetch(kb0, 0):
                cp.start()

        acc[...] = jnp.zeros_like(acc)
        acc_c[...] = jnp.zeros_like(acc_c)

        def step(k, v, ck_rows, diagonal):
            q, v = q_ref[0], v.astype(BF16)
            dout = do_ref[0].astype(BF16)
            a, c = acc[...], acc_c[...]
            for h, hm in enumerate(_lane_masks()):
                _, t = _fox_scores(q, k, ck_rows[h], hm, diagonal)
                p = jnp.exp(t + (_head_col(cq_ref[0], h) - _head_col(l_ref[0], h)))
                dom = jnp.where(hm, dout, jnp.zeros_like(dout))
                ds = p * (_dot(dom, v, NT) - _head_col(d_ref[0], h))
                a += _dot(ds.astype(BF16), jnp.where(hm, k, jnp.zeros_like(k)), NN)
                c = jnp.where(hm, c + jnp.sum(ds, axis=-1, keepdims=True), c)
            acc[...] = a
            acc_c[...] = c

        def walk(i, carry):
            slot = lax.rem(i, 2)

            @pl.when(i + 1 < n_far)
            def _():
                for cp in fetch(kb0 + i + 1, 1 - slot):
                    cp.start()

            for cp in fetch(kb0 + i, slot):
                cp.wait()
            step(kbuf[slot], vbuf[slot], [cbuf[slot, pl.ds(h, 1), :] for h in range(2)], False)
            return carry

        lax.fori_loop(0, n_far, walk, 0)
        pl.when(kb0 < qb)(lambda: step(kp_ref[0], vp_ref[0], [ckp_ref[0, h:h + 1, :] for h in range(2)], False))
        step(k_ref[0], v_ref[0], [ck_ref[0, h:h + 1, :] for h in range(2)], True)
        dq_ref[0] = acc[...]
        dc_ref[0] = acc_c[...]

    r3 = lambda t: t.reshape(B, S, D)
    out = jax.ShapeDtypeStruct((B, S, D), F32)
    spec = pltpu.PrefetchScalarGridSpec(
        num_scalar_prefetch=1, grid=(B, nb, HP),
        in_specs=_fox_q_specs(5, TB) + _fox_k_specs(TB, HP, 0) + _fox_k_specs(TB, HP, -1) + [_ANY] * 3,
        out_specs=_fox_q_specs(2, TB),
        scratch_shapes=_fox_k_buffers(TB) + [pltpu.VMEM((TB, LANES), F32), pltpu.VMEM((TB, LANES), F32)])
    k3, v3 = r3(kh), kv.reshape(B, S, 2 * D)
    dq, dc = pl.pallas_call(
        body, name=name, grid_spec=spec, out_shape=[out, out], compiler_params=_params("parallel", "parallel", "parallel"),
    )(first, r3(qh), r3(cq), r3(do), r3(lse), r3(delta), k3, v3, ck, k3, v3, ck, k3, v3, ck)
    return dq.reshape(B * S, D), dc.reshape(B * S, D)


def _fox_bwd_dkv(qh, kh, kv, cq, ck, last, do, lse, delta, B, S, D, name):
    HP = D // LANES
    TB = FOX_BLOCK
    nb = S // TB

    def body(last_ref, k_ref, v_ref, ck_ref, q_ref, cq_ref, do_ref, l_ref, d_ref, qn_ref, cqn_ref, don_ref, ln_ref,
             dn_ref, q_hbm, cq_hbm, do_hbm, l_hbm, d_hbm, dk_ref, dv_ref, dc_ref,
             qbuf, cqbuf, dobuf, lbuf, dbuf, sems, dk_acc, dv_acc, dc_acc):
        b, kb, hp = pl.program_id(0), pl.program_id(1), pl.program_id(2)
        qb_last = last_ref[(b * HP + hp) * nb + kb]
        n_far = jnp.maximum(qb_last - kb - 1, 0)

        def fetch(qb, slot):
            rows = pl.ds(pl.multiple_of(qb * TB, TB), TB)
            cols = pl.ds(pl.multiple_of(hp * LANES, LANES), LANES)
            srcs = [t.at[b, rows, cols] for t in (q_hbm, cq_hbm, do_hbm, l_hbm, d_hbm)]
            return _fox_fetch(srcs, (qbuf, cqbuf, dobuf, lbuf, dbuf), sems, slot)

        @pl.when(n_far > 0)
        def _():
            for cp in fetch(kb + 2, 0):
                cp.start()

        dk_acc[...] = jnp.zeros_like(dk_acc)
        dv_acc[...] = jnp.zeros_like(dv_acc)
        dc_acc[...] = jnp.zeros_like(dc_acc)

        def step(q, cqv, dout, lse_t, del_t, diagonal):
            k, v = k_ref[0], v_ref[0].astype(BF16)
            dout = dout.astype(BF16)
            dk, dv, dc = dk_acc[...], dv_acc[...], dc_acc[...]
            row = _iota2((8, TB), 0)
            for h, hm in enumerate(_lane_masks()):
                qm, t = _fox_scores(q, k, ck_ref[0, h:h + 1, :], hm, diagonal)
                p = jnp.exp(t + (_head_col(cqv, h) - _head_col(lse_t, h)))
                dom = jnp.where(hm, dout, jnp.zeros_like(dout))
                ds = p * (_dot(dom, v, NT) - _head_col(del_t, h))
                dv += _dot(p.astype(BF16), dom, TN)
                dk += _dot(ds.astype(BF16), qm, TN)
                dc = jnp.where(row == h, dc - jnp.sum(ds, axis=0, keepdims=True), dc)
            dk_acc[...] = dk
            dv_acc[...] = dv
            dc_acc[...] = dc

        step(q_ref[0], cq_ref[0], do_ref[0], l_ref[0], d_ref[0], True)
        pl.when(kb < qb_last)(lambda: step(qn_ref[0], cqn_ref[0], don_ref[0], ln_ref[0], dn_ref[0], False))

        def walk(i, carry):
            slot = lax.rem(i, 2)

            @pl.when(i + 1 < n_far)
            def _():
                for cp in fetch(kb + 3 + i, 1 - slot):
                    cp.start()

            for cp in fetch(kb + 2 + i, slot):
                cp.wait()
            step(qbuf[slot], cqbuf[slot], dobuf[slot], lbuf[slot], dbuf[slot], False)
            return carry

        lax.fori_loop(0, n_far, walk, 0)
        dk_ref[0] = dk_acc[...]
        dv_ref[0] = dv_acc[...].astype(BF16)
        dc_ref[0] = dc_acc[...]

    blk = (1, TB, LANES)
    kmap = lambda b, i, h, last: (b, i, h)
    vmap = lambda b, i, h, last: (b, i, HP + h)
    cmap = lambda b, i, h, last: (b * HP + h, 0, i)
    nmap = lambda b, i, h, last: (b, jnp.minimum(i + 1, nb - 1), h)
    r3 = lambda t: t.reshape(B, S, D)
    tile = lambda dt: pltpu.VMEM((2, TB, LANES), dt)
    qside = [r3(qh), r3(cq), r3(do), r3(lse), r3(delta)]
    spec = pltpu.PrefetchScalarGridSpec(
        num_scalar_prefetch=1, grid=(B, nb, HP),
        in_specs=[pl.BlockSpec(blk, kmap), pl.BlockSpec(blk, vmap), pl.BlockSpec((1, 8, TB), cmap)]
        + [pl.BlockSpec(blk, kmap)] * 5 + [pl.BlockSpec(blk, nmap)] * 5 + [_ANY] * 5,
        out_specs=[pl.BlockSpec(blk, kmap), pl.BlockSpec(blk, kmap), pl.BlockSpec((1, 8, TB), cmap)],
        scratch_shapes=[tile(BF16), tile(F32), tile(F32), tile(F32), tile(F32), pltpu.SemaphoreType.DMA((2, 5)),
                        pltpu.VMEM((TB, LANES), F32), pltpu.VMEM((TB, LANES), F32), pltpu.VMEM((8, TB), F32)])
    dk, dv, dc = pl.pallas_call(
        body, name=name, grid_spec=spec,
        out_shape=[jax.ShapeDtypeStruct((B, S, D), F32), jax.ShapeDtypeStruct((B, S, D), BF16),
                   jax.ShapeDtypeStruct((B * HP, 8, S), F32)],
        compiler_params=_params("parallel", "parallel", "parallel"),
    )(last, r3(kh), kv.reshape(B, S, 2 * D), ck, *qside, *qside, *qside)
    return dk.reshape(B * S, D), dv.reshape(B * S, D), dc


def _loss_and_grad(y, target, name):
    T, D = y.shape
    tm = _pick(T, 256, 8)

    def body(y_ref, t_ref, p_ref, dy_ref):
        e = y_ref[...] - t_ref[...]
        dy_ref[...] = e * (1.0 / D)
        p_ref[...] = jnp.sum(e * e, axis=0, keepdims=True)[None]

    blk = pl.BlockSpec((tm, D), lambda i: (i, 0))
    return pl.pallas_call(
        body, name=name, grid=(T // tm,), in_specs=[blk, blk],
        out_specs=[pl.BlockSpec((1, 1, D), lambda i: (i, 0, 0)), blk],
        out_shape=[jax.ShapeDtypeStruct((T // tm, 1, D), F32), jax.ShapeDtypeStruct((T, D), F32)],
        compiler_params=_params("parallel"),
    )(y, target)


def _pair_sum(x, got, name):
    _, R, C = x.shape
    tr = _pick(R, max(16, (8 * PACK_ROWS * LANES) // C), 16)

    def body(c_ref, x_ref, g_ref, o_ref):
        o_ref[0] = (x_ref[0].astype(F32) + g_ref[0].astype(F32)).astype(o_ref.dtype)

    spec = pltpu.PrefetchScalarGridSpec(
        num_scalar_prefetch=1, grid=(4, R // tr),
        in_specs=[pl.BlockSpec((1, tr, C), lambda a, i, c: (2 * a + c[0], i, 0)),
                  pl.BlockSpec((1, tr, C), lambda a, i, c: (a, i, 0))],
        out_specs=pl.BlockSpec((1, tr, C), lambda a, i, c: (a, i, 0)))
    return pl.pallas_call(
        body, name=name, grid_spec=spec, out_shape=jax.ShapeDtypeStruct((4, R, C), x.dtype),
        compiler_params=_params("parallel", "parallel"),
    )(lax.axis_index("c").astype(jnp.int32).reshape(1), x, got)


def _adamw(parts, w, m, v, name):
    R, C = w.shape
    P = parts.shape[0]
    tr = _pick(R, max(16, (PACK_ROWS * LANES) // C), 16)

    def body(p_ref, w_ref, m_ref, v_ref, g_out, d_out, m_out, v_out):
        g = p_ref[0].astype(F32)
        for s in range(1, P):
            g = g + p_ref[s].astype(F32)
        mn = ADAM_B1 * m_ref[...] + (1.0 - ADAM_B1) * g
        vn = ADAM_B2 * v_ref[...] + (1.0 - ADAM_B2) * (g * g)
        m_hat = mn / (1.0 - ADAM_B1 ** ADAM_STEP)
        v_hat = vn / (1.0 - ADAM_B2 ** ADAM_STEP)
        g_out[...] = g
        d_out[...] = -ADAM_LR * (m_hat / (jnp.sqrt(v_hat) + ADAM_EPS) + ADAM_WD * w_ref[...])
        m_out[...] = mn
        v_out[...] = vn

    blk = pl.BlockSpec((tr, C), lambda i: (i, 0))
    out = jax.ShapeDtypeStruct((R, C), F32)
    return pl.pallas_call(
        body, name=name, grid=(R // tr,),
        in_specs=[pl.BlockSpec((P, tr, C), lambda i: (0, i, 0)), blk, blk, blk],
        out_specs=[blk] * 4, out_shape=[out] * 4, compiler_params=_params("parallel"),
    )(parts, w, m, v)


def _mesh_pos():
    return lax.axis_index("x"), lax.axis_index("y"), lax.axis_index("c")


def _comm_call(body, xs, out_shapes, name):
    n = len(xs)
    any_spec = pl.BlockSpec(memory_space=pl.ANY)
    return pl.pallas_call(
        body, name=name, out_shape=out_shapes, in_specs=[any_spec] * n, out_specs=[any_spec] * n,
        scratch_shapes=[pltpu.SemaphoreType.DMA((7, n)), pltpu.SemaphoreType.DMA((7, n)), pltpu.SemaphoreType.DMA((n,))],
    )(*xs)


def _all_gather(xs, name):
    n = len(xs)

    def body(*refs):
        start, finish = _gather_phases(refs[:n], refs[n:2 * n], *refs[2 * n:])
        start()
        finish()

    return _comm_call(body, xs, _gathered_shapes(xs), name)


def _gathered_shapes(xs):
    return [jax.ShapeDtypeStruct((N_DEV,) + x.shape, x.dtype) for x in xs]


def _gather_phases(x_refs, out_refs, send_sems, recv_sems, local_sems):
    n = len(x_refs)
    x_, y_, c_ = _mesh_pos()
    me, sibling = (x_, y_, c_), (x_, y_, 1 - c_)
    chips = [(1 - x_, y_), (x_, 1 - y_), (1 - x_, 1 - y_)]

    def slot(t, px, py, pc):
        return out_refs[t].at[4 * px + 2 * py + pc]

    def copy(t, k, block, to, src=None):
        return pltpu.make_async_remote_copy(
            src_ref=slot(t, *block) if src is None else src, dst_ref=slot(t, *block),
            send_sem=send_sems.at[k, t], recv_sem=recv_sems.at[k, t], device_id=to, device_id_type=MESH_ID)

    def own_copies():
        mine = [pltpu.make_async_copy(x_refs[t], slot(t, *me), local_sems.at[t]) for t in range(n)]
        first = [copy(t, 0, me, sibling, src=x_refs[t]) for t in range(n)]
        first += [copy(t, 1 + j, me, (*chip, c_), src=x_refs[t]) for j, chip in enumerate(chips) for t in range(n)]
        return mine, first

    def start():
        mine, first = own_copies()
        for cp in mine + first:
            cp.start()

    def finish():
        mine, first = own_copies()
        passed = []
        for j, chip in enumerate(chips):
            for t in range(n):
                copy(t, 1 + j, (*chip, c_), me).wait_recv()
                passed.append(copy(t, 4 + j, (*chip, c_), sibling))
                passed[-1].start()
        for t in range(n):
            copy(t, 0, sibling, me).wait_recv()
        for j, chip in enumerate(chips):
            for t in range(n):
                copy(t, 4 + j, (*chip, 1 - c_), me).wait_recv()
        for cp in first + passed:
            cp.wait_send()
        for cp in mine:
            cp.wait()

    return start, finish


def _call(body, *, name, grid, in_specs, out_specs, out_shape, scratch_shapes=(), semantics, args, gather=None,
          exchange=None):
    if gather is None and exchange is None:
        return pl.pallas_call(
            body, name=name, grid=grid, in_specs=in_specs, out_specs=out_specs, out_shape=out_shape,
            scratch_shapes=list(scratch_shapes), compiler_params=_params(*semantics))(*args), None
    (phases, shapes), gather = (GATHER, gather) if exchange is None else (CHIP_EXCHANGE, exchange)
    n, n_in, n_sc = len(gather), len(in_specs), len(scratch_shapes)
    outs = list(out_shape) if isinstance(out_shape, (list, tuple)) else [out_shape]
    specs = list(out_specs) if isinstance(out_specs, (list, tuple)) else [out_specs]
    n_out = len(outs)

    def carried(*refs):
        ins, xs = refs[:n_in], refs[n_in:n_in + n]
        ys, gs = refs[n_in + n:n_in + n + n_out], refs[n_in + n + n_out:n_in + 2 * n + n_out]
        scratch, sems = refs[n_in + 2 * n + n_out:n_in + 2 * n + n_out + n_sc], refs[n_in + 2 * n + n_out + n_sc:]
        start, finish = phases(xs, gs, *sems)
        at_first = functools.reduce(jnp.logical_and, [pl.program_id(a) == 0 for a in range(len(grid))])
        at_last = functools.reduce(jnp.logical_and, [pl.program_id(a) == grid[a] - 1 for a in range(len(grid))])
        pl.when(at_first)(start)
        body(*ins, *ys, *scratch)
        pl.when(at_last)(finish)

    results = pl.pallas_call(
        carried, name=name, grid=grid, in_specs=list(in_specs) + [_ANY] * n, out_specs=specs + [_ANY] * n,
        out_shape=outs + shapes(gather),
        scratch_shapes=list(scratch_shapes) + [pltpu.SemaphoreType.DMA((7, n)), pltpu.SemaphoreType.DMA((7, n)),
                                               pltpu.SemaphoreType.DMA((n,))],
        compiler_params=_params(*(["arbitrary"] * len(grid))))(*args, *gather)
    main = results[:n_out]
    return (main if isinstance(out_shape, (list, tuple)) else main[0]), list(results[n_out:])


def _pair_exchange(xs, name):
    n = len(xs)

    def body(*refs):
        x_refs, out_refs = refs[:n], refs[n:2 * n]
        send_sems, recv_sems, _ = refs[2 * n:]
        x_, y_, c_ = _mesh_pos()
        copies = [pltpu.make_async_remote_copy(
            src_ref=x_refs[t].at[2 * a + 1 - c_], dst_ref=out_refs[t].at[a],
            send_sem=send_sems.at[a, t], recv_sem=recv_sems.at[a, t],
            device_id=(x_, y_, 1 - c_), device_id_type=MESH_ID) for a in range(4) for t in range(n)]
        for cp in copies:
            cp.start()
        for cp in copies:
            cp.wait_recv()
        for cp in copies:
            cp.wait_send()

    return _comm_call(body, xs, [jax.ShapeDtypeStruct((4,) + x.shape[1:], x.dtype) for x in xs], name)


def _chip_exchange(xs, name):
    n = len(xs)

    def body(*refs):
        start, finish = _chip_phases(refs[:n], refs[n:2 * n], *refs[2 * n:])
        start()
        finish()

    return _comm_call(body, xs, _same_shapes(xs), name)


def _same_shapes(xs):
    return [jax.ShapeDtypeStruct(x.shape, x.dtype) for x in xs]


def _chip_phases(x_refs, out_refs, send_sems, recv_sems, local_sems):
    n = len(x_refs)
    x_, y_, c_ = _mesh_pos()
    chip = 2 * x_ + y_

    def all_copies():
        mine = [pltpu.make_async_copy(x_refs[t].at[chip], out_refs[t].at[chip], local_sems.at[t]) for t in range(n)]
        copies = []
        for k in range(1, 4):
            px, py = x_ ^ (k >> 1), y_ ^ (k & 1)
            for t in range(n):
                copies.append(pltpu.make_async_remote_copy(
                    src_ref=x_refs[t].at[2 * px + py], dst_ref=out_refs[t].at[chip],
                    send_sem=send_sems.at[k - 1, t], recv_sem=recv_sems.at[k - 1, t],
                    device_id=(px, py, c_), device_id_type=MESH_ID))
        return mine, copies

    def start():
        mine, copies = all_copies()
        for cp in mine + copies:
            cp.start()

    def finish():
        mine, copies = all_copies()
        for cp in copies:
            cp.wait_recv()
        for cp in copies:
            cp.wait_send()
        for cp in mine:
            cp.wait()

    return start, finish


GATHER = (_gather_phases, _gathered_shapes)
CHIP_EXCHANGE = (_chip_phases, _same_shapes)


class _Exchanges:
    class Job:
        def __init__(self, arrays, finish):
            self.arrays, self.finish = arrays, finish

    def __init__(self):
        self.jobs = []

    def add(self, arrays, finish):
        self.jobs.append(self.Job(arrays, finish))

    def take(self):
        return self.jobs.pop(0) if self.jobs else None


def _ffn_fwd(h, norm, w_in, w_out, tag, gather=None):
    T, D = h.shape
    F = w_out.shape[0]
    hn = _rmsnorm_fwd(h, norm, f"{tag}_norm")
    got = None
    if gather is None:
        gate, up, act = _mm_swiglu_fwd(hn, w_in, f"{tag}_in")
    else:
        (gate, up, act), got = _mm_swiglu_fwd(hn, w_in, f"{tag}_in", gather=gather)
    out = _mm(act, w_out, "nn", M=T, N=D, K=F, res=h, scale=0.5, name=f"{tag}_out")
    return out, (h, hn, gate, up, act), got


def _ffn_bwd(dout, saved, norm, w_in, w_out, tag, exchanges=None, ready=None, index=None):
    h, hn, gate, up, act = saved
    T, D = h.shape
    F = w_out.shape[0]
    job = exchanges.take() if exchanges else None
    dg, du, parts = _mm_swiglu_bwd(dout, w_out, gate, up, f"{tag}_dact", exchange=job.arrays if job else None)
    if job:
        job.finish(parts)
    dw_out = _mm(act, dout, "tn", M=F, N=D, K=T, scale=0.5, out_dtype=BF16, name=f"{tag}_dwout")
    dw_in = _mm(hn, dg, "tn", M=D, N=2 * F, K=T, b2=du, out_dtype=BF16, name=f"{tag}_dwin")
    if ready is not None:
        ready({("ffn_w_in", index): dw_in, ("ffn_w_out", index): dw_out})
    job = exchanges.take() if exchanges else None
    dhn = _mm(dg, w_in, "nt", M=T, N=D, K=2 * F, a_rest=(du,), name=f"{tag}_dhn", exchange=job.arrays if job else None)
    if job:
        dhn, parts = dhn
        job.finish(parts)
    dh, dnorm = _rmsnorm_bwd(h, norm, dhn, dout, f"{tag}_dnorm")
    return dh, dnorm, dw_in, dw_out


def _local_step(x, positions, target, W, pending=None, exchanges=None, on_grads=None):
    B, S, D = x.shape
    T = B * S
    H, HP = D // HEAD_DIM, D // LANES
    G = {}
    pending = pending or {}
    sending = lambda call: pending[call][0] if call in pending else None

    def arrived(call, got):
        if call in pending:
            pending[call][1](W, got)

    def ready(grads):
        if on_grads is not None:
            on_grads(grads)

    h0 = x.reshape(T, D)
    cs = _rope_tables(positions, "rope_tables")

    h1, ffn00, got = _ffn_fwd(h0, W["ffn_norm"][0, 0], W["ffn_w_in"][0, 0], W["ffn_w_out"][0, 0], "ffn00",
                              gather=sending("ffn00_in"))
    arrived("ffn00_in", got)
    hna = _rmsnorm_fwd(h1, W["mix_norm"][0], "mixa_norm")
    qkv = _mm(hna, W["a_w_qkv"], "nn", M=T, N=9 * D, K=D, name="a_qkv", gather=sending("a_qkv"))
    if sending("a_qkv") is not None:
        qkv, got = qkv
        arrived("a_qkv", got)
    qs, ks, os_, lses = [], [], [], []
    for g in range(3):
        qs.append(_qknorm_fwd(qkv, 3 * g * D, W["a_q_norm"][0, g], cs, D, F32, f"a_qnorm{g}", out_scale=SCALE))
        ks.append(_qknorm_fwd(qkv, (3 * g + 1) * D, W["a_k_norm"][0, g], cs, D, F32, f"a_knorm{g}"))
        o, l, *got = _band_fwd(qs[g], ks[g], qkv, g, B, S, D, f"a_band{g}", gather=sending(f"a_band{g}"))
        arrived(f"a_band{g}", got[0] if got else None)
        os_.append(o)
        lses.append(l)
    mixed, lse_a = _mix(os_, lses, "a_mix")
    h2 = _mm(mixed, W["a_w_o"], "nn", M=T, N=D, K=D, res=h1, name="a_out")
    h3, ffn01, _ = _ffn_fwd(h2, W["ffn_norm"][0, 1], W["ffn_w_in"][0, 1], W["ffn_w_out"][0, 1], "ffn01")

    hnkv = _rmsnorm_fwd(h3, W["kv_norm"], "kv_norm")
    kv = _mm(hnkv, W["kv_w_kv"], "nn", M=T, N=2 * D, K=D, name="kv_proj")
    fpre = _mm(hnkv, W["kv_w_f"], "nn", M=T, N=LANES, K=D, name="kv_gate_proj")
    b_f = jnp.pad(W["kv_b_f"], (0, LANES - H)).reshape(1, LANES)
    kh = _qknorm_fwd(kv, 0, W["kv_k_norm"], None, D, BF16, "kv_knorm")
    cum, cq = _gate_fwd(fpre, b_f, B, S, D, "kv_gate")
    ck = cum.reshape(B, S, LANES)[:, :, :H].reshape(B, S, HP, 2).transpose(0, 2, 3, 1)
    ck = jnp.pad(ck, ((0, 0), (0, 0), (0, 6), (0, 0))).reshape(B * HP, 8, S)
    h4, ffn10, _ = _ffn_fwd(h3, W["ffn_norm"][1, 0], W["ffn_w_in"][1, 0], W["ffn_w_out"][1, 0], "ffn10")
    hnb = _rmsnorm_fwd(h4, W["mix_norm"][1], "mixb_norm")
    qraw = _mm(hnb, W["b_w_q"], "nn", M=T, N=D, K=D, name="b_q")
    qh = _qknorm_fwd(qraw, 0, W["b_q_norm"][0], None, D, BF16, "b_qnorm", out_scale=SCALE)
    first, last = _fox_live_blocks(cum, W["b_q_norm"][0], W["kv_k_norm"], B, S, D)
    ob, lse_b = _fox_fwd(qh, kh, kv, cq, ck, first, B, S, D, "b_fox")
    h5 = _mm(ob, W["b_w_o"], "nn", M=T, N=D, K=D, res=h4, name="b_out")
    h6, ffn11, _ = _ffn_fwd(h5, W["ffn_norm"][1, 1], W["ffn_w_in"][1, 1], W["ffn_w_out"][1, 1], "ffn11")

    sq, dy = _loss_and_grad(h6, target.reshape(T, D), "loss")

    dn = [[None, None], [None, None]]
    dwi = [[None, None], [None, None]]
    dwo = [[None, None], [None, None]]
    dh5, dn[1][1], dwi[1][1], dwo[1][1] = _ffn_bwd(dy, ffn11, W["ffn_norm"][1, 1], W["ffn_w_in"][1, 1], W["ffn_w_out"][1, 1], "ffn11",
                                                   exchanges, ready, 3)
    dob = _mm(dh5, W["b_w_o"], "nt", M=T, N=D, K=D, name="b_dout")
    G["b_w_o"] = _mm(ob, dh5, "tn", M=D, N=D, K=T, out_dtype=BF16, name="b_dwo")[None]
    delta_b = _head_rowsum(dob, ob, "b_delta")
    dqh, dcq = _fox_bwd_dq(qh, kh, kv, cq, ck, first, dob, lse_b, delta_b, B, S, D, "b_fox_dq")
    dkh, dv_b, dck = _fox_bwd_dkv(qh, kh, kv, cq, ck, last, dob, lse_b, delta_b, B, S, D, "b_fox_dkv")
    dqraw, dg = _qknorm_bwd(qraw, 0, W["b_q_norm"][0], None, dqh, D, "b_dqnorm", out_scale=SCALE)
    G["b_q_norm"] = dg[None]
    dhnb = _mm(dqraw, W["b_w_q"], "nt", M=T, N=D, K=D, name="b_dhn")
    G["b_w_q"] = _mm(hnb, dqraw, "tn", M=D, N=D, K=T, out_dtype=BF16, name="b_dwq")[None]
    dh4, dmix_b = _rmsnorm_bwd(h4, W["mix_norm"][1], dhnb, dh5, "mixb_dnorm")
    dh3, dn[1][0], dwi[1][0], dwo[1][0] = _ffn_bwd(dh4, ffn10, W["ffn_norm"][1, 0], W["ffn_w_in"][1, 0], W["ffn_w_out"][1, 0], "ffn10",
                                                   exchanges, ready, 2)

    dkraw, G["kv_k_norm"] = _qknorm_bwd(kv, 0, W["kv_k_norm"], None, dkh, D, "kv_dknorm")
    dcum = dck.reshape(B, HP, 8, S)[:, :, :2].transpose(0, 3, 1, 2).reshape(T, H)
    dcum = jnp.pad(dcum, ((0, 0), (0, LANES - H)))
    dcum_q = jnp.pad(dcq.reshape(T, H, HEAD_DIM)[:, :, 0], ((0, 0), (0, LANES - H)))
    dfpre, db_f = _gate_bwd(dcum, dcum_q, fpre, b_f, B, S, "kv_dgate")
    G["kv_b_f"] = db_f[0, :H]
    dhnkv = _mm(dkraw, W["kv_w_kv"], "nt", M=T, N=D, K=D, name="kv_dhn_k")
    dhnkv = _mm(dv_b, W["kv_w_kv"], "nt", M=T, N=D, K=D, b_off=(0, D), res=dhnkv, name="kv_dhn_v")
    dhnkv = _mm(dfpre, W["kv_w_f"], "nt", M=T, N=D, K=LANES, res=dhnkv, name="kv_dhn_f")
    dw_k = _mm(hnkv, dkraw, "tn", M=D, N=D, K=T, out_dtype=BF16, name="kv_dwk")
    dw_v = _mm(hnkv, dv_b, "tn", M=D, N=D, K=T, out_dtype=BF16, name="kv_dwv")
    dw_f = _mm(hnkv, dfpre, "tn", M=D, N=LANES, K=T, out_dtype=BF16, name="kv_dwf")
    G["kv_w"] = jnp.concatenate([dw_k, dw_v, dw_f[:, :H]], axis=1)
    dh3, G["kv_norm"] = _rmsnorm_bwd(h3, W["kv_norm"], dhnkv, dh3, "kv_dnorm")
    ready({"kv_w": G["kv_w"], "b_w_q": G["b_w_q"][0], "b_w_o": G["b_w_o"][0]})

    dh2, dn[0][1], dwi[0][1], dwo[0][1] = _ffn_bwd(dh3, ffn01, W["ffn_norm"][0, 1], W["ffn_w_in"][0, 1], W["ffn_w_out"][0, 1], "ffn01",
                                                   exchanges, ready, 1)
    dmixed = _mm(dh2, W["a_w_o"], "nt", M=T, N=D, K=D, name="a_dmixed")
    G["a_w_o"] = _mm(mixed, dh2, "tn", M=D, N=D, K=T, out_dtype=BF16, name="a_dwo")[None]
    ready({"a_w_o": G["a_w_o"][0]})
    delta_a = _head_rowsum(dmixed, mixed, "a_delta")
    dhna = None
    dw_qkv, dqn, dkn = [], [], []
    for g in range(3):
        dq, dk, dv = _band_bwd(qs[g], ks[g], qkv, dmixed, lse_a, delta_a, g, B, S, D, f"a_dband{g}", exchanges)
        dqr, dgq = _qknorm_bwd(qkv, 3 * g * D, W["a_q_norm"][0, g], cs, dq, D, f"a_dqnorm{g}", out_scale=SCALE)
        dkr, dgk = _qknorm_bwd(qkv, (3 * g + 1) * D, W["a_k_norm"][0, g], cs, dk, D, f"a_dknorm{g}")
        dqn.append(dgq)
        dkn.append(dgk)
        dhna = _mm(dqr, W["a_w_qkv"], "nt", M=T, N=D, K=3 * D, a_rest=(dkr, dv), b_off=(0, 3 * g * D), res=dhna,
                   name=f"a_dhn{g}")
        for j, d in enumerate((dqr, dkr, dv)):
            dw_qkv.append(_mm(hna, d, "tn", M=D, N=D, K=T, out_dtype=BF16, name=f"a_dwqkv{3 * g + j}"))
    G["a_w_qkv"] = jnp.concatenate(dw_qkv, axis=1)[None]
    ready({"a_w_qkv": G["a_w_qkv"][0]})
    G["a_q_norm"] = jnp.stack(dqn)[None]
    G["a_k_norm"] = jnp.stack(dkn)[None]
    dh1, dmix_a = _rmsnorm_bwd(h1, W["mix_norm"][0], dhna, dh2, "mixa_dnorm")
    dh0, dn[0][0], dwi[0][0], dwo[0][0] = _ffn_bwd(dh1, ffn00, W["ffn_norm"][0, 0], W["ffn_w_in"][0, 0], W["ffn_w_out"][0, 0], "ffn00",
                                                   exchanges, ready, 0)

    G["mix_norm"] = jnp.stack([dmix_a, dmix_b])
    G["ffn_norm"] = jnp.stack([jnp.stack(r) for r in dn])
    G["ffn_w_in"] = jnp.stack([jnp.stack(r) for r in dwi])
    G["ffn_w_out"] = jnp.stack([jnp.stack(r) for r in dwo])
    return jnp.sum(sq), dh0.reshape(B, S, D), G


SMALL =["mix_norm", "kv_norm", "a_q_norm", "a_k_norm", "kv_b_f", "kv_k_norm", "b_q_norm"]


def _pack(blocks, lead):
    flat = [b.reshape(b.shape[:lead] + (-1,)) for b in blocks]
    flat = jnp.concatenate(flat, axis=lead)
    n = flat.shape[-1]
    unit = PACK_ROWS * LANES
    padded = -(-n // unit) * unit if n > unit else -(-n // (16 * LANES)) * 16 * LANES
    flat = jnp.pad(flat, [(0, 0)] * lead + [(0, padded - n)])
    return flat.reshape(flat.shape[:lead] + (padded // LANES, LANES))


def _unpack(packed, shapes, lead):
    flat = packed.reshape(packed.shape[:lead] + (-1,))
    out, off = [], 0
    for shp in shapes:
        n = int(np.prod(shp))
        out.append(flat[..., off:off + n].reshape(packed.shape[:lead] + tuple(shp)))
        off += n
    return out


def kernel(x, positions, ffn_norm, ffn_w_in, ffn_w_out, mix_norm, a_w_qkv, a_q_norm, a_k_norm, a_w_o, kv_norm, kv_w, kv_b_f, kv_k_norm, b_w_q, b_q_norm, b_w_o, loss_target, m_ffn_norm, m_ffn_w_in, m_ffn_w_out, m_mix_norm, m_a_w_qkv, m_a_q_norm, m_a_k_norm, m_a_w_o, m_kv_norm, m_kv_w, m_kv_b_f, m_kv_k_norm, m_b_w_q, m_b_q_norm, m_b_w_o, v_ffn_norm, v_ffn_w_in, v_ffn_w_out, v_mix_norm, v_a_w_qkv, v_a_q_norm, v_a_k_norm, v_a_w_o, v_kv_norm, v_kv_w, v_kv_b_f, v_kv_k_norm, v_b_w_q, v_b_q_norm, v_b_w_o):
    names = ["ffn_norm", "ffn_w_in", "ffn_w_out", "mix_norm", "a_w_qkv", "a_q_norm", "a_k_norm", "a_w_o", "kv_norm",
             "kv_w", "kv_b_f", "kv_k_norm", "b_w_q", "b_q_norm", "b_w_o"]
    w = dict(zip(names, (ffn_norm, ffn_w_in, ffn_w_out, mix_norm, a_w_qkv, a_q_norm, a_k_norm, a_w_o, kv_norm, kv_w,
                         kv_b_f, kv_k_norm, b_w_q, b_q_norm, b_w_o)))
    m = dict(zip(names, (m_ffn_norm, m_ffn_w_in, m_ffn_w_out, m_mix_norm, m_a_w_qkv, m_a_q_norm, m_a_k_norm, m_a_w_o,
                         m_kv_norm, m_kv_w, m_kv_b_f, m_kv_k_norm, m_b_w_q, m_b_q_norm, m_b_w_o)))
    v = dict(zip(names, (v_ffn_norm, v_ffn_w_in, v_ffn_w_out, v_mix_norm, v_a_w_qkv, v_a_q_norm, v_a_k_norm, v_a_w_o,
                         v_kv_norm, v_kv_w, v_kv_b_f, v_kv_k_norm, v_b_w_q, v_b_q_norm, v_b_w_o)))
    B, S, D = x.shape
    H = D // HEAD_DIM
    F = ffn_w_out.shape[2] * N_DEV
    Ds, Fs = D // N_DEV, F // N_DEV
    me = 4 * lax.axis_index("x") + 2 * lax.axis_index("y") + lax.axis_index("c")
    squares = ("a_w_o", "b_w_q", "b_w_o")

    def shard_views(t):
        return [t["ffn_w_in"].reshape(4 * D, -1), t["ffn_w_out"].reshape(-1, D), t["a_w_qkv"][0], t["kv_w"],
                jnp.concatenate([t[n][0] for n in squares], axis=0)]

    def to_shards(views):
        sq3 = views[4].reshape(3, 1, Ds, D)
        out = {"ffn_w_in": views[0].reshape(ffn_w_in.shape), "ffn_w_out": views[1].reshape(ffn_w_out.shape),
               "a_w_qkv": views[2][None], "kv_w": views[3]}
        out.update({n: sq3[i] for i, n in enumerate(squares)})
        return out

    sh_in, sh_out, sh_qkv, sh_kv, sh_sq = [s.astype(BF16) for s in shard_views(w)]
    in_of, out_of = (lambda f: sh_in[f * D:(f + 1) * D]), (lambda f: sh_out[f * Fs:(f + 1) * Fs])
    square_of = lambda i: sh_sq[i * Ds:(i + 1) * Ds]
    side_by_side = lambda g: g.transpose(1, 0, 2).reshape(g.shape[1], N_DEV * g.shape[2])
    stacked = lambda g: g.reshape(N_DEV * g.shape[1], g.shape[2])

    def store_ffn(f):
        def store(W, got):
            W["ffn_w_in"][f // 2, f % 2], W["ffn_w_out"][f // 2, f % 2] = side_by_side(got[0]), stacked(got[1])
        return store

    def store_mixer_a(W, got):
        W["a_w_qkv"], W["a_w_o"] = side_by_side(got[0]), stacked(got[1])

    def store_layer_b(W, got):
        store_ffn(1)(W, got)
        kv_full = side_by_side(got[2])
        W["kv_w_kv"] = kv_full[:, :2 * D]
        W["kv_w_f"] = jnp.pad(kv_full[:, 2 * D:], ((0, 0), (0, LANES - H)))
        W["b_w_q"], W["b_w_o"] = stacked(got[3]), stacked(got[4])

    norm_rows = jnp.pad(ffn_norm.reshape(4, Ds), ((0, 4), (0, 0)))
    first = _all_gather([in_of(0), out_of(0), norm_rows], "gather_weights")
    W = {n: w[n] for n in SMALL}
    W["ffn_norm"] = first[2][:, :4].transpose(1, 0, 2).reshape(2, 2, D)
    W["ffn_w_in"], W["ffn_w_out"] = {}, {}
    store_ffn(0)(W, first)
    pending = {"ffn00_in": ([sh_qkv, square_of(0)], store_mixer_a),
               "a_qkv": ([in_of(1), out_of(1), sh_kv, square_of(1), square_of(2)], store_layer_b),
               "a_band0": ([in_of(2), out_of(2)], store_ffn(2)),
               "a_band1": ([in_of(3), out_of(3)], store_ffn(3))}

    wmv = [shard_views(t) for t in (w, m, v)]
    columns = lambda g: g.reshape(g.shape[0], N_DEV, -1).transpose(1, 0, 2)
    rows = lambda g: g.reshape(N_DEV, g.shape[0] // N_DEV, g.shape[1])
    square_index = {n: i for i, n in enumerate(squares)}

    def placed(key):
        name, f = key if isinstance(key, tuple) else (key, None)
        if name == "ffn_w_in":
            return columns, 0, slice(f * D, (f + 1) * D)
        if name == "ffn_w_out":
            return rows, 1, slice(f * Fs, (f + 1) * Fs)
        if name in square_index:
            return rows, 4, slice(square_index[name] * Ds, (square_index[name] + 1) * Ds)
        return columns, {"a_w_qkv": 2, "kv_w": 3}[name], slice(None)

    exchanges = _Exchanges()
    updated = [{} for _ in wmv[0]]
    groups = []

    def on_grads(grads):
        tag = str(len(groups))
        groups.append(tag)
        keys = list(grads)
        blocks = [placed(k)[0](grads[k]).astype(BF16) for k in keys]
        got = _pair_exchange(blocks, "exchange_pair" + tag)
        sums = [_pair_sum(b, s, f"pair_sum{tag}_{i}") for i, (b, s) in enumerate(zip(blocks, got))]

        def finish(parts):
            for i, k in enumerate(keys):
                _, view, span = placed(k)
                outs = _adamw(parts[i], *(t[view][span] for t in wmv), f"adamw{tag}_{i}")
                updated[view][span.start or 0] = outs

        exchanges.add(sums, finish)

    sq, grad_x, G = _local_step(x, positions, loss_target, W, pending, exchanges, on_grads)
    loss = lax.psum(sq * (0.5 / D), ("x", "y", "c"))
    while exchanges.jobs:
        job = exchanges.take()
        job.finish(_chip_exchange(job.arrays, "exchange_chips" + str(len(exchanges.jobs))))
    new = {}
    for k, kind in enumerate(("grad", "delta", "new_m", "new_v")):
        whole = [jnp.concatenate([u[r][k] for r in sorted(u)], axis=0) for u in updated]
        for n, arr in to_shards(whole).items():
            new[kind, n] = arr

    small_all = ["ffn_norm"] + SMALL
    vec_parts = _all_gather([_pack([G[n] for n in small_all], 0)], "gather_vector_grads")[0]
    vec_parts = _unpack(vec_parts, [G[n].shape for n in small_all], 1)
    norm_g = lax.dynamic_slice_in_dim(vec_parts[0], me * ffn_norm.shape[2], ffn_norm.shape[2], axis=3)
    rep_g = _pack(vec_parts[1:], 1)
    for ns, parts_s in ((["ffn_norm"], _pack([norm_g], 1)), (SMALL, rep_g)):
        outs = _adamw(parts_s, *(_pack([t[n] for n in ns], 0) for t in (w, m, v)), "adamw_" + ns[0])
        for kind, packed in zip(("grad", "delta", "new_m", "new_v"), outs):
            for n, arr in zip(ns, _unpack(packed, [w[n].shape for n in ns], 0)):
                new[kind, n] = arr

    result = [loss, grad_x]
    for kind in ("grad", "delta", "new_m", "new_v"):
        result += [new[kind, n] for n in names]
    return tuple(result)
```

```python
import functools

import jax
import jax.numpy as jnp
import numpy as np
from jax import lax
from jax.experimental import pallas as pl
from jax.experimental.pallas import tpu as pltpu

F32 = jnp.float32
BF16 = jnp.bfloat16
HIGHEST = lax.Precision.HIGHEST

HEAD_DIM = 64
ROT_DIM = HEAD_DIM // 4
ROPE_THETA = 500000.0
DILATED_GROUPS = ((128, 1), (512, 4), (2048, 16))
BAND_BLOCK = 128
EPS = 1e-6
ADAM_LR, ADAM_B1, ADAM_B2, ADAM_EPS, ADAM_WD, ADAM_STEP = 0.001, 0.9, 0.999, 1e-08, 0.01, 10

LANES = 128
N_DEV = 8
VMEM_LIMIT = 56 * 1024 * 1024
NEG = -1e30
SCALE = HEAD_DIM ** -0.5
PACK_ROWS = 1024
MESH_ID = pl.DeviceIdType.MESH


def _params(*sem):
    return pltpu.CompilerParams(dimension_semantics=sem, vmem_limit_bytes=VMEM_LIMIT)


def _divs(n, cap, mult=LANES):
    d = [t for t in range(mult, min(n, cap) + 1, mult) if n % t == 0]
    return d or [n]


def _pick(n, cap, mult=LANES):
    return _divs(n, cap, mult)[-1]


def _mm_tiles(M, N, K, budget=1408 * 1024):
    best = None
    for tm in _divs(M, 1408):
        for tn in _divs(N, 1408):
            if tm * tn <= budget and (best is None or tm * tn > best[0] * best[1]):
                best = (tm, tn)
    if best is None:
        best = (_divs(M, 1408)[0], _divs(N, 1408)[0])
    tm, tn = best
    tks = [t for t in _divs(K, 2048) if (tm + tn) * t <= 3 * 1024 * 1024]
    tk = tks[-1] if tks else _divs(K, 2048)[0]
    return tm, tn, tk


def _dot(a, b, dims, precision=None):
    return lax.dot_general(a, b, (dims, ((), ())), precision=precision, preferred_element_type=F32)


NN = ((1,), (0,))
NT = ((1,), (1,))
TN = ((0,), (0,))


def _dot_split(x, mat):
    hi = x.astype(BF16)
    lo = (x - hi.astype(F32)).astype(BF16)
    m = mat.astype(BF16)
    return _dot(hi, m, NN) + _dot(lo, m, NN)


def _dot_once(x, mat):
    return _dot(x.astype(BF16), mat.astype(BF16), NN)


def _mm(a, b, mode, *, M, N, K, a_off=(0, 0), b_off=(0, 0), a_rest=(), b2=None, res=None, scale=1.0, out_dtype=F32,
        name, gather=None, exchange=None):
    a_parts = 1 + len(a_rest)
    tm, tn, tk = _mm_tiles(M, N // 2 if b2 is not None else N, K // a_parts)
    nk = K // tk
    k_part, n_half = nk // a_parts, (N // tn) // 2
    if mode == "tn":
        a_blk, b_blk, dims = (tk, tm), (tk, tn), TN
    elif mode == "nt":
        a_blk, b_blk, dims = (tm, tk), (tn, tk), NT
    else:
        a_blk, b_blk, dims = (tm, tk), (tk, tn), NN
    for off, blk in ((a_off, a_blk), (b_off, b_blk)):
        assert off[0] % blk[0] == 0 and off[1] % blk[1] == 0, (name, off, blk)
    ao = (a_off[0] // a_blk[0], a_off[1] // a_blk[1])
    bo = (b_off[0] // b_blk[0], b_off[1] // b_blk[1])
    if mode == "tn":
        a_map = lambda i, j, k: (k + ao[0], i + ao[1])
        b_map = lambda i, j, k: (k + bo[0], j + bo[1])
    elif mode == "nt":
        a_map = lambda i, j, k: (i + ao[0], k + ao[1])
        b_map = lambda i, j, k: (j + bo[0], k + bo[1])
    else:
        a_map = lambda i, j, k: (i + ao[0], k + ao[1])
        b_map = lambda i, j, k: (k + bo[0], j + bo[1])
    has_res = res is not None
    if a_rest:
        assert mode != "tn" and a_off == (0, 0)
        part_map = lambda p: (lambda i, j, k: (i, jnp.clip(k - p * k_part, 0, k_part - 1)))
        a_map = part_map(0)
    if b2 is not None:
        assert mode != "nt" and b_off == (0, 0)
        b_map = lambda i, j, k: (k, jnp.minimum(j, n_half - 1))
        b2_map = lambda i, j, k: (k, jnp.maximum(j - n_half, 0))

    def body(*refs):
        refs = list(refs)
        a_ref, b_ref = refs.pop(0), refs.pop(0)
        a_val, b_val = a_ref[...].astype(BF16), b_ref[...].astype(BF16)
        for p in range(1, a_parts):
            a_val = jnp.where(pl.program_id(2) >= p * k_part, refs.pop(0)[...].astype(BF16), a_val)
        if b2 is not None:
            b_val = jnp.where(pl.program_id(1) < n_half, b_val, refs.pop(0)[...].astype(BF16))
        r_ref = refs.pop(0) if has_res else None
        o_ref = refs.pop(0)

        def finish(total):
            out = total * scale
            if has_res:
                out = r_ref[...] + out
            o_ref[...] = out.astype(out_dtype)

        part = _dot(a_val, b_val, dims)
        if nk == 1:
            finish(part)
            return
        acc = refs[-1]
        k = pl.program_id(2)

        @pl.when(k == 0)
        def _():
            acc[...] = part

        @pl.when(k > 0)
        def _():
            acc[...] += part

        pl.when(k == nk - 1)(lambda: finish(acc[...]))

    in_specs = [pl.BlockSpec(a_blk, a_map), pl.BlockSpec(b_blk, b_map)]
    args = [a, b]
    for p, extra in enumerate(a_rest, 1):
        in_specs.append(pl.BlockSpec(a_blk, part_map(p)))
        args.append(extra)
    if b2 is not None:
        in_specs.append(pl.BlockSpec(b_blk, b2_map))
        args.append(b2)
    if has_res:
        in_specs.append(pl.BlockSpec((tm, tn), lambda i, j, k: (i, j)))
        args.append(res)
    out, gathered = _call(
        body, name=name, grid=(M // tm, N // tn, nk), in_specs=in_specs,
        out_specs=pl.BlockSpec((tm, tn), lambda i, j, k: (i, j)),
        out_shape=jax.ShapeDtypeStruct((M, N), out_dtype),
        scratch_shapes=[pltpu.VMEM((tm, tn), F32)] if nk > 1 else [],
        semantics=("parallel", "parallel", "arbitrary"), args=args, gather=gather, exchange=exchange)
    return out if gather is None and exchange is None else (out, gathered)


def _rmsnorm_fwd(x, g, name):
    T, D = x.shape
    tm = _pick(T, 512, 8)

    def body(x_ref, g_ref, y_ref):
        xv = x_ref[...]
        r = lax.rsqrt(jnp.mean(xv * xv, axis=-1, keepdims=True) + EPS)
        y_ref[...] = (xv * r * g_ref[...]).astype(BF16)

    return pl.pallas_call(
        body, name=name, grid=(T // tm,),
        in_specs=[pl.BlockSpec((tm, D), lambda i: (i, 0)), pl.BlockSpec((1, D), lambda i: (0, 0))],
        out_specs=pl.BlockSpec((tm, D), lambda i: (i, 0)),
        out_shape=jax.ShapeDtypeStruct((T, D), BF16), compiler_params=_params("parallel"),
    )(x, g.reshape(1, D))


def _rmsnorm_bwd(x, g, dy, dres, name):
    T, D = x.shape
    tm = _pick(T, 512, 8)

    def body(x_ref, g_ref, dy_ref, dres_ref, dx_ref, dg_ref):
        xv = x_ref[...]
        r = lax.rsqrt(jnp.mean(xv * xv, axis=-1, keepdims=True) + EPS)
        xn = xv * r
        d = dy_ref[...]
        dgy = d * g_ref[...]
        dx_ref[...] = dres_ref[...] + r * (dgy - xn * jnp.mean(dgy * xn, axis=-1, keepdims=True))

        @pl.when(pl.program_id(0) == 0)
        def _():
            dg_ref[...] = jnp.zeros_like(dg_ref)

        dg_ref[...] += jnp.sum(d * xn, axis=0, keepdims=True)

    row = pl.BlockSpec((tm, D), lambda i: (i, 0))
    vec = pl.BlockSpec((1, D), lambda i: (0, 0))
    dx, dg = pl.pallas_call(
        body, name=name, grid=(T // tm,), in_specs=[row, vec, row, row], out_specs=[row, vec],
        out_shape=[jax.ShapeDtypeStruct((T, D), F32), jax.ShapeDtypeStruct((1, D), F32)],
        compiler_params=_params("arbitrary"),
    )(x, g.reshape(1, D), dy, dres)
    return dx, dg.reshape(D)


def _sigmoid(x):
    return 1.0 / (1.0 + jnp.exp(-x))


MXU_WIDTH = 256


def _col_chunks(n):
    return [(lo, min(lo + MXU_WIDTH, n)) for lo in range(0, n, MXU_WIDTH)]


def _mm_swiglu_fwd(hn, w_in, name, gather=None):
    T, D = hn.shape
    F = w_in.shape[1] // 2
    tm, tn, _ = _mm_tiles(T, F, D)
    assert (tm + 2 * tn) * D <= 4 * 1024 * 1024, "the whole contraction is held in one step"
    nf = F // tn

    def body(a_ref, bg_ref, bu_ref, g_ref, u_ref, act_ref):
        a = a_ref[...].astype(BF16)
        for lo, hi in _col_chunks(tn):
            gv = _dot(a, bg_ref[:, lo:hi].astype(BF16), NN)
            uv = _dot(a, bu_ref[:, lo:hi].astype(BF16), NN)
            g_ref[:, lo:hi] = gv.astype(BF16)
            u_ref[:, lo:hi] = uv.astype(BF16)
            act_ref[:, lo:hi] = (gv * _sigmoid(gv) * uv).astype(BF16)

    tile = pl.BlockSpec((tm, tn), lambda j, i: (i, j))
    out = jax.ShapeDtypeStruct((T, F), BF16)
    outs, gathered = _call(
        body, name=name, grid=(nf, T // tm),
        in_specs=[pl.BlockSpec((tm, D), lambda j, i: (i, 0)), pl.BlockSpec((D, tn), lambda j, i: (0, j)),
                  pl.BlockSpec((D, tn), lambda j, i: (0, j + nf))],
        out_specs=[tile, tile, tile], out_shape=[out, out, out], semantics=("parallel", "parallel"),
        args=(hn, w_in, w_in), gather=gather)
    return outs if gather is None else (outs, gathered)


def _mm_swiglu_bwd(dout, w_out, gate, up, name, exchange=None):
    T, D = dout.shape
    F = w_out.shape[0]
    tm, tn, _ = _mm_tiles(T, F, D)
    assert (tm + tn) * D <= 4 * 1024 * 1024, "the whole contraction is held in one step"

    def body(a_ref, b_ref, g_ref, u_ref, dg_ref, du_ref):
        a = a_ref[...].astype(BF16)
        for lo, hi in _col_chunks(tn):
            d = _dot(a, b_ref[lo:hi, :].astype(BF16), NT) * 0.5
            gv, uv = g_ref[:, lo:hi].astype(F32), u_ref[:, lo:hi].astype(F32)
            s = _sigmoid(gv)
            gs = gv * s
            dg_ref[:, lo:hi] = (d * uv * (s + gs - gs * s)).astype(BF16)
            du_ref[:, lo:hi] = (d * gs).astype(BF16)

    tile = pl.BlockSpec((tm, tn), lambda j, i: (i, j))
    out = jax.ShapeDtypeStruct((T, F), BF16)
    (dg, du), parts = _call(
        body, name=name, grid=(F // tn, T // tm),
        in_specs=[pl.BlockSpec((tm, D), lambda j, i: (i, 0)), pl.BlockSpec((tn, D), lambda j, i: (j, 0)), tile, tile],
        out_specs=[tile, tile], out_shape=[out, out], semantics=("parallel", "parallel"),
        args=(dout, w_out, gate, up), exchange=exchange)
    return dg, du, parts


def _iota2(shape, dim):
    return lax.broadcasted_iota(jnp.int32, shape, dim)


def _head_ones(value=1.0):
    r, c = _iota2((LANES, LANES), 0), _iota2((LANES, LANES), 1)
    return jnp.where((r >> 6) == (c >> 6), value, 0.0).astype(F32)


def _rot_matrix(transpose):
    r, c = _iota2((LANES, LANES), 0), _iota2((LANES, LANES), 1)
    if transpose:
        r, c = c, r
    half = ROT_DIM // 2
    cm = c & (HEAD_DIM - 1)
    neg = (r == c + half) & (cm < half)
    pos = (r == c - half) & (cm >= half) & (cm < ROT_DIM)
    return pos.astype(F32) - neg.astype(F32)


def _rope_tables(positions, name):
    T = positions.size
    tm = _pick(T, 1024, 8)
    inv = ROPE_THETA ** (-jnp.arange(0, ROT_DIM, 2, dtype=F32) / ROT_DIM)
    lane = jnp.concatenate([inv, inv, jnp.zeros((HEAD_DIM - ROT_DIM,), F32)])
    lane = jnp.tile(lane, 2).reshape(1, LANES)

    def body(p_ref, f_ref, c_ref, s_ref):
        ang = p_ref[...].astype(F32) * f_ref[...]
        c_ref[...] = jnp.cos(ang)
        s_ref[...] = jnp.sin(ang)

    out = jax.ShapeDtypeStruct((T, LANES), F32)
    blk = pl.BlockSpec((tm, LANES), lambda i: (i, 0))
    return pl.pallas_call(
        body, name=name, grid=(T // tm,),
        in_specs=[pl.BlockSpec((tm, 1), lambda i: (i, 0)), pl.BlockSpec((1, LANES), lambda i: (0, 0))],
        out_specs=[blk, blk], out_shape=[out, out], compiler_params=_params("parallel"),
    )(positions.reshape(T, 1), lane)


def _qknorm_fwd(src, col_off, gain, cs, D, out_dtype, name, out_scale=1.0):
    T = src.shape[0]
    HP = D // LANES
    tm = _pick(T, 1024, 8)
    co = col_off // LANES
    rope = cs is not None

    def body(*refs):
        if rope:
            x_ref, g_ref, c_ref, s_ref, o_ref = refs
        else:
            x_ref, g_ref, o_ref = refs
        xv = x_ref[...]
        ms = _dot_once(xv * xv, _head_ones(1.0 / HEAD_DIM))
        y = xv * lax.rsqrt(ms + EPS) * g_ref[...]
        if rope:
            y = y * c_ref[...] + _dot_once(y, _rot_matrix(False)) * s_ref[...]
        o_ref[...] = (y if out_scale == 1.0 else y * out_scale).astype(out_dtype)

    in_specs = [pl.BlockSpec((tm, LANES), lambda i, h: (i, co + h)), pl.BlockSpec((1, LANES), lambda i, h: (0, 0))]
    args = [src, jnp.tile(gain.reshape(1, HEAD_DIM), (1, 2))]
    if rope:
        in_specs += [pl.BlockSpec((tm, LANES), lambda i, h: (i, 0))] * 2
        args += list(cs)
    return pl.pallas_call(
        body, name=name, grid=(T // tm, HP), in_specs=in_specs,
        out_specs=pl.BlockSpec((tm, LANES), lambda i, h: (i, h)),
        out_shape=jax.ShapeDtypeStruct((T, D), out_dtype), compiler_params=_params("parallel", "parallel"),
    )(*args)


def _qknorm_bwd(src, col_off, gain, cs, dout, D, name, out_scale=1.0):
    T = src.shape[0]
    HP = D // LANES
    tm = _pick(T, 1024, 8)
    co = col_off // LANES
    rope = cs is not None

    def body(*refs):
        if rope:
            x_ref, g_ref, d_ref, c_ref, s_ref, dx_ref, dg_ref = refs
        else:
            x_ref, g_ref, d_ref, dx_ref, dg_ref = refs
        xv = x_ref[...]
        mean_of = _head_ones(1.0 / HEAD_DIM)
        ms = _dot_once(xv * xv, mean_of)
        r = lax.rsqrt(ms + EPS)
        xn = xv * r
        d = d_ref[...] if out_scale == 1.0 else d_ref[...] * out_scale
        if rope:
            d = d * c_ref[...] + _dot_once(d * s_ref[...], _rot_matrix(True))
        dgy = d * g_ref[...]
        mean = _dot_once(dgy * xn, mean_of)
        dx_ref[...] = (r * (dgy - xn * mean)).astype(BF16)

        @pl.when(pl.program_id(1) == 0)
        def _():
            dg_ref[...] = jnp.zeros_like(dg_ref)

        dg_ref[...] += jnp.sum(d * xn, axis=0, keepdims=True)[None]

    in_specs = [pl.BlockSpec((tm, LANES), lambda h, i: (i, co + h)), pl.BlockSpec((1, LANES), lambda h, i: (0, 0)),
                pl.BlockSpec((tm, LANES), lambda h, i: (i, h))]
    args = [src, jnp.tile(gain.reshape(1, HEAD_DIM), (1, 2)), dout]
    if rope:
        in_specs += [pl.BlockSpec((tm, LANES), lambda h, i: (i, 0))] * 2
        args += list(cs)
    dx, dg = pl.pallas_call(
        body, name=name, grid=(HP, T // tm), in_specs=in_specs,
        out_specs=[pl.BlockSpec((tm, LANES), lambda h, i: (i, h)), pl.BlockSpec((1, 1, LANES), lambda h, i: (h, 0, 0))],
        out_shape=[jax.ShapeDtypeStruct((T, D), BF16), jax.ShapeDtypeStruct((HP, 1, LANES), F32)],
        compiler_params=_params("parallel", "arbitrary"),
    )(*args)
    return dx, dg.reshape(2 * HP, HEAD_DIM).sum(axis=0)


def _head_rowsum(a, b, name):
    T, D = a.shape
    tm = _pick(T, 1024, 8)

    def body(a_ref, b_ref, o_ref):
        o_ref[...] = _dot_split(a_ref[...] * b_ref[...], _head_ones())

    blk = pl.BlockSpec((tm, LANES), lambda i, h: (i, h))
    return pl.pallas_call(
        body, name=name, grid=(T // tm, D // LANES), in_specs=[blk, blk], out_specs=blk,
        out_shape=jax.ShapeDtypeStruct((T, D), F32), compiler_params=_params("parallel", "parallel"),
    )(a, b)


def _lane_masks():
    lane = _iota2((1, LANES), 1)
    return (lane < HEAD_DIM, lane >= HEAD_DIM)


BAND_GROUP = 4


def _band_geometry(g, S):
    window, dil = DILATED_GROUPS[g]
    chunk = BAND_BLOCK * dil
    assert window // dil == BAND_BLOCK and S % chunk == 0
    if dil == 1:
        assert (S // chunk) % BAND_GROUP == 0
        return dil, chunk, BAND_GROUP, 1
    assert dil % BAND_GROUP == 0
    return dil, chunk, 1, dil // BAND_GROUP


def _band_rows(i, rho, dil, shift=0):
    if dil > 1:
        return pl.ds(rho * BAND_GROUP + i, BAND_BLOCK, stride=dil), shift != 0
    j = i + shift
    outside = j < 0 or j >= BAND_GROUP
    return pl.ds(0 if outside else j * BAND_BLOCK, BAND_BLOCK), outside


def _stack_heads(x):
    hm0, hm1 = _lane_masks()
    zero = jnp.zeros_like(x)
    return jnp.concatenate([jnp.where(hm0, x, zero), jnp.where(hm1, x, zero)], axis=0)


def _unstack_heads(x2):
    hm0, _ = _lane_masks()
    return jnp.where(hm0, x2[:BAND_BLOCK], x2[BAND_BLOCK:])


def _band_masks():
    qi = _iota2((2 * BAND_BLOCK, 2 * BAND_BLOCK), 0) & (BAND_BLOCK - 1)
    kj = _iota2((2 * BAND_BLOCK, 2 * BAND_BLOCK), 1)
    own = (kj >= BAND_BLOCK) & (kj - BAND_BLOCK <= qi)
    prev = (kj < BAND_BLOCK) & (kj >= qi)
    return own, prev


def _band_fwd(qh, kh, qkv, g, B, S, D, name, gather=None):
    dil, CH, NB, RG = _band_geometry(g, S)
    HP = D // LANES
    vo = (3 * g + 2) * HP
    nsp = S // (CH * NB)

    def body(q_ref, k_ref, kp_ref, v_ref, vp_ref, o_ref, lse_ref):
        n, rho = pl.program_id(1), pl.program_id(3)
        own, prev = _band_masks()
        for i in range(BAND_GROUP):
            rows, _ = _band_rows(i, rho, dil)
            prows, outside = _band_rows(i, rho, dil, shift=-1)
            q, kc, vc = (t[0, rows, :].astype(BF16) for t in (q_ref, k_ref, v_ref))
            kp, vp = (t[0, prows, :].astype(BF16) for t in ((kp_ref, vp_ref) if outside else (k_ref, v_ref)))
            valid = (own | (prev & (n > 0))) if outside else (own | prev)
            s = jnp.where(valid, _dot(_stack_heads(q), jnp.concatenate([kp, kc], axis=0), NT), NEG)
            m = jnp.max(s, axis=-1, keepdims=True)
            e = jnp.exp(s - m)
            l = jnp.sum(e, axis=-1, keepdims=True)
            o2 = _dot((e * (1.0 / l)).astype(BF16), jnp.concatenate([vp, vc], axis=0), NN)
            o_ref[0, rows, :] = _unstack_heads(o2)
            lse_ref[0, rows, :] = _unstack_heads(m + jnp.log(l))

    span, chunk = (1, CH * NB, LANES), (1, CH, LANES)
    cur = lambda b, n, h, r: (b, n, h)
    prev = lambda b, n, h, r: (b, jnp.maximum(n * NB - 1, 0), h)
    vcur = lambda b, n, h, r: (b, n, vo + h)
    vprev = lambda b, n, h, r: (b, jnp.maximum(n * NB - 1, 0), vo + h)
    out = jax.ShapeDtypeStruct((B, S, D), F32)
    q3, k3, v3 = qh.reshape(B, S, D), kh.reshape(B, S, D), qkv.reshape(B, S, 9 * D)
    (o, lse), gathered = _call(
        body, name=name, grid=(B, nsp, HP, RG),
        in_specs=[pl.BlockSpec(span, cur), pl.BlockSpec(span, cur), pl.BlockSpec(chunk, prev),
                  pl.BlockSpec(span, vcur), pl.BlockSpec(chunk, vprev)],
        out_specs=[pl.BlockSpec(span, cur), pl.BlockSpec(span, cur)], out_shape=[out, out],
        semantics=("parallel", "parallel", "parallel", "arbitrary"), args=(q3, k3, k3, v3, v3), gather=gather)
    o, lse = o.reshape(B * S, D), lse.reshape(B * S, D)
    return (o, lse) if gather is None else (o, lse, gathered)


def _head_col(x, h):
    lane = _iota2((1, LANES), 1)
    return jnp.sum(jnp.where(lane == h * HEAD_DIM, x, 0.0), axis=-1, keepdims=True)


def _band_bwd(qh, kh, qkv, do, lse, delta, g, B, S, D, name, exchanges=None):
    dil, CH, NB, RG = _band_geometry(g, S)
    HP = D // LANES
    vo = (3 * g + 2) * HP
    nsp, nch = S // (CH * NB), S // CH

    def stacked_cols(tile):
        return jnp.concatenate([_head_col(tile, 0), _head_col(tile, 1)], axis=0)

    def body(q_ref, qn_ref, do_ref, don_ref, l_ref, ln_ref, d_ref, dn_ref, k_ref, kp_ref, v_ref, vp_ref,
             dq_ref, dk_ref, dv_ref):
        n, rho = pl.program_id(1), pl.program_id(3)
        own, prev = _band_masks()
        band = (_iota2((2 * BAND_BLOCK, BAND_BLOCK), 1) >= (_iota2((2 * BAND_BLOCK, BAND_BLOCK), 0) & (BAND_BLOCK - 1)))
        for i in range(BAND_GROUP):
            rows, _ = _band_rows(i, rho, dil)
            prows, p_out = _band_rows(i, rho, dil, shift=-1)
            nrows, n_out = _band_rows(i, rho, dil, shift=1)
            kc, vc = (t[0, rows, :].astype(BF16) for t in (k_ref, v_ref))
            kp, vp = (t[0, prows, :].astype(BF16) for t in ((kp_ref, vp_ref) if p_out else (k_ref, v_ref)))
            q2, do2 = (_stack_heads(t[0, rows, :].astype(BF16)) for t in (q_ref, do_ref))
            qn2, don2 = (_stack_heads(t[0, nrows, :].astype(BF16)) for t in ((qn_ref, don_ref) if n_out else (q_ref, do_ref)))
            lse2, del2 = (stacked_cols(t[0, rows, :]) for t in (l_ref, d_ref))
            lsen2, deln2 = (stacked_cols(t[0, nrows, :]) for t in ((ln_ref, dn_ref) if n_out else (l_ref, d_ref)))
            valid = (own | (prev & (n > 0))) if p_out else (own | prev)
            next_ok = (band & (n < nsp - 1)) if n_out else band
            k2, v2 = jnp.concatenate([kp, kc], axis=0), jnp.concatenate([vp, vc], axis=0)
            p = jnp.exp(jnp.where(valid, _dot(q2, k2, NT) - lse2, NEG))
            ds = (p * (_dot(do2, v2, NT) - del2)).astype(BF16)
            pc = jnp.exp(jnp.where(next_ok, _dot(qn2, kc, NT) - lsen2, NEG))
            dsc = (pc * (_dot(don2, vc, NT) - deln2)).astype(BF16)
            dq_ref[0, rows, :] = _unstack_heads(_dot(ds, k2, NN))
            dk_ref[0, rows, :] = _dot(ds[:, BAND_BLOCK:], q2, TN) + _dot(dsc, qn2, TN)
            dv_ref[0, rows, :] = _dot(p[:, BAND_BLOCK:].astype(BF16), do2, TN) + _dot(pc.astype(BF16), don2, TN)

    span, chunk = (1, CH * NB, LANES), (1, CH, LANES)
    cur = lambda b, n, h, r: (b, n, h)
    prev = lambda b, n, h, r: (b, jnp.maximum(n * NB - 1, 0), h)
    nxt = lambda b, n, h, r: (b, jnp.minimum((n + 1) * NB, nch - 1), h)
    vcur = lambda b, n, h, r: (b, n, vo + h)
    vprev = lambda b, n, h, r: (b, jnp.maximum(n * NB - 1, 0), vo + h)
    q3, k3, do3, l3, d3 = (t.reshape(B, S, D) for t in (qh, kh, do, lse, delta))
    v3 = qkv.reshape(B, S, 9 * D)
    sp, ch = (lambda m: pl.BlockSpec(span, m)), (lambda m: pl.BlockSpec(chunk, m))
    out = jax.ShapeDtypeStruct((B, S, D), F32)
    job = exchanges.take() if exchanges else None
    (dq, dk, dv), parts = _call(
        body, name=name, grid=(B, nsp, HP, RG),
        in_specs=[sp(cur), ch(nxt), sp(cur), ch(nxt), sp(cur), ch(nxt), sp(cur), ch(nxt),
                  sp(cur), ch(prev), sp(vcur), ch(vprev)],
        out_specs=[sp(cur)] * 3, out_shape=[out] * 3, semantics=("parallel", "parallel", "parallel", "arbitrary"),
        args=(q3, q3, do3, do3, l3, l3, d3, d3, k3, k3, v3, v3), exchange=job.arrays if job else None)
    if job:
        job.finish(parts)
    return dq.reshape(B * S, D), dk.reshape(B * S, D), dv.reshape(B * S, D)


def _mix(os_, lses, name):
    T, D = os_[0].shape
    tm = _pick(T, 512, 8)

    def body(o0, o1, o2, l0, l1, l2, mix_ref, lse_ref):
        a, b, c = l0[...], l1[...], l2[...]
        m = jnp.maximum(jnp.maximum(a, b), c)
        ea, eb, ec = jnp.exp(a - m), jnp.exp(b - m), jnp.exp(c - m)
        s = ea + eb + ec
        mix_ref[...] = (ea / s) * o0[...] + (eb / s) * o1[...] + (ec / s) * o2[...]
        lse_ref[...] = m + jnp.log(s)

    blk = pl.BlockSpec((tm, D), lambda i: (i, 0))
    out = jax.ShapeDtypeStruct((T, D), F32)
    return pl.pallas_call(
        body, name=name, grid=(T // tm,), in_specs=[blk] * 6, out_specs=[blk, blk], out_shape=[out, out],
        compiler_params=_params("parallel"),
    )(*os_, *lses)


def _gate_fwd(fpre, b_f, B, S, D, name):
    nb = S // LANES

    def body(f_ref, b_ref, cum_ref, rep_ref, carry):
        @pl.when(pl.program_id(1) == 0)
        def _():
            carry[...] = jnp.zeros_like(carry)

        z = f_ref[...] + b_ref[...]
        lf = jnp.minimum(z, 0.0) - jnp.log(1.0 + jnp.exp(-jnp.abs(z)))
        tri = (_iota2((LANES, LANES), 0) >= _iota2((LANES, LANES), 1)).astype(F32)
        c = _dot(tri, lf, NN, HIGHEST) + carry[...]
        cum_ref[...] = c
        carry[...] += jnp.sum(lf, axis=0, keepdims=True)
        spread = ((_iota2((LANES, D), 1) >> 6) == _iota2((LANES, D), 0)).astype(F32)
        rep_ref[...] = _dot(c, spread, NN, HIGHEST)

    return pl.pallas_call(
        body, name=name, grid=(B, nb),
        in_specs=[pl.BlockSpec((LANES, LANES), lambda b, j: (b * nb + j, 0)), pl.BlockSpec((1, LANES), lambda b, j: (0, 0))],
        out_specs=[pl.BlockSpec((LANES, LANES), lambda b, j: (b * nb + j, 0)), pl.BlockSpec((LANES, D), lambda b, j: (b * nb + j, 0))],
        out_shape=[jax.ShapeDtypeStruct((B * S, LANES), F32), jax.ShapeDtypeStruct((B * S, D), F32)],
        scratch_shapes=[pltpu.VMEM((1, LANES), F32)], compiler_params=_params("arbitrary", "arbitrary"),
    )(fpre, b_f)


def _gate_bwd(dcum_k, dcum_q, fpre, b_f, B, S, name):
    nb = S // LANES

    def body(d_ref, dq_ref, f_ref, b_ref, o_ref, db_ref, carry):
        first = (pl.program_id(0) == 0) & (pl.program_id(1) == 0)

        @pl.when(pl.program_id(1) == 0)
        def _():
            carry[...] = jnp.zeros_like(carry)

        @pl.when(first)
        def _():
            db_ref[...] = jnp.zeros_like(db_ref)

        d = d_ref[...] + dq_ref[...]
        tri = (_iota2((LANES, LANES), 0) <= _iota2((LANES, LANES), 1)).astype(F32)
        dlf = _dot(tri, d, NN, HIGHEST) + carry[...]
        carry[...] += jnp.sum(d, axis=0, keepdims=True)
        z = f_ref[...] + b_ref[...]
        dz = dlf * (1.0 / (1.0 + jnp.exp(z)))
        o_ref[...] = dz.astype(BF16)
        db_ref[...] += jnp.sum(dz, axis=0, keepdims=True)

    rev = lambda b, j: (b * nb + nb - 1 - j, 0)
    blk = pl.BlockSpec((LANES, LANES), rev)
    vec = pl.BlockSpec((1, LANES), lambda b, j: (0, 0))
    return pl.pallas_call(
        body, name=name, grid=(B, nb), in_specs=[blk, blk, blk, vec], out_specs=[blk, vec],
        out_shape=[jax.ShapeDtypeStruct((B * S, LANES), BF16), jax.ShapeDtypeStruct((1, LANES), F32)],
        scratch_shapes=[pltpu.VMEM((1, LANES), F32)], compiler_params=_params("arbitrary", "arbitrary"),
    )(dcum_k, dcum_q, fpre, b_f)


def _fox_scores(q, k, ck_row, hm, diagonal):
    TB = q.shape[0]
    qm = jnp.where(hm, q, jnp.zeros_like(q))
    t = _dot(qm, k, NT) - ck_row
    if diagonal:
        t = jnp.where(_iota2((TB, TB), 1) <= _iota2((TB, TB), 0), t, NEG)
    return qm, t


def _fox_fetch(srcs, bufs, sems, slot):
    return [pltpu.make_async_copy(src, buf.at[slot], sems.at[slot, i]) for i, (src, buf) in enumerate(zip(srcs, bufs))]


FOX_BLOCK = 256
EXP_UNDERFLOW = 104.0


def _fox_live_blocks(cum, gain_q, gain_k, B, S, D):
    H, HP = D // HEAD_DIM, D // LANES
    nb = S // FOX_BLOCK
    bound = 1.02 * HEAD_DIM * SCALE * jnp.max(jnp.abs(gain_q)) * jnp.max(jnp.abs(gain_k))
    c = cum.reshape(B, S, LANES)[:, :, :H]
    decay = c[:, 0::FOX_BLOCK, None, :] - c[:, None, FOX_BLOCK - 1::FOX_BLOCK, :]
    below = jnp.arange(nb)[None, :] < jnp.arange(nb)[:, None]
    dead = (decay < -(EXP_UNDERFLOW + 2.0 * bound)) & below[None, :, :, None]
    live = jnp.logical_not(dead.reshape(B, nb, nb, HP, 2).all(axis=-1))
    first = jnp.argmax(live, axis=2)
    last = nb - 1 - jnp.argmax(live[:, ::-1], axis=1)
    to_table = lambda t: t.transpose(0, 2, 1).reshape(B * HP * nb).astype(jnp.int32)
    return to_table(first), to_table(last)


def _fox_fwd(qh, kh, kv, cq, ck, first, B, S, D, name):
    HP = D // LANES
    TB = FOX_BLOCK
    nb = S // TB

    def body(first_ref, q_ref, cq_ref, k_ref, v_ref, ck_ref, kp_ref, vp_ref, ckp_ref, k_hbm, v_hbm, ck_hbm,
             o_ref, lse_ref, kbuf, vbuf, cbuf, sems, m_sc, l_sc, acc):
        b, qb, hp = pl.program_id(0), pl.program_id(1), pl.program_id(2)
        kb0 = first_ref[(b * HP + hp) * nb + qb]
        n_far = jnp.maximum(qb - 1 - kb0, 0)

        def fetch(kb, slot):
            rows = pl.ds(pl.multiple_of(kb * TB, TB), TB)
            srcs = (k_hbm.at[b, rows, pl.ds(pl.multiple_of(hp * LANES, LANES), LANES)],
                    v_hbm.at[b, rows, pl.ds(pl.multiple_of((HP + hp) * LANES, LANES), LANES)],
                    ck_hbm.at[b * HP + hp, :, rows])
            return _fox_fetch(srcs, (kbuf, vbuf, cbuf), sems, slot)

        @pl.when(n_far > 0)
        def _():
            for cp in fetch(kb0, 0):
                cp.start()

        m_sc[...] = jnp.full_like(m_sc, NEG)
        l_sc[...] = jnp.zeros_like(l_sc)
        acc[...] = jnp.zeros_like(acc)

        def step(k, v, ck_rows, diagonal):
            q, v = q_ref[0], v.astype(BF16)
            cqv = cq_ref[0]
            a = acc[...]
            for h, hm in enumerate(_lane_masks()):
                _, t = _fox_scores(q, k, ck_rows[h], hm, diagonal)
                cqh = _head_col(cqv, h)
                m_prev = jnp.max(m_sc[h], axis=-1, keepdims=True)
                l_prev = jnp.max(l_sc[h], axis=-1, keepdims=True)
                m_new = jnp.maximum(m_prev, jnp.max(t, axis=-1, keepdims=True) + cqh)
                alpha = jnp.exp(m_prev - m_new)
                p = jnp.exp(t + (cqh - m_new))
                l_new = alpha * l_prev + jnp.sum(p, axis=-1, keepdims=True)
                a = jnp.where(hm, alpha * a + _dot(p.astype(BF16), v, NN), a)
                m_sc[h] = jnp.broadcast_to(m_new, (TB, LANES))
                l_sc[h] = jnp.broadcast_to(l_new, (TB, LANES))
            acc[...] = a

        def walk(i, carry):
            slot = lax.rem(i, 2)

            @pl.when(i + 1 < n_far)
            def _():
                for cp in fetch(kb0 + i + 1, 1 - slot):
                    cp.start()

            for cp in fetch(kb0 + i, slot):
                cp.wait()
            step(kbuf[slot], vbuf[slot], [cbuf[slot, pl.ds(h, 1), :] for h in range(2)], False)
            return carry

        lax.fori_loop(0, n_far, walk, 0)
        pl.when(kb0 < qb)(lambda: step(kp_ref[0], vp_ref[0], [ckp_ref[0, h:h + 1, :] for h in range(2)], False))
        step(k_ref[0], v_ref[0], [ck_ref[0, h:h + 1, :] for h in range(2)], True)
        a = acc[...]
        out = a
        lse = jnp.zeros_like(a)
        for h, hm in enumerate(_lane_masks()):
            out = jnp.where(hm, a / l_sc[h], out)
            lse = jnp.where(hm, m_sc[h] + jnp.log(l_sc[h]), lse)
        o_ref[0] = out
        lse_ref[0] = lse

    out = jax.ShapeDtypeStruct((B, S, D), F32)
    spec = pltpu.PrefetchScalarGridSpec(
        num_scalar_prefetch=1, grid=(B, nb, HP),
        in_specs=_fox_q_specs(2, TB) + _fox_k_specs(TB, HP, 0) + _fox_k_specs(TB, HP, -1) + [_ANY] * 3,
        out_specs=_fox_q_specs(2, TB), scratch_shapes=_fox_k_buffers(TB) + [
            pltpu.VMEM((2, TB, LANES), F32), pltpu.VMEM((2, TB, LANES), F32), pltpu.VMEM((TB, LANES), F32)])
    k3, v3 = kh.reshape(B, S, D), kv.reshape(B, S, 2 * D)
    o, lse = pl.pallas_call(
        body, name=name, grid_spec=spec, out_shape=[out, out], compiler_params=_params("parallel", "parallel", "parallel"),
    )(first, qh.reshape(B, S, D), cq.reshape(B, S, D), k3, v3, ck, k3, v3, ck, k3, v3, ck)
    return o.reshape(B * S, D), lse.reshape(B * S, D)


_ANY = pl.BlockSpec(memory_space=pl.ANY)


def _fox_q_specs(n, TB):
    return [pl.BlockSpec((1, TB, LANES), lambda b, i, h, table: (b, i, h))] * n


def _fox_k_specs(TB, HP, shift):
    blk = lambda i: jnp.maximum(i + shift, 0)
    return [pl.BlockSpec((1, TB, LANES), lambda b, i, h, table: (b, blk(i), h)),
            pl.BlockSpec((1, TB, LANES), lambda b, i, h, table: (b, blk(i), HP + h)),
            pl.BlockSpec((1, 8, TB), lambda b, i, h, table: (b * HP + h, 0, blk(i)))]


def _fox_k_buffers(TB):
    return [pltpu.VMEM((2, TB, LANES), BF16), pltpu.VMEM((2, TB, LANES), F32), pltpu.VMEM((2, 8, TB), F32),
            pltpu.SemaphoreType.DMA((2, 3))]


def _fox_bwd_dq(qh, kh, kv, cq, ck, first, do, lse, delta, B, S, D, name):
    HP = D // LANES
    TB = FOX_BLOCK
    nb = S // TB

    def body(first_ref, q_ref, cq_ref, do_ref, l_ref, d_ref, k_ref, v_ref, ck_ref, kp_ref, vp_ref, ckp_ref,
             k_hbm, v_hbm, ck_hbm, dq_ref, dc_ref, kbuf, vbuf, cbuf, sems, acc, acc_c):
        b, qb, hp = pl.program_id(0), pl.program_id(1), pl.program_id(2)
        kb0 = first_ref[(b * HP + hp) * nb + qb]
        n_far = jnp.maximum(qb - 1 - kb0, 0)

        def fetch(kb, slot):
            rows = pl.ds(pl.multiple_of(kb * TB, TB), TB)
            srcs = (k_hbm.at[b, rows, pl.ds(pl.multiple_of(hp * LANES, LANES), LANES)],
                    v_hbm.at[b, rows, pl.ds(pl.multiple_of((HP + hp) * LANES, LANES), LANES)],
                    ck_hbm.at[b * HP + hp, :, rows])
            return _fox_fetch(srcs, (kbuf, vbuf, cbuf), sems, slot)

        @pl.when(n_far > 0)
        def _():
            for cp in fetch(kb0, 0):
                cp.start()

        acc[...] = jnp.zeros_like(acc)
        acc_c[...] = jnp.zeros_like(acc_c)

        def step(k, v, ck_rows, diagonal):
            q, v = q_ref[0], v.astype(BF16)
            dout = do_ref[0].astype(BF16)
            a, c = acc[...], acc_c[...]
            for h, hm in enumerate(_lane_masks()):
                _, t = _fox_scores(q, k, ck_rows[h], hm, diagonal)
                p = jnp.exp(t + (_head_col(cq_ref[0], h) - _head_col(l_ref[0], h)))
                dom = jnp.where(hm, dout, jnp.zeros_like(dout))
                ds = p * (_dot(dom, v, NT) - _head_col(d_ref[0], h))
                a += _dot(ds.astype(BF16), jnp.where(hm, k, jnp.zeros_like(k)), NN)
                c = jnp.where(hm, c + jnp.sum(ds, axis=-1, keepdims=True), c)
            acc[...] = a
            acc_c[...] = c

        def walk(i, carry):
            slot = lax.rem(i, 2)

            @pl.when(i + 1 < n_far)
            def _():
                for cp in fetch(kb0 + i + 1, 1 - slot):
                    cp.start()

            for cp in fetch(kb0 + i, slot):
                cp.wait()
            step(kbuf[slot], vbuf[slot], [cbuf[slot, pl.ds(h, 1), :] for h in range(2)], False)
            return carry

        lax.fori_loop(0, n_far, walk, 0)
        pl.when(kb0 < qb)(lambda: step(kp_ref[0], vp_ref[0], [ckp_ref[0, h:h + 1, :] for h in range(2)], False))
        step(k_ref[0], v_ref[0], [ck_ref[0, h:h + 1, :] for h in range(2)], True)
        dq_ref[0] = acc[...]
        dc_ref[0] = acc_c[...]

    r3 = lambda t: t.reshape(B, S, D)
    out = jax.ShapeDtypeStruct((B, S, D), F32)
    spec = pltpu.PrefetchScalarGridSpec(
        num_scalar_prefetch=1, grid=(B, nb, HP),
        in_specs=_fox_q_specs(5, TB) + _fox_k_specs(TB, HP, 0) + _fox_k_specs(TB, HP, -1) + [_ANY] * 3,
        out_specs=_fox_q_specs(2, TB),
        scratch_shapes=_fox_k_buffers(TB) + [pltpu.VMEM((TB, LANES), F32), pltpu.VMEM((TB, LANES), F32)])
    k3, v3 = r3(kh), kv.reshape(B, S, 2 * D)
    dq, dc = pl.pallas_call(
        body, name=name, grid_spec=spec, out_shape=[out, out], compiler_params=_params("parallel", "parallel", "parallel"),
    )(first, r3(qh), r3(cq), r3(do), r3(lse), r3(delta), k3, v3, ck, k3, v3, ck, k3, v3, ck)
    return dq.reshape(B * S, D), dc.reshape(B * S, D)


def _fox_bwd_dkv(qh, kh, kv, cq, ck, last, do, lse, delta, B, S, D, name):
    HP = D // LANES
    TB = FOX_BLOCK
    nb = S // TB

    def body(last_ref, k_ref, v_ref, ck_ref, q_ref, cq_ref, do_ref, l_ref, d_ref, qn_ref, cqn_ref, don_ref, ln_ref,
             dn_ref, q_hbm, cq_hbm, do_hbm, l_hbm, d_hbm, dk_ref, dv_ref, dc_ref,
             qbuf, cqbuf, dobuf, lbuf, dbuf, sems, dk_acc, dv_acc, dc_acc):
        b, kb, hp = pl.program_id(0), pl.program_id(1), pl.program_id(2)
        qb_last = last_ref[(b * HP + hp) * nb + kb]
        n_far = jnp.maximum(qb_last - kb - 1, 0)

        def fetch(qb, slot):
            rows = pl.ds(pl.multiple_of(qb * TB, TB), TB)
            cols = pl.ds(pl.multiple_of(hp * LANES, LANES), LANES)
            srcs = [t.at[b, rows, cols] for t in (q_hbm, cq_hbm, do_hbm, l_hbm, d_hbm)]
            return _fox_fetch(srcs, (qbuf, cqbuf, dobuf, lbuf, dbuf), sems, slot)

        @pl.when(n_far > 0)
        def _():
            for cp in fetch(kb + 2, 0):
                cp.start()

        dk_acc[...] = jnp.zeros_like(dk_acc)
        dv_acc[...] = jnp.zeros_like(dv_acc)
        dc_acc[...] = jnp.zeros_like(dc_acc)

        def step(q, cqv, dout, lse_t, del_t, diagonal):
            k, v = k_ref[0], v_ref[0].astype(BF16)
            dout = dout.astype(BF16)
            dk, dv, dc = dk_acc[...], dv_acc[...], dc_acc[...]
            row = _iota2((8, TB), 0)
            for h, hm in enumerate(_lane_masks()):
                qm, t = _fox_scores(q, k, ck_ref[0, h:h + 1, :], hm, diagonal)
                p = jnp.exp(t + (_head_col(cqv, h) - _head_col(lse_t, h)))
                dom = jnp.where(hm, dout, jnp.zeros_like(dout))
                ds = p * (_dot(dom, v, NT) - _head_col(del_t, h))
                dv += _dot(p.astype(BF16), dom, TN)
                dk += _dot(ds.astype(BF16), qm, TN)
                dc = jnp.where(row == h, dc - jnp.sum(ds, axis=0, keepdims=True), dc)
            dk_acc[...] = dk
            dv_acc[...] = dv
            dc_acc[...] = dc

        step(q_ref[0], cq_ref[0], do_ref[0], l_ref[0], d_ref[0], True)
        pl.when(kb < qb_last)(lambda: step(qn_ref[0], cqn_ref[0], don_ref[0], ln_ref[0], dn_ref[0], False))

        def walk(i, carry):
            slot = lax.rem(i, 2)

            @pl.when(i + 1 < n_far)
            def _():
                for cp in fetch(kb + 3 + i, 1 - slot):
                    cp.start()

            for cp in fetch(kb + 2 + i, slot):
                cp.wait()
            step(qbuf[slot], cqbuf[slot], dobuf[slot], lbuf[slot], dbuf[slot], False)
            return carry

        lax.fori_loop(0, n_far, walk, 0)
        dk_ref[0] = dk_acc[...]
        dv_ref[0] = dv_acc[...].astype(BF16)
        dc_ref[0] = dc_acc[...]

    blk = (1, TB, LANES)
    kmap = lambda b, i, h, last: (b, i, h)
    vmap = lambda b, i, h, last: (b, i, HP + h)
    cmap = lambda b, i, h, last: (b * HP + h, 0, i)
    nmap = lambda b, i, h, last: (b, jnp.minimum(i + 1, nb - 1), h)
    r3 = lambda t: t.reshape(B, S, D)
    tile = lambda dt: pltpu.VMEM((2, TB, LANES), dt)
    qside = [r3(qh), r3(cq), r3(do), r3(lse), r3(delta)]
    spec = pltpu.PrefetchScalarGridSpec(
        num_scalar_prefetch=1, grid=(B, nb, HP),
        in_specs=[pl.BlockSpec(blk, kmap), pl.BlockSpec(blk, vmap), pl.BlockSpec((1, 8, TB), cmap)]
        + [pl.BlockSpec(blk, kmap)] * 5 + [pl.BlockSpec(blk, nmap)] * 5 + [_ANY] * 5,
        out_specs=[pl.BlockSpec(blk, kmap), pl.BlockSpec(blk, kmap), pl.BlockSpec((1, 8, TB), cmap)],
        scratch_shapes=[tile(BF16), tile(F32), tile(F32), tile(F32), tile(F32), pltpu.SemaphoreType.DMA((2, 5)),
                        pltpu.VMEM((TB, LANES), F32), pltpu.VMEM((TB, LANES), F32), pltpu.VMEM((8, TB), F32)])
    dk, dv, dc = pl.pallas_call(
        body, name=name, grid_spec=spec,
        out_shape=[jax.ShapeDtypeStruct((B, S, D), F32), jax.ShapeDtypeStruct((B, S, D), BF16),
                   jax.ShapeDtypeStruct((B * HP, 8, S), F32)],
        compiler_params=_params("parallel", "parallel", "parallel"),
    )(last, r3(kh), kv.reshape(B, S, 2 * D), ck, *qside, *qside, *qside)
    return dk.reshape(B * S, D), dv.reshape(B * S, D), dc


def _loss_and_grad(y, target, name):
    T, D = y.shape
    tm = _pick(T, 512, 8)

    def body(y_ref, t_ref, p_ref, dy_ref):
        e = y_ref[...] - t_ref[...]
        dy_ref[...] = e * (1.0 / D)
        p_ref[...] = jnp.sum(e * e, axis=0, keepdims=True)[None]

    blk = pl.BlockSpec((tm, D), lambda i: (i, 0))
    return pl.pallas_call(
        body, name=name, grid=(T // tm,), in_specs=[blk, blk],
        out_specs=[pl.BlockSpec((1, 1, D), lambda i: (i, 0, 0)), blk],
        out_shape=[jax.ShapeDtypeStruct((T // tm, 1, D), F32), jax.ShapeDtypeStruct((T, D), F32)],
        compiler_params=_params("parallel"),
    )(y, target)


def _pair_sum(x, got, name):
    _, R, C = x.shape
    tr = _pick(R, max(16, (8 * PACK_ROWS * LANES) // C), 16)

    def body(c_ref, x_ref, g_ref, o_ref):
        o_ref[0] = (x_ref[0].astype(F32) + g_ref[0].astype(F32)).astype(o_ref.dtype)

    spec = pltpu.PrefetchScalarGridSpec(
        num_scalar_prefetch=1, grid=(4, R // tr),
        in_specs=[pl.BlockSpec((1, tr, C), lambda a, i, c: (2 * a + c[0], i, 0)),
                  pl.BlockSpec((1, tr, C), lambda a, i, c: (a, i, 0))],
        out_specs=pl.BlockSpec((1, tr, C), lambda a, i, c: (a, i, 0)))
    return pl.pallas_call(
        body, name=name, grid_spec=spec, out_shape=jax.ShapeDtypeStruct((4, R, C), x.dtype),
        compiler_params=_params("parallel", "parallel"),
    )(lax.axis_index("c").astype(jnp.int32).reshape(1), x, got)


def _adamw(parts, w, m, v, name):
    R, C = w.shape
    P = parts.shape[0]
    tr = _pick(R, max(16, (PACK_ROWS * LANES) // C), 16)

    def body(p_ref, w_ref, m_ref, v_ref, g_out, d_out, m_out, v_out):
        g = p_ref[0].astype(F32)
        for s in range(1, P):
            g = g + p_ref[s].astype(F32)
        mn = ADAM_B1 * m_ref[...] + (1.0 - ADAM_B1) * g
        vn = ADAM_B2 * v_ref[...] + (1.0 - ADAM_B2) * (g * g)
        m_hat = mn / (1.0 - ADAM_B1 ** ADAM_STEP)
        v_hat = vn / (1.0 - ADAM_B2 ** ADAM_STEP)
        g_out[...] = g
        d_out[...] = -ADAM_LR * (m_hat / (jnp.sqrt(v_hat) + ADAM_EPS) + ADAM_WD * w_ref[...])
        m_out[...] = mn
        v_out[...] = vn

    blk = pl.BlockSpec((tr, C), lambda i: (i, 0))
    out = jax.ShapeDtypeStruct((R, C), F32)
    return pl.pallas_call(
        body, name=name, grid=(R // tr,),
        in_specs=[pl.BlockSpec((P, tr, C), lambda i: (0, i, 0)), blk, blk, blk],
        out_specs=[blk] * 4, out_shape=[out] * 4, compiler_params=_params("parallel"),
    )(parts, w, m, v)


def _mesh_pos():
    return lax.axis_index("x"), lax.axis_index("y"), lax.axis_index("c")


def _comm_call(body, xs, out_shapes, name):
    n = len(xs)
    any_spec = pl.BlockSpec(memory_space=pl.ANY)
    return pl.pallas_call(
        body, name=name, out_shape=out_shapes, in_specs=[any_spec] * n, out_specs=[any_spec] * n,
        scratch_shapes=[pltpu.SemaphoreType.DMA((7, n)), pltpu.SemaphoreType.DMA((7, n)), pltpu.SemaphoreType.DMA((n,))],
    )(*xs)


def _all_gather(xs, name):
    n = len(xs)

    def body(*refs):
        start, finish = _gather_phases(refs[:n], refs[n:2 * n], *refs[2 * n:])
        start()
        finish()

    return _comm_call(body, xs, _gathered_shapes(xs), name)


def _gathered_shapes(xs):
    return [jax.ShapeDtypeStruct((N_DEV,) + x.shape, x.dtype) for x in xs]


def _gather_phases(x_refs, out_refs, send_sems, recv_sems, local_sems):
    n = len(x_refs)
    x_, y_, c_ = _mesh_pos()
    me, sibling = (x_, y_, c_), (x_, y_, 1 - c_)
    chips = [(1 - x_, y_), (x_, 1 - y_), (1 - x_, 1 - y_)]

    def slot(t, px, py, pc):
        return out_refs[t].at[4 * px + 2 * py + pc]

    def copy(t, k, block, to, src=None):
        return pltpu.make_async_remote_copy(
            src_ref=slot(t, *block) if src is None else src, dst_ref=slot(t, *block),
            send_sem=send_sems.at[k, t], recv_sem=recv_sems.at[k, t], device_id=to, device_id_type=MESH_ID)

    def own_copies():
        mine = [pltpu.make_async_copy(x_refs[t], slot(t, *me), local_sems.at[t]) for t in range(n)]
        first = [copy(t, 0, me, sibling, src=x_refs[t]) for t in range(n)]
        first += [copy(t, 1 + j, me, (*chip, c_), src=x_refs[t]) for j, chip in enumerate(chips) for t in range(n)]
        return mine, first

    def start():
        mine, first = own_copies()
        for cp in mine + first:
            cp.start()

    def finish():
        mine, first = own_copies()
        passed = []
        for j, chip in enumerate(chips):
            for t in range(n):
                copy(t, 1 + j, (*chip, c_), me).wait_recv()
                passed.append(copy(t, 4 + j, (*chip, c_), sibling))
                passed[-1].start()
        for t in range(n):
            copy(t, 0, sibling, me).wait_recv()
        for j, chip in enumerate(chips):
            for t in range(n):
                copy(t, 4 + j, (*chip, 1 - c_), me).wait_recv()
        for cp in first + passed:
            cp.wait_send()
        for cp in mine:
            cp.wait()

    return start, finish


def _call(body, *, name, grid, in_specs, out_specs, out_shape, scratch_shapes=(), semantics, args, gather=None,
          exchange=None):
    if gather is None and exchange is None:
        return pl.pallas_call(
            body, name=name, grid=grid, in_specs=in_specs, out_specs=out_specs, out_shape=out_shape,
            scratch_shapes=list(scratch_shapes), compiler_params=_params(*semantics))(*args), None
    (phases, shapes), gather = (GATHER, gather) if exchange is None else (CHIP_EXCHANGE, exchange)
    n, n_in, n_sc = len(gather), len(in_specs), len(scratch_shapes)
    outs = list(out_shape) if isinstance(out_shape, (list, tuple)) else [out_shape]
    specs = list(out_specs) if isinstance(out_specs, (list, tuple)) else [out_specs]
    n_out = len(outs)

    def carried(*refs):
        ins, xs = refs[:n_in], refs[n_in:n_in + n]
        ys, gs = refs[n_in + n:n_in + n + n_out], refs[n_in + n + n_out:n_in + 2 * n + n_out]
        scratch, sems = refs[n_in + 2 * n + n_out:n_in + 2 * n + n_out + n_sc], refs[n_in + 2 * n + n_out + n_sc:]
        start, finish = phases(xs, gs, *sems)
        at_first = functools.reduce(jnp.logical_and, [pl.program_id(a) == 0 for a in range(len(grid))])
        at_last = functools.reduce(jnp.logical_and, [pl.program_id(a) == grid[a] - 1 for a in range(len(grid))])
        pl.when(at_first)(start)
        body(*ins, *ys, *scratch)
        pl.when(at_last)(finish)

    results = pl.pallas_call(
        carried, name=name, grid=grid, in_specs=list(in_specs) + [_ANY] * n, out_specs=specs + [_ANY] * n,
        out_shape=outs + shapes(gather),
        scratch_shapes=list(scratch_shapes) + [pltpu.SemaphoreType.DMA((7, n)), pltpu.SemaphoreType.DMA((7, n)),
                                               pltpu.SemaphoreType.DMA((n,))],
        compiler_params=_params(*(["arbitrary"] * len(grid))))(*args, *gather)
    main = results[:n_out]
    return (main if isinstance(out_shape, (list, tuple)) else main[0]), list(results[n_out:])


def _pair_exchange(xs, name):
    n = len(xs)

    def body(*refs):
        x_refs, out_refs = refs[:n], refs[n:2 * n]
        send_sems, recv_sems, _ = refs[2 * n:]
        x_, y_, c_ = _mesh_pos()
        copies = [pltpu.make_async_remote_copy(
            src_ref=x_refs[t].at[2 * a + 1 - c_], dst_ref=out_refs[t].at[a],
            send_sem=send_sems.at[a, t], recv_sem=recv_sems.at[a, t],
            device_id=(x_, y_, 1 - c_), device_id_type=MESH_ID) for a in range(4) for t in range(n)]
        for cp in copies:
            cp.start()
        for cp in copies:
            cp.wait_recv()
        for cp in copies:
            cp.wait_send()

    return _comm_call(body, xs, [jax.ShapeDtypeStruct((4,) + x.shape[1:], x.dtype) for x in xs], name)


def _chip_exchange(xs, name):
    n = len(xs)

    def body(*refs):
        start, finish = _chip_phases(refs[:n], refs[n:2 * n], *refs[2 * n:])
        start()
        finish()

    return _comm_call(body, xs, _same_shapes(xs), name)


def _same_shapes(xs):
    return [jax.ShapeDtypeStruct(x.shape, x.dtype) for x in xs]


def _chip_phases(x_refs, out_refs, send_sems, recv_sems, local_sems):
    n = len(x_refs)
    x_, y_, c_ = _mesh_pos()
    chip = 2 * x_ + y_

    def all_copies():
        mine = [pltpu.make_async_copy(x_refs[t].at[chip], out_refs[t].at[chip], local_sems.at[t]) for t in range(n)]
        copies = []
        for k in range(1, 4):
            px, py = x_ ^ (k >> 1), y_ ^ (k & 1)
            for t in range(n):
                copies.append(pltpu.make_async_remote_copy(
                    src_ref=x_refs[t].at[2 * px + py], dst_ref=out_refs[t].at[chip],
                    send_sem=send_sems.at[k - 1, t], recv_sem=recv_sems.at[k - 1, t],
                    device_id=(px, py, c_), device_id_type=MESH_ID))
        return mine, copies

    def start():
        mine, copies = all_copies()
        for cp in mine + copies:
            cp.start()

    def finish():
        mine, copies = all_copies()
        for cp in copies:
            cp.wait_recv()
        for cp in copies:
            cp.wait_send()
        for cp in mine:
            cp.wait()

    return start, finish


GATHER = (_gather_phases, _gathered_shapes)
CHIP_EXCHANGE = (_chip_phases, _same_shapes)


class _Exchanges:
    class Job:
        def __init__(self, arrays, finish):
            self.arrays, self.finish = arrays, finish

    def __init__(self):
        self.jobs = []

    def add(self, arrays, finish):
        self.jobs.append(self.Job(arrays, finish))

    def take(self):
        return self.jobs.pop(0) if self.jobs else None


def _ffn_fwd(h, norm, w_in, w_out, tag, gather=None):
    T, D = h.shape
    F = w_out.shape[0]
    hn = _rmsnorm_fwd(h, norm, f"{tag}_norm")
    got = None
    if gather is None:
        gate, up, act = _mm_swiglu_fwd(hn, w_in, f"{tag}_in")
    else:
        (gate, up, act), got = _mm_swiglu_fwd(hn, w_in, f"{tag}_in", gather=gather)
    out = _mm(act, w_out, "nn", M=T, N=D, K=F, res=h, scale=0.5, name=f"{tag}_out")
    return out, (h, hn, gate, up, act), got


def _ffn_bwd(dout, saved, norm, w_in, w_out, tag, exchanges=None, ready=None, index=None):
    h, hn, gate, up, act = saved
    T, D = h.shape
    F = w_out.shape[0]
    job = exchanges.take() if exchanges else None
    dg, du, parts = _mm_swiglu_bwd(dout, w_out, gate, up, f"{tag}_dact", exchange=job.arrays if job else None)
    if job:
        job.finish(parts)
    dw_out = _mm(act, dout, "tn", M=F, N=D, K=T, scale=0.5, out_dtype=BF16, name=f"{tag}_dwout")
    dw_in = _mm(hn, dg, "tn", M=D, N=2 * F, K=T, b2=du, out_dtype=BF16, name=f"{tag}_dwin")
    if ready is not None:
        ready({("ffn_w_in", index): dw_in, ("ffn_w_out", index): dw_out})
    job = exchanges.take() if exchanges else None
    dhn = _mm(dg, w_in, "nt", M=T, N=D, K=2 * F, a_rest=(du,), name=f"{tag}_dhn", exchange=job.arrays if job else None)
    if job:
        dhn, parts = dhn
        job.finish(parts)
    dh, dnorm = _rmsnorm_bwd(h, norm, dhn, dout, f"{tag}_dnorm")
    return dh, dnorm, dw_in, dw_out


def _local_step(x, positions, target, W, pending=None, exchanges=None, on_grads=None):
    B, S, D = x.shape
    T = B * S
    H, HP = D // HEAD_DIM, D // LANES
    G = {}
    pending = pending or {}
    sending = lambda call: pending[call][0] if call in pending else None

    def arrived(call, got):
        if call in pending:
            pending[call][1](W, got)

    def ready(grads):
        if on_grads is not None:
            on_grads(grads)

    h0 = x.reshape(T, D)
    cs = _rope_tables(positions, "rope_tables")

    h1, ffn00, got = _ffn_fwd(h0, W["ffn_norm"][0, 0], W["ffn_w_in"][0, 0], W["ffn_w_out"][0, 0], "ffn00",
                              gather=sending("ffn00_in"))
    arrived("ffn00_in", got)
    hna = _rmsnorm_fwd(h1, W["mix_norm"][0], "mixa_norm")
    qkv = _mm(hna, W["a_w_qkv"], "nn", M=T, N=9 * D, K=D, name="a_qkv", gather=sending("a_qkv"))
    if sending("a_qkv") is not None:
        qkv, got = qkv
        arrived("a_qkv", got)
    qs, ks, os_, lses = [], [], [], []
    for g in range(3):
        qs.append(_qknorm_fwd(qkv, 3 * g * D, W["a_q_norm"][0, g], cs, D, F32, f"a_qnorm{g}", out_scale=SCALE))
        ks.append(_qknorm_fwd(qkv, (3 * g + 1) * D, W["a_k_norm"][0, g], cs, D, F32, f"a_knorm{g}"))
        o, l, *got = _band_fwd(qs[g], ks[g], qkv, g, B, S, D, f"a_band{g}", gather=sending(f"a_band{g}"))
        arrived(f"a_band{g}", got[0] if got else None)
        os_.append(o)
        lses.append(l)
    mixed, lse_a = _mix(os_, lses, "a_mix")
    h2 = _mm(mixed, W["a_w_o"], "nn", M=T, N=D, K=D, res=h1, name="a_out")
    h3, ffn01, _ = _ffn_fwd(h2, W["ffn_norm"][0, 1], W["ffn_w_in"][0, 1], W["ffn_w_out"][0, 1], "ffn01")

    hnkv = _rmsnorm_fwd(h3, W["kv_norm"], "kv_norm")
    kv = _mm(hnkv, W["kv_w_kv"], "nn", M=T, N=2 * D, K=D, name="kv_proj")
    fpre = _mm(hnkv, W["kv_w_f"], "nn", M=T, N=LANES, K=D, name="kv_gate_proj")
    b_f = jnp.pad(W["kv_b_f"], (0, LANES - H)).reshape(1, LANES)
    kh = _qknorm_fwd(kv, 0, W["kv_k_norm"], None, D, BF16, "kv_knorm")
    cum, cq = _gate_fwd(fpre, b_f, B, S, D, "kv_gate")
    ck = cum.reshape(B, S, LANES)[:, :, :H].reshape(B, S, HP, 2).transpose(0, 2, 3, 1)
    ck = jnp.pad(ck, ((0, 0), (0, 0), (0, 6), (0, 0))).reshape(B * HP, 8, S)
    h4, ffn10, _ = _ffn_fwd(h3, W["ffn_norm"][1, 0], W["ffn_w_in"][1, 0], W["ffn_w_out"][1, 0], "ffn10")
    hnb = _rmsnorm_fwd(h4, W["mix_norm"][1], "mixb_norm")
    qraw = _mm(hnb, W["b_w_q"], "nn", M=T, N=D, K=D, name="b_q")
    qh = _qknorm_fwd(qraw, 0, W["b_q_norm"][0], None, D, BF16, "b_qnorm", out_scale=SCALE)
    first, last = _fox_live_blocks(cum, W["b_q_norm"][0], W["kv_k_norm"], B, S, D)
    ob, lse_b = _fox_fwd(qh, kh, kv, cq, ck, first, B, S, D, "b_fox")
    h5 = _mm(ob, W["b_w_o"], "nn", M=T, N=D, K=D, res=h4, name="b_out")
    h6, ffn11, _ = _ffn_fwd(h5, W["ffn_norm"][1, 1], W["ffn_w_in"][1, 1], W["ffn_w_out"][1, 1], "ffn11")

    sq, dy = _loss_and_grad(h6, target.reshape(T, D), "loss")

    dn = [[None, None], [None, None]]
    dwi = [[None, None], [None, None]]
    dwo = [[None, None], [None, None]]
    dh5, dn[1][1], dwi[1][1], dwo[1][1] = _ffn_bwd(dy, ffn11, W["ffn_norm"][1, 1], W["ffn_w_in"][1, 1], W["ffn_w_out"][1, 1], "ffn11",
                                                   exchanges, ready, 3)
    dob = _mm(dh5, W["b_w_o"], "nt", M=T, N=D, K=D, name="b_dout")
    G["b_w_o"] = _mm(ob, dh5, "tn", M=D, N=D, K=T, out_dtype=BF16, name="b_dwo")[None]
    delta_b = _head_rowsum(dob, ob, "b_delta")
    dqh, dcq = _fox_bwd_dq(qh, kh, kv, cq, ck, first, dob, lse_b, delta_b, B, S, D, "b_fox_dq")
    dkh, dv_b, dck = _fox_bwd_dkv(qh, kh, kv, cq, ck, last, dob, lse_b, delta_b, B, S, D, "b_fox_dkv")
    dqraw, dg = _qknorm_bwd(qraw, 0, W["b_q_norm"][0], None, dqh, D, "b_dqnorm", out_scale=SCALE)
    G["b_q_norm"] = dg[None]
    dhnb = _mm(dqraw, W["b_w_q"], "nt", M=T, N=D, K=D, name="b_dhn")
    G["b_w_q"] = _mm(hnb, dqraw, "tn", M=D, N=D, K=T, out_dtype=BF16, name="b_dwq")[None]
    dh4, dmix_b = _rmsnorm_bwd(h4, W["mix_norm"][1], dhnb, dh5, "mixb_dnorm")
    dh3, dn[1][0], dwi[1][0], dwo[1][0] = _ffn_bwd(dh4, ffn10, W["ffn_norm"][1, 0], W["ffn_w_in"][1, 0], W["ffn_w_out"][1, 0], "ffn10",
                                                   exchanges, ready, 2)

    dkraw, G["kv_k_norm"] = _qknorm_bwd(kv, 0, W["kv_k_norm"], None, dkh, D, "kv_dknorm")
    dcum = dck.reshape(B, HP, 8, S)[:, :, :2].transpose(0, 3, 1, 2).reshape(T, H)
    dcum = jnp.pad(dcum, ((0, 0), (0, LANES - H)))
    dcum_q = jnp.pad(dcq.reshape(T, H, HEAD_DIM)[:, :, 0], ((0, 0), (0, LANES - H)))
    dfpre, db_f = _gate_bwd(dcum, dcum_q, fpre, b_f, B, S, "kv_dgate")
    G["kv_b_f"] = db_f[0, :H]
    dhnkv = _mm(dkraw, W["kv_w_kv"], "nt", M=T, N=D, K=D, name="kv_dhn_k")
    dhnkv = _mm(dv_b, W["kv_w_kv"], "nt", M=T, N=D, K=D, b_off=(0, D), res=dhnkv, name="kv_dhn_v")
    dhnkv = _mm(dfpre, W["kv_w_f"], "nt", M=T, N=D, K=LANES, res=dhnkv, name="kv_dhn_f")
    dw_k = _mm(hnkv, dkraw, "tn", M=D, N=D, K=T, out_dtype=BF16, name="kv_dwk")
    dw_v = _mm(hnkv, dv_b, "tn", M=D, N=D, K=T, out_dtype=BF16, name="kv_dwv")
    dw_f = _mm(hnkv, dfpre, "tn", M=D, N=LANES, K=T, out_dtype=BF16, name="kv_dwf")
    G["kv_w"] = jnp.concatenate([dw_k, dw_v, dw_f[:, :H]], axis=1)
    dh3, G["kv_norm"] = _rmsnorm_bwd(h3, W["kv_norm"], dhnkv, dh3, "kv_dnorm")
    ready({"kv_w": G["kv_w"], "b_w_q": G["b_w_q"][0], "b_w_o": G["b_w_o"][0]})

    dh2, dn[0][1], dwi[0][1], dwo[0][1] = _ffn_bwd(dh3, ffn01, W["ffn_norm"][0, 1], W["ffn_w_in"][0, 1], W["ffn_w_out"][0, 1], "ffn01",
                                                   exchanges, ready, 1)
    dmixed = _mm(dh2, W["a_w_o"], "nt", M=T, N=D, K=D, name="a_dmixed")
    G["a_w_o"] = _mm(mixed, dh2, "tn", M=D, N=D, K=T, out_dtype=BF16, name="a_dwo")[None]
    ready({"a_w_o": G["a_w_o"][0]})
    delta_a = _head_rowsum(dmixed, mixed, "a_delta")
    dhna = None
    dw_qkv, dqn, dkn = [], [], []
    for g in range(3):
        dq, dk, dv = _band_bwd(qs[g], ks[g], qkv, dmixed, lse_a, delta_a, g, B, S, D, f"a_dband{g}", exchanges)
        dqr, dgq = _qknorm_bwd(qkv, 3 * g * D, W["a_q_norm"][0, g], cs, dq, D, f"a_dqnorm{g}", out_scale=SCALE)
        dkr, dgk = _qknorm_bwd(qkv, (3 * g + 1) * D, W["a_k_norm"][0, g], cs, dk, D, f"a_dknorm{g}")
        dqn.append(dgq)
        dkn.append(dgk)
        dhna = _mm(dqr, W["a_w_qkv"], "nt", M=T, N=D, K=3 * D, a_rest=(dkr, dv), b_off=(0, 3 * g * D), res=dhna,
                   name=f"a_dhn{g}")
        for j, d in enumerate((dqr, dkr, dv)):
            dw_qkv.append(_mm(hna, d, "tn", M=D, N=D, K=T, out_dtype=BF16, name=f"a_dwqkv{3 * g + j}"))
    G["a_w_qkv"] = jnp.concatenate(dw_qkv, axis=1)[None]
    ready({"a_w_qkv": G["a_w_qkv"][0]})
    G["a_q_norm"] = jnp.stack(dqn)[None]
    G["a_k_norm"] = jnp.stack(dkn)[None]
    dh1, dmix_a = _rmsnorm_bwd(h1, W["mix_norm"][0], dhna, dh2, "mixa_dnorm")
    dh0, dn[0][0], dwi[0][0], dwo[0][0] = _ffn_bwd(dh1, ffn00, W["ffn_norm"][0, 0], W["ffn_w_in"][0, 0], W["ffn_w_out"][0, 0], "ffn00",
                                                   exchanges, ready, 0)

    G["mix_norm"] = jnp.stack([dmix_a, dmix_b])
    G["ffn_norm"] = jnp.stack([jnp.stack(r) for r in dn])
    G["ffn_w_in"] = jnp.stack([jnp.stack(r) for r in dwi])
    G["ffn_w_out"] = jnp.stack([jnp.stack(r) for r in dwo])
    return jnp.sum(sq), dh0.reshape(B, S, D), G


SMALL =["mix_norm", "kv_norm", "a_q_norm", "a_k_norm", "kv_b_f", "kv_k_norm", "b_q_norm"]


def _pack(blocks, lead):
    flat = [b.reshape(b.shape[:lead] + (-1,)) for b in blocks]
    flat = jnp.concatenate(flat, axis=lead)
    n = flat.shape[-1]
    unit = PACK_ROWS * LANES
    padded = -(-n // unit) * unit if n > unit else -(-n // (16 * LANES)) * 16 * LANES
    flat = jnp.pad(flat, [(0, 0)] * lead + [(0, padded - n)])
    return flat.reshape(flat.shape[:lead] + (padded // LANES, LANES))


def _unpack(packed, shapes, lead):
    flat = packed.reshape(packed.shape[:lead] + (-1,))
    out, off = [], 0
    for shp in shapes:
        n = int(np.prod(shp))
        out.append(flat[..., off:off + n].reshape(packed.shape[:lead] + tuple(shp)))
        off += n
    return out


def kernel(x, positions, ffn_norm, ffn_w_in, ffn_w_out, mix_norm, a_w_qkv, a_q_norm, a_k_norm, a_w_o, kv_norm, kv_w, kv_b_f, kv_k_norm, b_w_q, b_q_norm, b_w_o, loss_target, m_ffn_norm, m_ffn_w_in, m_ffn_w_out, m_mix_norm, m_a_w_qkv, m_a_q_norm, m_a_k_norm, m_a_w_o, m_kv_norm, m_kv_w, m_kv_b_f, m_kv_k_norm, m_b_w_q, m_b_q_norm, m_b_w_o, v_ffn_norm, v_ffn_w_in, v_ffn_w_out, v_mix_norm, v_a_w_qkv, v_a_q_norm, v_a_k_norm, v_a_w_o, v_kv_norm, v_kv_w, v_kv_b_f, v_kv_k_norm, v_b_w_q, v_b_q_norm, v_b_w_o):
    names = ["ffn_norm", "ffn_w_in", "ffn_w_out", "mix_norm", "a_w_qkv", "a_q_norm", "a_k_norm", "a_w_o", "kv_norm",
             "kv_w", "kv_b_f", "kv_k_norm", "b_w_q", "b_q_norm", "b_w_o"]
    w = dict(zip(names, (ffn_norm, ffn_w_in, ffn_w_out, mix_norm, a_w_qkv, a_q_norm, a_k_norm, a_w_o, kv_norm, kv_w,
                         kv_b_f, kv_k_norm, b_w_q, b_q_norm, b_w_o)))
    m = dict(zip(names, (m_ffn_norm, m_ffn_w_in, m_ffn_w_out, m_mix_norm, m_a_w_qkv, m_a_q_norm, m_a_k_norm, m_a_w_o,
                         m_kv_norm, m_kv_w, m_kv_b_f, m_kv_k_norm, m_b_w_q, m_b_q_norm, m_b_w_o)))
    v = dict(zip(names, (v_ffn_norm, v_ffn_w_in, v_ffn_w_out, v_mix_norm, v_a_w_qkv, v_a_q_norm, v_a_k_norm, v_a_w_o,
                         v_kv_norm, v_kv_w, v_kv_b_f, v_kv_k_norm, v_b_w_q, v_b_q_norm, v_b_w_o)))
    B, S, D = x.shape
    H = D // HEAD_DIM
    F = ffn_w_out.shape[2] * N_DEV
    Ds, Fs = D // N_DEV, F // N_DEV
    me = 4 * lax.axis_index("x") + 2 * lax.axis_index("y") + lax.axis_index("c")
    squares = ("a_w_o", "b_w_q", "b_w_o")

    def shard_views(t):
        return [t["ffn_w_in"].reshape(4 * D, -1), t["ffn_w_out"].reshape(-1, D), t["a_w_qkv"][0], t["kv_w"],
                jnp.concatenate([t[n][0] for n in squares], axis=0)]

    def to_shards(views):
        sq3 = views[4].reshape(3, 1, Ds, D)
        out = {"ffn_w_in": views[0].reshape(ffn_w_in.shape), "ffn_w_out": views[1].reshape(ffn_w_out.shape),
               "a_w_qkv": views[2][None], "kv_w": views[3]}
        out.update({n: sq3[i] for i, n in enumerate(squares)})
        return out

    sh_in, sh_out, sh_qkv, sh_kv, sh_sq = [s.astype(BF16) for s in shard_views(w)]
    in_of, out_of = (lambda f: sh_in[f * D:(f + 1) * D]), (lambda f: sh_out[f * Fs:(f + 1) * Fs])
    square_of = lambda i: sh_sq[i * Ds:(i + 1) * Ds]
    side_by_side = lambda g: g.transpose(1, 0, 2).reshape(g.shape[1], N_DEV * g.shape[2])
    stacked = lambda g: g.reshape(N_DEV * g.shape[1], g.shape[2])

    def store_ffn(f):
        def store(W, got):
            W["ffn_w_in"][f // 2, f % 2], W["ffn_w_out"][f // 2, f % 2] = side_by_side(got[0]), stacked(got[1])
        return store

    def store_mixer_a(W, got):
        W["a_w_qkv"], W["a_w_o"] = side_by_side(got[0]), stacked(got[1])

    def store_layer_b(W, got):
        store_ffn(1)(W, got)
        kv_full = side_by_side(got[2])
        W["kv_w_kv"] = kv_full[:, :2 * D]
        W["kv_w_f"] = jnp.pad(kv_full[:, 2 * D:], ((0, 0), (0, LANES - H)))
        W["b_w_q"], W["b_w_o"] = stacked(got[3]), stacked(got[4])

    norm_rows = jnp.pad(ffn_norm.reshape(4, Ds), ((0, 4), (0, 0)))
    first = _all_gather([in_of(0), out_of(0), norm_rows], "gather_weights")
    W = {n: w[n] for n in SMALL}
    W["ffn_norm"] = first[2][:, :4].transpose(1, 0, 2).reshape(2, 2, D)
    W["ffn_w_in"], W["ffn_w_out"] = {}, {}
    store_ffn(0)(W, first)
    pending = {"ffn00_in": ([sh_qkv, square_of(0)], store_mixer_a),
               "a_qkv": ([in_of(1), out_of(1), sh_kv, square_of(1), square_of(2)], store_layer_b),
               "a_band0": ([in_of(2), out_of(2)], store_ffn(2)),
               "a_band1": ([in_of(3), out_of(3)], store_ffn(3))}

    wmv = [shard_views(t) for t in (w, m, v)]
    columns = lambda g: g.reshape(g.shape[0], N_DEV, -1).transpose(1, 0, 2)
    rows = lambda g: g.reshape(N_DEV, g.shape[0] // N_DEV, g.shape[1])
    square_index = {n: i for i, n in enumerate(squares)}

    def placed(key):
        name, f = key if isinstance(key, tuple) else (key, None)
        if name == "ffn_w_in":
            return columns, 0, slice(f * D, (f + 1) * D)
        if name == "ffn_w_out":
            return rows, 1, slice(f * Fs, (f + 1) * Fs)
        if name in square_index:
            return rows, 4, slice(square_index[name] * Ds, (square_index[name] + 1) * Ds)
        return columns, {"a_w_qkv": 2, "kv_w": 3}[name], slice(None)

    exchanges = _Exchanges()
    updated = [{} for _ in wmv[0]]
    groups = []

    def on_grads(grads):
        tag = str(len(groups))
        groups.append(tag)
        keys = list(grads)
        blocks = [placed(k)[0](grads[k]).astype(BF16) for k in keys]
        got = _pair_exchange(blocks, "exchange_pair" + tag)
        sums = [_pair_sum(b, s, f"pair_sum{tag}_{i}") for i, (b, s) in enumerate(zip(blocks, got))]

        def finish(parts):
            for i, k in enumerate(keys):
                _, view, span = placed(k)
                outs = _adamw(parts[i], *(t[view][span] for t in wmv), f"adamw{tag}_{i}")
                updated[view][span.start or 0] = outs

        exchanges.add(sums, finish)

    sq, grad_x, G = _local_step(x, positions, loss_target, W, pending, exchanges, on_grads)
    loss = lax.psum(sq * (0.5 / D), ("x", "y", "c"))
    while exchanges.jobs:
        job = exchanges.take()
        job.finish(_chip_exchange(job.arrays, "exchange_chips" + str(len(exchanges.jobs))))
    new = {}
    for k, kind in enumerate(("grad", "delta", "new_m", "new_v")):
        whole = [jnp.concatenate([u[r][k] for r in sorted(u)], axis=0) for u in updated]
        for n, arr in to_shards(whole).items():
            new[kind, n] = arr

    small_all = ["ffn_norm"] + SMALL
    vec_parts = _all_gather([_pack([G[n] for n in small_all], 0)], "gather_vector_grads")[0]
    vec_parts = _unpack(vec_parts, [G[n].shape for n in small_all], 1)
    norm_g = lax.dynamic_slice_in_dim(vec_parts[0], me * ffn_norm.shape[2], ffn_norm.shape[2], axis=3)
    rep_g = _pack(vec_parts[1:], 1)
    for ns, parts_s in ((["ffn_norm"], _pack([norm_g], 1)), (SMALL, rep_g)):
        outs = _adamw(parts_s, *(_pack([t[n] for n in ns], 0) for t in (w, m, v)), "adamw_" + ns[0])
        for kind, packed in zip(("grad", "delta", "new_m", "new_v"), outs):
            for n, arr in zip(ns, _unpack(packed, [w[n].shape for n in ns], 0)):
                new[kind, n] = arr

    result = [loss, grad_x]
    for kind in ("grad", "delta", "new_m", "new_v"):
        result += [new[kind, n] for n in names]
    return tuple(result)
```
